```python
import jax, jax.numpy as jnp
from jax import lax
import numpy as np

D_MODEL = 1024
BATCH = 8
SEQ = 4096
DEPTH = 4

PLE_DIM = 256
D_MIX = D_MODEL
W_GRP = D_MIX // 4
N_HEADS_GRP = 4
HEAD_DIM = W_GRP // N_HEADS_GRP
GMLP_CHUNK = 128
RGLRU_CONV = 4
RGLRU_C = 8.0
HGRN_CHUNK = 64
POOL_WINDOWS = (2, 4, 8, 16)
D_FF = 2816
FFN_CONV = 3
EPS = 1e-6
COLS_A = 2 * W_GRP
COLS_B = 2 * W_GRP
COLS_C = 4 * W_GRP
COLS_D = W_GRP
OFF_B = COLS_A
OFF_C = OFF_B + COLS_B
OFF_D = OFF_C + COLS_C
D_PROJ = OFF_D + COLS_D

kernel_name = "hymba_style_gmlp_rglru_hgrn2_pool_hybrid"


def rms_norm(x, g):
    xf = x.astype(jnp.float32)
    y = xf * lax.rsqrt(jnp.mean(xf * xf, axis=-1, keepdims=True) + EPS)
    return (y * g.astype(jnp.float32)).astype(x.dtype)


def causal_dwconv(x, w, b):
    k_width = w.shape[0]
    s = x.shape[1]
    xp = jnp.pad(x, ((0, 0), (k_width - 1, 0), (0, 0)))
    y = b
    for k in range(k_width):
        y = y + xp[:, k:k + s] * w[k]
    return y


def gmlp_mixer(ab, ln_g, ln_b, ws, bs):
    bsz, s, _ = ab.shape
    ab = jax.nn.gelu(ab)
    u, v = jnp.split(ab, 2, axis=-1)
    vf = v.astype(jnp.float32)
    mu = jnp.mean(vf, axis=-1, keepdims=True)
    var = jnp.mean(jnp.square(vf - mu), axis=-1, keepdims=True)
    vn = ((vf - mu) * lax.rsqrt(var + EPS) * ln_g.astype(jnp.float32) + ln_b.astype(jnp.float32)).astype(v.dtype)
    vn = vn.reshape(bsz, s // GMLP_CHUNK, GMLP_CHUNK, N_HEADS_GRP, HEAD_DIM)
    mask = jnp.tril(jnp.ones((GMLP_CHUNK, GMLP_CHUNK), dtype=bool))
    wm = jnp.where(mask, ws, jnp.zeros_like(ws))
    sv = jnp.einsum('hts,bnshd->bnthd', wm, vn) + bs.T[:, :, None]
    return u * sv.reshape(bsz, s, W_GRP)


def rglru_mixer(xb, gb, conv_w, conv_b, wa, ba, wx, bx, lam):
    bsz, s, _ = xb.shape
    xc = causal_dwconv(xb, conv_w, conv_b)
    xh = xc.reshape(bsz, s, N_HEADS_GRP, HEAD_DIM)
    r = jax.nn.sigmoid(jnp.einsum('bshd,hde->bshe', xh, wa).reshape(bsz, s, W_GRP) + ba)
    i = jax.nn.sigmoid(jnp.einsum('bshd,hde->bshe', xh, wx).reshape(bsz, s, W_GRP) + bx)
    log_a = -RGLRU_C * r.astype(jnp.float32) * jax.nn.softplus(-lam.astype(jnp.float32))
    a = jnp.exp(log_a)
    mult = jnp.sqrt(-jnp.expm1(2.0 * log_a))
    bterm = mult * (i * xc).astype(jnp.float32)

    def combine(c1, c2):
        a1, b1 = c1
        a2, b2 = c2
        return a1 * a2, a2 * b1 + b2

    _, h = lax.associative_scan(combine, (a, bterm), axis=1)
    return h.astype(xb.dtype) * jax.nn.gelu(gb)


def hgrn2_mixer(q, f, i, g, lb, norm_g):
    bsz, s, _ = q.shape
    n_chunks = s // HGRN_CHUNK
    qf = jax.nn.silu(q.astype(jnp.float32))
    fgate = lb + (1.0 - lb) * jax.nn.sigmoid(f.astype(jnp.float32))
    log_f = jnp.log(fgate)
    kf = 1.0 - fgate
    vf = i.astype(jnp.float32)

    def to_chunks(t):
        return t.reshape(bsz, n_chunks, HGRN_CHUNK, N_HEADS_GRP, HEAD_DIM).transpose(1, 0, 3, 2, 4)

    qc, kc, vc = to_chunks(qf), to_chunks(kf), to_chunks(vf)
    bc = jnp.cumsum(to_chunks(log_f), axis=3)
    mask = jnp.tril(jnp.ones((HGRN_CHUNK, HGRN_CHUNK), dtype=bool))[:, :, None]

    def step(state, xs):
        qq, kk, vv, bb = xs
        diff = bb[:, :, :, None, :] - bb[:, :, None, :, :]
        decay = jnp.exp(jnp.where(mask, diff, -jnp.inf))
        att = jnp.einsum('bhtd,bhsd,bhtsd->bhts', qq, kk, decay)
        o = jnp.einsum('bhts,bhsv->bhtv', att, vv) + jnp.einsum('bhtd,bhdv->bhtv', qq * jnp.exp(bb), state)
        bl = bb[:, :, -1:, :]
        new_state = jnp.exp(bl[:, :, 0, :])[..., None] * state + jnp.einsum('bhsd,bhsv->bhdv', kk * jnp.exp(bl - bb), vv)
        return new_state, o

    s0 = jnp.zeros((bsz, N_HEADS_GRP, HEAD_DIM, HEAD_DIM), jnp.float32)
    _, o = lax.scan(step, s0, (qc, kc, vc, bc))
    o = o.transpose(1, 0, 3, 2, 4).reshape(bsz, s, N_HEADS_GRP, HEAD_DIM)
    o = o * lax.rsqrt(jnp.mean(o * o, axis=-1, keepdims=True) + EPS) * norm_g.astype(jnp.float32)
    o = o.reshape(bsz, s, W_GRP) * jax.nn.silu(g.astype(jnp.float32))
    return o.astype(q.dtype)


def pool_mixer(xd, wd, scale):
    bsz, s, _ = xd.shape
    xf = xd.astype(jnp.float32)
    cs = jnp.cumsum(xf, axis=1)
    pos = jnp.arange(1, s + 1, dtype=jnp.float32)[None, :, None]
    outs = []
    for j, w in enumerate(POOL_WINDOWS):
        c = cs[..., j * HEAD_DIM:(j + 1) * HEAD_DIM]
        shifted = jnp.pad(c, ((0, 0), (w, 0), (0, 0)))[:, :s]
        mean = (c - shifted) / jnp.minimum(pos, float(w))
        outs.append(mean - xf[..., j * HEAD_DIM:(j + 1) * HEAD_DIM])
    pooled = jnp.stack(outs, axis=2)
    y = jnp.einsum('bsgd,gde->bsge', pooled, wd.astype(jnp.float32)).reshape(bsz, s, W_GRP)
    return (y * scale.astype(jnp.float32)).astype(xd.dtype)


def _fwd_setup_inputs(seed: int = 0) -> dict:
    key = jax.random.key(seed)
    ks = jax.random.split(key, 32)

    def nrm(k, shape, scale):
        return jax.random.normal(k, shape, jnp.float32) * scale

    u = jax.random.uniform(ks[14], (DEPTH, W_GRP), jnp.float32, 0.9, 0.999)
    a_base = u ** (1.0 / RGLRU_C)
    b_lam = jnp.log(a_base) - jnp.log1p(-a_base)
    return {
        "x": nrm(ks[0], (BATCH, SEQ, D_MODEL), 1.0),
        "p": nrm(ks[1], (DEPTH, BATCH, SEQ, PLE_DIM), 1.0),
        "norm1_g": 1.0 + nrm(ks[2], (DEPTH, D_MODEL), 0.02),
        "w_in": nrm(ks[3], (DEPTH, D_MODEL, D_PROJ), D_MODEL ** -0.5),
        "a_ln_g": 1.0 + nrm(ks[4], (DEPTH, W_GRP), 0.02),
        "a_ln_b": nrm(ks[5], (DEPTH, W_GRP), 0.02),
        "a_ws": nrm(ks[6], (DEPTH, N_HEADS_GRP, GMLP_CHUNK, GMLP_CHUNK), GMLP_CHUNK ** -0.5),
        "a_bs": 1.0 + nrm(ks[7], (DEPTH, N_HEADS_GRP, GMLP_CHUNK), 0.1),
        "b_conv_w": nrm(ks[8], (DEPTH, RGLRU_CONV, W_GRP), RGLRU_CONV ** -0.5),
        "b_conv_b": nrm(ks[9], (DEPTH, W_GRP), 0.02),
        "b_wa": nrm(ks[10], (DEPTH, N_HEADS_GRP, HEAD_DIM, HEAD_DIM), HEAD_DIM ** -0.5),
        "b_ba": nrm(ks[11], (DEPTH, W_GRP), 0.02),
        "b_wx": nrm(ks[12], (DEPTH, N_HEADS_GRP, HEAD_DIM, HEAD_DIM), HEAD_DIM ** -0.5),
        "b_bx": nrm(ks[13], (DEPTH, W_GRP), 0.02),
        "b_lam": b_lam,
        "c_lb": nrm(ks[15], (DEPTH, W_GRP), 0.5),
        "c_norm_g": 1.0 + nrm(ks[16], (DEPTH, HEAD_DIM), 0.02),
        "d_w": nrm(ks[17], (DEPTH, N_HEADS_GRP, HEAD_DIM, HEAD_DIM), HEAD_DIM ** -0.5),
        "d_scale": 1.0 + nrm(ks[18], (DEPTH, W_GRP), 0.1),
        "w_out": nrm(ks[19], (DEPTH, D_MIX, D_MODEL), D_MIX ** -0.5),
        "norm2_g": 1.0 + nrm(ks[20], (DEPTH, D_MODEL), 0.02),
        "w_up": nrm(ks[21], (DEPTH, D_MODEL, 2 * D_FF), D_MODEL ** -0.5),
        "ffn_conv_w": nrm(ks[22], (DEPTH, FFN_CONV, 2 * D_FF), FFN_CONV ** -0.5),
        "ffn_conv_b": nrm(ks[23], (DEPTH, 2 * D_FF), 0.02),
        "w_down": nrm(ks[24], (DEPTH, D_FF, D_MODEL), D_FF ** -0.5),
        "norm3_g": 1.0 + nrm(ks[25], (DEPTH, D_MODEL), 0.02),
        "w_pe": nrm(ks[26], (DEPTH, PLE_DIM, D_MODEL), PLE_DIM ** -0.5),
        "w_pg": nrm(ks[27], (DEPTH, D_MODEL, D_MODEL), D_MODEL ** -0.5),
        "final_g": 1.0 + nrm(ks[28], (D_MODEL,), 0.02),
    }


def _fwd_reference(x, p, norm1_g, w_in, a_ln_g, a_ln_b, a_ws, a_bs, b_conv_w, b_conv_b, b_wa, b_ba, b_wx, b_bx, b_lam, c_lb, c_norm_g, d_w, d_scale, w_out, norm2_g, w_up, ffn_conv_w, ffn_conv_b, w_down, norm3_g, w_pe, w_pg, final_g):
    lbs = jnp.cumsum(jax.nn.softmax(c_lb.astype(jnp.float32), axis=0), axis=0)
    lbs = lbs - lbs[0:1]
    for l in range(DEPTH):
        h = rms_norm(x, norm1_g[l])
        z = h @ w_in[l]
        y_a = gmlp_mixer(z[..., :OFF_B], a_ln_g[l], a_ln_b[l], a_ws[l], a_bs[l])
        y_b = rglru_mixer(z[..., OFF_B:OFF_B + W_GRP], z[..., OFF_B + W_GRP:OFF_C],
                          b_conv_w[l], b_conv_b[l], b_wa[l], b_ba[l], b_wx[l], b_bx[l], b_lam[l])
        zc = z[..., OFF_C:OFF_D]
        y_c = hgrn2_mixer(zc[..., :W_GRP], zc[..., W_GRP:2 * W_GRP], zc[..., 2 * W_GRP:3 * W_GRP],
                          zc[..., 3 * W_GRP:], lbs[l], c_norm_g[l])
        y_d = pool_mixer(z[..., OFF_D:], d_w[l], d_scale[l])
        mix = jnp.concatenate([y_a, y_b, y_c, y_d], axis=-1)
        x = x + mix @ w_out[l]
        hf = rms_norm(x, norm2_g[l]) @ w_up[l]
        hf = causal_dwconv(hf, ffn_conv_w[l], ffn_conv_b[l])
        gt, val = jnp.split(hf, 2, axis=-1)
        x = x + (jax.nn.gelu(gt) * val) @ w_down[l]
        gate = jax.nn.sigmoid(rms_norm(x, norm3_g[l]) @ w_pg[l])
        x = x + (p[l] @ w_pe[l]) * gate
    return rms_norm(x, final_g)


import jax as _jax
import jax.numpy as _jnp

TWIN_FORMAT = 'train_step'
FWD_PARAMS = ['x', 'p', 'norm1_g', 'w_in', 'a_ln_g', 'a_ln_b', 'a_ws', 'a_bs', 'b_conv_w', 'b_conv_b', 'b_wa', 'b_ba', 'b_wx', 'b_bx', 'b_lam', 'c_lb', 'c_norm_g', 'd_w', 'd_scale', 'w_out', 'norm2_g', 'w_up', 'ffn_conv_w', 'ffn_conv_b', 'w_down', 'norm3_g', 'w_pe', 'w_pg', 'final_g']
TWIN_WEIGHTS = ['norm1_g', 'w_in', 'a_ln_g', 'a_ln_b', 'a_ws', 'a_bs', 'b_conv_w', 'b_conv_b', 'b_wa', 'b_ba', 'b_wx', 'b_bx', 'b_lam', 'c_lb', 'c_norm_g', 'd_w', 'd_scale', 'w_out', 'norm2_g', 'w_up', 'ffn_conv_w', 'ffn_conv_b', 'w_down', 'norm3_g', 'w_pe', 'w_pg', 'final_g']
TWIN_DIFF_INPUT = 'x'
TWIN_INPUTS = ['x', 'p', 'norm1_g', 'w_in', 'a_ln_g', 'a_ln_b', 'a_ws', 'a_bs', 'b_conv_w', 'b_conv_b', 'b_wa', 'b_ba', 'b_wx', 'b_bx', 'b_lam', 'c_lb', 'c_norm_g', 'd_w', 'd_scale', 'w_out', 'norm2_g', 'w_up', 'ffn_conv_w', 'ffn_conv_b', 'w_down', 'norm3_g', 'w_pe', 'w_pg', 'final_g', 'loss_target', 'm_norm1_g', 'm_w_in', 'm_a_ln_g', 'm_a_ln_b', 'm_a_ws', 'm_a_bs', 'm_b_conv_w', 'm_b_conv_b', 'm_b_wa', 'm_b_ba', 'm_b_wx', 'm_b_bx', 'm_b_lam', 'm_c_lb', 'm_c_norm_g', 'm_d_w', 'm_d_scale', 'm_w_out', 'm_norm2_g', 'm_w_up', 'm_ffn_conv_w', 'm_ffn_conv_b', 'm_w_down', 'm_norm3_g', 'm_w_pe', 'm_w_pg', 'm_final_g', 'v_norm1_g', 'v_w_in', 'v_a_ln_g', 'v_a_ln_b', 'v_a_ws', 'v_a_bs', 'v_b_conv_w', 'v_b_conv_b', 'v_b_wa', 'v_b_ba', 'v_b_wx', 'v_b_bx', 'v_b_lam', 'v_c_lb', 'v_c_norm_g', 'v_d_w', 'v_d_scale', 'v_w_out', 'v_norm2_g', 'v_w_up', 'v_ffn_conv_w', 'v_ffn_conv_b', 'v_w_down', 'v_norm3_g', 'v_w_pe', 'v_w_pg', 'v_final_g']
TWIN_OUTPUTS = ['loss', 'grad_x', 'grad_norm1_g', 'grad_w_in', 'grad_a_ln_g', 'grad_a_ln_b', 'grad_a_ws', 'grad_a_bs', 'grad_b_conv_w', 'grad_b_conv_b', 'grad_b_wa', 'grad_b_ba', 'grad_b_wx', 'grad_b_bx', 'grad_b_lam', 'grad_c_lb', 'grad_c_norm_g', 'grad_d_w', 'grad_d_scale', 'grad_w_out', 'grad_norm2_g', 'grad_w_up', 'grad_ffn_conv_w', 'grad_ffn_conv_b', 'grad_w_down', 'grad_norm3_g', 'grad_w_pe', 'grad_w_pg', 'grad_final_g', 'delta_norm1_g', 'delta_w_in', 'delta_a_ln_g', 'delta_a_ln_b', 'delta_a_ws', 'delta_a_bs', 'delta_b_conv_w', 'delta_b_conv_b', 'delta_b_wa', 'delta_b_ba', 'delta_b_wx', 'delta_b_bx', 'delta_b_lam', 'delta_c_lb', 'delta_c_norm_g', 'delta_d_w', 'delta_d_scale', 'delta_w_out', 'delta_norm2_g', 'delta_w_up', 'delta_ffn_conv_w', 'delta_ffn_conv_b', 'delta_w_down', 'delta_norm3_g', 'delta_w_pe', 'delta_w_pg', 'delta_final_g', 'new_m_norm1_g', 'new_m_w_in', 'new_m_a_ln_g', 'new_m_a_ln_b', 'new_m_a_ws', 'new_m_a_bs', 'new_m_b_conv_w', 'new_m_b_conv_b', 'new_m_b_wa', 'new_m_b_ba', 'new_m_b_wx', 'new_m_b_bx', 'new_m_b_lam', 'new_m_c_lb', 'new_m_c_norm_g', 'new_m_d_w', 'new_m_d_scale', 'new_m_w_out', 'new_m_norm2_g', 'new_m_w_up', 'new_m_ffn_conv_w', 'new_m_ffn_conv_b', 'new_m_w_down', 'new_m_norm3_g', 'new_m_w_pe', 'new_m_w_pg', 'new_m_final_g', 'new_v_norm1_g', 'new_v_w_in', 'new_v_a_ln_g', 'new_v_a_ln_b', 'new_v_a_ws', 'new_v_a_bs', 'new_v_b_conv_w', 'new_v_b_conv_b', 'new_v_b_wa', 'new_v_b_ba', 'new_v_b_wx', 'new_v_b_bx', 'new_v_b_lam', 'new_v_c_lb', 'new_v_c_norm_g', 'new_v_d_w', 'new_v_d_scale', 'new_v_w_out', 'new_v_norm2_g', 'new_v_w_up', 'new_v_ffn_conv_w', 'new_v_ffn_conv_b', 'new_v_w_down', 'new_v_norm3_g', 'new_v_w_pe', 'new_v_w_pg', 'new_v_final_g']
TWIN_LEAF_KINDS = {'loss': 'loss', 'grad_x': 'grad_x', 'grad_norm1_g': 'grad_w', 'grad_w_in': 'grad_w', 'grad_a_ln_g': 'grad_w', 'grad_a_ln_b': 'grad_w', 'grad_a_ws': 'grad_w', 'grad_a_bs': 'grad_w', 'grad_b_conv_w': 'grad_w', 'grad_b_conv_b': 'grad_w', 'grad_b_wa': 'grad_w', 'grad_b_ba': 'grad_w', 'grad_b_wx': 'grad_w', 'grad_b_bx': 'grad_w', 'grad_b_lam': 'grad_w', 'grad_c_lb': 'grad_w', 'grad_c_norm_g': 'grad_w', 'grad_d_w': 'grad_w', 'grad_d_scale': 'grad_w', 'grad_w_out': 'grad_w', 'grad_norm2_g': 'grad_w', 'grad_w_up': 'grad_w', 'grad_ffn_conv_w': 'grad_w', 'grad_ffn_conv_b': 'grad_w', 'grad_w_down': 'grad_w', 'grad_norm3_g': 'grad_w', 'grad_w_pe': 'grad_w', 'grad_w_pg': 'grad_w', 'grad_final_g': 'grad_w', 'delta_norm1_g': 'delta_w', 'delta_w_in': 'delta_w', 'delta_a_ln_g': 'delta_w', 'delta_a_ln_b': 'delta_w', 'delta_a_ws': 'delta_w', 'delta_a_bs': 'delta_w', 'delta_b_conv_w': 'delta_w', 'delta_b_conv_b': 'delta_w', 'delta_b_wa': 'delta_w', 'delta_b_ba': 'delta_w', 'delta_b_wx': 'delta_w', 'delta_b_bx': 'delta_w', 'delta_b_lam': 'delta_w', 'delta_c_lb': 'delta_w', 'delta_c_norm_g': 'delta_w', 'delta_d_w': 'delta_w', 'delta_d_scale': 'delta_w', 'delta_w_out': 'delta_w', 'delta_norm2_g': 'delta_w', 'delta_w_up': 'delta_w', 'delta_ffn_conv_w': 'delta_w', 'delta_ffn_conv_b': 'delta_w', 'delta_w_down': 'delta_w', 'delta_norm3_g': 'delta_w', 'delta_w_pe': 'delta_w', 'delta_w_pg': 'delta_w', 'delta_final_g': 'delta_w', 'new_m_norm1_g': 'new_m', 'new_m_w_in': 'new_m', 'new_m_a_ln_g': 'new_m', 'new_m_a_ln_b': 'new_m', 'new_m_a_ws': 'new_m', 'new_m_a_bs': 'new_m', 'new_m_b_conv_w': 'new_m', 'new_m_b_conv_b': 'new_m', 'new_m_b_wa': 'new_m', 'new_m_b_ba': 'new_m', 'new_m_b_wx': 'new_m', 'new_m_b_bx': 'new_m', 'new_m_b_lam': 'new_m', 'new_m_c_lb': 'new_m', 'new_m_c_norm_g': 'new_m', 'new_m_d_w': 'new_m', 'new_m_d_scale': 'new_m', 'new_m_w_out': 'new_m', 'new_m_norm2_g': 'new_m', 'new_m_w_up': 'new_m', 'new_m_ffn_conv_w': 'new_m', 'new_m_ffn_conv_b': 'new_m', 'new_m_w_down': 'new_m', 'new_m_norm3_g': 'new_m', 'new_m_w_pe': 'new_m', 'new_m_w_pg': 'new_m', 'new_m_final_g': 'new_m', 'new_v_norm1_g': 'new_v', 'new_v_w_in': 'new_v', 'new_v_a_ln_g': 'new_v', 'new_v_a_ln_b': 'new_v', 'new_v_a_ws': 'new_v', 'new_v_a_bs': 'new_v', 'new_v_b_conv_w': 'new_v', 'new_v_b_conv_b': 'new_v', 'new_v_b_wa': 'new_v', 'new_v_b_ba': 'new_v', 'new_v_b_wx': 'new_v', 'new_v_b_bx': 'new_v', 'new_v_b_lam': 'new_v', 'new_v_c_lb': 'new_v', 'new_v_c_norm_g': 'new_v', 'new_v_d_w': 'new_v', 'new_v_d_scale': 'new_v', 'new_v_w_out': 'new_v', 'new_v_norm2_g': 'new_v', 'new_v_w_up': 'new_v', 'new_v_ffn_conv_w': 'new_v', 'new_v_ffn_conv_b': 'new_v', 'new_v_w_down': 'new_v', 'new_v_norm3_g': 'new_v', 'new_v_w_pe': 'new_v', 'new_v_w_pg': 'new_v', 'new_v_final_g': 'new_v'}


def _forward(args):
    return _fwd_reference(*[args[k] for k in FWD_PARAMS])


def _output_shape():
    def fwd():
        inp = _fwd_setup_inputs(0)
        return _fwd_reference(*[inp[k] for k in FWD_PARAMS])
    out = _jax.eval_shape(fwd)
    return out.shape, out.dtype

N_MICROBATCH = 1
ADAM_LR = 0.001
ADAM_B1 = 0.9
ADAM_B2 = 0.999
ADAM_EPS = 1e-08
ADAM_WD = 0.01
ADAM_STEP = 10
PER_EXAMPLE_BATCH_AXIS = {'x': 0, 'p': 1, 'loss_target': 0}
SHARED_INPUTS = []
_WEIGHT_DTYPES = {'norm1_g': _jnp.float32, 'w_in': _jnp.float32, 'a_ln_g': _jnp.float32, 'a_ln_b': _jnp.float32, 'a_ws': _jnp.float32, 'a_bs': _jnp.float32, 'b_conv_w': _jnp.float32, 'b_conv_b': _jnp.float32, 'b_wa': _jnp.float32, 'b_ba': _jnp.float32, 'b_wx': _jnp.float32, 'b_bx': _jnp.float32, 'b_lam': _jnp.float32, 'c_lb': _jnp.float32, 'c_norm_g': _jnp.float32, 'd_w': _jnp.float32, 'd_scale': _jnp.float32, 'w_out': _jnp.float32, 'norm2_g': _jnp.float32, 'w_up': _jnp.float32, 'ffn_conv_w': _jnp.float32, 'ffn_conv_b': _jnp.float32, 'w_down': _jnp.float32, 'norm3_g': _jnp.float32, 'w_pe': _jnp.float32, 'w_pg': _jnp.float32, 'final_g': _jnp.float32}
MOMENT_SCALE = {'norm1_g': 1.056979e-01, 'w_in': 7.360082e-02, 'a_ln_g': 5.636915e-02, 'a_ln_b': 5.987924e-02, 'a_ws': 4.015947e-02, 'a_bs': 5.835450e-02, 'b_conv_w': 8.209070e-02, 'b_conv_b': 6.137967e-01, 'b_wa': 2.812808e-02, 'b_ba': 2.102940e-02, 'b_wx': 5.124706e-02, 'b_bx': 2.863443e-02, 'b_lam': 4.271352e-02, 'c_lb': 7.579221e-03, 'c_norm_g': 1.555673e-01, 'd_w': 1.091126e-01, 'd_scale': 1.158539e-01, 'w_out': 9.309179e-02, 'norm2_g': 9.894357e-02, 'w_up': 4.265025e-02, 'ffn_conv_w': 4.255828e-02, 'ffn_conv_b': 4.554780e-02, 'w_down': 6.962595e-02, 'norm3_g': 2.244089e-02, 'w_pe': 5.637338e-02, 'w_pg': 2.208236e-02, 'final_g': 3.201966e+01}


def _to_microbatches(a, axis):
    t = _jnp.moveaxis(a, axis, 0)
    t = t.reshape((N_MICROBATCH, t.shape[0] // N_MICROBATCH) + t.shape[1:])
    return _jnp.moveaxis(t, 1, axis + 1)


def setup_inputs(seed: int = 0) -> dict:
    inp = _fwd_setup_inputs(seed)
    key = _jax.random.fold_in(_jax.random.key(seed), 7919)
    shape, _ = _output_shape()
    out = dict(inp)
    out["loss_target"] = _jax.random.normal(_jax.random.fold_in(key, 0), shape, _jnp.float32)
    for i, name in enumerate(TWIN_WEIGHTS):
        w = inp[name].astype(_jnp.float32)
        if MOMENT_SCALE is None:
            s = _jnp.sqrt(_jnp.mean(_jnp.square(w)) + 1e-30)
        else:
            s = MOMENT_SCALE[name]
        km, kv = _jax.random.split(_jax.random.fold_in(key, i + 1))
        out[name] = w
        out["m_" + name] = s * _jax.random.normal(km, w.shape, _jnp.float32)
        out["v_" + name] = (s * s) * _jax.random.uniform(kv, w.shape, _jnp.float32, 0.5, 1.5)
    if N_MICROBATCH > 1:
        for name, axis in PER_EXAMPLE_BATCH_AXIS.items():
            out[name] = _to_microbatches(out[name], axis)
    return {'x': out['x'], 'p': out['p'], 'norm1_g': out['norm1_g'], 'w_in': out['w_in'], 'a_ln_g': out['a_ln_g'], 'a_ln_b': out['a_ln_b'], 'a_ws': out['a_ws'], 'a_bs': out['a_bs'], 'b_conv_w': out['b_conv_w'], 'b_conv_b': out['b_conv_b'], 'b_wa': out['b_wa'], 'b_ba': out['b_ba'], 'b_wx': out['b_wx'], 'b_bx': out['b_bx'], 'b_lam': out['b_lam'], 'c_lb': out['c_lb'], 'c_norm_g': out['c_norm_g'], 'd_w': out['d_w'], 'd_scale': out['d_scale'], 'w_out': out['w_out'], 'norm2_g': out['norm2_g'], 'w_up': out['w_up'], 'ffn_conv_w': out['ffn_conv_w'], 'ffn_conv_b': out['ffn_conv_b'], 'w_down': out['w_down'], 'norm3_g': out['norm3_g'], 'w_pe': out['w_pe'], 'w_pg': out['w_pg'], 'final_g': out['final_g'], 'loss_target': out['loss_target'], 'm_norm1_g': out['m_norm1_g'], 'm_w_in': out['m_w_in'], 'm_a_ln_g': out['m_a_ln_g'], 'm_a_ln_b': out['m_a_ln_b'], 'm_a_ws': out['m_a_ws'], 'm_a_bs': out['m_a_bs'], 'm_b_conv_w': out['m_b_conv_w'], 'm_b_conv_b': out['m_b_conv_b'], 'm_b_wa': out['m_b_wa'], 'm_b_ba': out['m_b_ba'], 'm_b_wx': out['m_b_wx'], 'm_b_bx': out['m_b_bx'], 'm_b_lam': out['m_b_lam'], 'm_c_lb': out['m_c_lb'], 'm_c_norm_g': out['m_c_norm_g'], 'm_d_w': out['m_d_w'], 'm_d_scale': out['m_d_scale'], 'm_w_out': out['m_w_out'], 'm_norm2_g': out['m_norm2_g'], 'm_w_up': out['m_w_up'], 'm_ffn_conv_w': out['m_ffn_conv_w'], 'm_ffn_conv_b': out['m_ffn_conv_b'], 'm_w_down': out['m_w_down'], 'm_norm3_g': out['m_norm3_g'], 'm_w_pe': out['m_w_pe'], 'm_w_pg': out['m_w_pg'], 'm_final_g': out['m_final_g'], 'v_norm1_g': out['v_norm1_g'], 'v_w_in': out['v_w_in'], 'v_a_ln_g': out['v_a_ln_g'], 'v_a_ln_b': out['v_a_ln_b'], 'v_a_ws': out['v_a_ws'], 'v_a_bs': out['v_a_bs'], 'v_b_conv_w': out['v_b_conv_w'], 'v_b_conv_b': out['v_b_conv_b'], 'v_b_wa': out['v_b_wa'], 'v_b_ba': out['v_b_ba'], 'v_b_wx': out['v_b_wx'], 'v_b_bx': out['v_b_bx'], 'v_b_lam': out['v_b_lam'], 'v_c_lb': out['v_c_lb'], 'v_c_norm_g': out['v_c_norm_g'], 'v_d_w': out['v_d_w'], 'v_d_scale': out['v_d_scale'], 'v_w_out': out['v_w_out'], 'v_norm2_g': out['v_norm2_g'], 'v_w_up': out['v_w_up'], 'v_ffn_conv_w': out['v_ffn_conv_w'], 'v_ffn_conv_b': out['v_ffn_conv_b'], 'v_w_down': out['v_w_down'], 'v_norm3_g': out['v_norm3_g'], 'v_w_pe': out['v_w_pe'], 'v_w_pg': out['v_w_pg'], 'v_final_g': out['v_final_g']}


def _loss(weights, diff, rest, loss_target):
    with _jax.named_scope("forward"):
        args = {**rest, TWIN_DIFF_INPUT: diff, **{k: w.astype(_WEIGHT_DTYPES[k]) for k, w in weights.items()}}
        y = _forward(args)
    with _jax.named_scope("loss_head"):
        err = _jnp.square(y.astype(_jnp.float32) - loss_target)
        return 0.5 * _jnp.sum(_jnp.mean(err, axis=-1)) if err.ndim else 0.5 * err


def _adamw(w, g, m, v):
    m = ADAM_B1 * m + (1.0 - ADAM_B1) * g
    v = ADAM_B2 * v + (1.0 - ADAM_B2) * _jnp.square(g)
    m_hat = m / (1.0 - ADAM_B1 ** ADAM_STEP)
    v_hat = v / (1.0 - ADAM_B2 ** ADAM_STEP)
    delta = -ADAM_LR * (m_hat / (_jnp.sqrt(v_hat) + ADAM_EPS) + ADAM_WD * w)
    return delta, m, v


def reference(x, p, norm1_g, w_in, a_ln_g, a_ln_b, a_ws, a_bs, b_conv_w, b_conv_b, b_wa, b_ba, b_wx, b_bx, b_lam, c_lb, c_norm_g, d_w, d_scale, w_out, norm2_g, w_up, ffn_conv_w, ffn_conv_b, w_down, norm3_g, w_pe, w_pg, final_g, loss_target, m_norm1_g, m_w_in, m_a_ln_g, m_a_ln_b, m_a_ws, m_a_bs, m_b_conv_w, m_b_conv_b, m_b_wa, m_b_ba, m_b_wx, m_b_bx, m_b_lam, m_c_lb, m_c_norm_g, m_d_w, m_d_scale, m_w_out, m_norm2_g, m_w_up, m_ffn_conv_w, m_ffn_conv_b, m_w_down, m_norm3_g, m_w_pe, m_w_pg, m_final_g, v_norm1_g, v_w_in, v_a_ln_g, v_a_ln_b, v_a_ws, v_a_bs, v_b_conv_w, v_b_conv_b, v_b_wa, v_b_ba, v_b_wx, v_b_bx, v_b_lam, v_c_lb, v_c_norm_g, v_d_w, v_d_scale, v_w_out, v_norm2_g, v_w_up, v_ffn_conv_w, v_ffn_conv_b, v_w_down, v_norm3_g, v_w_pe, v_w_pg, v_final_g):
    given = dict(x=x, p=p, norm1_g=norm1_g, w_in=w_in, a_ln_g=a_ln_g, a_ln_b=a_ln_b, a_ws=a_ws, a_bs=a_bs, b_conv_w=b_conv_w, b_conv_b=b_conv_b, b_wa=b_wa, b_ba=b_ba, b_wx=b_wx, b_bx=b_bx, b_lam=b_lam, c_lb=c_lb, c_norm_g=c_norm_g, d_w=d_w, d_scale=d_scale, w_out=w_out, norm2_g=norm2_g, w_up=w_up, ffn_conv_w=ffn_conv_w, ffn_conv_b=ffn_conv_b, w_down=w_down, norm3_g=norm3_g, w_pe=w_pe, w_pg=w_pg, final_g=final_g, loss_target=loss_target, m_norm1_g=m_norm1_g, m_w_in=m_w_in, m_a_ln_g=m_a_ln_g, m_a_ln_b=m_a_ln_b, m_a_ws=m_a_ws, m_a_bs=m_a_bs, m_b_conv_w=m_b_conv_w, m_b_conv_b=m_b_conv_b, m_b_wa=m_b_wa, m_b_ba=m_b_ba, m_b_wx=m_b_wx, m_b_bx=m_b_bx, m_b_lam=m_b_lam, m_c_lb=m_c_lb, m_c_norm_g=m_c_norm_g, m_d_w=m_d_w, m_d_scale=m_d_scale, m_w_out=m_w_out, m_norm2_g=m_norm2_g, m_w_up=m_w_up, m_ffn_conv_w=m_ffn_conv_w, m_ffn_conv_b=m_ffn_conv_b, m_w_down=m_w_down, m_norm3_g=m_norm3_g, m_w_pe=m_w_pe, m_w_pg=m_w_pg, m_final_g=m_final_g, v_norm1_g=v_norm1_g, v_w_in=v_w_in, v_a_ln_g=v_a_ln_g, v_a_ln_b=v_a_ln_b, v_a_ws=v_a_ws, v_a_bs=v_a_bs, v_b_conv_w=v_b_conv_w, v_b_conv_b=v_b_conv_b, v_b_wa=v_b_wa, v_b_ba=v_b_ba, v_b_wx=v_b_wx, v_b_bx=v_b_bx, v_b_lam=v_b_lam, v_c_lb=v_c_lb, v_c_norm_g=v_c_norm_g, v_d_w=v_d_w, v_d_scale=v_d_scale, v_w_out=v_w_out, v_norm2_g=v_norm2_g, v_w_up=v_w_up, v_ffn_conv_w=v_ffn_conv_w, v_ffn_conv_b=v_ffn_conv_b, v_w_down=v_w_down, v_norm3_g=v_norm3_g, v_w_pe=v_w_pe, v_w_pg=v_w_pg, v_final_g=v_final_g)
    weights = {n: given[n] for n in TWIN_WEIGHTS}
    shared = {n: given[n] for n in SHARED_INPUTS}
    per_example = {n: given[n] for n in ['x', 'p']}
    grad_fn = _jax.value_and_grad(_loss, argnums=(0, 1))

    def one_microbatch(ex, loss_target):
        ex = dict(ex)
        diff = ex.pop(TWIN_DIFF_INPUT)
        return grad_fn(weights, diff, {**shared, **ex}, loss_target)

    if N_MICROBATCH == 1:
        loss, (grad_w, grad_x) = one_microbatch(per_example, given["loss_target"])
    else:
        def body(carry, xs):
            loss_sum, grad_sum = carry
            l_k, (gw_k, gx_k) = one_microbatch(xs[0], xs[1])
            with _jax.named_scope("update"):
                return (loss_sum + l_k, _jax.tree.map(_jnp.add, grad_sum, gw_k)), gx_k

        init = (_jnp.zeros((), _jnp.float32), _jax.tree.map(_jnp.zeros_like, weights))
        (loss, grad_w), grad_x = _jax.lax.scan(body, init, (per_example, given["loss_target"]))
    with _jax.named_scope("update"):
        delta_w, new_m, new_v = {}, {}, {}
        for n in TWIN_WEIGHTS:
            delta_w[n], new_m[n], new_v[n] = _adamw(weights[n], grad_w[n], given["m_" + n], given["v_" + n])
    return (loss, grad_x, *[grad_w[n] for n in TWIN_WEIGHTS], *[delta_w[n] for n in TWIN_WEIGHTS],
            *[new_m[n] for n in TWIN_WEIGHTS], *[new_v[n] for n in TWIN_WEIGHTS])
```

```python
import functools

import jax
import jax.numpy as jnp
from jax import lax
from jax.experimental import pallas as pl
from jax.experimental.pallas import tpu as pltpu

F32 = jnp.float32
BF16 = jnp.bfloat16
MESH = pl.DeviceIdType.MESH

D_MODEL = 1024
DEPTH = 4
PLE_DIM = 256
W_GRP = 256
N_HEADS = 4
HEAD_DIM = 64
GMLP_CHUNK = 128
RGLRU_C = 8.0
HGRN_CHUNK = 64
HGRN_SUB = 16
POOL_WINDOWS = (2, 4, 8, 16)
D_FF = 2816
D_PROJ = 2304
EPS = 1e-6
ADAM_LR = 0.001
ADAM_B1 = 0.9
ADAM_B2 = 0.999
ADAM_EPS = 1e-08
ADAM_WD = 0.01
ADAM_STEP = 10

N_DEV = 8
MIB = 2 ** 20
V7X_VMEM_BYTES = 64 * MIB
HGRN_EXP_CLAMP = 60.0

WEIGHT_NAMES = ['norm1_g', 'w_in', 'a_ln_g', 'a_ln_b', 'a_ws', 'a_bs', 'b_conv_w', 'b_conv_b', 'b_wa', 'b_ba', 'b_wx',
                'b_bx', 'b_lam', 'c_lb', 'c_norm_g', 'd_w', 'd_scale', 'w_out', 'norm2_g', 'w_up', 'ffn_conv_w',
                'ffn_conv_b', 'w_down', 'norm3_g', 'w_pe', 'w_pg', 'final_g']
BIG_NAMES = ('w_in', 'w_out', 'w_up', 'w_down', 'w_pe', 'w_pg')
SLAB_ROWS = (('w_in', 288), ('w_out', 128), ('w_up', 704), ('w_down', 352), ('w_pe', 32), ('w_pg', 128))
LAYER_ROWS = sum(r for _, r in SLAB_ROWS)
PACK_ROWS = DEPTH * LAYER_ROWS


def _vmem_limit(block_bytes):
    want = 2 * block_bytes + 24 * MIB
    return int(min(max(want, 32 * MIB), V7X_VMEM_BYTES - 8 * MIB))


def _pcall(body, *, name, out_shape, grid=None, in_specs=None, out_specs=None, scratch_shapes=(),
           semantics=None, block_bytes=0):
    kw = {}
    if grid is not None:
        kw["grid"] = grid
    if in_specs is not None:
        kw["in_specs"] = in_specs
    if out_specs is not None:
        kw["out_specs"] = out_specs
    params = pltpu.CompilerParams(dimension_semantics=semantics, vmem_limit_bytes=_vmem_limit(block_bytes))
    return pl.pallas_call(body, name=name, out_shape=out_shape, scratch_shapes=list(scratch_shapes),
                          compiler_params=params, **kw)


def _pick(n, cands):
    for c in cands:
        if n % c == 0:
            return c
    return n


def _nbytes(shape, dtype):
    n = 1
    for s in shape:
        n *= s
    return n * jnp.dtype(dtype).itemsize


def _sds(shape, dtype):
    return jax.ShapeDtypeStruct(tuple(shape), dtype)


def _rows_of(shape):
    return lax.broadcasted_iota(jnp.int32, shape, 0)


def _lanes_of(shape):
    return lax.broadcasted_iota(jnp.int32, shape, 1)


def _sdn(x, k, fill):
    n = x.shape[0]
    return jnp.where(_rows_of(x.shape) >= k, pltpu.roll(x, k % n, 0), fill)


def _sup(x, k, fill):
    n = x.shape[0]
    return jnp.where(_rows_of(x.shape) < n - k, pltpu.roll(x, (n - k) % n, 0), fill)


@functools.partial(jax.custom_vjp, nondiff_argnums=(1,))
def _shift_dn(x, k):
    return _sdn(x, k, 0.0)


def _shift_dn_fwd(x, k):
    return _sdn(x, k, 0.0), None


def _shift_dn_bwd(k, _, g):
    return (_sup(g, k, 0.0),)


_shift_dn.defvjp(_shift_dn_fwd, _shift_dn_bwd)


def _lin_scan_impl(a, b, h0):
    n = a.shape[0]
    aa, bb = a, b
    k = 1
    while k < n:
        bb = aa * _sdn(bb, k, 0.0) + bb
        aa = aa * _sdn(aa, k, 1.0)
        k *= 2
    return bb + aa * h0


@jax.custom_vjp
def _lin_scan(a, b, h0):
    return _lin_scan_impl(a, b, h0)


def _lin_scan_fwd(a, b, h0):
    h = _lin_scan_impl(a, b, h0)
    return h, (a, h, h0)


def _lin_scan_bwd(res, g):
    a, h, h0 = res
    n = a.shape[0]
    cc, gg = _sup(a, 1, 0.0), g
    k = 1
    while k < n:
        gg = gg + cc * _sup(gg, k, 0.0)
        cc = cc * _sup(cc, k, 1.0)
        k *= 2
    first = _rows_of(a.shape) == 0
    hprev = jnp.where(first, h0, _sdn(h, 1, 0.0))
    dh0 = jnp.sum(jnp.where(first, a * gg, 0.0), axis=0, keepdims=True)
    return gg * hprev, gg, dh0


_lin_scan.defvjp(_lin_scan_fwd, _lin_scan_bwd)


def _cumsum_sub_impl(x):
    pos = _rows_of(x.shape) % HGRN_SUB
    k = 1
    while k < HGRN_SUB:
        x = x + jnp.where(pos >= k, pltpu.roll(x, k, 0), 0.0)
        k *= 2
    return x


@jax.custom_vjp
def _cumsum_sub(x):
    return _cumsum_sub_impl(x)


def _cumsum_sub_fwd(x):
    return _cumsum_sub_impl(x), None


def _cumsum_sub_bwd(_, g):
    n = g.shape[0]
    pos = _rows_of(g.shape) % HGRN_SUB
    k = 1
    while k < HGRN_SUB:
        g = g + jnp.where(pos < HGRN_SUB - k, pltpu.roll(g, n - k, 0), 0.0)
        k *= 2
    return (g,)


_cumsum_sub.defvjp(_cumsum_sub_fwd, _cumsum_sub_bwd)


def _dot(a, b, ca, cb):
    return lax.dot_general(a.astype(BF16), b.astype(BF16), (((ca,), (cb,)), ((), ())), preferred_element_type=F32)


@jax.custom_vjp
def _mm(a, b):
    return _dot(a, b, 1, 0)


def _mm_fwd(a, b):
    return _dot(a, b, 1, 0), (a, b)


def _mm_bwd(res, g):
    a, b = res
    return _dot(g, b, 1, 1), _dot(a, g, 0, 0)


_mm.defvjp(_mm_fwd, _mm_bwd)


@jax.custom_vjp
def _mm_nt(a, b):
    return _dot(a, b, 1, 1)


def _mm_nt_fwd(a, b):
    return _dot(a, b, 1, 1), (a, b)


def _mm_nt_bwd(res, g):
    a, b = res
    return _dot(g, b, 1, 0), _dot(g, a, 0, 0)


_mm_nt.defvjp(_mm_nt_fwd, _mm_nt_bwd)


@jax.custom_vjp
def _mm_tn(a, b):
    return _dot(a, b, 0, 0)


def _mm_tn_fwd(a, b):
    return _dot(a, b, 0, 0), (a, b)


def _mm_tn_bwd(res, g):
    a, b = res
    return _dot(b, g, 1, 1), _dot(a, g, 1, 0)


_mm_tn.defvjp(_mm_tn_fwd, _mm_tn_bwd)


def _head_mask(shape, h):
    return (_lanes_of(shape) // HEAD_DIM) == h


def _stack_heads(x):
    return jnp.concatenate([jnp.where(_head_mask(x.shape, h), x, 0.0) for h in range(N_HEADS)], axis=0)


def _unstack_heads(p):
    r = p.shape[0] // N_HEADS
    out = None
    for h in range(N_HEADS):
        blk = p[h * r:(h + 1) * r]
        term = jnp.where(_head_mask(blk.shape, h), blk, 0.0)
        out = term if out is None else out + term
    return out


def _segmean_impl(x):
    n = x.shape[1]
    same = (lax.broadcasted_iota(jnp.int32, (n, n), 0) // HEAD_DIM) == (lax.broadcasted_iota(jnp.int32, (n, n), 1) // HEAD_DIM)
    m = jnp.where(same, 1.0 / HEAD_DIM, 0.0).astype(BF16)
    hi = x.astype(BF16)
    lo = (x - hi.astype(F32)).astype(BF16)
    dn = (((1,), (0,)), ((), ()))
    return (lax.dot_general(hi, m, dn, preferred_element_type=F32)
            + lax.dot_general(lo, m, dn, preferred_element_type=F32))


@jax.custom_vjp
def _segmean(x):
    return _segmean_impl(x)


def _segmean_fwd(x):
    return _segmean_impl(x), None


def _segmean_bwd(_, g):
    return (_segmean_impl(g),)


_segmean.defvjp(_segmean_fwd, _segmean_bwd)


def _log1p(u):
    w = 1.0 + u
    return jnp.where(w == 1.0, u, jnp.log(w) * (u / (w - 1.0)))


def _softplus(y):
    return jnp.maximum(y, 0.0) + _log1p(jnp.exp(-jnp.abs(y)))


def _rms(x, g):
    return x * lax.rsqrt(jnp.mean(x * x, axis=-1, keepdims=True) + EPS) * g


def _gmlp_chunk(zu, zv, ln_g, ln_b, wcat, bfull):
    u = jax.nn.gelu(zu)
    v = jax.nn.gelu(zv)
    mu = jnp.mean(v, axis=-1, keepdims=True)
    var = jnp.mean(jnp.square(v - mu), axis=-1, keepdims=True)
    vn = (v - mu) * lax.rsqrt(var + EPS) * ln_g + ln_b
    sv = _unstack_heads(_mm(wcat, vn)) + bfull
    return u * sv


def _rglru_tile(xb_ext, gb, h0, cw0, cw1, cw2, cw3, cb, wa, ba, wx, bx, lam):
    xc = (cb + cw0 * _shift_dn(xb_ext, 3) + cw1 * _shift_dn(xb_ext, 2) + cw2 * _shift_dn(xb_ext, 1) + cw3 * xb_ext)[8:]
    r = jax.nn.sigmoid(_mm(xc, wa) + ba)
    i = jax.nn.sigmoid(_mm(xc, wx) + bx)
    log_a = (-RGLRU_C) * r * _softplus(-lam)
    a = jnp.exp(log_a)
    mult = jnp.sqrt(-jnp.tanh(log_a) * (a * a + 1.0))
    h = _lin_scan(a, mult * (i * xc), h0)
    y = h * jax.nn.gelu(gb)
    h_last = jnp.sum(jnp.where(_rows_of(h.shape) == h.shape[0] - 1, h, 0.0), axis=0, keepdims=True)
    return y, h_last


def _pool_tile(xd_ext, inv, wd, scale):
    s1 = xd_ext + _shift_dn(xd_ext, 1)
    s2 = s1 + _shift_dn(s1, 2)
    s3 = s2 + _shift_dn(s2, 4)
    s4 = s3 + _shift_dn(s3, 8)
    grp = _lanes_of(xd_ext.shape) // HEAD_DIM
    win = jnp.where(grp == 0, s1, jnp.where(grp == 1, s2, jnp.where(grp == 2, s3, s4)))
    pooled = win[16:] * inv - xd_ext[16:]
    return _mm(pooled, wd) * scale


def _hgrn_chunk(q, f, i, g, st, lb, ngf):
    n = q.shape[0]
    nsub = n // HGRN_SUB
    qs = jax.nn.silu(q)
    fg = lb + (1.0 - lb) * jax.nn.sigmoid(f)
    lf = jnp.log(fg)
    k = 1.0 - fg
    bl = _cumsum_sub(lf)
    row = _rows_of(q.shape)
    blk = row // HGRN_SUB
    betas = [jnp.zeros_like(lb)]
    for s in range(nsub):
        tot = jnp.sum(jnp.where(row == s * HGRN_SUB + HGRN_SUB - 1, bl, 0.0), axis=0, keepdims=True)
        betas.append(betas[-1] + tot)
    b_end = betas[nsub]
    beta_full = jnp.zeros_like(q)
    for s in range(1, nsub):
        beta_full = jnp.where(blk == s, betas[s], beta_full)
    qh = qs * jnp.exp(bl)
    qt = qh * jnp.exp(beta_full)
    b_all = beta_full + bl
    kt = k * jnp.exp(b_end - b_all)
    outs = []
    for s in range(nsub):
        kh = k * jnp.exp(jnp.minimum(betas[s] - b_all, HGRN_EXP_CLAMP))
        qstk = _stack_heads(qh[s * HGRN_SUB:(s + 1) * HGRN_SUB])
        att = _mm_nt(qstk, kh)
        ar = _rows_of(att.shape) % HGRN_SUB + s * HGRN_SUB
        att = jnp.where(_lanes_of(att.shape) <= ar, att, 0.0)
        outs.append(_unstack_heads(_mm(att, i)))
    o = jnp.concatenate(outs, axis=0) + _mm_nt(qt, st)
    same = (_rows_of(st.shape) // HEAD_DIM) == (_lanes_of(st.shape) // HEAD_DIM)
    st_new = st * jnp.exp(b_end) + jnp.where(same, _mm_tn(i, kt), 0.0)
    on = o * lax.rsqrt(_segmean(o * o) + EPS) * ngf
    return on * jax.nn.silu(g), st_new


def _ffn_tile(eg, ev, wg0, wg1, wg2, bg, wv0, wv1, wv2, bv):
    gt = (bg + wg0 * _shift_dn(eg, 2) + wg1 * _shift_dn(eg, 1) + wg2 * eg)[8:]
    val = (bv + wv0 * _shift_dn(ev, 2) + wv1 * _shift_dn(ev, 1) + wv2 * ev)[8:]
    return jax.nn.gelu(gt) * val


def _matmul(a, b, *, name, nt=False, res=None, out_dtype=F32):
    m, k = a.shape
    n = b.shape[0] if nt else b.shape[1]
    tm = _pick(m, (512, 256, 128))
    tn = _pick(n, (512, 384, 1408, 256, 128))
    dims = (((1,), (1,)), ((), ())) if nt else (((1,), (0,)), ((), ()))

    def body(*refs):
        if res is None:
            a_ref, b_ref, o_ref = refs
        else:
            a_ref, b_ref, r_ref, o_ref = refs
        acc = lax.dot_general(a_ref[...], b_ref[...], dims, preferred_element_type=F32)
        if res is not None:
            acc = acc + r_ref[...]
        o_ref[...] = acc.astype(out_dtype)

    in_specs = [pl.BlockSpec((tm, k), lambda i, j: (i, 0)),
                pl.BlockSpec((tn, k), lambda i, j: (j, 0)) if nt else pl.BlockSpec((k, tn), lambda i, j: (0, j))]
    args = [a, b]
    blk = _nbytes((tm, k), a.dtype) + _nbytes((k, tn), b.dtype) + _nbytes((tm, tn), out_dtype) + _nbytes((tm, tn), F32)
    if res is not None:
        in_specs.append(pl.BlockSpec((tm, tn), lambda i, j: (i, j)))
        args.append(res)
        blk += _nbytes((tm, tn), F32)
    return _pcall(body, name=name, out_shape=_sds((m, n), out_dtype), grid=(m // tm, n // tn), in_specs=in_specs,
                  out_specs=pl.BlockSpec((tm, tn), lambda i, j: (i, j)), semantics=("parallel", "parallel"),
                  block_bytes=blk)(*args)


def _matmul_tn(a, b, *, name, out_dtype=BF16):
    m, k1 = a.shape
    n = b.shape[1]
    tk = _pick(k1, (512, 384, 256, 128))
    tn = _pick(n, (512, 256, 128))

    def body(a_ref, b_ref, o_ref):
        o_ref[...] = lax.dot_general(a_ref[...], b_ref[...], (((0,), (0,)), ((), ())),
                                     preferred_element_type=F32).astype(out_dtype)

    blk = _nbytes((m, tk), a.dtype) + _nbytes((m, tn), b.dtype) + _nbytes((tk, tn), F32) + _nbytes((m, tk), a.dtype)
    return _pcall(body, name=name, out_shape=_sds((k1, n), out_dtype), grid=(k1 // tk, n // tn),
                  in_specs=[pl.BlockSpec((m, tk), lambda i, j: (0, i)), pl.BlockSpec((m, tn), lambda i, j: (0, j))],
                  out_specs=pl.BlockSpec((tk, tn), lambda i, j: (i, j)), semantics=("parallel", "parallel"),
                  block_bytes=blk)(a, b)


def _rms_fwd(x, g, *, name):
    s, d = x.shape
    tm = _pick(s, (512, 256))

    def body(x_ref, g_ref, o_ref):
        o_ref[...] = _rms(x_ref[...], g_ref[...]).astype(BF16)

    return _pcall(body, name=name, out_shape=_sds((s, d), BF16), grid=(s // tm,),
                  in_specs=[pl.BlockSpec((tm, d), lambda i: (i, 0)), pl.BlockSpec((1, d), lambda i: (0, 0))],
                  out_specs=pl.BlockSpec((tm, d), lambda i: (i, 0)), semantics=("parallel",),
                  block_bytes=3 * _nbytes((tm, d), F32))(x, g)


def _rms_bwd(x, g, dh, dres, *, name):
    s, d = x.shape
    tm = _pick(s, (256, 128))

    def body(x_ref, g_ref, dh_ref, dr_ref, dx_ref, dxb_ref, dg_ref):
        _, vjp = jax.vjp(_rms, x_ref[...], g_ref[...])
        dxn, dg = vjp(dh_ref[...])
        dx = dr_ref[...] + dxn
        dx_ref[...] = dx
        dxb_ref[...] = dx.astype(BF16)

        @pl.when(pl.program_id(0) == 0)
        def _():
            dg_ref[...] = jnp.zeros_like(dg_ref)

        dg_ref[...] += dg

    row = pl.BlockSpec((tm, d), lambda i: (i, 0))
    vec = pl.BlockSpec((1, d), lambda i: (0, 0))
    return _pcall(body, name=name, out_shape=(_sds((s, d), F32), _sds((s, d), BF16), _sds((1, d), F32)),
                  grid=(s // tm,), in_specs=[row, vec, row, row], out_specs=(row, row, vec),
                  semantics=("arbitrary",), block_bytes=8 * _nbytes((tm, d), F32))(x, g, dh, dres)


def _ple_fwd(x, gl, pe, *, name):
    s, d = x.shape
    tm = _pick(s, (512, 256))

    def body(x_ref, gl_ref, pe_ref, o_ref):
        o_ref[...] = x_ref[...] + pe_ref[...] * jax.nn.sigmoid(gl_ref[...])

    row = pl.BlockSpec((tm, d), lambda i: (i, 0))
    return _pcall(body, name=name, out_shape=_sds((s, d), F32), grid=(s // tm,), in_specs=[row, row, row],
                  out_specs=row, semantics=("parallel",), block_bytes=4 * _nbytes((tm, d), F32))(x, gl, pe)


def _ple_bwd(dx, gl, pe, *, name):
    s, d = dx.shape
    tm = _pick(s, (512, 256))

    def body(dx_ref, gl_ref, pe_ref, dpe_ref, dgl_ref):
        gate = jax.nn.sigmoid(gl_ref[...])
        dxv = dx_ref[...]
        dpe_ref[...] = (dxv * gate).astype(BF16)
        dgl_ref[...] = (dxv * pe_ref[...] * gate * (1.0 - gate)).astype(BF16)

    row = pl.BlockSpec((tm, d), lambda i: (i, 0))
    return _pcall(body, name=name, out_shape=(_sds((s, d), BF16), _sds((s, d), BF16)), grid=(s // tm,),
                  in_specs=[row, row, row], out_specs=(row, row), semantics=("parallel",),
                  block_bytes=5 * _nbytes((tm, d), F32))(dx, gl, pe)


def _loss_head(x, g, target, *, name):
    s, d = x.shape
    tm = _pick(s, (256, 128))

    def tile_loss(xv, gv, tv):
        err = jnp.square(_rms(xv, gv) - tv)
        return 0.5 * jnp.sum(jnp.mean(err, axis=-1, keepdims=True), axis=0, keepdims=True)

    def body(x_ref, g_ref, t_ref, l_ref, dx_ref, dg_ref):
        lv, vjp = jax.vjp(tile_loss, x_ref[...], g_ref[...], t_ref[...])
        dxv, dgv, _ = vjp(jnp.ones((1, 1), F32))
        dx_ref[...] = dxv

        @pl.when(pl.program_id(0) == 0)
        def _():
            l_ref[...] = jnp.zeros_like(l_ref)
            dg_ref[...] = jnp.zeros_like(dg_ref)

        l_ref[...] += jnp.broadcast_to(lv, l_ref.shape)
        dg_ref[...] += dgv

    row = pl.BlockSpec((tm, d), lambda i: (i, 0))
    vec = pl.BlockSpec((1, d), lambda i: (0, 0))
    return _pcall(body, name=name, out_shape=(_sds((8, 128), F32), _sds((s, d), F32), _sds((1, d), F32)),
                  grid=(s // tm,), in_specs=[row, vec, row],
                  out_specs=(pl.BlockSpec((8, 128), lambda i: (0, 0)), row, vec), semantics=("arbitrary",),
                  block_bytes=8 * _nbytes((tm, d), F32))(x, g, target)


def _acc_out(ref, val, first):
    @pl.when(first)
    def _():
        ref[...] = jnp.zeros_like(ref)

    ref[...] += val


def _gmlp_fwd(z, ln_g, ln_b, wcat, bfull, *, name):
    s = z.shape[0]
    t = _pick(s, (512, 256, 128))
    nch = t // GMLP_CHUNK

    def body(zu_ref, zv_ref, g_ref, b_ref, w_ref, bf_ref, o_ref):
        for c in range(nch):
            rows = pl.ds(c * GMLP_CHUNK, GMLP_CHUNK)
            o_ref[rows, :] = _gmlp_chunk(zu_ref[rows, :], zv_ref[rows, :], g_ref[...], b_ref[...], w_ref[...],
                                         bf_ref[...]).astype(BF16)

    col = lambda c: pl.BlockSpec((t, W_GRP), lambda i: (i, c))
    full = lambda a: pl.BlockSpec(a.shape, lambda i: (0, 0))
    return _pcall(body, name=name, out_shape=_sds((s, W_GRP), BF16), grid=(s // t,),
                  in_specs=[col(0), col(1), full(ln_g), full(ln_b), full(wcat), full(bfull)],
                  out_specs=pl.BlockSpec((t, W_GRP), lambda i: (i, 0)), semantics=("parallel",),
                  block_bytes=4 * _nbytes((t, W_GRP), F32))(z, z, ln_g, ln_b, wcat, bfull)


def _gmlp_bwd(z, dmix, ln_g, ln_b, wcat, bfull, *, name):
    s = z.shape[0]
    t = _pick(s, (512, 256, 128))
    nch = t // GMLP_CHUNK

    def body(zu_ref, zv_ref, dy_ref, g_ref, b_ref, w_ref, bf_ref, du_ref, dv_ref, dg_ref, db_ref, dw_ref, dbf_ref):
        acc = None
        for c in range(nch):
            rows = pl.ds(c * GMLP_CHUNK, GMLP_CHUNK)
            _, vjp = jax.vjp(_gmlp_chunk, zu_ref[rows, :], zv_ref[rows, :], g_ref[...], b_ref[...], w_ref[...],
                             bf_ref[...])
            du, dv, *dps = vjp(dy_ref[rows, :])
            du_ref[rows, :] = du.astype(BF16)
            dv_ref[rows, :] = dv.astype(BF16)
            acc = dps if acc is None else [x + y for x, y in zip(acc, dps)]
        first = pl.program_id(0) == 0
        for ref, val in zip((dg_ref, db_ref, dw_ref, dbf_ref), acc):
            _acc_out(ref, val, first)

    col = lambda c: pl.BlockSpec((t, W_GRP), lambda i: (i, c))
    full = lambda a: pl.BlockSpec(a.shape, lambda i: (0, 0))
    params = (ln_g, ln_b, wcat, bfull)
    return _pcall(body, name=name,
                  out_shape=(_sds((s, W_GRP), BF16), _sds((s, W_GRP), BF16)) + tuple(_sds(a.shape, F32) for a in params),
                  grid=(s // t,), in_specs=[col(0), col(1), col(0)] + [full(a) for a in params],
                  out_specs=(col(0), col(0)) + tuple(full(a) for a in params), semantics=("arbitrary",),
                  block_bytes=8 * _nbytes((t, W_GRP), F32))(z, z, dmix, *params)


def _rglru_fwd(z, prm, *, name):
    s = z.shape[0]
    t = _pick(s, (512, 256, 128))
    nt = s // t

    def body(xb_ref, halo_ref, gb_ref, *rest):
        prm_refs, (y_ref, h0s_ref, h_scr) = rest[:len(prm)], rest[len(prm):]
        i = pl.program_id(0)

        @pl.when(i == 0)
        def _():
            h_scr[...] = jnp.zeros_like(h_scr)

        halo = jnp.where(i == 0, 0.0, halo_ref[...])
        h0 = h_scr[...]
        y, h_last = _rglru_tile(jnp.concatenate([halo, xb_ref[...]], axis=0), gb_ref[...], h0,
                                *[r[...] for r in prm_refs])
        y_ref[...] = y.astype(BF16)
        h0s_ref[...] = jnp.broadcast_to(h0, h0s_ref.shape)
        h_scr[...] = h_last

    full = lambda a: pl.BlockSpec(a.shape, lambda i: (0,) * a.ndim)
    in_specs = [pl.BlockSpec((t, W_GRP), lambda i: (i, 2)),
                pl.BlockSpec((8, W_GRP), lambda i: (jnp.maximum(i * (t // 8) - 1, 0), 2)),
                pl.BlockSpec((t, W_GRP), lambda i: (i, 3))] + [full(a) for a in prm]
    return _pcall(body, name=name, out_shape=(_sds((s, W_GRP), BF16), _sds((nt, 8, W_GRP), F32)), grid=(nt,),
                  in_specs=in_specs,
                  out_specs=(pl.BlockSpec((t, W_GRP), lambda i: (i, 0)), pl.BlockSpec((None, 8, W_GRP), lambda i: (i, 0, 0))),
                  scratch_shapes=[pltpu.VMEM((1, W_GRP), F32)], semantics=("arbitrary",),
                  block_bytes=24 * _nbytes((t, W_GRP), F32))(z, z, z, *prm)


def _rglru_bwd(z, dmix, h0s, prm, *, name):
    s = z.shape[0]
    t = _pick(s, (512, 256, 128))
    nt = s // t
    npm = len(prm)

    def body(xb_ref, halo_ref, gb_ref, dy_ref, h0s_ref, *rest):
        prm_refs = rest[:npm]
        dxb_ref, dgb_ref = rest[npm:npm + 2]
        dprm_refs = rest[npm + 2:2 * npm + 2]
        dh_scr, dhalo_scr = rest[2 * npm + 2:]
        i = pl.program_id(0)
        r = nt - 1 - i

        @pl.when(i == 0)
        def _():
            dh_scr[...] = jnp.zeros_like(dh_scr)
            dhalo_scr[...] = jnp.zeros_like(dhalo_scr)

        halo = jnp.where(r == 0, 0.0, halo_ref[...])
        h0 = h0s_ref[0:1, :]
        _, vjp = jax.vjp(_rglru_tile, jnp.concatenate([halo, xb_ref[...]], axis=0), gb_ref[...], h0,
                         *[p[...] for p in prm_refs])
        dext, dgb, _dh0, *dps = vjp((dy_ref[...], dh_scr[...]))
        dmain = dext[8:]
        dxb = jnp.concatenate([dmain[:t - 8], dmain[t - 8:] + dhalo_scr[...]], axis=0)
        dxb_ref[...] = dxb.astype(BF16)
        dgb_ref[...] = dgb.astype(BF16)
        dh_scr[...] = _dh0
        dhalo_scr[...] = dext[:8]
        for ref, val in zip(dprm_refs, dps):
            _acc_out(ref, val, i == 0)

    full = lambda a: pl.BlockSpec(a.shape, lambda i: (0,) * a.ndim)
    rev = lambda c: pl.BlockSpec((t, W_GRP), lambda i: (nt - 1 - i, c))
    in_specs = [rev(2), pl.BlockSpec((8, W_GRP), lambda i: (jnp.maximum((nt - 1 - i) * (t // 8) - 1, 0), 2)), rev(3),
                rev(1), pl.BlockSpec((None, 8, W_GRP), lambda i: (nt - 1 - i, 0, 0))] + [full(a) for a in prm]
    return _pcall(body, name=name,
                  out_shape=(_sds((s, W_GRP), BF16), _sds((s, W_GRP), BF16)) + tuple(_sds(a.shape, F32) for a in prm),
                  grid=(nt,), in_specs=in_specs, out_specs=(rev(0), rev(0)) + tuple(full(a) for a in prm),
                  scratch_shapes=[pltpu.VMEM((1, W_GRP), F32), pltpu.VMEM((8, W_GRP), F32)],
                  semantics=("arbitrary",), block_bytes=40 * _nbytes((t, W_GRP), F32))(z, z, z, dmix, h0s, *prm)


def _pool_inv(i, t):
    pos = (_rows_of((t, W_GRP)) + i * t + 1).astype(F32)
    grp = _lanes_of((t, W_GRP)) // HEAD_DIM
    win = jnp.where(grp == 0, float(POOL_WINDOWS[0]), jnp.where(grp == 1, float(POOL_WINDOWS[1]),
                    jnp.where(grp == 2, float(POOL_WINDOWS[2]), float(POOL_WINDOWS[3]))))
    return 1.0 / jnp.minimum(pos, win)


def _pool_fwd(z, wd, scale, *, name):
    s = z.shape[0]
    t = _pick(s, (512, 256, 128))

    def body(x_ref, halo_ref, wd_ref, sc_ref, y_ref):
        i = pl.program_id(0)
        halo = jnp.where(i == 0, 0.0, halo_ref[...])
        y = _pool_tile(jnp.concatenate([halo, x_ref[...]], axis=0), _pool_inv(i, t), wd_ref[...], sc_ref[...])
        y_ref[...] = y.astype(BF16)

    full = lambda a: pl.BlockSpec(a.shape, lambda i: (0, 0))
    in_specs = [pl.BlockSpec((t, W_GRP), lambda i: (i, 8)),
                pl.BlockSpec((16, W_GRP), lambda i: (jnp.maximum(i * (t // 16) - 1, 0), 8)), full(wd), full(scale)]
    return _pcall(body, name=name, out_shape=_sds((s, W_GRP), BF16), grid=(s // t,), in_specs=in_specs,
                  out_specs=pl.BlockSpec((t, W_GRP), lambda i: (i, 0)), semantics=("parallel",),
                  block_bytes=12 * _nbytes((t, W_GRP), F32))(z, z, wd, scale)


def _pool_bwd(z, dmix, wd, scale, *, name):
    s = z.shape[0]
    t = _pick(s, (512, 256, 128))
    nt = s // t

    def body(x_ref, halo_ref, dy_ref, wd_ref, sc_ref, dx_ref, dwd_ref, dsc_ref, dhalo_scr):
        i = pl.program_id(0)
        r = nt - 1 - i

        @pl.when(i == 0)
        def _():
            dhalo_scr[...] = jnp.zeros_like(dhalo_scr)

        halo = jnp.where(r == 0, 0.0, halo_ref[...])
        inv = _pool_inv(r, t)
        _, vjp = jax.vjp(lambda e, w, sc: _pool_tile(e, inv, w, sc), jnp.concatenate([halo, x_ref[...]], axis=0),
                         wd_ref[...], sc_ref[...])
        dext, dwd, dsc = vjp(dy_ref[...])
        dmain = dext[16:]
        dx = jnp.concatenate([dmain[:t - 16], dmain[t - 16:] + dhalo_scr[...]], axis=0)
        dx_ref[...] = dx.astype(BF16)
        dhalo_scr[...] = dext[:16]
        _acc_out(dwd_ref, dwd, i == 0)
        _acc_out(dsc_ref, dsc, i == 0)

    full = lambda a: pl.BlockSpec(a.shape, lambda i: (0, 0))
    rev = lambda c: pl.BlockSpec((t, W_GRP), lambda i: (nt - 1 - i, c))
    in_specs = [rev(8), pl.BlockSpec((16, W_GRP), lambda i: (jnp.maximum((nt - 1 - i) * (t // 16) - 1, 0), 8)), rev(3),
                full(wd), full(scale)]
    return _pcall(body, name=name, out_shape=(_sds((s, W_GRP), BF16), _sds(wd.shape, F32), _sds(scale.shape, F32)),
                  grid=(nt,), in_specs=in_specs, out_specs=(rev(0), full(wd), full(scale)),
                  scratch_shapes=[pltpu.VMEM((16, W_GRP), F32)], semantics=("arbitrary",),
                  block_bytes=20 * _nbytes((t, W_GRP), F32))(z, z, dmix, wd, scale)


def _hgrn_fwd(z, lb, ngf, *, name):
    s = z.shape[0]
    c = HGRN_CHUNK
    nc = s // c

    def body(q_ref, f_ref, i_ref, g_ref, lb_ref, ng_ref, y_ref, sts_ref, st_scr):
        @pl.when(pl.program_id(0) == 0)
        def _():
            st_scr[...] = jnp.zeros_like(st_scr)

        st = st_scr[...]
        sts_ref[...] = st
        y, st_new = _hgrn_chunk(q_ref[...], f_ref[...], i_ref[...], g_ref[...], st, lb_ref[...], ng_ref[...])
        y_ref[...] = y.astype(BF16)
        st_scr[...] = st_new

    col = lambda k: pl.BlockSpec((c, W_GRP), lambda i: (i, k))
    vec = pl.BlockSpec((1, W_GRP), lambda i: (0, 0))
    return _pcall(body, name=name, out_shape=(_sds((s, W_GRP), BF16), _sds((nc, W_GRP, W_GRP), F32)), grid=(nc,),
                  in_specs=[col(4), col(5), col(6), col(7), vec, vec],
                  out_specs=(pl.BlockSpec((c, W_GRP), lambda i: (i, 0)), pl.BlockSpec((None, W_GRP, W_GRP), lambda i: (i, 0, 0))),
                  scratch_shapes=[pltpu.VMEM((W_GRP, W_GRP), F32)], semantics=("arbitrary",),
                  block_bytes=16 * _nbytes((W_GRP, W_GRP), F32))(z, z, z, z, lb, ngf)


def _hgrn_bwd(z, dmix, sts, lb, ngf, *, name):
    s = z.shape[0]
    c = HGRN_CHUNK
    nc = s // c

    def body(q_ref, f_ref, i_ref, g_ref, dy_ref, st_ref, lb_ref, ng_ref, dz_ref, dlb_ref, dng_ref, dst_scr):
        i = pl.program_id(0)

        @pl.when(i == 0)
        def _():
            dst_scr[...] = jnp.zeros_like(dst_scr)

        _, vjp = jax.vjp(_hgrn_chunk, q_ref[...], f_ref[...], i_ref[...], g_ref[...], st_ref[...], lb_ref[...],
                         ng_ref[...])
        dq, df, di, dg, dst, dlb, dng = vjp((dy_ref[...], dst_scr[...]))
        dz_ref[...] = jnp.concatenate([dq, df, di, dg], axis=1).astype(BF16)
        dst_scr[...] = dst
        _acc_out(dlb_ref, dlb, i == 0)
        _acc_out(dng_ref, dng, i == 0)

    rev = lambda k: pl.BlockSpec((c, W_GRP), lambda i: (nc - 1 - i, k))
    vec = pl.BlockSpec((1, W_GRP), lambda i: (0, 0))
    return _pcall(body, name=name, out_shape=(_sds((s, 4 * W_GRP), BF16), _sds((1, W_GRP), F32), _sds((1, W_GRP), F32)),
                  grid=(nc,),
                  in_specs=[rev(4), rev(5), rev(6), rev(7), rev(2),
                            pl.BlockSpec((None, W_GRP, W_GRP), lambda i: (nc - 1 - i, 0, 0)), vec, vec],
                  out_specs=(pl.BlockSpec((c, 4 * W_GRP), lambda i: (nc - 1 - i, 0)), vec, vec),
                  scratch_shapes=[pltpu.VMEM((W_GRP, W_GRP), F32)], semantics=("arbitrary",),
                  block_bytes=32 * _nbytes((W_GRP, W_GRP), F32))(z, z, z, z, dmix, sts, lb, ngf)


def _lbs_fwd(c_lb, *, name):
    def body(c_ref, o_ref):
        c = c_ref[...]
        e = jnp.exp(c - jnp.max(c, axis=0, keepdims=True))
        sm = e / jnp.sum(e, axis=0, keepdims=True)
        run = jnp.zeros((1, W_GRP), F32)
        o_ref[0:1, :] = run
        for l in range(1, DEPTH):
            run = run + sm[l:l + 1]
            o_ref[l:l + 1, :] = run

    return _pcall(body, name=name, out_shape=_sds((DEPTH, W_GRP), F32))(c_lb)


def _lbs_bwd(c_lb, dlbs, *, name):
    def body(c_ref, d_ref, o_ref):
        c = c_ref[...]
        e = jnp.exp(c - jnp.max(c, axis=0, keepdims=True))
        sm = e / jnp.sum(e, axis=0, keepdims=True)
        d = d_ref[...]
        dsm = [None] * DEPTH
        run = jnp.zeros((1, W_GRP), F32)
        for l in range(DEPTH - 1, 0, -1):
            run = run + d[l:l + 1]
            dsm[l] = run
        dsm[0] = jnp.zeros((1, W_GRP), F32)
        inner = sum(sm[l:l + 1] * dsm[l] for l in range(DEPTH))
        for l in range(DEPTH):
            o_ref[l:l + 1, :] = sm[l:l + 1] * (dsm[l] - inner)

    return _pcall(body, name=name, out_shape=_sds((DEPTH, W_GRP), F32))(c_lb, dlbs)


def _ffn_fwd(hg, hv, cwg, cbg, cwv, cbv, *, name):
    s, n = hg.shape
    t = _pick(s, (256, 128))
    cw = _pick(n, (1408, 256, 128))

    def body(g_ref, gh_ref, v_ref, vh_ref, wg_ref, bg_ref, wv_ref, bv_ref, o_ref):
        first = pl.program_id(1) == 0
        eg = jnp.concatenate([jnp.where(first, 0.0, gh_ref[...]), g_ref[...]], axis=0)
        ev = jnp.concatenate([jnp.where(first, 0.0, vh_ref[...]), v_ref[...]], axis=0)
        o_ref[...] = _ffn_tile(eg, ev, wg_ref[0:1, :], wg_ref[1:2, :], wg_ref[2:3, :], bg_ref[...],
                               wv_ref[0:1, :], wv_ref[1:2, :], wv_ref[2:3, :], bv_ref[...]).astype(BF16)

    main = pl.BlockSpec((t, cw), lambda j, i: (i, j))
    halo = pl.BlockSpec((8, cw), lambda j, i: (jnp.maximum(i * (t // 8) - 1, 0), j))
    w3 = pl.BlockSpec((3, cw), lambda j, i: (0, j))
    w1 = pl.BlockSpec((1, cw), lambda j, i: (0, j))
    return _pcall(body, name=name, out_shape=_sds((s, n), BF16), grid=(n // cw, s // t),
                  in_specs=[main, halo, main, halo, w3, w1, w3, w1], out_specs=main,
                  semantics=("parallel", "parallel"), block_bytes=12 * _nbytes((t, cw), F32))(
                      hg, hg, hv, hv, cwg, cbg, cwv, cbv)


def _ffn_bwd(hg, hv, da, cwg, cbg, cwv, cbv, *, name):
    s, n = hg.shape
    t = _pick(s, (256, 128))
    cw = _pick(n, (1408, 256, 128))
    nt = s // t

    def body(g_ref, gh_ref, v_ref, vh_ref, da_ref, wg_ref, bg_ref, wv_ref, bv_ref, dg_ref, dv_ref, dwg_ref, dwv_ref,
             cg_scr, cv_scr):
        i = pl.program_id(1)
        r = nt - 1 - i

        @pl.when(i == 0)
        def _():
            cg_scr[...] = jnp.zeros_like(cg_scr)
            cv_scr[...] = jnp.zeros_like(cv_scr)

        eg = jnp.concatenate([jnp.where(r == 0, 0.0, gh_ref[...]), g_ref[...]], axis=0)
        ev = jnp.concatenate([jnp.where(r == 0, 0.0, vh_ref[...]), v_ref[...]], axis=0)
        _, vjp = jax.vjp(_ffn_tile, eg, ev, wg_ref[0:1, :], wg_ref[1:2, :], wg_ref[2:3, :], bg_ref[...],
                         wv_ref[0:1, :], wv_ref[1:2, :], wv_ref[2:3, :], bv_ref[...])
        deg, dev, *dps = vjp(da_ref[...])
        for dext, scr, ref in ((deg, cg_scr, dg_ref), (dev, cv_scr, dv_ref)):
            dmain = dext[8:]
            ref[...] = jnp.concatenate([dmain[:t - 8], dmain[t - 8:] + scr[...]], axis=0).astype(BF16)
            scr[...] = dext[:8]
        zeros = jnp.zeros((4, cw), F32)
        _acc_out(dwg_ref, jnp.concatenate(dps[0:4] + [zeros], axis=0), i == 0)
        _acc_out(dwv_ref, jnp.concatenate(dps[4:8] + [zeros], axis=0), i == 0)

    main = pl.BlockSpec((t, cw), lambda j, i: (nt - 1 - i, j))
    halo = pl.BlockSpec((8, cw), lambda j, i: (jnp.maximum((nt - 1 - i) * (t // 8) - 1, 0), j))
    w3 = pl.BlockSpec((3, cw), lambda j, i: (0, j))
    w1 = pl.BlockSpec((1, cw), lambda j, i: (0, j))
    w8 = pl.BlockSpec((8, cw), lambda j, i: (0, j))
    return _pcall(body, name=name,
                  out_shape=(_sds((s, n), BF16), _sds((s, n), BF16), _sds((8, n), F32), _sds((8, n), F32)),
                  grid=(n // cw, nt), in_specs=[main, halo, main, halo, main, w3, w1, w3, w1],
                  out_specs=(main, main, w8, w8),
                  scratch_shapes=[pltpu.VMEM((8, cw), F32), pltpu.VMEM((8, cw), F32)],
                  semantics=("parallel", "arbitrary"), block_bytes=24 * _nbytes((t, cw), F32))(
                      hg, hg, hv, hv, da, cwg, cbg, cwv, cbv)


def _all_gather(x, *, name):
    r, c = x.shape

    def body(x_ref, out_ref, send_sems, recv_sems, local_sem):
        mx, my, mc = lax.axis_index("x"), lax.axis_index("y"), lax.axis_index("c")
        me, sibling = (mx, my, mc), (mx, my, 1 - mc)
        chips = [(1 - mx, my), (mx, 1 - my), (1 - mx, 1 - my)]

        def slot(px, py, pc):
            return out_ref.at[4 * px + 2 * py + pc]

        def copy(k, block, to, src=None):
            return pltpu.make_async_remote_copy(src_ref=slot(*block) if src is None else src, dst_ref=slot(*block),
                                                send_sem=send_sems.at[k], recv_sem=recv_sems.at[k],
                                                device_id=to, device_id_type=MESH)

        mine = pltpu.make_async_copy(x_ref, slot(*me), local_sem)
        mine.start()
        first = [copy(0, me, sibling, src=x_ref)]
        first += [copy(1 + j, me, (*chip, mc), src=x_ref) for j, chip in enumerate(chips)]
        for cp in first:
            cp.start()
        passed = [copy(4 + j, (*chip, mc), sibling) for j, chip in enumerate(chips)]
        for j, chip in enumerate(chips):
            copy(1 + j, (*chip, mc), me).wait_recv()
            passed[j].start()
        copy(0, sibling, me).wait_recv()
        for j, chip in enumerate(chips):
            copy(4 + j, (*chip, 1 - mc), me).wait_recv()
        for cp in first + passed:
            cp.wait_send()
        mine.wait()

    hbm = pl.BlockSpec(memory_space=pl.ANY)
    return _pcall(body, name=name, out_shape=_sds((N_DEV, r, c), x.dtype), in_specs=[hbm], out_specs=hbm,
                  scratch_shapes=[pltpu.SemaphoreType.DMA((7,)), pltpu.SemaphoreType.DMA((7,)),
                                  pltpu.SemaphoreType.DMA(())])(x)


def _swap_sibling(x, *, name):
    def body(x_ref, out_ref, send_sem, recv_sem):
        sibling = (lax.axis_index("x"), lax.axis_index("y"), 1 - lax.axis_index("c"))
        cp = pltpu.make_async_remote_copy(src_ref=x_ref, dst_ref=out_ref, send_sem=send_sem, recv_sem=recv_sem,
                                          device_id=sibling, device_id_type=MESH)
        cp.start()
        cp.wait()

    hbm = pl.BlockSpec(memory_space=pl.ANY)
    return _pcall(body, name=name, out_shape=_sds(x.shape, x.dtype), in_specs=[hbm], out_specs=hbm,
                  scratch_shapes=[pltpu.SemaphoreType.DMA(()), pltpu.SemaphoreType.DMA(())])(x)


def _exchange_chips(p, *, name):
    def body(p_ref, out_ref, send_sems, recv_sems, local_sem):
        mx, my, mc = lax.axis_index("x"), lax.axis_index("y"), lax.axis_index("c")
        mine_q = 2 * mx + my
        chips = [(1 - mx, my), (mx, 1 - my), (1 - mx, 1 - my)]

        def copy(k, chip):
            return pltpu.make_async_remote_copy(src_ref=p_ref.at[2 * chip[0] + chip[1]], dst_ref=out_ref.at[mine_q],
                                                send_sem=send_sems.at[k], recv_sem=recv_sems.at[k],
                                                device_id=(*chip, mc), device_id_type=MESH)

        def arrival(k, chip):
            return pltpu.make_async_remote_copy(src_ref=p_ref.at[mine_q], dst_ref=out_ref.at[2 * chip[0] + chip[1]],
                                                send_sem=send_sems.at[k], recv_sem=recv_sems.at[k],
                                                device_id=(*chip, mc), device_id_type=MESH)

        own = pltpu.make_async_copy(p_ref.at[mine_q], out_ref.at[mine_q], local_sem)
        own.start()
        sends = [copy(k, chip) for k, chip in enumerate(chips)]
        for cp in sends:
            cp.start()
        for k, chip in enumerate(chips):
            arrival(k, chip).wait_recv()
        for cp in sends:
            cp.wait_send()
        own.wait()

    hbm = pl.BlockSpec(memory_space=pl.ANY)
    return _pcall(body, name=name, out_shape=_sds(p.shape, p.dtype), in_specs=[hbm], out_specs=hbm,
                  scratch_shapes=[pltpu.SemaphoreType.DMA((3,)), pltpu.SemaphoreType.DMA((3,)),
                                  pltpu.SemaphoreType.DMA(())])(p)


def _add_pairs(a, b, *, name):
    q, r, c = a.shape
    tr = _pick(r, (544, 408, 272, 136, 64, 32, 16, 8))

    def body(a_ref, b_ref, o_ref):
        o_ref[...] = (a_ref[...].astype(F32) + b_ref[...].astype(F32)).astype(o_ref.dtype)

    blk = pl.BlockSpec((None, tr, c), lambda i, j: (i, j, 0))
    return _pcall(body, name=name, out_shape=_sds(a.shape, a.dtype), grid=(q, r // tr), in_specs=[blk, blk],
                  out_specs=blk, semantics=("parallel", "parallel"), block_bytes=4 * _nbytes((tr, c), F32))(a, b)


def _sum_slots(p, *, name):
    q, r, c = p.shape
    tr = _pick(r, (544, 408, 272, 192, 136, 64, 32, 16, 8))

    def body(p_ref, o_ref):
        acc = p_ref[0].astype(F32)
        for k in range(1, q):
            acc = acc + p_ref[k].astype(F32)
        o_ref[...] = acc

    return _pcall(body, name=name, out_shape=_sds((r, c), F32), grid=(r // tr,),
                  in_specs=[pl.BlockSpec((q, tr, c), lambda i: (0, i, 0))],
                  out_specs=pl.BlockSpec((tr, c), lambda i: (i, 0)), semantics=("parallel",),
                  block_bytes=(q + 2) * _nbytes((tr, c), F32))(p)


def _adamw(w, g, m, v, *, name):
    r, c = w.shape
    tr = _pick(r, (512, 256, 192, 128, 64, 32, 16, 8))
    c1 = 1.0 / (1.0 - ADAM_B1 ** ADAM_STEP)
    c2 = 1.0 / (1.0 - ADAM_B2 ** ADAM_STEP)

    def body(w_ref, g_ref, m_ref, v_ref, d_ref, nm_ref, nv_ref):
        gv = g_ref[...]
        nm = ADAM_B1 * m_ref[...] + (1.0 - ADAM_B1) * gv
        nv = ADAM_B2 * v_ref[...] + (1.0 - ADAM_B2) * jnp.square(gv)
        d_ref[...] = -ADAM_LR * ((nm * c1) / (jnp.sqrt(nv * c2) + ADAM_EPS) + ADAM_WD * w_ref[...])
        nm_ref[...] = nm
        nv_ref[...] = nv

    blk = pl.BlockSpec((tr, c), lambda i: (i, 0))
    out = _sds((r, c), F32)
    return _pcall(body, name=name, out_shape=(out, out, out), grid=(r // tr,), in_specs=[blk] * 4,
                  out_specs=(blk, blk, blk), semantics=("parallel",), block_bytes=7 * _nbytes((tr, c), F32))(w, g, m, v)


def _pack_flat(arrs, rows, cols=1024):
    flat = jnp.concatenate([a.reshape(-1).astype(F32) for a in arrs])
    pad = rows * cols - flat.shape[0]
    return jnp.pad(flat, (0, pad)).reshape(rows, cols)


def _unpack_flat(buf, shapes):
    flat = buf.reshape(-1)
    out, off = [], 0
    for shp in shapes:
        n = 1
        for s in shp:
            n *= s
        out.append(flat[off:off + n].reshape(shp))
        off += n
    return out


def _flat_rows(shapes, cols=1024):
    n = sum(functools.reduce(lambda a, b: a * b, shp, 1) for shp in shapes)
    rows = -(-n // cols)
    return -(-rows // 64) * 64


def _block_diag(w4):
    out = jnp.zeros((W_GRP, W_GRP), w4.dtype)
    for h in range(N_HEADS):
        out = lax.dynamic_update_slice(out, w4[h], (h * HEAD_DIM, h * HEAD_DIM))
    return out


def _diag_blocks(w):
    return jnp.stack([w[h * HEAD_DIM:(h + 1) * HEAD_DIM, h * HEAD_DIM:(h + 1) * HEAD_DIM] for h in range(N_HEADS)])


def _pack_big_shards(w):
    rows = []
    for l in range(DEPTH):
        rows += [w['w_in'][l].T, w['w_out'][l], w['w_up'][l].T, w['w_down'][l],
                 w['w_pe'][l].T.reshape(32, 1024), w['w_pg'][l]]
    return jnp.concatenate(rows, axis=0)


def _unpack_big_full(g, l):
    out, off = {}, l * LAYER_ROWS
    for nm, r in SLAB_ROWS:
        blk = g[:, off:off + r, :]
        if nm == 'w_pe':
            out[nm] = blk.reshape(N_DEV, 128, PLE_DIM).reshape(N_DEV * 128, PLE_DIM)
        elif nm == 'w_up':
            out['w_up_g'] = blk[:N_DEV // 2].reshape(D_FF, D_MODEL)
            out['w_up_v'] = blk[N_DEV // 2:].reshape(D_FF, D_MODEL)
        else:
            out[nm] = blk.reshape(N_DEV * r, D_MODEL)
        off += r
    return out


def _pack_big_grads(gl):
    cols = []
    for l in range(DEPTH):
        g = gl[l]
        cols += [g['w_in'].reshape(N_DEV, 288, D_MODEL), g['w_out'].reshape(N_DEV, 128, D_MODEL),
                 jnp.concatenate([g['w_up_g'], g['w_up_v']], axis=0).reshape(N_DEV, 704, D_MODEL),
                 g['w_down'].reshape(N_DEV, 352, D_MODEL),
                 g['w_pe'].reshape(N_DEV, 128, PLE_DIM).reshape(N_DEV, 32, D_MODEL), g['w_pg'].reshape(N_DEV, 128, D_MODEL)]
    return jnp.concatenate(cols, axis=1)


def _unpack_big_shard(gs):
    out = {nm: [] for nm in BIG_NAMES}
    for l in range(DEPTH):
        off = l * LAYER_ROWS
        for nm, r in SLAB_ROWS:
            blk = gs[off:off + r]
            if nm in ('w_in', 'w_up'):
                blk = blk.T
            elif nm == 'w_pe':
                blk = blk.reshape(128, PLE_DIM).T
            out[nm].append(blk)
            off += r
    return {nm: jnp.stack(v) for nm, v in out.items()}


def _layer_params(w, lbs, l):
    tril = jnp.tril(jnp.ones((GMLP_CHUNK, GMLP_CHUNK), bool))
    row = lambda a: a.reshape(1, -1)
    return dict(
        g1=row(w['norm1_g'][l]), g2=row(w['norm2_g'][l]), g3=row(w['norm3_g'][l]),
        a_ln_g=row(w['a_ln_g'][l]), a_ln_b=row(w['a_ln_b'][l]),
        a_wcat=jnp.where(tril, w['a_ws'][l], 0.0).reshape(N_HEADS * GMLP_CHUNK, GMLP_CHUNK),
        a_bfull=jnp.repeat(w['a_bs'][l].T, HEAD_DIM, axis=1),
        b_prm=[row(w['b_conv_w_full'][l][k]) for k in range(4)] + [
            row(w['b_conv_b'][l]), _block_diag(w['b_wa'][l]), row(w['b_ba'][l]), _block_diag(w['b_wx'][l]),
            row(w['b_bx'][l]), row(w['b_lam'][l])],
        c_lb=row(lbs[l]), c_ngf=row(jnp.tile(w['c_norm_g'][l], N_HEADS)),
        d_wd=_block_diag(w['d_w'][l]), d_scale=row(w['d_scale'][l]),
        f_cwg=w['ffn_conv_w_full'][l][:, :D_FF], f_cwv=w['ffn_conv_w_full'][l][:, D_FF:],
        f_cbg=row(w['ffn_conv_b'][l][:D_FF]), f_cbv=row(w['ffn_conv_b'][l][D_FF:]),
    )


def _layer_fwd(x, p_bf, wb, sp, l):
    n = lambda s: f"l{l}_{s}"
    h = _rms_fwd(x, sp['g1'], name=n("norm1"))
    z = _matmul(h, wb['w_in'], nt=True, name=n("proj_in"))
    ya = _gmlp_fwd(z, sp['a_ln_g'], sp['a_ln_b'], sp['a_wcat'], sp['a_bfull'], name=n("gmlp"))
    yb, h0s = _rglru_fwd(z, sp['b_prm'], name=n("rglru"))
    yc, sts = _hgrn_fwd(z, sp['c_lb'], sp['c_ngf'], name=n("hgrn"))
    yd = _pool_fwd(z, sp['d_wd'], sp['d_scale'], name=n("pool"))
    mix = jnp.concatenate([ya, yb, yc, yd], axis=1)
    x1 = _matmul(mix, wb['w_out'], res=x, name=n("proj_out"))
    h2 = _rms_fwd(x1, sp['g2'], name=n("norm2"))
    hg = _matmul(h2, wb['w_up_g'], nt=True, name=n("up_gate"))
    hv = _matmul(h2, wb['w_up_v'], nt=True, name=n("up_val"))
    a = _ffn_fwd(hg, hv, sp['f_cwg'], sp['f_cbg'], sp['f_cwv'], sp['f_cbv'], name=n("ffn_gate"))
    x2 = _matmul(a, wb['w_down'], res=x1, name=n("down"))
    h3 = _rms_fwd(x2, sp['g3'], name=n("norm3"))
    gl = _matmul(h3, wb['w_pg'], name=n("ple_gate"))
    pe = _matmul(p_bf, wb['w_pe'], nt=True, name=n("ple_emb"))
    x3 = _ple_fwd(x2, gl, pe, name=n("ple"))
    saved = dict(x=x, h=h, z=z, h0s=h0s, sts=sts, mix=mix, x1=x1, h2=h2, hg=hg, hv=hv, a=a, x2=x2, h3=h3, gl=gl, pe=pe)
    return x3, saved


def _layer_bwd(dx3, sv, p_bf, wb, sp, l):
    n = lambda s: f"l{l}_{s}_bwd"
    gb, gs = {}, {}
    dpe, dgl = _ple_bwd(dx3, sv['gl'], sv['pe'], name=n("ple"))
    gb['w_pe'] = _matmul_tn(dpe, p_bf, name=n("ple_emb_w"))
    gb['w_pg'] = _matmul_tn(sv['h3'], dgl, name=n("ple_gate_w"))
    dh3 = _matmul(dgl, wb['w_pg'], nt=True, name=n("ple_gate_x"))
    dx2, dx2b, gs['norm3_g'] = _rms_bwd(sv['x2'], sp['g3'], dh3, dx3, name=n("norm3"))
    da = _matmul(dx2b, wb['w_down'], nt=True, name=n("down_x"))
    gb['w_down'] = _matmul_tn(sv['a'], dx2b, name=n("down_w"))
    dhg, dhv, dwg, dwv = _ffn_bwd(sv['hg'], sv['hv'], da, sp['f_cwg'], sp['f_cbg'], sp['f_cwv'], sp['f_cbv'],
                                  name=n("ffn_gate"))
    gs['ffn_conv_w'] = jnp.concatenate([dwg[0:3], dwv[0:3]], axis=1)
    gs['ffn_conv_b'] = jnp.concatenate([dwg[3], dwv[3]], axis=0)
    gb['w_up_g'] = _matmul_tn(dhg, sv['h2'], name=n("up_gate_w"))
    gb['w_up_v'] = _matmul_tn(dhv, sv['h2'], name=n("up_val_w"))
    dh2 = _matmul(dhg, wb['w_up_g'], name=n("up_gate_x"))
    dh2 = _matmul(dhv, wb['w_up_v'], res=dh2, name=n("up_val_x"))
    dx1, dx1b, gs['norm2_g'] = _rms_bwd(sv['x1'], sp['g2'], dh2, dx2, name=n("norm2"))
    dmix = _matmul(dx1b, wb['w_out'], nt=True, name=n("proj_out_x"))
    gb['w_out'] = _matmul_tn(sv['mix'], dx1b, name=n("proj_out_w"))
    z = sv['z']
    dzu, dzv, gs['a_ln_g'], gs['a_ln_b'], dwcat, dbfull = _gmlp_bwd(
        z, dmix, sp['a_ln_g'], sp['a_ln_b'], sp['a_wcat'], sp['a_bfull'], name=n("gmlp"))
    tril = jnp.tril(jnp.ones((GMLP_CHUNK, GMLP_CHUNK), bool))
    gs['a_ws'] = jnp.where(tril, dwcat.reshape(N_HEADS, GMLP_CHUNK, GMLP_CHUNK), 0.0)
    gs['a_bs'] = dbfull.reshape(GMLP_CHUNK, N_HEADS, HEAD_DIM).sum(-1).T
    dzb, dzg, *dbp = _rglru_bwd(z, dmix, sv['h0s'], sp['b_prm'], name=n("rglru"))
    gs['b_conv_w'] = jnp.concatenate(dbp[0:4], axis=0)
    gs['b_conv_b'], gs['b_ba'], gs['b_bx'], gs['b_lam'] = dbp[4], dbp[6], dbp[8], dbp[9]
    gs['b_wa'], gs['b_wx'] = _diag_blocks(dbp[5]), _diag_blocks(dbp[7])
    dzc, dlb, dngf = _hgrn_bwd(z, dmix, sv['sts'], sp['c_lb'], sp['c_ngf'], name=n("hgrn"))
    gs['c_lbs'] = dlb
    gs['c_norm_g'] = dngf.reshape(N_HEADS, HEAD_DIM).sum(0)
    dzd, dwd, gs['d_scale'] = _pool_bwd(z, dmix, sp['d_wd'], sp['d_scale'], name=n("pool"))
    gs['d_w'] = _diag_blocks(dwd)
    dz = jnp.concatenate([dzu, dzv, dzb, dzg, dzc, dzd], axis=1)
    gb['w_in'] = _matmul_tn(dz, sv['h'], name=n("proj_in_w"))
    dh = _matmul(dz, wb['w_in'], name=n("proj_in_x"))
    dx0, _, gs['norm1_g'] = _rms_bwd(sv['x'], sp['g1'], dh, dx1, name=n("norm1"))
    return dx0, gb, gs


SMALL_NAMES = [nm for nm in WEIGHT_NAMES if nm not in BIG_NAMES]


def _step(w, m, v, x, p, target):
    s = x.shape[1]
    dev = 4 * lax.axis_index("x") + 2 * lax.axis_index("y") + lax.axis_index("c")
    xs = x.reshape(s, D_MODEL)

    gathered = _all_gather(_pack_big_shards(w).astype(BF16), name="gather_weights")
    conv_shapes = [w['b_conv_w'].shape, w['ffn_conv_w'].shape]
    conv_rows = _flat_rows(conv_shapes)
    conv_all = _all_gather(_pack_flat([w['b_conv_w'], w['ffn_conv_w']], conv_rows), name="gather_conv_weights")
    parts = [_unpack_flat(conv_all[d], conv_shapes) for d in range(N_DEV)]
    wf = dict(w)
    wf['b_conv_w_full'] = jnp.concatenate([pt[0] for pt in parts], axis=-1)
    wf['ffn_conv_w_full'] = jnp.concatenate([pt[1] for pt in parts], axis=-1)
    lbs = _lbs_fwd(w['c_lb'], name="hgrn_bounds")

    xl, saved, wbs, sps = xs, [], [], []
    for l in range(DEPTH):
        wb = _unpack_big_full(gathered, l)
        sp = _layer_params(wf, lbs, l)
        p_bf = p[l, 0].astype(BF16)
        xl, sv = _layer_fwd(xl, p_bf, wb, sp, l)
        saved.append((sv, p_bf))
        wbs.append(wb)
        sps.append(sp)
    loss_part, dx, dfinal = _loss_head(xl, w['final_g'].reshape(1, D_MODEL), target.reshape(s, D_MODEL), name="loss_head")
    loss = lax.psum(loss_part[0, 0], ("x", "y", "c"))

    big, small = [None] * DEPTH, [None] * DEPTH
    for l in range(DEPTH - 1, -1, -1):
        sv, p_bf = saved[l]
        dx, big[l], small[l] = _layer_bwd(dx, sv, p_bf, wbs[l], sps[l], l)
    grad_x = dx.reshape(1, s, D_MODEL)

    packed = _pack_big_grads(big).reshape(4, 2, PACK_ROWS, D_MODEL)
    core = lax.axis_index("c")
    own = lax.dynamic_index_in_dim(packed, core, axis=1, keepdims=False)
    other = lax.dynamic_index_in_dim(packed, 1 - core, axis=1, keepdims=False)
    from_sibling = _swap_sibling(other, name="reduce_pair")
    chip_sum = _add_pairs(own, from_sibling, name="reduce_pair_add")
    from_chips = _exchange_chips(chip_sum, name="reduce_chips")
    gbig = _unpack_big_shard(_sum_slots(from_chips, name="reduce_chips_add"))

    lbs_grad = jnp.concatenate([small[l].pop('c_lbs') for l in range(DEPTH)], axis=0)
    c_lb_grad = _lbs_bwd(w['c_lb'], lbs_grad, name="hgrn_bounds_bwd")
    per_layer = [nm for nm in SMALL_NAMES if nm not in ('c_lb', 'final_g')]
    small_parts = {nm: jnp.stack([small[l][nm].reshape(wf[nm + '_full'].shape[1:] if nm in ('b_conv_w', 'ffn_conv_w')
                                                       else w[nm].shape[1:]) for l in range(DEPTH)])
                   for nm in per_layer}
    small_parts['c_lb'] = c_lb_grad
    small_parts['final_g'] = dfinal.reshape(D_MODEL)
    small_shapes = [small_parts[nm].shape for nm in SMALL_NAMES]
    small_rows = _flat_rows(small_shapes)
    small_all = _all_gather(_pack_flat([small_parts[nm] for nm in SMALL_NAMES], small_rows), name="gather_small_grads")
    gsmall = dict(zip(SMALL_NAMES, _unpack_flat(_sum_slots(small_all, name="sum_small_grads"), small_shapes)))
    for nm in ('b_conv_w', 'ffn_conv_w'):
        width = w[nm].shape[-1]
        gsmall[nm] = lax.dynamic_slice_in_dim(gsmall[nm], dev * width, width, axis=2)

    grads, delta, new_m, new_v = {}, {}, {}, {}
    for nm in BIG_NAMES:
        shp = w[nm].shape
        as2d = lambda a: a.reshape(shp[0] * shp[1], shp[2])
        grads[nm] = gbig[nm]
        d, nm_, nv_ = _adamw(as2d(w[nm]), as2d(gbig[nm]), as2d(m[nm]), as2d(v[nm]), name=f"adamw_{nm}")
        delta[nm], new_m[nm], new_v[nm] = d.reshape(shp), nm_.reshape(shp), nv_.reshape(shp)
    shapes = [w[nm].shape for nm in SMALL_NAMES]
    rows = _flat_rows(shapes)
    pk = lambda t: _pack_flat([t[nm] for nm in SMALL_NAMES], rows)
    d, nm_, nv_ = _adamw(pk(w), pk(gsmall), pk(m), pk(v), name="adamw_small")
    for nm, dd, mm_, vv_ in zip(SMALL_NAMES, _unpack_flat(d, shapes), _unpack_flat(nm_, shapes), _unpack_flat(nv_, shapes)):
        grads[nm], delta[nm], new_m[nm], new_v[nm] = gsmall[nm], dd, mm_, vv_

    return (loss, grad_x, *[grads[nm] for nm in WEIGHT_NAMES], *[delta[nm] for nm in WEIGHT_NAMES],
            *[new_m[nm] for nm in WEIGHT_NAMES], *[new_v[nm] for nm in WEIGHT_NAMES])


def kernel(x, p, norm1_g, w_in, a_ln_g, a_ln_b, a_ws, a_bs, b_conv_w, b_conv_b, b_wa, b_ba, b_wx, b_bx, b_lam, c_lb, c_norm_g, d_w, d_scale, w_out, norm2_g, w_up, ffn_conv_w, ffn_conv_b, w_down, norm3_g, w_pe, w_pg, final_g, loss_target, m_norm1_g, m_w_in, m_a_ln_g, m_a_ln_b, m_a_ws, m_a_bs, m_b_conv_w, m_b_conv_b, m_b_wa, m_b_ba, m_b_wx, m_b_bx, m_b_lam, m_c_lb, m_c_norm_g, m_d_w, m_d_scale, m_w_out, m_norm2_g, m_w_up, m_ffn_conv_w, m_ffn_conv_b, m_w_down, m_norm3_g, m_w_pe, m_w_pg, m_final_g, v_norm1_g, v_w_in, v_a_ln_g, v_a_ln_b, v_a_ws, v_a_bs, v_b_conv_w, v_b_conv_b, v_b_wa, v_b_ba, v_b_wx, v_b_bx, v_b_lam, v_c_lb, v_c_norm_g, v_d_w, v_d_scale, v_w_out, v_norm2_g, v_w_up, v_ffn_conv_w, v_ffn_conv_b, v_w_down, v_norm3_g, v_w_pe, v_w_pg, v_final_g):
    w = dict(norm1_g=norm1_g, w_in=w_in, a_ln_g=a_ln_g, a_ln_b=a_ln_b, a_ws=a_ws, a_bs=a_bs, b_conv_w=b_conv_w, b_conv_b=b_conv_b, b_wa=b_wa, b_ba=b_ba, b_wx=b_wx, b_bx=b_bx, b_lam=b_lam, c_lb=c_lb, c_norm_g=c_norm_g, d_w=d_w, d_scale=d_scale, w_out=w_out, norm2_g=norm2_g, w_up=w_up, ffn_conv_w=ffn_conv_w, ffn_conv_b=ffn_conv_b, w_down=w_down, norm3_g=norm3_g, w_pe=w_pe, w_pg=w_pg, final_g=final_g)
    m = dict(norm1_g=m_norm1_g, w_in=m_w_in, a_ln_g=m_a_ln_g, a_ln_b=m_a_ln_b, a_ws=m_a_ws, a_bs=m_a_bs, b_conv_w=m_b_conv_w, b_conv_b=m_b_conv_b, b_wa=m_b_wa, b_ba=m_b_ba, b_wx=m_b_wx, b_bx=m_b_bx, b_lam=m_b_lam, c_lb=m_c_lb, c_norm_g=m_c_norm_g, d_w=m_d_w, d_scale=m_d_scale, w_out=m_w_out, norm2_g=m_norm2_g, w_up=m_w_up, ffn_conv_w=m_ffn_conv_w, ffn_conv_b=m_ffn_conv_b, w_down=m_w_down, norm3_g=m_norm3_g, w_pe=m_w_pe, w_pg=m_w_pg, final_g=m_final_g)
    v = dict(norm1_g=v_norm1_g, w_in=v_w_in, a_ln_g=v_a_ln_g, a_ln_b=v_a_ln_b, a_ws=v_a_ws, a_bs=v_a_bs, b_conv_w=v_b_conv_w, b_conv_b=v_b_conv_b, b_wa=v_b_wa, b_ba=v_b_ba, b_wx=v_b_wx, b_bx=v_b_bx, b_lam=v_b_lam, c_lb=v_c_lb, c_norm_g=v_c_norm_g, d_w=v_d_w, d_scale=v_d_scale, w_out=v_w_out, norm2_g=v_norm2_g, w_up=v_w_up, ffn_conv_w=v_ffn_conv_w, ffn_conv_b=v_ffn_conv_b, w_down=v_w_down, norm3_g=v_norm3_g, w_pe=v_w_pe, w_pg=v_w_pg, final_g=v_final_g)
    return _step(w, m, v, x, p, loss_target)
```

```python
import functools

import jax
import jax.numpy as jnp
from jax import lax
from jax.experimental import pallas as pl
from jax.experimental.pallas import tpu as pltpu

F32 = jnp.float32
BF16 = jnp.bfloat16
MESH = pl.DeviceIdType.MESH

D_MODEL = 1024
DEPTH = 4
PLE_DIM = 256
W_GRP = 256
N_HEADS = 4
HEAD_DIM = 64
GMLP_CHUNK = 128
RGLRU_C = 8.0
HGRN_CHUNK = 64
HGRN_SUB = 16
POOL_WINDOWS = (2, 4, 8, 16)
D_FF = 2816
D_PROJ = 2304
EPS = 1e-6
ADAM_LR = 0.001
ADAM_B1 = 0.9
ADAM_B2 = 0.999
ADAM_EPS = 1e-08
ADAM_WD = 0.01
ADAM_STEP = 10

N_DEV = 8
MIB = 2 ** 20
V7X_VMEM_BYTES = 64 * MIB
HGRN_EXP_CLAMP = 60.0

WEIGHT_NAMES = ['norm1_g', 'w_in', 'a_ln_g', 'a_ln_b', 'a_ws', 'a_bs', 'b_conv_w', 'b_conv_b', 'b_wa', 'b_ba', 'b_wx',
                'b_bx', 'b_lam', 'c_lb', 'c_norm_g', 'd_w', 'd_scale', 'w_out', 'norm2_g', 'w_up', 'ffn_conv_w',
                'ffn_conv_b', 'w_down', 'norm3_g', 'w_pe', 'w_pg', 'final_g']
BIG_NAMES = ('w_in', 'w_out', 'w_up', 'w_down', 'w_pe', 'w_pg')
SLAB_ROWS = (('w_in', 288), ('w_out', 128), ('w_up', 704), ('w_down', 352), ('w_pe', 32), ('w_pg', 128))
LAYER_ROWS = sum(r for _, r in SLAB_ROWS)
PACK_ROWS = DEPTH * LAYER_ROWS


def _vmem_limit(block_bytes):
    want = 2 * block_bytes + 24 * MIB
    return int(min(max(want, 32 * MIB), V7X_VMEM_BYTES - 8 * MIB))


def _pcall(body, *, name, out_shape, grid=None, in_specs=None, out_specs=None, scratch_shapes=(),
           semantics=None, block_bytes=0):
    kw = {}
    if grid is not None:
        kw["grid"] = grid
    if in_specs is not None:
        kw["in_specs"] = in_specs
    if out_specs is not None:
        kw["out_specs"] = out_specs
    params = pltpu.CompilerParams(dimension_semantics=semantics, vmem_limit_bytes=_vmem_limit(block_bytes))
    return pl.pallas_call(body, name=name, out_shape=out_shape, scratch_shapes=list(scratch_shapes),
                          compiler_params=params, **kw)


def _pick(n, cands):
    for c in cands:
        if n % c == 0:
            return c
    return n


def _nbytes(shape, dtype):
    n = 1
    for s in shape:
        n *= s
    return n * jnp.dtype(dtype).itemsize


def _sds(shape, dtype):
    return jax.ShapeDtypeStruct(tuple(shape), dtype)


class _Sel:
    def __init__(self, arr, *idx):
        self.arr, self.idx = arr, tuple(idx)
        self.shape = arr.shape[len(idx):]
        self.ndim = len(self.shape)
        self.dtype = arr.dtype


def _arr(a):
    return a.arr if isinstance(a, _Sel) else a


def _spec(a, block=None, index=None):
    block = tuple(a.shape) if block is None else tuple(block)
    index = (lambda *g: (0,) * len(block)) if index is None else index
    if isinstance(a, _Sel):
        lead = a.idx
        return pl.BlockSpec((None,) * len(lead) + block, lambda *g: lead + tuple(index(*g)))
    return pl.BlockSpec(block, lambda *g: tuple(index(*g)))


def _ospec(a):
    return pl.BlockSpec(tuple(a.shape), lambda *g: (0,) * a.ndim)


def _rows_of(shape):
    return lax.broadcasted_iota(jnp.int32, shape, 0)


def _lanes_of(shape):
    return lax.broadcasted_iota(jnp.int32, shape, 1)


def _sdn(x, k, fill):
    n = x.shape[0]
    return jnp.where(_rows_of(x.shape) >= k, pltpu.roll(x, k % n, 0), fill)


def _sup(x, k, fill):
    n = x.shape[0]
    return jnp.where(_rows_of(x.shape) < n - k, pltpu.roll(x, (n - k) % n, 0), fill)


@functools.partial(jax.custom_vjp, nondiff_argnums=(1,))
def _shift_dn(x, k):
    return _sdn(x, k, 0.0)


def _shift_dn_fwd(x, k):
    return _sdn(x, k, 0.0), None


def _shift_dn_bwd(k, _, g):
    return (_sup(g, k, 0.0),)


_shift_dn.defvjp(_shift_dn_fwd, _shift_dn_bwd)


def _lin_scan_impl(a, b, h0):
    n = a.shape[0]
    aa, bb = a, b
    k = 1
    while k < n:
        bb = aa * _sdn(bb, k, 0.0) + bb
        aa = aa * _sdn(aa, k, 1.0)
        k *= 2
    return bb + aa * h0


@jax.custom_vjp
def _lin_scan(a, b, h0):
    return _lin_scan_impl(a, b, h0)


def _lin_scan_fwd(a, b, h0):
    h = _lin_scan_impl(a, b, h0)
    return h, (a, h, h0)


def _lin_scan_bwd(res, g):
    a, h, h0 = res
    n = a.shape[0]
    cc, gg = _sup(a, 1, 0.0), g
    k = 1
    while k < n:
        gg = gg + cc * _sup(gg, k, 0.0)
        cc = cc * _sup(cc, k, 1.0)
        k *= 2
    first = _rows_of(a.shape) == 0
    hprev = jnp.where(first, h0, _sdn(h, 1, 0.0))
    dh0 = jnp.sum(jnp.where(first, a * gg, 0.0), axis=0, keepdims=True)
    return gg * hprev, gg, dh0


_lin_scan.defvjp(_lin_scan_fwd, _lin_scan_bwd)


def _cumsum_sub_impl(x):
    pos = _rows_of(x.shape) % HGRN_SUB
    k = 1
    while k < HGRN_SUB:
        x = x + jnp.where(pos >= k, pltpu.roll(x, k, 0), 0.0)
        k *= 2
    return x


@jax.custom_vjp
def _cumsum_sub(x):
    return _cumsum_sub_impl(x)


def _cumsum_sub_fwd(x):
    return _cumsum_sub_impl(x), None


def _cumsum_sub_bwd(_, g):
    n = g.shape[0]
    pos = _rows_of(g.shape) % HGRN_SUB
    k = 1
    while k < HGRN_SUB:
        g = g + jnp.where(pos < HGRN_SUB - k, pltpu.roll(g, n - k, 0), 0.0)
        k *= 2
    return (g,)


_cumsum_sub.defvjp(_cumsum_sub_fwd, _cumsum_sub_bwd)


def _dot(a, b, ca, cb):
    return lax.dot_general(a.astype(BF16), b.astype(BF16), (((ca,), (cb,)), ((), ())), preferred_element_type=F32)


@jax.custom_vjp
def _mm(a, b):
    return _dot(a, b, 1, 0)


def _mm_fwd(a, b):
    return _dot(a, b, 1, 0), (a, b)


def _mm_bwd(res, g):
    a, b = res
    return _dot(g, b, 1, 1), _dot(a, g, 0, 0)


_mm.defvjp(_mm_fwd, _mm_bwd)


@jax.custom_vjp
def _mm_nt(a, b):
    return _dot(a, b, 1, 1)


def _mm_nt_fwd(a, b):
    return _dot(a, b, 1, 1), (a, b)


def _mm_nt_bwd(res, g):
    a, b = res
    return _dot(g, b, 1, 0), _dot(g, a, 0, 0)


_mm_nt.defvjp(_mm_nt_fwd, _mm_nt_bwd)


@jax.custom_vjp
def _mm_tn(a, b):
    return _dot(a, b, 0, 0)


def _mm_tn_fwd(a, b):
    return _dot(a, b, 0, 0), (a, b)


def _mm_tn_bwd(res, g):
    a, b = res
    return _dot(b, g, 1, 1), _dot(a, g, 1, 0)


_mm_tn.defvjp(_mm_tn_fwd, _mm_tn_bwd)


def _head_mask(shape, h):
    return (_lanes_of(shape) // HEAD_DIM) == h


def _stack_heads(x):
    return jnp.concatenate([jnp.where(_head_mask(x.shape, h), x, 0.0) for h in range(N_HEADS)], axis=0)


def _unstack_heads(p):
    r = p.shape[0] // N_HEADS
    out = None
    for h in range(N_HEADS):
        blk = p[h * r:(h + 1) * r]
        term = jnp.where(_head_mask(blk.shape, h), blk, 0.0)
        out = term if out is None else out + term
    return out


def _segmean_impl(x):
    n = x.shape[1]
    same = (lax.broadcasted_iota(jnp.int32, (n, n), 0) // HEAD_DIM) == (lax.broadcasted_iota(jnp.int32, (n, n), 1) // HEAD_DIM)
    m = jnp.where(same, 1.0 / HEAD_DIM, 0.0).astype(BF16)
    hi = x.astype(BF16)
    lo = (x - hi.astype(F32)).astype(BF16)
    dn = (((1,), (0,)), ((), ()))
    return (lax.dot_general(hi, m, dn, preferred_element_type=F32)
            + lax.dot_general(lo, m, dn, preferred_element_type=F32))


@jax.custom_vjp
def _segmean(x):
    return _segmean_impl(x)


def _segmean_fwd(x):
    return _segmean_impl(x), None


def _segmean_bwd(_, g):
    return (_segmean_impl(g),)


_segmean.defvjp(_segmean_fwd, _segmean_bwd)


def _log1p(u):
    w = 1.0 + u
    return jnp.where(w == 1.0, u, jnp.log(w) * (u / (w - 1.0)))


def _softplus(y):
    return jnp.maximum(y, 0.0) + _log1p(jnp.exp(-jnp.abs(y)))


def _rms(x, g):
    return x * lax.rsqrt(jnp.mean(x * x, axis=-1, keepdims=True) + EPS) * g


def _gmlp_chunk(zu, zv, ln_g, ln_b, wcat, bfull):
    u = jax.nn.gelu(zu)
    v = jax.nn.gelu(zv)
    mu = jnp.mean(v, axis=-1, keepdims=True)
    var = jnp.mean(jnp.square(v - mu), axis=-1, keepdims=True)
    vn = (v - mu) * lax.rsqrt(var + EPS) * ln_g + ln_b
    sv = _unstack_heads(_mm(wcat, vn)) + bfull
    return u * sv


def _rglru_tile(xb_ext, gb, h0, cw, cb, wa, ba, wx, bx, lam):
    xc = (cb + cw[0:1] * _shift_dn(xb_ext, 3) + cw[1:2] * _shift_dn(xb_ext, 2) + cw[2:3] * _shift_dn(xb_ext, 1)
          + cw[3:4] * xb_ext)[8:]
    r = jax.nn.sigmoid(_mm(xc, wa) + ba)
    i = jax.nn.sigmoid(_mm(xc, wx) + bx)
    log_a = (-RGLRU_C) * r * _softplus(-lam)
    a = jnp.exp(log_a)
    mult = jnp.sqrt(-jnp.tanh(log_a) * (a * a + 1.0))
    h = _lin_scan(a, mult * (i * xc), h0)
    y = h * jax.nn.gelu(gb)
    h_last = jnp.sum(jnp.where(_rows_of(h.shape) == h.shape[0] - 1, h, 0.0), axis=0, keepdims=True)
    return y, h_last


def _pool_tile(xd_ext, inv, wd, scale):
    s1 = xd_ext + _shift_dn(xd_ext, 1)
    s2 = s1 + _shift_dn(s1, 2)
    s3 = s2 + _shift_dn(s2, 4)
    s4 = s3 + _shift_dn(s3, 8)
    grp = _lanes_of(xd_ext.shape) // HEAD_DIM
    win = jnp.where(grp == 0, s1, jnp.where(grp == 1, s2, jnp.where(grp == 2, s3, s4)))
    pooled = win[16:] * inv - xd_ext[16:]
    return _mm(pooled, wd) * scale


def _hgrn_chunk(q, f, i, g, st, lb, ngf):
    n = q.shape[0]
    nsub = n // HGRN_SUB
    qs = jax.nn.silu(q)
    fg = lb + (1.0 - lb) * jax.nn.sigmoid(f)
    lf = jnp.log(fg)
    k = 1.0 - fg
    bl = _cumsum_sub(lf)
    row = _rows_of(q.shape)
    blk = row // HGRN_SUB
    betas = [jnp.zeros_like(lb)]
    for s in range(nsub):
        tot = jnp.sum(jnp.where(row == s * HGRN_SUB + HGRN_SUB - 1, bl, 0.0), axis=0, keepdims=True)
        betas.append(betas[-1] + tot)
    b_end = betas[nsub]
    beta_full = jnp.zeros_like(q)
    for s in range(1, nsub):
        beta_full = jnp.where(blk == s, betas[s], beta_full)
    qh = qs * jnp.exp(bl)
    qt = qh * jnp.exp(beta_full)
    b_all = beta_full + bl
    kt = k * jnp.exp(b_end - b_all)
    outs = []
    for s in range(nsub):
        kh = k * jnp.exp(jnp.minimum(betas[s] - b_all, HGRN_EXP_CLAMP))
        qstk = _stack_heads(qh[s * HGRN_SUB:(s + 1) * HGRN_SUB])
        att = _mm_nt(qstk, kh)
        ar = _rows_of(att.shape) % HGRN_SUB + s * HGRN_SUB
        att = jnp.where(_lanes_of(att.shape) <= ar, att, 0.0)
        outs.append(_unstack_heads(_mm(att, i)))
    o = jnp.concatenate(outs, axis=0) + _mm_nt(qt, st)
    same = (_rows_of(st.shape) // HEAD_DIM) == (_lanes_of(st.shape) // HEAD_DIM)
    st_new = st * jnp.exp(b_end) + jnp.where(same, _mm_tn(i, kt), 0.0)
    on = o * lax.rsqrt(_segmean(o * o) + EPS) * ngf
    return on * jax.nn.silu(g), st_new


def _ffn_tile(eg, ev, wg, bg, wv, bv):
    gt = (bg + wg[0:1] * _shift_dn(eg, 2) + wg[1:2] * _shift_dn(eg, 1) + wg[2:3] * eg)[8:]
    val = (bv + wv[0:1] * _shift_dn(ev, 2) + wv[1:2] * _shift_dn(ev, 1) + wv[2:3] * ev)[8:]
    return jax.nn.gelu(gt) * val


MXU_WIDTH = 256
MATMUL_BLOCK_BUDGET = 18 * MIB


def _matmul_tiles(m, k, n, a_dtype, b_dtype, out_dtype, has_res):
    best = None
    for tm in (2048, 1024, 512, 256):
        if m % tm:
            continue
        for tn in (1024, 768, 1408, 512, 256, 128):
            if n % tn:
                continue
            blk = (_nbytes((tm, k), a_dtype) + _nbytes((k, tn), b_dtype) + _nbytes((tm, tn), out_dtype)
                   + (_nbytes((tm, tn), F32) if has_res else 0))
            if blk > MATMUL_BLOCK_BUDGET:
                continue
            waste = -(-tn // MXU_WIDTH) * MXU_WIDTH / tn
            cost = (m // tm) * (n // tn) + 64 * (waste - 1.0)
            if best is None or cost < best[0]:
                best = (cost, tm, tn, blk)
    assert best is not None, (m, k, n)
    return best[1:]


def _matmul(a, b, *, name, nt=False, res=None, out_dtype=F32):
    m, k = a.shape
    n = b.shape[0] if nt else b.shape[1]
    tm, tn, blk = _matmul_tiles(m, k, n, a.dtype, b.dtype, out_dtype, res is not None)
    dims = (((1,), (1,)), ((), ())) if nt else (((1,), (0,)), ((), ()))

    def body(*refs):
        if res is None:
            a_ref, b_ref, o_ref = refs
        else:
            a_ref, b_ref, r_ref, o_ref = refs
        acc = lax.dot_general(a_ref[...], b_ref[...], dims, preferred_element_type=F32)
        if res is not None:
            acc = acc + r_ref[...]
        o_ref[...] = acc.astype(out_dtype)

    in_specs = [pl.BlockSpec((tm, k), lambda i, j: (i, 0)),
                _spec(b, (tn, k), lambda i, j: (j, 0)) if nt else _spec(b, (k, tn), lambda i, j: (0, j))]
    args = [a, _arr(b)]
    if res is not None:
        in_specs.append(pl.BlockSpec((tm, tn), lambda i, j: (i, j)))
        args.append(res)
    return _pcall(body, name=name, out_shape=_sds((m, n), out_dtype), grid=(m // tm, n // tn), in_specs=in_specs,
                  out_specs=pl.BlockSpec((tm, tn), lambda i, j: (i, j)), semantics=("parallel", "parallel"),
                  block_bytes=blk + _nbytes((tm, tn), F32))(*args)


def _matmul_tn(a, b, *, name, out_dtype=BF16):
    m, k1 = a.shape
    n = b.shape[1]
    tk = _pick(k1, (512, 256, 128))

    def body(a_ref, b_ref, o_ref):
        o_ref[...] = lax.dot_general(a_ref[...], b_ref[...], (((0,), (0,)), ((), ())),
                                     preferred_element_type=F32).astype(out_dtype)

    blk = 2 * _nbytes((m, tk), a.dtype) + _nbytes((m, n), b.dtype) + _nbytes((tk, n), F32)
    return _pcall(body, name=name, out_shape=_sds((k1, n), out_dtype), grid=(k1 // tk,),
                  in_specs=[pl.BlockSpec((m, tk), lambda i: (0, i)), pl.BlockSpec((m, n), lambda i: (0, 0))],
                  out_specs=pl.BlockSpec((tk, n), lambda i: (i, 0)), semantics=("parallel",),
                  block_bytes=blk)(a, b)


def _rms_fwd(x, g, *, name):
    s, d = x.shape
    tm = _pick(s, (512, 256))

    def body(x_ref, g_ref, o_ref):
        o_ref[...] = _rms(x_ref[...], g_ref[...]).astype(BF16)

    return _pcall(body, name=name, out_shape=_sds((s, d), BF16), grid=(s // tm,),
                  in_specs=[pl.BlockSpec((tm, d), lambda i: (i, 0)), _spec(g)],
                  out_specs=pl.BlockSpec((tm, d), lambda i: (i, 0)), semantics=("parallel",),
                  block_bytes=3 * _nbytes((tm, d), F32))(x, _arr(g))


def _rms_bwd(x, g, dh, dres, *, name):
    s, d = x.shape
    tm = _pick(s, (256, 128))

    def body(x_ref, g_ref, dh_ref, dr_ref, dx_ref, dxb_ref, dg_ref):
        _, vjp = jax.vjp(_rms, x_ref[...], g_ref[...])
        dxn, dg = vjp(dh_ref[...])
        dx = dr_ref[...] + dxn
        dx_ref[...] = dx
        dxb_ref[...] = dx.astype(BF16)

        @pl.when(pl.program_id(0) == 0)
        def _():
            dg_ref[...] = jnp.zeros_like(dg_ref)

        dg_ref[...] += dg

    row = pl.BlockSpec((tm, d), lambda i: (i, 0))
    vec = pl.BlockSpec((1, d), lambda i: (0, 0))
    return _pcall(body, name=name, out_shape=(_sds((s, d), F32), _sds((s, d), BF16), _sds((1, d), F32)),
                  grid=(s // tm,), in_specs=[row, _spec(g), row, row], out_specs=(row, row, vec),
                  semantics=("arbitrary",), block_bytes=8 * _nbytes((tm, d), F32))(x, _arr(g), dh, dres)


def _ple_fwd(x, gl, pe, *, name):
    s, d = x.shape
    tm = _pick(s, (512, 256))

    def body(x_ref, gl_ref, pe_ref, o_ref):
        o_ref[...] = x_ref[...] + pe_ref[...] * jax.nn.sigmoid(gl_ref[...])

    row = pl.BlockSpec((tm, d), lambda i: (i, 0))
    return _pcall(body, name=name, out_shape=_sds((s, d), F32), grid=(s // tm,), in_specs=[row, row, row],
                  out_specs=row, semantics=("parallel",), block_bytes=4 * _nbytes((tm, d), F32))(x, gl, pe)


def _ple_bwd(dx, gl, pe, *, name):
    s, d = dx.shape
    tm = _pick(s, (512, 256))

    def body(dx_ref, gl_ref, pe_ref, dpe_ref, dgl_ref):
        gate = jax.nn.sigmoid(gl_ref[...])
        dxv = dx_ref[...]
        dpe_ref[...] = (dxv * gate).astype(BF16)
        dgl_ref[...] = (dxv * pe_ref[...] * gate * (1.0 - gate)).astype(BF16)

    row = pl.BlockSpec((tm, d), lambda i: (i, 0))
    return _pcall(body, name=name, out_shape=(_sds((s, d), BF16), _sds((s, d), BF16)), grid=(s // tm,),
                  in_specs=[row, row, row], out_specs=(row, row), semantics=("parallel",),
                  block_bytes=5 * _nbytes((tm, d), F32))(dx, gl, pe)


def _loss_head(x, g, target, *, name):
    s, d = x.shape
    tm = _pick(s, (256, 128))

    def tile_loss(xv, gv, tv):
        err = jnp.square(_rms(xv, gv) - tv)
        return 0.5 * jnp.sum(jnp.mean(err, axis=-1, keepdims=True), axis=0, keepdims=True)

    def body(x_ref, g_ref, t_ref, l_ref, dx_ref, dg_ref):
        lv, vjp = jax.vjp(tile_loss, x_ref[...], g_ref[...], t_ref[...])
        dxv, dgv, _ = vjp(jnp.ones((1, 1), F32))
        dx_ref[...] = dxv

        @pl.when(pl.program_id(0) == 0)
        def _():
            l_ref[...] = jnp.zeros_like(l_ref)
            dg_ref[...] = jnp.zeros_like(dg_ref)

        l_ref[...] += jnp.broadcast_to(lv, l_ref.shape)
        dg_ref[...] += dgv

    row = pl.BlockSpec((tm, d), lambda i: (i, 0))
    vec = pl.BlockSpec((1, d), lambda i: (0, 0))
    return _pcall(body, name=name, out_shape=(_sds((8, 128), F32), _sds((s, d), F32), _sds((1, d), F32)),
                  grid=(s // tm,), in_specs=[row, vec, row],
                  out_specs=(pl.BlockSpec((8, 128), lambda i: (0, 0)), row, vec), semantics=("arbitrary",),
                  block_bytes=8 * _nbytes((tm, d), F32))(x, g, target)


def _acc_out(ref, val, first):
    @pl.when(first)
    def _():
        ref[...] = jnp.zeros_like(ref)

    ref[...] += val


def _gmlp_fwd(z, ln_g, ln_b, wcat, bfull, *, name):
    s = z.shape[0]
    t = _pick(s, (512, 256, 128))
    nch = t // GMLP_CHUNK

    def body(zu_ref, zv_ref, g_ref, b_ref, w_ref, bf_ref, o_ref):
        for c in range(nch):
            rows = pl.ds(c * GMLP_CHUNK, GMLP_CHUNK)
            o_ref[rows, :] = _gmlp_chunk(zu_ref[rows, :], zv_ref[rows, :], g_ref[...], b_ref[...], w_ref[...],
                                         bf_ref[...]).astype(BF16)

    col = lambda c: pl.BlockSpec((t, W_GRP), lambda i: (i, c))
    params = (ln_g, ln_b, wcat, bfull)
    return _pcall(body, name=name, out_shape=_sds((s, W_GRP), BF16), grid=(s // t,),
                  in_specs=[col(0), col(1)] + [_spec(a) for a in params],
                  out_specs=pl.BlockSpec((t, W_GRP), lambda i: (i, 0)), semantics=("parallel",),
                  block_bytes=4 * _nbytes((t, W_GRP), F32))(z, z, *[_arr(a) for a in params])


def _gmlp_bwd(z, dmix, ln_g, ln_b, wcat, bfull, *, name):
    s = z.shape[0]
    t = _pick(s, (512, 256, 128))
    nch = t // GMLP_CHUNK

    def body(zu_ref, zv_ref, dy_ref, g_ref, b_ref, w_ref, bf_ref, du_ref, dv_ref, dg_ref, db_ref, dw_ref, dbf_ref):
        acc = None
        for c in range(nch):
            rows = pl.ds(c * GMLP_CHUNK, GMLP_CHUNK)
            _, vjp = jax.vjp(_gmlp_chunk, zu_ref[rows, :], zv_ref[rows, :], g_ref[...], b_ref[...], w_ref[...],
                             bf_ref[...])
            du, dv, *dps = vjp(dy_ref[rows, :])
            du_ref[rows, :] = du.astype(BF16)
            dv_ref[rows, :] = dv.astype(BF16)
            acc = dps if acc is None else [x + y for x, y in zip(acc, dps)]
        first = pl.program_id(0) == 0
        for ref, val in zip((dg_ref, db_ref, dw_ref, dbf_ref), acc):
            _acc_out(ref, val, first)

    col = lambda c: pl.BlockSpec((t, W_GRP), lambda i: (i, c))
    params = (ln_g, ln_b, wcat, bfull)
    return _pcall(body, name=name,
                  out_shape=(_sds((s, W_GRP), BF16), _sds((s, W_GRP), BF16)) + tuple(_sds(a.shape, F32) for a in params),
                  grid=(s // t,), in_specs=[col(0), col(1), col(0)] + [_spec(a) for a in params],
                  out_specs=(col(0), col(0)) + tuple(_ospec(a) for a in params), semantics=("arbitrary",),
                  block_bytes=8 * _nbytes((t, W_GRP), F32))(z, z, dmix, *[_arr(a) for a in params])


def _rglru_fwd(z, prm, *, name):
    s = z.shape[0]
    t = _pick(s, (512, 256, 128))
    nt = s // t

    def body(xb_ref, halo_ref, gb_ref, *rest):
        prm_refs, (y_ref, h0s_ref, h_scr) = rest[:len(prm)], rest[len(prm):]
        i = pl.program_id(0)

        @pl.when(i == 0)
        def _():
            h_scr[...] = jnp.zeros_like(h_scr)

        halo = jnp.where(i == 0, 0.0, halo_ref[...])
        h0 = h_scr[...]
        y, h_last = _rglru_tile(jnp.concatenate([halo, xb_ref[...]], axis=0), gb_ref[...], h0,
                                *[r[...] for r in prm_refs])
        y_ref[...] = y.astype(BF16)
        h0s_ref[...] = jnp.broadcast_to(h0, h0s_ref.shape)
        h_scr[...] = h_last

    in_specs = [pl.BlockSpec((t, W_GRP), lambda i: (i, 2)),
                pl.BlockSpec((8, W_GRP), lambda i: (jnp.maximum(i * (t // 8) - 1, 0), 2)),
                pl.BlockSpec((t, W_GRP), lambda i: (i, 3))] + [_spec(a) for a in prm]
    return _pcall(body, name=name, out_shape=(_sds((s, W_GRP), BF16), _sds((nt, 8, W_GRP), F32)), grid=(nt,),
                  in_specs=in_specs,
                  out_specs=(pl.BlockSpec((t, W_GRP), lambda i: (i, 0)), pl.BlockSpec((None, 8, W_GRP), lambda i: (i, 0, 0))),
                  scratch_shapes=[pltpu.VMEM((1, W_GRP), F32)], semantics=("arbitrary",),
                  block_bytes=24 * _nbytes((t, W_GRP), F32))(z, z, z, *[_arr(a) for a in prm])


def _rglru_bwd(z, dmix, h0s, prm, *, name):
    s = z.shape[0]
    t = _pick(s, (512, 256, 128))
    nt = s // t
    npm = len(prm)

    def body(xb_ref, halo_ref, gb_ref, dy_ref, h0s_ref, *rest):
        prm_refs = rest[:npm]
        dxb_ref, dgb_ref = rest[npm:npm + 2]
        dprm_refs = rest[npm + 2:2 * npm + 2]
        dh_scr, dhalo_scr = rest[2 * npm + 2:]
        i = pl.program_id(0)
        r = nt - 1 - i

        @pl.when(i == 0)
        def _():
            dh_scr[...] = jnp.zeros_like(dh_scr)
            dhalo_scr[...] = jnp.zeros_like(dhalo_scr)

        halo = jnp.where(r == 0, 0.0, halo_ref[...])
        h0 = h0s_ref[0:1, :]
        _, vjp = jax.vjp(_rglru_tile, jnp.concatenate([halo, xb_ref[...]], axis=0), gb_ref[...], h0,
                         *[p[...] for p in prm_refs])
        dext, dgb, _dh0, *dps = vjp((dy_ref[...], dh_scr[...]))
        dmain = dext[8:]
        dxb = jnp.concatenate([dmain[:t - 8], dmain[t - 8:] + dhalo_scr[...]], axis=0)
        dxb_ref[...] = dxb.astype(BF16)
        dgb_ref[...] = dgb.astype(BF16)
        dh_scr[...] = _dh0
        dhalo_scr[...] = dext[:8]
        for ref, val in zip(dprm_refs, dps):
            _acc_out(ref, val, i == 0)

    rev = lambda c: pl.BlockSpec((t, W_GRP), lambda i: (nt - 1 - i, c))
    in_specs = [rev(2), pl.BlockSpec((8, W_GRP), lambda i: (jnp.maximum((nt - 1 - i) * (t // 8) - 1, 0), 2)), rev(3),
                rev(1), pl.BlockSpec((None, 8, W_GRP), lambda i: (nt - 1 - i, 0, 0))] + [_spec(a) for a in prm]
    return _pcall(body, name=name,
                  out_shape=(_sds((s, W_GRP), BF16), _sds((s, W_GRP), BF16)) + tuple(_sds(a.shape, F32) for a in prm),
                  grid=(nt,), in_specs=in_specs, out_specs=(rev(0), rev(0)) + tuple(_ospec(a) for a in prm),
                  scratch_shapes=[pltpu.VMEM((1, W_GRP), F32), pltpu.VMEM((8, W_GRP), F32)],
                  semantics=("arbitrary",), block_bytes=40 * _nbytes((t, W_GRP), F32))(z, z, z, dmix, h0s, *[_arr(a) for a in prm])


def _pool_inv(i, t):
    pos = (_rows_of((t, W_GRP)) + i * t + 1).astype(F32)
    grp = _lanes_of((t, W_GRP)) // HEAD_DIM
    win = jnp.where(grp == 0, float(POOL_WINDOWS[0]), jnp.where(grp == 1, float(POOL_WINDOWS[1]),
                    jnp.where(grp == 2, float(POOL_WINDOWS[2]), float(POOL_WINDOWS[3]))))
    return 1.0 / jnp.minimum(pos, win)


def _pool_fwd(z, wd, scale, *, name):
    s = z.shape[0]
    t = _pick(s, (512, 256, 128))

    def body(x_ref, halo_ref, wd_ref, sc_ref, y_ref):
        i = pl.program_id(0)
        halo = jnp.where(i == 0, 0.0, halo_ref[...])
        y = _pool_tile(jnp.concatenate([halo, x_ref[...]], axis=0), _pool_inv(i, t), wd_ref[...], sc_ref[...])
        y_ref[...] = y.astype(BF16)

    in_specs = [pl.BlockSpec((t, W_GRP), lambda i: (i, 8)),
                pl.BlockSpec((16, W_GRP), lambda i: (jnp.maximum(i * (t // 16) - 1, 0), 8)), _spec(wd), _spec(scale)]
    return _pcall(body, name=name, out_shape=_sds((s, W_GRP), BF16), grid=(s // t,), in_specs=in_specs,
                  out_specs=pl.BlockSpec((t, W_GRP), lambda i: (i, 0)), semantics=("parallel",),
                  block_bytes=12 * _nbytes((t, W_GRP), F32))(z, z, _arr(wd), _arr(scale))


def _pool_bwd(z, dmix, wd, scale, *, name):
    s = z.shape[0]
    t = _pick(s, (512, 256, 128))
    nt = s // t

    def body(x_ref, halo_ref, dy_ref, wd_ref, sc_ref, dx_ref, dwd_ref, dsc_ref, dhalo_scr):
        i = pl.program_id(0)
        r = nt - 1 - i

        @pl.when(i == 0)
        def _():
            dhalo_scr[...] = jnp.zeros_like(dhalo_scr)

        halo = jnp.where(r == 0, 0.0, halo_ref[...])
        inv = _pool_inv(r, t)
        _, vjp = jax.vjp(lambda e, w, sc: _pool_tile(e, inv, w, sc), jnp.concatenate([halo, x_ref[...]], axis=0),
                         wd_ref[...], sc_ref[...])
        dext, dwd, dsc = vjp(dy_ref[...])
        dmain = dext[16:]
        dx = jnp.concatenate([dmain[:t - 16], dmain[t - 16:] + dhalo_scr[...]], axis=0)
        dx_ref[...] = dx.astype(BF16)
        dhalo_scr[...] = dext[:16]
        _acc_out(dwd_ref, dwd, i == 0)
        _acc_out(dsc_ref, dsc, i == 0)

    rev = lambda c: pl.BlockSpec((t, W_GRP), lambda i: (nt - 1 - i, c))
    in_specs = [rev(8), pl.BlockSpec((16, W_GRP), lambda i: (jnp.maximum((nt - 1 - i) * (t // 16) - 1, 0), 8)), rev(3),
                _spec(wd), _spec(scale)]
    return _pcall(body, name=name, out_shape=(_sds((s, W_GRP), BF16), _sds(wd.shape, F32), _sds(scale.shape, F32)),
                  grid=(nt,), in_specs=in_specs, out_specs=(rev(0), _ospec(wd), _ospec(scale)),
                  scratch_shapes=[pltpu.VMEM((16, W_GRP), F32)], semantics=("arbitrary",),
                  block_bytes=20 * _nbytes((t, W_GRP), F32))(z, z, dmix, _arr(wd), _arr(scale))


def _hgrn_fwd(z, lb, ngf, *, name):
    s = z.shape[0]
    c = HGRN_CHUNK
    nc = s // c

    def body(q_ref, f_ref, i_ref, g_ref, lb_ref, ng_ref, y_ref, sts_ref, st_scr):
        @pl.when(pl.program_id(0) == 0)
        def _():
            st_scr[...] = jnp.zeros_like(st_scr)

        st = st_scr[...]
        sts_ref[...] = st
        y, st_new = _hgrn_chunk(q_ref[...], f_ref[...], i_ref[...], g_ref[...], st, lb_ref[...], ng_ref[...])
        y_ref[...] = y.astype(BF16)
        st_scr[...] = st_new

    col = lambda k: pl.BlockSpec((c, W_GRP), lambda i: (i, k))
    vec = pl.BlockSpec((1, W_GRP), lambda i: (0, 0))
    return _pcall(body, name=name, out_shape=(_sds((s, W_GRP), BF16), _sds((nc, W_GRP, W_GRP), F32)), grid=(nc,),
                  in_specs=[col(4), col(5), col(6), col(7), _spec(lb), _spec(ngf)],
                  out_specs=(pl.BlockSpec((c, W_GRP), lambda i: (i, 0)), pl.BlockSpec((None, W_GRP, W_GRP), lambda i: (i, 0, 0))),
                  scratch_shapes=[pltpu.VMEM((W_GRP, W_GRP), F32)], semantics=("arbitrary",),
                  block_bytes=16 * _nbytes((W_GRP, W_GRP), F32))(z, z, z, z, _arr(lb), _arr(ngf))


def _hgrn_bwd(z, dmix, sts, lb, ngf, *, name):
    s = z.shape[0]
    c = HGRN_CHUNK
    nc = s // c

    def body(q_ref, f_ref, i_ref, g_ref, dy_ref, st_ref, lb_ref, ng_ref, dz_ref, dlb_ref, dng_ref, dst_scr):
        i = pl.program_id(0)

        @pl.when(i == 0)
        def _():
            dst_scr[...] = jnp.zeros_like(dst_scr)

        _, vjp = jax.vjp(_hgrn_chunk, q_ref[...], f_ref[...], i_ref[...], g_ref[...], st_ref[...], lb_ref[...],
                         ng_ref[...])
        dq, df, di, dg, dst, dlb, dng = vjp((dy_ref[...], dst_scr[...]))
        dz_ref[...] = jnp.concatenate([dq, df, di, dg], axis=1).astype(BF16)
        dst_scr[...] = dst
        _acc_out(dlb_ref, dlb, i == 0)
        _acc_out(dng_ref, dng, i == 0)

    rev = lambda k: pl.BlockSpec((c, W_GRP), lambda i: (nc - 1 - i, k))
    vec = pl.BlockSpec((1, W_GRP), lambda i: (0, 0))
    return _pcall(body, name=name, out_shape=(_sds((s, 4 * W_GRP), BF16), _sds((1, W_GRP), F32), _sds((1, W_GRP), F32)),
                  grid=(nc,),
                  in_specs=[rev(4), rev(5), rev(6), rev(7), rev(2),
                            pl.BlockSpec((None, W_GRP, W_GRP), lambda i: (nc - 1 - i, 0, 0)), _spec(lb), _spec(ngf)],
                  out_specs=(pl.BlockSpec((c, 4 * W_GRP), lambda i: (nc - 1 - i, 0)), vec, vec),
                  scratch_shapes=[pltpu.VMEM((W_GRP, W_GRP), F32)], semantics=("arbitrary",),
                  block_bytes=32 * _nbytes((W_GRP, W_GRP), F32))(z, z, z, z, dmix, sts, _arr(lb), _arr(ngf))


def _lbs_fwd(c_lb, *, name):
    def body(c_ref, o_ref):
        c = c_ref[...]
        e = jnp.exp(c - jnp.max(c, axis=0, keepdims=True))
        sm = e / jnp.sum(e, axis=0, keepdims=True)
        run = jnp.zeros((1, W_GRP), F32)
        o_ref[0:1, :] = run
        for l in range(1, DEPTH):
            run = run + sm[l:l + 1]
            o_ref[l:l + 1, :] = run

    return _pcall(body, name=name, out_shape=_sds((DEPTH, W_GRP), F32))(c_lb)


def _lbs_bwd(c_lb, dlbs, *, name):
    def body(c_ref, d_ref, o_ref):
        c = c_ref[...]
        e = jnp.exp(c - jnp.max(c, axis=0, keepdims=True))
        sm = e / jnp.sum(e, axis=0, keepdims=True)
        d = d_ref[...]
        dsm = [None] * DEPTH
        run = jnp.zeros((1, W_GRP), F32)
        for l in range(DEPTH - 1, 0, -1):
            run = run + d[l:l + 1]
            dsm[l] = run
        dsm[0] = jnp.zeros((1, W_GRP), F32)
        inner = sum(sm[l:l + 1] * dsm[l] for l in range(DEPTH))
        for l in range(DEPTH):
            o_ref[l:l + 1, :] = sm[l:l + 1] * (dsm[l] - inner)

    return _pcall(body, name=name, out_shape=_sds((DEPTH, W_GRP), F32))(c_lb, dlbs)


def _ffn_fwd(hg, hv, cwf, cbf, *, name):
    s, n = hg.shape
    t = _pick(s, (256, 128))
    cw = _pick(n, (1408, 256, 128))
    nj = n // cw

    def body(g_ref, gh_ref, v_ref, vh_ref, wg_ref, bg_ref, wv_ref, bv_ref, o_ref):
        first = pl.program_id(1) == 0
        eg = jnp.concatenate([jnp.where(first, 0.0, gh_ref[...]), g_ref[...]], axis=0)
        ev = jnp.concatenate([jnp.where(first, 0.0, vh_ref[...]), v_ref[...]], axis=0)
        o_ref[...] = _ffn_tile(eg, ev, wg_ref[...], bg_ref[...], wv_ref[...], bv_ref[...]).astype(BF16)

    main = pl.BlockSpec((t, cw), lambda j, i: (i, j))
    halo = pl.BlockSpec((8, cw), lambda j, i: (jnp.maximum(i * (t // 8) - 1, 0), j))
    taps = lambda off: _spec(cwf, (3, cw), lambda j, i: (0, j + off))
    bias = lambda off: _spec(cbf, (1, cw), lambda j, i: (0, j + off))
    return _pcall(body, name=name, out_shape=_sds((s, n), BF16), grid=(nj, s // t),
                  in_specs=[main, halo, main, halo, taps(0), bias(0), taps(nj), bias(nj)], out_specs=main,
                  semantics=("parallel", "parallel"), block_bytes=12 * _nbytes((t, cw), F32))(
                      hg, hg, hv, hv, _arr(cwf), _arr(cbf), _arr(cwf), _arr(cbf))


def _ffn_bwd(hg, hv, da, cwf, cbf, *, name):
    s, n = hg.shape
    t = _pick(s, (256, 128))
    cw = _pick(n, (1408, 256, 128))
    nt = s // t
    nj = n // cw

    def body(g_ref, gh_ref, v_ref, vh_ref, da_ref, wg_ref, bg_ref, wv_ref, bv_ref, dg_ref, dv_ref, dwg_ref, dwv_ref,
             cg_scr, cv_scr):
        i = pl.program_id(1)
        r = nt - 1 - i

        @pl.when(i == 0)
        def _():
            cg_scr[...] = jnp.zeros_like(cg_scr)
            cv_scr[...] = jnp.zeros_like(cv_scr)

        eg = jnp.concatenate([jnp.where(r == 0, 0.0, gh_ref[...]), g_ref[...]], axis=0)
        ev = jnp.concatenate([jnp.where(r == 0, 0.0, vh_ref[...]), v_ref[...]], axis=0)
        _, vjp = jax.vjp(_ffn_tile, eg, ev, wg_ref[...], bg_ref[...], wv_ref[...], bv_ref[...])
        deg, dev, dwg, dbg, dwv, dbv = vjp(da_ref[...])
        for dext, scr, ref in ((deg, cg_scr, dg_ref), (dev, cv_scr, dv_ref)):
            dmain = dext[8:]
            ref[...] = jnp.concatenate([dmain[:t - 8], dmain[t - 8:] + scr[...]], axis=0).astype(BF16)
            scr[...] = dext[:8]
        zeros = jnp.zeros((4, cw), F32)
        _acc_out(dwg_ref, jnp.concatenate([dwg, dbg, zeros], axis=0), i == 0)
        _acc_out(dwv_ref, jnp.concatenate([dwv, dbv, zeros], axis=0), i == 0)

    main = pl.BlockSpec((t, cw), lambda j, i: (nt - 1 - i, j))
    halo = pl.BlockSpec((8, cw), lambda j, i: (jnp.maximum((nt - 1 - i) * (t // 8) - 1, 0), j))
    taps = lambda off: _spec(cwf, (3, cw), lambda j, i: (0, j + off))
    bias = lambda off: _spec(cbf, (1, cw), lambda j, i: (0, j + off))
    w8 = pl.BlockSpec((8, cw), lambda j, i: (0, j))
    return _pcall(body, name=name,
                  out_shape=(_sds((s, n), BF16), _sds((s, n), BF16), _sds((8, n), F32), _sds((8, n), F32)),
                  grid=(nj, nt), in_specs=[main, halo, main, halo, main, taps(0), bias(0), taps(nj), bias(nj)],
                  out_specs=(main, main, w8, w8),
                  scratch_shapes=[pltpu.VMEM((8, cw), F32), pltpu.VMEM((8, cw), F32)],
                  semantics=("parallel", "arbitrary"), block_bytes=24 * _nbytes((t, cw), F32))(
                      hg, hg, hv, hv, da, _arr(cwf), _arr(cbf), _arr(cwf), _arr(cbf))


def _all_gather(x, *, name):
    r, c = x.shape

    def body(x_ref, out_ref, send_sems, recv_sems, local_sem):
        mx, my, mc = lax.axis_index("x"), lax.axis_index("y"), lax.axis_index("c")
        me, sibling = (mx, my, mc), (mx, my, 1 - mc)
        chips = [(1 - mx, my), (mx, 1 - my), (1 - mx, 1 - my)]

        def slot(px, py, pc):
            return out_ref.at[4 * px + 2 * py + pc]

        def copy(k, block, to, src=None):
            return pltpu.make_async_remote_copy(src_ref=slot(*block) if src is None else src, dst_ref=slot(*block),
                                                send_sem=send_sems.at[k], recv_sem=recv_sems.at[k],
                                                device_id=to, device_id_type=MESH)

        mine = pltpu.make_async_copy(x_ref, slot(*me), local_sem)
        mine.start()
        first = [copy(0, me, sibling, src=x_ref)]
        first += [copy(1 + j, me, (*chip, mc), src=x_ref) for j, chip in enumerate(chips)]
        for cp in first:
            cp.start()
        passed = [copy(4 + j, (*chip, mc), sibling) for j, chip in enumerate(chips)]
        for j, chip in enumerate(chips):
            copy(1 + j, (*chip, mc), me).wait_recv()
            passed[j].start()
        copy(0, sibling, me).wait_recv()
        for j, chip in enumerate(chips):
            copy(4 + j, (*chip, 1 - mc), me).wait_recv()
        for cp in first + passed:
            cp.wait_send()
        mine.wait()

    hbm = pl.BlockSpec(memory_space=pl.ANY)
    return _pcall(body, name=name, out_shape=_sds((N_DEV, r, c), x.dtype), in_specs=[hbm], out_specs=hbm,
                  scratch_shapes=[pltpu.SemaphoreType.DMA((7,)), pltpu.SemaphoreType.DMA((7,)),
                                  pltpu.SemaphoreType.DMA(())])(x)


def _swap_sibling(x, *, name):
    def body(x_ref, out_ref, send_sem, recv_sem):
        sibling = (lax.axis_index("x"), lax.axis_index("y"), 1 - lax.axis_index("c"))
        cp = pltpu.make_async_remote_copy(src_ref=x_ref, dst_ref=out_ref, send_sem=send_sem, recv_sem=recv_sem,
                                          device_id=sibling, device_id_type=MESH)
        cp.start()
        cp.wait()

    hbm = pl.BlockSpec(memory_space=pl.ANY)
    return _pcall(body, name=name, out_shape=_sds(x.shape, x.dtype), in_specs=[hbm], out_specs=hbm,
                  scratch_shapes=[pltpu.SemaphoreType.DMA(()), pltpu.SemaphoreType.DMA(())])(x)


def _exchange_chips(p, *, name):
    def body(p_ref, out_ref, send_sems, recv_sems, local_sem):
        mx, my, mc = lax.axis_index("x"), lax.axis_index("y"), lax.axis_index("c")
        mine_q = 2 * mx + my
        chips = [(1 - mx, my), (mx, 1 - my), (1 - mx, 1 - my)]

        def copy(k, chip):
            return pltpu.make_async_remote_copy(src_ref=p_ref.at[2 * chip[0] + chip[1]], dst_ref=out_ref.at[mine_q],
                                                send_sem=send_sems.at[k], recv_sem=recv_sems.at[k],
                                                device_id=(*chip, mc), device_id_type=MESH)

        def arrival(k, chip):
            return pltpu.make_async_remote_copy(src_ref=p_ref.at[mine_q], dst_ref=out_ref.at[2 * chip[0] + chip[1]],
                                                send_sem=send_sems.at[k], recv_sem=recv_sems.at[k],
                                                device_id=(*chip, mc), device_id_type=MESH)

        own = pltpu.make_async_copy(p_ref.at[mine_q], out_ref.at[mine_q], local_sem)
        own.start()
        sends = [copy(k, chip) for k, chip in enumerate(chips)]
        for cp in sends:
            cp.start()
        for k, chip in enumerate(chips):
            arrival(k, chip).wait_recv()
        for cp in sends:
            cp.wait_send()
        own.wait()

    hbm = pl.BlockSpec(memory_space=pl.ANY)
    return _pcall(body, name=name, out_shape=_sds(p.shape, p.dtype), in_specs=[hbm], out_specs=hbm,
                  scratch_shapes=[pltpu.SemaphoreType.DMA((3,)), pltpu.SemaphoreType.DMA((3,)),
                                  pltpu.SemaphoreType.DMA(())])(p)


def _add_pairs(a, b, *, name):
    q, r, c = a.shape
    tr = _pick(r, (544, 408, 272, 136, 64, 32, 16, 8))

    def body(a_ref, b_ref, o_ref):
        o_ref[...] = (a_ref[...].astype(F32) + b_ref[...].astype(F32)).astype(o_ref.dtype)

    blk = pl.BlockSpec((None, tr, c), lambda i, j: (i, j, 0))
    return _pcall(body, name=name, out_shape=_sds(a.shape, a.dtype), grid=(q, r // tr), in_specs=[blk, blk],
                  out_specs=blk, semantics=("parallel", "parallel"), block_bytes=4 * _nbytes((tr, c), F32))(a, b)


def _sum_slots(p, *, name):
    q, r, c = p.shape
    tr = _pick(r, (544, 408, 272, 192, 136, 64, 32, 16, 8))

    def body(p_ref, o_ref):
        acc = p_ref[0].astype(F32)
        for k in range(1, q):
            acc = acc + p_ref[k].astype(F32)
        o_ref[...] = acc

    return _pcall(body, name=name, out_shape=_sds((r, c), F32), grid=(r // tr,),
                  in_specs=[pl.BlockSpec((q, tr, c), lambda i: (0, i, 0))],
                  out_specs=pl.BlockSpec((tr, c), lambda i: (i, 0)), semantics=("parallel",),
                  block_bytes=(q + 2) * _nbytes((tr, c), F32))(p)


def _adamw(w, g, m, v, *, name):
    r, c = w.shape
    tr = _pick(r, (512, 256, 192, 128, 64, 32, 16, 8))
    c1 = 1.0 / (1.0 - ADAM_B1 ** ADAM_STEP)
    c2 = 1.0 / (1.0 - ADAM_B2 ** ADAM_STEP)

    def body(w_ref, g_ref, m_ref, v_ref, d_ref, nm_ref, nv_ref):
        gv = g_ref[...]
        nm = ADAM_B1 * m_ref[...] + (1.0 - ADAM_B1) * gv
        nv = ADAM_B2 * v_ref[...] + (1.0 - ADAM_B2) * jnp.square(gv)
        d_ref[...] = -ADAM_LR * ((nm * c1) / (jnp.sqrt(nv * c2) + ADAM_EPS) + ADAM_WD * w_ref[...])
        nm_ref[...] = nm
        nv_ref[...] = nv

    blk = pl.BlockSpec((tr, c), lambda i: (i, 0))
    out = _sds((r, c), F32)
    return _pcall(body, name=name, out_shape=(out, out, out), grid=(r // tr,), in_specs=[blk] * 4,
                  out_specs=(blk, blk, blk), semantics=("parallel",), block_bytes=7 * _nbytes((tr, c), F32))(w, g, m, v)


def _pack_flat(arrs, rows, cols=1024):
    flat = jnp.concatenate([a.reshape(-1).astype(F32) for a in arrs])
    pad = rows * cols - flat.shape[0]
    return jnp.pad(flat, (0, pad)).reshape(rows, cols)


def _unpack_flat(buf, shapes):
    flat = buf.reshape(-1)
    out, off = [], 0
    for shp in shapes:
        n = 1
        for s in shp:
            n *= s
        out.append(flat[off:off + n].reshape(shp))
        off += n
    return out


def _flat_rows(shapes, cols=1024):
    n = sum(functools.reduce(lambda a, b: a * b, shp, 1) for shp in shapes)
    rows = -(-n // cols)
    return -(-rows // 64) * 64


def _block_diag(w):
    eye = jnp.eye(N_HEADS, dtype=w.dtype)
    return (w[:, :, :, None, :] * eye[None, :, None, :, None]).reshape(w.shape[0], W_GRP, W_GRP)


def _diag_blocks(w):
    w5 = w.reshape(w.shape[0], N_HEADS, HEAD_DIM, N_HEADS, HEAD_DIM)
    return jnp.stack([w5[:, h, :, h, :] for h in range(N_HEADS)], axis=1)


def _pack_big_shards(w):
    rows = []
    for l in range(DEPTH):
        rows += [w['w_in'][l].T, w['w_out'][l], w['w_up'][l].T, w['w_down'][l],
                 w['w_pe'][l].T.reshape(32, 1024), w['w_pg'][l]]
    return jnp.concatenate(rows, axis=0)


def _unpack_big_full(g, l):
    out, off = {}, l * LAYER_ROWS
    for nm, r in SLAB_ROWS:
        blk = g[:, off:off + r, :]
        if nm == 'w_pe':
            out[nm] = blk.reshape(N_DEV, 128, PLE_DIM).reshape(N_DEV * 128, PLE_DIM)
        elif nm == 'w_up':
            out['w_up_g'] = blk[:N_DEV // 2].reshape(D_FF, D_MODEL)
            out['w_up_v'] = blk[N_DEV // 2:].reshape(D_FF, D_MODEL)
        else:
            out[nm] = blk.reshape(N_DEV * r, D_MODEL)
        off += r
    return out


def _pack_big_grads(gl):
    cols = []
    for l in range(DEPTH):
        g = gl[l]
        cols += [g['w_in'].reshape(N_DEV, 288, D_MODEL), g['w_out'].reshape(N_DEV, 128, D_MODEL),
                 jnp.concatenate([g['w_up_g'], g['w_up_v']], axis=0).reshape(N_DEV, 704, D_MODEL),
                 g['w_down'].reshape(N_DEV, 352, D_MODEL),
                 g['w_pe'].reshape(N_DEV, 128, PLE_DIM).reshape(N_DEV, 32, D_MODEL), g['w_pg'].reshape(N_DEV, 128, D_MODEL)]
    return jnp.concatenate(cols, axis=1)


def _unpack_big_shard(gs):
    out = {nm: [] for nm in BIG_NAMES}
    for l in range(DEPTH):
        off = l * LAYER_ROWS
        for nm, r in SLAB_ROWS:
            blk = gs[off:off + r]
            if nm in ('w_in', 'w_up'):
                blk = blk.T
            elif nm == 'w_pe':
                blk = blk.reshape(128, PLE_DIM).T
            out[nm].append(blk)
            off += r
    return {nm: jnp.stack(v) for nm, v in out.items()}


def _stacked_params(w, lbs):
    tril = jnp.tril(jnp.ones((GMLP_CHUNK, GMLP_CHUNK), bool))
    row = lambda a: a.reshape(DEPTH, 1, -1)
    return dict(
        g1=row(w['norm1_g']), g2=row(w['norm2_g']), g3=row(w['norm3_g']),
        a_ln_g=row(w['a_ln_g']), a_ln_b=row(w['a_ln_b']),
        a_wcat=jnp.where(tril, w['a_ws'], 0.0).reshape(DEPTH, N_HEADS * GMLP_CHUNK, GMLP_CHUNK),
        a_bfull=jnp.repeat(jnp.swapaxes(w['a_bs'], 1, 2), HEAD_DIM, axis=2),
        b_cw=w['b_conv_w_full'], b_cb=row(w['b_conv_b']), b_wa=_block_diag(w['b_wa']), b_ba=row(w['b_ba']),
        b_wx=_block_diag(w['b_wx']), b_bx=row(w['b_bx']), b_lam=row(w['b_lam']),
        c_lb=row(lbs), c_ngf=row(jnp.tile(w['c_norm_g'], (1, N_HEADS))),
        d_wd=_block_diag(w['d_w']), d_scale=row(w['d_scale']),
        f_cw=w['ffn_conv_w_full'], f_cb=row(w['ffn_conv_b']),
    )


B_PRM = ('b_cw', 'b_cb', 'b_wa', 'b_ba', 'b_wx', 'b_bx', 'b_lam')


def _layer_fwd(x, p_bf, wb, sp, l):
    n = lambda s: f"l{l}_{s}"
    h = _rms_fwd(x, sp['g1'], name=n("norm1"))
    z = _matmul(h, wb['w_in'], nt=True, name=n("proj_in"))
    ya = _gmlp_fwd(z, sp['a_ln_g'], sp['a_ln_b'], sp['a_wcat'], sp['a_bfull'], name=n("gmlp"))
    yb, h0s = _rglru_fwd(z, [sp[k] for k in B_PRM], name=n("rglru"))
    yc, sts = _hgrn_fwd(z, sp['c_lb'], sp['c_ngf'], name=n("hgrn"))
    yd = _pool_fwd(z, sp['d_wd'], sp['d_scale'], name=n("pool"))
    mix = jnp.concatenate([ya, yb, yc, yd], axis=1)
    x1 = _matmul(mix, wb['w_out'], res=x, name=n("proj_out"))
    h2 = _rms_fwd(x1, sp['g2'], name=n("norm2"))
    hg = _matmul(h2, wb['w_up_g'], nt=True, name=n("up_gate"))
    hv = _matmul(h2, wb['w_up_v'], nt=True, name=n("up_val"))
    a = _ffn_fwd(hg, hv, sp['f_cw'], sp['f_cb'], name=n("ffn_gate"))
    x2 = _matmul(a, wb['w_down'], res=x1, name=n("down"))
    h3 = _rms_fwd(x2, sp['g3'], name=n("norm3"))
    gl = _matmul(h3, wb['w_pg'], name=n("ple_gate"))
    pe = _matmul(p_bf, wb['w_pe'], nt=True, name=n("ple_emb"))
    x3 = _ple_fwd(x2, gl, pe, name=n("ple"))
    saved = dict(x=x, h=h, z=z, h0s=h0s, sts=sts, mix=mix, x1=x1, h2=h2, hg=hg, hv=hv, a=a, x2=x2, h3=h3, gl=gl, pe=pe)
    return x3, saved


def _layer_bwd(dx3, sv, p_bf, wb, sp, l):
    n = lambda s: f"l{l}_{s}_bwd"
    gb, gs = {}, {}
    dpe, dgl = _ple_bwd(dx3, sv['gl'], sv['pe'], name=n("ple"))
    gb['w_pe'] = _matmul_tn(dpe, p_bf, name=n("ple_emb_w"))
    gb['w_pg'] = _matmul_tn(sv['h3'], dgl, name=n("ple_gate_w"))
    dh3 = _matmul(dgl, wb['w_pg'], nt=True, name=n("ple_gate_x"))
    dx2, dx2b, gs['norm3_g'] = _rms_bwd(sv['x2'], sp['g3'], dh3, dx3, name=n("norm3"))
    da = _matmul(dx2b, wb['w_down'], nt=True, name=n("down_x"))
    gb['w_down'] = _matmul_tn(sv['a'], dx2b, name=n("down_w"))
    dhg, dhv, gs['f_dwg'], gs['f_dwv'] = _ffn_bwd(sv['hg'], sv['hv'], da, sp['f_cw'], sp['f_cb'], name=n("ffn_gate"))
    gb['w_up_g'] = _matmul_tn(dhg, sv['h2'], name=n("up_gate_w"))
    gb['w_up_v'] = _matmul_tn(dhv, sv['h2'], name=n("up_val_w"))
    dh2 = _matmul(dhg, wb['w_up_g'], name=n("up_gate_x"))
    dh2 = _matmul(dhv, wb['w_up_v'], res=dh2, name=n("up_val_x"))
    dx1, dx1b, gs['norm2_g'] = _rms_bwd(sv['x1'], sp['g2'], dh2, dx2, name=n("norm2"))
    dmix = _matmul(dx1b, wb['w_out'], nt=True, name=n("proj_out_x"))
    gb['w_out'] = _matmul_tn(sv['mix'], dx1b, name=n("proj_out_w"))
    z = sv['z']
    dzu, dzv, gs['a_ln_g'], gs['a_ln_b'], gs['a_wcat'], gs['a_bfull'] = _gmlp_bwd(
        z, dmix, sp['a_ln_g'], sp['a_ln_b'], sp['a_wcat'], sp['a_bfull'], name=n("gmlp"))
    dzb, dzg, *dbp = _rglru_bwd(z, dmix, sv['h0s'], [sp[k] for k in B_PRM], name=n("rglru"))
    gs.update(zip(B_PRM, dbp))
    dzc, gs['c_lb'], gs['c_ngf'] = _hgrn_bwd(z, dmix, sv['sts'], sp['c_lb'], sp['c_ngf'], name=n("hgrn"))
    dzd, gs['d_wd'], gs['d_scale'] = _pool_bwd(z, dmix, sp['d_wd'], sp['d_scale'], name=n("pool"))
    dz = jnp.concatenate([dzu, dzv, dzb, dzg, dzc, dzd], axis=1)
    gb['w_in'] = _matmul_tn(dz, sv['h'], name=n("proj_in_w"))
    dh = _matmul(dz, wb['w_in'], name=n("proj_in_x"))
    dx0, _, gs['norm1_g'] = _rms_bwd(sv['x'], sp['g1'], dh, dx1, name=n("norm1"))
    return dx0, gb, gs


SMALL_NAMES = [nm for nm in WEIGHT_NAMES if nm not in BIG_NAMES]


def _small_grads(raw):
    st = {k: jnp.stack([raw[l][k] for l in range(DEPTH)]) for k in raw[0]}
    tril = jnp.tril(jnp.ones((GMLP_CHUNK, GMLP_CHUNK), bool))
    vec = lambda a: a.reshape(DEPTH, -1)
    out = {nm: vec(st[k]) for nm, k in (('norm1_g', 'norm1_g'), ('norm2_g', 'norm2_g'), ('norm3_g', 'norm3_g'),
                                        ('a_ln_g', 'a_ln_g'), ('a_ln_b', 'a_ln_b'), ('b_conv_b', 'b_cb'),
                                        ('b_ba', 'b_ba'), ('b_bx', 'b_bx'), ('b_lam', 'b_lam'), ('c_lb', 'c_lb'),
                                        ('d_scale', 'd_scale'))}
    out['a_ws'] = jnp.where(tril, st['a_wcat'].reshape(DEPTH, N_HEADS, GMLP_CHUNK, GMLP_CHUNK), 0.0)
    out['a_bs'] = jnp.swapaxes(st['a_bfull'].reshape(DEPTH, GMLP_CHUNK, N_HEADS, HEAD_DIM).sum(-1), 1, 2)
    out['b_conv_w'] = st['b_cw']
    out['b_wa'], out['b_wx'], out['d_w'] = _diag_blocks(st['b_wa']), _diag_blocks(st['b_wx']), _diag_blocks(st['d_wd'])
    out['c_norm_g'] = st['c_ngf'].reshape(DEPTH, N_HEADS, HEAD_DIM).sum(1)
    out['ffn_conv_w'] = jnp.concatenate([st['f_dwg'][:, 0:3], st['f_dwv'][:, 0:3]], axis=2)
    out['ffn_conv_b'] = jnp.concatenate([st['f_dwg'][:, 3], st['f_dwv'][:, 3]], axis=1)
    return out


def _step(w, m, v, x, p, target):
    s = x.shape[1]
    dev = 4 * lax.axis_index("x") + 2 * lax.axis_index("y") + lax.axis_index("c")
    xs = x.reshape(s, D_MODEL)

    gathered = _all_gather(_pack_big_shards(w).astype(BF16), name="gather_weights")
    conv_shapes = [w['b_conv_w'].shape, w['ffn_conv_w'].shape]
    conv_rows = _flat_rows(conv_shapes)
    conv_all = _all_gather(_pack_flat([w['b_conv_w'], w['ffn_conv_w']], conv_rows), name="gather_conv_weights")
    parts = [_unpack_flat(conv_all[d], conv_shapes) for d in range(N_DEV)]
    wf = dict(w)
    wf['b_conv_w_full'] = jnp.concatenate([pt[0] for pt in parts], axis=-1)
    wf['ffn_conv_w_full'] = jnp.concatenate([pt[1] for pt in parts], axis=-1)
    lbs = _lbs_fwd(w['c_lb'], name="hgrn_bounds")

    stacked = _stacked_params(wf, lbs)
    p_all = p.reshape(DEPTH, s, PLE_DIM).astype(BF16)
    xl, saved, wbs, sps = xs, [], [], []
    for l in range(DEPTH):
        wb = _unpack_big_full(gathered, l)
        sp = {k: _Sel(a, l) for k, a in stacked.items()}
        p_bf = p_all[l]
        xl, sv = _layer_fwd(xl, p_bf, wb, sp, l)
        saved.append((sv, p_bf))
        wbs.append(wb)
        sps.append(sp)
    loss_part, dx, dfinal = _loss_head(xl, w['final_g'].reshape(1, D_MODEL), target.reshape(s, D_MODEL), name="loss_head")
    loss = lax.psum(loss_part[0, 0], ("x", "y", "c"))

    big, small = [None] * DEPTH, [None] * DEPTH
    for l in range(DEPTH - 1, -1, -1):
        sv, p_bf = saved[l]
        dx, big[l], small[l] = _layer_bwd(dx, sv, p_bf, wbs[l], sps[l], l)
    grad_x = dx.reshape(1, s, D_MODEL)

    packed = _pack_big_grads(big).reshape(4, 2, PACK_ROWS, D_MODEL)
    core = lax.axis_index("c")
    own = lax.dynamic_index_in_dim(packed, core, axis=1, keepdims=False)
    other = lax.dynamic_index_in_dim(packed, 1 - core, axis=1, keepdims=False)
    from_sibling = _swap_sibling(other, name="reduce_pair")
    chip_sum = _add_pairs(own, from_sibling, name="reduce_pair_add")
    from_chips = _exchange_chips(chip_sum, name="reduce_chips")
    gbig = _unpack_big_shard(_sum_slots(from_chips, name="reduce_chips_add"))

    small_parts = _small_grads(small)
    small_parts['c_lb'] = _lbs_bwd(w['c_lb'], small_parts['c_lb'], name="hgrn_bounds_bwd")
    small_parts['final_g'] = dfinal.reshape(D_MODEL)
    small_shapes = [small_parts[nm].shape for nm in SMALL_NAMES]
    small_rows = _flat_rows(small_shapes)
    small_all = _all_gather(_pack_flat([small_parts[nm] for nm in SMALL_NAMES], small_rows), name="gather_small_grads")
    gsmall = dict(zip(SMALL_NAMES, _unpack_flat(_sum_slots(small_all, name="sum_small_grads"), small_shapes)))
    for nm in ('b_conv_w', 'ffn_conv_w'):
        width = w[nm].shape[-1]
        gsmall[nm] = lax.dynamic_slice_in_dim(gsmall[nm], dev * width, width, axis=2)

    grads, delta, new_m, new_v = {}, {}, {}, {}
    for nm in BIG_NAMES:
        shp = w[nm].shape
        as2d = lambda a: a.reshape(shp[0] * shp[1], shp[2])
        grads[nm] = gbig[nm]
        d, nm_, nv_ = _adamw(as2d(w[nm]), as2d(gbig[nm]), as2d(m[nm]), as2d(v[nm]), name=f"adamw_{nm}")
        delta[nm], new_m[nm], new_v[nm] = d.reshape(shp), nm_.reshape(shp), nv_.reshape(shp)
    shapes = [w[nm].shape for nm in SMALL_NAMES]
    rows = _flat_rows(shapes)
    pk = lambda t: _pack_flat([t[nm] for nm in SMALL_NAMES], rows)
    d, nm_, nv_ = _adamw(pk(w), pk(gsmall), pk(m), pk(v), name="adamw_small")
    for nm, dd, mm_, vv_ in zip(SMALL_NAMES, _unpack_flat(d, shapes), _unpack_flat(nm_, shapes), _unpack_flat(nv_, shapes)):
        grads[nm], delta[nm], new_m[nm], new_v[nm] = gsmall[nm], dd, mm_, vv_

    return (loss, grad_x, *[grads[nm] for nm in WEIGHT_NAMES], *[delta[nm] for nm in WEIGHT_NAMES],
            *[new_m[nm] for nm in WEIGHT_NAMES], *[new_v[nm] for nm in WEIGHT_NAMES])


def kernel(x, p, norm1_g, w_in, a_ln_g, a_ln_b, a_ws, a_bs, b_conv_w, b_conv_b, b_wa, b_ba, b_wx, b_bx, b_lam, c_lb, c_norm_g, d_w, d_scale, w_out, norm2_g, w_up, ffn_conv_w, ffn_conv_b, w_down, norm3_g, w_pe, w_pg, final_g, loss_target, m_norm1_g, m_w_in, m_a_ln_g, m_a_ln_b, m_a_ws, m_a_bs, m_b_conv_w, m_b_conv_b, m_b_wa, m_b_ba, m_b_wx, m_b_bx, m_b_lam, m_c_lb, m_c_norm_g, m_d_w, m_d_scale, m_w_out, m_norm2_g, m_w_up, m_ffn_conv_w, m_ffn_conv_b, m_w_down, m_norm3_g, m_w_pe, m_w_pg, m_final_g, v_norm1_g, v_w_in, v_a_ln_g, v_a_ln_b, v_a_ws, v_a_bs, v_b_conv_w, v_b_conv_b, v_b_wa, v_b_ba, v_b_wx, v_b_bx, v_b_lam, v_c_lb, v_c_norm_g, v_d_w, v_d_scale, v_w_out, v_norm2_g, v_w_up, v_ffn_conv_w, v_ffn_conv_b, v_w_down, v_norm3_g, v_w_pe, v_w_pg, v_final_g):
    w = dict(norm1_g=norm1_g, w_in=w_in, a_ln_g=a_ln_g, a_ln_b=a_ln_b, a_ws=a_ws, a_bs=a_bs, b_conv_w=b_conv_w, b_conv_b=b_conv_b, b_wa=b_wa, b_ba=b_ba, b_wx=b_wx, b_bx=b_bx, b_lam=b_lam, c_lb=c_lb, c_norm_g=c_norm_g, d_w=d_w, d_scale=d_scale, w_out=w_out, norm2_g=norm2_g, w_up=w_up, ffn_conv_w=ffn_conv_w, ffn_conv_b=ffn_conv_b, w_down=w_down, norm3_g=norm3_g, w_pe=w_pe, w_pg=w_pg, final_g=final_g)
    m = dict(norm1_g=m_norm1_g, w_in=m_w_in, a_ln_g=m_a_ln_g, a_ln_b=m_a_ln_b, a_ws=m_a_ws, a_bs=m_a_bs, b_conv_w=m_b_conv_w, b_conv_b=m_b_conv_b, b_wa=m_b_wa, b_ba=m_b_ba, b_wx=m_b_wx, b_bx=m_b_bx, b_lam=m_b_lam, c_lb=m_c_lb, c_norm_g=m_c_norm_g, d_w=m_d_w, d_scale=m_d_scale, w_out=m_w_out, norm2_g=m_norm2_g, w_up=m_w_up, ffn_conv_w=m_ffn_conv_w, ffn_conv_b=m_ffn_conv_b, w_down=m_w_down, norm3_g=m_norm3_g, w_pe=m_w_pe, w_pg=m_w_pg, final_g=m_final_g)
    v = dict(norm1_g=v_norm1_g, w_in=v_w_in, a_ln_g=v_a_ln_g, a_ln_b=v_a_ln_b, a_ws=v_a_ws, a_bs=v_a_bs, b_conv_w=v_b_conv_w, b_conv_b=v_b_conv_b, b_wa=v_b_wa, b_ba=v_b_ba, b_wx=v_b_wx, b_bx=v_b_bx, b_lam=v_b_lam, c_lb=v_c_lb, c_norm_g=v_c_norm_g, d_w=v_d_w, d_scale=v_d_scale, w_out=v_w_out, norm2_g=v_norm2_g, w_up=v_w_up, ffn_conv_w=v_ffn_conv_w, ffn_conv_b=v_ffn_conv_b, w_down=v_w_down, norm3_g=v_norm3_g, w_pe=v_w_pe, w_pg=v_w_pg, final_g=v_final_g)
    return _step(w, m, v, x, p, loss_target)
```

```python
import functools

import jax
import jax.numpy as jnp
from jax import lax
from jax.experimental import pallas as pl
from jax.experimental.pallas import tpu as pltpu

F32 = jnp.float32
BF16 = jnp.bfloat16
MESH = pl.DeviceIdType.MESH

D_MODEL = 1024
DEPTH = 4
PLE_DIM = 256
W_GRP = 256
N_HEADS = 4
HEAD_DIM = 64
GMLP_CHUNK = 128
RGLRU_C = 8.0
HGRN_CHUNK = 64
HGRN_SUB = 16
POOL_WINDOWS = (2, 4, 8, 16)
D_FF = 2816
D_PROJ = 2304
EPS = 1e-6
ADAM_LR = 0.001
ADAM_B1 = 0.9
ADAM_B2 = 0.999
ADAM_EPS = 1e-08
ADAM_WD = 0.01
ADAM_STEP = 10

N_DEV = 8
MIB = 2 ** 20
V7X_VMEM_BYTES = 64 * MIB
HGRN_EXP_CLAMP = 60.0

WEIGHT_NAMES = ['norm1_g', 'w_in', 'a_ln_g', 'a_ln_b', 'a_ws', 'a_bs', 'b_conv_w', 'b_conv_b', 'b_wa', 'b_ba', 'b_wx',
                'b_bx', 'b_lam', 'c_lb', 'c_norm_g', 'd_w', 'd_scale', 'w_out', 'norm2_g', 'w_up', 'ffn_conv_w',
                'ffn_conv_b', 'w_down', 'norm3_g', 'w_pe', 'w_pg', 'final_g']
BIG_NAMES = ('w_in', 'w_out', 'w_up', 'w_down', 'w_pe', 'w_pg')
SLAB_ROWS = (('w_in', 288), ('w_out', 128), ('w_up', 704), ('w_down', 352), ('w_pe', 32), ('w_pg', 128))
LAYER_ROWS = sum(r for _, r in SLAB_ROWS)
PACK_ROWS = DEPTH * LAYER_ROWS


def _vmem_limit(block_bytes):
    want = 2 * block_bytes + 24 * MIB
    return int(min(max(want, 32 * MIB), V7X_VMEM_BYTES - 8 * MIB))


def _pcall(body, *, name, out_shape, grid=None, in_specs=None, out_specs=None, scratch_shapes=(),
           semantics=None, block_bytes=0):
    kw = {}
    if grid is not None:
        kw["grid"] = grid
    if in_specs is not None:
        kw["in_specs"] = in_specs
    if out_specs is not None:
        kw["out_specs"] = out_specs
    params = pltpu.CompilerParams(dimension_semantics=semantics, vmem_limit_bytes=_vmem_limit(block_bytes))
    return pl.pallas_call(body, name=name, out_shape=out_shape, scratch_shapes=list(scratch_shapes),
                          compiler_params=params, **kw)


def _pick(n, cands):
    for c in cands:
        if n % c == 0:
            return c
    return n


def _nbytes(shape, dtype):
    n = 1
    for s in shape:
        n *= s
    return n * jnp.dtype(dtype).itemsize


def _sds(shape, dtype):
    return jax.ShapeDtypeStruct(tuple(shape), dtype)


class _Sel:
    def __init__(self, arr, *idx):
        self.arr, self.idx = arr, tuple(idx)
        self.shape = arr.shape[len(idx):]
        self.ndim = len(self.shape)
        self.dtype = arr.dtype


def _arr(a):
    return a.arr if isinstance(a, _Sel) else a


def _spec(a, block=None, index=None):
    block = tuple(a.shape) if block is None else tuple(block)
    index = (lambda *g: (0,) * len(block)) if index is None else index
    if isinstance(a, _Sel):
        lead = a.idx
        return pl.BlockSpec((None,) * len(lead) + block, lambda *g: lead + tuple(index(*g)))
    return pl.BlockSpec(block, lambda *g: tuple(index(*g)))


def _ospec(a):
    return pl.BlockSpec(tuple(a.shape), lambda *g: (0,) * a.ndim)


def _rows_of(shape):
    return lax.broadcasted_iota(jnp.int32, shape, 0)


def _lanes_of(shape):
    return lax.broadcasted_iota(jnp.int32, shape, 1)


def _sdn(x, k, fill):
    n = x.shape[0]
    return jnp.where(_rows_of(x.shape) >= k, pltpu.roll(x, k % n, 0), fill)


def _sup(x, k, fill):
    n = x.shape[0]
    return jnp.where(_rows_of(x.shape) < n - k, pltpu.roll(x, (n - k) % n, 0), fill)


@functools.partial(jax.custom_vjp, nondiff_argnums=(1,))
def _shift_dn(x, k):
    return _sdn(x, k, 0.0)


def _shift_dn_fwd(x, k):
    return _sdn(x, k, 0.0), None


def _shift_dn_bwd(k, _, g):
    return (_sup(g, k, 0.0),)


_shift_dn.defvjp(_shift_dn_fwd, _shift_dn_bwd)


def _lin_scan_impl(a, b, h0):
    n = a.shape[0]
    aa, bb = a, b
    k = 1
    while k < n:
        bb = aa * _sdn(bb, k, 0.0) + bb
        aa = aa * _sdn(aa, k, 1.0)
        k *= 2
    return bb + aa * h0


@jax.custom_vjp
def _lin_scan(a, b, h0):
    return _lin_scan_impl(a, b, h0)


def _lin_scan_fwd(a, b, h0):
    h = _lin_scan_impl(a, b, h0)
    return h, (a, h, h0)


def _lin_scan_bwd(res, g):
    a, h, h0 = res
    n = a.shape[0]
    cc, gg = _sup(a, 1, 0.0), g
    k = 1
    while k < n:
        gg = gg + cc * _sup(gg, k, 0.0)
        cc = cc * _sup(cc, k, 1.0)
        k *= 2
    first = _rows_of(a.shape) == 0
    hprev = jnp.where(first, h0, _sdn(h, 1, 0.0))
    dh0 = jnp.sum(jnp.where(first, a * gg, 0.0), axis=0, keepdims=True)
    return gg * hprev, gg, dh0


_lin_scan.defvjp(_lin_scan_fwd, _lin_scan_bwd)


def _cumsum_sub_impl(x):
    pos = _rows_of(x.shape) % HGRN_SUB
    k = 1
    while k < HGRN_SUB:
        x = x + jnp.where(pos >= k, pltpu.roll(x, k, 0), 0.0)
        k *= 2
    return x


@jax.custom_vjp
def _cumsum_sub(x):
    return _cumsum_sub_impl(x)


def _cumsum_sub_fwd(x):
    return _cumsum_sub_impl(x), None


def _cumsum_sub_bwd(_, g):
    n = g.shape[0]
    pos = _rows_of(g.shape) % HGRN_SUB
    k = 1
    while k < HGRN_SUB:
        g = g + jnp.where(pos < HGRN_SUB - k, pltpu.roll(g, n - k, 0), 0.0)
        k *= 2
    return (g,)


_cumsum_sub.defvjp(_cumsum_sub_fwd, _cumsum_sub_bwd)


def _dot(a, b, ca, cb):
    return lax.dot_general(a.astype(BF16), b.astype(BF16), (((ca,), (cb,)), ((), ())), preferred_element_type=F32)


@jax.custom_vjp
def _mm(a, b):
    return _dot(a, b, 1, 0)


def _mm_fwd(a, b):
    return _dot(a, b, 1, 0), (a, b)


def _mm_bwd(res, g):
    a, b = res
    return _dot(g, b, 1, 1), _dot(a, g, 0, 0)


_mm.defvjp(_mm_fwd, _mm_bwd)


@jax.custom_vjp
def _mm_nt(a, b):
    return _dot(a, b, 1, 1)


def _mm_nt_fwd(a, b):
    return _dot(a, b, 1, 1), (a, b)


def _mm_nt_bwd(res, g):
    a, b = res
    return _dot(g, b, 1, 0), _dot(g, a, 0, 0)


_mm_nt.defvjp(_mm_nt_fwd, _mm_nt_bwd)


@jax.custom_vjp
def _mm_tn(a, b):
    return _dot(a, b, 0, 0)


def _mm_tn_fwd(a, b):
    return _dot(a, b, 0, 0), (a, b)


def _mm_tn_bwd(res, g):
    a, b = res
    return _dot(b, g, 1, 1), _dot(a, g, 1, 0)


_mm_tn.defvjp(_mm_tn_fwd, _mm_tn_bwd)


def _head_mask(shape, h):
    return (_lanes_of(shape) // HEAD_DIM) == h


def _stack_heads(x):
    return jnp.concatenate([jnp.where(_head_mask(x.shape, h), x, 0.0) for h in range(N_HEADS)], axis=0)


def _unstack_heads(p):
    r = p.shape[0] // N_HEADS
    out = None
    for h in range(N_HEADS):
        blk = p[h * r:(h + 1) * r]
        term = jnp.where(_head_mask(blk.shape, h), blk, 0.0)
        out = term if out is None else out + term
    return out


def _segmean_impl(x):
    n = x.shape[1]
    same = (lax.broadcasted_iota(jnp.int32, (n, n), 0) // HEAD_DIM) == (lax.broadcasted_iota(jnp.int32, (n, n), 1) // HEAD_DIM)
    m = jnp.where(same, 1.0 / HEAD_DIM, 0.0).astype(BF16)
    hi = x.astype(BF16)
    lo = (x - hi.astype(F32)).astype(BF16)
    dn = (((1,), (0,)), ((), ()))
    return (lax.dot_general(hi, m, dn, preferred_element_type=F32)
            + lax.dot_general(lo, m, dn, preferred_element_type=F32))


@jax.custom_vjp
def _segmean(x):
    return _segmean_impl(x)


def _segmean_fwd(x):
    return _segmean_impl(x), None


def _segmean_bwd(_, g):
    return (_segmean_impl(g),)


_segmean.defvjp(_segmean_fwd, _segmean_bwd)


def _log1p(u):
    w = 1.0 + u
    return jnp.where(w == 1.0, u, jnp.log(w) * (u / (w - 1.0)))


def _softplus(y):
    return jnp.maximum(y, 0.0) + _log1p(jnp.exp(-jnp.abs(y)))


def _rms(x, g):
    return x * lax.rsqrt(jnp.mean(x * x, axis=-1, keepdims=True) + EPS) * g


def _gmlp_chunk(zu, zv, ln_g, ln_b, wcat, bfull):
    u = jax.nn.gelu(zu)
    v = jax.nn.gelu(zv)
    mu = jnp.mean(v, axis=-1, keepdims=True)
    var = jnp.mean(jnp.square(v - mu), axis=-1, keepdims=True)
    vn = (v - mu) * lax.rsqrt(var + EPS) * ln_g + ln_b
    sv = _unstack_heads(_mm(wcat, vn)) + bfull
    return u * sv


def _rglru_tile(xb_ext, gb, h0, cw, cb, wa, ba, wx, bx, lam):
    xc = (cb + cw[0:1] * _shift_dn(xb_ext, 3) + cw[1:2] * _shift_dn(xb_ext, 2) + cw[2:3] * _shift_dn(xb_ext, 1)
          + cw[3:4] * xb_ext)[8:]
    r = jax.nn.sigmoid(_mm(xc, wa) + ba)
    i = jax.nn.sigmoid(_mm(xc, wx) + bx)
    log_a = (-RGLRU_C) * r * _softplus(-lam)
    a = jnp.exp(log_a)
    mult = jnp.sqrt(-jnp.tanh(log_a) * (a * a + 1.0))
    h = _lin_scan(a, mult * (i * xc), h0)
    y = h * jax.nn.gelu(gb)
    h_last = jnp.sum(jnp.where(_rows_of(h.shape) == h.shape[0] - 1, h, 0.0), axis=0, keepdims=True)
    return y, h_last


def _pool_tile(xd_ext, inv, wd, scale):
    s1 = xd_ext + _shift_dn(xd_ext, 1)
    s2 = s1 + _shift_dn(s1, 2)
    s3 = s2 + _shift_dn(s2, 4)
    s4 = s3 + _shift_dn(s3, 8)
    grp = _lanes_of(xd_ext.shape) // HEAD_DIM
    win = jnp.where(grp == 0, s1, jnp.where(grp == 1, s2, jnp.where(grp == 2, s3, s4)))
    pooled = win[16:] * inv - xd_ext[16:]
    return _mm(pooled, wd) * scale


def _hgrn_chunk(q, f, i, g, st, lb, ngf):
    n = q.shape[0]
    nsub = n // HGRN_SUB
    qs = jax.nn.silu(q)
    fg = lb + (1.0 - lb) * jax.nn.sigmoid(f)
    lf = jnp.log(fg)
    k = 1.0 - fg
    bl = _cumsum_sub(lf)
    row = _rows_of(q.shape)
    blk = row // HGRN_SUB
    betas = [jnp.zeros_like(lb)]
    for s in range(nsub):
        tot = jnp.sum(jnp.where(row == s * HGRN_SUB + HGRN_SUB - 1, bl, 0.0), axis=0, keepdims=True)
        betas.append(betas[-1] + tot)
    b_end = betas[nsub]
    beta_full = jnp.zeros_like(q)
    for s in range(1, nsub):
        beta_full = jnp.where(blk == s, betas[s], beta_full)
    qh = qs * jnp.exp(bl)
    qt = qh * jnp.exp(beta_full)
    b_all = beta_full + bl
    kt = k * jnp.exp(b_end - b_all)
    outs = []
    for s in range(nsub):
        kh = k * jnp.exp(jnp.minimum(betas[s] - b_all, HGRN_EXP_CLAMP))
        qstk = _stack_heads(qh[s * HGRN_SUB:(s + 1) * HGRN_SUB])
        att = _mm_nt(qstk, kh)
        ar = _rows_of(att.shape) % HGRN_SUB + s * HGRN_SUB
        att = jnp.where(_lanes_of(att.shape) <= ar, att, 0.0)
        outs.append(_unstack_heads(_mm(att, i)))
    o = jnp.concatenate(outs, axis=0) + _mm_nt(qt, st)
    same = (_rows_of(st.shape) // HEAD_DIM) == (_lanes_of(st.shape) // HEAD_DIM)
    st_new = st * jnp.exp(b_end) + jnp.where(same, _mm_tn(i, kt), 0.0)
    on = o * lax.rsqrt(_segmean(o * o) + EPS) * ngf
    return on * jax.nn.silu(g), st_new


def _ffn_tile(eg, ev, wg, bg, wv, bv):
    gt = (bg + wg[0:1] * _shift_dn(eg, 2) + wg[1:2] * _shift_dn(eg, 1) + wg[2:3] * eg)[8:]
    val = (bv + wv[0:1] * _shift_dn(ev, 2) + wv[1:2] * _shift_dn(ev, 1) + wv[2:3] * ev)[8:]
    return jax.nn.gelu(gt) * val


MXU_WIDTH = 256
MATMUL_BLOCK_BUDGET = 18 * MIB


def _matmul_tiles(m, k, n, a_dtype, b_dtype, out_dtype, has_res):
    best = None
    for tm in (2048, 1024, 512, 256):
        if m % tm:
            continue
        for tn in (1024, 768, 1408, 512, 256, 128):
            if n % tn:
                continue
            blk = (_nbytes((tm, k), a_dtype) + _nbytes((k, tn), b_dtype) + _nbytes((tm, tn), out_dtype)
                   + (_nbytes((tm, tn), F32) if has_res else 0))
            if blk > MATMUL_BLOCK_BUDGET:
                continue
            waste = -(-tn // MXU_WIDTH) * MXU_WIDTH / tn
            cost = (m // tm) * (n // tn) + 64 * (waste - 1.0)
            if best is None or cost < best[0]:
                best = (cost, tm, tn, blk)
    assert best is not None, (m, k, n)
    return best[1:]


def _matmul(a, b, *, name, nt=False, res=None, out_dtype=F32):
    m, k = a.shape
    n = b.shape[0] if nt else b.shape[1]
    tm, tn, blk = _matmul_tiles(m, k, n, a.dtype, b.dtype, out_dtype, res is not None)
    dims = (((1,), (1,)), ((), ())) if nt else (((1,), (0,)), ((), ()))

    def body(*refs):
        if res is None:
            a_ref, b_ref, o_ref = refs
        else:
            a_ref, b_ref, r_ref, o_ref = refs
        acc = lax.dot_general(a_ref[...], b_ref[...], dims, preferred_element_type=F32)
        if res is not None:
            acc = acc + r_ref[...]
        o_ref[...] = acc.astype(out_dtype)

    in_specs = [pl.BlockSpec((tm, k), lambda i, j: (i, 0)),
                _spec(b, (tn, k), lambda i, j: (j, 0)) if nt else _spec(b, (k, tn), lambda i, j: (0, j))]
    args = [a, _arr(b)]
    if res is not None:
        in_specs.append(pl.BlockSpec((tm, tn), lambda i, j: (i, j)))
        args.append(res)
    return _pcall(body, name=name, out_shape=_sds((m, n), out_dtype), grid=(m // tm, n // tn), in_specs=in_specs,
                  out_specs=pl.BlockSpec((tm, tn), lambda i, j: (i, j)), semantics=("parallel", "parallel"),
                  block_bytes=blk + _nbytes((tm, tn), F32))(*args)


def _matmul_tn(a, b, *, name, out_dtype=BF16):
    m, k1 = a.shape
    n = b.shape[1]
    tk = _pick(k1, (512, 256, 128))

    def body(a_ref, b_ref, o_ref):
        o_ref[...] = lax.dot_general(a_ref[...], b_ref[...], (((0,), (0,)), ((), ())),
                                     preferred_element_type=F32).astype(out_dtype)

    blk = 2 * _nbytes((m, tk), a.dtype) + _nbytes((m, n), b.dtype) + _nbytes((tk, n), F32)
    return _pcall(body, name=name, out_shape=_sds((k1, n), out_dtype), grid=(k1 // tk,),
                  in_specs=[pl.BlockSpec((m, tk), lambda i: (0, i)), pl.BlockSpec((m, n), lambda i: (0, 0))],
                  out_specs=pl.BlockSpec((tk, n), lambda i: (i, 0)), semantics=("parallel",),
                  block_bytes=blk)(a, b)


def _rms_fwd(x, g, *, name):
    s, d = x.shape
    tm = _pick(s, (512, 256))

    def body(x_ref, g_ref, o_ref):
        o_ref[...] = _rms(x_ref[...], g_ref[...]).astype(BF16)

    return _pcall(body, name=name, out_shape=_sds((s, d), BF16), grid=(s // tm,),
                  in_specs=[pl.BlockSpec((tm, d), lambda i: (i, 0)), _spec(g)],
                  out_specs=pl.BlockSpec((tm, d), lambda i: (i, 0)), semantics=("parallel",),
                  block_bytes=3 * _nbytes((tm, d), F32))(x, _arr(g))


def _rms_bwd(x, g, dh, dres, *, name):
    s, d = x.shape
    tm = _pick(s, (256, 128))

    def body(x_ref, g_ref, dh_ref, dr_ref, dx_ref, dxb_ref, dg_ref):
        _, vjp = jax.vjp(_rms, x_ref[...], g_ref[...])
        dxn, dg = vjp(dh_ref[...])
        dx = dr_ref[...] + dxn
        dx_ref[...] = dx
        dxb_ref[...] = dx.astype(BF16)

        @pl.when(pl.program_id(0) == 0)
        def _():
            dg_ref[...] = jnp.zeros_like(dg_ref)

        dg_ref[...] += dg

    row = pl.BlockSpec((tm, d), lambda i: (i, 0))
    vec = pl.BlockSpec((1, d), lambda i: (0, 0))
    return _pcall(body, name=name, out_shape=(_sds((s, d), F32), _sds((s, d), BF16), _sds((1, d), F32)),
                  grid=(s // tm,), in_specs=[row, _spec(g), row, row], out_specs=(row, row, vec),
                  semantics=("arbitrary",), block_bytes=8 * _nbytes((tm, d), F32))(x, _arr(g), dh, dres)


def _ple_fwd(x, gl, pe, *, name):
    s, d = x.shape
    tm = _pick(s, (512, 256))

    def body(x_ref, gl_ref, pe_ref, o_ref):
        o_ref[...] = x_ref[...] + pe_ref[...] * jax.nn.sigmoid(gl_ref[...])

    row = pl.BlockSpec((tm, d), lambda i: (i, 0))
    return _pcall(body, name=name, out_shape=_sds((s, d), F32), grid=(s // tm,), in_specs=[row, row, row],
                  out_specs=row, semantics=("parallel",), block_bytes=4 * _nbytes((tm, d), F32))(x, gl, pe)


def _ple_bwd(dx, gl, pe, *, name):
    s, d = dx.shape
    tm = _pick(s, (512, 256))

    def body(dx_ref, gl_ref, pe_ref, dpe_ref, dgl_ref):
        gate = jax.nn.sigmoid(gl_ref[...])
        dxv = dx_ref[...]
        dpe_ref[...] = (dxv * gate).astype(BF16)
        dgl_ref[...] = (dxv * pe_ref[...] * gate * (1.0 - gate)).astype(BF16)

    row = pl.BlockSpec((tm, d), lambda i: (i, 0))
    return _pcall(body, name=name, out_shape=(_sds((s, d), BF16), _sds((s, d), BF16)), grid=(s // tm,),
                  in_specs=[row, row, row], out_specs=(row, row), semantics=("parallel",),
                  block_bytes=5 * _nbytes((tm, d), F32))(dx, gl, pe)


def _loss_head(x, g, target, *, name):
    s, d = x.shape
    tm = _pick(s, (256, 128))

    def tile_loss(xv, gv, tv):
        err = jnp.square(_rms(xv, gv) - tv)
        return 0.5 * jnp.sum(jnp.mean(err, axis=-1, keepdims=True), axis=0, keepdims=True)

    def body(x_ref, g_ref, t_ref, l_ref, dx_ref, dg_ref):
        lv, vjp = jax.vjp(tile_loss, x_ref[...], g_ref[...], t_ref[...])
        dxv, dgv, _ = vjp(jnp.ones((1, 1), F32))
        dx_ref[...] = dxv

        @pl.when(pl.program_id(0) == 0)
        def _():
            l_ref[...] = jnp.zeros_like(l_ref)
            dg_ref[...] = jnp.zeros_like(dg_ref)

        l_ref[...] += jnp.broadcast_to(lv, l_ref.shape)
        dg_ref[...] += dgv

    row = pl.BlockSpec((tm, d), lambda i: (i, 0))
    vec = pl.BlockSpec((1, d), lambda i: (0, 0))
    return _pcall(body, name=name, out_shape=(_sds((8, 128), F32), _sds((s, d), F32), _sds((1, d), F32)),
                  grid=(s // tm,), in_specs=[row, vec, row],
                  out_specs=(pl.BlockSpec((8, 128), lambda i: (0, 0)), row, vec), semantics=("arbitrary",),
                  block_bytes=8 * _nbytes((tm, d), F32))(x, g, target)


def _acc_out(ref, val, first):
    @pl.when(first)
    def _():
        ref[...] = jnp.zeros_like(ref)

    ref[...] += val


def _gmlp_fwd(z, ln_g, ln_b, wcat, bfull, *, name):
    s = z.shape[0]
    t = _pick(s, (512, 256, 128))
    nch = t // GMLP_CHUNK

    def body(zu_ref, zv_ref, g_ref, b_ref, w_ref, bf_ref, o_ref):
        for c in range(nch):
            rows = pl.ds(c * GMLP_CHUNK, GMLP_CHUNK)
            o_ref[rows, :] = _gmlp_chunk(zu_ref[rows, :], zv_ref[rows, :], g_ref[...], b_ref[...], w_ref[...],
                                         bf_ref[...]).astype(BF16)

    col = lambda c: pl.BlockSpec((t, W_GRP), lambda i: (i, c))
    params = (ln_g, ln_b, wcat, bfull)
    return _pcall(body, name=name, out_shape=_sds((s, W_GRP), BF16), grid=(s // t,),
                  in_specs=[col(0), col(1)] + [_spec(a) for a in params],
                  out_specs=pl.BlockSpec((t, W_GRP), lambda i: (i, 0)), semantics=("parallel",),
                  block_bytes=4 * _nbytes((t, W_GRP), F32))(z, z, *[_arr(a) for a in params])


def _gmlp_bwd(z, dmix, ln_g, ln_b, wcat, bfull, *, name):
    s = z.shape[0]
    t = _pick(s, (512, 256, 128))
    nch = t // GMLP_CHUNK

    def body(zu_ref, zv_ref, dy_ref, g_ref, b_ref, w_ref, bf_ref, du_ref, dv_ref, dg_ref, db_ref, dw_ref, dbf_ref):
        acc = None
        for c in range(nch):
            rows = pl.ds(c * GMLP_CHUNK, GMLP_CHUNK)
            _, vjp = jax.vjp(_gmlp_chunk, zu_ref[rows, :], zv_ref[rows, :], g_ref[...], b_ref[...], w_ref[...],
                             bf_ref[...])
            du, dv, *dps = vjp(dy_ref[rows, :])
            du_ref[rows, :] = du.astype(BF16)
            dv_ref[rows, :] = dv.astype(BF16)
            acc = dps if acc is None else [x + y for x, y in zip(acc, dps)]
        first = pl.program_id(0) == 0
        for ref, val in zip((dg_ref, db_ref, dw_ref, dbf_ref), acc):
            _acc_out(ref, val, first)

    col = lambda c: pl.BlockSpec((t, W_GRP), lambda i: (i, c))
    params = (ln_g, ln_b, wcat, bfull)
    return _pcall(body, name=name,
                  out_shape=(_sds((s, W_GRP), BF16), _sds((s, W_GRP), BF16)) + tuple(_sds(a.shape, F32) for a in params),
                  grid=(s // t,), in_specs=[col(0), col(1), col(0)] + [_spec(a) for a in params],
                  out_specs=(col(0), col(0)) + tuple(_ospec(a) for a in params), semantics=("arbitrary",),
                  block_bytes=8 * _nbytes((t, W_GRP), F32))(z, z, dmix, *[_arr(a) for a in params])


def _rglru_fwd(z, prm, *, name):
    s = z.shape[0]
    t = _pick(s, (512, 256, 128))
    nt = s // t

    def body(xb_ref, halo_ref, gb_ref, *rest):
        prm_refs, (y_ref, h0s_ref, h_scr) = rest[:len(prm)], rest[len(prm):]
        i = pl.program_id(0)

        @pl.when(i == 0)
        def _():
            h_scr[...] = jnp.zeros_like(h_scr)

        halo = jnp.where(i == 0, 0.0, halo_ref[...])
        h0 = h_scr[...]
        y, h_last = _rglru_tile(jnp.concatenate([halo, xb_ref[...]], axis=0), gb_ref[...], h0,
                                *[r[...] for r in prm_refs])
        y_ref[...] = y.astype(BF16)
        h0s_ref[...] = jnp.broadcast_to(h0, h0s_ref.shape)
        h_scr[...] = h_last

    in_specs = [pl.BlockSpec((t, W_GRP), lambda i: (i, 2)),
                pl.BlockSpec((8, W_GRP), lambda i: (jnp.maximum(i * (t // 8) - 1, 0), 2)),
                pl.BlockSpec((t, W_GRP), lambda i: (i, 3))] + [_spec(a) for a in prm]
    return _pcall(body, name=name, out_shape=(_sds((s, W_GRP), BF16), _sds((nt, 8, W_GRP), F32)), grid=(nt,),
                  in_specs=in_specs,
                  out_specs=(pl.BlockSpec((t, W_GRP), lambda i: (i, 0)), pl.BlockSpec((None, 8, W_GRP), lambda i: (i, 0, 0))),
                  scratch_shapes=[pltpu.VMEM((1, W_GRP), F32)], semantics=("arbitrary",),
                  block_bytes=24 * _nbytes((t, W_GRP), F32))(z, z, z, *[_arr(a) for a in prm])


def _rglru_bwd(z, dmix, h0s, prm, *, name):
    s = z.shape[0]
    t = _pick(s, (512, 256, 128))
    nt = s // t
    npm = len(prm)

    def body(xb_ref, halo_ref, gb_ref, dy_ref, h0s_ref, *rest):
        prm_refs = rest[:npm]
        dxb_ref, dgb_ref = rest[npm:npm + 2]
        dprm_refs = rest[npm + 2:2 * npm + 2]
        dh_scr, dhalo_scr = rest[2 * npm + 2:]
        i = pl.program_id(0)
        r = nt - 1 - i

        @pl.when(i == 0)
        def _():
            dh_scr[...] = jnp.zeros_like(dh_scr)
            dhalo_scr[...] = jnp.zeros_like(dhalo_scr)

        halo = jnp.where(r == 0, 0.0, halo_ref[...])
        h0 = h0s_ref[0:1, :]
        _, vjp = jax.vjp(_rglru_tile, jnp.concatenate([halo, xb_ref[...]], axis=0), gb_ref[...], h0,
                         *[p[...] for p in prm_refs])
        dext, dgb, _dh0, *dps = vjp((dy_ref[...], dh_scr[...]))
        dmain = dext[8:]
        dxb = jnp.concatenate([dmain[:t - 8], dmain[t - 8:] + dhalo_scr[...]], axis=0)
        dxb_ref[...] = dxb.astype(BF16)
        dgb_ref[...] = dgb.astype(BF16)
        dh_scr[...] = _dh0
        dhalo_scr[...] = dext[:8]
        for ref, val in zip(dprm_refs, dps):
            _acc_out(ref, val, i == 0)

    rev = lambda c: pl.BlockSpec((t, W_GRP), lambda i: (nt - 1 - i, c))
    in_specs = [rev(2), pl.BlockSpec((8, W_GRP), lambda i: (jnp.maximum((nt - 1 - i) * (t // 8) - 1, 0), 2)), rev(3),
                rev(1), pl.BlockSpec((None, 8, W_GRP), lambda i: (nt - 1 - i, 0, 0))] + [_spec(a) for a in prm]
    return _pcall(body, name=name,
                  out_shape=(_sds((s, W_GRP), BF16), _sds((s, W_GRP), BF16)) + tuple(_sds(a.shape, F32) for a in prm),
                  grid=(nt,), in_specs=in_specs, out_specs=(rev(0), rev(0)) + tuple(_ospec(a) for a in prm),
                  scratch_shapes=[pltpu.VMEM((1, W_GRP), F32), pltpu.VMEM((8, W_GRP), F32)],
                  semantics=("arbitrary",), block_bytes=40 * _nbytes((t, W_GRP), F32))(z, z, z, dmix, h0s, *[_arr(a) for a in prm])


def _pool_inv(i, t):
    pos = (_rows_of((t, W_GRP)) + i * t + 1).astype(F32)
    grp = _lanes_of((t, W_GRP)) // HEAD_DIM
    win = jnp.where(grp == 0, float(POOL_WINDOWS[0]), jnp.where(grp == 1, float(POOL_WINDOWS[1]),
                    jnp.where(grp == 2, float(POOL_WINDOWS[2]), float(POOL_WINDOWS[3]))))
    return 1.0 / jnp.minimum(pos, win)


def _pool_fwd(z, wd, scale, *, name):
    s = z.shape[0]
    t = _pick(s, (512, 256, 128))

    def body(x_ref, halo_ref, wd_ref, sc_ref, y_ref):
        i = pl.program_id(0)
        halo = jnp.where(i == 0, 0.0, halo_ref[...])
        y = _pool_tile(jnp.concatenate([halo, x_ref[...]], axis=0), _pool_inv(i, t), wd_ref[...], sc_ref[...])
        y_ref[...] = y.astype(BF16)

    in_specs = [pl.BlockSpec((t, W_GRP), lambda i: (i, 8)),
                pl.BlockSpec((16, W_GRP), lambda i: (jnp.maximum(i * (t // 16) - 1, 0), 8)), _spec(wd), _spec(scale)]
    return _pcall(body, name=name, out_shape=_sds((s, W_GRP), BF16), grid=(s // t,), in_specs=in_specs,
                  out_specs=pl.BlockSpec((t, W_GRP), lambda i: (i, 0)), semantics=("parallel",),
                  block_bytes=12 * _nbytes((t, W_GRP), F32))(z, z, _arr(wd), _arr(scale))


def _pool_bwd(z, dmix, wd, scale, *, name):
    s = z.shape[0]
    t = _pick(s, (512, 256, 128))
    nt = s // t

    def body(x_ref, halo_ref, dy_ref, wd_ref, sc_ref, dx_ref, dwd_ref, dsc_ref, dhalo_scr):
        i = pl.program_id(0)
        r = nt - 1 - i

        @pl.when(i == 0)
        def _():
            dhalo_scr[...] = jnp.zeros_like(dhalo_scr)

        halo = jnp.where(r == 0, 0.0, halo_ref[...])
        inv = _pool_inv(r, t)
        _, vjp = jax.vjp(lambda e, w, sc: _pool_tile(e, inv, w, sc), jnp.concatenate([halo, x_ref[...]], axis=0),
                         wd_ref[...], sc_ref[...])
        dext, dwd, dsc = vjp(dy_ref[...])
        dmain = dext[16:]
        dx = jnp.concatenate([dmain[:t - 16], dmain[t - 16:] + dhalo_scr[...]], axis=0)
        dx_ref[...] = dx.astype(BF16)
        dhalo_scr[...] = dext[:16]
        _acc_out(dwd_ref, dwd, i == 0)
        _acc_out(dsc_ref, dsc, i == 0)

    rev = lambda c: pl.BlockSpec((t, W_GRP), lambda i: (nt - 1 - i, c))
    in_specs = [rev(8), pl.BlockSpec((16, W_GRP), lambda i: (jnp.maximum((nt - 1 - i) * (t // 16) - 1, 0), 8)), rev(3),
                _spec(wd), _spec(scale)]
    return _pcall(body, name=name, out_shape=(_sds((s, W_GRP), BF16), _sds(wd.shape, F32), _sds(scale.shape, F32)),
                  grid=(nt,), in_specs=in_specs, out_specs=(rev(0), _ospec(wd), _ospec(scale)),
                  scratch_shapes=[pltpu.VMEM((16, W_GRP), F32)], semantics=("arbitrary",),
                  block_bytes=20 * _nbytes((t, W_GRP), F32))(z, z, dmix, _arr(wd), _arr(scale))


def _hgrn_fwd(z, lb, ngf, *, name):
    s = z.shape[0]
    c = HGRN_CHUNK
    nc = s // c

    def body(q_ref, f_ref, i_ref, g_ref, lb_ref, ng_ref, y_ref, sts_ref, st_scr):
        @pl.when(pl.program_id(0) == 0)
        def _():
            st_scr[...] = jnp.zeros_like(st_scr)

        st = st_scr[...]
        sts_ref[...] = st
        y, st_new = _hgrn_chunk(q_ref[...], f_ref[...], i_ref[...], g_ref[...], st, lb_ref[...], ng_ref[...])
        y_ref[...] = y.astype(BF16)
        st_scr[...] = st_new

    col = lambda k: pl.BlockSpec((c, W_GRP), lambda i: (i, k))
    vec = pl.BlockSpec((1, W_GRP), lambda i: (0, 0))
    return _pcall(body, name=name, out_shape=(_sds((s, W_GRP), BF16), _sds((nc, W_GRP, W_GRP), F32)), grid=(nc,),
                  in_specs=[col(4), col(5), col(6), col(7), _spec(lb), _spec(ngf)],
                  out_specs=(pl.BlockSpec((c, W_GRP), lambda i: (i, 0)), pl.BlockSpec((None, W_GRP, W_GRP), lambda i: (i, 0, 0))),
                  scratch_shapes=[pltpu.VMEM((W_GRP, W_GRP), F32)], semantics=("arbitrary",),
                  block_bytes=16 * _nbytes((W_GRP, W_GRP), F32))(z, z, z, z, _arr(lb), _arr(ngf))


def _hgrn_bwd(z, dmix, sts, lb, ngf, *, name):
    s = z.shape[0]
    c = HGRN_CHUNK
    nc = s // c

    def body(q_ref, f_ref, i_ref, g_ref, dy_ref, st_ref, lb_ref, ng_ref, dz_ref, dlb_ref, dng_ref, dst_scr):
        i = pl.program_id(0)

        @pl.when(i == 0)
        def _():
            dst_scr[...] = jnp.zeros_like(dst_scr)

        _, vjp = jax.vjp(_hgrn_chunk, q_ref[...], f_ref[...], i_ref[...], g_ref[...], st_ref[...], lb_ref[...],
                         ng_ref[...])
        dq, df, di, dg, dst, dlb, dng = vjp((dy_ref[...], dst_scr[...]))
        dz_ref[...] = jnp.concatenate([dq, df, di, dg], axis=1).astype(BF16)
        dst_scr[...] = dst
        _acc_out(dlb_ref, dlb, i == 0)
        _acc_out(dng_ref, dng, i == 0)

    rev = lambda k: pl.BlockSpec((c, W_GRP), lambda i: (nc - 1 - i, k))
    vec = pl.BlockSpec((1, W_GRP), lambda i: (0, 0))
    return _pcall(body, name=name, out_shape=(_sds((s, 4 * W_GRP), BF16), _sds((1, W_GRP), F32), _sds((1, W_GRP), F32)),
                  grid=(nc,),
                  in_specs=[rev(4), rev(5), rev(6), rev(7), rev(2),
                            pl.BlockSpec((None, W_GRP, W_GRP), lambda i: (nc - 1 - i, 0, 0)), _spec(lb), _spec(ngf)],
                  out_specs=(pl.BlockSpec((c, 4 * W_GRP), lambda i: (nc - 1 - i, 0)), vec, vec),
                  scratch_shapes=[pltpu.VMEM((W_GRP, W_GRP), F32)], semantics=("arbitrary",),
                  block_bytes=32 * _nbytes((W_GRP, W_GRP), F32))(z, z, z, z, dmix, sts, _arr(lb), _arr(ngf))


def _lbs_fwd(c_lb, *, name):
    def body(c_ref, o_ref):
        c = c_ref[...]
        e = jnp.exp(c - jnp.max(c, axis=0, keepdims=True))
        sm = e / jnp.sum(e, axis=0, keepdims=True)
        run = jnp.zeros((1, W_GRP), F32)
        o_ref[0:1, :] = run
        for l in range(1, DEPTH):
            run = run + sm[l:l + 1]
            o_ref[l:l + 1, :] = run

    return _pcall(body, name=name, out_shape=_sds((DEPTH, W_GRP), F32))(c_lb)


def _lbs_bwd(c_lb, dlbs, *, name):
    def body(c_ref, d_ref, o_ref):
        c = c_ref[...]
        e = jnp.exp(c - jnp.max(c, axis=0, keepdims=True))
        sm = e / jnp.sum(e, axis=0, keepdims=True)
        d = d_ref[...]
        dsm = [None] * DEPTH
        run = jnp.zeros((1, W_GRP), F32)
        for l in range(DEPTH - 1, 0, -1):
            run = run + d[l:l + 1]
            dsm[l] = run
        dsm[0] = jnp.zeros((1, W_GRP), F32)
        inner = sum(sm[l:l + 1] * dsm[l] for l in range(DEPTH))
        for l in range(DEPTH):
            o_ref[l:l + 1, :] = sm[l:l + 1] * (dsm[l] - inner)

    return _pcall(body, name=name, out_shape=_sds((DEPTH, W_GRP), F32))(c_lb, dlbs)


def _ffn_fwd(hg, hv, cwf, cbf, *, name):
    s, n = hg.shape
    t = _pick(s, (256, 128))
    cw = _pick(n, (1408, 256, 128))
    nj = n // cw

    def body(g_ref, gh_ref, v_ref, vh_ref, wg_ref, bg_ref, wv_ref, bv_ref, o_ref):
        first = pl.program_id(1) == 0
        eg = jnp.concatenate([jnp.where(first, 0.0, gh_ref[...]), g_ref[...]], axis=0)
        ev = jnp.concatenate([jnp.where(first, 0.0, vh_ref[...]), v_ref[...]], axis=0)
        o_ref[...] = _ffn_tile(eg, ev, wg_ref[...], bg_ref[...], wv_ref[...], bv_ref[...]).astype(BF16)

    main = pl.BlockSpec((t, cw), lambda j, i: (i, j))
    halo = pl.BlockSpec((8, cw), lambda j, i: (jnp.maximum(i * (t // 8) - 1, 0), j))
    taps = lambda off: _spec(cwf, (3, cw), lambda j, i: (0, j + off))
    bias = lambda off: _spec(cbf, (1, cw), lambda j, i: (0, j + off))
    return _pcall(body, name=name, out_shape=_sds((s, n), BF16), grid=(nj, s // t),
                  in_specs=[main, halo, main, halo, taps(0), bias(0), taps(nj), bias(nj)], out_specs=main,
                  semantics=("parallel", "parallel"), block_bytes=12 * _nbytes((t, cw), F32))(
                      hg, hg, hv, hv, _arr(cwf), _arr(cbf), _arr(cwf), _arr(cbf))


def _ffn_bwd(hg, hv, da, cwf, cbf, *, name):
    s, n = hg.shape
    t = _pick(s, (256, 128))
    cw = _pick(n, (1408, 256, 128))
    nt = s // t
    nj = n // cw

    def body(g_ref, gh_ref, v_ref, vh_ref, da_ref, wg_ref, bg_ref, wv_ref, bv_ref, dg_ref, dv_ref, dwg_ref, dwv_ref,
             cg_scr, cv_scr):
        i = pl.program_id(1)
        r = nt - 1 - i

        @pl.when(i == 0)
        def _():
            cg_scr[...] = jnp.zeros_like(cg_scr)
            cv_scr[...] = jnp.zeros_like(cv_scr)

        eg = jnp.concatenate([jnp.where(r == 0, 0.0, gh_ref[...]), g_ref[...]], axis=0)
        ev = jnp.concatenate([jnp.where(r == 0, 0.0, vh_ref[...]), v_ref[...]], axis=0)
        _, vjp = jax.vjp(_ffn_tile, eg, ev, wg_ref[...], bg_ref[...], wv_ref[...], bv_ref[...])
        deg, dev, dwg, dbg, dwv, dbv = vjp(da_ref[...])
        for dext, scr, ref in ((deg, cg_scr, dg_ref), (dev, cv_scr, dv_ref)):
            dmain = dext[8:]
            ref[...] = jnp.concatenate([dmain[:t - 8], dmain[t - 8:] + scr[...]], axis=0).astype(BF16)
            scr[...] = dext[:8]
        zeros = jnp.zeros((4, cw), F32)
        _acc_out(dwg_ref, jnp.concatenate([dwg, dbg, zeros], axis=0), i == 0)
        _acc_out(dwv_ref, jnp.concatenate([dwv, dbv, zeros], axis=0), i == 0)

    main = pl.BlockSpec((t, cw), lambda j, i: (nt - 1 - i, j))
    halo = pl.BlockSpec((8, cw), lambda j, i: (jnp.maximum((nt - 1 - i) * (t // 8) - 1, 0), j))
    taps = lambda off: _spec(cwf, (3, cw), lambda j, i: (0, j + off))
    bias = lambda off: _spec(cbf, (1, cw), lambda j, i: (0, j + off))
    w8 = pl.BlockSpec((8, cw), lambda j, i: (0, j))
    return _pcall(body, name=name,
                  out_shape=(_sds((s, n), BF16), _sds((s, n), BF16), _sds((8, n), F32), _sds((8, n), F32)),
                  grid=(nj, nt), in_specs=[main, halo, main, halo, main, taps(0), bias(0), taps(nj), bias(nj)],
                  out_specs=(main, main, w8, w8),
                  scratch_shapes=[pltpu.VMEM((8, cw), F32), pltpu.VMEM((8, cw), F32)],
                  semantics=("parallel", "arbitrary"), block_bytes=24 * _nbytes((t, cw), F32))(
                      hg, hg, hv, hv, da, _arr(cwf), _arr(cbf), _arr(cwf), _arr(cbf))


def _all_gather(x, *, name):
    r, c = x.shape

    def body(x_ref, out_ref, send_sems, recv_sems, local_sem):
        mx, my, mc = lax.axis_index("x"), lax.axis_index("y"), lax.axis_index("c")
        me, sibling = (mx, my, mc), (mx, my, 1 - mc)
        chips = [(1 - mx, my), (mx, 1 - my), (1 - mx, 1 - my)]

        def slot(px, py, pc):
            return out_ref.at[4 * px + 2 * py + pc]

        def copy(k, block, to, src=None):
            return pltpu.make_async_remote_copy(src_ref=slot(*block) if src is None else src, dst_ref=slot(*block),
                                                send_sem=send_sems.at[k], recv_sem=recv_sems.at[k],
                                                device_id=to, device_id_type=MESH)

        mine = pltpu.make_async_copy(x_ref, slot(*me), local_sem)
        mine.start()
        first = [copy(0, me, sibling, src=x_ref)]
        first += [copy(1 + j, me, (*chip, mc), src=x_ref) for j, chip in enumerate(chips)]
        for cp in first:
            cp.start()
        passed = [copy(4 + j, (*chip, mc), sibling) for j, chip in enumerate(chips)]
        for j, chip in enumerate(chips):
            copy(1 + j, (*chip, mc), me).wait_recv()
            passed[j].start()
        copy(0, sibling, me).wait_recv()
        for j, chip in enumerate(chips):
            copy(4 + j, (*chip, 1 - mc), me).wait_recv()
        for cp in first + passed:
            cp.wait_send()
        mine.wait()

    hbm = pl.BlockSpec(memory_space=pl.ANY)
    return _pcall(body, name=name, out_shape=_sds((N_DEV, r, c), x.dtype), in_specs=[hbm], out_specs=hbm,
                  scratch_shapes=[pltpu.SemaphoreType.DMA((7,)), pltpu.SemaphoreType.DMA((7,)),
                                  pltpu.SemaphoreType.DMA(())])(x)


def _swap_sibling(x, *, name):
    def body(x_ref, out_ref, send_sem, recv_sem):
        sibling = (lax.axis_index("x"), lax.axis_index("y"), 1 - lax.axis_index("c"))
        cp = pltpu.make_async_remote_copy(src_ref=x_ref, dst_ref=out_ref, send_sem=send_sem, recv_sem=recv_sem,
                                          device_id=sibling, device_id_type=MESH)
        cp.start()
        cp.wait()

    hbm = pl.BlockSpec(memory_space=pl.ANY)
    return _pcall(body, name=name, out_shape=_sds(x.shape, x.dtype), in_specs=[hbm], out_specs=hbm,
                  scratch_shapes=[pltpu.SemaphoreType.DMA(()), pltpu.SemaphoreType.DMA(())])(x)


def _exchange_chips(p, *, name):
    def body(p_ref, out_ref, send_sems, recv_sems, local_sem):
        mx, my, mc = lax.axis_index("x"), lax.axis_index("y"), lax.axis_index("c")
        mine_q = 2 * mx + my
        chips = [(1 - mx, my), (mx, 1 - my), (1 - mx, 1 - my)]

        def copy(k, chip):
            return pltpu.make_async_remote_copy(src_ref=p_ref.at[2 * chip[0] + chip[1]], dst_ref=out_ref.at[mine_q],
                                                send_sem=send_sems.at[k], recv_sem=recv_sems.at[k],
                                                device_id=(*chip, mc), device_id_type=MESH)

        def arrival(k, chip):
            return pltpu.make_async_remote_copy(src_ref=p_ref.at[mine_q], dst_ref=out_ref.at[2 * chip[0] + chip[1]],
                                                send_sem=send_sems.at[k], recv_sem=recv_sems.at[k],
                                                device_id=(*chip, mc), device_id_type=MESH)

        own = pltpu.make_async_copy(p_ref.at[mine_q], out_ref.at[mine_q], local_sem)
        own.start()
        sends = [copy(k, chip) for k, chip in enumerate(chips)]
        for cp in sends:
            cp.start()
        for k, chip in enumerate(chips):
            arrival(k, chip).wait_recv()
        for cp in sends:
            cp.wait_send()
        own.wait()

    hbm = pl.BlockSpec(memory_space=pl.ANY)
    return _pcall(body, name=name, out_shape=_sds(p.shape, p.dtype), in_specs=[hbm], out_specs=hbm,
                  scratch_shapes=[pltpu.SemaphoreType.DMA((3,)), pltpu.SemaphoreType.DMA((3,)),
                                  pltpu.SemaphoreType.DMA(())])(p)


def _add_pairs(a, b, *, name):
    q, r, c = a.shape
    tr = _pick(r, (544, 408, 272, 136, 64, 32, 16, 8))

    def body(a_ref, b_ref, o_ref):
        o_ref[...] = (a_ref[...].astype(F32) + b_ref[...].astype(F32)).astype(o_ref.dtype)

    blk = pl.BlockSpec((None, tr, c), lambda i, j: (i, j, 0))
    return _pcall(body, name=name, out_shape=_sds(a.shape, a.dtype), grid=(q, r // tr), in_specs=[blk, blk],
                  out_specs=blk, semantics=("parallel", "parallel"), block_bytes=4 * _nbytes((tr, c), F32))(a, b)


def _sum_slots(p, *, name):
    q, r, c = p.shape
    tr = _pick(r, (544, 408, 272, 192, 136, 64, 32, 16, 8))

    def body(p_ref, o_ref):
        acc = p_ref[0].astype(F32)
        for k in range(1, q):
            acc = acc + p_ref[k].astype(F32)
        o_ref[...] = acc

    return _pcall(body, name=name, out_shape=_sds((r, c), F32), grid=(r // tr,),
                  in_specs=[pl.BlockSpec((q, tr, c), lambda i: (0, i, 0))],
                  out_specs=pl.BlockSpec((tr, c), lambda i: (i, 0)), semantics=("parallel",),
                  block_bytes=(q + 2) * _nbytes((tr, c), F32))(p)


BIG_COMM = (('w_in', 288, D_MODEL), ('w_out', 128, D_MODEL), ('w_up', 704, D_MODEL), ('w_down', 352, D_MODEL),
            ('w_pe', 128, PLE_DIM), ('w_pg', 128, D_MODEL))
HBM_SPEC = pl.BlockSpec(memory_space=pl.ANY)


def _gather_layer(shards, l, *, name):
    na = len(shards)

    def body(*refs):
        x_refs, out_refs = refs[:na], refs[na:2 * na]
        send_sems, recv_sems, local_sems = refs[2 * na:]
        mx, my, mc = lax.axis_index("x"), lax.axis_index("y"), lax.axis_index("c")
        me, sibling = (mx, my, mc), (mx, my, 1 - mc)
        chips = [(1 - mx, my), (mx, 1 - my), (1 - mx, 1 - my)]

        def slot(a, px, py, pc):
            return out_refs[a].at[4 * px + 2 * py + pc]

        def copy(k, a, block, to, own=False):
            return pltpu.make_async_remote_copy(src_ref=x_refs[a].at[l] if own else slot(a, *block),
                                                dst_ref=slot(a, *block), send_sem=send_sems.at[k, a],
                                                recv_sem=recv_sems.at[k, a], device_id=to, device_id_type=MESH)

        mine = [pltpu.make_async_copy(x_refs[a].at[l], slot(a, *me), local_sems.at[a]) for a in range(na)]
        for cp in mine:
            cp.start()
        first = []
        for a in range(na):
            first.append(copy(0, a, me, sibling, own=True))
            first += [copy(1 + j, a, me, (*chip, mc), own=True) for j, chip in enumerate(chips)]
        for cp in first:
            cp.start()
        passed = []
        for j, chip in enumerate(chips):
            for a in range(na):
                copy(1 + j, a, (*chip, mc), me).wait_recv()
                fwd = copy(4 + j, a, (*chip, mc), sibling)
                fwd.start()
                passed.append(fwd)
        for a in range(na):
            copy(0, a, sibling, me).wait_recv()
        for j, chip in enumerate(chips):
            for a in range(na):
                copy(4 + j, a, (*chip, 1 - mc), me).wait_recv()
        for cp in first + passed:
            cp.wait_send()
        for cp in mine:
            cp.wait()

    return _pcall(body, name=name, out_shape=tuple(_sds((N_DEV,) + x.shape[1:], x.dtype) for x in shards),
                  in_specs=[HBM_SPEC] * na, out_specs=(HBM_SPEC,) * na,
                  scratch_shapes=[pltpu.SemaphoreType.DMA((7, na)), pltpu.SemaphoreType.DMA((7, na)),
                                  pltpu.SemaphoreType.DMA((na,))])(*shards)


def _pair_swap(grads, *, name):
    na = len(grads)

    def body(*refs):
        g_refs, own_refs, recv_refs = refs[:na], refs[na:2 * na], refs[2 * na:3 * na]
        send_sems, recv_sems, local_sems = refs[3 * na:]
        mx, my, mc = lax.axis_index("x"), lax.axis_index("y"), lax.axis_index("c")
        sibling = (mx, my, 1 - mc)
        for a in range(na):
            for q in range(4):
                pltpu.make_async_copy(g_refs[a].at[q, mc], own_refs[a].at[q], local_sems.at[a]).start()
                pltpu.make_async_remote_copy(src_ref=g_refs[a].at[q, 1 - mc], dst_ref=recv_refs[a].at[q],
                                             send_sem=send_sems.at[a], recv_sem=recv_sems.at[a],
                                             device_id=sibling, device_id_type=MESH).start()
        for a in range(na):
            pltpu.make_async_remote_copy(src_ref=recv_refs[a], dst_ref=recv_refs[a], send_sem=send_sems.at[a],
                                         recv_sem=recv_sems.at[a], device_id=sibling, device_id_type=MESH).wait()
            pltpu.make_async_copy(own_refs[a], own_refs[a], local_sems.at[a]).wait()

    half = tuple(_sds((4,) + g.shape[2:], g.dtype) for g in grads)
    return _pcall(body, name=name, out_shape=half + half, in_specs=[HBM_SPEC] * na, out_specs=(HBM_SPEC,) * (2 * na),
                  scratch_shapes=[pltpu.SemaphoreType.DMA((na,)), pltpu.SemaphoreType.DMA((na,)),
                                  pltpu.SemaphoreType.DMA((na,))])(*grads)


def _add_slabs(own, recv, *, name):
    na = len(own)

    def body(*refs):
        for a in range(na):
            refs[2 * na + a][...] = (refs[a][...].astype(F32) + refs[na + a][...].astype(F32)).astype(BF16)

    specs = [pl.BlockSpec((None,) + x.shape[1:], lambda q: (q, 0, 0)) for x in own]
    blk = sum(_nbytes(x.shape[1:], F32) for x in own)
    return _pcall(body, name=name, out_shape=tuple(_sds(x.shape, BF16) for x in own), grid=(4,),
                  in_specs=specs + specs, out_specs=tuple(specs), semantics=("parallel",), block_bytes=2 * blk)(*own, *recv)


def _chip_exchange(parts, *, name):
    na = len(parts)

    def body(*refs):
        p_refs, out_refs = refs[:na], refs[na:2 * na]
        send_sems, recv_sems, local_sems = refs[2 * na:]
        mx, my, mc = lax.axis_index("x"), lax.axis_index("y"), lax.axis_index("c")
        mine_q = 2 * mx + my
        chips = [(1 - mx, my), (mx, 1 - my), (1 - mx, 1 - my)]
        owns = [pltpu.make_async_copy(p_refs[a].at[mine_q], out_refs[a].at[mine_q], local_sems.at[a]) for a in range(na)]
        for cp in owns:
            cp.start()
        sends = []
        for a in range(na):
            for k, chip in enumerate(chips):
                sends.append(pltpu.make_async_remote_copy(
                    src_ref=p_refs[a].at[2 * chip[0] + chip[1]], dst_ref=out_refs[a].at[mine_q],
                    send_sem=send_sems.at[k, a], recv_sem=recv_sems.at[k, a], device_id=(*chip, mc), device_id_type=MESH))
        for cp in sends:
            cp.start()
        for a in range(na):
            for k, chip in enumerate(chips):
                pltpu.make_async_remote_copy(
                    src_ref=p_refs[a].at[mine_q], dst_ref=out_refs[a].at[2 * chip[0] + chip[1]],
                    send_sem=send_sems.at[k, a], recv_sem=recv_sems.at[k, a], device_id=(*chip, mc),
                    device_id_type=MESH).wait_recv()
        for cp in sends:
            cp.wait_send()
        for cp in owns:
            cp.wait()

    return _pcall(body, name=name, out_shape=tuple(_sds(x.shape, x.dtype) for x in parts), in_specs=[HBM_SPEC] * na,
                  out_specs=(HBM_SPEC,) * na,
                  scratch_shapes=[pltpu.SemaphoreType.DMA((3, na)), pltpu.SemaphoreType.DMA((3, na)),
                                  pltpu.SemaphoreType.DMA((na,))])(*parts)


def _sum_chips(parts, *, name):
    na = len(parts)

    def body(*refs):
        for a in range(na):
            p_ref = refs[a]
            acc = p_ref[0].astype(F32)
            for k in range(1, 4):
                acc = acc + p_ref[k].astype(F32)
            refs[na + a][...] = acc

    half = lambda x: x.shape[1] // 2
    in_specs = [pl.BlockSpec((4, half(x), x.shape[2]), lambda i: (0, i, 0)) for x in parts]
    out_specs = tuple(pl.BlockSpec((half(x), x.shape[2]), lambda i: (i, 0)) for x in parts)
    blk = sum(_nbytes((6, half(x), x.shape[2]), BF16) for x in parts)
    return _pcall(body, name=name, out_shape=tuple(_sds(x.shape[1:], F32) for x in parts), grid=(2,),
                  in_specs=in_specs, out_specs=out_specs, semantics=("parallel",), block_bytes=blk)(*parts)


def _reduce_layer(grads, l):
    n = lambda s: f"l{l}_{s}"
    views = [g.reshape(4, 2, g.shape[0] // N_DEV, g.shape[1]) for g in grads]
    both = _pair_swap(views, name=n("reduce_pair"))
    na = len(grads)
    chip_sum = _add_slabs(both[:na], both[na:], name=n("reduce_pair_add"))
    from_chips = _chip_exchange(chip_sum, name=n("reduce_chips"))
    return _sum_chips(from_chips, name=n("reduce_chips_add"))


def _adamw(w, g, m, v, *, name):
    lead, (r, c) = w.shape[:-2], w.shape[-2:]
    tr = _pick(r, (512, 256, 192, 128, 64, 32, 16, 8))
    c1 = 1.0 / (1.0 - ADAM_B1 ** ADAM_STEP)
    c2 = 1.0 / (1.0 - ADAM_B2 ** ADAM_STEP)

    def body(w_ref, g_ref, m_ref, v_ref, d_ref, nm_ref, nv_ref):
        gv = g_ref[...]
        nm = ADAM_B1 * m_ref[...] + (1.0 - ADAM_B1) * gv
        nv = ADAM_B2 * v_ref[...] + (1.0 - ADAM_B2) * jnp.square(gv)
        d_ref[...] = -ADAM_LR * ((nm * c1) / (jnp.sqrt(nv * c2) + ADAM_EPS) + ADAM_WD * w_ref[...])
        nm_ref[...] = nm
        nv_ref[...] = nv

    if lead:
        blk = pl.BlockSpec((None, tr, c), lambda k, i: (k, i, 0))
        grid, sem = (lead[0], r // tr), ("parallel", "parallel")
    else:
        blk = pl.BlockSpec((tr, c), lambda i: (i, 0))
        grid, sem = (r // tr,), ("parallel",)
    out = _sds(w.shape, F32)
    return _pcall(body, name=name, out_shape=(out, out, out), grid=grid, in_specs=[blk] * 4,
                  out_specs=(blk, blk, blk), semantics=sem, block_bytes=7 * _nbytes((tr, c), F32))(w, g, m, v)


def _pack_flat(arrs, rows, cols=1024):
    flat = jnp.concatenate([a.reshape(-1).astype(F32) for a in arrs])
    pad = rows * cols - flat.shape[0]
    return jnp.pad(flat, (0, pad)).reshape(rows, cols)


def _unpack_flat(buf, shapes):
    flat = buf.reshape(-1)
    out, off = [], 0
    for shp in shapes:
        n = 1
        for s in shp:
            n *= s
        out.append(flat[off:off + n].reshape(shp))
        off += n
    return out


def _flat_rows(shapes, cols=1024):
    n = sum(functools.reduce(lambda a, b: a * b, shp, 1) for shp in shapes)
    rows = -(-n // cols)
    return -(-rows // 64) * 64


def _block_diag(w):
    eye = jnp.eye(N_HEADS, dtype=w.dtype)
    return (w[:, :, :, None, :] * eye[None, :, None, :, None]).reshape(w.shape[0], W_GRP, W_GRP)


def _diag_blocks(w):
    w5 = w.reshape(w.shape[0], N_HEADS, HEAD_DIM, N_HEADS, HEAD_DIM)
    return jnp.stack([w5[:, h, :, h, :] for h in range(N_HEADS)], axis=1)


def _pack_big_shards(w):
    rows = []
    for l in range(DEPTH):
        rows += [w['w_in'][l].T, w['w_out'][l], w['w_up'][l].T, w['w_down'][l],
                 w['w_pe'][l].T.reshape(32, 1024), w['w_pg'][l]]
    return jnp.concatenate(rows, axis=0)


def _unpack_big_full(g, l):
    out, off = {}, l * LAYER_ROWS
    for nm, r in SLAB_ROWS:
        blk = g[:, off:off + r, :]
        if nm == 'w_pe':
            out[nm] = blk.reshape(N_DEV, 128, PLE_DIM).reshape(N_DEV * 128, PLE_DIM)
        elif nm == 'w_up':
            out['w_up_g'] = blk[:N_DEV // 2].reshape(D_FF, D_MODEL)
            out['w_up_v'] = blk[N_DEV // 2:].reshape(D_FF, D_MODEL)
        else:
            out[nm] = blk.reshape(N_DEV * r, D_MODEL)
        off += r
    return out


def _pack_big_grads(gl):
    cols = []
    for l in range(DEPTH):
        g = gl[l]
        cols += [g['w_in'].reshape(N_DEV, 288, D_MODEL), g['w_out'].reshape(N_DEV, 128, D_MODEL),
                 jnp.concatenate([g['w_up_g'], g['w_up_v']], axis=0).reshape(N_DEV, 704, D_MODEL),
                 g['w_down'].reshape(N_DEV, 352, D_MODEL),
                 g['w_pe'].reshape(N_DEV, 128, PLE_DIM).reshape(N_DEV, 32, D_MODEL), g['w_pg'].reshape(N_DEV, 128, D_MODEL)]
    return jnp.concatenate(cols, axis=1)


def _unpack_big_shard(gs):
    out = {nm: [] for nm in BIG_NAMES}
    for l in range(DEPTH):
        off = l * LAYER_ROWS
        for nm, r in SLAB_ROWS:
            blk = gs[off:off + r]
            if nm in ('w_in', 'w_up'):
                blk = blk.T
            elif nm == 'w_pe':
                blk = blk.reshape(128, PLE_DIM).T
            out[nm].append(blk)
            off += r
    return {nm: jnp.stack(v) for nm, v in out.items()}


def _stacked_params(w, lbs):
    tril = jnp.tril(jnp.ones((GMLP_CHUNK, GMLP_CHUNK), bool))
    row = lambda a: a.reshape(DEPTH, 1, -1)
    return dict(
        g1=row(w['norm1_g']), g2=row(w['norm2_g']), g3=row(w['norm3_g']),
        a_ln_g=row(w['a_ln_g']), a_ln_b=row(w['a_ln_b']),
        a_wcat=jnp.where(tril, w['a_ws'], 0.0).reshape(DEPTH, N_HEADS * GMLP_CHUNK, GMLP_CHUNK),
        a_bfull=jnp.repeat(jnp.swapaxes(w['a_bs'], 1, 2), HEAD_DIM, axis=2),
        b_cw=w['b_conv_w_full'], b_cb=row(w['b_conv_b']), b_wa=_block_diag(w['b_wa']), b_ba=row(w['b_ba']),
        b_wx=_block_diag(w['b_wx']), b_bx=row(w['b_bx']), b_lam=row(w['b_lam']),
        c_lb=row(lbs), c_ngf=row(jnp.tile(w['c_norm_g'], (1, N_HEADS))),
        d_wd=_block_diag(w['d_w']), d_scale=row(w['d_scale']),
        f_cw=w['ffn_conv_w_full'], f_cb=row(w['ffn_conv_b']),
    )


B_PRM = ('b_cw', 'b_cb', 'b_wa', 'b_ba', 'b_wx', 'b_bx', 'b_lam')


def _layer_fwd(x, p_bf, wb, sp, l):
    n = lambda s: f"l{l}_{s}"
    h = _rms_fwd(x, sp['g1'], name=n("norm1"))
    z = _matmul(h, wb['w_in'], nt=True, name=n("proj_in"))
    ya = _gmlp_fwd(z, sp['a_ln_g'], sp['a_ln_b'], sp['a_wcat'], sp['a_bfull'], name=n("gmlp"))
    yb, h0s = _rglru_fwd(z, [sp[k] for k in B_PRM], name=n("rglru"))
    yc, sts = _hgrn_fwd(z, sp['c_lb'], sp['c_ngf'], name=n("hgrn"))
    yd = _pool_fwd(z, sp['d_wd'], sp['d_scale'], name=n("pool"))
    mix = jnp.concatenate([ya, yb, yc, yd], axis=1)
    x1 = _matmul(mix, wb['w_out'], res=x, name=n("proj_out"))
    h2 = _rms_fwd(x1, sp['g2'], name=n("norm2"))
    hg = _matmul(h2, wb['w_up_g'], nt=True, name=n("up_gate"))
    hv = _matmul(h2, wb['w_up_v'], nt=True, name=n("up_val"))
    a = _ffn_fwd(hg, hv, sp['f_cw'], sp['f_cb'], name=n("ffn_gate"))
    x2 = _matmul(a, wb['w_down'], res=x1, name=n("down"))
    h3 = _rms_fwd(x2, sp['g3'], name=n("norm3"))
    gl = _matmul(h3, wb['w_pg'], name=n("ple_gate"))
    pe = _matmul(p_bf, wb['w_pe'], nt=True, name=n("ple_emb"))
    x3 = _ple_fwd(x2, gl, pe, name=n("ple"))
    saved = dict(x=x, h=h, z=z, h0s=h0s, sts=sts, mix=mix, x1=x1, h2=h2, hg=hg, hv=hv, a=a, x2=x2, h3=h3, gl=gl, pe=pe)
    return x3, saved


def _layer_bwd(dx3, sv, p_bf, wb, sp, l):
    n = lambda s: f"l{l}_{s}_bwd"
    gb, gs = {}, {}
    dpe, dgl = _ple_bwd(dx3, sv['gl'], sv['pe'], name=n("ple"))
    gb['w_pe'] = _matmul_tn(dpe, p_bf, name=n("ple_emb_w"))
    gb['w_pg'] = _matmul_tn(sv['h3'], dgl, name=n("ple_gate_w"))
    dh3 = _matmul(dgl, wb['w_pg'], nt=True, name=n("ple_gate_x"))
    dx2, dx2b, gs['norm3_g'] = _rms_bwd(sv['x2'], sp['g3'], dh3, dx3, name=n("norm3"))
    da = _matmul(dx2b, wb['w_down'], nt=True, name=n("down_x"))
    gb['w_down'] = _matmul_tn(sv['a'], dx2b, name=n("down_w"))
    dhg, dhv, gs['f_dwg'], gs['f_dwv'] = _ffn_bwd(sv['hg'], sv['hv'], da, sp['f_cw'], sp['f_cb'], name=n("ffn_gate"))
    gb['w_up_g'] = _matmul_tn(dhg, sv['h2'], name=n("up_gate_w"))
    gb['w_up_v'] = _matmul_tn(dhv, sv['h2'], name=n("up_val_w"))
    dh2 = _matmul(dhg, wb['w_up_g'], name=n("up_gate_x"))
    dh2 = _matmul(dhv, wb['w_up_v'], res=dh2, name=n("up_val_x"))
    dx1, dx1b, gs['norm2_g'] = _rms_bwd(sv['x1'], sp['g2'], dh2, dx2, name=n("norm2"))
    dmix = _matmul(dx1b, wb['w_out'], nt=True, name=n("proj_out_x"))
    gb['w_out'] = _matmul_tn(sv['mix'], dx1b, name=n("proj_out_w"))
    z = sv['z']
    dzu, dzv, gs['a_ln_g'], gs['a_ln_b'], gs['a_wcat'], gs['a_bfull'] = _gmlp_bwd(
        z, dmix, sp['a_ln_g'], sp['a_ln_b'], sp['a_wcat'], sp['a_bfull'], name=n("gmlp"))
    dzb, dzg, *dbp = _rglru_bwd(z, dmix, sv['h0s'], [sp[k] for k in B_PRM], name=n("rglru"))
    gs.update(zip(B_PRM, dbp))
    dzc, gs['c_lb'], gs['c_ngf'] = _hgrn_bwd(z, dmix, sv['sts'], sp['c_lb'], sp['c_ngf'], name=n("hgrn"))
    dzd, gs['d_wd'], gs['d_scale'] = _pool_bwd(z, dmix, sp['d_wd'], sp['d_scale'], name=n("pool"))
    dz = jnp.concatenate([dzu, dzv, dzb, dzg, dzc, dzd], axis=1)
    gb['w_in'] = _matmul_tn(dz, sv['h'], name=n("proj_in_w"))
    dh = _matmul(dz, wb['w_in'], name=n("proj_in_x"))
    dx0, _, gs['norm1_g'] = _rms_bwd(sv['x'], sp['g1'], dh, dx1, name=n("norm1"))
    return dx0, gb, gs


SMALL_NAMES = [nm for nm in WEIGHT_NAMES if nm not in BIG_NAMES]
COL_SHARDED = ('w_in', 'w_up', 'w_pe')


def _comm_shards(w):
    return [(jnp.swapaxes(w[nm], 1, 2) if nm in COL_SHARDED else w[nm]).astype(BF16) for nm, _, _ in BIG_COMM]


def _full_weights(gathered):
    out = {nm: g.reshape(N_DEV * r, c) for g, (nm, r, c) in zip(gathered, BIG_COMM)}
    halves = out.pop('w_up').reshape(2, D_FF, D_MODEL)
    out['w_up_g'], out['w_up_v'] = _Sel(halves, 0), _Sel(halves, 1)
    return out


def _small_grads(raw):
    st = {k: jnp.stack([raw[l][k] for l in range(DEPTH)]) for k in raw[0]}
    tril = jnp.tril(jnp.ones((GMLP_CHUNK, GMLP_CHUNK), bool))
    vec = lambda a: a.reshape(DEPTH, -1)
    out = {nm: vec(st[k]) for nm, k in (('norm1_g', 'norm1_g'), ('norm2_g', 'norm2_g'), ('norm3_g', 'norm3_g'),
                                        ('a_ln_g', 'a_ln_g'), ('a_ln_b', 'a_ln_b'), ('b_conv_b', 'b_cb'),
                                        ('b_ba', 'b_ba'), ('b_bx', 'b_bx'), ('b_lam', 'b_lam'), ('c_lb', 'c_lb'),
                                        ('d_scale', 'd_scale'))}
    out['a_ws'] = jnp.where(tril, st['a_wcat'].reshape(DEPTH, N_HEADS, GMLP_CHUNK, GMLP_CHUNK), 0.0)
    out['a_bs'] = jnp.swapaxes(st['a_bfull'].reshape(DEPTH, GMLP_CHUNK, N_HEADS, HEAD_DIM).sum(-1), 1, 2)
    out['b_conv_w'] = st['b_cw']
    out['b_wa'], out['b_wx'], out['d_w'] = _diag_blocks(st['b_wa']), _diag_blocks(st['b_wx']), _diag_blocks(st['d_wd'])
    out['c_norm_g'] = st['c_ngf'].reshape(DEPTH, N_HEADS, HEAD_DIM).sum(1)
    out['ffn_conv_w'] = jnp.concatenate([st['f_dwg'][:, 0:3], st['f_dwv'][:, 0:3]], axis=2)
    out['ffn_conv_b'] = jnp.concatenate([st['f_dwg'][:, 3], st['f_dwv'][:, 3]], axis=1)
    return out


def _step(w, m, v, x, p, target):
    s = x.shape[1]
    dev = 4 * lax.axis_index("x") + 2 * lax.axis_index("y") + lax.axis_index("c")
    xs = x.reshape(s, D_MODEL)

    shards = _comm_shards(w)
    conv_shapes = [w['b_conv_w'].shape, w['ffn_conv_w'].shape]
    conv_rows = _flat_rows(conv_shapes)
    conv_all = _all_gather(_pack_flat([w['b_conv_w'], w['ffn_conv_w']], conv_rows), name="gather_conv_weights")
    parts = [_unpack_flat(conv_all[d], conv_shapes) for d in range(N_DEV)]
    wf = dict(w)
    wf['b_conv_w_full'] = jnp.concatenate([pt[0] for pt in parts], axis=-1)
    wf['ffn_conv_w_full'] = jnp.concatenate([pt[1] for pt in parts], axis=-1)
    lbs = _lbs_fwd(w['c_lb'], name="hgrn_bounds")

    stacked = _stacked_params(wf, lbs)
    p_all = p.reshape(DEPTH, s, PLE_DIM).astype(BF16)
    xl, saved, wbs, sps = xs, [], [], []
    for l in range(DEPTH):
        wb = _full_weights(_gather_layer(shards, l, name=f"l{l}_gather_weights"))
        sp = {k: _Sel(a, l) for k, a in stacked.items()}
        p_bf = p_all[l]
        xl, sv = _layer_fwd(xl, p_bf, wb, sp, l)
        saved.append((sv, p_bf))
        wbs.append(wb)
        sps.append(sp)
    loss_part, dx, dfinal = _loss_head(xl, w['final_g'].reshape(1, D_MODEL), target.reshape(s, D_MODEL), name="loss_head")
    loss = lax.psum(loss_part[0, 0], ("x", "y", "c"))

    reduced, small = [None] * DEPTH, [None] * DEPTH
    for l in range(DEPTH - 1, -1, -1):
        sv, p_bf = saved[l]
        dx, gb, small[l] = _layer_bwd(dx, sv, p_bf, wbs[l], sps[l], l)
        gb['w_up'] = jnp.concatenate([gb.pop('w_up_g'), gb.pop('w_up_v')], axis=0)
        reduced[l] = _reduce_layer([gb[nm] for nm, _, _ in BIG_COMM], l)
    grad_x = dx.reshape(1, s, D_MODEL)
    gbig = {}
    for a, (nm, _, _) in enumerate(BIG_COMM):
        g = jnp.stack([reduced[l][a] for l in range(DEPTH)])
        gbig[nm] = jnp.swapaxes(g, 1, 2) if nm in COL_SHARDED else g

    small_parts = _small_grads(small)
    small_parts['c_lb'] = _lbs_bwd(w['c_lb'], small_parts['c_lb'], name="hgrn_bounds_bwd")
    small_parts['final_g'] = dfinal.reshape(D_MODEL)
    small_shapes = [small_parts[nm].shape for nm in SMALL_NAMES]
    small_rows = _flat_rows(small_shapes)
    small_all = _all_gather(_pack_flat([small_parts[nm] for nm in SMALL_NAMES], small_rows), name="gather_small_grads")
    gsmall = dict(zip(SMALL_NAMES, _unpack_flat(_sum_slots(small_all, name="sum_small_grads"), small_shapes)))
    for nm in ('b_conv_w', 'ffn_conv_w'):
        width = w[nm].shape[-1]
        gsmall[nm] = lax.dynamic_slice_in_dim(gsmall[nm], dev * width, width, axis=2)

    grads, delta, new_m, new_v = {}, {}, {}, {}
    for nm in BIG_NAMES:
        grads[nm] = gbig[nm]
        delta[nm], new_m[nm], new_v[nm] = _adamw(w[nm], gbig[nm], m[nm], v[nm], name=f"adamw_{nm}")
    shapes = [w[nm].shape for nm in SMALL_NAMES]
    rows = _flat_rows(shapes)
    pk = lambda t: _pack_flat([t[nm] for nm in SMALL_NAMES], rows)
    d, nm_, nv_ = _adamw(pk(w), pk(gsmall), pk(m), pk(v), name="adamw_small")
    for nm, dd, mm_, vv_ in zip(SMALL_NAMES, _unpack_flat(d, shapes), _unpack_flat(nm_, shapes), _unpack_flat(nv_, shapes)):
        grads[nm], delta[nm], new_m[nm], new_v[nm] = gsmall[nm], dd, mm_, vv_

    return (loss, grad_x, *[grads[nm] for nm in WEIGHT_NAMES], *[delta[nm] for nm in WEIGHT_NAMES],
            *[new_m[nm] for nm in WEIGHT_NAMES], *[new_v[nm] for nm in WEIGHT_NAMES])


def kernel(x, p, norm1_g, w_in, a_ln_g, a_ln_b, a_ws, a_bs, b_conv_w, b_conv_b, b_wa, b_ba, b_wx, b_bx, b_lam, c_lb, c_norm_g, d_w, d_scale, w_out, norm2_g, w_up, ffn_conv_w, ffn_conv_b, w_down, norm3_g, w_pe, w_pg, final_g, loss_target, m_norm1_g, m_w_in, m_a_ln_g, m_a_ln_b, m_a_ws, m_a_bs, m_b_conv_w, m_b_conv_b, m_b_wa, m_b_ba, m_b_wx, m_b_bx, m_b_lam, m_c_lb, m_c_norm_g, m_d_w, m_d_scale, m_w_out, m_norm2_g, m_w_up, m_ffn_conv_w, m_ffn_conv_b, m_w_down, m_norm3_g, m_w_pe, m_w_pg, m_final_g, v_norm1_g, v_w_in, v_a_ln_g, v_a_ln_b, v_a_ws, v_a_bs, v_b_conv_w, v_b_conv_b, v_b_wa, v_b_ba, v_b_wx, v_b_bx, v_b_lam, v_c_lb, v_c_norm_g, v_d_w, v_d_scale, v_w_out, v_norm2_g, v_w_up, v_ffn_conv_w, v_ffn_conv_b, v_w_down, v_norm3_g, v_w_pe, v_w_pg, v_final_g):
    w = dict(norm1_g=norm1_g, w_in=w_in, a_ln_g=a_ln_g, a_ln_b=a_ln_b, a_ws=a_ws, a_bs=a_bs, b_conv_w=b_conv_w, b_conv_b=b_conv_b, b_wa=b_wa, b_ba=b_ba, b_wx=b_wx, b_bx=b_bx, b_lam=b_lam, c_lb=c_lb, c_norm_g=c_norm_g, d_w=d_w, d_scale=d_scale, w_out=w_out, norm2_g=norm2_g, w_up=w_up, ffn_conv_w=ffn_conv_w, ffn_conv_b=ffn_conv_b, w_down=w_down, norm3_g=norm3_g, w_pe=w_pe, w_pg=w_pg, final_g=final_g)
    m = dict(norm1_g=m_norm1_g, w_in=m_w_in, a_ln_g=m_a_ln_g, a_ln_b=m_a_ln_b, a_ws=m_a_ws, a_bs=m_a_bs, b_conv_w=m_b_conv_w, b_conv_b=m_b_conv_b, b_wa=m_b_wa, b_ba=m_b_ba, b_wx=m_b_wx, b_bx=m_b_bx, b_lam=m_b_lam, c_lb=m_c_lb, c_norm_g=m_c_norm_g, d_w=m_d_w, d_scale=m_d_scale, w_out=m_w_out, norm2_g=m_norm2_g, w_up=m_w_up, ffn_conv_w=m_ffn_conv_w, ffn_conv_b=m_ffn_conv_b, w_down=m_w_down, norm3_g=m_norm3_g, w_pe=m_w_pe, w_pg=m_w_pg, final_g=m_final_g)
    v = dict(norm1_g=v_norm1_g, w_in=v_w_in, a_ln_g=v_a_ln_g, a_ln_b=v_a_ln_b, a_ws=v_a_ws, a_bs=v_a_bs, b_conv_w=v_b_conv_w, b_conv_b=v_b_conv_b, b_wa=v_b_wa, b_ba=v_b_ba, b_wx=v_b_wx, b_bx=v_b_bx, b_lam=v_b_lam, c_lb=v_c_lb, c_norm_g=v_c_norm_g, d_w=v_d_w, d_scale=v_d_scale, w_out=v_w_out, norm2_g=v_norm2_g, w_up=v_w_up, ffn_conv_w=v_ffn_conv_w, ffn_conv_b=v_ffn_conv_b, w_down=v_w_down, norm3_g=v_norm3_g, w_pe=v_w_pe, w_pg=v_w_pg, final_g=v_final_g)
    return _step(w, m, v, x, p, loss_target)
```

```python
import functools

import jax
import jax.numpy as jnp
from jax import lax
from jax.experimental import pallas as pl
from jax.experimental.pallas import tpu as pltpu

F32 = jnp.float32
BF16 = jnp.bfloat16
MESH = pl.DeviceIdType.MESH

D_MODEL = 1024
DEPTH = 4
PLE_DIM = 256
W_GRP = 256
N_HEADS = 4
HEAD_DIM = 64
GMLP_CHUNK = 128
RGLRU_C = 8.0
HGRN_CHUNK = 64
HGRN_SUB = 16
POOL_WINDOWS = (2, 4, 8, 16)
D_FF = 2816
D_PROJ = 2304
EPS = 1e-6
ADAM_LR = 0.001
ADAM_B1 = 0.9
ADAM_B2 = 0.999
ADAM_EPS = 1e-08
ADAM_WD = 0.01
ADAM_STEP = 10

N_DEV = 8
MIB = 2 ** 20
V7X_VMEM_BYTES = 64 * MIB
HGRN_EXP_CLAMP = 60.0

WEIGHT_NAMES = ['norm1_g', 'w_in', 'a_ln_g', 'a_ln_b', 'a_ws', 'a_bs', 'b_conv_w', 'b_conv_b', 'b_wa', 'b_ba', 'b_wx',
                'b_bx', 'b_lam', 'c_lb', 'c_norm_g', 'd_w', 'd_scale', 'w_out', 'norm2_g', 'w_up', 'ffn_conv_w',
                'ffn_conv_b', 'w_down', 'norm3_g', 'w_pe', 'w_pg', 'final_g']
BIG_NAMES = ('w_in', 'w_out', 'w_up', 'w_down', 'w_pe', 'w_pg')
SLAB_ROWS = (('w_in', 288), ('w_out', 128), ('w_up', 704), ('w_down', 352), ('w_pe', 32), ('w_pg', 128))
LAYER_ROWS = sum(r for _, r in SLAB_ROWS)
PACK_ROWS = DEPTH * LAYER_ROWS


def _vmem_limit(block_bytes):
    want = 2 * block_bytes + 24 * MIB
    return int(min(max(want, 32 * MIB), V7X_VMEM_BYTES - 8 * MIB))


def _pcall(body, *, name, out_shape, grid=None, in_specs=None, out_specs=None, scratch_shapes=(),
           semantics=None, block_bytes=0):
    kw = {}
    if grid is not None:
        kw["grid"] = grid
    if in_specs is not None:
        kw["in_specs"] = in_specs
    if out_specs is not None:
        kw["out_specs"] = out_specs
    params = pltpu.CompilerParams(dimension_semantics=semantics, vmem_limit_bytes=_vmem_limit(block_bytes))
    return pl.pallas_call(body, name=name, out_shape=out_shape, scratch_shapes=list(scratch_shapes),
                          compiler_params=params, **kw)


def _pick(n, cands):
    for c in cands:
        if n % c == 0:
            return c
    return n


def _nbytes(shape, dtype):
    n = 1
    for s in shape:
        n *= s
    return n * jnp.dtype(dtype).itemsize


def _sds(shape, dtype):
    return jax.ShapeDtypeStruct(tuple(shape), dtype)


class _Sel:
    def __init__(self, arr, *idx):
        self.arr, self.idx = arr, tuple(idx)
        self.shape = arr.shape[len(idx):]
        self.ndim = len(self.shape)
        self.dtype = arr.dtype


def _arr(a):
    return a.arr if isinstance(a, _Sel) else a


def _spec(a, block=None, index=None):
    block = tuple(a.shape) if block is None else tuple(block)
    index = (lambda *g: (0,) * len(block)) if index is None else index
    if isinstance(a, _Sel):
        lead = a.idx
        return pl.BlockSpec((None,) * len(lead) + block, lambda *g: lead + tuple(index(*g)))
    return pl.BlockSpec(block, lambda *g: tuple(index(*g)))


def _ospec(a):
    return pl.BlockSpec(tuple(a.shape), lambda *g: (0,) * a.ndim)


def _rows_of(shape):
    return lax.broadcasted_iota(jnp.int32, shape, 0)


def _lanes_of(shape):
    return lax.broadcasted_iota(jnp.int32, shape, 1)


def _sdn(x, k, fill):
    n = x.shape[0]
    return jnp.where(_rows_of(x.shape) >= k, pltpu.roll(x, k % n, 0), fill)


def _sup(x, k, fill):
    n = x.shape[0]
    return jnp.where(_rows_of(x.shape) < n - k, pltpu.roll(x, (n - k) % n, 0), fill)


@functools.partial(jax.custom_vjp, nondiff_argnums=(1,))
def _shift_dn(x, k):
    return _sdn(x, k, 0.0)


def _shift_dn_fwd(x, k):
    return _sdn(x, k, 0.0), None


def _shift_dn_bwd(k, _, g):
    return (_sup(g, k, 0.0),)


_shift_dn.defvjp(_shift_dn_fwd, _shift_dn_bwd)


def _lin_scan_impl(a, b, h0):
    n = a.shape[0]
    aa, bb = a, b
    k = 1
    while k < n:
        bb = aa * _sdn(bb, k, 0.0) + bb
        aa = aa * _sdn(aa, k, 1.0)
        k *= 2
    return bb + aa * h0


@jax.custom_vjp
def _lin_scan(a, b, h0):
    return _lin_scan_impl(a, b, h0)


def _lin_scan_fwd(a, b, h0):
    h = _lin_scan_impl(a, b, h0)
    return h, (a, h, h0)


def _lin_scan_bwd(res, g):
    a, h, h0 = res
    n = a.shape[0]
    cc, gg = _sup(a, 1, 0.0), g
    k = 1
    while k < n:
        gg = gg + cc * _sup(gg, k, 0.0)
        cc = cc * _sup(cc, k, 1.0)
        k *= 2
    first = _rows_of(a.shape) == 0
    hprev = jnp.where(first, h0, _sdn(h, 1, 0.0))
    dh0 = jnp.sum(jnp.where(first, a * gg, 0.0), axis=0, keepdims=True)
    return gg * hprev, gg, dh0


_lin_scan.defvjp(_lin_scan_fwd, _lin_scan_bwd)


def _cumsum_sub_impl(x):
    pos = _rows_of(x.shape) % HGRN_SUB
    k = 1
    while k < HGRN_SUB:
        x = x + jnp.where(pos >= k, pltpu.roll(x, k, 0), 0.0)
        k *= 2
    return x


@jax.custom_vjp
def _cumsum_sub(x):
    return _cumsum_sub_impl(x)


def _cumsum_sub_fwd(x):
    return _cumsum_sub_impl(x), None


def _cumsum_sub_bwd(_, g):
    n = g.shape[0]
    pos = _rows_of(g.shape) % HGRN_SUB
    k = 1
    while k < HGRN_SUB:
        g = g + jnp.where(pos < HGRN_SUB - k, pltpu.roll(g, n - k, 0), 0.0)
        k *= 2
    return (g,)


_cumsum_sub.defvjp(_cumsum_sub_fwd, _cumsum_sub_bwd)


def _dot(a, b, ca, cb):
    return lax.dot_general(a.astype(BF16), b.astype(BF16), (((ca,), (cb,)), ((), ())), preferred_element_type=F32)


@jax.custom_vjp
def _mm(a, b):
    return _dot(a, b, 1, 0)


def _mm_fwd(a, b):
    return _dot(a, b, 1, 0), (a, b)


def _mm_bwd(res, g):
    a, b = res
    return _dot(g, b, 1, 1), _dot(a, g, 0, 0)


_mm.defvjp(_mm_fwd, _mm_bwd)


@jax.custom_vjp
def _mm_nt(a, b):
    return _dot(a, b, 1, 1)


def _mm_nt_fwd(a, b):
    return _dot(a, b, 1, 1), (a, b)


def _mm_nt_bwd(res, g):
    a, b = res
    return _dot(g, b, 1, 0), _dot(g, a, 0, 0)


_mm_nt.defvjp(_mm_nt_fwd, _mm_nt_bwd)


@jax.custom_vjp
def _mm_tn(a, b):
    return _dot(a, b, 0, 0)


def _mm_tn_fwd(a, b):
    return _dot(a, b, 0, 0), (a, b)


def _mm_tn_bwd(res, g):
    a, b = res
    return _dot(b, g, 1, 1), _dot(a, g, 1, 0)


_mm_tn.defvjp(_mm_tn_fwd, _mm_tn_bwd)


def _head_mask(shape, h):
    return (_lanes_of(shape) // HEAD_DIM) == h


def _stack_heads(x):
    return jnp.concatenate([jnp.where(_head_mask(x.shape, h), x, 0.0) for h in range(N_HEADS)], axis=0)


def _unstack_heads(p):
    r = p.shape[0] // N_HEADS
    out = None
    for h in range(N_HEADS):
        blk = p[h * r:(h + 1) * r]
        term = jnp.where(_head_mask(blk.shape, h), blk, 0.0)
        out = term if out is None else out + term
    return out


def _segmean_impl(x):
    n = x.shape[1]
    same = (lax.broadcasted_iota(jnp.int32, (n, n), 0) // HEAD_DIM) == (lax.broadcasted_iota(jnp.int32, (n, n), 1) // HEAD_DIM)
    m = jnp.where(same, 1.0 / HEAD_DIM, 0.0).astype(BF16)
    hi = x.astype(BF16)
    lo = (x - hi.astype(F32)).astype(BF16)
    dn = (((1,), (0,)), ((), ()))
    return (lax.dot_general(hi, m, dn, preferred_element_type=F32)
            + lax.dot_general(lo, m, dn, preferred_element_type=F32))


@jax.custom_vjp
def _segmean(x):
    return _segmean_impl(x)


def _segmean_fwd(x):
    return _segmean_impl(x), None


def _segmean_bwd(_, g):
    return (_segmean_impl(g),)


_segmean.defvjp(_segmean_fwd, _segmean_bwd)


def _log1p(u):
    w = 1.0 + u
    return jnp.where(w == 1.0, u, jnp.log(w) * (u / (w - 1.0)))


def _softplus(y):
    return jnp.maximum(y, 0.0) + _log1p(jnp.exp(-jnp.abs(y)))


def _rms(x, g):
    return x * lax.rsqrt(jnp.mean(x * x, axis=-1, keepdims=True) + EPS) * g


def _gmlp_chunk(zu, zv, ln_g, ln_b, wcat, bfull):
    u = jax.nn.gelu(zu)
    v = jax.nn.gelu(zv)
    mu = jnp.mean(v, axis=-1, keepdims=True)
    var = jnp.mean(jnp.square(v - mu), axis=-1, keepdims=True)
    vn = (v - mu) * lax.rsqrt(var + EPS) * ln_g + ln_b
    sv = _unstack_heads(_mm(wcat, vn)) + bfull
    return u * sv


def _rglru_tile(xb_ext, gb, h0, cw, cb, wa, ba, wx, bx, lam):
    xc = (cb + cw[0:1] * _shift_dn(xb_ext, 3) + cw[1:2] * _shift_dn(xb_ext, 2) + cw[2:3] * _shift_dn(xb_ext, 1)
          + cw[3:4] * xb_ext)[8:]
    r = jax.nn.sigmoid(_mm(xc, wa) + ba)
    i = jax.nn.sigmoid(_mm(xc, wx) + bx)
    log_a = (-RGLRU_C) * r * _softplus(-lam)
    a = jnp.exp(log_a)
    mult = jnp.sqrt(-jnp.tanh(log_a) * (a * a + 1.0))
    h = _lin_scan(a, mult * (i * xc), h0)
    y = h * jax.nn.gelu(gb)
    h_last = jnp.sum(jnp.where(_rows_of(h.shape) == h.shape[0] - 1, h, 0.0), axis=0, keepdims=True)
    return y, h_last


def _pool_tile(xd_ext, inv, wd, scale):
    s1 = xd_ext + _shift_dn(xd_ext, 1)
    s2 = s1 + _shift_dn(s1, 2)
    s3 = s2 + _shift_dn(s2, 4)
    s4 = s3 + _shift_dn(s3, 8)
    grp = _lanes_of(xd_ext.shape) // HEAD_DIM
    win = jnp.where(grp == 0, s1, jnp.where(grp == 1, s2, jnp.where(grp == 2, s3, s4)))
    pooled = win[16:] * inv - xd_ext[16:]
    return _mm(pooled, wd) * scale


def _hgrn_chunk(q, f, i, g, st, lb, ngf):
    n = q.shape[0]
    nsub = n // HGRN_SUB
    qs = jax.nn.silu(q)
    fg = lb + (1.0 - lb) * jax.nn.sigmoid(f)
    lf = jnp.log(fg)
    k = 1.0 - fg
    bl = _cumsum_sub(lf)
    row = _rows_of(q.shape)
    blk = row // HGRN_SUB
    betas = [jnp.zeros_like(lb)]
    for s in range(nsub):
        tot = jnp.sum(jnp.where(row == s * HGRN_SUB + HGRN_SUB - 1, bl, 0.0), axis=0, keepdims=True)
        betas.append(betas[-1] + tot)
    b_end = betas[nsub]
    beta_full = jnp.zeros_like(q)
    for s in range(1, nsub):
        beta_full = jnp.where(blk == s, betas[s], beta_full)
    qh = qs * jnp.exp(bl)
    qt = qh * jnp.exp(beta_full)
    b_all = beta_full + bl
    kt = k * jnp.exp(b_end - b_all)
    outs = []
    for s in range(nsub):
        kh = k * jnp.exp(jnp.minimum(betas[s] - b_all, HGRN_EXP_CLAMP))
        qstk = _stack_heads(qh[s * HGRN_SUB:(s + 1) * HGRN_SUB])
        att = _mm_nt(qstk, kh)
        ar = _rows_of(att.shape) % HGRN_SUB + s * HGRN_SUB
        att = jnp.where(_lanes_of(att.shape) <= ar, att, 0.0)
        outs.append(_unstack_heads(_mm(att, i)))
    o = jnp.concatenate(outs, axis=0) + _mm_nt(qt, st)
    same = (_rows_of(st.shape) // HEAD_DIM) == (_lanes_of(st.shape) // HEAD_DIM)
    st_new = st * jnp.exp(b_end) + jnp.where(same, _mm_tn(i, kt), 0.0)
    on = o * lax.rsqrt(_segmean(o * o) + EPS) * ngf
    return on * jax.nn.silu(g), st_new


def _ffn_tile(eg, ev, wg, bg, wv, bv):
    gt = (bg + wg[0:1] * _shift_dn(eg, 2) + wg[1:2] * _shift_dn(eg, 1) + wg[2:3] * eg)[8:]
    val = (bv + wv[0:1] * _shift_dn(ev, 2) + wv[1:2] * _shift_dn(ev, 1) + wv[2:3] * ev)[8:]
    return jax.nn.gelu(gt) * val


MXU_WIDTH = 256
MATMUL_BLOCK_BUDGET = 18 * MIB


def _matmul_tiles(m, k, n, a_dtype, b_dtype, out_dtype, has_res):
    best = None
    for tm in (2048, 1024, 512, 256):
        if m % tm:
            continue
        for tn in (1024, 768, 1408, 512, 256, 128):
            if n % tn:
                continue
            blk = (_nbytes((tm, k), a_dtype) + _nbytes((k, tn), b_dtype) + _nbytes((tm, tn), out_dtype)
                   + (_nbytes((tm, tn), F32) if has_res else 0))
            if blk > MATMUL_BLOCK_BUDGET:
                continue
            waste = -(-tn // MXU_WIDTH) * MXU_WIDTH / tn
            cost = (m // tm) * (n // tn) + 64 * (waste - 1.0)
            if best is None or cost < best[0]:
                best = (cost, tm, tn, blk)
    assert best is not None, (m, k, n)
    return best[1:]


def _matmul(a, b, *, name, nt=False, res=None, out_dtype=F32):
    m, k = a.shape
    n = b.shape[0] if nt else b.shape[1]
    tm, tn, blk = _matmul_tiles(m, k, n, a.dtype, b.dtype, out_dtype, res is not None)
    dims = (((1,), (1,)), ((), ())) if nt else (((1,), (0,)), ((), ()))

    def body(*refs):
        if res is None:
            a_ref, b_ref, o_ref = refs
        else:
            a_ref, b_ref, r_ref, o_ref = refs
        acc = lax.dot_general(a_ref[...], b_ref[...], dims, preferred_element_type=F32)
        if res is not None:
            acc = acc + r_ref[...]
        o_ref[...] = acc.astype(out_dtype)

    in_specs = [pl.BlockSpec((tm, k), lambda i, j: (i, 0)),
                _spec(b, (tn, k), lambda i, j: (j, 0)) if nt else _spec(b, (k, tn), lambda i, j: (0, j))]
    args = [a, _arr(b)]
    if res is not None:
        in_specs.append(pl.BlockSpec((tm, tn), lambda i, j: (i, j)))
        args.append(res)
    return _pcall(body, name=name, out_shape=_sds((m, n), out_dtype), grid=(m // tm, n // tn), in_specs=in_specs,
                  out_specs=pl.BlockSpec((tm, tn), lambda i, j: (i, j)), semantics=("parallel", "parallel"),
                  block_bytes=blk + _nbytes((tm, tn), F32))(*args)


def _matmul_tn(a, b, *, name, out_dtype=BF16):
    m, k1 = a.shape
    n = b.shape[1]
    tk = _pick(k1, (512, 256, 128))

    def body(a_ref, b_ref, o_ref):
        o_ref[...] = lax.dot_general(a_ref[...], b_ref[...], (((0,), (0,)), ((), ())),
                                     preferred_element_type=F32).astype(out_dtype)

    blk = 2 * _nbytes((m, tk), a.dtype) + _nbytes((m, n), b.dtype) + _nbytes((tk, n), F32)
    return _pcall(body, name=name, out_shape=_sds((k1, n), out_dtype), grid=(k1 // tk,),
                  in_specs=[pl.BlockSpec((m, tk), lambda i: (0, i)), pl.BlockSpec((m, n), lambda i: (0, 0))],
                  out_specs=pl.BlockSpec((tk, n), lambda i: (i, 0)), semantics=("parallel",),
                  block_bytes=blk)(a, b)


def _rms_fwd(x, g, *, name):
    s, d = x.shape
    tm = _pick(s, (512, 256))

    def body(x_ref, g_ref, o_ref):
        o_ref[...] = _rms(x_ref[...], g_ref[...]).astype(BF16)

    return _pcall(body, name=name, out_shape=_sds((s, d), BF16), grid=(s // tm,),
                  in_specs=[pl.BlockSpec((tm, d), lambda i: (i, 0)), _spec(g)],
                  out_specs=pl.BlockSpec((tm, d), lambda i: (i, 0)), semantics=("parallel",),
                  block_bytes=3 * _nbytes((tm, d), F32))(x, _arr(g))


def _rms_bwd(x, g, dh, dres, *, name):
    s, d = x.shape
    tm = _pick(s, (256, 128))

    def body(x_ref, g_ref, dh_ref, dr_ref, dx_ref, dxb_ref, dg_ref):
        _, vjp = jax.vjp(_rms, x_ref[...], g_ref[...])
        dxn, dg = vjp(dh_ref[...])
        dx = dr_ref[...] + dxn
        dx_ref[...] = dx
        dxb_ref[...] = dx.astype(BF16)

        @pl.when(pl.program_id(0) == 0)
        def _():
            dg_ref[...] = jnp.zeros_like(dg_ref)

        dg_ref[...] += dg

    row = pl.BlockSpec((tm, d), lambda i: (i, 0))
    vec = pl.BlockSpec((1, d), lambda i: (0, 0))
    return _pcall(body, name=name, out_shape=(_sds((s, d), F32), _sds((s, d), BF16), _sds((1, d), F32)),
                  grid=(s // tm,), in_specs=[row, _spec(g), row, row], out_specs=(row, row, vec),
                  semantics=("arbitrary",), block_bytes=8 * _nbytes((tm, d), F32))(x, _arr(g), dh, dres)


def _ple_fwd(x, gl, pe, *, name):
    s, d = x.shape
    tm = _pick(s, (512, 256))

    def body(x_ref, gl_ref, pe_ref, o_ref):
        o_ref[...] = x_ref[...] + pe_ref[...] * jax.nn.sigmoid(gl_ref[...])

    row = pl.BlockSpec((tm, d), lambda i: (i, 0))
    return _pcall(body, name=name, out_shape=_sds((s, d), F32), grid=(s // tm,), in_specs=[row, row, row],
                  out_specs=row, semantics=("parallel",), block_bytes=4 * _nbytes((tm, d), F32))(x, gl, pe)


def _ple_bwd(dx, gl, pe, *, name):
    s, d = dx.shape
    tm = _pick(s, (512, 256))

    def body(dx_ref, gl_ref, pe_ref, dpe_ref, dgl_ref):
        gate = jax.nn.sigmoid(gl_ref[...])
        dxv = dx_ref[...]
        dpe_ref[...] = (dxv * gate).astype(BF16)
        dgl_ref[...] = (dxv * pe_ref[...] * gate * (1.0 - gate)).astype(BF16)

    row = pl.BlockSpec((tm, d), lambda i: (i, 0))
    return _pcall(body, name=name, out_shape=(_sds((s, d), BF16), _sds((s, d), BF16)), grid=(s // tm,),
                  in_specs=[row, row, row], out_specs=(row, row), semantics=("parallel",),
                  block_bytes=5 * _nbytes((tm, d), F32))(dx, gl, pe)


def _loss_head(x, g, target, *, name):
    s, d = x.shape
    tm = _pick(s, (256, 128))

    def tile_loss(xv, gv, tv):
        err = jnp.square(_rms(xv, gv) - tv)
        return 0.5 * jnp.sum(jnp.mean(err, axis=-1, keepdims=True), axis=0, keepdims=True)

    def body(x_ref, g_ref, t_ref, l_ref, dx_ref, dg_ref):
        lv, vjp = jax.vjp(tile_loss, x_ref[...], g_ref[...], t_ref[...])
        dxv, dgv, _ = vjp(jnp.ones((1, 1), F32))
        dx_ref[...] = dxv

        @pl.when(pl.program_id(0) == 0)
        def _():
            l_ref[...] = jnp.zeros_like(l_ref)
            dg_ref[...] = jnp.zeros_like(dg_ref)

        l_ref[...] += jnp.broadcast_to(lv, l_ref.shape)
        dg_ref[...] += dgv

    row = pl.BlockSpec((tm, d), lambda i: (i, 0))
    vec = pl.BlockSpec((1, d), lambda i: (0, 0))
    return _pcall(body, name=name, out_shape=(_sds((8, 128), F32), _sds((s, d), F32), _sds((1, d), F32)),
                  grid=(s // tm,), in_specs=[row, vec, row],
                  out_specs=(pl.BlockSpec((8, 128), lambda i: (0, 0)), row, vec), semantics=("arbitrary",),
                  block_bytes=8 * _nbytes((tm, d), F32))(x, g, target)


def _acc_out(ref, val, first):
    @pl.when(first)
    def _():
        ref[...] = jnp.zeros_like(ref)

    ref[...] += val


def _gmlp_fwd(z, ln_g, ln_b, wcat, bfull, *, name):
    s = z.shape[0]
    t = _pick(s, (512, 256, 128))
    nch = t // GMLP_CHUNK

    def body(zu_ref, zv_ref, g_ref, b_ref, w_ref, bf_ref, o_ref):
        for c in range(nch):
            rows = pl.ds(c * GMLP_CHUNK, GMLP_CHUNK)
            o_ref[rows, :] = _gmlp_chunk(zu_ref[rows, :], zv_ref[rows, :], g_ref[...], b_ref[...], w_ref[...],
                                         bf_ref[...]).astype(BF16)

    col = lambda c: pl.BlockSpec((t, W_GRP), lambda i: (i, c))
    params = (ln_g, ln_b, wcat, bfull)
    return _pcall(body, name=name, out_shape=_sds((s, W_GRP), BF16), grid=(s // t,),
                  in_specs=[col(0), col(1)] + [_spec(a) for a in params],
                  out_specs=pl.BlockSpec((t, W_GRP), lambda i: (i, 0)), semantics=("parallel",),
                  block_bytes=4 * _nbytes((t, W_GRP), F32))(z, z, *[_arr(a) for a in params])


def _gmlp_bwd(z, dmix, ln_g, ln_b, wcat, bfull, *, name):
    s = z.shape[0]
    t = _pick(s, (512, 256, 128))
    nch = t // GMLP_CHUNK

    def body(zu_ref, zv_ref, dy_ref, g_ref, b_ref, w_ref, bf_ref, du_ref, dv_ref, dg_ref, db_ref, dw_ref, dbf_ref):
        acc = None
        for c in range(nch):
            rows = pl.ds(c * GMLP_CHUNK, GMLP_CHUNK)
            _, vjp = jax.vjp(_gmlp_chunk, zu_ref[rows, :], zv_ref[rows, :], g_ref[...], b_ref[...], w_ref[...],
                             bf_ref[...])
            du, dv, *dps = vjp(dy_ref[rows, :])
            du_ref[rows, :] = du.astype(BF16)
            dv_ref[rows, :] = dv.astype(BF16)
            acc = dps if acc is None else [x + y for x, y in zip(acc, dps)]
        first = pl.program_id(0) == 0
        for ref, val in zip((dg_ref, db_ref, dw_ref, dbf_ref), acc):
            _acc_out(ref, val, first)

    col = lambda c: pl.BlockSpec((t, W_GRP), lambda i: (i, c))
    params = (ln_g, ln_b, wcat, bfull)
    return _pcall(body, name=name,
                  out_shape=(_sds((s, W_GRP), BF16), _sds((s, W_GRP), BF16)) + tuple(_sds(a.shape, F32) for a in params),
                  grid=(s // t,), in_specs=[col(0), col(1), col(0)] + [_spec(a) for a in params],
                  out_specs=(col(0), col(0)) + tuple(_ospec(a) for a in params), semantics=("arbitrary",),
                  block_bytes=8 * _nbytes((t, W_GRP), F32))(z, z, dmix, *[_arr(a) for a in params])


def _rglru_fwd(z, prm, *, name):
    s = z.shape[0]
    t = _pick(s, (512, 256, 128))
    nt = s // t

    def body(xb_ref, halo_ref, gb_ref, *rest):
        prm_refs, (y_ref, h0s_ref, h_scr) = rest[:len(prm)], rest[len(prm):]
        i = pl.program_id(0)

        @pl.when(i == 0)
        def _():
            h_scr[...] = jnp.zeros_like(h_scr)

        halo = jnp.where(i == 0, 0.0, halo_ref[...])
        h0 = h_scr[...]
        y, h_last = _rglru_tile(jnp.concatenate([halo, xb_ref[...]], axis=0), gb_ref[...], h0,
                                *[r[...] for r in prm_refs])
        y_ref[...] = y.astype(BF16)
        h0s_ref[...] = jnp.broadcast_to(h0, h0s_ref.shape)
        h_scr[...] = h_last

    in_specs = [pl.BlockSpec((t, W_GRP), lambda i: (i, 2)),
                pl.BlockSpec((8, W_GRP), lambda i: (jnp.maximum(i * (t // 8) - 1, 0), 2)),
                pl.BlockSpec((t, W_GRP), lambda i: (i, 3))] + [_spec(a) for a in prm]
    return _pcall(body, name=name, out_shape=(_sds((s, W_GRP), BF16), _sds((nt, 8, W_GRP), F32)), grid=(nt,),
                  in_specs=in_specs,
                  out_specs=(pl.BlockSpec((t, W_GRP), lambda i: (i, 0)), pl.BlockSpec((None, 8, W_GRP), lambda i: (i, 0, 0))),
                  scratch_shapes=[pltpu.VMEM((1, W_GRP), F32)], semantics=("arbitrary",),
                  block_bytes=24 * _nbytes((t, W_GRP), F32))(z, z, z, *[_arr(a) for a in prm])


def _rglru_bwd(z, dmix, h0s, prm, *, name):
    s = z.shape[0]
    t = _pick(s, (512, 256, 128))
    nt = s // t
    npm = len(prm)

    def body(xb_ref, halo_ref, gb_ref, dy_ref, h0s_ref, *rest):
        prm_refs = rest[:npm]
        dxb_ref, dgb_ref = rest[npm:npm + 2]
        dprm_refs = rest[npm + 2:2 * npm + 2]
        dh_scr, dhalo_scr = rest[2 * npm + 2:]
        i = pl.program_id(0)
        r = nt - 1 - i

        @pl.when(i == 0)
        def _():
            dh_scr[...] = jnp.zeros_like(dh_scr)
            dhalo_scr[...] = jnp.zeros_like(dhalo_scr)

        halo = jnp.where(r == 0, 0.0, halo_ref[...])
        h0 = h0s_ref[0:1, :]
        _, vjp = jax.vjp(_rglru_tile, jnp.concatenate([halo, xb_ref[...]], axis=0), gb_ref[...], h0,
                         *[p[...] for p in prm_refs])
        dext, dgb, _dh0, *dps = vjp((dy_ref[...], dh_scr[...]))
        dmain = dext[8:]
        dxb = jnp.concatenate([dmain[:t - 8], dmain[t - 8:] + dhalo_scr[...]], axis=0)
        dxb_ref[...] = dxb.astype(BF16)
        dgb_ref[...] = dgb.astype(BF16)
        dh_scr[...] = _dh0
        dhalo_scr[...] = dext[:8]
        for ref, val in zip(dprm_refs, dps):
            _acc_out(ref, val, i == 0)

    rev = lambda c: pl.BlockSpec((t, W_GRP), lambda i: (nt - 1 - i, c))
    in_specs = [rev(2), pl.BlockSpec((8, W_GRP), lambda i: (jnp.maximum((nt - 1 - i) * (t // 8) - 1, 0), 2)), rev(3),
                rev(1), pl.BlockSpec((None, 8, W_GRP), lambda i: (nt - 1 - i, 0, 0))] + [_spec(a) for a in prm]
    return _pcall(body, name=name,
                  out_shape=(_sds((s, W_GRP), BF16), _sds((s, W_GRP), BF16)) + tuple(_sds(a.shape, F32) for a in prm),
                  grid=(nt,), in_specs=in_specs, out_specs=(rev(0), rev(0)) + tuple(_ospec(a) for a in prm),
                  scratch_shapes=[pltpu.VMEM((1, W_GRP), F32), pltpu.VMEM((8, W_GRP), F32)],
                  semantics=("arbitrary",), block_bytes=40 * _nbytes((t, W_GRP), F32))(z, z, z, dmix, h0s, *[_arr(a) for a in prm])


def _pool_inv(i, t):
    pos = (_rows_of((t, W_GRP)) + i * t + 1).astype(F32)
    grp = _lanes_of((t, W_GRP)) // HEAD_DIM
    win = jnp.where(grp == 0, float(POOL_WINDOWS[0]), jnp.where(grp == 1, float(POOL_WINDOWS[1]),
                    jnp.where(grp == 2, float(POOL_WINDOWS[2]), float(POOL_WINDOWS[3]))))
    return 1.0 / jnp.minimum(pos, win)


def _pool_fwd(z, wd, scale, *, name):
    s = z.shape[0]
    t = _pick(s, (512, 256, 128))

    def body(x_ref, halo_ref, wd_ref, sc_ref, y_ref):
        i = pl.program_id(0)
        halo = jnp.where(i == 0, 0.0, halo_ref[...])
        y = _pool_tile(jnp.concatenate([halo, x_ref[...]], axis=0), _pool_inv(i, t), wd_ref[...], sc_ref[...])
        y_ref[...] = y.astype(BF16)

    in_specs = [pl.BlockSpec((t, W_GRP), lambda i: (i, 8)),
                pl.BlockSpec((16, W_GRP), lambda i: (jnp.maximum(i * (t // 16) - 1, 0), 8)), _spec(wd), _spec(scale)]
    return _pcall(body, name=name, out_shape=_sds((s, W_GRP), BF16), grid=(s // t,), in_specs=in_specs,
                  out_specs=pl.BlockSpec((t, W_GRP), lambda i: (i, 0)), semantics=("parallel",),
                  block_bytes=12 * _nbytes((t, W_GRP), F32))(z, z, _arr(wd), _arr(scale))


def _pool_bwd(z, dmix, wd, scale, *, name):
    s = z.shape[0]
    t = _pick(s, (512, 256, 128))
    nt = s // t

    def body(x_ref, halo_ref, dy_ref, wd_ref, sc_ref, dx_ref, dwd_ref, dsc_ref, dhalo_scr):
        i = pl.program_id(0)
        r = nt - 1 - i

        @pl.when(i == 0)
        def _():
            dhalo_scr[...] = jnp.zeros_like(dhalo_scr)

        halo = jnp.where(r == 0, 0.0, halo_ref[...])
        inv = _pool_inv(r, t)
        _, vjp = jax.vjp(lambda e, w, sc: _pool_tile(e, inv, w, sc), jnp.concatenate([halo, x_ref[...]], axis=0),
                         wd_ref[...], sc_ref[...])
        dext, dwd, dsc = vjp(dy_ref[...])
        dmain = dext[16:]
        dx = jnp.concatenate([dmain[:t - 16], dmain[t - 16:] + dhalo_scr[...]], axis=0)
        dx_ref[...] = dx.astype(BF16)
        dhalo_scr[...] = dext[:16]
        _acc_out(dwd_ref, dwd, i == 0)
        _acc_out(dsc_ref, dsc, i == 0)

    rev = lambda c: pl.BlockSpec((t, W_GRP), lambda i: (nt - 1 - i, c))
    in_specs = [rev(8), pl.BlockSpec((16, W_GRP), lambda i: (jnp.maximum((nt - 1 - i) * (t // 16) - 1, 0), 8)), rev(3),
                _spec(wd), _spec(scale)]
    return _pcall(body, name=name, out_shape=(_sds((s, W_GRP), BF16), _sds(wd.shape, F32), _sds(scale.shape, F32)),
                  grid=(nt,), in_specs=in_specs, out_specs=(rev(0), _ospec(wd), _ospec(scale)),
                  scratch_shapes=[pltpu.VMEM((16, W_GRP), F32)], semantics=("arbitrary",),
                  block_bytes=20 * _nbytes((t, W_GRP), F32))(z, z, dmix, _arr(wd), _arr(scale))


def _hgrn_fwd(z, lb, ngf, *, name):
    s = z.shape[0]
    c = HGRN_CHUNK
    nc = s // c

    def body(q_ref, f_ref, i_ref, g_ref, lb_ref, ng_ref, y_ref, sts_ref, st_scr):
        @pl.when(pl.program_id(0) == 0)
        def _():
            st_scr[...] = jnp.zeros_like(st_scr)

        st = st_scr[...]
        sts_ref[...] = st
        y, st_new = _hgrn_chunk(q_ref[...], f_ref[...], i_ref[...], g_ref[...], st, lb_ref[...], ng_ref[...])
        y_ref[...] = y.astype(BF16)
        st_scr[...] = st_new

    col = lambda k: pl.BlockSpec((c, W_GRP), lambda i: (i, k))
    vec = pl.BlockSpec((1, W_GRP), lambda i: (0, 0))
    return _pcall(body, name=name, out_shape=(_sds((s, W_GRP), BF16), _sds((nc, W_GRP, W_GRP), F32)), grid=(nc,),
                  in_specs=[col(4), col(5), col(6), col(7), _spec(lb), _spec(ngf)],
                  out_specs=(pl.BlockSpec((c, W_GRP), lambda i: (i, 0)), pl.BlockSpec((None, W_GRP, W_GRP), lambda i: (i, 0, 0))),
                  scratch_shapes=[pltpu.VMEM((W_GRP, W_GRP), F32)], semantics=("arbitrary",),
                  block_bytes=16 * _nbytes((W_GRP, W_GRP), F32))(z, z, z, z, _arr(lb), _arr(ngf))


def _hgrn_bwd(z, dmix, sts, lb, ngf, *, name):
    s = z.shape[0]
    c = HGRN_CHUNK
    nc = s // c

    def body(q_ref, f_ref, i_ref, g_ref, dy_ref, st_ref, lb_ref, ng_ref, dz_ref, dlb_ref, dng_ref, dst_scr):
        i = pl.program_id(0)

        @pl.when(i == 0)
        def _():
            dst_scr[...] = jnp.zeros_like(dst_scr)

        _, vjp = jax.vjp(_hgrn_chunk, q_ref[...], f_ref[...], i_ref[...], g_ref[...], st_ref[...], lb_ref[...],
                         ng_ref[...])
        dq, df, di, dg, dst, dlb, dng = vjp((dy_ref[...], dst_scr[...]))
        dz_ref[...] = jnp.concatenate([dq, df, di, dg], axis=1).astype(BF16)
        dst_scr[...] = dst
        _acc_out(dlb_ref, dlb, i == 0)
        _acc_out(dng_ref, dng, i == 0)

    rev = lambda k: pl.BlockSpec((c, W_GRP), lambda i: (nc - 1 - i, k))
    vec = pl.BlockSpec((1, W_GRP), lambda i: (0, 0))
    return _pcall(body, name=name, out_shape=(_sds((s, 4 * W_GRP), BF16), _sds((1, W_GRP), F32), _sds((1, W_GRP), F32)),
                  grid=(nc,),
                  in_specs=[rev(4), rev(5), rev(6), rev(7), rev(2),
                            pl.BlockSpec((None, W_GRP, W_GRP), lambda i: (nc - 1 - i, 0, 0)), _spec(lb), _spec(ngf)],
                  out_specs=(pl.BlockSpec((c, 4 * W_GRP), lambda i: (nc - 1 - i, 0)), vec, vec),
                  scratch_shapes=[pltpu.VMEM((W_GRP, W_GRP), F32)], semantics=("arbitrary",),
                  block_bytes=32 * _nbytes((W_GRP, W_GRP), F32))(z, z, z, z, dmix, sts, _arr(lb), _arr(ngf))


def _lbs_fwd(c_lb, *, name):
    def body(c_ref, o_ref):
        c = c_ref[...]
        e = jnp.exp(c - jnp.max(c, axis=0, keepdims=True))
        sm = e / jnp.sum(e, axis=0, keepdims=True)
        run = jnp.zeros((1, W_GRP), F32)
        o_ref[0:1, :] = run
        for l in range(1, DEPTH):
            run = run + sm[l:l + 1]
            o_ref[l:l + 1, :] = run

    return _pcall(body, name=name, out_shape=_sds((DEPTH, W_GRP), F32))(c_lb)


def _lbs_bwd(c_lb, dlbs, *, name):
    def body(c_ref, d_ref, o_ref):
        c = c_ref[...]
        e = jnp.exp(c - jnp.max(c, axis=0, keepdims=True))
        sm = e / jnp.sum(e, axis=0, keepdims=True)
        d = d_ref[...]
        dsm = [None] * DEPTH
        run = jnp.zeros((1, W_GRP), F32)
        for l in range(DEPTH - 1, 0, -1):
            run = run + d[l:l + 1]
            dsm[l] = run
        dsm[0] = jnp.zeros((1, W_GRP), F32)
        inner = sum(sm[l:l + 1] * dsm[l] for l in range(DEPTH))
        for l in range(DEPTH):
            o_ref[l:l + 1, :] = sm[l:l + 1] * (dsm[l] - inner)

    return _pcall(body, name=name, out_shape=_sds((DEPTH, W_GRP), F32))(c_lb, dlbs)


def _ffn_fwd(hg, hv, cwf, cbf, *, name):
    s, n = hg.shape
    t = _pick(s, (256, 128))
    cw = _pick(n, (1408, 256, 128))
    nj = n // cw

    def body(g_ref, gh_ref, v_ref, vh_ref, wg_ref, bg_ref, wv_ref, bv_ref, o_ref):
        first = pl.program_id(1) == 0
        eg = jnp.concatenate([jnp.where(first, 0.0, gh_ref[...]), g_ref[...]], axis=0)
        ev = jnp.concatenate([jnp.where(first, 0.0, vh_ref[...]), v_ref[...]], axis=0)
        o_ref[...] = _ffn_tile(eg, ev, wg_ref[...], bg_ref[...], wv_ref[...], bv_ref[...]).astype(BF16)

    main = pl.BlockSpec((t, cw), lambda j, i: (i, j))
    halo = pl.BlockSpec((8, cw), lambda j, i: (jnp.maximum(i * (t // 8) - 1, 0), j))
    taps = lambda off: _spec(cwf, (3, cw), lambda j, i: (0, j + off))
    bias = lambda off: _spec(cbf, (1, cw), lambda j, i: (0, j + off))
    return _pcall(body, name=name, out_shape=_sds((s, n), BF16), grid=(nj, s // t),
                  in_specs=[main, halo, main, halo, taps(0), bias(0), taps(nj), bias(nj)], out_specs=main,
                  semantics=("parallel", "parallel"), block_bytes=12 * _nbytes((t, cw), F32))(
                      hg, hg, hv, hv, _arr(cwf), _arr(cbf), _arr(cwf), _arr(cbf))


def _ffn_bwd(hg, hv, da, cwf, cbf, *, name):
    s, n = hg.shape
    t = _pick(s, (256, 128))
    cw = _pick(n, (1408, 256, 128))
    nt = s // t
    nj = n // cw

    def body(g_ref, gh_ref, v_ref, vh_ref, da_ref, wg_ref, bg_ref, wv_ref, bv_ref, dg_ref, dv_ref, dwg_ref, dwv_ref,
             cg_scr, cv_scr):
        i = pl.program_id(1)
        r = nt - 1 - i

        @pl.when(i == 0)
        def _():
            cg_scr[...] = jnp.zeros_like(cg_scr)
            cv_scr[...] = jnp.zeros_like(cv_scr)

        eg = jnp.concatenate([jnp.where(r == 0, 0.0, gh_ref[...]), g_ref[...]], axis=0)
        ev = jnp.concatenate([jnp.where(r == 0, 0.0, vh_ref[...]), v_ref[...]], axis=0)
        _, vjp = jax.vjp(_ffn_tile, eg, ev, wg_ref[...], bg_ref[...], wv_ref[...], bv_ref[...])
        deg, dev, dwg, dbg, dwv, dbv = vjp(da_ref[...])
        for dext, scr, ref in ((deg, cg_scr, dg_ref), (dev, cv_scr, dv_ref)):
            dmain = dext[8:]
            ref[...] = jnp.concatenate([dmain[:t - 8], dmain[t - 8:] + scr[...]], axis=0).astype(BF16)
            scr[...] = dext[:8]
        zeros = jnp.zeros((4, cw), F32)
        _acc_out(dwg_ref, jnp.concatenate([dwg, dbg, zeros], axis=0), i == 0)
        _acc_out(dwv_ref, jnp.concatenate([dwv, dbv, zeros], axis=0), i == 0)

    main = pl.BlockSpec((t, cw), lambda j, i: (nt - 1 - i, j))
    halo = pl.BlockSpec((8, cw), lambda j, i: (jnp.maximum((nt - 1 - i) * (t // 8) - 1, 0), j))
    taps = lambda off: _spec(cwf, (3, cw), lambda j, i: (0, j + off))
    bias = lambda off: _spec(cbf, (1, cw), lambda j, i: (0, j + off))
    w8 = pl.BlockSpec((8, cw), lambda j, i: (0, j))
    return _pcall(body, name=name,
                  out_shape=(_sds((s, n), BF16), _sds((s, n), BF16), _sds((8, n), F32), _sds((8, n), F32)),
                  grid=(nj, nt), in_specs=[main, halo, main, halo, main, taps(0), bias(0), taps(nj), bias(nj)],
                  out_specs=(main, main, w8, w8),
                  scratch_shapes=[pltpu.VMEM((8, cw), F32), pltpu.VMEM((8, cw), F32)],
                  semantics=("parallel", "arbitrary"), block_bytes=24 * _nbytes((t, cw), F32))(
                      hg, hg, hv, hv, da, _arr(cwf), _arr(cbf), _arr(cwf), _arr(cbf))


def _all_gather(x, *, name):
    r, c = x.shape

    def body(x_ref, out_ref, send_sems, recv_sems, local_sem):
        mx, my, mc = lax.axis_index("x"), lax.axis_index("y"), lax.axis_index("c")
        me, sibling = (mx, my, mc), (mx, my, 1 - mc)
        chips = [(1 - mx, my), (mx, 1 - my), (1 - mx, 1 - my)]

        def slot(px, py, pc):
            return out_ref.at[4 * px + 2 * py + pc]

        def copy(k, block, to, src=None):
            return pltpu.make_async_remote_copy(src_ref=slot(*block) if src is None else src, dst_ref=slot(*block),
                                                send_sem=send_sems.at[k], recv_sem=recv_sems.at[k],
                                                device_id=to, device_id_type=MESH)

        mine = pltpu.make_async_copy(x_ref, slot(*me), local_sem)
        mine.start()
        first = [copy(0, me, sibling, src=x_ref)]
        first += [copy(1 + j, me, (*chip, mc), src=x_ref) for j, chip in enumerate(chips)]
        for cp in first:
            cp.start()
        passed = [copy(4 + j, (*chip, mc), sibling) for j, chip in enumerate(chips)]
        for j, chip in enumerate(chips):
            copy(1 + j, (*chip, mc), me).wait_recv()
            passed[j].start()
        copy(0, sibling, me).wait_recv()
        for j, chip in enumerate(chips):
            copy(4 + j, (*chip, 1 - mc), me).wait_recv()
        for cp in first + passed:
            cp.wait_send()
        mine.wait()

    hbm = pl.BlockSpec(memory_space=pl.ANY)
    return _pcall(body, name=name, out_shape=_sds((N_DEV, r, c), x.dtype), in_specs=[hbm], out_specs=hbm,
                  scratch_shapes=[pltpu.SemaphoreType.DMA((7,)), pltpu.SemaphoreType.DMA((7,)),
                                  pltpu.SemaphoreType.DMA(())])(x)


def _swap_sibling(x, *, name):
    def body(x_ref, out_ref, send_sem, recv_sem):
        sibling = (lax.axis_index("x"), lax.axis_index("y"), 1 - lax.axis_index("c"))
        cp = pltpu.make_async_remote_copy(src_ref=x_ref, dst_ref=out_ref, send_sem=send_sem, recv_sem=recv_sem,
                                          device_id=sibling, device_id_type=MESH)
        cp.start()
        cp.wait()

    hbm = pl.BlockSpec(memory_space=pl.ANY)
    return _pcall(body, name=name, out_shape=_sds(x.shape, x.dtype), in_specs=[hbm], out_specs=hbm,
                  scratch_shapes=[pltpu.SemaphoreType.DMA(()), pltpu.SemaphoreType.DMA(())])(x)


def _exchange_chips(p, *, name):
    def body(p_ref, out_ref, send_sems, recv_sems, local_sem):
        mx, my, mc = lax.axis_index("x"), lax.axis_index("y"), lax.axis_index("c")
        mine_q = 2 * mx + my
        chips = [(1 - mx, my), (mx, 1 - my), (1 - mx, 1 - my)]

        def copy(k, chip):
            return pltpu.make_async_remote_copy(src_ref=p_ref.at[2 * chip[0] + chip[1]], dst_ref=out_ref.at[mine_q],
                                                send_sem=send_sems.at[k], recv_sem=recv_sems.at[k],
                                                device_id=(*chip, mc), device_id_type=MESH)

        def arrival(k, chip):
            return pltpu.make_async_remote_copy(src_ref=p_ref.at[mine_q], dst_ref=out_ref.at[2 * chip[0] + chip[1]],
                                                send_sem=send_sems.at[k], recv_sem=recv_sems.at[k],
                                                device_id=(*chip, mc), device_id_type=MESH)

        own = pltpu.make_async_copy(p_ref.at[mine_q], out_ref.at[mine_q], local_sem)
        own.start()
        sends = [copy(k, chip) for k, chip in enumerate(chips)]
        for cp in sends:
            cp.start()
        for k, chip in enumerate(chips):
            arrival(k, chip).wait_recv()
        for cp in sends:
            cp.wait_send()
        own.wait()

    hbm = pl.BlockSpec(memory_space=pl.ANY)
    return _pcall(body, name=name, out_shape=_sds(p.shape, p.dtype), in_specs=[hbm], out_specs=hbm,
                  scratch_shapes=[pltpu.SemaphoreType.DMA((3,)), pltpu.SemaphoreType.DMA((3,)),
                                  pltpu.SemaphoreType.DMA(())])(p)


def _add_pairs(a, b, *, name):
    q, r, c = a.shape
    tr = _pick(r, (544, 408, 272, 136, 64, 32, 16, 8))

    def body(a_ref, b_ref, o_ref):
        o_ref[...] = (a_ref[...].astype(F32) + b_ref[...].astype(F32)).astype(o_ref.dtype)

    blk = pl.BlockSpec((None, tr, c), lambda i, j: (i, j, 0))
    return _pcall(body, name=name, out_shape=_sds(a.shape, a.dtype), grid=(q, r // tr), in_specs=[blk, blk],
                  out_specs=blk, semantics=("parallel", "parallel"), block_bytes=4 * _nbytes((tr, c), F32))(a, b)


def _sum_slots(p, *, name):
    q, r, c = p.shape
    tr = _pick(r, (544, 408, 272, 192, 136, 64, 32, 16, 8))

    def body(p_ref, o_ref):
        acc = p_ref[0].astype(F32)
        for k in range(1, q):
            acc = acc + p_ref[k].astype(F32)
        o_ref[...] = acc

    return _pcall(body, name=name, out_shape=_sds((r, c), F32), grid=(r // tr,),
                  in_specs=[pl.BlockSpec((q, tr, c), lambda i: (0, i, 0))],
                  out_specs=pl.BlockSpec((tr, c), lambda i: (i, 0)), semantics=("parallel",),
                  block_bytes=(q + 2) * _nbytes((tr, c), F32))(p)


BIG_COMM = (('w_in', 288, D_MODEL), ('w_out', 128, D_MODEL), ('w_up', 704, D_MODEL), ('w_down', 352, D_MODEL),
            ('w_pe', 128, PLE_DIM), ('w_pg', 128, D_MODEL))
HBM_SPEC = pl.BlockSpec(memory_space=pl.ANY)


def _gather_layer(shards, l, *, name):
    na = len(shards)

    def body(*refs):
        x_refs, out_refs = refs[:na], refs[na:2 * na]
        send_sems, recv_sems, local_sems = refs[2 * na:]
        mx, my, mc = lax.axis_index("x"), lax.axis_index("y"), lax.axis_index("c")
        me, sibling = (mx, my, mc), (mx, my, 1 - mc)
        chips = [(1 - mx, my), (mx, 1 - my), (1 - mx, 1 - my)]

        def slot(a, px, py, pc):
            return out_refs[a].at[4 * px + 2 * py + pc]

        def copy(k, a, block, to, own=False):
            return pltpu.make_async_remote_copy(src_ref=x_refs[a].at[l] if own else slot(a, *block),
                                                dst_ref=slot(a, *block), send_sem=send_sems.at[k, a],
                                                recv_sem=recv_sems.at[k, a], device_id=to, device_id_type=MESH)

        mine = [pltpu.make_async_copy(x_refs[a].at[l], slot(a, *me), local_sems.at[a]) for a in range(na)]
        for cp in mine:
            cp.start()
        first = []
        for a in range(na):
            first.append(copy(0, a, me, sibling, own=True))
            first += [copy(1 + j, a, me, (*chip, mc), own=True) for j, chip in enumerate(chips)]
        for cp in first:
            cp.start()
        passed = []
        for j, chip in enumerate(chips):
            for a in range(na):
                copy(1 + j, a, (*chip, mc), me).wait_recv()
                fwd = copy(4 + j, a, (*chip, mc), sibling)
                fwd.start()
                passed.append(fwd)
        for a in range(na):
            copy(0, a, sibling, me).wait_recv()
        for j, chip in enumerate(chips):
            for a in range(na):
                copy(4 + j, a, (*chip, 1 - mc), me).wait_recv()
        for cp in first + passed:
            cp.wait_send()
        for cp in mine:
            cp.wait()

    return _pcall(body, name=name, out_shape=tuple(_sds((N_DEV,) + x.shape[1:], x.dtype) for x in shards),
                  in_specs=[HBM_SPEC] * na, out_specs=(HBM_SPEC,) * na,
                  scratch_shapes=[pltpu.SemaphoreType.DMA((7, na)), pltpu.SemaphoreType.DMA((7, na)),
                                  pltpu.SemaphoreType.DMA((na,))])(*shards)


SEM_SPEC = pl.BlockSpec(memory_space=pltpu.SEMAPHORE)
DATAFLOW_EFFECT = pltpu.SideEffectType.DATAFLOW_SIDE_EFFECTING


def _place_own(shards, *, name):
    na = len(shards)

    def body(*refs):
        x_refs, land_refs, sems = refs[:na], refs[na:2 * na], refs[2 * na]
        me = 4 * lax.axis_index("x") + 2 * lax.axis_index("y") + lax.axis_index("c")
        cps = [pltpu.make_async_copy(x_refs[a], land_refs[a].at[me], sems.at[a]) for a in range(na)]
        for cp in cps:
            cp.start()
        for cp in cps:
            cp.wait()

    return _pcall(body, name=name, out_shape=tuple(_sds((N_DEV,) + x.shape, x.dtype) for x in shards),
                  in_specs=[HBM_SPEC] * na, out_specs=(HBM_SPEC,) * na,
                  scratch_shapes=[pltpu.SemaphoreType.DMA((na,))])(*shards)


def _gather_start(shards, lands, *, name):
    na = len(shards)

    def body(*refs):
        x_refs, land_refs = refs[:na], refs[na:2 * na]
        send_sems, recv_sems = refs[2 * na], refs[2 * na + 1]
        token = refs[-1]
        mx, my, mc = lax.axis_index("x"), lax.axis_index("y"), lax.axis_index("c")
        me = 4 * mx + 2 * my + mc
        peers = [(mx, my, 1 - mc)]
        for px, py in ((1 - mx, my), (mx, 1 - my), (1 - mx, 1 - my)):
            peers += [(px, py, mc), (px, py, 1 - mc)]
        for a in range(na):
            for peer in peers:
                pltpu.make_async_remote_copy(src_ref=x_refs[a], dst_ref=land_refs[a].at[me], send_sem=send_sems.at[a],
                                             recv_sem=recv_sems.at[a], device_id=peer, device_id_type=MESH).start()
        token[...] = jnp.zeros_like(token)

    hbm = lambda x: pltpu.HBM(x.shape, x.dtype)
    out_shape = ((pltpu.SemaphoreType.DMA((na,)), pltpu.SemaphoreType.DMA((na,))) + tuple(hbm(x) for x in shards)
                 + tuple(hbm(x) for x in lands) + (_sds((8, 128), F32),))
    params = pltpu.CompilerParams(has_side_effects=DATAFLOW_EFFECT)
    pin = lambda x: pltpu.with_memory_space_constraint(x, pltpu.HBM)
    return pl.pallas_call(body, name=name, out_shape=out_shape, in_specs=[HBM_SPEC] * (2 * na),
                          out_specs=(SEM_SPEC, SEM_SPEC) + (HBM_SPEC,) * (2 * na) + (pl.BlockSpec(memory_space=pltpu.VMEM),),
                          input_output_aliases={i: 2 + i for i in range(2 * na)}, compiler_params=params)(
                              *[pin(x) for x in shards], *[pin(x) for x in lands])


def _gather_wait(started, after, *, name):
    send_sems, recv_sems, *bufs, _ = started
    na = len(bufs) // 2

    def body(*refs):
        land_refs = refs[na:2 * na]
        s_sems, r_sems = refs[2 * na], refs[2 * na + 1]
        me = (lax.axis_index("x"), lax.axis_index("y"), lax.axis_index("c"))
        for a in range(na):
            seven = land_refs[a].at[pl.ds(0, N_DEV - 1)]
            cp = pltpu.make_async_remote_copy(src_ref=seven, dst_ref=seven, send_sem=s_sems.at[a], recv_sem=r_sems.at[a],
                                              device_id=me, device_id_type=MESH)
            cp.wait_send()
            cp.wait_recv()

    hbm = lambda x: pltpu.HBM(x.shape, x.dtype)
    params = pltpu.CompilerParams(has_side_effects=DATAFLOW_EFFECT)
    outs = pl.pallas_call(body, name=name, out_shape=tuple(hbm(x) for x in bufs),
                          in_specs=[HBM_SPEC] * (2 * na) + [SEM_SPEC, SEM_SPEC, HBM_SPEC],
                          out_specs=(HBM_SPEC,) * (2 * na), input_output_aliases={i: i for i in range(2 * na)},
                          compiler_params=params)(*bufs, send_sems, recv_sems, after)
    return outs[na:]


def _pair_swap(grads, *, name):
    na = len(grads)

    def body(*refs):
        g_refs, recv_refs = refs[:na], refs[na:2 * na]
        send_sems, recv_sems = refs[2 * na:]
        mx, my, mc = lax.axis_index("x"), lax.axis_index("y"), lax.axis_index("c")
        sibling = (mx, my, 1 - mc)
        for a in range(na):
            for q in range(4):
                pltpu.make_async_remote_copy(src_ref=g_refs[a].at[q, 1 - mc], dst_ref=recv_refs[a].at[q],
                                             send_sem=send_sems.at[a], recv_sem=recv_sems.at[a],
                                             device_id=sibling, device_id_type=MESH).start()
        for a in range(na):
            pltpu.make_async_remote_copy(src_ref=recv_refs[a], dst_ref=recv_refs[a], send_sem=send_sems.at[a],
                                         recv_sem=recv_sems.at[a], device_id=sibling, device_id_type=MESH).wait()

    half = tuple(_sds((4,) + g.shape[2:], g.dtype) for g in grads)
    return _pcall(body, name=name, out_shape=half, in_specs=[HBM_SPEC] * na, out_specs=(HBM_SPEC,) * na,
                  scratch_shapes=[pltpu.SemaphoreType.DMA((na,)), pltpu.SemaphoreType.DMA((na,))])(*grads)


def _add_slabs(grads, recv, core, *, name):
    na = len(grads)

    def body(core_ref, *refs):
        for a in range(na):
            refs[2 * na + a][...] = (refs[a][...].astype(F32) + refs[na + a][...].astype(F32)).astype(BF16)

    own_specs = [pl.BlockSpec((None, None) + x.shape[2:], lambda q, core_ref: (q, core_ref[0], 0, 0)) for x in grads]
    specs = [pl.BlockSpec((None,) + x.shape[1:], lambda q, core_ref: (q, 0, 0)) for x in recv]
    blk = sum(_nbytes(x.shape[1:], F32) for x in recv)
    grid_spec = pltpu.PrefetchScalarGridSpec(num_scalar_prefetch=1, grid=(4,), in_specs=own_specs + specs,
                                             out_specs=tuple(specs))
    params = pltpu.CompilerParams(dimension_semantics=("parallel",), vmem_limit_bytes=_vmem_limit(2 * blk))
    return pl.pallas_call(body, name=name, out_shape=tuple(_sds(x.shape, BF16) for x in recv), grid_spec=grid_spec,
                          compiler_params=params)(core, *grads, *recv)


def _chip_exchange(parts, *, name):
    na = len(parts)

    def body(*refs):
        p_refs, out_refs = refs[:na], refs[na:2 * na]
        send_sems, recv_sems, local_sems = refs[2 * na:]
        mx, my, mc = lax.axis_index("x"), lax.axis_index("y"), lax.axis_index("c")
        mine_q = 2 * mx + my
        chips = [(1 - mx, my), (mx, 1 - my), (1 - mx, 1 - my)]
        owns = [pltpu.make_async_copy(p_refs[a].at[mine_q], out_refs[a].at[mine_q], local_sems.at[a]) for a in range(na)]
        for cp in owns:
            cp.start()
        sends = []
        for a in range(na):
            for k, chip in enumerate(chips):
                sends.append(pltpu.make_async_remote_copy(
                    src_ref=p_refs[a].at[2 * chip[0] + chip[1]], dst_ref=out_refs[a].at[mine_q],
                    send_sem=send_sems.at[k, a], recv_sem=recv_sems.at[k, a], device_id=(*chip, mc), device_id_type=MESH))
        for cp in sends:
            cp.start()
        for a in range(na):
            for k, chip in enumerate(chips):
                pltpu.make_async_remote_copy(
                    src_ref=p_refs[a].at[mine_q], dst_ref=out_refs[a].at[2 * chip[0] + chip[1]],
                    send_sem=send_sems.at[k, a], recv_sem=recv_sems.at[k, a], device_id=(*chip, mc),
                    device_id_type=MESH).wait_recv()
        for cp in sends:
            cp.wait_send()
        for cp in owns:
            cp.wait()

    return _pcall(body, name=name, out_shape=tuple(_sds(x.shape, x.dtype) for x in parts), in_specs=[HBM_SPEC] * na,
                  out_specs=(HBM_SPEC,) * na,
                  scratch_shapes=[pltpu.SemaphoreType.DMA((3, na)), pltpu.SemaphoreType.DMA((3, na)),
                                  pltpu.SemaphoreType.DMA((na,))])(*parts)


def _sum_chips(parts, *, name):
    na = len(parts)

    def body(*refs):
        for a in range(na):
            p_ref = refs[a]
            acc = p_ref[0].astype(F32)
            for k in range(1, 4):
                acc = acc + p_ref[k].astype(F32)
            refs[na + a][...] = acc

    half = lambda x: x.shape[1] // 2
    in_specs = [pl.BlockSpec((4, half(x), x.shape[2]), lambda i: (0, i, 0)) for x in parts]
    out_specs = tuple(pl.BlockSpec((half(x), x.shape[2]), lambda i: (i, 0)) for x in parts)
    blk = sum(_nbytes((6, half(x), x.shape[2]), BF16) for x in parts)
    return _pcall(body, name=name, out_shape=tuple(_sds(x.shape[1:], F32) for x in parts), grid=(2,),
                  in_specs=in_specs, out_specs=out_specs, semantics=("parallel",), block_bytes=blk)(*parts)


def _reduce_layer(grads, l):
    n = lambda s: f"l{l}_{s}"
    views = [g.reshape(4, 2, g.shape[0] // N_DEV, g.shape[1]) for g in grads]
    recv = _pair_swap(views, name=n("reduce_pair"))
    core = lax.axis_index("c").astype(jnp.int32).reshape(1)
    chip_sum = _add_slabs(views, recv, core, name=n("reduce_pair_add"))
    from_chips = _chip_exchange(chip_sum, name=n("reduce_chips"))
    return _sum_chips(from_chips, name=n("reduce_chips_add"))


def _adamw(w, g, m, v, *, name):
    lead, (r, c) = w.shape[:-2], w.shape[-2:]
    tr = _pick(r, (512, 256, 192, 128, 64, 32, 16, 8))
    c1 = 1.0 / (1.0 - ADAM_B1 ** ADAM_STEP)
    c2 = 1.0 / (1.0 - ADAM_B2 ** ADAM_STEP)

    def body(w_ref, g_ref, m_ref, v_ref, d_ref, nm_ref, nv_ref):
        gv = g_ref[...]
        nm = ADAM_B1 * m_ref[...] + (1.0 - ADAM_B1) * gv
        nv = ADAM_B2 * v_ref[...] + (1.0 - ADAM_B2) * jnp.square(gv)
        d_ref[...] = -ADAM_LR * ((nm * c1) / (jnp.sqrt(nv * c2) + ADAM_EPS) + ADAM_WD * w_ref[...])
        nm_ref[...] = nm
        nv_ref[...] = nv

    if lead:
        blk = pl.BlockSpec((None, tr, c), lambda k, i: (k, i, 0))
        grid, sem = (lead[0], r // tr), ("parallel", "parallel")
    else:
        blk = pl.BlockSpec((tr, c), lambda i: (i, 0))
        grid, sem = (r // tr,), ("parallel",)
    out = _sds(w.shape, F32)
    return _pcall(body, name=name, out_shape=(out, out, out), grid=grid, in_specs=[blk] * 4,
                  out_specs=(blk, blk, blk), semantics=sem, block_bytes=7 * _nbytes((tr, c), F32))(w, g, m, v)


def _pack_flat(arrs, rows, cols=1024):
    flat = jnp.concatenate([a.reshape(-1).astype(F32) for a in arrs])
    pad = rows * cols - flat.shape[0]
    return jnp.pad(flat, (0, pad)).reshape(rows, cols)


def _unpack_flat(buf, shapes):
    flat = buf.reshape(-1)
    out, off = [], 0
    for shp in shapes:
        n = 1
        for s in shp:
            n *= s
        out.append(flat[off:off + n].reshape(shp))
        off += n
    return out


def _flat_rows(shapes, cols=1024):
    n = sum(functools.reduce(lambda a, b: a * b, shp, 1) for shp in shapes)
    rows = -(-n // cols)
    return -(-rows // 64) * 64


def _block_diag(w):
    eye = jnp.eye(N_HEADS, dtype=w.dtype)
    return (w[:, :, :, None, :] * eye[None, :, None, :, None]).reshape(w.shape[0], W_GRP, W_GRP)


def _diag_blocks(w):
    w5 = w.reshape(w.shape[0], N_HEADS, HEAD_DIM, N_HEADS, HEAD_DIM)
    return jnp.stack([w5[:, h, :, h, :] for h in range(N_HEADS)], axis=1)


def _pack_big_shards(w):
    rows = []
    for l in range(DEPTH):
        rows += [w['w_in'][l].T, w['w_out'][l], w['w_up'][l].T, w['w_down'][l],
                 w['w_pe'][l].T.reshape(32, 1024), w['w_pg'][l]]
    return jnp.concatenate(rows, axis=0)


def _unpack_big_full(g, l):
    out, off = {}, l * LAYER_ROWS
    for nm, r in SLAB_ROWS:
        blk = g[:, off:off + r, :]
        if nm == 'w_pe':
            out[nm] = blk.reshape(N_DEV, 128, PLE_DIM).reshape(N_DEV * 128, PLE_DIM)
        elif nm == 'w_up':
            out['w_up_g'] = blk[:N_DEV // 2].reshape(D_FF, D_MODEL)
            out['w_up_v'] = blk[N_DEV // 2:].reshape(D_FF, D_MODEL)
        else:
            out[nm] = blk.reshape(N_DEV * r, D_MODEL)
        off += r
    return out


def _pack_big_grads(gl):
    cols = []
    for l in range(DEPTH):
        g = gl[l]
        cols += [g['w_in'].reshape(N_DEV, 288, D_MODEL), g['w_out'].reshape(N_DEV, 128, D_MODEL),
                 jnp.concatenate([g['w_up_g'], g['w_up_v']], axis=0).reshape(N_DEV, 704, D_MODEL),
                 g['w_down'].reshape(N_DEV, 352, D_MODEL),
                 g['w_pe'].reshape(N_DEV, 128, PLE_DIM).reshape(N_DEV, 32, D_MODEL), g['w_pg'].reshape(N_DEV, 128, D_MODEL)]
    return jnp.concatenate(cols, axis=1)


def _unpack_big_shard(gs):
    out = {nm: [] for nm in BIG_NAMES}
    for l in range(DEPTH):
        off = l * LAYER_ROWS
        for nm, r in SLAB_ROWS:
            blk = gs[off:off + r]
            if nm in ('w_in', 'w_up'):
                blk = blk.T
            elif nm == 'w_pe':
                blk = blk.reshape(128, PLE_DIM).T
            out[nm].append(blk)
            off += r
    return {nm: jnp.stack(v) for nm, v in out.items()}


def _stacked_params(w, lbs):
    tril = jnp.tril(jnp.ones((GMLP_CHUNK, GMLP_CHUNK), bool))
    row = lambda a: a.reshape(DEPTH, 1, -1)
    return dict(
        g1=row(w['norm1_g']), g2=row(w['norm2_g']), g3=row(w['norm3_g']),
        a_ln_g=row(w['a_ln_g']), a_ln_b=row(w['a_ln_b']),
        a_wcat=jnp.where(tril, w['a_ws'], 0.0).reshape(DEPTH, N_HEADS * GMLP_CHUNK, GMLP_CHUNK),
        a_bfull=jnp.repeat(jnp.swapaxes(w['a_bs'], 1, 2), HEAD_DIM, axis=2),
        b_cw=w['b_conv_w_full'], b_cb=row(w['b_conv_b']), b_wa=_block_diag(w['b_wa']), b_ba=row(w['b_ba']),
        b_wx=_block_diag(w['b_wx']), b_bx=row(w['b_bx']), b_lam=row(w['b_lam']),
        c_lb=row(lbs), c_ngf=row(jnp.tile(w['c_norm_g'], (1, N_HEADS))),
        d_wd=_block_diag(w['d_w']), d_scale=row(w['d_scale']),
        f_cw=w['ffn_conv_w_full'], f_cb=row(w['ffn_conv_b']),
    )


B_PRM = ('b_cw', 'b_cb', 'b_wa', 'b_ba', 'b_wx', 'b_bx', 'b_lam')


def _layer_fwd(x, p_bf, wb, sp, l):
    n = lambda s: f"l{l}_{s}"
    h = _rms_fwd(x, sp['g1'], name=n("norm1"))
    z = _matmul(h, wb['w_in'], nt=True, name=n("proj_in"))
    ya = _gmlp_fwd(z, sp['a_ln_g'], sp['a_ln_b'], sp['a_wcat'], sp['a_bfull'], name=n("gmlp"))
    yb, h0s = _rglru_fwd(z, [sp[k] for k in B_PRM], name=n("rglru"))
    yc, sts = _hgrn_fwd(z, sp['c_lb'], sp['c_ngf'], name=n("hgrn"))
    yd = _pool_fwd(z, sp['d_wd'], sp['d_scale'], name=n("pool"))
    mix = jnp.concatenate([ya, yb, yc, yd], axis=1)
    x1 = _matmul(mix, wb['w_out'], res=x, name=n("proj_out"))
    h2 = _rms_fwd(x1, sp['g2'], name=n("norm2"))
    hg = _matmul(h2, wb['w_up_g'], nt=True, name=n("up_gate"))
    hv = _matmul(h2, wb['w_up_v'], nt=True, name=n("up_val"))
    a = _ffn_fwd(hg, hv, sp['f_cw'], sp['f_cb'], name=n("ffn_gate"))
    x2 = _matmul(a, wb['w_down'], res=x1, name=n("down"))
    h3 = _rms_fwd(x2, sp['g3'], name=n("norm3"))
    gl = _matmul(h3, wb['w_pg'], name=n("ple_gate"))
    pe = _matmul(p_bf, wb['w_pe'], nt=True, name=n("ple_emb"))
    x3 = _ple_fwd(x2, gl, pe, name=n("ple"))
    saved = dict(x=x, h=h, z=z, h0s=h0s, sts=sts, mix=mix, x1=x1, h2=h2, hg=hg, hv=hv, a=a, x2=x2, h3=h3, gl=gl, pe=pe)
    return x3, saved


def _layer_bwd(dx3, sv, p_bf, wb, sp, l):
    n = lambda s: f"l{l}_{s}_bwd"
    gb, gs = {}, {}
    dpe, dgl = _ple_bwd(dx3, sv['gl'], sv['pe'], name=n("ple"))
    gb['w_pe'] = _matmul_tn(dpe, p_bf, name=n("ple_emb_w"))
    gb['w_pg'] = _matmul_tn(sv['h3'], dgl, name=n("ple_gate_w"))
    dh3 = _matmul(dgl, wb['w_pg'], nt=True, name=n("ple_gate_x"))
    dx2, dx2b, gs['norm3_g'] = _rms_bwd(sv['x2'], sp['g3'], dh3, dx3, name=n("norm3"))
    da = _matmul(dx2b, wb['w_down'], nt=True, name=n("down_x"))
    gb['w_down'] = _matmul_tn(sv['a'], dx2b, name=n("down_w"))
    dhg, dhv, gs['f_dwg'], gs['f_dwv'] = _ffn_bwd(sv['hg'], sv['hv'], da, sp['f_cw'], sp['f_cb'], name=n("ffn_gate"))
    gb['w_up_g'] = _matmul_tn(dhg, sv['h2'], name=n("up_gate_w"))
    gb['w_up_v'] = _matmul_tn(dhv, sv['h2'], name=n("up_val_w"))
    dh2 = _matmul(dhg, wb['w_up_g'], name=n("up_gate_x"))
    dh2 = _matmul(dhv, wb['w_up_v'], res=dh2, name=n("up_val_x"))
    dx1, dx1b, gs['norm2_g'] = _rms_bwd(sv['x1'], sp['g2'], dh2, dx2, name=n("norm2"))
    dmix = _matmul(dx1b, wb['w_out'], nt=True, name=n("proj_out_x"))
    gb['w_out'] = _matmul_tn(sv['mix'], dx1b, name=n("proj_out_w"))
    z = sv['z']
    dzu, dzv, gs['a_ln_g'], gs['a_ln_b'], gs['a_wcat'], gs['a_bfull'] = _gmlp_bwd(
        z, dmix, sp['a_ln_g'], sp['a_ln_b'], sp['a_wcat'], sp['a_bfull'], name=n("gmlp"))
    dzb, dzg, *dbp = _rglru_bwd(z, dmix, sv['h0s'], [sp[k] for k in B_PRM], name=n("rglru"))
    gs.update(zip(B_PRM, dbp))
    dzc, gs['c_lb'], gs['c_ngf'] = _hgrn_bwd(z, dmix, sv['sts'], sp['c_lb'], sp['c_ngf'], name=n("hgrn"))
    dzd, gs['d_wd'], gs['d_scale'] = _pool_bwd(z, dmix, sp['d_wd'], sp['d_scale'], name=n("pool"))
    dz = jnp.concatenate([dzu, dzv, dzb, dzg, dzc, dzd], axis=1)
    gb['w_in'] = _matmul_tn(dz, sv['h'], name=n("proj_in_w"))
    dh = _matmul(dz, wb['w_in'], name=n("proj_in_x"))
    dx0, _, gs['norm1_g'] = _rms_bwd(sv['x'], sp['g1'], dh, dx1, name=n("norm1"))
    return dx0, gb, gs


SMALL_NAMES = [nm for nm in WEIGHT_NAMES if nm not in BIG_NAMES]
COL_SHARDED = ('w_in', 'w_up', 'w_pe')


def _comm_shards(w):
    return [(jnp.swapaxes(w[nm], 1, 2) if nm in COL_SHARDED else w[nm]).astype(BF16) for nm, _, _ in BIG_COMM]


def _full_weights(gathered):
    out = {nm: g.reshape(N_DEV * r, c) for g, (nm, r, c) in zip(gathered, BIG_COMM)}
    halves = out.pop('w_up').reshape(2, D_FF, D_MODEL)
    out['w_up_g'], out['w_up_v'] = _Sel(halves, 0), _Sel(halves, 1)
    return out


def _small_grads(raw):
    st = {k: jnp.stack([raw[l][k] for l in range(DEPTH)]) for k in raw[0]}
    tril = jnp.tril(jnp.ones((GMLP_CHUNK, GMLP_CHUNK), bool))
    vec = lambda a: a.reshape(DEPTH, -1)
    out = {nm: vec(st[k]) for nm, k in (('norm1_g', 'norm1_g'), ('norm2_g', 'norm2_g'), ('norm3_g', 'norm3_g'),
                                        ('a_ln_g', 'a_ln_g'), ('a_ln_b', 'a_ln_b'), ('b_conv_b', 'b_cb'),
                                        ('b_ba', 'b_ba'), ('b_bx', 'b_bx'), ('b_lam', 'b_lam'), ('c_lb', 'c_lb'),
                                        ('d_scale', 'd_scale'))}
    out['a_ws'] = jnp.where(tril, st['a_wcat'].reshape(DEPTH, N_HEADS, GMLP_CHUNK, GMLP_CHUNK), 0.0)
    out['a_bs'] = jnp.swapaxes(st['a_bfull'].reshape(DEPTH, GMLP_CHUNK, N_HEADS, HEAD_DIM).sum(-1), 1, 2)
    out['b_conv_w'] = st['b_cw']
    out['b_wa'], out['b_wx'], out['d_w'] = _diag_blocks(st['b_wa']), _diag_blocks(st['b_wx']), _diag_blocks(st['d_wd'])
    out['c_norm_g'] = st['c_ngf'].reshape(DEPTH, N_HEADS, HEAD_DIM).sum(1)
    out['ffn_conv_w'] = jnp.concatenate([st['f_dwg'][:, 0:3], st['f_dwv'][:, 0:3]], axis=2)
    out['ffn_conv_b'] = jnp.concatenate([st['f_dwg'][:, 3], st['f_dwv'][:, 3]], axis=1)
    return out


def _step(w, m, v, x, p, target):
    s = x.shape[1]
    dev = 4 * lax.axis_index("x") + 2 * lax.axis_index("y") + lax.axis_index("c")
    xs = x.reshape(s, D_MODEL)

    shards = _comm_shards(w)
    conv_shapes = [w['b_conv_w'].shape, w['ffn_conv_w'].shape]
    conv_rows = _flat_rows(conv_shapes)
    conv_all = _all_gather(_pack_flat([w['b_conv_w'], w['ffn_conv_w']], conv_rows), name="gather_conv_weights")
    parts = [_unpack_flat(conv_all[d], conv_shapes) for d in range(N_DEV)]
    wf = dict(w)
    wf['b_conv_w_full'] = jnp.concatenate([pt[0] for pt in parts], axis=-1)
    wf['ffn_conv_w_full'] = jnp.concatenate([pt[1] for pt in parts], axis=-1)
    lbs = _lbs_fwd(w['c_lb'], name="hgrn_bounds")

    stacked = _stacked_params(wf, lbs)
    p_all = p.reshape(DEPTH, s, PLE_DIM).astype(BF16)
    xl, saved, wbs, sps = xs, [], [], []
    gathered = _gather_layer(shards, 0, name="l0_gather_weights")
    for l in range(DEPTH):
        sp = {k: _Sel(a, l) for k, a in stacked.items()}
        if l + 1 < DEPTH:
            own = [x[l + 1] for x in shards]
            started = _gather_start(own, _place_own(own, name=f"l{l + 1}_gather_place"), name=f"l{l + 1}_gather_start")
            sp['g1'] = stacked['g1'][l] + started[-1][0, 0]
        wb = _full_weights(gathered)
        p_bf = p_all[l]
        xl, sv = _layer_fwd(xl, p_bf, wb, sp, l)
        if l + 1 < DEPTH:
            gathered = _gather_wait(started, xl, name=f"l{l + 1}_gather_wait")
        saved.append((sv, p_bf))
        wbs.append(wb)
        sps.append(sp)
    loss_part, dx, dfinal = _loss_head(xl, w['final_g'].reshape(1, D_MODEL), target.reshape(s, D_MODEL), name="loss_head")
    loss = lax.psum(loss_part[0, 0], ("x", "y", "c"))

    reduced, small = [None] * DEPTH, [None] * DEPTH
    for l in range(DEPTH - 1, -1, -1):
        sv, p_bf = saved[l]
        dx, gb, small[l] = _layer_bwd(dx, sv, p_bf, wbs[l], sps[l], l)
        gb['w_up'] = jnp.concatenate([gb.pop('w_up_g'), gb.pop('w_up_v')], axis=0)
        reduced[l] = _reduce_layer([gb[nm] for nm, _, _ in BIG_COMM], l)
    grad_x = dx.reshape(1, s, D_MODEL)
    gbig = {}
    for a, (nm, _, _) in enumerate(BIG_COMM):
        g = jnp.stack([reduced[l][a] for l in range(DEPTH)])
        gbig[nm] = jnp.swapaxes(g, 1, 2) if nm in COL_SHARDED else g

    small_parts = _small_grads(small)
    small_parts['c_lb'] = _lbs_bwd(w['c_lb'], small_parts['c_lb'], name="hgrn_bounds_bwd")
    small_parts['final_g'] = dfinal.reshape(D_MODEL)
    small_shapes = [small_parts[nm].shape for nm in SMALL_NAMES]
    small_rows = _flat_rows(small_shapes)
    small_all = _all_gather(_pack_flat([small_parts[nm] for nm in SMALL_NAMES], small_rows), name="gather_small_grads")
    gsmall = dict(zip(SMALL_NAMES, _unpack_flat(_sum_slots(small_all, name="sum_small_grads"), small_shapes)))
    for nm in ('b_conv_w', 'ffn_conv_w'):
        width = w[nm].shape[-1]
        gsmall[nm] = lax.dynamic_slice_in_dim(gsmall[nm], dev * width, width, axis=2)

    grads, delta, new_m, new_v = {}, {}, {}, {}
    for nm in BIG_NAMES:
        grads[nm] = gbig[nm]
        delta[nm], new_m[nm], new_v[nm] = _adamw(w[nm], gbig[nm], m[nm], v[nm], name=f"adamw_{nm}")
    shapes = [w[nm].shape for nm in SMALL_NAMES]
    rows = _flat_rows(shapes)
    pk = lambda t: _pack_flat([t[nm] for nm in SMALL_NAMES], rows)
    d, nm_, nv_ = _adamw(pk(w), pk(gsmall), pk(m), pk(v), name="adamw_small")
    for nm, dd, mm_, vv_ in zip(SMALL_NAMES, _unpack_flat(d, shapes), _unpack_flat(nm_, shapes), _unpack_flat(nv_, shapes)):
        grads[nm], delta[nm], new_m[nm], new_v[nm] = gsmall[nm], dd, mm_, vv_

    return (loss, grad_x, *[grads[nm] for nm in WEIGHT_NAMES], *[delta[nm] for nm in WEIGHT_NAMES],
            *[new_m[nm] for nm in WEIGHT_NAMES], *[new_v[nm] for nm in WEIGHT_NAMES])


def kernel(x, p, norm1_g, w_in, a_ln_g, a_ln_b, a_ws, a_bs, b_conv_w, b_conv_b, b_wa, b_ba, b_wx, b_bx, b_lam, c_lb, c_norm_g, d_w, d_scale, w_out, norm2_g, w_up, ffn_conv_w, ffn_conv_b, w_down, norm3_g, w_pe, w_pg, final_g, loss_target, m_norm1_g, m_w_in, m_a_ln_g, m_a_ln_b, m_a_ws, m_a_bs, m_b_conv_w, m_b_conv_b, m_b_wa, m_b_ba, m_b_wx, m_b_bx, m_b_lam, m_c_lb, m_c_norm_g, m_d_w, m_d_scale, m_w_out, m_norm2_g, m_w_up, m_ffn_conv_w, m_ffn_conv_b, m_w_down, m_norm3_g, m_w_pe, m_w_pg, m_final_g, v_norm1_g, v_w_in, v_a_ln_g, v_a_ln_b, v_a_ws, v_a_bs, v_b_conv_w, v_b_conv_b, v_b_wa, v_b_ba, v_b_wx, v_b_bx, v_b_lam, v_c_lb, v_c_norm_g, v_d_w, v_d_scale, v_w_out, v_norm2_g, v_w_up, v_ffn_conv_w, v_ffn_conv_b, v_w_down, v_norm3_g, v_w_pe, v_w_pg, v_final_g):
    w = dict(norm1_g=norm1_g, w_in=w_in, a_ln_g=a_ln_g, a_ln_b=a_ln_b, a_ws=a_ws, a_bs=a_bs, b_conv_w=b_conv_w, b_conv_b=b_conv_b, b_wa=b_wa, b_ba=b_ba, b_wx=b_wx, b_bx=b_bx, b_lam=b_lam, c_lb=c_lb, c_norm_g=c_norm_g, d_w=d_w, d_scale=d_scale, w_out=w_out, norm2_g=norm2_g, w_up=w_up, ffn_conv_w=ffn_conv_w, ffn_conv_b=ffn_conv_b, w_down=w_down, norm3_g=norm3_g, w_pe=w_pe, w_pg=w_pg, final_g=final_g)
    m = dict(norm1_g=m_norm1_g, w_in=m_w_in, a_ln_g=m_a_ln_g, a_ln_b=m_a_ln_b, a_ws=m_a_ws, a_bs=m_a_bs, b_conv_w=m_b_conv_w, b_conv_b=m_b_conv_b, b_wa=m_b_wa, b_ba=m_b_ba, b_wx=m_b_wx, b_bx=m_b_bx, b_lam=m_b_lam, c_lb=m_c_lb, c_norm_g=m_c_norm_g, d_w=m_d_w, d_scale=m_d_scale, w_out=m_w_out, norm2_g=m_norm2_g, w_up=m_w_up, ffn_conv_w=m_ffn_conv_w, ffn_conv_b=m_ffn_conv_b, w_down=m_w_down, norm3_g=m_norm3_g, w_pe=m_w_pe, w_pg=m_w_pg, final_g=m_final_g)
    v = dict(norm1_g=v_norm1_g, w_in=v_w_in, a_ln_g=v_a_ln_g, a_ln_b=v_a_ln_b, a_ws=v_a_ws, a_bs=v_a_bs, b_conv_w=v_b_conv_w, b_conv_b=v_b_conv_b, b_wa=v_b_wa, b_ba=v_b_ba, b_wx=v_b_wx, b_bx=v_b_bx, b_lam=v_b_lam, c_lb=v_c_lb, c_norm_g=v_c_norm_g, d_w=v_d_w, d_scale=v_d_scale, w_out=v_w_out, norm2_g=v_norm2_g, w_up=v_w_up, ffn_conv_w=v_ffn_conv_w, ffn_conv_b=v_ffn_conv_b, w_down=v_w_down, norm3_g=v_norm3_g, w_pe=v_w_pe, w_pg=v_w_pg, final_g=v_final_g)
    return _step(w, m, v, x, p, loss_target)
```

```python
import functools

import jax
import jax.numpy as jnp
from jax import lax
from jax.experimental import pallas as pl
from jax.experimental.pallas import tpu as pltpu

F32 = jnp.float32
BF16 = jnp.bfloat16
MESH = pl.DeviceIdType.MESH

D_MODEL = 1024
DEPTH = 4
PLE_DIM = 256
W_GRP = 256
N_HEADS = 4
HEAD_DIM = 64
GMLP_CHUNK = 128
RGLRU_C = 8.0
HGRN_CHUNK = 64
HGRN_SUB = 16
POOL_WINDOWS = (2, 4, 8, 16)
D_FF = 2816
D_PROJ = 2304
EPS = 1e-6
ADAM_LR = 0.001
ADAM_B1 = 0.9
ADAM_B2 = 0.999
ADAM_EPS = 1e-08
ADAM_WD = 0.01
ADAM_STEP = 10

N_DEV = 8
MIB = 2 ** 20
V7X_VMEM_BYTES = 64 * MIB
HGRN_EXP_CLAMP = 60.0

WEIGHT_NAMES = ['norm1_g', 'w_in', 'a_ln_g', 'a_ln_b', 'a_ws', 'a_bs', 'b_conv_w', 'b_conv_b', 'b_wa', 'b_ba', 'b_wx',
                'b_bx', 'b_lam', 'c_lb', 'c_norm_g', 'd_w', 'd_scale', 'w_out', 'norm2_g', 'w_up', 'ffn_conv_w',
                'ffn_conv_b', 'w_down', 'norm3_g', 'w_pe', 'w_pg', 'final_g']
BIG_NAMES = ('w_in', 'w_out', 'w_up', 'w_down', 'w_pe', 'w_pg')


def _vmem_limit(block_bytes):
    want = 2 * block_bytes + 24 * MIB
    return int(min(max(want, 32 * MIB), V7X_VMEM_BYTES - 8 * MIB))


def _pcall(body, *, name, out_shape, grid=None, in_specs=None, out_specs=None, scratch_shapes=(),
           semantics=None, block_bytes=0):
    kw = {}
    if grid is not None:
        kw["grid"] = grid
    if in_specs is not None:
        kw["in_specs"] = in_specs
    if out_specs is not None:
        kw["out_specs"] = out_specs
    params = pltpu.CompilerParams(dimension_semantics=semantics, vmem_limit_bytes=_vmem_limit(block_bytes))
    return pl.pallas_call(body, name=name, out_shape=out_shape, scratch_shapes=list(scratch_shapes),
                          compiler_params=params, **kw)


def _pick(n, cands):
    for c in cands:
        if n % c == 0:
            return c
    return n


def _nbytes(shape, dtype):
    n = 1
    for s in shape:
        n *= s
    return n * jnp.dtype(dtype).itemsize


def _sds(shape, dtype):
    return jax.ShapeDtypeStruct(tuple(shape), dtype)


class _Sel:
    def __init__(self, arr, *idx):
        self.arr, self.idx = arr, tuple(idx)
        self.shape = arr.shape[len(idx):]
        self.ndim = len(self.shape)
        self.dtype = arr.dtype


def _arr(a):
    return a.arr if isinstance(a, _Sel) else a


def _spec(a, block=None, index=None):
    block = tuple(a.shape) if block is None else tuple(block)
    index = (lambda *g: (0,) * len(block)) if index is None else index
    if isinstance(a, _Sel):
        lead = a.idx
        return pl.BlockSpec((None,) * len(lead) + block, lambda *g: lead + tuple(index(*g)))
    return pl.BlockSpec(block, lambda *g: tuple(index(*g)))


def _ospec(a):
    return pl.BlockSpec(tuple(a.shape), lambda *g: (0,) * a.ndim)


def _rows_of(shape):
    return lax.broadcasted_iota(jnp.int32, shape, 0)


def _lanes_of(shape):
    return lax.broadcasted_iota(jnp.int32, shape, 1)


def _sdn(x, k, fill):
    n = x.shape[0]
    return jnp.where(_rows_of(x.shape) >= k, pltpu.roll(x, k % n, 0), fill)


def _sup(x, k, fill):
    n = x.shape[0]
    return jnp.where(_rows_of(x.shape) < n - k, pltpu.roll(x, (n - k) % n, 0), fill)


@functools.partial(jax.custom_vjp, nondiff_argnums=(1,))
def _shift_dn(x, k):
    return _sdn(x, k, 0.0)


def _shift_dn_fwd(x, k):
    return _sdn(x, k, 0.0), None


def _shift_dn_bwd(k, _, g):
    return (_sup(g, k, 0.0),)


_shift_dn.defvjp(_shift_dn_fwd, _shift_dn_bwd)


def _lin_scan_impl(a, b, h0):
    n = a.shape[0]
    aa, bb = a, b
    k = 1
    while k < n:
        bb = aa * _sdn(bb, k, 0.0) + bb
        aa = aa * _sdn(aa, k, 1.0)
        k *= 2
    return bb + aa * h0


@jax.custom_vjp
def _lin_scan(a, b, h0):
    return _lin_scan_impl(a, b, h0)


def _lin_scan_fwd(a, b, h0):
    h = _lin_scan_impl(a, b, h0)
    return h, (a, h, h0)


def _lin_scan_bwd(res, g):
    a, h, h0 = res
    n = a.shape[0]
    cc, gg = _sup(a, 1, 0.0), g
    k = 1
    while k < n:
        gg = gg + cc * _sup(gg, k, 0.0)
        cc = cc * _sup(cc, k, 1.0)
        k *= 2
    first = _rows_of(a.shape) == 0
    hprev = jnp.where(first, h0, _sdn(h, 1, 0.0))
    dh0 = jnp.sum(jnp.where(first, a * gg, 0.0), axis=0, keepdims=True)
    return gg * hprev, gg, dh0


_lin_scan.defvjp(_lin_scan_fwd, _lin_scan_bwd)


def _cumsum_sub_impl(x):
    pos = _rows_of(x.shape) % HGRN_SUB
    k = 1
    while k < HGRN_SUB:
        x = x + jnp.where(pos >= k, pltpu.roll(x, k, 0), 0.0)
        k *= 2
    return x


@jax.custom_vjp
def _cumsum_sub(x):
    return _cumsum_sub_impl(x)


def _cumsum_sub_fwd(x):
    return _cumsum_sub_impl(x), None


def _cumsum_sub_bwd(_, g):
    n = g.shape[0]
    pos = _rows_of(g.shape) % HGRN_SUB
    k = 1
    while k < HGRN_SUB:
        g = g + jnp.where(pos < HGRN_SUB - k, pltpu.roll(g, n - k, 0), 0.0)
        k *= 2
    return (g,)


_cumsum_sub.defvjp(_cumsum_sub_fwd, _cumsum_sub_bwd)


def _dot(a, b, ca, cb):
    return lax.dot_general(a.astype(BF16), b.astype(BF16), (((ca,), (cb,)), ((), ())), preferred_element_type=F32)


@jax.custom_vjp
def _mm(a, b):
    return _dot(a, b, 1, 0)


def _mm_fwd(a, b):
    return _dot(a, b, 1, 0), (a, b)


def _mm_bwd(res, g):
    a, b = res
    return _dot(g, b, 1, 1), _dot(a, g, 0, 0)


_mm.defvjp(_mm_fwd, _mm_bwd)


@jax.custom_vjp
def _mm_nt(a, b):
    return _dot(a, b, 1, 1)


def _mm_nt_fwd(a, b):
    return _dot(a, b, 1, 1), (a, b)


def _mm_nt_bwd(res, g):
    a, b = res
    return _dot(g, b, 1, 0), _dot(g, a, 0, 0)


_mm_nt.defvjp(_mm_nt_fwd, _mm_nt_bwd)


@jax.custom_vjp
def _mm_tn(a, b):
    return _dot(a, b, 0, 0)


def _mm_tn_fwd(a, b):
    return _dot(a, b, 0, 0), (a, b)


def _mm_tn_bwd(res, g):
    a, b = res
    return _dot(b, g, 1, 1), _dot(a, g, 1, 0)


_mm_tn.defvjp(_mm_tn_fwd, _mm_tn_bwd)


def _head_mask(shape, h):
    return (_lanes_of(shape) // HEAD_DIM) == h


def _stack_heads(x):
    return jnp.concatenate([jnp.where(_head_mask(x.shape, h), x, 0.0) for h in range(N_HEADS)], axis=0)


def _unstack_heads(p):
    r = p.shape[0] // N_HEADS
    out = None
    for h in range(N_HEADS):
        blk = p[h * r:(h + 1) * r]
        term = jnp.where(_head_mask(blk.shape, h), blk, 0.0)
        out = term if out is None else out + term
    return out


def _segmean_impl(x):
    n = x.shape[1]
    same = (lax.broadcasted_iota(jnp.int32, (n, n), 0) // HEAD_DIM) == (lax.broadcasted_iota(jnp.int32, (n, n), 1) // HEAD_DIM)
    m = jnp.where(same, 1.0 / HEAD_DIM, 0.0).astype(BF16)
    hi = x.astype(BF16)
    lo = (x - hi.astype(F32)).astype(BF16)
    dn = (((1,), (0,)), ((), ()))
    return (lax.dot_general(hi, m, dn, preferred_element_type=F32)
            + lax.dot_general(lo, m, dn, preferred_element_type=F32))


@jax.custom_vjp
def _segmean(x):
    return _segmean_impl(x)


def _segmean_fwd(x):
    return _segmean_impl(x), None


def _segmean_bwd(_, g):
    return (_segmean_impl(g),)


_segmean.defvjp(_segmean_fwd, _segmean_bwd)


def _log1p(u):
    w = 1.0 + u
    return jnp.where(w == 1.0, u, jnp.log(w) * (u / (w - 1.0)))


def _softplus(y):
    return jnp.maximum(y, 0.0) + _log1p(jnp.exp(-jnp.abs(y)))


def _rms(x, g):
    return x * lax.rsqrt(jnp.mean(x * x, axis=-1, keepdims=True) + EPS) * g


def _gmlp_chunk(zu, zv, ln_g, ln_b, wcat, bfull):
    u = jax.nn.gelu(zu)
    v = jax.nn.gelu(zv)
    mu = jnp.mean(v, axis=-1, keepdims=True)
    var = jnp.mean(jnp.square(v - mu), axis=-1, keepdims=True)
    vn = (v - mu) * lax.rsqrt(var + EPS) * ln_g + ln_b
    sv = _unstack_heads(_mm(wcat, vn)) + bfull
    return u * sv


def _rglru_tile(xb_ext, gb, h0, cw, cb, wa, ba, wx, bx, lam):
    xc = (cb + cw[0:1] * _shift_dn(xb_ext, 3) + cw[1:2] * _shift_dn(xb_ext, 2) + cw[2:3] * _shift_dn(xb_ext, 1)
          + cw[3:4] * xb_ext)[8:]
    r = jax.nn.sigmoid(_mm(xc, wa) + ba)
    i = jax.nn.sigmoid(_mm(xc, wx) + bx)
    log_a = (-RGLRU_C) * r * _softplus(-lam)
    a = jnp.exp(log_a)
    mult = jnp.sqrt(-jnp.tanh(log_a) * (a * a + 1.0))
    h = _lin_scan(a, mult * (i * xc), h0)
    y = h * jax.nn.gelu(gb)
    h_last = jnp.sum(jnp.where(_rows_of(h.shape) == h.shape[0] - 1, h, 0.0), axis=0, keepdims=True)
    return y, h_last


def _pool_tile(xd_ext, inv, wd, scale):
    s1 = xd_ext + _shift_dn(xd_ext, 1)
    s2 = s1 + _shift_dn(s1, 2)
    s3 = s2 + _shift_dn(s2, 4)
    s4 = s3 + _shift_dn(s3, 8)
    grp = _lanes_of(xd_ext.shape) // HEAD_DIM
    win = jnp.where(grp == 0, s1, jnp.where(grp == 1, s2, jnp.where(grp == 2, s3, s4)))
    pooled = win[16:] * inv - xd_ext[16:]
    return _mm(pooled, wd) * scale


def _hgrn_chunk(q, f, i, g, st, lb, ngf):
    n = q.shape[0]
    nsub = n // HGRN_SUB
    qs = jax.nn.silu(q)
    fg = lb + (1.0 - lb) * jax.nn.sigmoid(f)
    lf = jnp.log(fg)
    k = 1.0 - fg
    bl = _cumsum_sub(lf)
    row = _rows_of(q.shape)
    blk = row // HGRN_SUB
    betas = [jnp.zeros_like(lb)]
    for s in range(nsub):
        tot = jnp.sum(jnp.where(row == s * HGRN_SUB + HGRN_SUB - 1, bl, 0.0), axis=0, keepdims=True)
        betas.append(betas[-1] + tot)
    b_end = betas[nsub]
    beta_full = jnp.zeros_like(q)
    for s in range(1, nsub):
        beta_full = jnp.where(blk == s, betas[s], beta_full)
    qh = qs * jnp.exp(bl)
    qt = qh * jnp.exp(beta_full)
    b_all = beta_full + bl
    kt = k * jnp.exp(b_end - b_all)
    outs = []
    for s in range(nsub):
        kh = k * jnp.exp(jnp.minimum(betas[s] - b_all, HGRN_EXP_CLAMP))
        qstk = _stack_heads(qh[s * HGRN_SUB:(s + 1) * HGRN_SUB])
        att = _mm_nt(qstk, kh)
        ar = _rows_of(att.shape) % HGRN_SUB + s * HGRN_SUB
        att = jnp.where(_lanes_of(att.shape) <= ar, att, 0.0)
        outs.append(_unstack_heads(_mm(att, i)))
    o = jnp.concatenate(outs, axis=0) + _mm_nt(qt, st)
    same = (_rows_of(st.shape) // HEAD_DIM) == (_lanes_of(st.shape) // HEAD_DIM)
    st_new = st * jnp.exp(b_end) + jnp.where(same, _mm_tn(i, kt), 0.0)
    on = o * lax.rsqrt(_segmean(o * o) + EPS) * ngf
    return on * jax.nn.silu(g), st_new


def _ffn_tile(eg, ev, wg, bg, wv, bv):
    gt = (bg + wg[0:1] * _shift_dn(eg, 2) + wg[1:2] * _shift_dn(eg, 1) + wg[2:3] * eg)[8:]
    val = (bv + wv[0:1] * _shift_dn(ev, 2) + wv[1:2] * _shift_dn(ev, 1) + wv[2:3] * ev)[8:]
    return jax.nn.gelu(gt) * val


MXU_WIDTH = 256
MATMUL_BLOCK_BUDGET = 18 * MIB


def _matmul_tiles(m, k, n, a_dtype, b_dtype, out_dtype, has_res):
    best = None
    for tm in (2048, 1024, 512, 256):
        if m % tm:
            continue
        for tn in (1024, 768, 1408, 512, 256, 128):
            if n % tn:
                continue
            blk = (_nbytes((tm, k), a_dtype) + _nbytes((k, tn), b_dtype) + _nbytes((tm, tn), out_dtype)
                   + (_nbytes((tm, tn), F32) if has_res else 0))
            if blk > MATMUL_BLOCK_BUDGET:
                continue
            waste = -(-tn // MXU_WIDTH) * MXU_WIDTH / tn
            cost = (m // tm) * (n // tn) + 64 * (waste - 1.0)
            if best is None or cost < best[0]:
                best = (cost, tm, tn, blk)
    assert best is not None, (m, k, n)
    return best[1:]


def _matmul(a, b, *, name, nt=False, res=None, out_dtype=F32):
    m, k = a.shape
    n = b.shape[0] if nt else b.shape[1]
    tm, tn, blk = _matmul_tiles(m, k, n, a.dtype, b.dtype, out_dtype, res is not None)
    dims = (((1,), (1,)), ((), ())) if nt else (((1,), (0,)), ((), ()))

    def body(*refs):
        if res is None:
            a_ref, b_ref, o_ref = refs
        else:
            a_ref, b_ref, r_ref, o_ref = refs
        acc = lax.dot_general(a_ref[...], b_ref[...], dims, preferred_element_type=F32)
        if res is not None:
            acc = acc + r_ref[...]
        o_ref[...] = acc.astype(out_dtype)

    in_specs = [pl.BlockSpec((tm, k), lambda i, j: (i, 0)),
                _spec(b, (tn, k), lambda i, j: (j, 0)) if nt else _spec(b, (k, tn), lambda i, j: (0, j))]
    args = [a, _arr(b)]
    if res is not None:
        in_specs.append(pl.BlockSpec((tm, tn), lambda i, j: (i, j)))
        args.append(res)
    return _pcall(body, name=name, out_shape=_sds((m, n), out_dtype), grid=(m // tm, n // tn), in_specs=in_specs,
                  out_specs=pl.BlockSpec((tm, tn), lambda i, j: (i, j)), semantics=("parallel", "parallel"),
                  block_bytes=blk + _nbytes((tm, tn), F32))(*args)


def _matmul_tn(a, b, *, name, out_dtype=BF16):
    m, k1 = a.shape
    n = b.shape[1]
    tk = _pick(k1, (512, 256, 128))

    def body(a_ref, b_ref, o_ref):
        o_ref[...] = lax.dot_general(a_ref[...], b_ref[...], (((0,), (0,)), ((), ())),
                                     preferred_element_type=F32).astype(out_dtype)

    blk = 2 * _nbytes((m, tk), a.dtype) + _nbytes((m, n), b.dtype) + _nbytes((tk, n), F32)
    return _pcall(body, name=name, out_shape=_sds((k1, n), out_dtype), grid=(k1 // tk,),
                  in_specs=[pl.BlockSpec((m, tk), lambda i: (0, i)), pl.BlockSpec((m, n), lambda i: (0, 0))],
                  out_specs=pl.BlockSpec((tk, n), lambda i: (i, 0)), semantics=("parallel",),
                  block_bytes=blk)(a, b)


def _rms_fwd(x, g, *, name):
    s, d = x.shape
    tm = _pick(s, (512, 256))

    def body(x_ref, g_ref, o_ref):
        o_ref[...] = _rms(x_ref[...], g_ref[...]).astype(BF16)

    return _pcall(body, name=name, out_shape=_sds((s, d), BF16), grid=(s // tm,),
                  in_specs=[pl.BlockSpec((tm, d), lambda i: (i, 0)), _spec(g)],
                  out_specs=pl.BlockSpec((tm, d), lambda i: (i, 0)), semantics=("parallel",),
                  block_bytes=3 * _nbytes((tm, d), F32))(x, _arr(g))


def _rms_bwd(x, g, dh, dres, *, name):
    s, d = x.shape
    tm = _pick(s, (256, 128))

    def body(x_ref, g_ref, dh_ref, dr_ref, dx_ref, dxb_ref, dg_ref):
        _, vjp = jax.vjp(_rms, x_ref[...], g_ref[...])
        dxn, dg = vjp(dh_ref[...])
        dx = dr_ref[...] + dxn
        dx_ref[...] = dx
        dxb_ref[...] = dx.astype(BF16)

        @pl.when(pl.program_id(0) == 0)
        def _():
            dg_ref[...] = jnp.zeros_like(dg_ref)

        dg_ref[...] += dg

    row = pl.BlockSpec((tm, d), lambda i: (i, 0))
    vec = pl.BlockSpec((1, d), lambda i: (0, 0))
    return _pcall(body, name=name, out_shape=(_sds((s, d), F32), _sds((s, d), BF16), _sds((1, d), F32)),
                  grid=(s // tm,), in_specs=[row, _spec(g), row, row], out_specs=(row, row, vec),
                  semantics=("arbitrary",), block_bytes=8 * _nbytes((tm, d), F32))(x, _arr(g), dh, dres)


def _ple_fwd(x, gl, pe, *, name):
    s, d = x.shape
    tm = _pick(s, (512, 256))

    def body(x_ref, gl_ref, pe_ref, o_ref):
        o_ref[...] = x_ref[...] + pe_ref[...] * jax.nn.sigmoid(gl_ref[...])

    row = pl.BlockSpec((tm, d), lambda i: (i, 0))
    return _pcall(body, name=name, out_shape=_sds((s, d), F32), grid=(s // tm,), in_specs=[row, row, row],
                  out_specs=row, semantics=("parallel",), block_bytes=4 * _nbytes((tm, d), F32))(x, gl, pe)


def _ple_bwd(dx, gl, pe, *, name):
    s, d = dx.shape
    tm = _pick(s, (512, 256))

    def body(dx_ref, gl_ref, pe_ref, dpe_ref, dgl_ref):
        gate = jax.nn.sigmoid(gl_ref[...])
        dxv = dx_ref[...]
        dpe_ref[...] = (dxv * gate).astype(BF16)
        dgl_ref[...] = (dxv * pe_ref[...] * gate * (1.0 - gate)).astype(BF16)

    row = pl.BlockSpec((tm, d), lambda i: (i, 0))
    return _pcall(body, name=name, out_shape=(_sds((s, d), BF16), _sds((s, d), BF16)), grid=(s // tm,),
                  in_specs=[row, row, row], out_specs=(row, row), semantics=("parallel",),
                  block_bytes=5 * _nbytes((tm, d), F32))(dx, gl, pe)


def _loss_head(x, g, target, *, name):
    s, d = x.shape
    tm = _pick(s, (256, 128))

    def tile_loss(xv, gv, tv):
        err = jnp.square(_rms(xv, gv) - tv)
        return 0.5 * jnp.sum(jnp.mean(err, axis=-1, keepdims=True), axis=0, keepdims=True)

    def body(x_ref, g_ref, t_ref, l_ref, dx_ref, dg_ref):
        lv, vjp = jax.vjp(tile_loss, x_ref[...], g_ref[...], t_ref[...])
        dxv, dgv, _ = vjp(jnp.ones((1, 1), F32))
        dx_ref[...] = dxv

        @pl.when(pl.program_id(0) == 0)
        def _():
            l_ref[...] = jnp.zeros_like(l_ref)
            dg_ref[...] = jnp.zeros_like(dg_ref)

        l_ref[...] += jnp.broadcast_to(lv, l_ref.shape)
        dg_ref[...] += dgv

    row = pl.BlockSpec((tm, d), lambda i: (i, 0))
    vec = pl.BlockSpec((1, d), lambda i: (0, 0))
    return _pcall(body, name=name, out_shape=(_sds((8, 128), F32), _sds((s, d), F32), _sds((1, d), F32)),
                  grid=(s // tm,), in_specs=[row, vec, row],
                  out_specs=(pl.BlockSpec((8, 128), lambda i: (0, 0)), row, vec), semantics=("arbitrary",),
                  block_bytes=8 * _nbytes((tm, d), F32))(x, g, target)


def _acc_out(ref, val, first):
    @pl.when(first)
    def _():
        ref[...] = jnp.zeros_like(ref)

    ref[...] += val


def _gmlp_fwd(z, ln_g, ln_b, wcat, bfull, *, name):
    s = z.shape[0]
    t = _pick(s, (512, 256, 128))
    nch = t // GMLP_CHUNK

    def body(zu_ref, zv_ref, g_ref, b_ref, w_ref, bf_ref, o_ref):
        for c in range(nch):
            rows = pl.ds(c * GMLP_CHUNK, GMLP_CHUNK)
            o_ref[rows, :] = _gmlp_chunk(zu_ref[rows, :], zv_ref[rows, :], g_ref[...], b_ref[...], w_ref[...],
                                         bf_ref[...]).astype(BF16)

    col = lambda c: pl.BlockSpec((t, W_GRP), lambda i: (i, c))
    params = (ln_g, ln_b, wcat, bfull)
    return _pcall(body, name=name, out_shape=_sds((s, W_GRP), BF16), grid=(s // t,),
                  in_specs=[col(0), col(1)] + [_spec(a) for a in params],
                  out_specs=pl.BlockSpec((t, W_GRP), lambda i: (i, 0)), semantics=("parallel",),
                  block_bytes=4 * _nbytes((t, W_GRP), F32))(z, z, *[_arr(a) for a in params])


def _gmlp_bwd(z, dmix, ln_g, ln_b, wcat, bfull, *, name):
    s = z.shape[0]
    t = _pick(s, (512, 256, 128))
    nch = t // GMLP_CHUNK

    def body(zu_ref, zv_ref, dy_ref, g_ref, b_ref, w_ref, bf_ref, du_ref, dv_ref, dg_ref, db_ref, dw_ref, dbf_ref):
        acc = None
        for c in range(nch):
            rows = pl.ds(c * GMLP_CHUNK, GMLP_CHUNK)
            _, vjp = jax.vjp(_gmlp_chunk, zu_ref[rows, :], zv_ref[rows, :], g_ref[...], b_ref[...], w_ref[...],
                             bf_ref[...])
            du, dv, *dps = vjp(dy_ref[rows, :])
            du_ref[rows, :] = du.astype(BF16)
            dv_ref[rows, :] = dv.astype(BF16)
            acc = dps if acc is None else [x + y for x, y in zip(acc, dps)]
        first = pl.program_id(0) == 0
        for ref, val in zip((dg_ref, db_ref, dw_ref, dbf_ref), acc):
            _acc_out(ref, val, first)

    col = lambda c: pl.BlockSpec((t, W_GRP), lambda i: (i, c))
    params = (ln_g, ln_b, wcat, bfull)
    return _pcall(body, name=name,
                  out_shape=(_sds((s, W_GRP), BF16), _sds((s, W_GRP), BF16)) + tuple(_sds(a.shape, F32) for a in params),
                  grid=(s // t,), in_specs=[col(0), col(1), col(0)] + [_spec(a) for a in params],
                  out_specs=(col(0), col(0)) + tuple(_ospec(a) for a in params), semantics=("arbitrary",),
                  block_bytes=8 * _nbytes((t, W_GRP), F32))(z, z, dmix, *[_arr(a) for a in params])


def _rglru_fwd(z, prm, *, name):
    s = z.shape[0]
    t = _pick(s, (512, 256, 128))
    nt = s // t

    def body(xb_ref, halo_ref, gb_ref, *rest):
        prm_refs, (y_ref, h0s_ref, h_scr) = rest[:len(prm)], rest[len(prm):]
        i = pl.program_id(0)

        @pl.when(i == 0)
        def _():
            h_scr[...] = jnp.zeros_like(h_scr)

        halo = jnp.where(i == 0, 0.0, halo_ref[...])
        h0 = h_scr[...]
        y, h_last = _rglru_tile(jnp.concatenate([halo, xb_ref[...]], axis=0), gb_ref[...], h0,
                                *[r[...] for r in prm_refs])
        y_ref[...] = y.astype(BF16)
        h0s_ref[...] = jnp.broadcast_to(h0, h0s_ref.shape)
        h_scr[...] = h_last

    in_specs = [pl.BlockSpec((t, W_GRP), lambda i: (i, 2)),
                pl.BlockSpec((8, W_GRP), lambda i: (jnp.maximum(i * (t // 8) - 1, 0), 2)),
                pl.BlockSpec((t, W_GRP), lambda i: (i, 3))] + [_spec(a) for a in prm]
    return _pcall(body, name=name, out_shape=(_sds((s, W_GRP), BF16), _sds((nt, 8, W_GRP), F32)), grid=(nt,),
                  in_specs=in_specs,
                  out_specs=(pl.BlockSpec((t, W_GRP), lambda i: (i, 0)), pl.BlockSpec((None, 8, W_GRP), lambda i: (i, 0, 0))),
                  scratch_shapes=[pltpu.VMEM((1, W_GRP), F32)], semantics=("arbitrary",),
                  block_bytes=24 * _nbytes((t, W_GRP), F32))(z, z, z, *[_arr(a) for a in prm])


def _rglru_bwd(z, dmix, h0s, prm, *, name):
    s = z.shape[0]
    t = _pick(s, (512, 256, 128))
    nt = s // t
    npm = len(prm)

    def body(xb_ref, halo_ref, gb_ref, dy_ref, h0s_ref, *rest):
        prm_refs = rest[:npm]
        dxb_ref, dgb_ref = rest[npm:npm + 2]
        dprm_refs = rest[npm + 2:2 * npm + 2]
        dh_scr, dhalo_scr = rest[2 * npm + 2:]
        i = pl.program_id(0)
        r = nt - 1 - i

        @pl.when(i == 0)
        def _():
            dh_scr[...] = jnp.zeros_like(dh_scr)
            dhalo_scr[...] = jnp.zeros_like(dhalo_scr)

        halo = jnp.where(r == 0, 0.0, halo_ref[...])
        h0 = h0s_ref[0:1, :]
        _, vjp = jax.vjp(_rglru_tile, jnp.concatenate([halo, xb_ref[...]], axis=0), gb_ref[...], h0,
                         *[p[...] for p in prm_refs])
        dext, dgb, _dh0, *dps = vjp((dy_ref[...], dh_scr[...]))
        dmain = dext[8:]
        dxb = jnp.concatenate([dmain[:t - 8], dmain[t - 8:] + dhalo_scr[...]], axis=0)
        dxb_ref[...] = dxb.astype(BF16)
        dgb_ref[...] = dgb.astype(BF16)
        dh_scr[...] = _dh0
        dhalo_scr[...] = dext[:8]
        for ref, val in zip(dprm_refs, dps):
            _acc_out(ref, val, i == 0)

    rev = lambda c: pl.BlockSpec((t, W_GRP), lambda i: (nt - 1 - i, c))
    in_specs = [rev(2), pl.BlockSpec((8, W_GRP), lambda i: (jnp.maximum((nt - 1 - i) * (t // 8) - 1, 0), 2)), rev(3),
                rev(1), pl.BlockSpec((None, 8, W_GRP), lambda i: (nt - 1 - i, 0, 0))] + [_spec(a) for a in prm]
    return _pcall(body, name=name,
                  out_shape=(_sds((s, W_GRP), BF16), _sds((s, W_GRP), BF16)) + tuple(_sds(a.shape, F32) for a in prm),
                  grid=(nt,), in_specs=in_specs, out_specs=(rev(0), rev(0)) + tuple(_ospec(a) for a in prm),
                  scratch_shapes=[pltpu.VMEM((1, W_GRP), F32), pltpu.VMEM((8, W_GRP), F32)],
                  semantics=("arbitrary",), block_bytes=40 * _nbytes((t, W_GRP), F32))(z, z, z, dmix, h0s, *[_arr(a) for a in prm])


def _pool_inv(i, t):
    pos = (_rows_of((t, W_GRP)) + i * t + 1).astype(F32)
    grp = _lanes_of((t, W_GRP)) // HEAD_DIM
    win = jnp.where(grp == 0, float(POOL_WINDOWS[0]), jnp.where(grp == 1, float(POOL_WINDOWS[1]),
                    jnp.where(grp == 2, float(POOL_WINDOWS[2]), float(POOL_WINDOWS[3]))))
    return 1.0 / jnp.minimum(pos, win)


def _pool_fwd(z, wd, scale, *, name):
    s = z.shape[0]
    t = _pick(s, (512, 256, 128))

    def body(x_ref, halo_ref, wd_ref, sc_ref, y_ref):
        i = pl.program_id(0)
        halo = jnp.where(i == 0, 0.0, halo_ref[...])
        y = _pool_tile(jnp.concatenate([halo, x_ref[...]], axis=0), _pool_inv(i, t), wd_ref[...], sc_ref[...])
        y_ref[...] = y.astype(BF16)

    in_specs = [pl.BlockSpec((t, W_GRP), lambda i: (i, 8)),
                pl.BlockSpec((16, W_GRP), lambda i: (jnp.maximum(i * (t // 16) - 1, 0), 8)), _spec(wd), _spec(scale)]
    return _pcall(body, name=name, out_shape=_sds((s, W_GRP), BF16), grid=(s // t,), in_specs=in_specs,
                  out_specs=pl.BlockSpec((t, W_GRP), lambda i: (i, 0)), semantics=("parallel",),
                  block_bytes=12 * _nbytes((t, W_GRP), F32))(z, z, _arr(wd), _arr(scale))


def _pool_bwd(z, dmix, wd, scale, *, name):
    s = z.shape[0]
    t = _pick(s, (512, 256, 128))
    nt = s // t

    def body(x_ref, halo_ref, dy_ref, wd_ref, sc_ref, dx_ref, dwd_ref, dsc_ref, dhalo_scr):
        i = pl.program_id(0)
        r = nt - 1 - i

        @pl.when(i == 0)
        def _():
            dhalo_scr[...] = jnp.zeros_like(dhalo_scr)

        halo = jnp.where(r == 0, 0.0, halo_ref[...])
        inv = _pool_inv(r, t)
        _, vjp = jax.vjp(lambda e, w, sc: _pool_tile(e, inv, w, sc), jnp.concatenate([halo, x_ref[...]], axis=0),
                         wd_ref[...], sc_ref[...])
        dext, dwd, dsc = vjp(dy_ref[...])
        dmain = dext[16:]
        dx = jnp.concatenate([dmain[:t - 16], dmain[t - 16:] + dhalo_scr[...]], axis=0)
        dx_ref[...] = dx.astype(BF16)
        dhalo_scr[...] = dext[:16]
        _acc_out(dwd_ref, dwd, i == 0)
        _acc_out(dsc_ref, dsc, i == 0)

    rev = lambda c: pl.BlockSpec((t, W_GRP), lambda i: (nt - 1 - i, c))
    in_specs = [rev(8), pl.BlockSpec((16, W_GRP), lambda i: (jnp.maximum((nt - 1 - i) * (t // 16) - 1, 0), 8)), rev(3),
                _spec(wd), _spec(scale)]
    return _pcall(body, name=name, out_shape=(_sds((s, W_GRP), BF16), _sds(wd.shape, F32), _sds(scale.shape, F32)),
                  grid=(nt,), in_specs=in_specs, out_specs=(rev(0), _ospec(wd), _ospec(scale)),
                  scratch_shapes=[pltpu.VMEM((16, W_GRP), F32)], semantics=("arbitrary",),
                  block_bytes=20 * _nbytes((t, W_GRP), F32))(z, z, dmix, _arr(wd), _arr(scale))


def _hgrn_fwd(z, lb, ngf, *, name):
    s = z.shape[0]
    c = HGRN_CHUNK
    nc = s // c

    def body(q_ref, f_ref, i_ref, g_ref, lb_ref, ng_ref, y_ref, sts_ref, st_scr):
        @pl.when(pl.program_id(0) == 0)
        def _():
            st_scr[...] = jnp.zeros_like(st_scr)

        st = st_scr[...]
        sts_ref[...] = st
        y, st_new = _hgrn_chunk(q_ref[...], f_ref[...], i_ref[...], g_ref[...], st, lb_ref[...], ng_ref[...])
        y_ref[...] = y.astype(BF16)
        st_scr[...] = st_new

    col = lambda k: pl.BlockSpec((c, W_GRP), lambda i: (i, k))
    vec = pl.BlockSpec((1, W_GRP), lambda i: (0, 0))
    return _pcall(body, name=name, out_shape=(_sds((s, W_GRP), BF16), _sds((nc, W_GRP, W_GRP), F32)), grid=(nc,),
                  in_specs=[col(4), col(5), col(6), col(7), _spec(lb), _spec(ngf)],
                  out_specs=(pl.BlockSpec((c, W_GRP), lambda i: (i, 0)), pl.BlockSpec((None, W_GRP, W_GRP), lambda i: (i, 0, 0))),
                  scratch_shapes=[pltpu.VMEM((W_GRP, W_GRP), F32)], semantics=("arbitrary",),
                  block_bytes=16 * _nbytes((W_GRP, W_GRP), F32))(z, z, z, z, _arr(lb), _arr(ngf))


def _hgrn_bwd(z, dmix, sts, lb, ngf, *, name):
    s = z.shape[0]
    c = HGRN_CHUNK
    nc = s // c

    def body(q_ref, f_ref, i_ref, g_ref, dy_ref, st_ref, lb_ref, ng_ref, dz_ref, dlb_ref, dng_ref, dst_scr):
        i = pl.program_id(0)

        @pl.when(i == 0)
        def _():
            dst_scr[...] = jnp.zeros_like(dst_scr)

        _, vjp = jax.vjp(_hgrn_chunk, q_ref[...], f_ref[...], i_ref[...], g_ref[...], st_ref[...], lb_ref[...],
                         ng_ref[...])
        dq, df, di, dg, dst, dlb, dng = vjp((dy_ref[...], dst_scr[...]))
        dz_ref[...] = jnp.concatenate([dq, df, di, dg], axis=1).astype(BF16)
        dst_scr[...] = dst
        _acc_out(dlb_ref, dlb, i == 0)
        _acc_out(dng_ref, dng, i == 0)

    rev = lambda k: pl.BlockSpec((c, W_GRP), lambda i: (nc - 1 - i, k))
    vec = pl.BlockSpec((1, W_GRP), lambda i: (0, 0))
    return _pcall(body, name=name, out_shape=(_sds((s, 4 * W_GRP), BF16), _sds((1, W_GRP), F32), _sds((1, W_GRP), F32)),
                  grid=(nc,),
                  in_specs=[rev(4), rev(5), rev(6), rev(7), rev(2),
                            pl.BlockSpec((None, W_GRP, W_GRP), lambda i: (nc - 1 - i, 0, 0)), _spec(lb), _spec(ngf)],
                  out_specs=(pl.BlockSpec((c, 4 * W_GRP), lambda i: (nc - 1 - i, 0)), vec, vec),
                  scratch_shapes=[pltpu.VMEM((W_GRP, W_GRP), F32)], semantics=("arbitrary",),
                  block_bytes=32 * _nbytes((W_GRP, W_GRP), F32))(z, z, z, z, dmix, sts, _arr(lb), _arr(ngf))


def _lbs_fwd(c_lb, *, name):
    def body(c_ref, o_ref):
        c = c_ref[...]
        e = jnp.exp(c - jnp.max(c, axis=0, keepdims=True))
        sm = e / jnp.sum(e, axis=0, keepdims=True)
        run = jnp.zeros((1, W_GRP), F32)
        o_ref[0:1, :] = run
        for l in range(1, DEPTH):
            run = run + sm[l:l + 1]
            o_ref[l:l + 1, :] = run

    return _pcall(body, name=name, out_shape=_sds((DEPTH, W_GRP), F32))(c_lb)


def _lbs_bwd(c_lb, dlbs, *, name):
    def body(c_ref, d_ref, o_ref):
        c = c_ref[...]
        e = jnp.exp(c - jnp.max(c, axis=0, keepdims=True))
        sm = e / jnp.sum(e, axis=0, keepdims=True)
        d = d_ref[...]
        dsm = [None] * DEPTH
        run = jnp.zeros((1, W_GRP), F32)
        for l in range(DEPTH - 1, 0, -1):
            run = run + d[l:l + 1]
            dsm[l] = run
        dsm[0] = jnp.zeros((1, W_GRP), F32)
        inner = sum(sm[l:l + 1] * dsm[l] for l in range(DEPTH))
        for l in range(DEPTH):
            o_ref[l:l + 1, :] = sm[l:l + 1] * (dsm[l] - inner)

    return _pcall(body, name=name, out_shape=_sds((DEPTH, W_GRP), F32))(c_lb, dlbs)


def _ffn_fwd(hg, hv, cwf, cbf, *, name):
    s, n = hg.shape
    t = _pick(s, (256, 128))
    cw = _pick(n, (1408, 256, 128))
    nj = n // cw

    def body(g_ref, gh_ref, v_ref, vh_ref, wg_ref, bg_ref, wv_ref, bv_ref, o_ref):
        first = pl.program_id(1) == 0
        eg = jnp.concatenate([jnp.where(first, 0.0, gh_ref[...]), g_ref[...]], axis=0)
        ev = jnp.concatenate([jnp.where(first, 0.0, vh_ref[...]), v_ref[...]], axis=0)
        o_ref[...] = _ffn_tile(eg, ev, wg_ref[...], bg_ref[...], wv_ref[...], bv_ref[...]).astype(BF16)

    main = pl.BlockSpec((t, cw), lambda j, i: (i, j))
    halo = pl.BlockSpec((8, cw), lambda j, i: (jnp.maximum(i * (t // 8) - 1, 0), j))
    taps = lambda off: _spec(cwf, (3, cw), lambda j, i: (0, j + off))
    bias = lambda off: _spec(cbf, (1, cw), lambda j, i: (0, j + off))
    return _pcall(body, name=name, out_shape=_sds((s, n), BF16), grid=(nj, s // t),
                  in_specs=[main, halo, main, halo, taps(0), bias(0), taps(nj), bias(nj)], out_specs=main,
                  semantics=("parallel", "parallel"), block_bytes=12 * _nbytes((t, cw), F32))(
                      hg, hg, hv, hv, _arr(cwf), _arr(cbf), _arr(cwf), _arr(cbf))


def _ffn_bwd(hg, hv, da, cwf, cbf, *, name):
    s, n = hg.shape
    t = _pick(s, (256, 128))
    cw = _pick(n, (1408, 256, 128))
    nt = s // t
    nj = n // cw

    def body(g_ref, gh_ref, v_ref, vh_ref, da_ref, wg_ref, bg_ref, wv_ref, bv_ref, dg_ref, dv_ref, dwg_ref, dwv_ref,
             cg_scr, cv_scr):
        i = pl.program_id(1)
        r = nt - 1 - i

        @pl.when(i == 0)
        def _():
            cg_scr[...] = jnp.zeros_like(cg_scr)
            cv_scr[...] = jnp.zeros_like(cv_scr)

        eg = jnp.concatenate([jnp.where(r == 0, 0.0, gh_ref[...]), g_ref[...]], axis=0)
        ev = jnp.concatenate([jnp.where(r == 0, 0.0, vh_ref[...]), v_ref[...]], axis=0)
        _, vjp = jax.vjp(_ffn_tile, eg, ev, wg_ref[...], bg_ref[...], wv_ref[...], bv_ref[...])
        deg, dev, dwg, dbg, dwv, dbv = vjp(da_ref[...])
        for dext, scr, ref in ((deg, cg_scr, dg_ref), (dev, cv_scr, dv_ref)):
            dmain = dext[8:]
            ref[...] = jnp.concatenate([dmain[:t - 8], dmain[t - 8:] + scr[...]], axis=0).astype(BF16)
            scr[...] = dext[:8]
        zeros = jnp.zeros((4, cw), F32)
        _acc_out(dwg_ref, jnp.concatenate([dwg, dbg, zeros], axis=0), i == 0)
        _acc_out(dwv_ref, jnp.concatenate([dwv, dbv, zeros], axis=0), i == 0)

    main = pl.BlockSpec((t, cw), lambda j, i: (nt - 1 - i, j))
    halo = pl.BlockSpec((8, cw), lambda j, i: (jnp.maximum((nt - 1 - i) * (t // 8) - 1, 0), j))
    taps = lambda off: _spec(cwf, (3, cw), lambda j, i: (0, j + off))
    bias = lambda off: _spec(cbf, (1, cw), lambda j, i: (0, j + off))
    w8 = pl.BlockSpec((8, cw), lambda j, i: (0, j))
    return _pcall(body, name=name,
                  out_shape=(_sds((s, n), BF16), _sds((s, n), BF16), _sds((8, n), F32), _sds((8, n), F32)),
                  grid=(nj, nt), in_specs=[main, halo, main, halo, main, taps(0), bias(0), taps(nj), bias(nj)],
                  out_specs=(main, main, w8, w8),
                  scratch_shapes=[pltpu.VMEM((8, cw), F32), pltpu.VMEM((8, cw), F32)],
                  semantics=("parallel", "arbitrary"), block_bytes=24 * _nbytes((t, cw), F32))(
                      hg, hg, hv, hv, da, _arr(cwf), _arr(cbf), _arr(cwf), _arr(cbf))


def _all_gather(x, *, name):
    r, c = x.shape

    def body(x_ref, out_ref, send_sems, recv_sems, local_sem):
        mx, my, mc = lax.axis_index("x"), lax.axis_index("y"), lax.axis_index("c")
        me, sibling = (mx, my, mc), (mx, my, 1 - mc)
        chips = [(1 - mx, my), (mx, 1 - my), (1 - mx, 1 - my)]

        def slot(px, py, pc):
            return out_ref.at[4 * px + 2 * py + pc]

        def copy(k, block, to, src=None):
            return pltpu.make_async_remote_copy(src_ref=slot(*block) if src is None else src, dst_ref=slot(*block),
                                                send_sem=send_sems.at[k], recv_sem=recv_sems.at[k],
                                                device_id=to, device_id_type=MESH)

        mine = pltpu.make_async_copy(x_ref, slot(*me), local_sem)
        mine.start()
        first = [copy(0, me, sibling, src=x_ref)]
        first += [copy(1 + j, me, (*chip, mc), src=x_ref) for j, chip in enumerate(chips)]
        for cp in first:
            cp.start()
        passed = [copy(4 + j, (*chip, mc), sibling) for j, chip in enumerate(chips)]
        for j, chip in enumerate(chips):
            copy(1 + j, (*chip, mc), me).wait_recv()
            passed[j].start()
        copy(0, sibling, me).wait_recv()
        for j, chip in enumerate(chips):
            copy(4 + j, (*chip, 1 - mc), me).wait_recv()
        for cp in first + passed:
            cp.wait_send()
        mine.wait()

    hbm = pl.BlockSpec(memory_space=pl.ANY)
    return _pcall(body, name=name, out_shape=_sds((N_DEV, r, c), x.dtype), in_specs=[hbm], out_specs=hbm,
                  scratch_shapes=[pltpu.SemaphoreType.DMA((7,)), pltpu.SemaphoreType.DMA((7,)),
                                  pltpu.SemaphoreType.DMA(())])(x)


def _sum_slots(p, *, name):
    q, r, c = p.shape
    tr = _pick(r, (544, 408, 272, 192, 136, 64, 32, 16, 8))

    def body(p_ref, o_ref):
        acc = p_ref[0].astype(F32)
        for k in range(1, q):
            acc = acc + p_ref[k].astype(F32)
        o_ref[...] = acc

    return _pcall(body, name=name, out_shape=_sds((r, c), F32), grid=(r // tr,),
                  in_specs=[pl.BlockSpec((q, tr, c), lambda i: (0, i, 0))],
                  out_specs=pl.BlockSpec((tr, c), lambda i: (i, 0)), semantics=("parallel",),
                  block_bytes=(q + 2) * _nbytes((tr, c), F32))(p)


BIG_COMM = (('w_in', 288, D_MODEL), ('w_out', 128, D_MODEL), ('w_up', 704, D_MODEL), ('w_down', 352, D_MODEL),
            ('w_pe', 128, PLE_DIM), ('w_pg', 128, D_MODEL))
HBM_SPEC = pl.BlockSpec(memory_space=pl.ANY)


def _gather_layer(shards, l, *, name):
    na = len(shards)

    def body(*refs):
        x_refs, out_refs = refs[:na], refs[na:2 * na]
        send_sems, recv_sems, local_sems = refs[2 * na:]
        mx, my, mc = lax.axis_index("x"), lax.axis_index("y"), lax.axis_index("c")
        me, sibling = (mx, my, mc), (mx, my, 1 - mc)
        chips = [(1 - mx, my), (mx, 1 - my), (1 - mx, 1 - my)]

        def slot(a, px, py, pc):
            return out_refs[a].at[4 * px + 2 * py + pc]

        def copy(k, a, block, to, own=False):
            return pltpu.make_async_remote_copy(src_ref=x_refs[a].at[l] if own else slot(a, *block),
                                                dst_ref=slot(a, *block), send_sem=send_sems.at[k, a],
                                                recv_sem=recv_sems.at[k, a], device_id=to, device_id_type=MESH)

        mine = [pltpu.make_async_copy(x_refs[a].at[l], slot(a, *me), local_sems.at[a]) for a in range(na)]
        for cp in mine:
            cp.start()
        first = []
        for a in range(na):
            first.append(copy(0, a, me, sibling, own=True))
            first += [copy(1 + j, a, me, (*chip, mc), own=True) for j, chip in enumerate(chips)]
        for cp in first:
            cp.start()
        passed = []
        for j, chip in enumerate(chips):
            for a in range(na):
                copy(1 + j, a, (*chip, mc), me).wait_recv()
                fwd = copy(4 + j, a, (*chip, mc), sibling)
                fwd.start()
                passed.append(fwd)
        for a in range(na):
            copy(0, a, sibling, me).wait_recv()
        for j, chip in enumerate(chips):
            for a in range(na):
                copy(4 + j, a, (*chip, 1 - mc), me).wait_recv()
        for cp in first + passed:
            cp.wait_send()
        for cp in mine:
            cp.wait()

    return _pcall(body, name=name, out_shape=tuple(_sds((N_DEV,) + x.shape[1:], x.dtype) for x in shards),
                  in_specs=[HBM_SPEC] * na, out_specs=(HBM_SPEC,) * na,
                  scratch_shapes=[pltpu.SemaphoreType.DMA((7, na)), pltpu.SemaphoreType.DMA((7, na)),
                                  pltpu.SemaphoreType.DMA((na,))])(*shards)


SEM_SPEC = pl.BlockSpec(memory_space=pltpu.SEMAPHORE)
DATAFLOW_EFFECT = pltpu.SideEffectType.DATAFLOW_SIDE_EFFECTING


def _place_own(srcs, after, *, name, per_peer=False):
    na = len(srcs)

    def body(*refs):
        x_refs, land_refs, sems = refs[:na], refs[na + len(after):2 * na + len(after)], refs[-1]
        me = 4 * lax.axis_index("x") + 2 * lax.axis_index("y") + lax.axis_index("c")
        cps = [pltpu.make_async_copy(x_refs[a].at[me] if per_peer else x_refs[a], land_refs[a].at[me], sems.at[a])
               for a in range(na)]
        for cp in cps:
            cp.start()
        for cp in cps:
            cp.wait()

    shape = lambda x: x.shape if per_peer else (N_DEV,) + x.shape
    return _pcall(body, name=name, out_shape=tuple(_sds(shape(x), x.dtype) for x in srcs),
                  in_specs=[HBM_SPEC] * (na + len(after)), out_specs=(HBM_SPEC,) * na,
                  scratch_shapes=[pltpu.SemaphoreType.DMA((na,))])(*srcs, *after)


def _exchange_start(srcs, lands, *, name, per_peer=False):
    na = len(srcs)

    def body(*refs):
        x_refs, land_refs = refs[:na], refs[na:2 * na]
        send_sems, recv_sems = refs[2 * na], refs[2 * na + 1]
        token = refs[-1]
        mx, my, mc = lax.axis_index("x"), lax.axis_index("y"), lax.axis_index("c")
        me = 4 * mx + 2 * my + mc
        peers = [(mx, my, 1 - mc)]
        for px, py in ((1 - mx, my), (mx, 1 - my), (1 - mx, 1 - my)):
            peers += [(px, py, mc), (px, py, 1 - mc)]
        for a in range(na):
            for peer in peers:
                src = x_refs[a].at[4 * peer[0] + 2 * peer[1] + peer[2]] if per_peer else x_refs[a]
                pltpu.make_async_remote_copy(src_ref=src, dst_ref=land_refs[a].at[me], send_sem=send_sems.at[a],
                                             recv_sem=recv_sems.at[a], device_id=peer, device_id_type=MESH).start()
        token[...] = jnp.zeros_like(token)

    hbm = lambda x: pltpu.HBM(x.shape, x.dtype)
    out_shape = ((pltpu.SemaphoreType.DMA((na,)), pltpu.SemaphoreType.DMA((na,))) + tuple(hbm(x) for x in srcs)
                 + tuple(hbm(x) for x in lands) + (_sds((8, 128), F32),))
    params = pltpu.CompilerParams(has_side_effects=DATAFLOW_EFFECT)
    pin = lambda x: pltpu.with_memory_space_constraint(x, pltpu.HBM)
    return pl.pallas_call(body, name=name, out_shape=out_shape, in_specs=[HBM_SPEC] * (2 * na),
                          out_specs=(SEM_SPEC, SEM_SPEC) + (HBM_SPEC,) * (2 * na) + (pl.BlockSpec(memory_space=pltpu.VMEM),),
                          input_output_aliases={i: 2 + i for i in range(2 * na)}, compiler_params=params)(
                              *[pin(x) for x in srcs], *[pin(x) for x in lands])


def _exchange_wait(started, after, *, name):
    send_sems, recv_sems, *bufs, _ = started
    na = len(bufs) // 2

    def body(*refs):
        land_refs = refs[na:2 * na]
        s_sems, r_sems = refs[2 * na], refs[2 * na + 1]
        me = (lax.axis_index("x"), lax.axis_index("y"), lax.axis_index("c"))
        for a in range(na):
            seven = land_refs[a].at[pl.ds(0, N_DEV - 1)]
            cp = pltpu.make_async_remote_copy(src_ref=seven, dst_ref=seven, send_sem=s_sems.at[a], recv_sem=r_sems.at[a],
                                              device_id=me, device_id_type=MESH)
            cp.wait_send()
            cp.wait_recv()

    hbm = lambda x: pltpu.HBM(x.shape, x.dtype)
    params = pltpu.CompilerParams(has_side_effects=DATAFLOW_EFFECT)
    outs = pl.pallas_call(body, name=name, out_shape=tuple(hbm(x) for x in bufs),
                          in_specs=[HBM_SPEC] * (2 * na) + [SEM_SPEC, SEM_SPEC, HBM_SPEC],
                          out_specs=(HBM_SPEC,) * (2 * na), input_output_aliases={i: i for i in range(2 * na)},
                          compiler_params=params)(*bufs, send_sems, recv_sems, after)
    return outs[na:]


def _pair_swap(grads, *, name):
    na = len(grads)

    def body(*refs):
        g_refs, recv_refs = refs[:na], refs[na:2 * na]
        send_sems, recv_sems = refs[2 * na:]
        mx, my, mc = lax.axis_index("x"), lax.axis_index("y"), lax.axis_index("c")
        sibling = (mx, my, 1 - mc)
        for a in range(na):
            for q in range(4):
                pltpu.make_async_remote_copy(src_ref=g_refs[a].at[q, 1 - mc], dst_ref=recv_refs[a].at[q],
                                             send_sem=send_sems.at[a], recv_sem=recv_sems.at[a],
                                             device_id=sibling, device_id_type=MESH).start()
        for a in range(na):
            pltpu.make_async_remote_copy(src_ref=recv_refs[a], dst_ref=recv_refs[a], send_sem=send_sems.at[a],
                                         recv_sem=recv_sems.at[a], device_id=sibling, device_id_type=MESH).wait()

    half = tuple(_sds((4,) + g.shape[2:], g.dtype) for g in grads)
    return _pcall(body, name=name, out_shape=half, in_specs=[HBM_SPEC] * na, out_specs=(HBM_SPEC,) * na,
                  scratch_shapes=[pltpu.SemaphoreType.DMA((na,)), pltpu.SemaphoreType.DMA((na,))])(*grads)


def _add_slabs(grads, recv, core, *, name):
    na = len(grads)

    def body(core_ref, *refs):
        for a in range(na):
            refs[2 * na + a][...] = (refs[a][...].astype(F32) + refs[na + a][...].astype(F32)).astype(BF16)

    own_specs = [pl.BlockSpec((None, None) + x.shape[2:], lambda q, core_ref: (q, core_ref[0], 0, 0)) for x in grads]
    specs = [pl.BlockSpec((None,) + x.shape[1:], lambda q, core_ref: (q, 0, 0)) for x in recv]
    blk = sum(_nbytes(x.shape[1:], F32) for x in recv)
    grid_spec = pltpu.PrefetchScalarGridSpec(num_scalar_prefetch=1, grid=(4,), in_specs=own_specs + specs,
                                             out_specs=tuple(specs))
    params = pltpu.CompilerParams(dimension_semantics=("parallel",), vmem_limit_bytes=_vmem_limit(2 * blk))
    return pl.pallas_call(body, name=name, out_shape=tuple(_sds(x.shape, BF16) for x in recv), grid_spec=grid_spec,
                          compiler_params=params)(core, *grads, *recv)


def _chip_exchange(parts, *, name):
    na = len(parts)

    def body(*refs):
        p_refs, out_refs = refs[:na], refs[na:2 * na]
        send_sems, recv_sems, local_sems = refs[2 * na:]
        mx, my, mc = lax.axis_index("x"), lax.axis_index("y"), lax.axis_index("c")
        mine_q = 2 * mx + my
        chips = [(1 - mx, my), (mx, 1 - my), (1 - mx, 1 - my)]
        owns = [pltpu.make_async_copy(p_refs[a].at[mine_q], out_refs[a].at[mine_q], local_sems.at[a]) for a in range(na)]
        for cp in owns:
            cp.start()
        sends = []
        for a in range(na):
            for k, chip in enumerate(chips):
                sends.append(pltpu.make_async_remote_copy(
                    src_ref=p_refs[a].at[2 * chip[0] + chip[1]], dst_ref=out_refs[a].at[mine_q],
                    send_sem=send_sems.at[k, a], recv_sem=recv_sems.at[k, a], device_id=(*chip, mc), device_id_type=MESH))
        for cp in sends:
            cp.start()
        for a in range(na):
            for k, chip in enumerate(chips):
                pltpu.make_async_remote_copy(
                    src_ref=p_refs[a].at[mine_q], dst_ref=out_refs[a].at[2 * chip[0] + chip[1]],
                    send_sem=send_sems.at[k, a], recv_sem=recv_sems.at[k, a], device_id=(*chip, mc),
                    device_id_type=MESH).wait_recv()
        for cp in sends:
            cp.wait_send()
        for cp in owns:
            cp.wait()

    return _pcall(body, name=name, out_shape=tuple(_sds(x.shape, x.dtype) for x in parts), in_specs=[HBM_SPEC] * na,
                  out_specs=(HBM_SPEC,) * na,
                  scratch_shapes=[pltpu.SemaphoreType.DMA((3, na)), pltpu.SemaphoreType.DMA((3, na)),
                                  pltpu.SemaphoreType.DMA((na,))])(*parts)


def _sum_chips(parts, *, name):
    na = len(parts)

    def body(*refs):
        for a in range(na):
            p_ref = refs[a]
            acc = p_ref[0].astype(F32)
            for k in range(1, p_ref.shape[0]):
                acc = acc + p_ref[k].astype(F32)
            refs[na + a][...] = acc

    half = lambda x: x.shape[1] // 2
    in_specs = [pl.BlockSpec((x.shape[0], half(x), x.shape[2]), lambda i: (0, i, 0)) for x in parts]
    out_specs = tuple(pl.BlockSpec((half(x), x.shape[2]), lambda i: (i, 0)) for x in parts)
    blk = sum(_nbytes((x.shape[0] + 2, half(x), x.shape[2]), BF16) for x in parts)
    return _pcall(body, name=name, out_shape=tuple(_sds(x.shape[1:], F32) for x in parts), grid=(2,),
                  in_specs=in_specs, out_specs=out_specs, semantics=("parallel",), block_bytes=blk)(*parts)


def _reduce_layer(grads, l):
    n = lambda s: f"l{l}_{s}"
    views = [g.reshape(4, 2, g.shape[0] // N_DEV, g.shape[1]) for g in grads]
    recv = _pair_swap(views, name=n("reduce_pair"))
    core = lax.axis_index("c").astype(jnp.int32).reshape(1)
    chip_sum = _add_slabs(views, recv, core, name=n("reduce_pair_add"))
    from_chips = _chip_exchange(chip_sum, name=n("reduce_chips"))
    return _sum_chips(from_chips, name=n("reduce_chips_add"))


def _adamw(w, g, m, v, *, name):
    lead, (r, c) = w.shape[:-2], w.shape[-2:]
    tr = _pick(r, (512, 256, 192, 128, 64, 32, 16, 8))
    c1 = 1.0 / (1.0 - ADAM_B1 ** ADAM_STEP)
    c2 = 1.0 / (1.0 - ADAM_B2 ** ADAM_STEP)

    def body(w_ref, g_ref, m_ref, v_ref, d_ref, nm_ref, nv_ref):
        gv = g_ref[...]
        nm = ADAM_B1 * m_ref[...] + (1.0 - ADAM_B1) * gv
        nv = ADAM_B2 * v_ref[...] + (1.0 - ADAM_B2) * jnp.square(gv)
        d_ref[...] = -ADAM_LR * ((nm * c1) / (jnp.sqrt(nv * c2) + ADAM_EPS) + ADAM_WD * w_ref[...])
        nm_ref[...] = nm
        nv_ref[...] = nv

    if lead:
        blk = pl.BlockSpec((None, tr, c), lambda k, i: (k, i, 0))
        grid, sem = (lead[0], r // tr), ("parallel", "parallel")
    else:
        blk = pl.BlockSpec((tr, c), lambda i: (i, 0))
        grid, sem = (r // tr,), ("parallel",)
    out = _sds(w.shape, F32)
    return _pcall(body, name=name, out_shape=(out, out, out), grid=grid, in_specs=[blk] * 4,
                  out_specs=(blk, blk, blk), semantics=sem, block_bytes=7 * _nbytes((tr, c), F32))(w, g, m, v)


def _pack_flat(arrs, rows, cols=1024):
    flat = jnp.concatenate([a.reshape(-1).astype(F32) for a in arrs])
    pad = rows * cols - flat.shape[0]
    return jnp.pad(flat, (0, pad)).reshape(rows, cols)


def _unpack_flat(buf, shapes):
    flat = buf.reshape(-1)
    out, off = [], 0
    for shp in shapes:
        n = 1
        for s in shp:
            n *= s
        out.append(flat[off:off + n].reshape(shp))
        off += n
    return out


def _flat_rows(shapes, cols=1024):
    n = sum(functools.reduce(lambda a, b: a * b, shp, 1) for shp in shapes)
    rows = -(-n // cols)
    return -(-rows // 64) * 64


def _block_diag(w):
    eye = jnp.eye(N_HEADS, dtype=w.dtype)
    return (w[:, :, :, None, :] * eye[None, :, None, :, None]).reshape(w.shape[0], W_GRP, W_GRP)


def _diag_blocks(w):
    w5 = w.reshape(w.shape[0], N_HEADS, HEAD_DIM, N_HEADS, HEAD_DIM)
    return jnp.stack([w5[:, h, :, h, :] for h in range(N_HEADS)], axis=1)


def _stacked_params(w, lbs):
    tril = jnp.tril(jnp.ones((GMLP_CHUNK, GMLP_CHUNK), bool))
    row = lambda a: a.reshape(DEPTH, 1, -1)
    return dict(
        g1=row(w['norm1_g']), g2=row(w['norm2_g']), g3=row(w['norm3_g']),
        a_ln_g=row(w['a_ln_g']), a_ln_b=row(w['a_ln_b']),
        a_wcat=jnp.where(tril, w['a_ws'], 0.0).reshape(DEPTH, N_HEADS * GMLP_CHUNK, GMLP_CHUNK),
        a_bfull=jnp.repeat(jnp.swapaxes(w['a_bs'], 1, 2), HEAD_DIM, axis=2),
        b_cw=w['b_conv_w_full'], b_cb=row(w['b_conv_b']), b_wa=_block_diag(w['b_wa']), b_ba=row(w['b_ba']),
        b_wx=_block_diag(w['b_wx']), b_bx=row(w['b_bx']), b_lam=row(w['b_lam']),
        c_lb=row(lbs), c_ngf=row(jnp.tile(w['c_norm_g'], (1, N_HEADS))),
        d_wd=_block_diag(w['d_w']), d_scale=row(w['d_scale']),
        f_cw=w['ffn_conv_w_full'], f_cb=row(w['ffn_conv_b']),
    )


B_PRM = ('b_cw', 'b_cb', 'b_wa', 'b_ba', 'b_wx', 'b_bx', 'b_lam')


def _layer_fwd(x, p_bf, wb, sp, l):
    n = lambda s: f"l{l}_{s}"
    h = _rms_fwd(x, sp['g1'], name=n("norm1"))
    z = _matmul(h, wb['w_in'], nt=True, name=n("proj_in"))
    ya = _gmlp_fwd(z, sp['a_ln_g'], sp['a_ln_b'], sp['a_wcat'], sp['a_bfull'], name=n("gmlp"))
    yb, h0s = _rglru_fwd(z, [sp[k] for k in B_PRM], name=n("rglru"))
    yc, sts = _hgrn_fwd(z, sp['c_lb'], sp['c_ngf'], name=n("hgrn"))
    yd = _pool_fwd(z, sp['d_wd'], sp['d_scale'], name=n("pool"))
    mix = jnp.concatenate([ya, yb, yc, yd], axis=1)
    x1 = _matmul(mix, wb['w_out'], res=x, name=n("proj_out"))
    h2 = _rms_fwd(x1, sp['g2'], name=n("norm2"))
    hg = _matmul(h2, wb['w_up_g'], nt=True, name=n("up_gate"))
    hv = _matmul(h2, wb['w_up_v'], nt=True, name=n("up_val"))
    a = _ffn_fwd(hg, hv, sp['f_cw'], sp['f_cb'], name=n("ffn_gate"))
    x2 = _matmul(a, wb['w_down'], res=x1, name=n("down"))
    h3 = _rms_fwd(x2, sp['g3'], name=n("norm3"))
    gl = _matmul(h3, wb['w_pg'], name=n("ple_gate"))
    pe = _matmul(p_bf, wb['w_pe'], nt=True, name=n("ple_emb"))
    x3 = _ple_fwd(x2, gl, pe, name=n("ple"))
    saved = dict(x=x, h=h, z=z, h0s=h0s, sts=sts, mix=mix, x1=x1, h2=h2, hg=hg, hv=hv, a=a, x2=x2, h3=h3, gl=gl, pe=pe)
    return x3, saved


def _layer_bwd(dx3, sv, p_bf, wb, sp, l):
    n = lambda s: f"l{l}_{s}_bwd"
    gb, gs = {}, {}
    dpe, dgl = _ple_bwd(dx3, sv['gl'], sv['pe'], name=n("ple"))
    gb['w_pe'] = _matmul_tn(dpe, p_bf, name=n("ple_emb_w"))
    gb['w_pg'] = _matmul_tn(sv['h3'], dgl, name=n("ple_gate_w"))
    dh3 = _matmul(dgl, wb['w_pg'], nt=True, name=n("ple_gate_x"))
    dx2, dx2b, gs['norm3_g'] = _rms_bwd(sv['x2'], sp['g3'], dh3, dx3, name=n("norm3"))
    da = _matmul(dx2b, wb['w_down'], nt=True, name=n("down_x"))
    gb['w_down'] = _matmul_tn(sv['a'], dx2b, name=n("down_w"))
    dhg, dhv, gs['f_dwg'], gs['f_dwv'] = _ffn_bwd(sv['hg'], sv['hv'], da, sp['f_cw'], sp['f_cb'], name=n("ffn_gate"))
    gb['w_up_g'] = _matmul_tn(dhg, sv['h2'], name=n("up_gate_w"))
    gb['w_up_v'] = _matmul_tn(dhv, sv['h2'], name=n("up_val_w"))
    dh2 = _matmul(dhg, wb['w_up_g'], name=n("up_gate_x"))
    dh2 = _matmul(dhv, wb['w_up_v'], res=dh2, name=n("up_val_x"))
    dx1, dx1b, gs['norm2_g'] = _rms_bwd(sv['x1'], sp['g2'], dh2, dx2, name=n("norm2"))
    dmix = _matmul(dx1b, wb['w_out'], nt=True, name=n("proj_out_x"))
    gb['w_out'] = _matmul_tn(sv['mix'], dx1b, name=n("proj_out_w"))
    z = sv['z']
    dzu, dzv, gs['a_ln_g'], gs['a_ln_b'], gs['a_wcat'], gs['a_bfull'] = _gmlp_bwd(
        z, dmix, sp['a_ln_g'], sp['a_ln_b'], sp['a_wcat'], sp['a_bfull'], name=n("gmlp"))
    dzb, dzg, *dbp = _rglru_bwd(z, dmix, sv['h0s'], [sp[k] for k in B_PRM], name=n("rglru"))
    gs.update(zip(B_PRM, dbp))
    dzc, gs['c_lb'], gs['c_ngf'] = _hgrn_bwd(z, dmix, sv['sts'], sp['c_lb'], sp['c_ngf'], name=n("hgrn"))
    dzd, gs['d_wd'], gs['d_scale'] = _pool_bwd(z, dmix, sp['d_wd'], sp['d_scale'], name=n("pool"))
    dz = jnp.concatenate([dzu, dzv, dzb, dzg, dzc, dzd], axis=1)
    gb['w_in'] = _matmul_tn(dz, sv['h'], name=n("proj_in_w"))
    dh = _matmul(dz, wb['w_in'], name=n("proj_in_x"))
    dx0, _, gs['norm1_g'] = _rms_bwd(sv['x'], sp['g1'], dh, dx1, name=n("norm1"))
    return dx0, gb, gs


SMALL_NAMES = [nm for nm in WEIGHT_NAMES if nm not in BIG_NAMES]
COL_SHARDED = ('w_in', 'w_up', 'w_pe')


def _comm_shards(w):
    return [(jnp.swapaxes(w[nm], 1, 2) if nm in COL_SHARDED else w[nm]).astype(BF16) for nm, _, _ in BIG_COMM]


def _full_weights(gathered):
    out = {nm: g.reshape(N_DEV * r, c) for g, (nm, r, c) in zip(gathered, BIG_COMM)}
    halves = out.pop('w_up').reshape(2, D_FF, D_MODEL)
    out['w_up_g'], out['w_up_v'] = _Sel(halves, 0), _Sel(halves, 1)
    return out


def _small_grads(raw):
    st = {k: jnp.stack([raw[l][k] for l in range(DEPTH)]) for k in raw[0]}
    tril = jnp.tril(jnp.ones((GMLP_CHUNK, GMLP_CHUNK), bool))
    vec = lambda a: a.reshape(DEPTH, -1)
    out = {nm: vec(st[k]) for nm, k in (('norm1_g', 'norm1_g'), ('norm2_g', 'norm2_g'), ('norm3_g', 'norm3_g'),
                                        ('a_ln_g', 'a_ln_g'), ('a_ln_b', 'a_ln_b'), ('b_conv_b', 'b_cb'),
                                        ('b_ba', 'b_ba'), ('b_bx', 'b_bx'), ('b_lam', 'b_lam'), ('c_lb', 'c_lb'),
                                        ('d_scale', 'd_scale'))}
    out['a_ws'] = jnp.where(tril, st['a_wcat'].reshape(DEPTH, N_HEADS, GMLP_CHUNK, GMLP_CHUNK), 0.0)
    out['a_bs'] = jnp.swapaxes(st['a_bfull'].reshape(DEPTH, GMLP_CHUNK, N_HEADS, HEAD_DIM).sum(-1), 1, 2)
    out['b_conv_w'] = st['b_cw']
    out['b_wa'], out['b_wx'], out['d_w'] = _diag_blocks(st['b_wa']), _diag_blocks(st['b_wx']), _diag_blocks(st['d_wd'])
    out['c_norm_g'] = st['c_ngf'].reshape(DEPTH, N_HEADS, HEAD_DIM).sum(1)
    out['ffn_conv_w'] = jnp.concatenate([st['f_dwg'][:, 0:3], st['f_dwv'][:, 0:3]], axis=2)
    out['ffn_conv_b'] = jnp.concatenate([st['f_dwg'][:, 3], st['f_dwv'][:, 3]], axis=1)
    return out


def _step(w, m, v, x, p, target):
    s = x.shape[1]
    dev = 4 * lax.axis_index("x") + 2 * lax.axis_index("y") + lax.axis_index("c")
    xs = x.reshape(s, D_MODEL)

    shards = _comm_shards(w)
    conv_shapes = [w['b_conv_w'].shape, w['ffn_conv_w'].shape]
    conv_rows = _flat_rows(conv_shapes)
    conv_all = _all_gather(_pack_flat([w['b_conv_w'], w['ffn_conv_w']], conv_rows), name="gather_conv_weights")
    parts = [_unpack_flat(conv_all[d], conv_shapes) for d in range(N_DEV)]
    wf = dict(w)
    wf['b_conv_w_full'] = jnp.concatenate([pt[0] for pt in parts], axis=-1)
    wf['ffn_conv_w_full'] = jnp.concatenate([pt[1] for pt in parts], axis=-1)
    lbs = _lbs_fwd(w['c_lb'], name="hgrn_bounds")

    stacked = _stacked_params(wf, lbs)
    p_all = p.reshape(DEPTH, s, PLE_DIM).astype(BF16)
    xl, saved, wbs, sps = xs, [], [], []
    gathered = _gather_layer(shards, 0, name="l0_gather_weights")
    for l in range(DEPTH):
        sp = {k: _Sel(a, l) for k, a in stacked.items()}
        if l + 1 < DEPTH:
            own = [x[l + 1] for x in shards]
            after = [conv_all, *gathered] if l == 0 else [xl]
            lands = _place_own(own, after, name=f"l{l + 1}_gather_place")
            started = _exchange_start(own, lands, name=f"l{l + 1}_gather_start")
            sp['g1'] = stacked['g1'][l] + started[-1][0, 0]
        wb = _full_weights(gathered)
        p_bf = p_all[l]
        xl, sv = _layer_fwd(xl, p_bf, wb, sp, l)
        if l + 1 < DEPTH:
            gathered = _exchange_wait(started, xl, name=f"l{l + 1}_gather_wait")
        saved.append((sv, p_bf))
        wbs.append(wb)
        sps.append(sp)
    loss_part, dx, dfinal = _loss_head(xl, w['final_g'].reshape(1, D_MODEL), target.reshape(s, D_MODEL), name="loss_head")
    loss = lax.psum(loss_part[0, 0], ("x", "y", "c"))

    reduced, small = [None] * DEPTH, [None] * DEPTH
    pending = None
    for l in range(DEPTH - 1, -1, -1):
        sv, p_bf = saved[l]
        sp = sps[l]
        if pending is not None:
            sp = dict(sp, g3=stacked['g3'][l] + pending[1][-1][0, 0])
        dx, gb, small[l] = _layer_bwd(dx, sv, p_bf, wbs[l], sp, l)
        gb['w_up'] = jnp.concatenate([gb.pop('w_up_g'), gb.pop('w_up_v')], axis=0)
        grads = [gb[nm] for nm, _, _ in BIG_COMM]
        if pending is not None:
            lands = _exchange_wait(pending[1], dx, name=f"l{pending[0]}_reduce_wait")
            reduced[pending[0]] = _sum_chips(lands, name=f"l{pending[0]}_reduce_sum")
            pending = None
        if l > 0:
            views = [g.reshape(N_DEV, g.shape[0] // N_DEV, g.shape[1]) for g in grads]
            lands = _place_own(views, [], name=f"l{l}_reduce_place", per_peer=True)
            pending = (l, _exchange_start(views, lands, name=f"l{l}_reduce_start", per_peer=True))
        else:
            reduced[0] = _reduce_layer(grads, 0)
    grad_x = dx.reshape(1, s, D_MODEL)
    gbig = {}
    for a, (nm, _, _) in enumerate(BIG_COMM):
        g = jnp.stack([reduced[l][a] for l in range(DEPTH)])
        gbig[nm] = jnp.swapaxes(g, 1, 2) if nm in COL_SHARDED else g

    small_parts = _small_grads(small)
    small_parts['c_lb'] = _lbs_bwd(w['c_lb'], small_parts['c_lb'], name="hgrn_bounds_bwd")
    small_parts['final_g'] = dfinal.reshape(D_MODEL)
    small_shapes = [small_parts[nm].shape for nm in SMALL_NAMES]
    small_rows = _flat_rows(small_shapes)
    small_all = _all_gather(_pack_flat([small_parts[nm] for nm in SMALL_NAMES], small_rows), name="gather_small_grads")
    gsmall = dict(zip(SMALL_NAMES, _unpack_flat(_sum_slots(small_all, name="sum_small_grads"), small_shapes)))
    for nm in ('b_conv_w', 'ffn_conv_w'):
        width = w[nm].shape[-1]
        gsmall[nm] = lax.dynamic_slice_in_dim(gsmall[nm], dev * width, width, axis=2)

    grads, delta, new_m, new_v = {}, {}, {}, {}
    for nm in BIG_NAMES:
        grads[nm] = gbig[nm]
        delta[nm], new_m[nm], new_v[nm] = _adamw(w[nm], gbig[nm], m[nm], v[nm], name=f"adamw_{nm}")
    shapes = [w[nm].shape for nm in SMALL_NAMES]
    rows = _flat_rows(shapes)
    pk = lambda t: _pack_flat([t[nm] for nm in SMALL_NAMES], rows)
    d, nm_, nv_ = _adamw(pk(w), pk(gsmall), pk(m), pk(v), name="adamw_small")
    for nm, dd, mm_, vv_ in zip(SMALL_NAMES, _unpack_flat(d, shapes), _unpack_flat(nm_, shapes), _unpack_flat(nv_, shapes)):
        grads[nm], delta[nm], new_m[nm], new_v[nm] = gsmall[nm], dd, mm_, vv_

    return (loss, grad_x, *[grads[nm] for nm in WEIGHT_NAMES], *[delta[nm] for nm in WEIGHT_NAMES],
            *[new_m[nm] for nm in WEIGHT_NAMES], *[new_v[nm] for nm in WEIGHT_NAMES])


def kernel(x, p, norm1_g, w_in, a_ln_g, a_ln_b, a_ws, a_bs, b_conv_w, b_conv_b, b_wa, b_ba, b_wx, b_bx, b_lam, c_lb, c_norm_g, d_w, d_scale, w_out, norm2_g, w_up, ffn_conv_w, ffn_conv_b, w_down, norm3_g, w_pe, w_pg, final_g, loss_target, m_norm1_g, m_w_in, m_a_ln_g, m_a_ln_b, m_a_ws, m_a_bs, m_b_conv_w, m_b_conv_b, m_b_wa, m_b_ba, m_b_wx, m_b_bx, m_b_lam, m_c_lb, m_c_norm_g, m_d_w, m_d_scale, m_w_out, m_norm2_g, m_w_up, m_ffn_conv_w, m_ffn_conv_b, m_w_down, m_norm3_g, m_w_pe, m_w_pg, m_final_g, v_norm1_g, v_w_in, v_a_ln_g, v_a_ln_b, v_a_ws, v_a_bs, v_b_conv_w, v_b_conv_b, v_b_wa, v_b_ba, v_b_wx, v_b_bx, v_b_lam, v_c_lb, v_c_norm_g, v_d_w, v_d_scale, v_w_out, v_norm2_g, v_w_up, v_ffn_conv_w, v_ffn_conv_b, v_w_down, v_norm3_g, v_w_pe, v_w_pg, v_final_g):
    w = dict(norm1_g=norm1_g, w_in=w_in, a_ln_g=a_ln_g, a_ln_b=a_ln_b, a_ws=a_ws, a_bs=a_bs, b_conv_w=b_conv_w, b_conv_b=b_conv_b, b_wa=b_wa, b_ba=b_ba, b_wx=b_wx, b_bx=b_bx, b_lam=b_lam, c_lb=c_lb, c_norm_g=c_norm_g, d_w=d_w, d_scale=d_scale, w_out=w_out, norm2_g=norm2_g, w_up=w_up, ffn_conv_w=ffn_conv_w, ffn_conv_b=ffn_conv_b, w_down=w_down, norm3_g=norm3_g, w_pe=w_pe, w_pg=w_pg, final_g=final_g)
    m = dict(norm1_g=m_norm1_g, w_in=m_w_in, a_ln_g=m_a_ln_g, a_ln_b=m_a_ln_b, a_ws=m_a_ws, a_bs=m_a_bs, b_conv_w=m_b_conv_w, b_conv_b=m_b_conv_b, b_wa=m_b_wa, b_ba=m_b_ba, b_wx=m_b_wx, b_bx=m_b_bx, b_lam=m_b_lam, c_lb=m_c_lb, c_norm_g=m_c_norm_g, d_w=m_d_w, d_scale=m_d_scale, w_out=m_w_out, norm2_g=m_norm2_g, w_up=m_w_up, ffn_conv_w=m_ffn_conv_w, ffn_conv_b=m_ffn_conv_b, w_down=m_w_down, norm3_g=m_norm3_g, w_pe=m_w_pe, w_pg=m_w_pg, final_g=m_final_g)
    v = dict(norm1_g=v_norm1_g, w_in=v_w_in, a_ln_g=v_a_ln_g, a_ln_b=v_a_ln_b, a_ws=v_a_ws, a_bs=v_a_bs, b_conv_w=v_b_conv_w, b_conv_b=v_b_conv_b, b_wa=v_b_wa, b_ba=v_b_ba, b_wx=v_b_wx, b_bx=v_b_bx, b_lam=v_b_lam, c_lb=v_c_lb, c_norm_g=v_c_norm_g, d_w=v_d_w, d_scale=v_d_scale, w_out=v_w_out, norm2_g=v_norm2_g, w_up=v_w_up, ffn_conv_w=v_ffn_conv_w, ffn_conv_b=v_ffn_conv_b, w_down=v_w_down, norm3_g=v_norm3_g, w_pe=v_w_pe, w_pg=v_w_pg, final_g=v_final_g)
    return _step(w, m, v, x, p, loss_target)
```

```python
import functools

import jax
import jax.numpy as jnp
from jax import lax
from jax.experimental import pallas as pl
from jax.experimental.pallas import tpu as pltpu

F32 = jnp.float32
BF16 = jnp.bfloat16
MESH = pl.DeviceIdType.MESH

D_MODEL = 1024
DEPTH = 4
PLE_DIM = 256
W_GRP = 256
N_HEADS = 4
HEAD_DIM = 64
GMLP_CHUNK = 128
RGLRU_C = 8.0
HGRN_CHUNK = 64
HGRN_SUB = 16
HGRN_STEP_CHUNKS = 2
POOL_WINDOWS = (2, 4, 8, 16)
D_FF = 2816
D_PROJ = 2304
EPS = 1e-6
ADAM_LR = 0.001
ADAM_B1 = 0.9
ADAM_B2 = 0.999
ADAM_EPS = 1e-08
ADAM_WD = 0.01
ADAM_STEP = 10

N_DEV = 8
MIB = 2 ** 20
V7X_VMEM_BYTES = 64 * MIB
HGRN_EXP_CLAMP = 60.0

WEIGHT_NAMES = ['norm1_g', 'w_in', 'a_ln_g', 'a_ln_b', 'a_ws', 'a_bs', 'b_conv_w', 'b_conv_b', 'b_wa', 'b_ba', 'b_wx',
                'b_bx', 'b_lam', 'c_lb', 'c_norm_g', 'd_w', 'd_scale', 'w_out', 'norm2_g', 'w_up', 'ffn_conv_w',
                'ffn_conv_b', 'w_down', 'norm3_g', 'w_pe', 'w_pg', 'final_g']
BIG_NAMES = ('w_in', 'w_out', 'w_up', 'w_down', 'w_pe', 'w_pg')


def _vmem_limit(block_bytes):
    want = 2 * block_bytes + 24 * MIB
    return int(min(max(want, 32 * MIB), V7X_VMEM_BYTES - 8 * MIB))


def _pcall(body, *, name, out_shape, grid=None, in_specs=None, out_specs=None, scratch_shapes=(),
           semantics=None, block_bytes=0, aliases=None):
    kw = {} if aliases is None else {"input_output_aliases": aliases}
    if grid is not None:
        kw["grid"] = grid
    if in_specs is not None:
        kw["in_specs"] = in_specs
    if out_specs is not None:
        kw["out_specs"] = out_specs
    params = pltpu.CompilerParams(dimension_semantics=semantics, vmem_limit_bytes=_vmem_limit(block_bytes))
    return pl.pallas_call(body, name=name, out_shape=out_shape, scratch_shapes=list(scratch_shapes),
                          compiler_params=params, **kw)


def _pick(n, cands):
    for c in cands:
        if n % c == 0:
            return c
    return n


def _nbytes(shape, dtype):
    n = 1
    for s in shape:
        n *= s
    return n * jnp.dtype(dtype).itemsize


def _sds(shape, dtype):
    return jax.ShapeDtypeStruct(tuple(shape), dtype)


class _Sel:
    def __init__(self, arr, *idx):
        self.arr, self.idx = arr, tuple(idx)
        self.shape = arr.shape[len(idx):]
        self.ndim = len(self.shape)
        self.dtype = arr.dtype


def _arr(a):
    return a.arr if isinstance(a, _Sel) else a


def _spec(a, block=None, index=None):
    block = tuple(a.shape) if block is None else tuple(block)
    index = (lambda *g: (0,) * len(block)) if index is None else index
    if isinstance(a, _Sel):
        lead = a.idx
        return pl.BlockSpec((None,) * len(lead) + block, lambda *g: lead + tuple(index(*g)))
    return pl.BlockSpec(block, lambda *g: tuple(index(*g)))


def _ospec(a):
    return pl.BlockSpec(tuple(a.shape), lambda *g: (0,) * a.ndim)


def _rows_of(shape):
    return lax.broadcasted_iota(jnp.int32, shape, 0)


def _lanes_of(shape):
    return lax.broadcasted_iota(jnp.int32, shape, 1)


def _sdn(x, k, fill):
    n = x.shape[0]
    return jnp.where(_rows_of(x.shape) >= k, pltpu.roll(x, k % n, 0), fill)


def _sup(x, k, fill):
    n = x.shape[0]
    return jnp.where(_rows_of(x.shape) < n - k, pltpu.roll(x, (n - k) % n, 0), fill)


@functools.partial(jax.custom_vjp, nondiff_argnums=(1,))
def _shift_dn(x, k):
    return pltpu.roll(x, k, 0)


def _shift_dn_fwd(x, k):
    return pltpu.roll(x, k, 0), None


def _shift_dn_bwd(k, _, g):
    return (pltpu.roll(g, g.shape[0] - k, 0),)


_shift_dn.defvjp(_shift_dn_fwd, _shift_dn_bwd)


def _lin_scan_impl(a, b, h0):
    n = a.shape[0]
    aa, bb = a, b
    k = 1
    while k < n:
        bb = aa * _sdn(bb, k, 0.0) + bb
        aa = aa * _sdn(aa, k, 1.0)
        k *= 2
    return bb + aa * h0


@jax.custom_vjp
def _lin_scan(a, b, h0):
    return _lin_scan_impl(a, b, h0)


def _lin_scan_fwd(a, b, h0):
    h = _lin_scan_impl(a, b, h0)
    return h, (a, h, h0)


def _lin_scan_bwd(res, g):
    a, h, h0 = res
    n = a.shape[0]
    cc, gg = _sup(a, 1, 0.0), g
    k = 1
    while k < n:
        gg = gg + cc * _sup(gg, k, 0.0)
        cc = cc * _sup(cc, k, 1.0)
        k *= 2
    first = _rows_of(a.shape) == 0
    hprev = jnp.where(first, h0, _sdn(h, 1, 0.0))
    dh0 = jnp.sum(jnp.where(first, a * gg, 0.0), axis=0, keepdims=True)
    return gg * hprev, gg, dh0


_lin_scan.defvjp(_lin_scan_fwd, _lin_scan_bwd)


def _cumsum_sub_impl(x):
    pos = _rows_of(x.shape) % HGRN_SUB
    k = 1
    while k < HGRN_SUB:
        x = x + jnp.where(pos >= k, pltpu.roll(x, k, 0), 0.0)
        k *= 2
    return x


@jax.custom_vjp
def _cumsum_sub(x):
    return _cumsum_sub_impl(x)


def _cumsum_sub_fwd(x):
    return _cumsum_sub_impl(x), None


def _cumsum_sub_bwd(_, g):
    n = g.shape[0]
    pos = _rows_of(g.shape) % HGRN_SUB
    k = 1
    while k < HGRN_SUB:
        g = g + jnp.where(pos < HGRN_SUB - k, pltpu.roll(g, n - k, 0), 0.0)
        k *= 2
    return (g,)


_cumsum_sub.defvjp(_cumsum_sub_fwd, _cumsum_sub_bwd)


def _dot(a, b, ca, cb):
    return lax.dot_general(a.astype(BF16), b.astype(BF16), (((ca,), (cb,)), ((), ())), preferred_element_type=F32)


@jax.custom_vjp
def _mm(a, b):
    return _dot(a, b, 1, 0)


def _mm_fwd(a, b):
    return _dot(a, b, 1, 0), (a, b)


def _mm_bwd(res, g):
    a, b = res
    return _dot(g, b, 1, 1), _dot(a, g, 0, 0)


_mm.defvjp(_mm_fwd, _mm_bwd)


@jax.custom_vjp
def _mm_nt(a, b):
    return _dot(a, b, 1, 1)


def _mm_nt_fwd(a, b):
    return _dot(a, b, 1, 1), (a, b)


def _mm_nt_bwd(res, g):
    a, b = res
    return _dot(g, b, 1, 0), _dot(g, a, 0, 0)


_mm_nt.defvjp(_mm_nt_fwd, _mm_nt_bwd)


@jax.custom_vjp
def _mm_tn(a, b):
    return _dot(a, b, 0, 0)


def _mm_tn_fwd(a, b):
    return _dot(a, b, 0, 0), (a, b)


def _mm_tn_bwd(res, g):
    a, b = res
    return _dot(b, g, 1, 1), _dot(a, g, 1, 0)


_mm_tn.defvjp(_mm_tn_fwd, _mm_tn_bwd)


def _head_mask(shape, h):
    return (_lanes_of(shape) // HEAD_DIM) == h


def _stack_heads(x):
    return jnp.concatenate([jnp.where(_head_mask(x.shape, h), x, 0.0) for h in range(N_HEADS)], axis=0)


def _unstack_heads(p):
    r = p.shape[0] // N_HEADS
    out = None
    for h in range(N_HEADS):
        blk = p[h * r:(h + 1) * r]
        term = jnp.where(_head_mask(blk.shape, h), blk, 0.0)
        out = term if out is None else out + term
    return out


def _segmean_impl(x):
    n = x.shape[1]
    same = (lax.broadcasted_iota(jnp.int32, (n, n), 0) // HEAD_DIM) == (lax.broadcasted_iota(jnp.int32, (n, n), 1) // HEAD_DIM)
    m = jnp.where(same, 1.0 / HEAD_DIM, 0.0).astype(BF16)
    hi = x.astype(BF16)
    lo = (x - hi.astype(F32)).astype(BF16)
    dn = (((1,), (0,)), ((), ()))
    return (lax.dot_general(hi, m, dn, preferred_element_type=F32)
            + lax.dot_general(lo, m, dn, preferred_element_type=F32))


@jax.custom_vjp
def _segmean(x):
    return _segmean_impl(x)


def _segmean_fwd(x):
    return _segmean_impl(x), None


def _segmean_bwd(_, g):
    return (_segmean_impl(g),)


_segmean.defvjp(_segmean_fwd, _segmean_bwd)


def _log1p(u):
    w = 1.0 + u
    return jnp.where(w == 1.0, u, jnp.log(w) * (u / (w - 1.0)))


def _softplus(y):
    return jnp.maximum(y, 0.0) + _log1p(jnp.exp(-jnp.abs(y)))


def _rms(x, g):
    return x * lax.rsqrt(jnp.mean(x * x, axis=-1, keepdims=True) + EPS) * g


def _gmlp_chunk(zu, zv, ln_g, ln_b, wcat, bfull):
    u = jax.nn.gelu(zu)
    v = jax.nn.gelu(zv)
    mu = jnp.mean(v, axis=-1, keepdims=True)
    var = jnp.mean(jnp.square(v - mu), axis=-1, keepdims=True)
    vn = (v - mu) * lax.rsqrt(var + EPS) * ln_g + ln_b
    sv = _unstack_heads(_mm(wcat, vn)) + bfull
    return u * sv


def _rglru_tile(xb_ext, gb, h0, cw, cb, wa, ba, wx, bx, lam):
    xc = (cb + cw[0:1] * _shift_dn(xb_ext, 3) + cw[1:2] * _shift_dn(xb_ext, 2) + cw[2:3] * _shift_dn(xb_ext, 1)
          + cw[3:4] * xb_ext)[8:]
    r = jax.nn.sigmoid(_mm(xc, wa) + ba)
    i = jax.nn.sigmoid(_mm(xc, wx) + bx)
    log_a = (-RGLRU_C) * r * _softplus(-lam)
    a = jnp.exp(log_a)
    mult = jnp.sqrt(-jnp.tanh(log_a) * (a * a + 1.0))
    h = _lin_scan(a, mult * (i * xc), h0)
    y = h * jax.nn.gelu(gb)
    h_last = jnp.sum(jnp.where(_rows_of(h.shape) == h.shape[0] - 1, h, 0.0), axis=0, keepdims=True)
    return y, h_last


def _pool_tile(xd_ext, inv, wd, scale):
    s1 = xd_ext + _shift_dn(xd_ext, 1)
    s2 = s1 + _shift_dn(s1, 2)
    s3 = s2 + _shift_dn(s2, 4)
    s4 = s3 + _shift_dn(s3, 8)
    grp = _lanes_of(xd_ext.shape) // HEAD_DIM
    win = jnp.where(grp == 0, s1, jnp.where(grp == 1, s2, jnp.where(grp == 2, s3, s4)))
    pooled = win[16:] * inv - xd_ext[16:]
    return _mm(pooled, wd) * scale


def _hgrn_chunk(q, f, i, g, st, lb, ngf):
    n = q.shape[0]
    nsub = n // HGRN_SUB
    qs = jax.nn.silu(q)
    fg = lb + (1.0 - lb) * jax.nn.sigmoid(f)
    lf = jnp.log(fg)
    k = 1.0 - fg
    bl = _cumsum_sub(lf)
    row = _rows_of(q.shape)
    blk = row // HGRN_SUB
    betas = [jnp.zeros_like(lb)]
    for s in range(nsub):
        tot = jnp.sum(jnp.where(row == s * HGRN_SUB + HGRN_SUB - 1, bl, 0.0), axis=0, keepdims=True)
        betas.append(betas[-1] + tot)
    b_end = betas[nsub]
    beta_full = jnp.zeros_like(q)
    for s in range(1, nsub):
        beta_full = jnp.where(blk == s, betas[s], beta_full)
    qh = qs * jnp.exp(bl)
    qt = qh * jnp.exp(beta_full)
    b_all = beta_full + bl
    kt = k * jnp.exp(b_end - b_all)
    outs = []
    for s in range(nsub):
        kh = k * jnp.exp(jnp.minimum(betas[s] - b_all, HGRN_EXP_CLAMP))
        qstk = _stack_heads(qh[s * HGRN_SUB:(s + 1) * HGRN_SUB])
        att = _mm_nt(qstk, kh)
        ar = _rows_of(att.shape) % HGRN_SUB + s * HGRN_SUB
        att = jnp.where(_lanes_of(att.shape) <= ar, att, 0.0)
        outs.append(_unstack_heads(_mm(att, i)))
    o = jnp.concatenate(outs, axis=0) + _mm_nt(qt, st)
    same = (_rows_of(st.shape) // HEAD_DIM) == (_lanes_of(st.shape) // HEAD_DIM)
    st_new = st * jnp.exp(b_end) + jnp.where(same, _mm_tn(i, kt), 0.0)
    on = o * lax.rsqrt(_segmean(o * o) + EPS) * ngf
    return on * jax.nn.silu(g), st_new


def _ffn_tile(eg, ev, wg, bg, wv, bv):
    gt = (bg + wg[0:1] * _shift_dn(eg, 2) + wg[1:2] * _shift_dn(eg, 1) + wg[2:3] * eg)[8:]
    val = (bv + wv[0:1] * _shift_dn(ev, 2) + wv[1:2] * _shift_dn(ev, 1) + wv[2:3] * ev)[8:]
    return jax.nn.gelu(gt) * val


MXU_WIDTH = 256
MATMUL_BLOCK_BUDGET = 18 * MIB


def _matmul_tiles(m, k, n, a_dtype, b_dtype, out_dtype, has_res):
    best = None
    for tm in (2048, 1024, 512, 256):
        if m % tm:
            continue
        for tn in (1024, 768, 1408, 512, 256, 128):
            if n % tn:
                continue
            blk = (_nbytes((tm, k), a_dtype) + _nbytes((k, tn), b_dtype) + _nbytes((tm, tn), out_dtype)
                   + (_nbytes((tm, tn), F32) if has_res else 0))
            if blk > MATMUL_BLOCK_BUDGET:
                continue
            waste = -(-tn // MXU_WIDTH) * MXU_WIDTH / tn
            cost = (m // tm) * (n // tn) + 64 * (waste - 1.0)
            if best is None or cost < best[0]:
                best = (cost, tm, tn, blk)
    assert best is not None, (m, k, n)
    return best[1:]


def _matmul(a, b, *, name, nt=False, res=None, out_dtype=F32):
    m, k = a.shape
    n = b.shape[0] if nt else b.shape[1]
    tm, tn, blk = _matmul_tiles(m, k, n, a.dtype, b.dtype, out_dtype, res is not None)
    dims = (((1,), (1,)), ((), ())) if nt else (((1,), (0,)), ((), ()))

    def body(*refs):
        if res is None:
            a_ref, b_ref, o_ref = refs
        else:
            a_ref, b_ref, r_ref, o_ref = refs
        acc = lax.dot_general(a_ref[...], b_ref[...], dims, preferred_element_type=F32)
        if res is not None:
            acc = acc + r_ref[...]
        o_ref[...] = acc.astype(out_dtype)

    in_specs = [pl.BlockSpec((tm, k), lambda i, j: (i, 0)),
                _spec(b, (tn, k), lambda i, j: (j, 0)) if nt else _spec(b, (k, tn), lambda i, j: (0, j))]
    args = [a, _arr(b)]
    if res is not None:
        in_specs.append(pl.BlockSpec((tm, tn), lambda i, j: (i, j)))
        args.append(res)
    return _pcall(body, name=name, out_shape=_sds((m, n), out_dtype), grid=(m // tm, n // tn), in_specs=in_specs,
                  out_specs=pl.BlockSpec((tm, tn), lambda i, j: (i, j)), semantics=("parallel", "parallel"),
                  block_bytes=blk + _nbytes((tm, tn), F32))(*args)


def _matmul_tn(a, b, *, name, out_dtype=BF16, out_rows=None, row_off=0, into=None):
    m, k1 = a.shape
    n = b.shape[1]
    tk = _pick(k1, (512, 256, 128))
    off = row_off // tk
    assert off * tk == row_off

    def body(a_ref, b_ref, *rest):
        rest[-1][...] = lax.dot_general(a_ref[...], b_ref[...], (((0,), (0,)), ((), ())),
                                        preferred_element_type=F32).astype(out_dtype)

    blk = 2 * _nbytes((m, tk), a.dtype) + _nbytes((m, n), b.dtype) + _nbytes((tk, n), F32)
    in_specs = [pl.BlockSpec((m, tk), lambda i: (0, i)), pl.BlockSpec((m, n), lambda i: (0, 0))]
    args = [a, b]
    if into is not None:
        in_specs.append(HBM_SPEC)
        args.append(into)
    return _pcall(body, name=name, out_shape=_sds((out_rows or k1, n), out_dtype), grid=(k1 // tk,), in_specs=in_specs,
                  out_specs=pl.BlockSpec((tk, n), lambda i: (i + off, 0)), semantics=("parallel",), block_bytes=blk,
                  aliases=None if into is None else {2: 0})(*args)


def _rms_fwd(x, g, *, name):
    s, d = x.shape
    tm = _pick(s, (512, 256))

    def body(x_ref, g_ref, o_ref):
        o_ref[...] = _rms(x_ref[...], g_ref[...]).astype(BF16)

    return _pcall(body, name=name, out_shape=_sds((s, d), BF16), grid=(s // tm,),
                  in_specs=[pl.BlockSpec((tm, d), lambda i: (i, 0)), _spec(g)],
                  out_specs=pl.BlockSpec((tm, d), lambda i: (i, 0)), semantics=("parallel",),
                  block_bytes=3 * _nbytes((tm, d), F32))(x, _arr(g))


def _rms_bwd(x, g, dh, dres, *, name):
    s, d = x.shape
    tm = _pick(s, (256, 128))

    def body(x_ref, g_ref, dh_ref, dr_ref, dx_ref, dxb_ref, dg_ref):
        _, vjp = jax.vjp(_rms, x_ref[...], g_ref[...])
        dxn, dg = vjp(dh_ref[...])
        dx = dr_ref[...] + dxn
        dx_ref[...] = dx
        dxb_ref[...] = dx.astype(BF16)

        @pl.when(pl.program_id(0) == 0)
        def _():
            dg_ref[...] = jnp.zeros_like(dg_ref)

        dg_ref[...] += dg

    row = pl.BlockSpec((tm, d), lambda i: (i, 0))
    vec = pl.BlockSpec((1, d), lambda i: (0, 0))
    return _pcall(body, name=name, out_shape=(_sds((s, d), F32), _sds((s, d), BF16), _sds((1, d), F32)),
                  grid=(s // tm,), in_specs=[row, _spec(g), row, row], out_specs=(row, row, vec),
                  semantics=("arbitrary",), block_bytes=8 * _nbytes((tm, d), F32))(x, _arr(g), dh, dres)


def _ple_fwd(x, gl, pe, *, name):
    s, d = x.shape
    tm = _pick(s, (512, 256))

    def body(x_ref, gl_ref, pe_ref, o_ref):
        o_ref[...] = x_ref[...] + pe_ref[...] * jax.nn.sigmoid(gl_ref[...])

    row = pl.BlockSpec((tm, d), lambda i: (i, 0))
    return _pcall(body, name=name, out_shape=_sds((s, d), F32), grid=(s // tm,), in_specs=[row, row, row],
                  out_specs=row, semantics=("parallel",), block_bytes=4 * _nbytes((tm, d), F32))(x, gl, pe)


def _ple_bwd(dx, gl, pe, *, name):
    s, d = dx.shape
    tm = _pick(s, (512, 256))

    def body(dx_ref, gl_ref, pe_ref, dpe_ref, dgl_ref):
        gate = jax.nn.sigmoid(gl_ref[...])
        dxv = dx_ref[...]
        dpe_ref[...] = (dxv * gate).astype(BF16)
        dgl_ref[...] = (dxv * pe_ref[...] * gate * (1.0 - gate)).astype(BF16)

    row = pl.BlockSpec((tm, d), lambda i: (i, 0))
    return _pcall(body, name=name, out_shape=(_sds((s, d), BF16), _sds((s, d), BF16)), grid=(s // tm,),
                  in_specs=[row, row, row], out_specs=(row, row), semantics=("parallel",),
                  block_bytes=5 * _nbytes((tm, d), F32))(dx, gl, pe)


def _loss_head(x, g, target, *, name):
    s, d = x.shape
    tm = _pick(s, (256, 128))

    def tile_loss(xv, gv, tv):
        err = jnp.square(_rms(xv, gv) - tv)
        return 0.5 * jnp.sum(jnp.mean(err, axis=-1, keepdims=True), axis=0, keepdims=True)

    def body(x_ref, g_ref, t_ref, l_ref, dx_ref, dg_ref):
        lv, vjp = jax.vjp(tile_loss, x_ref[...], g_ref[...], t_ref[...])
        dxv, dgv, _ = vjp(jnp.ones((1, 1), F32))
        dx_ref[...] = dxv

        @pl.when(pl.program_id(0) == 0)
        def _():
            l_ref[...] = jnp.zeros_like(l_ref)
            dg_ref[...] = jnp.zeros_like(dg_ref)

        l_ref[...] += jnp.broadcast_to(lv, l_ref.shape)
        dg_ref[...] += dgv

    row = pl.BlockSpec((tm, d), lambda i: (i, 0))
    vec = pl.BlockSpec((1, d), lambda i: (0, 0))
    return _pcall(body, name=name, out_shape=(_sds((8, 128), F32), _sds((s, d), F32), _sds((1, d), F32)),
                  grid=(s // tm,), in_specs=[row, vec, row],
                  out_specs=(pl.BlockSpec((8, 128), lambda i: (0, 0)), row, vec), semantics=("arbitrary",),
                  block_bytes=8 * _nbytes((tm, d), F32))(x, g, target)


def _acc_out(ref, val, first):
    @pl.when(first)
    def _():
        ref[...] = jnp.zeros_like(ref)

    ref[...] += val


def _gmlp_fwd(z, ln_g, ln_b, wcat, bfull, *, name):
    s = z.shape[0]
    t = _pick(s, (512, 256, 128))
    nch = t // GMLP_CHUNK

    def body(zu_ref, zv_ref, g_ref, b_ref, w_ref, bf_ref, o_ref):
        for c in range(nch):
            rows = pl.ds(c * GMLP_CHUNK, GMLP_CHUNK)
            o_ref[rows, :] = _gmlp_chunk(zu_ref[rows, :], zv_ref[rows, :], g_ref[...], b_ref[...], w_ref[...],
                                         bf_ref[...]).astype(BF16)

    col = lambda c: pl.BlockSpec((t, W_GRP), lambda i: (i, c))
    params = (ln_g, ln_b, wcat, bfull)
    return _pcall(body, name=name, out_shape=_sds((s, D_MODEL), BF16), grid=(s // t,),
                  in_specs=[col(0), col(1)] + [_spec(a) for a in params],
                  out_specs=pl.BlockSpec((t, W_GRP), lambda i: (i, 0)), semantics=("parallel",),
                  block_bytes=4 * _nbytes((t, W_GRP), F32))(z, z, *[_arr(a) for a in params])


def _gmlp_bwd(z, dmix, ln_g, ln_b, wcat, bfull, *, name):
    s = z.shape[0]
    t = _pick(s, (512, 256, 128))
    nch = t // GMLP_CHUNK

    def body(zu_ref, zv_ref, dy_ref, g_ref, b_ref, w_ref, bf_ref, dz_ref, dg_ref, db_ref, dw_ref, dbf_ref):
        acc = None
        for c in range(nch):
            rows = pl.ds(c * GMLP_CHUNK, GMLP_CHUNK)
            _, vjp = jax.vjp(_gmlp_chunk, zu_ref[rows, :], zv_ref[rows, :], g_ref[...], b_ref[...], w_ref[...],
                             bf_ref[...])
            du, dv, *dps = vjp(dy_ref[rows, :])
            dz_ref[rows, :] = jnp.concatenate([du, dv], axis=1).astype(BF16)
            acc = dps if acc is None else [x + y for x, y in zip(acc, dps)]
        first = pl.program_id(0) == 0
        for ref, val in zip((dg_ref, db_ref, dw_ref, dbf_ref), acc):
            _acc_out(ref, val, first)

    col = lambda c: pl.BlockSpec((t, W_GRP), lambda i: (i, c))
    params = (ln_g, ln_b, wcat, bfull)
    return _pcall(body, name=name,
                  out_shape=(_sds((s, D_PROJ), BF16),) + tuple(_sds(a.shape, F32) for a in params),
                  grid=(s // t,), in_specs=[col(0), col(1), col(0)] + [_spec(a) for a in params],
                  out_specs=(pl.BlockSpec((t, 2 * W_GRP), lambda i: (i, 0)),) + tuple(_ospec(a) for a in params),
                  semantics=("arbitrary",),
                  block_bytes=8 * _nbytes((t, W_GRP), F32))(z, z, dmix, *[_arr(a) for a in params])


def _rglru_fwd(z, prm, mix, *, name):
    s = z.shape[0]
    t = _pick(s, (512, 256, 128))
    nt = s // t

    def body(xb_ref, halo_ref, gb_ref, *rest):
        prm_refs, (y_ref, h0s_ref, h_scr) = rest[:len(prm)], rest[len(prm) + 1:]
        i = pl.program_id(0)

        @pl.when(i == 0)
        def _():
            h_scr[...] = jnp.zeros_like(h_scr)

        halo = jnp.where(i == 0, 0.0, halo_ref[...])
        h0 = h_scr[...]
        y, h_last = _rglru_tile(jnp.concatenate([halo, xb_ref[...]], axis=0), gb_ref[...], h0,
                                *[r[...] for r in prm_refs])
        y_ref[...] = y.astype(BF16)
        h0s_ref[...] = jnp.broadcast_to(h0, h0s_ref.shape)
        h_scr[...] = h_last

    in_specs = [pl.BlockSpec((t, W_GRP), lambda i: (i, 2)),
                pl.BlockSpec((8, W_GRP), lambda i: (jnp.maximum(i * (t // 8) - 1, 0), 2)),
                pl.BlockSpec((t, W_GRP), lambda i: (i, 3))] + [_spec(a) for a in prm] + [HBM_SPEC]
    return _pcall(body, name=name, out_shape=(_sds(mix.shape, BF16), _sds((nt, 8, W_GRP), F32)), grid=(nt,),
                  in_specs=in_specs,
                  out_specs=(pl.BlockSpec((t, W_GRP), lambda i: (i, 1)), pl.BlockSpec((None, 8, W_GRP), lambda i: (i, 0, 0))),
                  scratch_shapes=[pltpu.VMEM((1, W_GRP), F32)], semantics=("arbitrary",),
                  block_bytes=24 * _nbytes((t, W_GRP), F32), aliases={3 + len(prm): 0})(
                      z, z, z, *[_arr(a) for a in prm], mix)


def _rglru_bwd(z, dmix, h0s, prm, dz, *, name):
    s = z.shape[0]
    t = _pick(s, (512, 256, 128))
    nt = s // t
    npm = len(prm)

    def body(xb_ref, halo_ref, gb_ref, dy_ref, h0s_ref, *rest):
        prm_refs = rest[:npm]
        dz_ref = rest[npm + 1]
        dprm_refs = rest[npm + 2:2 * npm + 2]
        dh_scr, dhalo_scr = rest[2 * npm + 2:]
        i = pl.program_id(0)
        r = nt - 1 - i

        @pl.when(i == 0)
        def _():
            dh_scr[...] = jnp.zeros_like(dh_scr)
            dhalo_scr[...] = jnp.zeros_like(dhalo_scr)

        halo = jnp.where(r == 0, 0.0, halo_ref[...])
        h0 = h0s_ref[0:1, :]
        _, vjp = jax.vjp(_rglru_tile, jnp.concatenate([halo, xb_ref[...]], axis=0), gb_ref[...], h0,
                         *[p[...] for p in prm_refs])
        dext, dgb, _dh0, *dps = vjp((dy_ref[...], dh_scr[...]))
        dmain = dext[8:]
        dxb = jnp.concatenate([dmain[:t - 8], dmain[t - 8:] + dhalo_scr[...]], axis=0)
        dz_ref[...] = jnp.concatenate([dxb, dgb], axis=1).astype(BF16)
        dh_scr[...] = _dh0
        dhalo_scr[...] = dext[:8]
        for ref, val in zip(dprm_refs, dps):
            _acc_out(ref, val, i == 0)

    rev = lambda c: pl.BlockSpec((t, W_GRP), lambda i: (nt - 1 - i, c))
    in_specs = [rev(2), pl.BlockSpec((8, W_GRP), lambda i: (jnp.maximum((nt - 1 - i) * (t // 8) - 1, 0), 2)), rev(3),
                rev(1), pl.BlockSpec((None, 8, W_GRP), lambda i: (nt - 1 - i, 0, 0))] + [_spec(a) for a in prm] + [HBM_SPEC]
    return _pcall(body, name=name,
                  out_shape=(_sds(dz.shape, BF16),) + tuple(_sds(a.shape, F32) for a in prm),
                  grid=(nt,), in_specs=in_specs,
                  out_specs=(pl.BlockSpec((t, 2 * W_GRP), lambda i: (nt - 1 - i, 1)),) + tuple(_ospec(a) for a in prm),
                  scratch_shapes=[pltpu.VMEM((1, W_GRP), F32), pltpu.VMEM((8, W_GRP), F32)],
                  semantics=("arbitrary",), block_bytes=40 * _nbytes((t, W_GRP), F32), aliases={5 + npm: 0})(
                      z, z, z, dmix, h0s, *[_arr(a) for a in prm], dz)


def _pool_inv(i, t):
    pos = (_rows_of((t, W_GRP)) + i * t + 1).astype(F32)
    grp = _lanes_of((t, W_GRP)) // HEAD_DIM
    win = jnp.where(grp == 0, float(POOL_WINDOWS[0]), jnp.where(grp == 1, float(POOL_WINDOWS[1]),
                    jnp.where(grp == 2, float(POOL_WINDOWS[2]), float(POOL_WINDOWS[3]))))
    return 1.0 / jnp.minimum(pos, win)


def _pool_fwd(z, wd, scale, mix, *, name):
    s = z.shape[0]
    t = _pick(s, (512, 256, 128))

    def body(x_ref, halo_ref, wd_ref, sc_ref, _, y_ref):
        i = pl.program_id(0)
        halo = jnp.where(i == 0, 0.0, halo_ref[...])
        y = _pool_tile(jnp.concatenate([halo, x_ref[...]], axis=0), _pool_inv(i, t), wd_ref[...], sc_ref[...])
        y_ref[...] = y.astype(BF16)

    in_specs = [pl.BlockSpec((t, W_GRP), lambda i: (i, 8)),
                pl.BlockSpec((16, W_GRP), lambda i: (jnp.maximum(i * (t // 16) - 1, 0), 8)), _spec(wd), _spec(scale),
                HBM_SPEC]
    return _pcall(body, name=name, out_shape=_sds(mix.shape, BF16), grid=(s // t,), in_specs=in_specs,
                  out_specs=pl.BlockSpec((t, W_GRP), lambda i: (i, 3)), semantics=("parallel",),
                  block_bytes=12 * _nbytes((t, W_GRP), F32), aliases={4: 0})(z, z, _arr(wd), _arr(scale), mix)


def _pool_bwd(z, dmix, wd, scale, dz, *, name):
    s = z.shape[0]
    t = _pick(s, (512, 256, 128))
    nt = s // t

    def body(x_ref, halo_ref, dy_ref, wd_ref, sc_ref, _, dx_ref, dwd_ref, dsc_ref, dhalo_scr):
        i = pl.program_id(0)
        r = nt - 1 - i

        @pl.when(i == 0)
        def _():
            dhalo_scr[...] = jnp.zeros_like(dhalo_scr)

        halo = jnp.where(r == 0, 0.0, halo_ref[...])
        inv = _pool_inv(r, t)
        _, vjp = jax.vjp(lambda e, w, sc: _pool_tile(e, inv, w, sc), jnp.concatenate([halo, x_ref[...]], axis=0),
                         wd_ref[...], sc_ref[...])
        dext, dwd, dsc = vjp(dy_ref[...])
        dmain = dext[16:]
        dx = jnp.concatenate([dmain[:t - 16], dmain[t - 16:] + dhalo_scr[...]], axis=0)
        dx_ref[...] = dx.astype(BF16)
        dhalo_scr[...] = dext[:16]
        _acc_out(dwd_ref, dwd, i == 0)
        _acc_out(dsc_ref, dsc, i == 0)

    rev = lambda c: pl.BlockSpec((t, W_GRP), lambda i: (nt - 1 - i, c))
    in_specs = [rev(8), pl.BlockSpec((16, W_GRP), lambda i: (jnp.maximum((nt - 1 - i) * (t // 16) - 1, 0), 8)), rev(3),
                _spec(wd), _spec(scale), HBM_SPEC]
    return _pcall(body, name=name, out_shape=(_sds(dz.shape, BF16), _sds(wd.shape, F32), _sds(scale.shape, F32)),
                  grid=(nt,), in_specs=in_specs, out_specs=(rev(8), _ospec(wd), _ospec(scale)),
                  scratch_shapes=[pltpu.VMEM((16, W_GRP), F32)], semantics=("arbitrary",),
                  block_bytes=20 * _nbytes((t, W_GRP), F32), aliases={5: 0})(z, z, dmix, _arr(wd), _arr(scale), dz)


def _hgrn_fwd(z, lb, ngf, mix, *, name):
    s = z.shape[0]
    c = HGRN_CHUNK
    per = HGRN_STEP_CHUNKS
    ns = s // (c * per)

    def body(q_ref, f_ref, i_ref, g_ref, lb_ref, ng_ref, _, y_ref, sts_ref, st_scr):
        @pl.when(pl.program_id(0) == 0)
        def _():
            st_scr[...] = jnp.zeros_like(st_scr)

        st = st_scr[...]
        for k in range(per):
            rows = pl.ds(k * c, c)
            sts_ref[k] = st
            y, st = _hgrn_chunk(q_ref[rows, :], f_ref[rows, :], i_ref[rows, :], g_ref[rows, :], st, lb_ref[...],
                                ng_ref[...])
            y_ref[rows, :] = y.astype(BF16)
        st_scr[...] = st

    col = lambda k: pl.BlockSpec((per * c, W_GRP), lambda i: (i, k))
    return _pcall(body, name=name, out_shape=(_sds(mix.shape, BF16), _sds((ns * per, W_GRP, W_GRP), F32)), grid=(ns,),
                  in_specs=[col(4), col(5), col(6), col(7), _spec(lb), _spec(ngf), HBM_SPEC],
                  out_specs=(pl.BlockSpec((per * c, W_GRP), lambda i: (i, 2)),
                             pl.BlockSpec((per, W_GRP, W_GRP), lambda i: (i, 0, 0))),
                  scratch_shapes=[pltpu.VMEM((W_GRP, W_GRP), F32)], semantics=("arbitrary",),
                  block_bytes=16 * per * _nbytes((W_GRP, W_GRP), F32), aliases={6: 0})(
                      z, z, z, z, _arr(lb), _arr(ngf), mix)


def _hgrn_bwd(z, dmix, sts, lb, ngf, dz, *, name):
    s = z.shape[0]
    c = HGRN_CHUNK
    per = HGRN_STEP_CHUNKS
    ns = s // (c * per)

    def body(q_ref, f_ref, i_ref, g_ref, dy_ref, st_ref, lb_ref, ng_ref, _, dz_ref, dlb_ref, dng_ref, dst_scr):
        i = pl.program_id(0)

        @pl.when(i == 0)
        def _():
            dst_scr[...] = jnp.zeros_like(dst_scr)

        dst = dst_scr[...]
        dlb_sum = dng_sum = None
        for k in range(per - 1, -1, -1):
            rows = pl.ds(k * c, c)
            _, vjp = jax.vjp(_hgrn_chunk, q_ref[rows, :], f_ref[rows, :], i_ref[rows, :], g_ref[rows, :], st_ref[k],
                             lb_ref[...], ng_ref[...])
            dq, df, di, dg, dst, dlb, dng = vjp((dy_ref[rows, :], dst))
            dz_ref[rows, :] = jnp.concatenate([dq, df, di, dg], axis=1).astype(BF16)
            dlb_sum = dlb if dlb_sum is None else dlb_sum + dlb
            dng_sum = dng if dng_sum is None else dng_sum + dng
        dst_scr[...] = dst
        _acc_out(dlb_ref, dlb_sum, i == 0)
        _acc_out(dng_ref, dng_sum, i == 0)

    rev = lambda k: pl.BlockSpec((per * c, W_GRP), lambda i: (ns - 1 - i, k))
    vec = pl.BlockSpec((1, W_GRP), lambda i: (0, 0))
    return _pcall(body, name=name, out_shape=(_sds(dz.shape, BF16), _sds((1, W_GRP), F32), _sds((1, W_GRP), F32)),
                  grid=(ns,),
                  in_specs=[rev(4), rev(5), rev(6), rev(7), rev(2),
                            pl.BlockSpec((per, W_GRP, W_GRP), lambda i: (ns - 1 - i, 0, 0)), _spec(lb), _spec(ngf),
                            HBM_SPEC],
                  out_specs=(pl.BlockSpec((per * c, 4 * W_GRP), lambda i: (ns - 1 - i, 1)), vec, vec),
                  scratch_shapes=[pltpu.VMEM((W_GRP, W_GRP), F32)], semantics=("arbitrary",),
                  block_bytes=32 * per * _nbytes((W_GRP, W_GRP), F32), aliases={8: 0})(
                      z, z, z, z, dmix, sts, _arr(lb), _arr(ngf), dz)


def _lbs_fwd(c_lb, *, name):
    def body(c_ref, o_ref):
        c = c_ref[...]
        e = jnp.exp(c - jnp.max(c, axis=0, keepdims=True))
        sm = e / jnp.sum(e, axis=0, keepdims=True)
        run = jnp.zeros((1, W_GRP), F32)
        o_ref[0:1, :] = run
        for l in range(1, DEPTH):
            run = run + sm[l:l + 1]
            o_ref[l:l + 1, :] = run

    return _pcall(body, name=name, out_shape=_sds((DEPTH, W_GRP), F32))(c_lb)


def _lbs_bwd(c_lb, dlbs, *, name):
    def body(c_ref, d_ref, o_ref):
        c = c_ref[...]
        e = jnp.exp(c - jnp.max(c, axis=0, keepdims=True))
        sm = e / jnp.sum(e, axis=0, keepdims=True)
        d = d_ref[...]
        dsm = [None] * DEPTH
        run = jnp.zeros((1, W_GRP), F32)
        for l in range(DEPTH - 1, 0, -1):
            run = run + d[l:l + 1]
            dsm[l] = run
        dsm[0] = jnp.zeros((1, W_GRP), F32)
        inner = sum(sm[l:l + 1] * dsm[l] for l in range(DEPTH))
        for l in range(DEPTH):
            o_ref[l:l + 1, :] = sm[l:l + 1] * (dsm[l] - inner)

    return _pcall(body, name=name, out_shape=_sds((DEPTH, W_GRP), F32))(c_lb, dlbs)


def _ffn_fwd(hg, hv, cwf, cbf, *, name):
    s, n = hg.shape
    t = _pick(s, (256, 128))
    cw = _pick(n, (1408, 256, 128))
    nj = n // cw

    def body(g_ref, gh_ref, v_ref, vh_ref, wg_ref, bg_ref, wv_ref, bv_ref, o_ref):
        first = pl.program_id(1) == 0
        eg = jnp.concatenate([jnp.where(first, 0.0, gh_ref[...]), g_ref[...]], axis=0)
        ev = jnp.concatenate([jnp.where(first, 0.0, vh_ref[...]), v_ref[...]], axis=0)
        o_ref[...] = _ffn_tile(eg, ev, wg_ref[...], bg_ref[...], wv_ref[...], bv_ref[...]).astype(BF16)

    main = pl.BlockSpec((t, cw), lambda j, i: (i, j))
    halo = pl.BlockSpec((8, cw), lambda j, i: (jnp.maximum(i * (t // 8) - 1, 0), j))
    taps = lambda off: _spec(cwf, (3, cw), lambda j, i: (0, j + off))
    bias = lambda off: _spec(cbf, (1, cw), lambda j, i: (0, j + off))
    return _pcall(body, name=name, out_shape=_sds((s, n), BF16), grid=(nj, s // t),
                  in_specs=[main, halo, main, halo, taps(0), bias(0), taps(nj), bias(nj)], out_specs=main,
                  semantics=("parallel", "parallel"), block_bytes=12 * _nbytes((t, cw), F32))(
                      hg, hg, hv, hv, _arr(cwf), _arr(cbf), _arr(cwf), _arr(cbf))


def _ffn_bwd(hg, hv, da, cwf, cbf, *, name):
    s, n = hg.shape
    t = _pick(s, (256, 128))
    cw = _pick(n, (1408, 256, 128))
    nt = s // t
    nj = n // cw

    def body(g_ref, gh_ref, v_ref, vh_ref, da_ref, wg_ref, bg_ref, wv_ref, bv_ref, dg_ref, dv_ref, dwg_ref, dwv_ref,
             cg_scr, cv_scr):
        i = pl.program_id(1)
        r = nt - 1 - i

        @pl.when(i == 0)
        def _():
            cg_scr[...] = jnp.zeros_like(cg_scr)
            cv_scr[...] = jnp.zeros_like(cv_scr)

        eg = jnp.concatenate([jnp.where(r == 0, 0.0, gh_ref[...]), g_ref[...]], axis=0)
        ev = jnp.concatenate([jnp.where(r == 0, 0.0, vh_ref[...]), v_ref[...]], axis=0)
        _, vjp = jax.vjp(_ffn_tile, eg, ev, wg_ref[...], bg_ref[...], wv_ref[...], bv_ref[...])
        deg, dev, dwg, dbg, dwv, dbv = vjp(da_ref[...])
        for dext, scr, ref in ((deg, cg_scr, dg_ref), (dev, cv_scr, dv_ref)):
            dmain = dext[8:]
            ref[...] = jnp.concatenate([dmain[:t - 8], dmain[t - 8:] + scr[...]], axis=0).astype(BF16)
            scr[...] = dext[:8]
        zeros = jnp.zeros((4, cw), F32)
        _acc_out(dwg_ref, jnp.concatenate([dwg, dbg, zeros], axis=0), i == 0)
        _acc_out(dwv_ref, jnp.concatenate([dwv, dbv, zeros], axis=0), i == 0)

    main = pl.BlockSpec((t, cw), lambda j, i: (nt - 1 - i, j))
    halo = pl.BlockSpec((8, cw), lambda j, i: (jnp.maximum((nt - 1 - i) * (t // 8) - 1, 0), j))
    taps = lambda off: _spec(cwf, (3, cw), lambda j, i: (0, j + off))
    bias = lambda off: _spec(cbf, (1, cw), lambda j, i: (0, j + off))
    w8 = pl.BlockSpec((8, cw), lambda j, i: (0, j))
    return _pcall(body, name=name,
                  out_shape=(_sds((s, n), BF16), _sds((s, n), BF16), _sds((8, n), F32), _sds((8, n), F32)),
                  grid=(nj, nt), in_specs=[main, halo, main, halo, main, taps(0), bias(0), taps(nj), bias(nj)],
                  out_specs=(main, main, w8, w8),
                  scratch_shapes=[pltpu.VMEM((8, cw), F32), pltpu.VMEM((8, cw), F32)],
                  semantics=("parallel", "arbitrary"), block_bytes=24 * _nbytes((t, cw), F32))(
                      hg, hg, hv, hv, da, _arr(cwf), _arr(cbf), _arr(cwf), _arr(cbf))


def _all_gather(x, *, name):
    r, c = x.shape

    def body(x_ref, out_ref, send_sems, recv_sems, local_sem):
        mx, my, mc = lax.axis_index("x"), lax.axis_index("y"), lax.axis_index("c")
        me, sibling = (mx, my, mc), (mx, my, 1 - mc)
        chips = [(1 - mx, my), (mx, 1 - my), (1 - mx, 1 - my)]

        def slot(px, py, pc):
            return out_ref.at[4 * px + 2 * py + pc]

        def copy(k, block, to, src=None):
            return pltpu.make_async_remote_copy(src_ref=slot(*block) if src is None else src, dst_ref=slot(*block),
                                                send_sem=send_sems.at[k], recv_sem=recv_sems.at[k],
                                                device_id=to, device_id_type=MESH)

        mine = pltpu.make_async_copy(x_ref, slot(*me), local_sem)
        mine.start()
        first = [copy(0, me, sibling, src=x_ref)]
        first += [copy(1 + j, me, (*chip, mc), src=x_ref) for j, chip in enumerate(chips)]
        for cp in first:
            cp.start()
        passed = [copy(4 + j, (*chip, mc), sibling) for j, chip in enumerate(chips)]
        for j, chip in enumerate(chips):
            copy(1 + j, (*chip, mc), me).wait_recv()
            passed[j].start()
        copy(0, sibling, me).wait_recv()
        for j, chip in enumerate(chips):
            copy(4 + j, (*chip, 1 - mc), me).wait_recv()
        for cp in first + passed:
            cp.wait_send()
        mine.wait()

    hbm = pl.BlockSpec(memory_space=pl.ANY)
    return _pcall(body, name=name, out_shape=_sds((N_DEV, r, c), x.dtype), in_specs=[hbm], out_specs=hbm,
                  scratch_shapes=[pltpu.SemaphoreType.DMA((7,)), pltpu.SemaphoreType.DMA((7,)),
                                  pltpu.SemaphoreType.DMA(())])(x)


def _sum_slots(p, *, name):
    q, r, c = p.shape
    tr = _pick(r, (544, 408, 272, 192, 136, 64, 32, 16, 8))

    def body(p_ref, o_ref):
        acc = p_ref[0].astype(F32)
        for k in range(1, q):
            acc = acc + p_ref[k].astype(F32)
        o_ref[...] = acc

    return _pcall(body, name=name, out_shape=_sds((r, c), F32), grid=(r // tr,),
                  in_specs=[pl.BlockSpec((q, tr, c), lambda i: (0, i, 0))],
                  out_specs=pl.BlockSpec((tr, c), lambda i: (i, 0)), semantics=("parallel",),
                  block_bytes=(q + 2) * _nbytes((tr, c), F32))(p)


BIG_COMM = (('w_in', 288, D_MODEL), ('w_out', 128, D_MODEL), ('w_up', 704, D_MODEL), ('w_down', 352, D_MODEL),
            ('w_pe', 128, PLE_DIM), ('w_pg', 128, D_MODEL))
HBM_SPEC = pl.BlockSpec(memory_space=pl.ANY)


def _gather_layer(shards, l, *, name):
    na = len(shards)

    def body(*refs):
        x_refs, out_refs = refs[:na], refs[na:2 * na]
        send_sems, recv_sems, local_sems = refs[2 * na:]
        mx, my, mc = lax.axis_index("x"), lax.axis_index("y"), lax.axis_index("c")
        me, sibling = (mx, my, mc), (mx, my, 1 - mc)
        chips = [(1 - mx, my), (mx, 1 - my), (1 - mx, 1 - my)]

        def slot(a, px, py, pc):
            return out_refs[a].at[4 * px + 2 * py + pc]

        def copy(k, a, block, to, own=False):
            return pltpu.make_async_remote_copy(src_ref=x_refs[a].at[l] if own else slot(a, *block),
                                                dst_ref=slot(a, *block), send_sem=send_sems.at[k, a],
                                                recv_sem=recv_sems.at[k, a], device_id=to, device_id_type=MESH)

        mine = [pltpu.make_async_copy(x_refs[a].at[l], slot(a, *me), local_sems.at[a]) for a in range(na)]
        for cp in mine:
            cp.start()
        first = []
        for a in range(na):
            first.append(copy(0, a, me, sibling, own=True))
            first += [copy(1 + j, a, me, (*chip, mc), own=True) for j, chip in enumerate(chips)]
        for cp in first:
            cp.start()
        passed = []
        for j, chip in enumerate(chips):
            for a in range(na):
                copy(1 + j, a, (*chip, mc), me).wait_recv()
                fwd = copy(4 + j, a, (*chip, mc), sibling)
                fwd.start()
                passed.append(fwd)
        for a in range(na):
            copy(0, a, sibling, me).wait_recv()
        for j, chip in enumerate(chips):
            for a in range(na):
                copy(4 + j, a, (*chip, 1 - mc), me).wait_recv()
        for cp in first + passed:
            cp.wait_send()
        for cp in mine:
            cp.wait()

    return _pcall(body, name=name, out_shape=tuple(_sds((N_DEV,) + x.shape[1:], x.dtype) for x in shards),
                  in_specs=[HBM_SPEC] * na, out_specs=(HBM_SPEC,) * na,
                  scratch_shapes=[pltpu.SemaphoreType.DMA((7, na)), pltpu.SemaphoreType.DMA((7, na)),
                                  pltpu.SemaphoreType.DMA((na,))])(*shards)


SEM_SPEC = pl.BlockSpec(memory_space=pltpu.SEMAPHORE)
DATAFLOW_EFFECT = pltpu.SideEffectType.DATAFLOW_SIDE_EFFECTING


def _place_own(srcs, after, *, name, per_peer=False):
    na = len(srcs)

    def body(*refs):
        x_refs, land_refs, sems = refs[:na], refs[na + len(after):2 * na + len(after)], refs[-1]
        me = 4 * lax.axis_index("x") + 2 * lax.axis_index("y") + lax.axis_index("c")
        cps = [pltpu.make_async_copy(x_refs[a].at[me] if per_peer else x_refs[a], land_refs[a].at[me], sems.at[a])
               for a in range(na)]
        for cp in cps:
            cp.start()
        for cp in cps:
            cp.wait()

    shape = lambda x: x.shape if per_peer else (N_DEV,) + x.shape
    return _pcall(body, name=name, out_shape=tuple(_sds(shape(x), x.dtype) for x in srcs),
                  in_specs=[HBM_SPEC] * (na + len(after)), out_specs=(HBM_SPEC,) * na,
                  scratch_shapes=[pltpu.SemaphoreType.DMA((na,))])(*srcs, *after)


def _exchange_start(srcs, lands, *, name, per_peer=False):
    na = len(srcs)

    def body(*refs):
        x_refs, land_refs = refs[:na], refs[na:2 * na]
        send_sems, recv_sems = refs[2 * na], refs[2 * na + 1]
        token = refs[-1]
        mx, my, mc = lax.axis_index("x"), lax.axis_index("y"), lax.axis_index("c")
        me = 4 * mx + 2 * my + mc
        peers = [(mx, my, 1 - mc)]
        for px, py in ((1 - mx, my), (mx, 1 - my), (1 - mx, 1 - my)):
            peers += [(px, py, mc), (px, py, 1 - mc)]
        for a in range(na):
            for peer in peers:
                src = x_refs[a].at[4 * peer[0] + 2 * peer[1] + peer[2]] if per_peer else x_refs[a]
                pltpu.make_async_remote_copy(src_ref=src, dst_ref=land_refs[a].at[me], send_sem=send_sems.at[a],
                                             recv_sem=recv_sems.at[a], device_id=peer, device_id_type=MESH).start()
        token[...] = jnp.zeros_like(token)

    hbm = lambda x: pltpu.HBM(x.shape, x.dtype)
    out_shape = ((pltpu.SemaphoreType.DMA((na,)), pltpu.SemaphoreType.DMA((na,))) + tuple(hbm(x) for x in srcs)
                 + tuple(hbm(x) for x in lands) + (_sds((8, 128), F32),))
    params = pltpu.CompilerParams(has_side_effects=DATAFLOW_EFFECT)
    pin = lambda x: pltpu.with_memory_space_constraint(x, pltpu.HBM)
    return pl.pallas_call(body, name=name, out_shape=out_shape, in_specs=[HBM_SPEC] * (2 * na),
                          out_specs=(SEM_SPEC, SEM_SPEC) + (HBM_SPEC,) * (2 * na) + (pl.BlockSpec(memory_space=pltpu.VMEM),),
                          input_output_aliases={i: 2 + i for i in range(2 * na)}, compiler_params=params)(
                              *[pin(x) for x in srcs], *[pin(x) for x in lands])


def _exchange_wait(started, after, *, name):
    send_sems, recv_sems, *bufs, _ = started
    na = len(bufs) // 2

    def body(*refs):
        land_refs = refs[na:2 * na]
        s_sems, r_sems = refs[2 * na], refs[2 * na + 1]
        me = (lax.axis_index("x"), lax.axis_index("y"), lax.axis_index("c"))
        for a in range(na):
            seven = land_refs[a].at[pl.ds(0, N_DEV - 1)]
            cp = pltpu.make_async_remote_copy(src_ref=seven, dst_ref=seven, send_sem=s_sems.at[a], recv_sem=r_sems.at[a],
                                              device_id=me, device_id_type=MESH)
            cp.wait_send()
            cp.wait_recv()

    hbm = lambda x: pltpu.HBM(x.shape, x.dtype)
    params = pltpu.CompilerParams(has_side_effects=DATAFLOW_EFFECT)
    outs = pl.pallas_call(body, name=name, out_shape=tuple(hbm(x) for x in bufs),
                          in_specs=[HBM_SPEC] * (2 * na) + [SEM_SPEC, SEM_SPEC, HBM_SPEC],
                          out_specs=(HBM_SPEC,) * (2 * na), input_output_aliases={i: i for i in range(2 * na)},
                          compiler_params=params)(*bufs, send_sems, recv_sems, after)
    return outs[:na], outs[na:]


def _pair_swap(grads, *, name):
    na = len(grads)

    def body(*refs):
        g_refs, recv_refs = refs[:na], refs[na:2 * na]
        send_sems, recv_sems = refs[2 * na:]
        mx, my, mc = lax.axis_index("x"), lax.axis_index("y"), lax.axis_index("c")
        sibling = (mx, my, 1 - mc)
        for a in range(na):
            for q in range(4):
                pltpu.make_async_remote_copy(src_ref=g_refs[a].at[q, 1 - mc], dst_ref=recv_refs[a].at[q],
                                             send_sem=send_sems.at[a], recv_sem=recv_sems.at[a],
                                             device_id=sibling, device_id_type=MESH).start()
        for a in range(na):
            pltpu.make_async_remote_copy(src_ref=recv_refs[a], dst_ref=recv_refs[a], send_sem=send_sems.at[a],
                                         recv_sem=recv_sems.at[a], device_id=sibling, device_id_type=MESH).wait()

    half = tuple(_sds((4,) + g.shape[2:], g.dtype) for g in grads)
    return _pcall(body, name=name, out_shape=half, in_specs=[HBM_SPEC] * na, out_specs=(HBM_SPEC,) * na,
                  scratch_shapes=[pltpu.SemaphoreType.DMA((na,)), pltpu.SemaphoreType.DMA((na,))])(*grads)


def _add_slabs(grads, recv, core, *, name):
    na = len(grads)

    def body(core_ref, *refs):
        for a in range(na):
            refs[2 * na + a][...] = (refs[a][...].astype(F32) + refs[na + a][...].astype(F32)).astype(BF16)

    own_specs = [pl.BlockSpec((None, None) + x.shape[2:], lambda q, core_ref: (q, core_ref[0], 0, 0)) for x in grads]
    specs = [pl.BlockSpec((None,) + x.shape[1:], lambda q, core_ref: (q, 0, 0)) for x in recv]
    blk = sum(_nbytes(x.shape[1:], F32) for x in recv)
    grid_spec = pltpu.PrefetchScalarGridSpec(num_scalar_prefetch=1, grid=(4,), in_specs=own_specs + specs,
                                             out_specs=tuple(specs))
    params = pltpu.CompilerParams(dimension_semantics=("parallel",), vmem_limit_bytes=_vmem_limit(2 * blk))
    return pl.pallas_call(body, name=name, out_shape=tuple(_sds(x.shape, BF16) for x in recv), grid_spec=grid_spec,
                          compiler_params=params)(core, *grads, *recv)


def _chip_exchange(parts, *, name):
    na = len(parts)

    def body(*refs):
        p_refs, out_refs = refs[:na], refs[na:2 * na]
        send_sems, recv_sems, local_sems = refs[2 * na:]
        mx, my, mc = lax.axis_index("x"), lax.axis_index("y"), lax.axis_index("c")
        mine_q = 2 * mx + my
        chips = [(1 - mx, my), (mx, 1 - my), (1 - mx, 1 - my)]
        owns = [pltpu.make_async_copy(p_refs[a].at[mine_q], out_refs[a].at[mine_q], local_sems.at[a]) for a in range(na)]
        for cp in owns:
            cp.start()
        sends = []
        for a in range(na):
            for k, chip in enumerate(chips):
                sends.append(pltpu.make_async_remote_copy(
                    src_ref=p_refs[a].at[2 * chip[0] + chip[1]], dst_ref=out_refs[a].at[mine_q],
                    send_sem=send_sems.at[k, a], recv_sem=recv_sems.at[k, a], device_id=(*chip, mc), device_id_type=MESH))
        for cp in sends:
            cp.start()
        for a in range(na):
            for k, chip in enumerate(chips):
                pltpu.make_async_remote_copy(
                    src_ref=p_refs[a].at[mine_q], dst_ref=out_refs[a].at[2 * chip[0] + chip[1]],
                    send_sem=send_sems.at[k, a], recv_sem=recv_sems.at[k, a], device_id=(*chip, mc),
                    device_id_type=MESH).wait_recv()
        for cp in sends:
            cp.wait_send()
        for cp in owns:
            cp.wait()

    return _pcall(body, name=name, out_shape=tuple(_sds(x.shape, x.dtype) for x in parts), in_specs=[HBM_SPEC] * na,
                  out_specs=(HBM_SPEC,) * na,
                  scratch_shapes=[pltpu.SemaphoreType.DMA((3, na)), pltpu.SemaphoreType.DMA((3, na)),
                                  pltpu.SemaphoreType.DMA((na,))])(*parts)


def _sum_chips(parts, *, name):
    na = len(parts)

    def body(*refs):
        for a in range(na):
            p_ref = refs[a]
            acc = p_ref[0].astype(F32)
            for k in range(1, p_ref.shape[0]):
                acc = acc + p_ref[k].astype(F32)
            refs[na + a][...] = acc

    half = lambda x: x.shape[1] // 2
    in_specs = [pl.BlockSpec((x.shape[0], half(x), x.shape[2]), lambda i: (0, i, 0)) for x in parts]
    out_specs = tuple(pl.BlockSpec((half(x), x.shape[2]), lambda i: (i, 0)) for x in parts)
    blk = sum(_nbytes((x.shape[0] + 2, half(x), x.shape[2]), BF16) for x in parts)
    return _pcall(body, name=name, out_shape=tuple(_sds(x.shape[1:], F32) for x in parts), grid=(2,),
                  in_specs=in_specs, out_specs=out_specs, semantics=("parallel",), block_bytes=blk)(*parts)


def _sum_devices(lands, own, me, *, name):
    na = len(lands)

    def body(me_ref, *refs):
        mine = me_ref[0]
        for a in range(na):
            l_ref, o_ref = refs[a], refs[na + a]
            acc = None
            for k in range(N_DEV):
                term = jnp.where(mine == k, o_ref[...], l_ref[k]).astype(F32)
                acc = term if acc is None else acc + term
            refs[2 * na + a][...] = acc

    half = lambda x: x.shape[1] // 2
    land_specs = [pl.BlockSpec((N_DEV, half(x), x.shape[2]), lambda i, me_ref: (0, i, 0)) for x in lands]
    own_specs = [pl.BlockSpec((None, half(x), x.shape[2]), lambda i, me_ref: (me_ref[0], i, 0)) for x in lands]
    out_specs = tuple(pl.BlockSpec((half(x), x.shape[2]), lambda i, me_ref: (i, 0)) for x in lands)
    blk = sum(_nbytes((N_DEV + 3, half(x), x.shape[2]), BF16) for x in lands)
    grid_spec = pltpu.PrefetchScalarGridSpec(num_scalar_prefetch=1, grid=(2,), in_specs=land_specs + own_specs,
                                             out_specs=out_specs)
    params = pltpu.CompilerParams(dimension_semantics=("parallel",), vmem_limit_bytes=_vmem_limit(blk))
    return pl.pallas_call(body, name=name, out_shape=tuple(_sds(x.shape[1:], F32) for x in lands), grid_spec=grid_spec,
                          compiler_params=params)(me, *lands, *own)


def _reduce_layer(grads, l):
    n = lambda s: f"l{l}_{s}"
    views = [g.reshape(4, 2, g.shape[0] // N_DEV, g.shape[1]) for g in grads]
    recv = _pair_swap(views, name=n("reduce_pair"))
    core = lax.axis_index("c").astype(jnp.int32).reshape(1)
    chip_sum = _add_slabs(views, recv, core, name=n("reduce_pair_add"))
    from_chips = _chip_exchange(chip_sum, name=n("reduce_chips"))
    return _sum_chips(from_chips, name=n("reduce_chips_add"))


def _adamw(w, g, m, v, *, name):
    lead, (r, c) = w.shape[:-2], w.shape[-2:]
    tr = _pick(r, (512, 256, 192, 128, 64, 32, 16, 8))
    c1 = 1.0 / (1.0 - ADAM_B1 ** ADAM_STEP)
    c2 = 1.0 / (1.0 - ADAM_B2 ** ADAM_STEP)

    def body(w_ref, g_ref, m_ref, v_ref, d_ref, nm_ref, nv_ref):
        gv = g_ref[...]
        nm = ADAM_B1 * m_ref[...] + (1.0 - ADAM_B1) * gv
        nv = ADAM_B2 * v_ref[...] + (1.0 - ADAM_B2) * jnp.square(gv)
        d_ref[...] = -ADAM_LR * ((nm * c1) / (jnp.sqrt(nv * c2) + ADAM_EPS) + ADAM_WD * w_ref[...])
        nm_ref[...] = nm
        nv_ref[...] = nv

    if lead:
        blk = pl.BlockSpec((None, tr, c), lambda k, i: (k, i, 0))
        grid, sem = (lead[0], r // tr), ("parallel", "parallel")
    else:
        blk = pl.BlockSpec((tr, c), lambda i: (i, 0))
        grid, sem = (r // tr,), ("parallel",)
    out = _sds(w.shape, F32)
    return _pcall(body, name=name, out_shape=(out, out, out), grid=grid, in_specs=[blk] * 4,
                  out_specs=(blk, blk, blk), semantics=sem, block_bytes=7 * _nbytes((tr, c), F32))(w, g, m, v)


def _pack_flat(arrs, rows, cols=1024):
    flat = jnp.concatenate([a.reshape(-1).astype(F32) for a in arrs])
    pad = rows * cols - flat.shape[0]
    return jnp.pad(flat, (0, pad)).reshape(rows, cols)


def _unpack_flat(buf, shapes):
    flat = buf.reshape(-1)
    out, off = [], 0
    for shp in shapes:
        n = 1
        for s in shp:
            n *= s
        out.append(flat[off:off + n].reshape(shp))
        off += n
    return out


def _flat_rows(shapes, cols=1024):
    n = sum(functools.reduce(lambda a, b: a * b, shp, 1) for shp in shapes)
    rows = -(-n // cols)
    return -(-rows // 64) * 64


def _block_diag(w):
    eye = jnp.eye(N_HEADS, dtype=w.dtype)
    return (w[:, :, :, None, :] * eye[None, :, None, :, None]).reshape(w.shape[0], W_GRP, W_GRP)


def _diag_blocks(w):
    w5 = w.reshape(w.shape[0], N_HEADS, HEAD_DIM, N_HEADS, HEAD_DIM)
    return jnp.stack([w5[:, h, :, h, :] for h in range(N_HEADS)], axis=1)


def _stacked_params(w, lbs):
    tril = jnp.tril(jnp.ones((GMLP_CHUNK, GMLP_CHUNK), bool))
    row = lambda a: a.reshape(DEPTH, 1, -1)
    return dict(
        g1=row(w['norm1_g']), g2=row(w['norm2_g']), g3=row(w['norm3_g']),
        a_ln_g=row(w['a_ln_g']), a_ln_b=row(w['a_ln_b']),
        a_wcat=jnp.where(tril, w['a_ws'], 0.0).reshape(DEPTH, N_HEADS * GMLP_CHUNK, GMLP_CHUNK),
        a_bfull=jnp.repeat(jnp.swapaxes(w['a_bs'], 1, 2), HEAD_DIM, axis=2),
        b_cw=w['b_conv_w_full'], b_cb=row(w['b_conv_b']), b_wa=_block_diag(w['b_wa']), b_ba=row(w['b_ba']),
        b_wx=_block_diag(w['b_wx']), b_bx=row(w['b_bx']), b_lam=row(w['b_lam']),
        c_lb=row(lbs), c_ngf=row(jnp.tile(w['c_norm_g'], (1, N_HEADS))),
        d_wd=_block_diag(w['d_w']), d_scale=row(w['d_scale']),
        f_cw=w['ffn_conv_w_full'], f_cb=row(w['ffn_conv_b']),
    )


B_PRM = ('b_cw', 'b_cb', 'b_wa', 'b_ba', 'b_wx', 'b_bx', 'b_lam')


def _layer_fwd(x, p_bf, wb, sp, l):
    n = lambda s: f"l{l}_{s}"
    h = _rms_fwd(x, sp['g1'], name=n("norm1"))
    z = _matmul(h, wb['w_in'], nt=True, name=n("proj_in"))
    mix = _gmlp_fwd(z, sp['a_ln_g'], sp['a_ln_b'], sp['a_wcat'], sp['a_bfull'], name=n("gmlp"))
    mix, h0s = _rglru_fwd(z, [sp[k] for k in B_PRM], mix, name=n("rglru"))
    mix, sts = _hgrn_fwd(z, sp['c_lb'], sp['c_ngf'], mix, name=n("hgrn"))
    mix = _pool_fwd(z, sp['d_wd'], sp['d_scale'], mix, name=n("pool"))
    x1 = _matmul(mix, wb['w_out'], res=x, name=n("proj_out"))
    h2 = _rms_fwd(x1, sp['g2'], name=n("norm2"))
    hg = _matmul(h2, wb['w_up_g'], nt=True, name=n("up_gate"))
    hv = _matmul(h2, wb['w_up_v'], nt=True, name=n("up_val"))
    a = _ffn_fwd(hg, hv, sp['f_cw'], sp['f_cb'], name=n("ffn_gate"))
    x2 = _matmul(a, wb['w_down'], res=x1, name=n("down"))
    h3 = _rms_fwd(x2, sp['g3'], name=n("norm3"))
    gl = _matmul(h3, wb['w_pg'], name=n("ple_gate"))
    pe = _matmul(p_bf, wb['w_pe'], nt=True, name=n("ple_emb"))
    x3 = _ple_fwd(x2, gl, pe, name=n("ple"))
    saved = dict(x=x, h=h, z=z, h0s=h0s, sts=sts, mix=mix, x1=x1, h2=h2, hg=hg, hv=hv, a=a, x2=x2, h3=h3, gl=gl, pe=pe)
    return x3, saved


def _layer_bwd(dx3, sv, p_bf, wb, sp, l):
    n = lambda s: f"l{l}_{s}_bwd"
    gb, gs = {}, {}
    dpe, dgl = _ple_bwd(dx3, sv['gl'], sv['pe'], name=n("ple"))
    gb['w_pe'] = _matmul_tn(dpe, p_bf, name=n("ple_emb_w"))
    gb['w_pg'] = _matmul_tn(sv['h3'], dgl, name=n("ple_gate_w"))
    dh3 = _matmul(dgl, wb['w_pg'], nt=True, name=n("ple_gate_x"))
    dx2, dx2b, gs['norm3_g'] = _rms_bwd(sv['x2'], sp['g3'], dh3, dx3, name=n("norm3"))
    da = _matmul(dx2b, wb['w_down'], nt=True, name=n("down_x"))
    gb['w_down'] = _matmul_tn(sv['a'], dx2b, name=n("down_w"))
    dhg, dhv, gs['f_dwg'], gs['f_dwv'] = _ffn_bwd(sv['hg'], sv['hv'], da, sp['f_cw'], sp['f_cb'], name=n("ffn_gate"))
    gate_rows = _matmul_tn(dhg, sv['h2'], name=n("up_gate_w"), out_rows=2 * D_FF)
    gb['w_up'] = _matmul_tn(dhv, sv['h2'], name=n("up_val_w"), out_rows=2 * D_FF, row_off=D_FF, into=gate_rows)
    dh2 = _matmul(dhg, wb['w_up_g'], name=n("up_gate_x"))
    dh2 = _matmul(dhv, wb['w_up_v'], res=dh2, name=n("up_val_x"))
    dx1, dx1b, gs['norm2_g'] = _rms_bwd(sv['x1'], sp['g2'], dh2, dx2, name=n("norm2"))
    dmix = _matmul(dx1b, wb['w_out'], nt=True, name=n("proj_out_x"))
    gb['w_out'] = _matmul_tn(sv['mix'], dx1b, name=n("proj_out_w"))
    z = sv['z']
    dz, gs['a_ln_g'], gs['a_ln_b'], gs['a_wcat'], gs['a_bfull'] = _gmlp_bwd(
        z, dmix, sp['a_ln_g'], sp['a_ln_b'], sp['a_wcat'], sp['a_bfull'], name=n("gmlp"))
    dz, *dbp = _rglru_bwd(z, dmix, sv['h0s'], [sp[k] for k in B_PRM], dz, name=n("rglru"))
    gs.update(zip(B_PRM, dbp))
    dz, gs['c_lb'], gs['c_ngf'] = _hgrn_bwd(z, dmix, sv['sts'], sp['c_lb'], sp['c_ngf'], dz, name=n("hgrn"))
    dz, gs['d_wd'], gs['d_scale'] = _pool_bwd(z, dmix, sp['d_wd'], sp['d_scale'], dz, name=n("pool"))
    gb['w_in'] = _matmul_tn(dz, sv['h'], name=n("proj_in_w"))
    dh = _matmul(dz, wb['w_in'], name=n("proj_in_x"))
    dx0, _, gs['norm1_g'] = _rms_bwd(sv['x'], sp['g1'], dh, dx1, name=n("norm1"))
    return dx0, gb, gs


SMALL_NAMES = [nm for nm in WEIGHT_NAMES if nm not in BIG_NAMES]
COL_SHARDED = ('w_in', 'w_up', 'w_pe')


def _comm_shards(w):
    return [(jnp.swapaxes(w[nm], 1, 2) if nm in COL_SHARDED else w[nm]).astype(BF16) for nm, _, _ in BIG_COMM]


def _full_weights(gathered):
    out = {nm: g.reshape(N_DEV * r, c) for g, (nm, r, c) in zip(gathered, BIG_COMM)}
    halves = out.pop('w_up').reshape(2, D_FF, D_MODEL)
    out['w_up_g'], out['w_up_v'] = _Sel(halves, 0), _Sel(halves, 1)
    return out


def _small_grads(raw):
    st = {k: jnp.stack([raw[l][k] for l in range(DEPTH)]) for k in raw[0]}
    tril = jnp.tril(jnp.ones((GMLP_CHUNK, GMLP_CHUNK), bool))
    vec = lambda a: a.reshape(DEPTH, -1)
    out = {nm: vec(st[k]) for nm, k in (('norm1_g', 'norm1_g'), ('norm2_g', 'norm2_g'), ('norm3_g', 'norm3_g'),
                                        ('a_ln_g', 'a_ln_g'), ('a_ln_b', 'a_ln_b'), ('b_conv_b', 'b_cb'),
                                        ('b_ba', 'b_ba'), ('b_bx', 'b_bx'), ('b_lam', 'b_lam'), ('c_lb', 'c_lb'),
                                        ('d_scale', 'd_scale'))}
    out['a_ws'] = jnp.where(tril, st['a_wcat'].reshape(DEPTH, N_HEADS, GMLP_CHUNK, GMLP_CHUNK), 0.0)
    out['a_bs'] = jnp.swapaxes(st['a_bfull'].reshape(DEPTH, GMLP_CHUNK, N_HEADS, HEAD_DIM).sum(-1), 1, 2)
    out['b_conv_w'] = st['b_cw']
    out['b_wa'], out['b_wx'], out['d_w'] = _diag_blocks(st['b_wa']), _diag_blocks(st['b_wx']), _diag_blocks(st['d_wd'])
    out['c_norm_g'] = st['c_ngf'].reshape(DEPTH, N_HEADS, HEAD_DIM).sum(1)
    out['ffn_conv_w'] = jnp.concatenate([st['f_dwg'][:, 0:3], st['f_dwv'][:, 0:3]], axis=2)
    out['ffn_conv_b'] = jnp.concatenate([st['f_dwg'][:, 3], st['f_dwv'][:, 3]], axis=1)
    return out


def _step(w, m, v, x, p, target):
    s = x.shape[1]
    dev = 4 * lax.axis_index("x") + 2 * lax.axis_index("y") + lax.axis_index("c")
    xs = x.reshape(s, D_MODEL)

    shards = _comm_shards(w)
    conv_shapes = [w['b_conv_w'].shape, w['ffn_conv_w'].shape]
    conv_rows = _flat_rows(conv_shapes)
    conv_all = _all_gather(_pack_flat([w['b_conv_w'], w['ffn_conv_w']], conv_rows), name="gather_conv_weights")
    parts = [_unpack_flat(conv_all[d], conv_shapes) for d in range(N_DEV)]
    wf = dict(w)
    wf['b_conv_w_full'] = jnp.concatenate([pt[0] for pt in parts], axis=-1)
    wf['ffn_conv_w_full'] = jnp.concatenate([pt[1] for pt in parts], axis=-1)
    lbs = _lbs_fwd(w['c_lb'], name="hgrn_bounds")

    stacked = _stacked_params(wf, lbs)
    p_all = p.reshape(DEPTH, s, PLE_DIM).astype(BF16)
    xl, saved, wbs, sps = xs, [], [], []
    gathered = _gather_layer(shards, 0, name="l0_gather_weights")
    for l in range(DEPTH):
        sp = {k: _Sel(a, l) for k, a in stacked.items()}
        if l + 1 < DEPTH:
            own = [x[l + 1] for x in shards]
            after = [conv_all, *gathered] if l == 0 else [xl]
            lands = _place_own(own, after, name=f"l{l + 1}_gather_place")
            started = _exchange_start(own, lands, name=f"l{l + 1}_gather_start")
            sp['g1'] = stacked['g1'][l] + started[-1][0, 0]
        wb = _full_weights(gathered)
        p_bf = p_all[l]
        xl, sv = _layer_fwd(xl, p_bf, wb, sp, l)
        if l + 1 < DEPTH:
            gathered = _exchange_wait(started, xl, name=f"l{l + 1}_gather_wait")[1]
        saved.append((sv, p_bf))
        wbs.append(wb)
        sps.append(sp)
    loss_part, dx, dfinal = _loss_head(xl, w['final_g'].reshape(1, D_MODEL), target.reshape(s, D_MODEL), name="loss_head")
    loss = lax.psum(loss_part[0, 0], ("x", "y", "c"))

    reduced, small = [None] * DEPTH, [None] * DEPTH
    pending = None
    for l in range(DEPTH - 1, -1, -1):
        sv, p_bf = saved[l]
        sp = sps[l]
        if pending is not None:
            sp = dict(sp, g3=stacked['g3'][l] + pending[1][-1][0, 0])
        dx, gb, small[l] = _layer_bwd(dx, sv, p_bf, wbs[l], sp, l)
        grads =[gb[nm] for nm, _, _ in BIG_COMM]
        if pending is not None:
            own, lands = _exchange_wait(pending[1], dx, name=f"l{pending[0]}_reduce_wait")
            reduced[pending[0]] = _sum_devices(lands, own, dev.astype(jnp.int32).reshape(1), name=f"l{pending[0]}_reduce_sum")
            pending = None
        if l > 0:
            views = [g.reshape(N_DEV, g.shape[0] // N_DEV, g.shape[1]) for g in grads]
            lands = [lax.empty(g.shape, g.dtype) for g in views]
            pending = (l, _exchange_start(views, lands, name=f"l{l}_reduce_start", per_peer=True))
        else:
            reduced[0] = _reduce_layer(grads, 0)
    grad_x = dx.reshape(1, s, D_MODEL)
    gbig = {}
    for a, (nm, _, _) in enumerate(BIG_COMM):
        g = jnp.stack([reduced[l][a] for l in range(DEPTH)])
        gbig[nm] = jnp.swapaxes(g, 1, 2) if nm in COL_SHARDED else g

    small_parts = _small_grads(small)
    small_parts['c_lb'] = _lbs_bwd(w['c_lb'], small_parts['c_lb'], name="hgrn_bounds_bwd")
    small_parts['final_g'] = dfinal.reshape(D_MODEL)
    small_shapes = [small_parts[nm].shape for nm in SMALL_NAMES]
    small_rows = _flat_rows(small_shapes)
    small_all = _all_gather(_pack_flat([small_parts[nm] for nm in SMALL_NAMES], small_rows), name="gather_small_grads")
    gsmall = dict(zip(SMALL_NAMES, _unpack_flat(_sum_slots(small_all, name="sum_small_grads"), small_shapes)))
    for nm in ('b_conv_w', 'ffn_conv_w'):
        width = w[nm].shape[-1]
        gsmall[nm] = lax.dynamic_slice_in_dim(gsmall[nm], dev * width, width, axis=2)

    grads, delta, new_m, new_v = {}, {}, {}, {}
    for nm in BIG_NAMES:
        grads[nm] = gbig[nm]
        delta[nm], new_m[nm], new_v[nm] = _adamw(w[nm], gbig[nm], m[nm], v[nm], name=f"adamw_{nm}")
    shapes = [w[nm].shape for nm in SMALL_NAMES]
    rows = _flat_rows(shapes)
    pk = lambda t: _pack_flat([t[nm] for nm in SMALL_NAMES], rows)
    d, nm_, nv_ = _adamw(pk(w), pk(gsmall), pk(m), pk(v), name="adamw_small")
    for nm, dd, mm_, vv_ in zip(SMALL_NAMES, _unpack_flat(d, shapes), _unpack_flat(nm_, shapes), _unpack_flat(nv_, shapes)):
        grads[nm], delta[nm], new_m[nm], new_v[nm] = gsmall[nm], dd, mm_, vv_

    return (loss, grad_x, *[grads[nm] for nm in WEIGHT_NAMES], *[delta[nm] for nm in WEIGHT_NAMES],
            *[new_m[nm] for nm in WEIGHT_NAMES], *[new_v[nm] for nm in WEIGHT_NAMES])


def kernel(x, p, norm1_g, w_in, a_ln_g, a_ln_b, a_ws, a_bs, b_conv_w, b_conv_b, b_wa, b_ba, b_wx, b_bx, b_lam, c_lb, c_norm_g, d_w, d_scale, w_out, norm2_g, w_up, ffn_conv_w, ffn_conv_b, w_down, norm3_g, w_pe, w_pg, final_g, loss_target, m_norm1_g, m_w_in, m_a_ln_g, m_a_ln_b, m_a_ws, m_a_bs, m_b_conv_w, m_b_conv_b, m_b_wa, m_b_ba, m_b_wx, m_b_bx, m_b_lam, m_c_lb, m_c_norm_g, m_d_w, m_d_scale, m_w_out, m_norm2_g, m_w_up, m_ffn_conv_w, m_ffn_conv_b, m_w_down, m_norm3_g, m_w_pe, m_w_pg, m_final_g, v_norm1_g, v_w_in, v_a_ln_g, v_a_ln_b, v_a_ws, v_a_bs, v_b_conv_w, v_b_conv_b, v_b_wa, v_b_ba, v_b_wx, v_b_bx, v_b_lam, v_c_lb, v_c_norm_g, v_d_w, v_d_scale, v_w_out, v_norm2_g, v_w_up, v_ffn_conv_w, v_ffn_conv_b, v_w_down, v_norm3_g, v_w_pe, v_w_pg, v_final_g):
    w = dict(norm1_g=norm1_g, w_in=w_in, a_ln_g=a_ln_g, a_ln_b=a_ln_b, a_ws=a_ws, a_bs=a_bs, b_conv_w=b_conv_w, b_conv_b=b_conv_b, b_wa=b_wa, b_ba=b_ba, b_wx=b_wx, b_bx=b_bx, b_lam=b_lam, c_lb=c_lb, c_norm_g=c_norm_g, d_w=d_w, d_scale=d_scale, w_out=w_out, norm2_g=norm2_g, w_up=w_up, ffn_conv_w=ffn_conv_w, ffn_conv_b=ffn_conv_b, w_down=w_down, norm3_g=norm3_g, w_pe=w_pe, w_pg=w_pg, final_g=final_g)
    m = dict(norm1_g=m_norm1_g, w_in=m_w_in, a_ln_g=m_a_ln_g, a_ln_b=m_a_ln_b, a_ws=m_a_ws, a_bs=m_a_bs, b_conv_w=m_b_conv_w, b_conv_b=m_b_conv_b, b_wa=m_b_wa, b_ba=m_b_ba, b_wx=m_b_wx, b_bx=m_b_bx, b_lam=m_b_lam, c_lb=m_c_lb, c_norm_g=m_c_norm_g, d_w=m_d_w, d_scale=m_d_scale, w_out=m_w_out, norm2_g=m_norm2_g, w_up=m_w_up, ffn_conv_w=m_ffn_conv_w, ffn_conv_b=m_ffn_conv_b, w_down=m_w_down, norm3_g=m_norm3_g, w_pe=m_w_pe, w_pg=m_w_pg, final_g=m_final_g)
    v = dict(norm1_g=v_norm1_g, w_in=v_w_in, a_ln_g=v_a_ln_g, a_ln_b=v_a_ln_b, a_ws=v_a_ws, a_bs=v_a_bs, b_conv_w=v_b_conv_w, b_conv_b=v_b_conv_b, b_wa=v_b_wa, b_ba=v_b_ba, b_wx=v_b_wx, b_bx=v_b_bx, b_lam=v_b_lam, c_lb=v_c_lb, c_norm_g=v_c_norm_g, d_w=v_d_w, d_scale=v_d_scale, w_out=v_w_out, norm2_g=v_norm2_g, w_up=v_w_up, ffn_conv_w=v_ffn_conv_w, ffn_conv_b=v_ffn_conv_b, w_down=v_w_down, norm3_g=v_norm3_g, w_pe=v_w_pe, w_pg=v_w_pg, final_g=v_final_g)
    return _step(w, m, v, x, p, loss_target)
```

```python
import functools

import jax
import jax.numpy as jnp
from jax import lax
from jax.experimental import pallas as pl
from jax.experimental.pallas import tpu as pltpu

F32 = jnp.float32
BF16 = jnp.bfloat16
MESH = pl.DeviceIdType.MESH

D_MODEL = 1024
DEPTH = 4
PLE_DIM = 256
W_GRP = 256
N_HEADS = 4
HEAD_DIM = 64
GMLP_CHUNK = 128
RGLRU_C = 8.0
HGRN_CHUNK = 64
HGRN_SUB = 16
HGRN_STEP_CHUNKS = 4
POOL_WINDOWS = (2, 4, 8, 16)
D_FF = 2816
D_PROJ = 2304
EPS = 1e-6
ADAM_LR = 0.001
ADAM_B1 = 0.9
ADAM_B2 = 0.999
ADAM_EPS = 1e-08
ADAM_WD = 0.01
ADAM_STEP = 10

N_DEV = 8
MIB = 2 ** 20
V7X_VMEM_BYTES = 64 * MIB
HGRN_EXP_CLAMP = 60.0

WEIGHT_NAMES = ['norm1_g', 'w_in', 'a_ln_g', 'a_ln_b', 'a_ws', 'a_bs', 'b_conv_w', 'b_conv_b', 'b_wa', 'b_ba', 'b_wx',
                'b_bx', 'b_lam', 'c_lb', 'c_norm_g', 'd_w', 'd_scale', 'w_out', 'norm2_g', 'w_up', 'ffn_conv_w',
                'ffn_conv_b', 'w_down', 'norm3_g', 'w_pe', 'w_pg', 'final_g']
BIG_NAMES = ('w_in', 'w_out', 'w_up', 'w_down', 'w_pe', 'w_pg')


def _vmem_limit(block_bytes):
    want = 2 * block_bytes + 24 * MIB
    return int(min(max(want, 32 * MIB), V7X_VMEM_BYTES - 8 * MIB))


def _pcall(body, *, name, out_shape, grid=None, in_specs=None, out_specs=None, scratch_shapes=(),
           semantics=None, block_bytes=0, aliases=None):
    kw = {} if aliases is None else {"input_output_aliases": aliases}
    if grid is not None:
        kw["grid"] = grid
    if in_specs is not None:
        kw["in_specs"] = in_specs
    if out_specs is not None:
        kw["out_specs"] = out_specs
    params = pltpu.CompilerParams(dimension_semantics=semantics, vmem_limit_bytes=_vmem_limit(block_bytes))
    return pl.pallas_call(body, name=name, out_shape=out_shape, scratch_shapes=list(scratch_shapes),
                          compiler_params=params, **kw)


def _pick(n, cands):
    for c in cands:
        if n % c == 0:
            return c
    return n


def _nbytes(shape, dtype):
    n = 1
    for s in shape:
        n *= s
    return n * jnp.dtype(dtype).itemsize


def _sds(shape, dtype):
    return jax.ShapeDtypeStruct(tuple(shape), dtype)


class _Sel:
    def __init__(self, arr, *idx):
        self.arr, self.idx = arr, tuple(idx)
        self.shape = arr.shape[len(idx):]
        self.ndim = len(self.shape)
        self.dtype = arr.dtype


def _arr(a):
    return a.arr if isinstance(a, _Sel) else a


def _spec(a, block=None, index=None):
    block = tuple(a.shape) if block is None else tuple(block)
    index = (lambda *g: (0,) * len(block)) if index is None else index
    if isinstance(a, _Sel):
        lead = a.idx
        return pl.BlockSpec((None,) * len(lead) + block, lambda *g: lead + tuple(index(*g)))
    return pl.BlockSpec(block, lambda *g: tuple(index(*g)))


def _ospec(a):
    return pl.BlockSpec(tuple(a.shape), lambda *g: (0,) * a.ndim)


def _rows_of(shape):
    return lax.broadcasted_iota(jnp.int32, shape, 0)


def _lanes_of(shape):
    return lax.broadcasted_iota(jnp.int32, shape, 1)


def _sdn(x, k, fill):
    n = x.shape[0]
    return jnp.where(_rows_of(x.shape) >= k, pltpu.roll(x, k % n, 0), fill)


def _sup(x, k, fill):
    n = x.shape[0]
    return jnp.where(_rows_of(x.shape) < n - k, pltpu.roll(x, (n - k) % n, 0), fill)


@functools.partial(jax.custom_vjp, nondiff_argnums=(1,))
def _shift_dn(x, k):
    return pltpu.roll(x, k, 0)


def _shift_dn_fwd(x, k):
    return pltpu.roll(x, k, 0), None


def _shift_dn_bwd(k, _, g):
    return (pltpu.roll(g, g.shape[0] - k, 0),)


_shift_dn.defvjp(_shift_dn_fwd, _shift_dn_bwd)


def _lin_scan_impl(a, b, h0):
    n = a.shape[0]
    aa, bb = a, b
    k = 1
    while k < n:
        bb = aa * _sdn(bb, k, 0.0) + bb
        aa = aa * _sdn(aa, k, 1.0)
        k *= 2
    return bb + aa * h0


@jax.custom_vjp
def _lin_scan(a, b, h0):
    return _lin_scan_impl(a, b, h0)


def _lin_scan_fwd(a, b, h0):
    h = _lin_scan_impl(a, b, h0)
    return h, (a, h, h0)


def _lin_scan_bwd(res, g):
    a, h, h0 = res
    n = a.shape[0]
    cc, gg = _sup(a, 1, 0.0), g
    k = 1
    while k < n:
        gg = gg + cc * _sup(gg, k, 0.0)
        cc = cc * _sup(cc, k, 1.0)
        k *= 2
    first = _rows_of(a.shape) == 0
    hprev = jnp.where(first, h0, _sdn(h, 1, 0.0))
    dh0 = jnp.sum(jnp.where(first, a * gg, 0.0), axis=0, keepdims=True)
    return gg * hprev, gg, dh0


_lin_scan.defvjp(_lin_scan_fwd, _lin_scan_bwd)


def _cumsum_sub_impl(x):
    pos = _rows_of(x.shape) % HGRN_SUB
    k = 1
    while k < HGRN_SUB:
        x = x + jnp.where(pos >= k, pltpu.roll(x, k, 0), 0.0)
        k *= 2
    return x


@jax.custom_vjp
def _cumsum_sub(x):
    return _cumsum_sub_impl(x)


def _cumsum_sub_fwd(x):
    return _cumsum_sub_impl(x), None


def _cumsum_sub_bwd(_, g):
    n = g.shape[0]
    pos = _rows_of(g.shape) % HGRN_SUB
    k = 1
    while k < HGRN_SUB:
        g = g + jnp.where(pos < HGRN_SUB - k, pltpu.roll(g, n - k, 0), 0.0)
        k *= 2
    return (g,)


_cumsum_sub.defvjp(_cumsum_sub_fwd, _cumsum_sub_bwd)


def _dot(a, b, ca, cb):
    return lax.dot_general(a.astype(BF16), b.astype(BF16), (((ca,), (cb,)), ((), ())), preferred_element_type=F32)


@jax.custom_vjp
def _mm(a, b):
    return _dot(a, b, 1, 0)


def _mm_fwd(a, b):
    return _dot(a, b, 1, 0), (a, b)


def _mm_bwd(res, g):
    a, b = res
    return _dot(g, b, 1, 1), _dot(a, g, 0, 0)


_mm.defvjp(_mm_fwd, _mm_bwd)


@jax.custom_vjp
def _mm_nt(a, b):
    return _dot(a, b, 1, 1)


def _mm_nt_fwd(a, b):
    return _dot(a, b, 1, 1), (a, b)


def _mm_nt_bwd(res, g):
    a, b = res
    return _dot(g, b, 1, 0), _dot(g, a, 0, 0)


_mm_nt.defvjp(_mm_nt_fwd, _mm_nt_bwd)


@jax.custom_vjp
def _mm_tn(a, b):
    return _dot(a, b, 0, 0)


def _mm_tn_fwd(a, b):
    return _dot(a, b, 0, 0), (a, b)


def _mm_tn_bwd(res, g):
    a, b = res
    return _dot(b, g, 1, 1), _dot(a, g, 1, 0)


_mm_tn.defvjp(_mm_tn_fwd, _mm_tn_bwd)


def _head_mask(shape, h):
    return (_lanes_of(shape) // HEAD_DIM) == h


def _stack_heads(x):
    return jnp.concatenate([jnp.where(_head_mask(x.shape, h), x, 0.0) for h in range(N_HEADS)], axis=0)


def _unstack_heads(p):
    r = p.shape[0] // N_HEADS
    out = None
    for h in range(N_HEADS):
        blk = p[h * r:(h + 1) * r]
        term = jnp.where(_head_mask(blk.shape, h), blk, 0.0)
        out = term if out is None else out + term
    return out


def _segmean_impl(x):
    n = x.shape[1]
    same = (lax.broadcasted_iota(jnp.int32, (n, n), 0) // HEAD_DIM) == (lax.broadcasted_iota(jnp.int32, (n, n), 1) // HEAD_DIM)
    m = jnp.where(same, 1.0 / HEAD_DIM, 0.0).astype(BF16)
    hi = x.astype(BF16)
    lo = (x - hi.astype(F32)).astype(BF16)
    dn = (((1,), (0,)), ((), ()))
    return (lax.dot_general(hi, m, dn, preferred_element_type=F32)
            + lax.dot_general(lo, m, dn, preferred_element_type=F32))


@jax.custom_vjp
def _segmean(x):
    return _segmean_impl(x)


def _segmean_fwd(x):
    return _segmean_impl(x), None


def _segmean_bwd(_, g):
    return (_segmean_impl(g),)


_segmean.defvjp(_segmean_fwd, _segmean_bwd)


def _log1p(u):
    w = 1.0 + u
    return jnp.where(w == 1.0, u, jnp.log(w) * (u / (w - 1.0)))


def _softplus(y):
    return jnp.maximum(y, 0.0) + _log1p(jnp.exp(-jnp.abs(y)))


def _rms(x, g):
    return x * lax.rsqrt(jnp.mean(x * x, axis=-1, keepdims=True) + EPS) * g


def _gmlp_chunk(zu, zv, ln_g, ln_b, wcat, bfull):
    u = jax.nn.gelu(zu)
    v = jax.nn.gelu(zv)
    mu = jnp.mean(v, axis=-1, keepdims=True)
    var = jnp.mean(jnp.square(v - mu), axis=-1, keepdims=True)
    vn = (v - mu) * lax.rsqrt(var + EPS) * ln_g + ln_b
    sv = _unstack_heads(_mm(wcat, vn)) + bfull
    return u * sv


def _rglru_tile(xb_ext, gb, h0, cw, cb, wa, ba, wx, bx, lam):
    xc = (cb + cw[0:1] * _shift_dn(xb_ext, 3) + cw[1:2] * _shift_dn(xb_ext, 2) + cw[2:3] * _shift_dn(xb_ext, 1)
          + cw[3:4] * xb_ext)[8:]
    r = jax.nn.sigmoid(_mm(xc, wa) + ba)
    i = jax.nn.sigmoid(_mm(xc, wx) + bx)
    log_a = (-RGLRU_C) * r * _softplus(-lam)
    a = jnp.exp(log_a)
    mult = jnp.sqrt(-jnp.tanh(log_a) * (a * a + 1.0))
    h = _lin_scan(a, mult * (i * xc), h0)
    y = h * jax.nn.gelu(gb)
    h_last = jnp.sum(jnp.where(_rows_of(h.shape) == h.shape[0] - 1, h, 0.0), axis=0, keepdims=True)
    return y, h_last


def _pool_tile(xd_ext, inv, wd, scale):
    s1 = xd_ext + _shift_dn(xd_ext, 1)
    s2 = s1 + _shift_dn(s1, 2)
    s3 = s2 + _shift_dn(s2, 4)
    s4 = s3 + _shift_dn(s3, 8)
    grp = _lanes_of(xd_ext.shape) // HEAD_DIM
    win = jnp.where(grp == 0, s1, jnp.where(grp == 1, s2, jnp.where(grp == 2, s3, s4)))
    pooled = win[16:] * inv - xd_ext[16:]
    return _mm(pooled, wd) * scale


def _hgrn_chunk(q, f, i, g, st, lb, ngf):
    n = q.shape[0]
    nsub = n // HGRN_SUB
    qs = jax.nn.silu(q)
    fg = lb + (1.0 - lb) * jax.nn.sigmoid(f)
    lf = jnp.log(fg)
    k = 1.0 - fg
    bl = _cumsum_sub(lf)
    row = _rows_of(q.shape)
    blk = row // HGRN_SUB
    betas = [jnp.zeros_like(lb)]
    for s in range(nsub):
        tot = jnp.sum(jnp.where(row == s * HGRN_SUB + HGRN_SUB - 1, bl, 0.0), axis=0, keepdims=True)
        betas.append(betas[-1] + tot)
    b_end = betas[nsub]
    beta_full = jnp.zeros_like(q)
    for s in range(1, nsub):
        beta_full = jnp.where(blk == s, betas[s], beta_full)
    qh = qs * jnp.exp(bl)
    qt = qh * jnp.exp(beta_full)
    b_all = beta_full + bl
    kt = k * jnp.exp(b_end - b_all)
    outs = []
    for s in range(nsub):
        kh = k * jnp.exp(jnp.minimum(betas[s] - b_all, HGRN_EXP_CLAMP))
        qstk = _stack_heads(qh[s * HGRN_SUB:(s + 1) * HGRN_SUB])
        att = _mm_nt(qstk, kh)
        ar = _rows_of(att.shape) % HGRN_SUB + s * HGRN_SUB
        att = jnp.where(_lanes_of(att.shape) <= ar, att, 0.0)
        outs.append(_unstack_heads(_mm(att, i)))
    o = jnp.concatenate(outs, axis=0) + _mm_nt(qt, st)
    same = (_rows_of(st.shape) // HEAD_DIM) == (_lanes_of(st.shape) // HEAD_DIM)
    st_new = st * jnp.exp(b_end) + jnp.where(same, _mm_tn(i, kt), 0.0)
    on = o * lax.rsqrt(_segmean(o * o) + EPS) * ngf
    return on * jax.nn.silu(g), st_new


def _ffn_tile(eg, ev, wg, bg, wv, bv):
    gt = (bg + wg[0:1] * _shift_dn(eg, 2) + wg[1:2] * _shift_dn(eg, 1) + wg[2:3] * eg)[8:]
    val = (bv + wv[0:1] * _shift_dn(ev, 2) + wv[1:2] * _shift_dn(ev, 1) + wv[2:3] * ev)[8:]
    return jax.nn.gelu(gt) * val


MXU_WIDTH = 256
MATMUL_BLOCK_BUDGET = 18 * MIB


def _matmul_tiles(m, k, n, a_dtype, b_dtype, out_dtype, has_res):
    best = None
    for tm in (2048, 1024, 512, 256):
        if m % tm:
            continue
        for tn in (1024, 768, 1408, 512, 256, 128):
            if n % tn:
                continue
            blk = (_nbytes((tm, k), a_dtype) + _nbytes((k, tn), b_dtype) + _nbytes((tm, tn), out_dtype)
                   + (_nbytes((tm, tn), F32) if has_res else 0))
            if blk > MATMUL_BLOCK_BUDGET:
                continue
            waste = -(-tn // MXU_WIDTH) * MXU_WIDTH / tn
            cost = (m // tm) * (n // tn) + 64 * (waste - 1.0)
            if best is None or cost < best[0]:
                best = (cost, tm, tn, blk)
    assert best is not None, (m, k, n)
    return best[1:]


def _matmul(a, b, *, name, nt=False, res=None, out_dtype=F32):
    m, k = a.shape
    n = b.shape[0] if nt else b.shape[1]
    tm, tn, blk = _matmul_tiles(m, k, n, a.dtype, b.dtype, out_dtype, res is not None)
    dims = (((1,), (1,)), ((), ())) if nt else (((1,), (0,)), ((), ()))

    def body(*refs):
        if res is None:
            a_ref, b_ref, o_ref = refs
        else:
            a_ref, b_ref, r_ref, o_ref = refs
        acc = lax.dot_general(a_ref[...], b_ref[...], dims, preferred_element_type=F32)
        if res is not None:
            acc = acc + r_ref[...]
        o_ref[...] = acc.astype(out_dtype)

    in_specs = [pl.BlockSpec((tm, k), lambda i, j: (i, 0)),
                _spec(b, (tn, k), lambda i, j: (j, 0)) if nt else _spec(b, (k, tn), lambda i, j: (0, j))]
    args = [a, _arr(b)]
    if res is not None:
        in_specs.append(pl.BlockSpec((tm, tn), lambda i, j: (i, j)))
        args.append(res)
    return _pcall(body, name=name, out_shape=_sds((m, n), out_dtype), grid=(m // tm, n // tn), in_specs=in_specs,
                  out_specs=pl.BlockSpec((tm, tn), lambda i, j: (i, j)), semantics=("parallel", "parallel"),
                  block_bytes=blk + _nbytes((tm, tn), F32))(*args)


def _matmul_tn(a, b, *, name, out_dtype=BF16, out_rows=None, row_off=0, into=None):
    m, k1 = a.shape
    n = b.shape[1]
    tk = _pick(k1, (512, 256, 128))
    off = row_off // tk
    assert off * tk == row_off

    def body(a_ref, b_ref, *rest):
        rest[-1][...] = lax.dot_general(a_ref[...], b_ref[...], (((0,), (0,)), ((), ())),
                                        preferred_element_type=F32).astype(out_dtype)

    blk = 2 * _nbytes((m, tk), a.dtype) + _nbytes((m, n), b.dtype) + _nbytes((tk, n), F32)
    in_specs = [pl.BlockSpec((m, tk), lambda i: (0, i)), pl.BlockSpec((m, n), lambda i: (0, 0))]
    args = [a, b]
    if into is not None:
        in_specs.append(HBM_SPEC)
        args.append(into)
    return _pcall(body, name=name, out_shape=_sds((out_rows or k1, n), out_dtype), grid=(k1 // tk,), in_specs=in_specs,
                  out_specs=pl.BlockSpec((tk, n), lambda i: (i + off, 0)), semantics=("parallel",), block_bytes=blk,
                  aliases=None if into is None else {2: 0})(*args)


def _rms_fwd(x, g, *, name):
    s, d = x.shape
    tm = _pick(s, (512, 256))

    def body(x_ref, g_ref, o_ref):
        o_ref[...] = _rms(x_ref[...], g_ref[...]).astype(BF16)

    return _pcall(body, name=name, out_shape=_sds((s, d), BF16), grid=(s // tm,),
                  in_specs=[pl.BlockSpec((tm, d), lambda i: (i, 0)), _spec(g)],
                  out_specs=pl.BlockSpec((tm, d), lambda i: (i, 0)), semantics=("parallel",),
                  block_bytes=3 * _nbytes((tm, d), F32))(x, _arr(g))


def _rms_bwd(x, g, dh, dres, *, name):
    s, d = x.shape
    tm = _pick(s, (256, 128))

    def body(x_ref, g_ref, dh_ref, dr_ref, dx_ref, dxb_ref, dg_ref):
        _, vjp = jax.vjp(_rms, x_ref[...], g_ref[...])
        dxn, dg = vjp(dh_ref[...])
        dx = dr_ref[...] + dxn
        dx_ref[...] = dx
        dxb_ref[...] = dx.astype(BF16)

        @pl.when(pl.program_id(0) == 0)
        def _():
            dg_ref[...] = jnp.zeros_like(dg_ref)

        dg_ref[...] += dg

    row = pl.BlockSpec((tm, d), lambda i: (i, 0))
    vec = pl.BlockSpec((1, d), lambda i: (0, 0))
    return _pcall(body, name=name, out_shape=(_sds((s, d), F32), _sds((s, d), BF16), _sds((1, d), F32)),
                  grid=(s // tm,), in_specs=[row, _spec(g), row, row], out_specs=(row, row, vec),
                  semantics=("arbitrary",), block_bytes=8 * _nbytes((tm, d), F32))(x, _arr(g), dh, dres)


def _ple_fwd(x, gl, pe, *, name):
    s, d = x.shape
    tm = _pick(s, (512, 256))

    def body(x_ref, gl_ref, pe_ref, o_ref):
        o_ref[...] = x_ref[...] + pe_ref[...] * jax.nn.sigmoid(gl_ref[...])

    row = pl.BlockSpec((tm, d), lambda i: (i, 0))
    return _pcall(body, name=name, out_shape=_sds((s, d), F32), grid=(s // tm,), in_specs=[row, row, row],
                  out_specs=row, semantics=("parallel",), block_bytes=4 * _nbytes((tm, d), F32))(x, gl, pe)


def _ple_bwd(dx, gl, pe, *, name):
    s, d = dx.shape
    tm = _pick(s, (512, 256))

    def body(dx_ref, gl_ref, pe_ref, dpe_ref, dgl_ref):
        gate = jax.nn.sigmoid(gl_ref[...])
        dxv = dx_ref[...]
        dpe_ref[...] = (dxv * gate).astype(BF16)
        dgl_ref[...] = (dxv * pe_ref[...] * gate * (1.0 - gate)).astype(BF16)

    row = pl.BlockSpec((tm, d), lambda i: (i, 0))
    return _pcall(body, name=name, out_shape=(_sds((s, d), BF16), _sds((s, d), BF16)), grid=(s // tm,),
                  in_specs=[row, row, row], out_specs=(row, row), semantics=("parallel",),
                  block_bytes=5 * _nbytes((tm, d), F32))(dx, gl, pe)


def _loss_head(x, g, target, *, name):
    s, d = x.shape
    tm = _pick(s, (256, 128))

    def tile_loss(xv, gv, tv):
        err = jnp.square(_rms(xv, gv) - tv)
        return 0.5 * jnp.sum(jnp.mean(err, axis=-1, keepdims=True), axis=0, keepdims=True)

    def body(x_ref, g_ref, t_ref, l_ref, dx_ref, dg_ref):
        lv, vjp = jax.vjp(tile_loss, x_ref[...], g_ref[...], t_ref[...])
        dxv, dgv, _ = vjp(jnp.ones((1, 1), F32))
        dx_ref[...] = dxv

        @pl.when(pl.program_id(0) == 0)
        def _():
            l_ref[...] = jnp.zeros_like(l_ref)
            dg_ref[...] = jnp.zeros_like(dg_ref)

        l_ref[...] += jnp.broadcast_to(lv, l_ref.shape)
        dg_ref[...] += dgv

    row = pl.BlockSpec((tm, d), lambda i: (i, 0))
    vec = pl.BlockSpec((1, d), lambda i: (0, 0))
    return _pcall(body, name=name, out_shape=(_sds((8, 128), F32), _sds((s, d), F32), _sds((1, d), F32)),
                  grid=(s // tm,), in_specs=[row, vec, row],
                  out_specs=(pl.BlockSpec((8, 128), lambda i: (0, 0)), row, vec), semantics=("arbitrary",),
                  block_bytes=8 * _nbytes((tm, d), F32))(x, g, target)


def _acc_out(ref, val, first):
    @pl.when(first)
    def _():
        ref[...] = jnp.zeros_like(ref)

    ref[...] += val


def _gmlp_fwd(z, ln_g, ln_b, wcat, bfull, *, name):
    s = z.shape[0]
    t = _pick(s, (512, 256, 128))
    nch = t // GMLP_CHUNK

    def body(zu_ref, zv_ref, g_ref, b_ref, w_ref, bf_ref, o_ref):
        for c in range(nch):
            rows = pl.ds(c * GMLP_CHUNK, GMLP_CHUNK)
            o_ref[rows, :] = _gmlp_chunk(zu_ref[rows, :], zv_ref[rows, :], g_ref[...], b_ref[...], w_ref[...],
                                         bf_ref[...]).astype(BF16)

    col = lambda c: pl.BlockSpec((t, W_GRP), lambda i: (i, c))
    params = (ln_g, ln_b, wcat, bfull)
    return _pcall(body, name=name, out_shape=_sds((s, D_MODEL), BF16), grid=(s // t,),
                  in_specs=[col(0), col(1)] + [_spec(a) for a in params],
                  out_specs=pl.BlockSpec((t, W_GRP), lambda i: (i, 0)), semantics=("parallel",),
                  block_bytes=4 * _nbytes((t, W_GRP), F32))(z, z, *[_arr(a) for a in params])


def _gmlp_bwd(z, dmix, ln_g, ln_b, wcat, bfull, *, name):
    s = z.shape[0]
    t = _pick(s, (512, 256, 128))
    nch = t // GMLP_CHUNK

    def body(zu_ref, zv_ref, dy_ref, g_ref, b_ref, w_ref, bf_ref, dz_ref, dg_ref, db_ref, dw_ref, dbf_ref):
        acc = None
        for c in range(nch):
            rows = pl.ds(c * GMLP_CHUNK, GMLP_CHUNK)
            _, vjp = jax.vjp(_gmlp_chunk, zu_ref[rows, :], zv_ref[rows, :], g_ref[...], b_ref[...], w_ref[...],
                             bf_ref[...])
            du, dv, *dps = vjp(dy_ref[rows, :])
            dz_ref[rows, :] = jnp.concatenate([du, dv], axis=1).astype(BF16)
            acc = dps if acc is None else [x + y for x, y in zip(acc, dps)]
        first = pl.program_id(0) == 0
        for ref, val in zip((dg_ref, db_ref, dw_ref, dbf_ref), acc):
            _acc_out(ref, val, first)

    col = lambda c: pl.BlockSpec((t, W_GRP), lambda i: (i, c))
    params = (ln_g, ln_b, wcat, bfull)
    return _pcall(body, name=name,
                  out_shape=(_sds((s, D_PROJ), BF16),) + tuple(_sds(a.shape, F32) for a in params),
                  grid=(s // t,), in_specs=[col(0), col(1), col(0)] + [_spec(a) for a in params],
                  out_specs=(pl.BlockSpec((t, 2 * W_GRP), lambda i: (i, 0)),) + tuple(_ospec(a) for a in params),
                  semantics=("arbitrary",),
                  block_bytes=8 * _nbytes((t, W_GRP), F32))(z, z, dmix, *[_arr(a) for a in params])


def _rglru_fwd(z, prm, mix, *, name):
    s = z.shape[0]
    t = _pick(s, (512, 256, 128))
    nt = s // t

    def body(xb_ref, halo_ref, gb_ref, *rest):
        prm_refs, (y_ref, h0s_ref, h_scr) = rest[:len(prm)], rest[len(prm) + 1:]
        i = pl.program_id(0)

        @pl.when(i == 0)
        def _():
            h_scr[...] = jnp.zeros_like(h_scr)

        halo = jnp.where(i == 0, 0.0, halo_ref[...])
        h0 = h_scr[...]
        y, h_last = _rglru_tile(jnp.concatenate([halo, xb_ref[...]], axis=0), gb_ref[...], h0,
                                *[r[...] for r in prm_refs])
        y_ref[...] = y.astype(BF16)
        h0s_ref[...] = jnp.broadcast_to(h0, h0s_ref.shape)
        h_scr[...] = h_last

    in_specs = [pl.BlockSpec((t, W_GRP), lambda i: (i, 2)),
                pl.BlockSpec((8, W_GRP), lambda i: (jnp.maximum(i * (t // 8) - 1, 0), 2)),
                pl.BlockSpec((t, W_GRP), lambda i: (i, 3))] + [_spec(a) for a in prm] + [HBM_SPEC]
    return _pcall(body, name=name, out_shape=(_sds(mix.shape, BF16), _sds((nt, 8, W_GRP), F32)), grid=(nt,),
                  in_specs=in_specs,
                  out_specs=(pl.BlockSpec((t, W_GRP), lambda i: (i, 1)), pl.BlockSpec((None, 8, W_GRP), lambda i: (i, 0, 0))),
                  scratch_shapes=[pltpu.VMEM((1, W_GRP), F32)], semantics=("arbitrary",),
                  block_bytes=24 * _nbytes((t, W_GRP), F32), aliases={3 + len(prm): 0})(
                      z, z, z, *[_arr(a) for a in prm], mix)


def _rglru_bwd(z, dmix, h0s, prm, dz, *, name):
    s = z.shape[0]
    t = _pick(s, (512, 256, 128))
    nt = s // t
    npm = len(prm)

    def body(xb_ref, halo_ref, gb_ref, dy_ref, h0s_ref, *rest):
        prm_refs = rest[:npm]
        dz_ref = rest[npm + 1]
        dprm_refs = rest[npm + 2:2 * npm + 2]
        dh_scr, dhalo_scr = rest[2 * npm + 2:]
        i = pl.program_id(0)
        r = nt - 1 - i

        @pl.when(i == 0)
        def _():
            dh_scr[...] = jnp.zeros_like(dh_scr)
            dhalo_scr[...] = jnp.zeros_like(dhalo_scr)

        halo = jnp.where(r == 0, 0.0, halo_ref[...])
        h0 = h0s_ref[0:1, :]
        _, vjp = jax.vjp(_rglru_tile, jnp.concatenate([halo, xb_ref[...]], axis=0), gb_ref[...], h0,
                         *[p[...] for p in prm_refs])
        dext, dgb, _dh0, *dps = vjp((dy_ref[...], dh_scr[...]))
        dmain = dext[8:]
        dxb = jnp.concatenate([dmain[:t - 8], dmain[t - 8:] + dhalo_scr[...]], axis=0)
        dz_ref[...] = jnp.concatenate([dxb, dgb], axis=1).astype(BF16)
        dh_scr[...] = _dh0
        dhalo_scr[...] = dext[:8]
        for ref, val in zip(dprm_refs, dps):
            _acc_out(ref, val, i == 0)

    rev = lambda c: pl.BlockSpec((t, W_GRP), lambda i: (nt - 1 - i, c))
    in_specs = [rev(2), pl.BlockSpec((8, W_GRP), lambda i: (jnp.maximum((nt - 1 - i) * (t // 8) - 1, 0), 2)), rev(3),
                rev(1), pl.BlockSpec((None, 8, W_GRP), lambda i: (nt - 1 - i, 0, 0))] + [_spec(a) for a in prm] + [HBM_SPEC]
    return _pcall(body, name=name,
                  out_shape=(_sds(dz.shape, BF16),) + tuple(_sds(a.shape, F32) for a in prm),
                  grid=(nt,), in_specs=in_specs,
                  out_specs=(pl.BlockSpec((t, 2 * W_GRP), lambda i: (nt - 1 - i, 1)),) + tuple(_ospec(a) for a in prm),
                  scratch_shapes=[pltpu.VMEM((1, W_GRP), F32), pltpu.VMEM((8, W_GRP), F32)],
                  semantics=("arbitrary",), block_bytes=40 * _nbytes((t, W_GRP), F32), aliases={5 + npm: 0})(
                      z, z, z, dmix, h0s, *[_arr(a) for a in prm], dz)


def _pool_inv(i, t):
    pos = (_rows_of((t, W_GRP)) + i * t + 1).astype(F32)
    grp = _lanes_of((t, W_GRP)) // HEAD_DIM
    win = jnp.where(grp == 0, float(POOL_WINDOWS[0]), jnp.where(grp == 1, float(POOL_WINDOWS[1]),
                    jnp.where(grp == 2, float(POOL_WINDOWS[2]), float(POOL_WINDOWS[3]))))
    return 1.0 / jnp.minimum(pos, win)


def _pool_fwd(z, wd, scale, mix, *, name):
    s = z.shape[0]
    t = _pick(s, (512, 256, 128))

    def body(x_ref, halo_ref, wd_ref, sc_ref, _, y_ref):
        i = pl.program_id(0)
        halo = jnp.where(i == 0, 0.0, halo_ref[...])
        y = _pool_tile(jnp.concatenate([halo, x_ref[...]], axis=0), _pool_inv(i, t), wd_ref[...], sc_ref[...])
        y_ref[...] = y.astype(BF16)

    in_specs = [pl.BlockSpec((t, W_GRP), lambda i: (i, 8)),
                pl.BlockSpec((16, W_GRP), lambda i: (jnp.maximum(i * (t // 16) - 1, 0), 8)), _spec(wd), _spec(scale),
                HBM_SPEC]
    return _pcall(body, name=name, out_shape=_sds(mix.shape, BF16), grid=(s // t,), in_specs=in_specs,
                  out_specs=pl.BlockSpec((t, W_GRP), lambda i: (i, 3)), semantics=("parallel",),
                  block_bytes=12 * _nbytes((t, W_GRP), F32), aliases={4: 0})(z, z, _arr(wd), _arr(scale), mix)


def _pool_bwd(z, dmix, wd, scale, dz, *, name):
    s = z.shape[0]
    t = _pick(s, (512, 256, 128))
    nt = s // t

    def body(x_ref, halo_ref, dy_ref, wd_ref, sc_ref, _, dx_ref, dwd_ref, dsc_ref, dhalo_scr):
        i = pl.program_id(0)
        r = nt - 1 - i

        @pl.when(i == 0)
        def _():
            dhalo_scr[...] = jnp.zeros_like(dhalo_scr)

        halo = jnp.where(r == 0, 0.0, halo_ref[...])
        inv = _pool_inv(r, t)
        _, vjp = jax.vjp(lambda e, w, sc: _pool_tile(e, inv, w, sc), jnp.concatenate([halo, x_ref[...]], axis=0),
                         wd_ref[...], sc_ref[...])
        dext, dwd, dsc = vjp(dy_ref[...])
        dmain = dext[16:]
        dx = jnp.concatenate([dmain[:t - 16], dmain[t - 16:] + dhalo_scr[...]], axis=0)
        dx_ref[...] = dx.astype(BF16)
        dhalo_scr[...] = dext[:16]
        _acc_out(dwd_ref, dwd, i == 0)
        _acc_out(dsc_ref, dsc, i == 0)

    rev = lambda c: pl.BlockSpec((t, W_GRP), lambda i: (nt - 1 - i, c))
    in_specs = [rev(8), pl.BlockSpec((16, W_GRP), lambda i: (jnp.maximum((nt - 1 - i) * (t // 16) - 1, 0), 8)), rev(3),
                _spec(wd), _spec(scale), HBM_SPEC]
    return _pcall(body, name=name, out_shape=(_sds(dz.shape, BF16), _sds(wd.shape, F32), _sds(scale.shape, F32)),
                  grid=(nt,), in_specs=in_specs, out_specs=(rev(8), _ospec(wd), _ospec(scale)),
                  scratch_shapes=[pltpu.VMEM((16, W_GRP), F32)], semantics=("arbitrary",),
                  block_bytes=20 * _nbytes((t, W_GRP), F32), aliases={5: 0})(z, z, dmix, _arr(wd), _arr(scale), dz)


def _hgrn_fwd(z, lb, ngf, mix, *, name):
    s = z.shape[0]
    c = HGRN_CHUNK
    per = HGRN_STEP_CHUNKS
    ns = s // (c * per)

    def body(q_ref, f_ref, i_ref, g_ref, lb_ref, ng_ref, _, y_ref, sts_ref, st_scr):
        @pl.when(pl.program_id(0) == 0)
        def _():
            st_scr[...] = jnp.zeros_like(st_scr)

        st = st_scr[...]
        for k in range(per):
            rows = pl.ds(k * c, c)
            sts_ref[k] = st
            y, st = _hgrn_chunk(q_ref[rows, :], f_ref[rows, :], i_ref[rows, :], g_ref[rows, :], st, lb_ref[...],
                                ng_ref[...])
            y_ref[rows, :] = y.astype(BF16)
        st_scr[...] = st

    col = lambda k: pl.BlockSpec((per * c, W_GRP), lambda i: (i, k))
    return _pcall(body, name=name, out_shape=(_sds(mix.shape, BF16), _sds((ns * per, W_GRP, W_GRP), F32)), grid=(ns,),
                  in_specs=[col(4), col(5), col(6), col(7), _spec(lb), _spec(ngf), HBM_SPEC],
                  out_specs=(pl.BlockSpec((per * c, W_GRP), lambda i: (i, 2)),
                             pl.BlockSpec((per, W_GRP, W_GRP), lambda i: (i, 0, 0))),
                  scratch_shapes=[pltpu.VMEM((W_GRP, W_GRP), F32)], semantics=("arbitrary",),
                  block_bytes=16 * per * _nbytes((W_GRP, W_GRP), F32), aliases={6: 0})(
                      z, z, z, z, _arr(lb), _arr(ngf), mix)


def _hgrn_bwd(z, dmix, sts, lb, ngf, dz, *, name):
    s = z.shape[0]
    c = HGRN_CHUNK
    per = HGRN_STEP_CHUNKS
    ns = s // (c * per)

    def body(q_ref, f_ref, i_ref, g_ref, dy_ref, st_ref, lb_ref, ng_ref, _, dz_ref, dlb_ref, dng_ref, dst_scr):
        i = pl.program_id(0)

        @pl.when(i == 0)
        def _():
            dst_scr[...] = jnp.zeros_like(dst_scr)

        dst = dst_scr[...]
        dlb_sum = dng_sum = None
        for k in range(per - 1, -1, -1):
            rows = pl.ds(k * c, c)
            _, vjp = jax.vjp(_hgrn_chunk, q_ref[rows, :], f_ref[rows, :], i_ref[rows, :], g_ref[rows, :], st_ref[k],
                             lb_ref[...], ng_ref[...])
            dq, df, di, dg, dst, dlb, dng = vjp((dy_ref[rows, :], dst))
            dz_ref[rows, :] = jnp.concatenate([dq, df, di, dg], axis=1).astype(BF16)
            dlb_sum = dlb if dlb_sum is None else dlb_sum + dlb
            dng_sum = dng if dng_sum is None else dng_sum + dng
        dst_scr[...] = dst
        _acc_out(dlb_ref, dlb_sum, i == 0)
        _acc_out(dng_ref, dng_sum, i == 0)

    rev = lambda k: pl.BlockSpec((per * c, W_GRP), lambda i: (ns - 1 - i, k))
    vec = pl.BlockSpec((1, W_GRP), lambda i: (0, 0))
    return _pcall(body, name=name, out_shape=(_sds(dz.shape, BF16), _sds((1, W_GRP), F32), _sds((1, W_GRP), F32)),
                  grid=(ns,),
                  in_specs=[rev(4), rev(5), rev(6), rev(7), rev(2),
                            pl.BlockSpec((per, W_GRP, W_GRP), lambda i: (ns - 1 - i, 0, 0)), _spec(lb), _spec(ngf),
                            HBM_SPEC],
                  out_specs=(pl.BlockSpec((per * c, 4 * W_GRP), lambda i: (ns - 1 - i, 1)), vec, vec),
                  scratch_shapes=[pltpu.VMEM((W_GRP, W_GRP), F32)], semantics=("arbitrary",),
                  block_bytes=32 * per * _nbytes((W_GRP, W_GRP), F32), aliases={8: 0})(
                      z, z, z, z, dmix, sts, _arr(lb), _arr(ngf), dz)


def _lbs_fwd(c_lb, *, name):
    def body(c_ref, o_ref):
        c = c_ref[...]
        e = jnp.exp(c - jnp.max(c, axis=0, keepdims=True))
        sm = e / jnp.sum(e, axis=0, keepdims=True)
        run = jnp.zeros((1, W_GRP), F32)
        o_ref[0:1, :] = run
        for l in range(1, DEPTH):
            run = run + sm[l:l + 1]
            o_ref[l:l + 1, :] = run

    return _pcall(body, name=name, out_shape=_sds((DEPTH, W_GRP), F32))(c_lb)


def _lbs_bwd(c_lb, dlbs, *, name):
    def body(c_ref, d_ref, o_ref):
        c = c_ref[...]
        e = jnp.exp(c - jnp.max(c, axis=0, keepdims=True))
        sm = e / jnp.sum(e, axis=0, keepdims=True)
        d = d_ref[...]
        dsm = [None] * DEPTH
        run = jnp.zeros((1, W_GRP), F32)
        for l in range(DEPTH - 1, 0, -1):
            run = run + d[l:l + 1]
            dsm[l] = run
        dsm[0] = jnp.zeros((1, W_GRP), F32)
        inner = sum(sm[l:l + 1] * dsm[l] for l in range(DEPTH))
        for l in range(DEPTH):
            o_ref[l:l + 1, :] = sm[l:l + 1] * (dsm[l] - inner)

    return _pcall(body, name=name, out_shape=_sds((DEPTH, W_GRP), F32))(c_lb, dlbs)


def _ffn_fwd(hg, hv, cwf, cbf, *, name):
    s, n = hg.shape
    t = _pick(s, (256, 128))
    cw = _pick(n, (1408, 256, 128))
    nj = n // cw

    def body(g_ref, gh_ref, v_ref, vh_ref, wg_ref, bg_ref, wv_ref, bv_ref, o_ref):
        first = pl.program_id(1) == 0
        eg = jnp.concatenate([jnp.where(first, 0.0, gh_ref[...]), g_ref[...]], axis=0)
        ev = jnp.concatenate([jnp.where(first, 0.0, vh_ref[...]), v_ref[...]], axis=0)
        o_ref[...] = _ffn_tile(eg, ev, wg_ref[...], bg_ref[...], wv_ref[...], bv_ref[...]).astype(BF16)

    main = pl.BlockSpec((t, cw), lambda j, i: (i, j))
    halo = pl.BlockSpec((8, cw), lambda j, i: (jnp.maximum(i * (t // 8) - 1, 0), j))
    taps = lambda off: _spec(cwf, (3, cw), lambda j, i: (0, j + off))
    bias = lambda off: _spec(cbf, (1, cw), lambda j, i: (0, j + off))
    return _pcall(body, name=name, out_shape=_sds((s, n), BF16), grid=(nj, s // t),
                  in_specs=[main, halo, main, halo, taps(0), bias(0), taps(nj), bias(nj)], out_specs=main,
                  semantics=("parallel", "parallel"), block_bytes=12 * _nbytes((t, cw), F32))(
                      hg, hg, hv, hv, _arr(cwf), _arr(cbf), _arr(cwf), _arr(cbf))


def _ffn_bwd(hg, hv, da, cwf, cbf, *, name):
    s, n = hg.shape
    t = _pick(s, (256, 128))
    cw = _pick(n, (1408, 256, 128))
    nt = s // t
    nj = n // cw

    def body(g_ref, gh_ref, v_ref, vh_ref, da_ref, wg_ref, bg_ref, wv_ref, bv_ref, dg_ref, dv_ref, dwg_ref, dwv_ref,
             cg_scr, cv_scr):
        i = pl.program_id(1)
        r = nt - 1 - i

        @pl.when(i == 0)
        def _():
            cg_scr[...] = jnp.zeros_like(cg_scr)
            cv_scr[...] = jnp.zeros_like(cv_scr)

        eg = jnp.concatenate([jnp.where(r == 0, 0.0, gh_ref[...]), g_ref[...]], axis=0)
        ev = jnp.concatenate([jnp.where(r == 0, 0.0, vh_ref[...]), v_ref[...]], axis=0)
        _, vjp = jax.vjp(_ffn_tile, eg, ev, wg_ref[...], bg_ref[...], wv_ref[...], bv_ref[...])
        deg, dev, dwg, dbg, dwv, dbv = vjp(da_ref[...])
        for dext, scr, ref in ((deg, cg_scr, dg_ref), (dev, cv_scr, dv_ref)):
            dmain = dext[8:]
            ref[...] = jnp.concatenate([dmain[:t - 8], dmain[t - 8:] + scr[...]], axis=0).astype(BF16)
            scr[...] = dext[:8]
        zeros = jnp.zeros((4, cw), F32)
        _acc_out(dwg_ref, jnp.concatenate([dwg, dbg, zeros], axis=0), i == 0)
        _acc_out(dwv_ref, jnp.concatenate([dwv, dbv, zeros], axis=0), i == 0)

    main = pl.BlockSpec((t, cw), lambda j, i: (nt - 1 - i, j))
    halo = pl.BlockSpec((8, cw), lambda j, i: (jnp.maximum((nt - 1 - i) * (t // 8) - 1, 0), j))
    taps = lambda off: _spec(cwf, (3, cw), lambda j, i: (0, j + off))
    bias = lambda off: _spec(cbf, (1, cw), lambda j, i: (0, j + off))
    w8 = pl.BlockSpec((8, cw), lambda j, i: (0, j))
    return _pcall(body, name=name,
                  out_shape=(_sds((s, n), BF16), _sds((s, n), BF16), _sds((8, n), F32), _sds((8, n), F32)),
                  grid=(nj, nt), in_specs=[main, halo, main, halo, main, taps(0), bias(0), taps(nj), bias(nj)],
                  out_specs=(main, main, w8, w8),
                  scratch_shapes=[pltpu.VMEM((8, cw), F32), pltpu.VMEM((8, cw), F32)],
                  semantics=("parallel", "arbitrary"), block_bytes=24 * _nbytes((t, cw), F32))(
                      hg, hg, hv, hv, da, _arr(cwf), _arr(cbf), _arr(cwf), _arr(cbf))


def _all_gather(x, *, name):
    r, c = x.shape

    def body(x_ref, out_ref, send_sems, recv_sems, local_sem):
        mx, my, mc = lax.axis_index("x"), lax.axis_index("y"), lax.axis_index("c")
        me, sibling = (mx, my, mc), (mx, my, 1 - mc)
        chips = [(1 - mx, my), (mx, 1 - my), (1 - mx, 1 - my)]

        def slot(px, py, pc):
            return out_ref.at[4 * px + 2 * py + pc]

        def copy(k, block, to, src=None):
            return pltpu.make_async_remote_copy(src_ref=slot(*block) if src is None else src, dst_ref=slot(*block),
                                                send_sem=send_sems.at[k], recv_sem=recv_sems.at[k],
                                                device_id=to, device_id_type=MESH)

        mine = pltpu.make_async_copy(x_ref, slot(*me), local_sem)
        mine.start()
        first = [copy(0, me, sibling, src=x_ref)]
        first += [copy(1 + j, me, (*chip, mc), src=x_ref) for j, chip in enumerate(chips)]
        for cp in first:
            cp.start()
        passed = [copy(4 + j, (*chip, mc), sibling) for j, chip in enumerate(chips)]
        for j, chip in enumerate(chips):
            copy(1 + j, (*chip, mc), me).wait_recv()
            passed[j].start()
        copy(0, sibling, me).wait_recv()
        for j, chip in enumerate(chips):
            copy(4 + j, (*chip, 1 - mc), me).wait_recv()
        for cp in first + passed:
            cp.wait_send()
        mine.wait()

    hbm = pl.BlockSpec(memory_space=pl.ANY)
    return _pcall(body, name=name, out_shape=_sds((N_DEV, r, c), x.dtype), in_specs=[hbm], out_specs=hbm,
                  scratch_shapes=[pltpu.SemaphoreType.DMA((7,)), pltpu.SemaphoreType.DMA((7,)),
                                  pltpu.SemaphoreType.DMA(())])(x)


def _sum_slots(p, *, name):
    q, r, c = p.shape
    tr = _pick(r, (544, 408, 272, 192, 136, 64, 32, 16, 8))

    def body(p_ref, o_ref):
        acc = p_ref[0].astype(F32)
        for k in range(1, q):
            acc = acc + p_ref[k].astype(F32)
        o_ref[...] = acc

    return _pcall(body, name=name, out_shape=_sds((r, c), F32), grid=(r // tr,),
                  in_specs=[pl.BlockSpec((q, tr, c), lambda i: (0, i, 0))],
                  out_specs=pl.BlockSpec((tr, c), lambda i: (i, 0)), semantics=("parallel",),
                  block_bytes=(q + 2) * _nbytes((tr, c), F32))(p)


BIG_COMM = (('w_in', 288, D_MODEL), ('w_out', 128, D_MODEL), ('w_up', 704, D_MODEL), ('w_down', 352, D_MODEL),
            ('w_pe', 128, PLE_DIM), ('w_pg', 128, D_MODEL))
HBM_SPEC = pl.BlockSpec(memory_space=pl.ANY)


def _gather_layer(shards, l, *, name):
    na = len(shards)

    def body(*refs):
        x_refs, out_refs = refs[:na], refs[na:2 * na]
        send_sems, recv_sems, local_sems = refs[2 * na:]
        mx, my, mc = lax.axis_index("x"), lax.axis_index("y"), lax.axis_index("c")
        me, sibling = (mx, my, mc), (mx, my, 1 - mc)
        chips = [(1 - mx, my), (mx, 1 - my), (1 - mx, 1 - my)]

        def slot(a, px, py, pc):
            return out_refs[a].at[4 * px + 2 * py + pc]

        def copy(k, a, block, to, own=False):
            return pltpu.make_async_remote_copy(src_ref=x_refs[a].at[l] if own else slot(a, *block),
                                                dst_ref=slot(a, *block), send_sem=send_sems.at[k, a],
                                                recv_sem=recv_sems.at[k, a], device_id=to, device_id_type=MESH)

        mine = [pltpu.make_async_copy(x_refs[a].at[l], slot(a, *me), local_sems.at[a]) for a in range(na)]
        for cp in mine:
            cp.start()
        first = []
        for a in range(na):
            first.append(copy(0, a, me, sibling, own=True))
            first += [copy(1 + j, a, me, (*chip, mc), own=True) for j, chip in enumerate(chips)]
        for cp in first:
            cp.start()
        passed = []
        for j, chip in enumerate(chips):
            for a in range(na):
                copy(1 + j, a, (*chip, mc), me).wait_recv()
                fwd = copy(4 + j, a, (*chip, mc), sibling)
                fwd.start()
                passed.append(fwd)
        for a in range(na):
            copy(0, a, sibling, me).wait_recv()
        for j, chip in enumerate(chips):
            for a in range(na):
                copy(4 + j, a, (*chip, 1 - mc), me).wait_recv()
        for cp in first + passed:
            cp.wait_send()
        for cp in mine:
            cp.wait()

    return _pcall(body, name=name, out_shape=tuple(_sds((N_DEV,) + x.shape[1:], x.dtype) for x in shards),
                  in_specs=[HBM_SPEC] * na, out_specs=(HBM_SPEC,) * na,
                  scratch_shapes=[pltpu.SemaphoreType.DMA((7, na)), pltpu.SemaphoreType.DMA((7, na)),
                                  pltpu.SemaphoreType.DMA((na,))])(*shards)


SEM_SPEC = pl.BlockSpec(memory_space=pltpu.SEMAPHORE)
DATAFLOW_EFFECT = pltpu.SideEffectType.DATAFLOW_SIDE_EFFECTING


def _place_own(srcs, after, *, name):
    na = len(srcs)

    def body(*refs):
        x_refs, land_refs, sems = refs[:na], refs[na + len(after):2 * na + len(after)], refs[-1]
        me = 4 * lax.axis_index("x") + 2 * lax.axis_index("y") + lax.axis_index("c")
        cps = [pltpu.make_async_copy(x_refs[a], land_refs[a].at[me], sems.at[a]) for a in range(na)]
        for cp in cps:
            cp.start()
        for cp in cps:
            cp.wait()

    return _pcall(body, name=name, out_shape=tuple(_sds((N_DEV,) + x.shape, x.dtype) for x in srcs),
                  in_specs=[HBM_SPEC] * (na + len(after)), out_specs=(HBM_SPEC,) * na,
                  scratch_shapes=[pltpu.SemaphoreType.DMA((na,))])(*srcs, *after)


def _exchange_start(srcs, lands, *, name, per_peer=False):
    na = len(srcs)

    def body(*refs):
        x_refs, land_refs = refs[:na], refs[na:2 * na]
        send_sems, recv_sems = refs[2 * na], refs[2 * na + 1]
        token = refs[-1]
        mx, my, mc = lax.axis_index("x"), lax.axis_index("y"), lax.axis_index("c")
        me = 4 * mx + 2 * my + mc
        peers = [(mx, my, 1 - mc)]
        for px, py in ((1 - mx, my), (mx, 1 - my), (1 - mx, 1 - my)):
            peers += [(px, py, mc), (px, py, 1 - mc)]
        for a in range(na):
            for peer in peers:
                src = x_refs[a].at[4 * peer[0] + 2 * peer[1] + peer[2]] if per_peer else x_refs[a]
                pltpu.make_async_remote_copy(src_ref=src, dst_ref=land_refs[a].at[me], send_sem=send_sems.at[a],
                                             recv_sem=recv_sems.at[a], device_id=peer, device_id_type=MESH).start()
        token[...] = jnp.zeros_like(token)

    hbm = lambda x: pltpu.HBM(x.shape, x.dtype)
    out_shape = ((pltpu.SemaphoreType.DMA((na,)), pltpu.SemaphoreType.DMA((na,))) + tuple(hbm(x) for x in srcs)
                 + tuple(hbm(x) for x in lands) + (_sds((8, 128), F32),))
    params = pltpu.CompilerParams(has_side_effects=DATAFLOW_EFFECT)
    pin = lambda x: pltpu.with_memory_space_constraint(x, pltpu.HBM)
    return pl.pallas_call(body, name=name, out_shape=out_shape, in_specs=[HBM_SPEC] * (2 * na),
                          out_specs=(SEM_SPEC, SEM_SPEC) + (HBM_SPEC,) * (2 * na) + (pl.BlockSpec(memory_space=pltpu.VMEM),),
                          input_output_aliases={i: 2 + i for i in range(2 * na)}, compiler_params=params)(
                              *[pin(x) for x in srcs], *[pin(x) for x in lands])


def _exchange_wait(started, after, *, name):
    send_sems, recv_sems, *bufs, _ = started
    na = len(bufs) // 2

    def body(*refs):
        land_refs = refs[na:2 * na]
        s_sems, r_sems = refs[2 * na], refs[2 * na + 1]
        me = (lax.axis_index("x"), lax.axis_index("y"), lax.axis_index("c"))
        for a in range(na):
            seven = land_refs[a].at[pl.ds(0, N_DEV - 1)]
            cp = pltpu.make_async_remote_copy(src_ref=seven, dst_ref=seven, send_sem=s_sems.at[a], recv_sem=r_sems.at[a],
                                              device_id=me, device_id_type=MESH)
            cp.wait_send()
            cp.wait_recv()

    hbm = lambda x: pltpu.HBM(x.shape, x.dtype)
    params = pltpu.CompilerParams(has_side_effects=DATAFLOW_EFFECT)
    outs = pl.pallas_call(body, name=name, out_shape=tuple(hbm(x) for x in bufs),
                          in_specs=[HBM_SPEC] * (2 * na) + [SEM_SPEC, SEM_SPEC, HBM_SPEC],
                          out_specs=(HBM_SPEC,) * (2 * na), input_output_aliases={i: i for i in range(2 * na)},
                          compiler_params=params)(*bufs, send_sems, recv_sems, after)
    return outs[:na], outs[na:]


def _pair_swap(grads, *, name):
    na = len(grads)

    def body(*refs):
        g_refs, recv_refs = refs[:na], refs[na:2 * na]
        send_sems, recv_sems = refs[2 * na:]
        mx, my, mc = lax.axis_index("x"), lax.axis_index("y"), lax.axis_index("c")
        sibling = (mx, my, 1 - mc)
        for a in range(na):
            for q in range(4):
                pltpu.make_async_remote_copy(src_ref=g_refs[a].at[q, 1 - mc], dst_ref=recv_refs[a].at[q],
                                             send_sem=send_sems.at[a], recv_sem=recv_sems.at[a],
                                             device_id=sibling, device_id_type=MESH).start()
        for a in range(na):
            pltpu.make_async_remote_copy(src_ref=recv_refs[a], dst_ref=recv_refs[a], send_sem=send_sems.at[a],
                                         recv_sem=recv_sems.at[a], device_id=sibling, device_id_type=MESH).wait()

    half = tuple(_sds((4,) + g.shape[2:], g.dtype) for g in grads)
    return _pcall(body, name=name, out_shape=half, in_specs=[HBM_SPEC] * na, out_specs=(HBM_SPEC,) * na,
                  scratch_shapes=[pltpu.SemaphoreType.DMA((na,)), pltpu.SemaphoreType.DMA((na,))])(*grads)


def _add_slabs(grads, recv, core, *, name):
    na = len(grads)

    def body(core_ref, *refs):
        for a in range(na):
            refs[2 * na + a][...] = (refs[a][...].astype(F32) + refs[na + a][...].astype(F32)).astype(BF16)

    own_specs = [pl.BlockSpec((None, None) + x.shape[2:], lambda q, core_ref: (q, core_ref[0], 0, 0)) for x in grads]
    specs = [pl.BlockSpec((None,) + x.shape[1:], lambda q, core_ref: (q, 0, 0)) for x in recv]
    blk = sum(_nbytes(x.shape[1:], F32) for x in recv)
    grid_spec = pltpu.PrefetchScalarGridSpec(num_scalar_prefetch=1, grid=(4,), in_specs=own_specs + specs,
                                             out_specs=tuple(specs))
    params = pltpu.CompilerParams(dimension_semantics=("parallel",), vmem_limit_bytes=_vmem_limit(2 * blk))
    return pl.pallas_call(body, name=name, out_shape=tuple(_sds(x.shape, BF16) for x in recv), grid_spec=grid_spec,
                          compiler_params=params)(core, *grads, *recv)


def _chip_exchange(parts, *, name):
    na = len(parts)

    def body(*refs):
        p_refs, out_refs = refs[:na], refs[na:2 * na]
        send_sems, recv_sems, local_sems = refs[2 * na:]
        mx, my, mc = lax.axis_index("x"), lax.axis_index("y"), lax.axis_index("c")
        mine_q = 2 * mx + my
        chips = [(1 - mx, my), (mx, 1 - my), (1 - mx, 1 - my)]
        owns = [pltpu.make_async_copy(p_refs[a].at[mine_q], out_refs[a].at[mine_q], local_sems.at[a]) for a in range(na)]
        for cp in owns:
            cp.start()
        sends = []
        for a in range(na):
            for k, chip in enumerate(chips):
                sends.append(pltpu.make_async_remote_copy(
                    src_ref=p_refs[a].at[2 * chip[0] + chip[1]], dst_ref=out_refs[a].at[mine_q],
                    send_sem=send_sems.at[k, a], recv_sem=recv_sems.at[k, a], device_id=(*chip, mc), device_id_type=MESH))
        for cp in sends:
            cp.start()
        for a in range(na):
            for k, chip in enumerate(chips):
                pltpu.make_async_remote_copy(
                    src_ref=p_refs[a].at[mine_q], dst_ref=out_refs[a].at[2 * chip[0] + chip[1]],
                    send_sem=send_sems.at[k, a], recv_sem=recv_sems.at[k, a], device_id=(*chip, mc),
                    device_id_type=MESH).wait_recv()
        for cp in sends:
            cp.wait_send()
        for cp in owns:
            cp.wait()

    return _pcall(body, name=name, out_shape=tuple(_sds(x.shape, x.dtype) for x in parts), in_specs=[HBM_SPEC] * na,
                  out_specs=(HBM_SPEC,) * na,
                  scratch_shapes=[pltpu.SemaphoreType.DMA((3, na)), pltpu.SemaphoreType.DMA((3, na)),
                                  pltpu.SemaphoreType.DMA((na,))])(*parts)


def _sum_chips(parts, *, name):
    na = len(parts)

    def body(*refs):
        for a in range(na):
            p_ref = refs[a]
            acc = p_ref[0].astype(F32)
            for k in range(1, p_ref.shape[0]):
                acc = acc + p_ref[k].astype(F32)
            refs[na + a][...] = acc

    half = lambda x: x.shape[1] // 2
    in_specs = [pl.BlockSpec((x.shape[0], half(x), x.shape[2]), lambda i: (0, i, 0)) for x in parts]
    out_specs = tuple(pl.BlockSpec((half(x), x.shape[2]), lambda i: (i, 0)) for x in parts)
    blk = sum(_nbytes((x.shape[0] + 2, half(x), x.shape[2]), BF16) for x in parts)
    return _pcall(body, name=name, out_shape=tuple(_sds(x.shape[1:], F32) for x in parts), grid=(2,),
                  in_specs=in_specs, out_specs=out_specs, semantics=("parallel",), block_bytes=blk)(*parts)


def _sum_devices(lands, own, me, *, name):
    na = len(lands)

    def body(me_ref, *refs):
        mine = me_ref[0]
        for a in range(na):
            l_ref, o_ref = refs[a], refs[na + a]
            acc = None
            for k in range(N_DEV):
                term = jnp.where(mine == k, o_ref[...], l_ref[k]).astype(F32)
                acc = term if acc is None else acc + term
            refs[2 * na + a][...] = acc

    half = lambda x: x.shape[1] // 2
    land_specs = [pl.BlockSpec((N_DEV, half(x), x.shape[2]), lambda i, me_ref: (0, i, 0)) for x in lands]
    own_specs = [pl.BlockSpec((None, half(x), x.shape[2]), lambda i, me_ref: (me_ref[0], i, 0)) for x in lands]
    out_specs = tuple(pl.BlockSpec((half(x), x.shape[2]), lambda i, me_ref: (i, 0)) for x in lands)
    blk = sum(_nbytes((N_DEV + 3, half(x), x.shape[2]), BF16) for x in lands)
    grid_spec = pltpu.PrefetchScalarGridSpec(num_scalar_prefetch=1, grid=(2,), in_specs=land_specs + own_specs,
                                             out_specs=out_specs)
    params = pltpu.CompilerParams(dimension_semantics=("parallel",), vmem_limit_bytes=_vmem_limit(blk))
    return pl.pallas_call(body, name=name, out_shape=tuple(_sds(x.shape[1:], F32) for x in lands), grid_spec=grid_spec,
                          compiler_params=params)(me, *lands, *own)


def _reduce_layer(grads, l):
    n = lambda s: f"l{l}_{s}"
    views = [g.reshape(4, 2, g.shape[0] // N_DEV, g.shape[1]) for g in grads]
    recv = _pair_swap(views, name=n("reduce_pair"))
    core = lax.axis_index("c").astype(jnp.int32).reshape(1)
    chip_sum = _add_slabs(views, recv, core, name=n("reduce_pair_add"))
    from_chips = _chip_exchange(chip_sum, name=n("reduce_chips"))
    return _sum_chips(from_chips, name=n("reduce_chips_add"))


def _adamw(w, g, m, v, *, name):
    lead, (r, c) = w.shape[:-2], w.shape[-2:]
    tr = _pick(r, (512, 256, 192, 128, 64, 32, 16, 8))
    c1 = 1.0 / (1.0 - ADAM_B1 ** ADAM_STEP)
    c2 = 1.0 / (1.0 - ADAM_B2 ** ADAM_STEP)

    def body(w_ref, g_ref, m_ref, v_ref, d_ref, nm_ref, nv_ref):
        gv = g_ref[...]
        nm = ADAM_B1 * m_ref[...] + (1.0 - ADAM_B1) * gv
        nv = ADAM_B2 * v_ref[...] + (1.0 - ADAM_B2) * jnp.square(gv)
        d_ref[...] = -ADAM_LR * ((nm * c1) / (jnp.sqrt(nv * c2) + ADAM_EPS) + ADAM_WD * w_ref[...])
        nm_ref[...] = nm
        nv_ref[...] = nv

    if lead:
        blk = pl.BlockSpec((None, tr, c), lambda k, i: (k, i, 0))
        grid, sem = (lead[0], r // tr), ("parallel", "parallel")
    else:
        blk = pl.BlockSpec((tr, c), lambda i: (i, 0))
        grid, sem = (r // tr,), ("parallel",)
    out = _sds(w.shape, F32)
    return _pcall(body, name=name, out_shape=(out, out, out), grid=grid, in_specs=[blk] * 4,
                  out_specs=(blk, blk, blk), semantics=sem, block_bytes=7 * _nbytes((tr, c), F32))(w, g, m, v)


def _pack_flat(arrs, rows, cols=1024):
    flat = jnp.concatenate([a.reshape(-1).astype(F32) for a in arrs])
    pad = rows * cols - flat.shape[0]
    return jnp.pad(flat, (0, pad)).reshape(rows, cols)


def _unpack_flat(buf, shapes):
    flat = buf.reshape(-1)
    out, off = [], 0
    for shp in shapes:
        n = 1
        for s in shp:
            n *= s
        out.append(flat[off:off + n].reshape(shp))
        off += n
    return out


def _flat_rows(shapes, cols=1024):
    n = sum(functools.reduce(lambda a, b: a * b, shp, 1) for shp in shapes)
    rows = -(-n // cols)
    return -(-rows // 64) * 64


def _block_diag(w):
    eye = jnp.eye(N_HEADS, dtype=w.dtype)
    return (w[:, :, :, None, :] * eye[None, :, None, :, None]).reshape(w.shape[0], W_GRP, W_GRP)


def _diag_blocks(w):
    w5 = w.reshape(w.shape[0], N_HEADS, HEAD_DIM, N_HEADS, HEAD_DIM)
    return jnp.stack([w5[:, h, :, h, :] for h in range(N_HEADS)], axis=1)


def _stacked_params(w, lbs):
    tril = jnp.tril(jnp.ones((GMLP_CHUNK, GMLP_CHUNK), bool))
    row = lambda a: a.reshape(DEPTH, 1, -1)
    return dict(
        g1=row(w['norm1_g']), g2=row(w['norm2_g']), g3=row(w['norm3_g']),
        a_ln_g=row(w['a_ln_g']), a_ln_b=row(w['a_ln_b']),
        a_wcat=jnp.where(tril, w['a_ws'], 0.0).reshape(DEPTH, N_HEADS * GMLP_CHUNK, GMLP_CHUNK),
        a_bfull=jnp.repeat(jnp.swapaxes(w['a_bs'], 1, 2), HEAD_DIM, axis=2),
        b_cw=w['b_conv_w_full'], b_cb=row(w['b_conv_b']), b_wa=_block_diag(w['b_wa']), b_ba=row(w['b_ba']),
        b_wx=_block_diag(w['b_wx']), b_bx=row(w['b_bx']), b_lam=row(w['b_lam']),
        c_lb=row(lbs), c_ngf=row(jnp.tile(w['c_norm_g'], (1, N_HEADS))),
        d_wd=_block_diag(w['d_w']), d_scale=row(w['d_scale']),
        f_cw=w['ffn_conv_w_full'], f_cb=row(w['ffn_conv_b']),
    )


B_PRM = ('b_cw', 'b_cb', 'b_wa', 'b_ba', 'b_wx', 'b_bx', 'b_lam')


def _layer_fwd(x, p_bf, wb, sp, l):
    n = lambda s: f"l{l}_{s}"
    h = _rms_fwd(x, sp['g1'], name=n("norm1"))
    z = _matmul(h, wb['w_in'], nt=True, name=n("proj_in"))
    mix = _gmlp_fwd(z, sp['a_ln_g'], sp['a_ln_b'], sp['a_wcat'], sp['a_bfull'], name=n("gmlp"))
    mix, h0s = _rglru_fwd(z, [sp[k] for k in B_PRM], mix, name=n("rglru"))
    mix, sts = _hgrn_fwd(z, sp['c_lb'], sp['c_ngf'], mix, name=n("hgrn"))
    mix = _pool_fwd(z, sp['d_wd'], sp['d_scale'], mix, name=n("pool"))
    x1 = _matmul(mix, wb['w_out'], res=x, name=n("proj_out"))
    h2 = _rms_fwd(x1, sp['g2'], name=n("norm2"))
    hg = _matmul(h2, wb['w_up_g'], nt=True, name=n("up_gate"))
    hv = _matmul(h2, wb['w_up_v'], nt=True, name=n("up_val"))
    a = _ffn_fwd(hg, hv, sp['f_cw'], sp['f_cb'], name=n("ffn_gate"))
    x2 = _matmul(a, wb['w_down'], res=x1, name=n("down"))
    h3 = _rms_fwd(x2, sp['g3'], name=n("norm3"))
    gl = _matmul(h3, wb['w_pg'], name=n("ple_gate"))
    pe = _matmul(p_bf, wb['w_pe'], nt=True, name=n("ple_emb"))
    x3 = _ple_fwd(x2, gl, pe, name=n("ple"))
    saved = dict(x=x, h=h, z=z, h0s=h0s, sts=sts, mix=mix, x1=x1, h2=h2, hg=hg, hv=hv, a=a, x2=x2, h3=h3, gl=gl, pe=pe)
    return x3, saved


def _layer_bwd(dx3, sv, p_bf, wb, sp, l, mid=None):
    n = lambda s: f"l{l}_{s}_bwd"
    gb, gs = {}, {}
    dpe, dgl = _ple_bwd(dx3, sv['gl'], sv['pe'], name=n("ple"))
    gb['w_pe'] = _matmul_tn(dpe, p_bf, name=n("ple_emb_w"))
    gb['w_pg'] = _matmul_tn(sv['h3'], dgl, name=n("ple_gate_w"))
    dh3 = _matmul(dgl, wb['w_pg'], nt=True, name=n("ple_gate_x"))
    dx2, dx2b, gs['norm3_g'] = _rms_bwd(sv['x2'], sp['g3'], dh3, dx3, name=n("norm3"))
    da = _matmul(dx2b, wb['w_down'], nt=True, name=n("down_x"))
    gb['w_down'] = _matmul_tn(sv['a'], dx2b, name=n("down_w"))
    dhg, dhv, gs['f_dwg'], gs['f_dwv'] = _ffn_bwd(sv['hg'], sv['hv'], da, sp['f_cw'], sp['f_cb'], name=n("ffn_gate"))
    gate_rows = _matmul_tn(dhg, sv['h2'], name=n("up_gate_w"), out_rows=2 * D_FF)
    gb['w_up'] = _matmul_tn(dhv, sv['h2'], name=n("up_val_w"), out_rows=2 * D_FF, row_off=D_FF, into=gate_rows)
    if mid is not None:
        sp = mid(gb, sp)
    dh2 = _matmul(dhg, wb['w_up_g'], name=n("up_gate_x"))
    dh2 = _matmul(dhv, wb['w_up_v'], res=dh2, name=n("up_val_x"))
    dx1, dx1b, gs['norm2_g'] = _rms_bwd(sv['x1'], sp['g2'], dh2, dx2, name=n("norm2"))
    dmix = _matmul(dx1b, wb['w_out'], nt=True, name=n("proj_out_x"))
    gb['w_out'] = _matmul_tn(sv['mix'], dx1b, name=n("proj_out_w"))
    z = sv['z']
    dz, gs['a_ln_g'], gs['a_ln_b'], gs['a_wcat'], gs['a_bfull'] = _gmlp_bwd(
        z, dmix, sp['a_ln_g'], sp['a_ln_b'], sp['a_wcat'], sp['a_bfull'], name=n("gmlp"))
    dz, *dbp = _rglru_bwd(z, dmix, sv['h0s'], [sp[k] for k in B_PRM], dz, name=n("rglru"))
    gs.update(zip(B_PRM, dbp))
    dz, gs['c_lb'], gs['c_ngf'] = _hgrn_bwd(z, dmix, sv['sts'], sp['c_lb'], sp['c_ngf'], dz, name=n("hgrn"))
    dz, gs['d_wd'], gs['d_scale'] = _pool_bwd(z, dmix, sp['d_wd'], sp['d_scale'], dz, name=n("pool"))
    gb['w_in'] = _matmul_tn(dz, sv['h'], name=n("proj_in_w"))
    dh = _matmul(dz, wb['w_in'], name=n("proj_in_x"))
    dx0, _, gs['norm1_g'] = _rms_bwd(sv['x'], sp['g1'], dh, dx1, name=n("norm1"))
    return dx0, gb, gs


SMALL_NAMES = [nm for nm in WEIGHT_NAMES if nm not in BIG_NAMES]
COL_SHARDED = ('w_in', 'w_up', 'w_pe')


def _comm_shards(w):
    return [(jnp.swapaxes(w[nm], 1, 2) if nm in COL_SHARDED else w[nm]).astype(BF16) for nm, _, _ in BIG_COMM]


def _full_weights(gathered):
    out = {nm: g.reshape(N_DEV * r, c) for g, (nm, r, c) in zip(gathered, BIG_COMM)}
    halves = out.pop('w_up').reshape(2, D_FF, D_MODEL)
    out['w_up_g'], out['w_up_v'] = _Sel(halves, 0), _Sel(halves, 1)
    return out


def _small_grads(raw):
    st = {k: jnp.stack([raw[l][k] for l in range(DEPTH)]) for k in raw[0]}
    tril = jnp.tril(jnp.ones((GMLP_CHUNK, GMLP_CHUNK), bool))
    vec = lambda a: a.reshape(DEPTH, -1)
    out = {nm: vec(st[k]) for nm, k in (('norm1_g', 'norm1_g'), ('norm2_g', 'norm2_g'), ('norm3_g', 'norm3_g'),
                                        ('a_ln_g', 'a_ln_g'), ('a_ln_b', 'a_ln_b'), ('b_conv_b', 'b_cb'),
                                        ('b_ba', 'b_ba'), ('b_bx', 'b_bx'), ('b_lam', 'b_lam'), ('c_lb', 'c_lb'),
                                        ('d_scale', 'd_scale'))}
    out['a_ws'] = jnp.where(tril, st['a_wcat'].reshape(DEPTH, N_HEADS, GMLP_CHUNK, GMLP_CHUNK), 0.0)
    out['a_bs'] = jnp.swapaxes(st['a_bfull'].reshape(DEPTH, GMLP_CHUNK, N_HEADS, HEAD_DIM).sum(-1), 1, 2)
    out['b_conv_w'] = st['b_cw']
    out['b_wa'], out['b_wx'], out['d_w'] = _diag_blocks(st['b_wa']), _diag_blocks(st['b_wx']), _diag_blocks(st['d_wd'])
    out['c_norm_g'] = st['c_ngf'].reshape(DEPTH, N_HEADS, HEAD_DIM).sum(1)
    out['ffn_conv_w'] = jnp.concatenate([st['f_dwg'][:, 0:3], st['f_dwv'][:, 0:3]], axis=2)
    out['ffn_conv_b'] = jnp.concatenate([st['f_dwg'][:, 3], st['f_dwv'][:, 3]], axis=1)
    return out


def _step(w, m, v, x, p, target):
    s = x.shape[1]
    dev = 4 * lax.axis_index("x") + 2 * lax.axis_index("y") + lax.axis_index("c")
    xs = x.reshape(s, D_MODEL)

    shards = _comm_shards(w)
    conv_shapes = [w['b_conv_w'].shape, w['ffn_conv_w'].shape]
    conv_rows = _flat_rows(conv_shapes)
    conv_all = _all_gather(_pack_flat([w['b_conv_w'], w['ffn_conv_w']], conv_rows), name="gather_conv_weights")
    parts = [_unpack_flat(conv_all[d], conv_shapes) for d in range(N_DEV)]
    wf = dict(w)
    wf['b_conv_w_full'] = jnp.concatenate([pt[0] for pt in parts], axis=-1)
    wf['ffn_conv_w_full'] = jnp.concatenate([pt[1] for pt in parts], axis=-1)
    lbs = _lbs_fwd(w['c_lb'], name="hgrn_bounds")

    stacked = _stacked_params(wf, lbs)
    p_all = p.reshape(DEPTH, s, PLE_DIM).astype(BF16)
    xl, saved, wbs, sps = xs, [], [], []
    gathered = _gather_layer(shards, 0, name="l0_gather_weights")
    for l in range(DEPTH):
        sp = {k: _Sel(a, l) for k, a in stacked.items()}
        if l + 1 < DEPTH:
            own = [x[l + 1] for x in shards]
            after = [conv_all, *gathered] if l == 0 else [xl]
            lands = _place_own(own, after, name=f"l{l + 1}_gather_place")
            started = _exchange_start(own, lands, name=f"l{l + 1}_gather_start")
            sp['g1'] = stacked['g1'][l] + started[-1][0, 0]
        wb = _full_weights(gathered)
        p_bf = p_all[l]
        xl, sv = _layer_fwd(xl, p_bf, wb, sp, l)
        if l + 1 < DEPTH:
            gathered = _exchange_wait(started, xl, name=f"l{l + 1}_gather_wait")[1]
        saved.append((sv, p_bf))
        wbs.append(wb)
        sps.append(sp)
    loss_part, dx, dfinal = _loss_head(xl, w['final_g'].reshape(1, D_MODEL), target.reshape(s, D_MODEL), name="loss_head")
    loss = lax.psum(loss_part[0, 0], ("x", "y", "c"))

    dev1 = dev.astype(jnp.int32).reshape(1)
    names = [nm for nm, _, _ in BIG_COMM]

    def start_reduce(grads, name):
        views = [g.reshape(N_DEV, g.shape[0] // N_DEV, g.shape[1]) for g in grads]
        return _exchange_start(views, [lax.empty(g.shape, g.dtype) for g in views], name=name, per_peer=True)

    def finish_reduce(started, after, lname):
        own, lands = _exchange_wait(started, after, name=f"{lname}_reduce_wait")
        return _sum_devices(lands, own, dev1, name=f"{lname}_reduce_sum")

    reduced, small = [None] * DEPTH, [None] * DEPTH
    pending = None
    for l in range(DEPTH - 1, 0, -1):
        sv, p_bf = saved[l]
        sp = sps[l]
        if pending is not None:
            sp = dict(sp, g3=stacked['g3'][l] + pending[-1][0, 0])
        dx, gb, small[l] = _layer_bwd(dx, sv, p_bf, wbs[l], sp, l)
        if pending is not None:
            reduced[l + 1] = finish_reduce(pending, dx, f"l{l + 1}")
        pending = start_reduce([gb[nm] for nm in names], f"l{l}_reduce_start")
    early = ('w_up', 'w_down', 'w_pe', 'w_pg')
    mid_started = []

    def mid(gb, sp):
        mid_started.append(start_reduce([gb[nm] for nm in early], "l0_reduce_start"))
        return dict(sp, g2=stacked['g2'][0] + mid_started[0][-1][0, 0])

    sv, p_bf = saved[0]
    dx, gb, small[0] = _layer_bwd(dx, sv, p_bf, wbs[0], dict(sps[0], g3=stacked['g3'][0] + pending[-1][0, 0]), 0, mid=mid)
    reduced[1] = finish_reduce(pending, dx, "l1")
    late = dict(zip(('w_in', 'w_out'), _reduce_layer([gb['w_in'], gb['w_out']], 0)))
    late.update(zip(early, finish_reduce(mid_started[0], late['w_in'], "l0")))
    reduced[0] = [late[nm] for nm in names]
    grad_x = dx.reshape(1, s, D_MODEL)
    gbig = {}
    for a, (nm, _, _) in enumerate(BIG_COMM):
        g = jnp.stack([reduced[l][a] for l in range(DEPTH)])
        gbig[nm] = jnp.swapaxes(g, 1, 2) if nm in COL_SHARDED else g

    small_parts = _small_grads(small)
    small_parts['c_lb'] = _lbs_bwd(w['c_lb'], small_parts['c_lb'], name="hgrn_bounds_bwd")
    small_parts['final_g'] = dfinal.reshape(D_MODEL)
    small_shapes = [small_parts[nm].shape for nm in SMALL_NAMES]
    small_rows = _flat_rows(small_shapes)
    small_all = _all_gather(_pack_flat([small_parts[nm] for nm in SMALL_NAMES], small_rows), name="gather_small_grads")
    gsmall = dict(zip(SMALL_NAMES, _unpack_flat(_sum_slots(small_all, name="sum_small_grads"), small_shapes)))
    for nm in ('b_conv_w', 'ffn_conv_w'):
        width = w[nm].shape[-1]
        gsmall[nm] = lax.dynamic_slice_in_dim(gsmall[nm], dev * width, width, axis=2)

    grads, delta, new_m, new_v = {}, {}, {}, {}
    for nm in BIG_NAMES:
        grads[nm] = gbig[nm]
        delta[nm], new_m[nm], new_v[nm] = _adamw(w[nm], gbig[nm], m[nm], v[nm], name=f"adamw_{nm}")
    shapes = [w[nm].shape for nm in SMALL_NAMES]
    rows = _flat_rows(shapes)
    pk = lambda t: _pack_flat([t[nm] for nm in SMALL_NAMES], rows)
    d, nm_, nv_ = _adamw(pk(w), pk(gsmall), pk(m), pk(v), name="adamw_small")
    for nm, dd, mm_, vv_ in zip(SMALL_NAMES, _unpack_flat(d, shapes), _unpack_flat(nm_, shapes), _unpack_flat(nv_, shapes)):
        grads[nm], delta[nm], new_m[nm], new_v[nm] = gsmall[nm], dd, mm_, vv_

    return (loss, grad_x, *[grads[nm] for nm in WEIGHT_NAMES], *[delta[nm] for nm in WEIGHT_NAMES],
            *[new_m[nm] for nm in WEIGHT_NAMES], *[new_v[nm] for nm in WEIGHT_NAMES])


def kernel(x, p, norm1_g, w_in, a_ln_g, a_ln_b, a_ws, a_bs, b_conv_w, b_conv_b, b_wa, b_ba, b_wx, b_bx, b_lam, c_lb, c_norm_g, d_w, d_scale, w_out, norm2_g, w_up, ffn_conv_w, ffn_conv_b, w_down, norm3_g, w_pe, w_pg, final_g, loss_target, m_norm1_g, m_w_in, m_a_ln_g, m_a_ln_b, m_a_ws, m_a_bs, m_b_conv_w, m_b_conv_b, m_b_wa, m_b_ba, m_b_wx, m_b_bx, m_b_lam, m_c_lb, m_c_norm_g, m_d_w, m_d_scale, m_w_out, m_norm2_g, m_w_up, m_ffn_conv_w, m_ffn_conv_b, m_w_down, m_norm3_g, m_w_pe, m_w_pg, m_final_g, v_norm1_g, v_w_in, v_a_ln_g, v_a_ln_b, v_a_ws, v_a_bs, v_b_conv_w, v_b_conv_b, v_b_wa, v_b_ba, v_b_wx, v_b_bx, v_b_lam, v_c_lb, v_c_norm_g, v_d_w, v_d_scale, v_w_out, v_norm2_g, v_w_up, v_ffn_conv_w, v_ffn_conv_b, v_w_down, v_norm3_g, v_w_pe, v_w_pg, v_final_g):
    w = dict(norm1_g=norm1_g, w_in=w_in, a_ln_g=a_ln_g, a_ln_b=a_ln_b, a_ws=a_ws, a_bs=a_bs, b_conv_w=b_conv_w, b_conv_b=b_conv_b, b_wa=b_wa, b_ba=b_ba, b_wx=b_wx, b_bx=b_bx, b_lam=b_lam, c_lb=c_lb, c_norm_g=c_norm_g, d_w=d_w, d_scale=d_scale, w_out=w_out, norm2_g=norm2_g, w_up=w_up, ffn_conv_w=ffn_conv_w, ffn_conv_b=ffn_conv_b, w_down=w_down, norm3_g=norm3_g, w_pe=w_pe, w_pg=w_pg, final_g=final_g)
    m = dict(norm1_g=m_norm1_g, w_in=m_w_in, a_ln_g=m_a_ln_g, a_ln_b=m_a_ln_b, a_ws=m_a_ws, a_bs=m_a_bs, b_conv_w=m_b_conv_w, b_conv_b=m_b_conv_b, b_wa=m_b_wa, b_ba=m_b_ba, b_wx=m_b_wx, b_bx=m_b_bx, b_lam=m_b_lam, c_lb=m_c_lb, c_norm_g=m_c_norm_g, d_w=m_d_w, d_scale=m_d_scale, w_out=m_w_out, norm2_g=m_norm2_g, w_up=m_w_up, ffn_conv_w=m_ffn_conv_w, ffn_conv_b=m_ffn_conv_b, w_down=m_w_down, norm3_g=m_norm3_g, w_pe=m_w_pe, w_pg=m_w_pg, final_g=m_final_g)
    v = dict(norm1_g=v_norm1_g, w_in=v_w_in, a_ln_g=v_a_ln_g, a_ln_b=v_a_ln_b, a_ws=v_a_ws, a_bs=v_a_bs, b_conv_w=v_b_conv_w, b_conv_b=v_b_conv_b, b_wa=v_b_wa, b_ba=v_b_ba, b_wx=v_b_wx, b_bx=v_b_bx, b_lam=v_b_lam, c_lb=v_c_lb, c_norm_g=v_c_norm_g, d_w=v_d_w, d_scale=v_d_scale, w_out=v_w_out, norm2_g=v_norm2_g, w_up=v_w_up, ffn_conv_w=v_ffn_conv_w, ffn_conv_b=v_ffn_conv_b, w_down=v_w_down, norm3_g=v_norm3_g, w_pe=v_w_pe, w_pg=v_w_pg, final_g=v_final_g)
    return _step(w, m, v, x, p, loss_target)
```

```python
import functools

import jax
import jax.numpy as jnp
from jax import lax
from jax.experimental import pallas as pl
from jax.experimental.pallas import tpu as pltpu

F32 = jnp.float32
BF16 = jnp.bfloat16
MESH = pl.DeviceIdType.MESH

D_MODEL = 1024
DEPTH = 4
PLE_DIM = 256
W_GRP = 256
N_HEADS = 4
HEAD_DIM = 64
GMLP_CHUNK = 128
RGLRU_C = 8.0
HGRN_CHUNK = 64
HGRN_SUB = 16
HGRN_STEP_CHUNKS = 4
POOL_WINDOWS = (2, 4, 8, 16)
D_FF = 2816
D_PROJ = 2304
EPS = 1e-6
ADAM_LR = 0.001
ADAM_B1 = 0.9
ADAM_B2 = 0.999
ADAM_EPS = 1e-08
ADAM_WD = 0.01
ADAM_STEP = 10

N_DEV = 8
MIB = 2 ** 20
V7X_VMEM_BYTES = 64 * MIB
HGRN_EXP_CLAMP = 60.0

WEIGHT_NAMES = ['norm1_g', 'w_in', 'a_ln_g', 'a_ln_b', 'a_ws', 'a_bs', 'b_conv_w', 'b_conv_b', 'b_wa', 'b_ba', 'b_wx',
                'b_bx', 'b_lam', 'c_lb', 'c_norm_g', 'd_w', 'd_scale', 'w_out', 'norm2_g', 'w_up', 'ffn_conv_w',
                'ffn_conv_b', 'w_down', 'norm3_g', 'w_pe', 'w_pg', 'final_g']
BIG_NAMES = ('w_in', 'w_out', 'w_up', 'w_down', 'w_pe', 'w_pg')


def _vmem_limit(block_bytes):
    want = 2 * block_bytes + 24 * MIB
    return int(min(max(want, 32 * MIB), V7X_VMEM_BYTES - 8 * MIB))


def _in_hbm(x):
    return pltpu.with_memory_space_constraint(x, pltpu.HBM)


def _pcall(body, *, name, out_shape, grid=None, in_specs=None, out_specs=None, scratch_shapes=(),
           semantics=None, block_bytes=0, aliases=None):
    kw = {} if aliases is None else {"input_output_aliases": aliases}
    if grid is not None:
        kw["grid"] = grid
    if in_specs is not None:
        kw["in_specs"] = in_specs
    if out_specs is not None:
        kw["out_specs"] = out_specs
    params = pltpu.CompilerParams(dimension_semantics=semantics, vmem_limit_bytes=_vmem_limit(block_bytes))
    call = pl.pallas_call(body, name=name, out_shape=out_shape, scratch_shapes=list(scratch_shapes),
                          compiler_params=params, **kw)
    return lambda *args: call(*[_in_hbm(a) for a in args])


def _pick(n, cands):
    for c in cands:
        if n % c == 0:
            return c
    return n


def _nbytes(shape, dtype):
    n = 1
    for s in shape:
        n *= s
    return n * jnp.dtype(dtype).itemsize


def _sds(shape, dtype):
    return jax.ShapeDtypeStruct(tuple(shape), dtype)


class _Sel:
    def __init__(self, arr, *idx):
        self.arr, self.idx = arr, tuple(idx)
        self.shape = arr.shape[len(idx):]
        self.ndim = len(self.shape)
        self.dtype = arr.dtype


def _arr(a):
    return a.arr if isinstance(a, _Sel) else a


def _spec(a, block=None, index=None):
    block = tuple(a.shape) if block is None else tuple(block)
    index = (lambda *g: (0,) * len(block)) if index is None else index
    if isinstance(a, _Sel):
        lead = a.idx
        return pl.BlockSpec((None,) * len(lead) + block, lambda *g: lead + tuple(index(*g)))
    return pl.BlockSpec(block, lambda *g: tuple(index(*g)))


def _ospec(a):
    return pl.BlockSpec(tuple(a.shape), lambda *g: (0,) * a.ndim)


def _rows_of(shape):
    return lax.broadcasted_iota(jnp.int32, shape, 0)


def _lanes_of(shape):
    return lax.broadcasted_iota(jnp.int32, shape, 1)


def _sdn(x, k, fill):
    n = x.shape[0]
    return jnp.where(_rows_of(x.shape) >= k, pltpu.roll(x, k % n, 0), fill)


def _sup(x, k, fill):
    n = x.shape[0]
    return jnp.where(_rows_of(x.shape) < n - k, pltpu.roll(x, (n - k) % n, 0), fill)


@functools.partial(jax.custom_vjp, nondiff_argnums=(1,))
def _shift_dn(x, k):
    return pltpu.roll(x, k, 0)


def _shift_dn_fwd(x, k):
    return pltpu.roll(x, k, 0), None


def _shift_dn_bwd(k, _, g):
    return (pltpu.roll(g, g.shape[0] - k, 0),)


_shift_dn.defvjp(_shift_dn_fwd, _shift_dn_bwd)


def _lin_scan_impl(a, b, h0):
    n = a.shape[0]
    aa, bb = a, b
    k = 1
    while k < n:
        bb = aa * _sdn(bb, k, 0.0) + bb
        aa = aa * _sdn(aa, k, 1.0)
        k *= 2
    return bb + aa * h0


@jax.custom_vjp
def _lin_scan(a, b, h0):
    return _lin_scan_impl(a, b, h0)


def _lin_scan_fwd(a, b, h0):
    h = _lin_scan_impl(a, b, h0)
    return h, (a, h, h0)


def _lin_scan_bwd(res, g):
    a, h, h0 = res
    n = a.shape[0]
    cc, gg = _sup(a, 1, 0.0), g
    k = 1
    while k < n:
        gg = gg + cc * _sup(gg, k, 0.0)
        cc = cc * _sup(cc, k, 1.0)
        k *= 2
    first = _rows_of(a.shape) == 0
    hprev = jnp.where(first, h0, _sdn(h, 1, 0.0))
    dh0 = jnp.sum(jnp.where(first, a * gg, 0.0), axis=0, keepdims=True)
    return gg * hprev, gg, dh0


_lin_scan.defvjp(_lin_scan_fwd, _lin_scan_bwd)


def _cumsum_sub_impl(x):
    pos = _rows_of(x.shape) % HGRN_SUB
    k = 1
    while k < HGRN_SUB:
        x = x + jnp.where(pos >= k, pltpu.roll(x, k, 0), 0.0)
        k *= 2
    return x


@jax.custom_vjp
def _cumsum_sub(x):
    return _cumsum_sub_impl(x)


def _cumsum_sub_fwd(x):
    return _cumsum_sub_impl(x), None


def _cumsum_sub_bwd(_, g):
    n = g.shape[0]
    pos = _rows_of(g.shape) % HGRN_SUB
    k = 1
    while k < HGRN_SUB:
        g = g + jnp.where(pos < HGRN_SUB - k, pltpu.roll(g, n - k, 0), 0.0)
        k *= 2
    return (g,)


_cumsum_sub.defvjp(_cumsum_sub_fwd, _cumsum_sub_bwd)


def _dot(a, b, ca, cb):
    return lax.dot_general(a.astype(BF16), b.astype(BF16), (((ca,), (cb,)), ((), ())), preferred_element_type=F32)


@jax.custom_vjp
def _mm(a, b):
    return _dot(a, b, 1, 0)


def _mm_fwd(a, b):
    return _dot(a, b, 1, 0), (a, b)


def _mm_bwd(res, g):
    a, b = res
    return _dot(g, b, 1, 1), _dot(a, g, 0, 0)


_mm.defvjp(_mm_fwd, _mm_bwd)


@jax.custom_vjp
def _mm_nt(a, b):
    return _dot(a, b, 1, 1)


def _mm_nt_fwd(a, b):
    return _dot(a, b, 1, 1), (a, b)


def _mm_nt_bwd(res, g):
    a, b = res
    return _dot(g, b, 1, 0), _dot(g, a, 0, 0)


_mm_nt.defvjp(_mm_nt_fwd, _mm_nt_bwd)


@jax.custom_vjp
def _mm_tn(a, b):
    return _dot(a, b, 0, 0)


def _mm_tn_fwd(a, b):
    return _dot(a, b, 0, 0), (a, b)


def _mm_tn_bwd(res, g):
    a, b = res
    return _dot(b, g, 1, 1), _dot(a, g, 1, 0)


_mm_tn.defvjp(_mm_tn_fwd, _mm_tn_bwd)


def _head_mask(shape, h):
    return (_lanes_of(shape) // HEAD_DIM) == h


def _stack_heads(x):
    return jnp.concatenate([jnp.where(_head_mask(x.shape, h), x, 0.0) for h in range(N_HEADS)], axis=0)


def _unstack_heads(p):
    r = p.shape[0] // N_HEADS
    out = None
    for h in range(N_HEADS):
        blk = p[h * r:(h + 1) * r]
        term = jnp.where(_head_mask(blk.shape, h), blk, 0.0)
        out = term if out is None else out + term
    return out


def _segmean_impl(x):
    n = x.shape[1]
    same = (lax.broadcasted_iota(jnp.int32, (n, n), 0) // HEAD_DIM) == (lax.broadcasted_iota(jnp.int32, (n, n), 1) // HEAD_DIM)
    m = jnp.where(same, 1.0 / HEAD_DIM, 0.0).astype(BF16)
    hi = x.astype(BF16)
    lo = (x - hi.astype(F32)).astype(BF16)
    dn = (((1,), (0,)), ((), ()))
    return (lax.dot_general(hi, m, dn, preferred_element_type=F32)
            + lax.dot_general(lo, m, dn, preferred_element_type=F32))


@jax.custom_vjp
def _segmean(x):
    return _segmean_impl(x)


def _segmean_fwd(x):
    return _segmean_impl(x), None


def _segmean_bwd(_, g):
    return (_segmean_impl(g),)


_segmean.defvjp(_segmean_fwd, _segmean_bwd)


def _log1p(u):
    w = 1.0 + u
    return jnp.where(w == 1.0, u, jnp.log(w) * (u / (w - 1.0)))


def _softplus(y):
    return jnp.maximum(y, 0.0) + _log1p(jnp.exp(-jnp.abs(y)))


def _rms(x, g):
    return x * lax.rsqrt(jnp.mean(x * x, axis=-1, keepdims=True) + EPS) * g


def _gmlp_chunk(zu, zv, ln_g, ln_b, wcat, bfull):
    u = jax.nn.gelu(zu)
    v = jax.nn.gelu(zv)
    mu = jnp.mean(v, axis=-1, keepdims=True)
    var = jnp.mean(jnp.square(v - mu), axis=-1, keepdims=True)
    vn = (v - mu) * lax.rsqrt(var + EPS) * ln_g + ln_b
    sv = _unstack_heads(_mm(wcat, vn)) + bfull
    return u * sv


def _rglru_tile(xb_ext, gb, h0, cw, cb, wa, ba, wx, bx, lam):
    xc = (cb + cw[0:1] * _shift_dn(xb_ext, 3) + cw[1:2] * _shift_dn(xb_ext, 2) + cw[2:3] * _shift_dn(xb_ext, 1)
          + cw[3:4] * xb_ext)[8:]
    r = jax.nn.sigmoid(_mm(xc, wa) + ba)
    i = jax.nn.sigmoid(_mm(xc, wx) + bx)
    log_a = (-RGLRU_C) * r * _softplus(-lam)
    a = jnp.exp(log_a)
    mult = jnp.sqrt(-jnp.tanh(log_a) * (a * a + 1.0))
    h = _lin_scan(a, mult * (i * xc), h0)
    y = h * jax.nn.gelu(gb)
    h_last = jnp.sum(jnp.where(_rows_of(h.shape) == h.shape[0] - 1, h, 0.0), axis=0, keepdims=True)
    return y, h_last


def _pool_tile(xd_ext, inv, wd, scale):
    s1 = xd_ext + _shift_dn(xd_ext, 1)
    s2 = s1 + _shift_dn(s1, 2)
    s3 = s2 + _shift_dn(s2, 4)
    s4 = s3 + _shift_dn(s3, 8)
    grp = _lanes_of(xd_ext.shape) // HEAD_DIM
    win = jnp.where(grp == 0, s1, jnp.where(grp == 1, s2, jnp.where(grp == 2, s3, s4)))
    pooled = win[16:] * inv - xd_ext[16:]
    return _mm(pooled, wd) * scale


def _hgrn_chunk(q, f, i, g, st, lb, ngf):
    n = q.shape[0]
    nsub = n // HGRN_SUB
    qs = jax.nn.silu(q)
    fg = lb + (1.0 - lb) * jax.nn.sigmoid(f)
    lf = jnp.log(fg)
    k = 1.0 - fg
    bl = _cumsum_sub(lf)
    row = _rows_of(q.shape)
    blk = row // HGRN_SUB
    betas = [jnp.zeros_like(lb)]
    for s in range(nsub):
        tot = jnp.sum(jnp.where(row == s * HGRN_SUB + HGRN_SUB - 1, bl, 0.0), axis=0, keepdims=True)
        betas.append(betas[-1] + tot)
    b_end = betas[nsub]
    beta_full = jnp.zeros_like(q)
    for s in range(1, nsub):
        beta_full = jnp.where(blk == s, betas[s], beta_full)
    qh = qs * jnp.exp(bl)
    qt = qh * jnp.exp(beta_full)
    b_all = beta_full + bl
    kt = k * jnp.exp(b_end - b_all)
    outs = []
    for s in range(nsub):
        kh = k * jnp.exp(jnp.minimum(betas[s] - b_all, HGRN_EXP_CLAMP))
        qstk = _stack_heads(qh[s * HGRN_SUB:(s + 1) * HGRN_SUB])
        att = _mm_nt(qstk, kh)
        ar = _rows_of(att.shape) % HGRN_SUB + s * HGRN_SUB
        att = jnp.where(_lanes_of(att.shape) <= ar, att, 0.0)
        outs.append(_unstack_heads(_mm(att, i)))
    o = jnp.concatenate(outs, axis=0) + _mm_nt(qt, st)
    same = (_rows_of(st.shape) // HEAD_DIM) == (_lanes_of(st.shape) // HEAD_DIM)
    st_new = st * jnp.exp(b_end) + jnp.where(same, _mm_tn(i, kt), 0.0)
    on = o * lax.rsqrt(_segmean(o * o) + EPS) * ngf
    return on * jax.nn.silu(g), st_new


def _ffn_tile(eg, ev, wg, bg, wv, bv):
    gt = (bg + wg[0:1] * _shift_dn(eg, 2) + wg[1:2] * _shift_dn(eg, 1) + wg[2:3] * eg)[8:]
    val = (bv + wv[0:1] * _shift_dn(ev, 2) + wv[1:2] * _shift_dn(ev, 1) + wv[2:3] * ev)[8:]
    return jax.nn.gelu(gt) * val


MXU_WIDTH = 256
MATMUL_BLOCK_BUDGET = 18 * MIB


def _matmul_tiles(m, k, n, a_dtype, b_dtype, out_dtype, has_res):
    best = None
    for tm in (2048, 1024, 512, 256):
        if m % tm:
            continue
        for tn in (1024, 768, 1408, 512, 256, 128):
            if n % tn:
                continue
            blk = (_nbytes((tm, k), a_dtype) + _nbytes((k, tn), b_dtype) + _nbytes((tm, tn), out_dtype)
                   + (_nbytes((tm, tn), F32) if has_res else 0))
            if blk > MATMUL_BLOCK_BUDGET:
                continue
            waste = -(-tn // MXU_WIDTH) * MXU_WIDTH / tn
            cost = (m // tm) * (n // tn) + 64 * (waste - 1.0)
            if best is None or cost < best[0]:
                best = (cost, tm, tn, blk)
    assert best is not None, (m, k, n)
    return best[1:]


def _matmul(a, b, *, name, nt=False, res=None, out_dtype=F32):
    m, k = a.shape
    n = b.shape[0] if nt else b.shape[1]
    tm, tn, blk = _matmul_tiles(m, k, n, a.dtype, b.dtype, out_dtype, res is not None)
    dims = (((1,), (1,)), ((), ())) if nt else (((1,), (0,)), ((), ()))

    def body(*refs):
        if res is None:
            a_ref, b_ref, o_ref = refs
        else:
            a_ref, b_ref, r_ref, o_ref = refs
        acc = lax.dot_general(a_ref[...], b_ref[...], dims, preferred_element_type=F32)
        if res is not None:
            acc = acc + r_ref[...]
        o_ref[...] = acc.astype(out_dtype)

    in_specs = [pl.BlockSpec((tm, k), lambda i, j: (i, 0)),
                _spec(b, (tn, k), lambda i, j: (j, 0)) if nt else _spec(b, (k, tn), lambda i, j: (0, j))]
    args = [a, _arr(b)]
    if res is not None:
        in_specs.append(pl.BlockSpec((tm, tn), lambda i, j: (i, j)))
        args.append(res)
    return _pcall(body, name=name, out_shape=_sds((m, n), out_dtype), grid=(m // tm, n // tn), in_specs=in_specs,
                  out_specs=pl.BlockSpec((tm, tn), lambda i, j: (i, j)), semantics=("parallel", "parallel"),
                  block_bytes=blk + _nbytes((tm, tn), F32))(*args)


def _matmul_tn(a, b, *, name, out_dtype=BF16, out_rows=None, row_off=0, into=None):
    m, k1 = a.shape
    n = b.shape[1]
    tk = _pick(k1, (512, 256, 128))
    off = row_off // tk
    assert off * tk == row_off

    def body(a_ref, b_ref, *rest):
        rest[-1][...] = lax.dot_general(a_ref[...], b_ref[...], (((0,), (0,)), ((), ())),
                                        preferred_element_type=F32).astype(out_dtype)

    blk = 2 * _nbytes((m, tk), a.dtype) + _nbytes((m, n), b.dtype) + _nbytes((tk, n), F32)
    in_specs = [pl.BlockSpec((m, tk), lambda i: (0, i)), pl.BlockSpec((m, n), lambda i: (0, 0))]
    args = [a, b]
    if into is not None:
        in_specs.append(HBM_SPEC)
        args.append(into)
    return _pcall(body, name=name, out_shape=_sds((out_rows or k1, n), out_dtype), grid=(k1 // tk,), in_specs=in_specs,
                  out_specs=pl.BlockSpec((tk, n), lambda i: (i + off, 0)), semantics=("parallel",), block_bytes=blk,
                  aliases=None if into is None else {2: 0})(*args)


def _rms_fwd(x, g, *, name):
    s, d = x.shape
    tm = _pick(s, (512, 256))

    def body(x_ref, g_ref, o_ref):
        o_ref[...] = _rms(x_ref[...], g_ref[...]).astype(BF16)

    return _pcall(body, name=name, out_shape=_sds((s, d), BF16), grid=(s // tm,),
                  in_specs=[pl.BlockSpec((tm, d), lambda i: (i, 0)), _spec(g)],
                  out_specs=pl.BlockSpec((tm, d), lambda i: (i, 0)), semantics=("parallel",),
                  block_bytes=3 * _nbytes((tm, d), F32))(x, _arr(g))


def _rms_bwd(x, g, dh, dres, *, name):
    s, d = x.shape
    tm = _pick(s, (256, 128))

    def body(x_ref, g_ref, dh_ref, dr_ref, dx_ref, dxb_ref, dg_ref):
        _, vjp = jax.vjp(_rms, x_ref[...], g_ref[...])
        dxn, dg = vjp(dh_ref[...])
        dx = dr_ref[...] + dxn
        dx_ref[...] = dx
        dxb_ref[...] = dx.astype(BF16)

        @pl.when(pl.program_id(0) == 0)
        def _():
            dg_ref[...] = jnp.zeros_like(dg_ref)

        dg_ref[...] += dg

    row = pl.BlockSpec((tm, d), lambda i: (i, 0))
    vec = pl.BlockSpec((1, d), lambda i: (0, 0))
    return _pcall(body, name=name, out_shape=(_sds((s, d), F32), _sds((s, d), BF16), _sds((1, d), F32)),
                  grid=(s // tm,), in_specs=[row, _spec(g), row, row], out_specs=(row, row, vec),
                  semantics=("arbitrary",), block_bytes=8 * _nbytes((tm, d), F32))(x, _arr(g), dh, dres)


def _ple_fwd(x, gl, pe, *, name):
    s, d = x.shape
    tm = _pick(s, (512, 256))

    def body(x_ref, gl_ref, pe_ref, o_ref):
        o_ref[...] = x_ref[...] + pe_ref[...] * jax.nn.sigmoid(gl_ref[...])

    row = pl.BlockSpec((tm, d), lambda i: (i, 0))
    return _pcall(body, name=name, out_shape=_sds((s, d), F32), grid=(s // tm,), in_specs=[row, row, row],
                  out_specs=row, semantics=("parallel",), block_bytes=4 * _nbytes((tm, d), F32))(x, gl, pe)


def _ple_bwd(dx, gl, pe, *, name):
    s, d = dx.shape
    tm = _pick(s, (512, 256))

    def body(dx_ref, gl_ref, pe_ref, dpe_ref, dgl_ref):
        gate = jax.nn.sigmoid(gl_ref[...])
        dxv = dx_ref[...]
        dpe_ref[...] = (dxv * gate).astype(BF16)
        dgl_ref[...] = (dxv * pe_ref[...] * gate * (1.0 - gate)).astype(BF16)

    row = pl.BlockSpec((tm, d), lambda i: (i, 0))
    return _pcall(body, name=name, out_shape=(_sds((s, d), BF16), _sds((s, d), BF16)), grid=(s // tm,),
                  in_specs=[row, row, row], out_specs=(row, row), semantics=("parallel",),
                  block_bytes=5 * _nbytes((tm, d), F32))(dx, gl, pe)


def _loss_head(x, g, target, *, name):
    s, d = x.shape
    tm = _pick(s, (256, 128))

    def tile_loss(xv, gv, tv):
        err = jnp.square(_rms(xv, gv) - tv)
        return 0.5 * jnp.sum(jnp.mean(err, axis=-1, keepdims=True), axis=0, keepdims=True)

    def body(x_ref, g_ref, t_ref, l_ref, dx_ref, dg_ref):
        lv, vjp = jax.vjp(tile_loss, x_ref[...], g_ref[...], t_ref[...])
        dxv, dgv, _ = vjp(jnp.ones((1, 1), F32))
        dx_ref[...] = dxv

        @pl.when(pl.program_id(0) == 0)
        def _():
            l_ref[...] = jnp.zeros_like(l_ref)
            dg_ref[...] = jnp.zeros_like(dg_ref)

        l_ref[...] += jnp.broadcast_to(lv, l_ref.shape)
        dg_ref[...] += dgv

    row = pl.BlockSpec((tm, d), lambda i: (i, 0))
    vec = pl.BlockSpec((1, d), lambda i: (0, 0))
    return _pcall(body, name=name, out_shape=(_sds((8, 128), F32), _sds((s, d), F32), _sds((1, d), F32)),
                  grid=(s // tm,), in_specs=[row, vec, row],
                  out_specs=(pl.BlockSpec((8, 128), lambda i: (0, 0)), row, vec), semantics=("arbitrary",),
                  block_bytes=8 * _nbytes((tm, d), F32))(x, g, target)


def _acc_out(ref, val, first):
    @pl.when(first)
    def _():
        ref[...] = jnp.zeros_like(ref)

    ref[...] += val


def _gmlp_fwd(z, ln_g, ln_b, wcat, bfull, *, name):
    s = z.shape[0]
    t = _pick(s, (512, 256, 128))
    nch = t // GMLP_CHUNK

    def body(zu_ref, zv_ref, g_ref, b_ref, w_ref, bf_ref, o_ref):
        for c in range(nch):
            rows = pl.ds(c * GMLP_CHUNK, GMLP_CHUNK)
            o_ref[rows, :] = _gmlp_chunk(zu_ref[rows, :], zv_ref[rows, :], g_ref[...], b_ref[...], w_ref[...],
                                         bf_ref[...]).astype(BF16)

    col = lambda c: pl.BlockSpec((t, W_GRP), lambda i: (i, c))
    params = (ln_g, ln_b, wcat, bfull)
    return _pcall(body, name=name, out_shape=_sds((s, D_MODEL), BF16), grid=(s // t,),
                  in_specs=[col(0), col(1)] + [_spec(a) for a in params],
                  out_specs=pl.BlockSpec((t, W_GRP), lambda i: (i, 0)), semantics=("parallel",),
                  block_bytes=4 * _nbytes((t, W_GRP), F32))(z, z, *[_arr(a) for a in params])


def _gmlp_bwd(z, dmix, ln_g, ln_b, wcat, bfull, *, name):
    s = z.shape[0]
    t = _pick(s, (512, 256, 128))
    nch = t // GMLP_CHUNK

    def body(zu_ref, zv_ref, dy_ref, g_ref, b_ref, w_ref, bf_ref, dz_ref, dg_ref, db_ref, dw_ref, dbf_ref):
        acc = None
        for c in range(nch):
            rows = pl.ds(c * GMLP_CHUNK, GMLP_CHUNK)
            _, vjp = jax.vjp(_gmlp_chunk, zu_ref[rows, :], zv_ref[rows, :], g_ref[...], b_ref[...], w_ref[...],
                             bf_ref[...])
            du, dv, *dps = vjp(dy_ref[rows, :])
            dz_ref[rows, :] = jnp.concatenate([du, dv], axis=1).astype(BF16)
            acc = dps if acc is None else [x + y for x, y in zip(acc, dps)]
        first = pl.program_id(0) == 0
        for ref, val in zip((dg_ref, db_ref, dw_ref, dbf_ref), acc):
            _acc_out(ref, val, first)

    col = lambda c: pl.BlockSpec((t, W_GRP), lambda i: (i, c))
    params = (ln_g, ln_b, wcat, bfull)
    return _pcall(body, name=name,
                  out_shape=(_sds((s, D_PROJ), BF16),) + tuple(_sds(a.shape, F32) for a in params),
                  grid=(s // t,), in_specs=[col(0), col(1), col(0)] + [_spec(a) for a in params],
                  out_specs=(pl.BlockSpec((t, 2 * W_GRP), lambda i: (i, 0)),) + tuple(_ospec(a) for a in params),
                  semantics=("arbitrary",),
                  block_bytes=8 * _nbytes((t, W_GRP), F32))(z, z, dmix, *[_arr(a) for a in params])


def _rglru_fwd(z, prm, mix, *, name):
    s = z.shape[0]
    t = _pick(s, (512, 256, 128))
    nt = s // t

    def body(xb_ref, halo_ref, gb_ref, *rest):
        prm_refs, (y_ref, h0s_ref, h_scr) = rest[:len(prm)], rest[len(prm) + 1:]
        i = pl.program_id(0)

        @pl.when(i == 0)
        def _():
            h_scr[...] = jnp.zeros_like(h_scr)

        halo = jnp.where(i == 0, 0.0, halo_ref[...])
        h0 = h_scr[...]
        y, h_last = _rglru_tile(jnp.concatenate([halo, xb_ref[...]], axis=0), gb_ref[...], h0,
                                *[r[...] for r in prm_refs])
        y_ref[...] = y.astype(BF16)
        h0s_ref[...] = jnp.broadcast_to(h0, h0s_ref.shape)
        h_scr[...] = h_last

    in_specs = [pl.BlockSpec((t, W_GRP), lambda i: (i, 2)),
                pl.BlockSpec((8, W_GRP), lambda i: (jnp.maximum(i * (t // 8) - 1, 0), 2)),
                pl.BlockSpec((t, W_GRP), lambda i: (i, 3))] + [_spec(a) for a in prm] + [HBM_SPEC]
    return _pcall(body, name=name, out_shape=(_sds(mix.shape, BF16), _sds((nt, 8, W_GRP), F32)), grid=(nt,),
                  in_specs=in_specs,
                  out_specs=(pl.BlockSpec((t, W_GRP), lambda i: (i, 1)), pl.BlockSpec((None, 8, W_GRP), lambda i: (i, 0, 0))),
                  scratch_shapes=[pltpu.VMEM((1, W_GRP), F32)], semantics=("arbitrary",),
                  block_bytes=24 * _nbytes((t, W_GRP), F32), aliases={3 + len(prm): 0})(
                      z, z, z, *[_arr(a) for a in prm], mix)


def _rglru_bwd(z, dmix, h0s, prm, dz, *, name):
    s = z.shape[0]
    t = _pick(s, (512, 256, 128))
    nt = s // t
    npm = len(prm)

    def body(xb_ref, halo_ref, gb_ref, dy_ref, h0s_ref, *rest):
        prm_refs = rest[:npm]
        dz_ref = rest[npm + 1]
        dprm_refs = rest[npm + 2:2 * npm + 2]
        dh_scr, dhalo_scr = rest[2 * npm + 2:]
        i = pl.program_id(0)
        r = nt - 1 - i

        @pl.when(i == 0)
        def _():
            dh_scr[...] = jnp.zeros_like(dh_scr)
            dhalo_scr[...] = jnp.zeros_like(dhalo_scr)

        halo = jnp.where(r == 0, 0.0, halo_ref[...])
        h0 = h0s_ref[0:1, :]
        _, vjp = jax.vjp(_rglru_tile, jnp.concatenate([halo, xb_ref[...]], axis=0), gb_ref[...], h0,
                         *[p[...] for p in prm_refs])
        dext, dgb, _dh0, *dps = vjp((dy_ref[...], dh_scr[...]))
        dmain = dext[8:]
        dxb = jnp.concatenate([dmain[:t - 8], dmain[t - 8:] + dhalo_scr[...]], axis=0)
        dz_ref[...] = jnp.concatenate([dxb, dgb], axis=1).astype(BF16)
        dh_scr[...] = _dh0
        dhalo_scr[...] = dext[:8]
        for ref, val in zip(dprm_refs, dps):
            _acc_out(ref, val, i == 0)

    rev = lambda c: pl.BlockSpec((t, W_GRP), lambda i: (nt - 1 - i, c))
    in_specs = [rev(2), pl.BlockSpec((8, W_GRP), lambda i: (jnp.maximum((nt - 1 - i) * (t // 8) - 1, 0), 2)), rev(3),
                rev(1), pl.BlockSpec((None, 8, W_GRP), lambda i: (nt - 1 - i, 0, 0))] + [_spec(a) for a in prm] + [HBM_SPEC]
    return _pcall(body, name=name,
                  out_shape=(_sds(dz.shape, BF16),) + tuple(_sds(a.shape, F32) for a in prm),
                  grid=(nt,), in_specs=in_specs,
                  out_specs=(pl.BlockSpec((t, 2 * W_GRP), lambda i: (nt - 1 - i, 1)),) + tuple(_ospec(a) for a in prm),
                  scratch_shapes=[pltpu.VMEM((1, W_GRP), F32), pltpu.VMEM((8, W_GRP), F32)],
                  semantics=("arbitrary",), block_bytes=40 * _nbytes((t, W_GRP), F32), aliases={5 + npm: 0})(
                      z, z, z, dmix, h0s, *[_arr(a) for a in prm], dz)


def _pool_inv(i, t):
    pos = (_rows_of((t, W_GRP)) + i * t + 1).astype(F32)
    grp = _lanes_of((t, W_GRP)) // HEAD_DIM
    win = jnp.where(grp == 0, float(POOL_WINDOWS[0]), jnp.where(grp == 1, float(POOL_WINDOWS[1]),
                    jnp.where(grp == 2, float(POOL_WINDOWS[2]), float(POOL_WINDOWS[3]))))
    return 1.0 / jnp.minimum(pos, win)


def _pool_fwd(z, wd, scale, mix, *, name):
    s = z.shape[0]
    t = _pick(s, (512, 256, 128))

    def body(x_ref, halo_ref, wd_ref, sc_ref, _, y_ref):
        i = pl.program_id(0)
        halo = jnp.where(i == 0, 0.0, halo_ref[...])
        y = _pool_tile(jnp.concatenate([halo, x_ref[...]], axis=0), _pool_inv(i, t), wd_ref[...], sc_ref[...])
        y_ref[...] = y.astype(BF16)

    in_specs = [pl.BlockSpec((t, W_GRP), lambda i: (i, 8)),
                pl.BlockSpec((16, W_GRP), lambda i: (jnp.maximum(i * (t // 16) - 1, 0), 8)), _spec(wd), _spec(scale),
                HBM_SPEC]
    return _pcall(body, name=name, out_shape=_sds(mix.shape, BF16), grid=(s // t,), in_specs=in_specs,
                  out_specs=pl.BlockSpec((t, W_GRP), lambda i: (i, 3)), semantics=("parallel",),
                  block_bytes=12 * _nbytes((t, W_GRP), F32), aliases={4: 0})(z, z, _arr(wd), _arr(scale), mix)


def _pool_bwd(z, dmix, wd, scale, dz, *, name):
    s = z.shape[0]
    t = _pick(s, (512, 256, 128))
    nt = s // t

    def body(x_ref, halo_ref, dy_ref, wd_ref, sc_ref, _, dx_ref, dwd_ref, dsc_ref, dhalo_scr):
        i = pl.program_id(0)
        r = nt - 1 - i

        @pl.when(i == 0)
        def _():
            dhalo_scr[...] = jnp.zeros_like(dhalo_scr)

        halo = jnp.where(r == 0, 0.0, halo_ref[...])
        inv = _pool_inv(r, t)
        _, vjp = jax.vjp(lambda e, w, sc: _pool_tile(e, inv, w, sc), jnp.concatenate([halo, x_ref[...]], axis=0),
                         wd_ref[...], sc_ref[...])
        dext, dwd, dsc = vjp(dy_ref[...])
        dmain = dext[16:]
        dx = jnp.concatenate([dmain[:t - 16], dmain[t - 16:] + dhalo_scr[...]], axis=0)
        dx_ref[...] = dx.astype(BF16)
        dhalo_scr[...] = dext[:16]
        _acc_out(dwd_ref, dwd, i == 0)
        _acc_out(dsc_ref, dsc, i == 0)

    rev = lambda c: pl.BlockSpec((t, W_GRP), lambda i: (nt - 1 - i, c))
    in_specs = [rev(8), pl.BlockSpec((16, W_GRP), lambda i: (jnp.maximum((nt - 1 - i) * (t // 16) - 1, 0), 8)), rev(3),
                _spec(wd), _spec(scale), HBM_SPEC]
    return _pcall(body, name=name, out_shape=(_sds(dz.shape, BF16), _sds(wd.shape, F32), _sds(scale.shape, F32)),
                  grid=(nt,), in_specs=in_specs, out_specs=(rev(8), _ospec(wd), _ospec(scale)),
                  scratch_shapes=[pltpu.VMEM((16, W_GRP), F32)], semantics=("arbitrary",),
                  block_bytes=20 * _nbytes((t, W_GRP), F32), aliases={5: 0})(z, z, dmix, _arr(wd), _arr(scale), dz)


def _hgrn_fwd(z, lb, ngf, mix, *, name):
    s = z.shape[0]
    c = HGRN_CHUNK
    per = HGRN_STEP_CHUNKS
    ns = s // (c * per)

    def body(q_ref, f_ref, i_ref, g_ref, lb_ref, ng_ref, _, y_ref, sts_ref, st_scr):
        @pl.when(pl.program_id(0) == 0)
        def _():
            st_scr[...] = jnp.zeros_like(st_scr)

        st = st_scr[...]
        for k in range(per):
            rows = pl.ds(k * c, c)
            sts_ref[k] = st
            y, st = _hgrn_chunk(q_ref[rows, :], f_ref[rows, :], i_ref[rows, :], g_ref[rows, :], st, lb_ref[...],
                                ng_ref[...])
            y_ref[rows, :] = y.astype(BF16)
        st_scr[...] = st

    col = lambda k: pl.BlockSpec((per * c, W_GRP), lambda i: (i, k))
    return _pcall(body, name=name, out_shape=(_sds(mix.shape, BF16), _sds((ns * per, W_GRP, W_GRP), F32)), grid=(ns,),
                  in_specs=[col(4), col(5), col(6), col(7), _spec(lb), _spec(ngf), HBM_SPEC],
                  out_specs=(pl.BlockSpec((per * c, W_GRP), lambda i: (i, 2)),
                             pl.BlockSpec((per, W_GRP, W_GRP), lambda i: (i, 0, 0))),
                  scratch_shapes=[pltpu.VMEM((W_GRP, W_GRP), F32)], semantics=("arbitrary",),
                  block_bytes=16 * per * _nbytes((W_GRP, W_GRP), F32), aliases={6: 0})(
                      z, z, z, z, _arr(lb), _arr(ngf), mix)


def _hgrn_bwd(z, dmix, sts, lb, ngf, dz, *, name):
    s = z.shape[0]
    c = HGRN_CHUNK
    per = HGRN_STEP_CHUNKS
    ns = s // (c * per)

    def body(q_ref, f_ref, i_ref, g_ref, dy_ref, st_ref, lb_ref, ng_ref, _, dz_ref, dlb_ref, dng_ref, dst_scr):
        i = pl.program_id(0)

        @pl.when(i == 0)
        def _():
            dst_scr[...] = jnp.zeros_like(dst_scr)

        dst = dst_scr[...]
        dlb_sum = dng_sum = None
        for k in range(per - 1, -1, -1):
            rows = pl.ds(k * c, c)
            _, vjp = jax.vjp(_hgrn_chunk, q_ref[rows, :], f_ref[rows, :], i_ref[rows, :], g_ref[rows, :], st_ref[k],
                             lb_ref[...], ng_ref[...])
            dq, df, di, dg, dst, dlb, dng = vjp((dy_ref[rows, :], dst))
            dz_ref[rows, :] = jnp.concatenate([dq, df, di, dg], axis=1).astype(BF16)
            dlb_sum = dlb if dlb_sum is None else dlb_sum + dlb
            dng_sum = dng if dng_sum is None else dng_sum + dng
        dst_scr[...] = dst
        _acc_out(dlb_ref, dlb_sum, i == 0)
        _acc_out(dng_ref, dng_sum, i == 0)

    rev = lambda k: pl.BlockSpec((per * c, W_GRP), lambda i: (ns - 1 - i, k))
    vec = pl.BlockSpec((1, W_GRP), lambda i: (0, 0))
    return _pcall(body, name=name, out_shape=(_sds(dz.shape, BF16), _sds((1, W_GRP), F32), _sds((1, W_GRP), F32)),
                  grid=(ns,),
                  in_specs=[rev(4), rev(5), rev(6), rev(7), rev(2),
                            pl.BlockSpec((per, W_GRP, W_GRP), lambda i: (ns - 1 - i, 0, 0)), _spec(lb), _spec(ngf),
                            HBM_SPEC],
                  out_specs=(pl.BlockSpec((per * c, 4 * W_GRP), lambda i: (ns - 1 - i, 1)), vec, vec),
                  scratch_shapes=[pltpu.VMEM((W_GRP, W_GRP), F32)], semantics=("arbitrary",),
                  block_bytes=32 * per * _nbytes((W_GRP, W_GRP), F32), aliases={8: 0})(
                      z, z, z, z, dmix, sts, _arr(lb), _arr(ngf), dz)


def _lbs_fwd(c_lb, *, name):
    def body(c_ref, o_ref):
        c = c_ref[...]
        e = jnp.exp(c - jnp.max(c, axis=0, keepdims=True))
        sm = e / jnp.sum(e, axis=0, keepdims=True)
        run = jnp.zeros((1, W_GRP), F32)
        o_ref[0:1, :] = run
        for l in range(1, DEPTH):
            run = run + sm[l:l + 1]
            o_ref[l:l + 1, :] = run

    return _pcall(body, name=name, out_shape=_sds((DEPTH, W_GRP), F32))(c_lb)


def _lbs_bwd(c_lb, dlbs, *, name):
    def body(c_ref, d_ref, o_ref):
        c = c_ref[...]
        e = jnp.exp(c - jnp.max(c, axis=0, keepdims=True))
        sm = e / jnp.sum(e, axis=0, keepdims=True)
        d = d_ref[...]
        dsm = [None] * DEPTH
        run = jnp.zeros((1, W_GRP), F32)
        for l in range(DEPTH - 1, 0, -1):
            run = run + d[l:l + 1]
            dsm[l] = run
        dsm[0] = jnp.zeros((1, W_GRP), F32)
        inner = sum(sm[l:l + 1] * dsm[l] for l in range(DEPTH))
        for l in range(DEPTH):
            o_ref[l:l + 1, :] = sm[l:l + 1] * (dsm[l] - inner)

    return _pcall(body, name=name, out_shape=_sds((DEPTH, W_GRP), F32))(c_lb, dlbs)


def _ffn_fwd(hg, hv, cwf, cbf, *, name):
    s, n = hg.shape
    t = _pick(s, (256, 128))
    cw = _pick(n, (1408, 256, 128))
    nj = n // cw

    def body(g_ref, gh_ref, v_ref, vh_ref, wg_ref, bg_ref, wv_ref, bv_ref, o_ref):
        first = pl.program_id(1) == 0
        eg = jnp.concatenate([jnp.where(first, 0.0, gh_ref[...]), g_ref[...]], axis=0)
        ev = jnp.concatenate([jnp.where(first, 0.0, vh_ref[...]), v_ref[...]], axis=0)
        o_ref[...] = _ffn_tile(eg, ev, wg_ref[...], bg_ref[...], wv_ref[...], bv_ref[...]).astype(BF16)

    main = pl.BlockSpec((t, cw), lambda j, i: (i, j))
    halo = pl.BlockSpec((8, cw), lambda j, i: (jnp.maximum(i * (t // 8) - 1, 0), j))
    taps = lambda off: _spec(cwf, (3, cw), lambda j, i: (0, j + off))
    bias = lambda off: _spec(cbf, (1, cw), lambda j, i: (0, j + off))
    return _pcall(body, name=name, out_shape=_sds((s, n), BF16), grid=(nj, s // t),
                  in_specs=[main, halo, main, halo, taps(0), bias(0), taps(nj), bias(nj)], out_specs=main,
                  semantics=("parallel", "parallel"), block_bytes=12 * _nbytes((t, cw), F32))(
                      hg, hg, hv, hv, _arr(cwf), _arr(cbf), _arr(cwf), _arr(cbf))


def _ffn_bwd(hg, hv, da, cwf, cbf, *, name):
    s, n = hg.shape
    t = _pick(s, (256, 128))
    cw = _pick(n, (1408, 256, 128))
    nt = s // t
    nj = n // cw

    def body(g_ref, gh_ref, v_ref, vh_ref, da_ref, wg_ref, bg_ref, wv_ref, bv_ref, dg_ref, dv_ref, dwg_ref, dwv_ref,
             cg_scr, cv_scr):
        i = pl.program_id(1)
        r = nt - 1 - i

        @pl.when(i == 0)
        def _():
            cg_scr[...] = jnp.zeros_like(cg_scr)
            cv_scr[...] = jnp.zeros_like(cv_scr)

        eg = jnp.concatenate([jnp.where(r == 0, 0.0, gh_ref[...]), g_ref[...]], axis=0)
        ev = jnp.concatenate([jnp.where(r == 0, 0.0, vh_ref[...]), v_ref[...]], axis=0)
        _, vjp = jax.vjp(_ffn_tile, eg, ev, wg_ref[...], bg_ref[...], wv_ref[...], bv_ref[...])
        deg, dev, dwg, dbg, dwv, dbv = vjp(da_ref[...])
        for dext, scr, ref in ((deg, cg_scr, dg_ref), (dev, cv_scr, dv_ref)):
            dmain = dext[8:]
            ref[...] = jnp.concatenate([dmain[:t - 8], dmain[t - 8:] + scr[...]], axis=0).astype(BF16)
            scr[...] = dext[:8]
        zeros = jnp.zeros((4, cw), F32)
        _acc_out(dwg_ref, jnp.concatenate([dwg, dbg, zeros], axis=0), i == 0)
        _acc_out(dwv_ref, jnp.concatenate([dwv, dbv, zeros], axis=0), i == 0)

    main = pl.BlockSpec((t, cw), lambda j, i: (nt - 1 - i, j))
    halo = pl.BlockSpec((8, cw), lambda j, i: (jnp.maximum((nt - 1 - i) * (t // 8) - 1, 0), j))
    taps = lambda off: _spec(cwf, (3, cw), lambda j, i: (0, j + off))
    bias = lambda off: _spec(cbf, (1, cw), lambda j, i: (0, j + off))
    w8 = pl.BlockSpec((8, cw), lambda j, i: (0, j))
    return _pcall(body, name=name,
                  out_shape=(_sds((s, n), BF16), _sds((s, n), BF16), _sds((8, n), F32), _sds((8, n), F32)),
                  grid=(nj, nt), in_specs=[main, halo, main, halo, main, taps(0), bias(0), taps(nj), bias(nj)],
                  out_specs=(main, main, w8, w8),
                  scratch_shapes=[pltpu.VMEM((8, cw), F32), pltpu.VMEM((8, cw), F32)],
                  semantics=("parallel", "arbitrary"), block_bytes=24 * _nbytes((t, cw), F32))(
                      hg, hg, hv, hv, da, _arr(cwf), _arr(cbf), _arr(cwf), _arr(cbf))


def _all_gather(x, *, name):
    r, c = x.shape

    def body(x_ref, out_ref, send_sems, recv_sems, local_sem):
        mx, my, mc = lax.axis_index("x"), lax.axis_index("y"), lax.axis_index("c")
        me, sibling = (mx, my, mc), (mx, my, 1 - mc)
        chips = [(1 - mx, my), (mx, 1 - my), (1 - mx, 1 - my)]

        def slot(px, py, pc):
            return out_ref.at[4 * px + 2 * py + pc]

        def copy(k, block, to, src=None):
            return pltpu.make_async_remote_copy(src_ref=slot(*block) if src is None else src, dst_ref=slot(*block),
                                                send_sem=send_sems.at[k], recv_sem=recv_sems.at[k],
                                                device_id=to, device_id_type=MESH)

        mine = pltpu.make_async_copy(x_ref, slot(*me), local_sem)
        mine.start()
        first = [copy(0, me, sibling, src=x_ref)]
        first += [copy(1 + j, me, (*chip, mc), src=x_ref) for j, chip in enumerate(chips)]
        for cp in first:
            cp.start()
        passed = [copy(4 + j, (*chip, mc), sibling) for j, chip in enumerate(chips)]
        for j, chip in enumerate(chips):
            copy(1 + j, (*chip, mc), me).wait_recv()
            passed[j].start()
        copy(0, sibling, me).wait_recv()
        for j, chip in enumerate(chips):
            copy(4 + j, (*chip, 1 - mc), me).wait_recv()
        for cp in first + passed:
            cp.wait_send()
        mine.wait()

    hbm = pl.BlockSpec(memory_space=pl.ANY)
    return _pcall(body, name=name, out_shape=_sds((N_DEV, r, c), x.dtype), in_specs=[hbm], out_specs=hbm,
                  scratch_shapes=[pltpu.SemaphoreType.DMA((7,)), pltpu.SemaphoreType.DMA((7,)),
                                  pltpu.SemaphoreType.DMA(())])(x)


def _sum_slots(p, *, name):
    q, r, c = p.shape
    tr = _pick(r, (544, 408, 272, 192, 136, 64, 32, 16, 8))

    def body(p_ref, o_ref):
        acc = p_ref[0].astype(F32)
        for k in range(1, q):
            acc = acc + p_ref[k].astype(F32)
        o_ref[...] = acc

    return _pcall(body, name=name, out_shape=_sds((r, c), F32), grid=(r // tr,),
                  in_specs=[pl.BlockSpec((q, tr, c), lambda i: (0, i, 0))],
                  out_specs=pl.BlockSpec((tr, c), lambda i: (i, 0)), semantics=("parallel",),
                  block_bytes=(q + 2) * _nbytes((tr, c), F32))(p)


BIG_COMM = (('w_in', 288, D_MODEL), ('w_out', 128, D_MODEL), ('w_up', 704, D_MODEL), ('w_down', 352, D_MODEL),
            ('w_pe', 128, PLE_DIM), ('w_pg', 128, D_MODEL))
HBM_SPEC = pl.BlockSpec(memory_space=pl.ANY)


def _gather_layer(shards, l, *, name):
    na = len(shards)

    def body(*refs):
        x_refs, out_refs = refs[:na], refs[na:2 * na]
        send_sems, recv_sems, local_sems = refs[2 * na:]
        mx, my, mc = lax.axis_index("x"), lax.axis_index("y"), lax.axis_index("c")
        me, sibling = (mx, my, mc), (mx, my, 1 - mc)
        chips = [(1 - mx, my), (mx, 1 - my), (1 - mx, 1 - my)]

        def slot(a, px, py, pc):
            return out_refs[a].at[4 * px + 2 * py + pc]

        def copy(k, a, block, to, own=False):
            return pltpu.make_async_remote_copy(src_ref=x_refs[a].at[l] if own else slot(a, *block),
                                                dst_ref=slot(a, *block), send_sem=send_sems.at[k, a],
                                                recv_sem=recv_sems.at[k, a], device_id=to, device_id_type=MESH)

        mine = [pltpu.make_async_copy(x_refs[a].at[l], slot(a, *me), local_sems.at[a]) for a in range(na)]
        for cp in mine:
            cp.start()
        first = []
        for a in range(na):
            first.append(copy(0, a, me, sibling, own=True))
            first += [copy(1 + j, a, me, (*chip, mc), own=True) for j, chip in enumerate(chips)]
        for cp in first:
            cp.start()
        passed = []
        for j, chip in enumerate(chips):
            for a in range(na):
                copy(1 + j, a, (*chip, mc), me).wait_recv()
                fwd = copy(4 + j, a, (*chip, mc), sibling)
                fwd.start()
                passed.append(fwd)
        for a in range(na):
            copy(0, a, sibling, me).wait_recv()
        for j, chip in enumerate(chips):
            for a in range(na):
                copy(4 + j, a, (*chip, 1 - mc), me).wait_recv()
        for cp in first + passed:
            cp.wait_send()
        for cp in mine:
            cp.wait()

    return _pcall(body, name=name, out_shape=tuple(_sds((N_DEV,) + x.shape[1:], x.dtype) for x in shards),
                  in_specs=[HBM_SPEC] * na, out_specs=(HBM_SPEC,) * na,
                  scratch_shapes=[pltpu.SemaphoreType.DMA((7, na)), pltpu.SemaphoreType.DMA((7, na)),
                                  pltpu.SemaphoreType.DMA((na,))])(*shards)


SEM_SPEC = pl.BlockSpec(memory_space=pltpu.SEMAPHORE)
DATAFLOW_EFFECT = pltpu.SideEffectType.DATAFLOW_SIDE_EFFECTING


def _place_own(srcs, after, *, name):
    na = len(srcs)

    def body(*refs):
        x_refs, land_refs, sems = refs[:na], refs[na + len(after):2 * na + len(after)], refs[-1]
        me = 4 * lax.axis_index("x") + 2 * lax.axis_index("y") + lax.axis_index("c")
        cps = [pltpu.make_async_copy(x_refs[a], land_refs[a].at[me], sems.at[a]) for a in range(na)]
        for cp in cps:
            cp.start()
        for cp in cps:
            cp.wait()

    return _pcall(body, name=name, out_shape=tuple(_sds((N_DEV,) + x.shape, x.dtype) for x in srcs),
                  in_specs=[HBM_SPEC] * (na + len(after)), out_specs=(HBM_SPEC,) * na,
                  scratch_shapes=[pltpu.SemaphoreType.DMA((na,))])(*srcs, *after)


def _exchange_start(srcs, lands, *, name, per_peer=False):
    na = len(srcs)

    def body(*refs):
        x_refs, land_refs = refs[:na], refs[na:2 * na]
        send_sems, recv_sems = refs[2 * na], refs[2 * na + 1]
        token = refs[-1]
        mx, my, mc = lax.axis_index("x"), lax.axis_index("y"), lax.axis_index("c")
        me = 4 * mx + 2 * my + mc
        peers = [(mx, my, 1 - mc)]
        for px, py in ((1 - mx, my), (mx, 1 - my), (1 - mx, 1 - my)):
            peers += [(px, py, mc), (px, py, 1 - mc)]
        for a in range(na):
            for peer in peers:
                src = x_refs[a].at[4 * peer[0] + 2 * peer[1] + peer[2]] if per_peer else x_refs[a]
                pltpu.make_async_remote_copy(src_ref=src, dst_ref=land_refs[a].at[me], send_sem=send_sems.at[a],
                                             recv_sem=recv_sems.at[a], device_id=peer, device_id_type=MESH).start()
        token[...] = jnp.zeros_like(token)

    hbm = lambda x: pltpu.HBM(x.shape, x.dtype)
    out_shape = ((pltpu.SemaphoreType.DMA((na,)), pltpu.SemaphoreType.DMA((na,))) + tuple(hbm(x) for x in srcs)
                 + tuple(hbm(x) for x in lands) + (_sds((8, 128), F32),))
    params = pltpu.CompilerParams(has_side_effects=DATAFLOW_EFFECT)
    pin = lambda x: pltpu.with_memory_space_constraint(x, pltpu.HBM)
    return pl.pallas_call(body, name=name, out_shape=out_shape, in_specs=[HBM_SPEC] * (2 * na),
                          out_specs=(SEM_SPEC, SEM_SPEC) + (HBM_SPEC,) * (2 * na) + (pl.BlockSpec(memory_space=pltpu.VMEM),),
                          input_output_aliases={i: 2 + i for i in range(2 * na)}, compiler_params=params)(
                              *[pin(x) for x in srcs], *[pin(x) for x in lands])


def _exchange_wait(started, after, *, name):
    send_sems, recv_sems, *bufs, _ = started
    na = len(bufs) // 2

    def body(*refs):
        land_refs = refs[na:2 * na]
        s_sems, r_sems = refs[2 * na], refs[2 * na + 1]
        me = (lax.axis_index("x"), lax.axis_index("y"), lax.axis_index("c"))
        for a in range(na):
            seven = land_refs[a].at[pl.ds(0, N_DEV - 1)]
            cp = pltpu.make_async_remote_copy(src_ref=seven, dst_ref=seven, send_sem=s_sems.at[a], recv_sem=r_sems.at[a],
                                              device_id=me, device_id_type=MESH)
            cp.wait_send()
            cp.wait_recv()

    hbm = lambda x: pltpu.HBM(x.shape, x.dtype)
    params = pltpu.CompilerParams(has_side_effects=DATAFLOW_EFFECT)
    outs = pl.pallas_call(body, name=name, out_shape=tuple(hbm(x) for x in bufs),
                          in_specs=[HBM_SPEC] * (2 * na) + [SEM_SPEC, SEM_SPEC, HBM_SPEC],
                          out_specs=(HBM_SPEC,) * (2 * na), input_output_aliases={i: i for i in range(2 * na)},
                          compiler_params=params)(*bufs, send_sems, recv_sems, after)
    return outs[:na], outs[na:]


def _pair_swap(grads, *, name):
    na = len(grads)

    def body(*refs):
        g_refs, recv_refs = refs[:na], refs[na:2 * na]
        send_sems, recv_sems = refs[2 * na:]
        mx, my, mc = lax.axis_index("x"), lax.axis_index("y"), lax.axis_index("c")
        sibling = (mx, my, 1 - mc)
        for a in range(na):
            for q in range(4):
                pltpu.make_async_remote_copy(src_ref=g_refs[a].at[q, 1 - mc], dst_ref=recv_refs[a].at[q],
                                             send_sem=send_sems.at[a], recv_sem=recv_sems.at[a],
                                             device_id=sibling, device_id_type=MESH).start()
        for a in range(na):
            pltpu.make_async_remote_copy(src_ref=recv_refs[a], dst_ref=recv_refs[a], send_sem=send_sems.at[a],
                                         recv_sem=recv_sems.at[a], device_id=sibling, device_id_type=MESH).wait()

    half = tuple(_sds((4,) + g.shape[2:], g.dtype) for g in grads)
    return _pcall(body, name=name, out_shape=half, in_specs=[HBM_SPEC] * na, out_specs=(HBM_SPEC,) * na,
                  scratch_shapes=[pltpu.SemaphoreType.DMA((na,)), pltpu.SemaphoreType.DMA((na,))])(*grads)


def _add_slabs(grads, recv, core, *, name):
    na = len(grads)

    def body(core_ref, *refs):
        for a in range(na):
            refs[2 * na + a][...] = (refs[a][...].astype(F32) + refs[na + a][...].astype(F32)).astype(BF16)

    own_specs = [pl.BlockSpec((None, None) + x.shape[2:], lambda q, core_ref: (q, core_ref[0], 0, 0)) for x in grads]
    specs = [pl.BlockSpec((None,) + x.shape[1:], lambda q, core_ref: (q, 0, 0)) for x in recv]
    blk = sum(_nbytes(x.shape[1:], F32) for x in recv)
    grid_spec = pltpu.PrefetchScalarGridSpec(num_scalar_prefetch=1, grid=(4,), in_specs=own_specs + specs,
                                             out_specs=tuple(specs))
    params = pltpu.CompilerParams(dimension_semantics=("parallel",), vmem_limit_bytes=_vmem_limit(2 * blk))
    return pl.pallas_call(body, name=name, out_shape=tuple(_sds(x.shape, BF16) for x in recv), grid_spec=grid_spec,
                          compiler_params=params)(core, *grads, *recv)


def _chip_exchange(parts, *, name):
    na = len(parts)

    def body(*refs):
        p_refs, out_refs = refs[:na], refs[na:2 * na]
        send_sems, recv_sems, local_sems = refs[2 * na:]
        mx, my, mc = lax.axis_index("x"), lax.axis_index("y"), lax.axis_index("c")
        mine_q = 2 * mx + my
        chips = [(1 - mx, my), (mx, 1 - my), (1 - mx, 1 - my)]
        owns = [pltpu.make_async_copy(p_refs[a].at[mine_q], out_refs[a].at[mine_q], local_sems.at[a]) for a in range(na)]
        for cp in owns:
            cp.start()
        sends = []
        for a in range(na):
            for k, chip in enumerate(chips):
                sends.append(pltpu.make_async_remote_copy(
                    src_ref=p_refs[a].at[2 * chip[0] + chip[1]], dst_ref=out_refs[a].at[mine_q],
                    send_sem=send_sems.at[k, a], recv_sem=recv_sems.at[k, a], device_id=(*chip, mc), device_id_type=MESH))
        for cp in sends:
            cp.start()
        for a in range(na):
            for k, chip in enumerate(chips):
                pltpu.make_async_remote_copy(
                    src_ref=p_refs[a].at[mine_q], dst_ref=out_refs[a].at[2 * chip[0] + chip[1]],
                    send_sem=send_sems.at[k, a], recv_sem=recv_sems.at[k, a], device_id=(*chip, mc),
                    device_id_type=MESH).wait_recv()
        for cp in sends:
            cp.wait_send()
        for cp in owns:
            cp.wait()

    return _pcall(body, name=name, out_shape=tuple(_sds(x.shape, x.dtype) for x in parts), in_specs=[HBM_SPEC] * na,
                  out_specs=(HBM_SPEC,) * na,
                  scratch_shapes=[pltpu.SemaphoreType.DMA((3, na)), pltpu.SemaphoreType.DMA((3, na)),
                                  pltpu.SemaphoreType.DMA((na,))])(*parts)


def _sum_chips(parts, *, name):
    na = len(parts)

    def body(*refs):
        for a in range(na):
            p_ref = refs[a]
            acc = p_ref[0].astype(F32)
            for k in range(1, p_ref.shape[0]):
                acc = acc + p_ref[k].astype(F32)
            refs[na + a][...] = acc

    half = lambda x: x.shape[1] // 2
    in_specs = [pl.BlockSpec((x.shape[0], half(x), x.shape[2]), lambda i: (0, i, 0)) for x in parts]
    out_specs = tuple(pl.BlockSpec((half(x), x.shape[2]), lambda i: (i, 0)) for x in parts)
    blk = sum(_nbytes((x.shape[0] + 2, half(x), x.shape[2]), BF16) for x in parts)
    return _pcall(body, name=name, out_shape=tuple(_sds(x.shape[1:], F32) for x in parts), grid=(2,),
                  in_specs=in_specs, out_specs=out_specs, semantics=("parallel",), block_bytes=blk)(*parts)


def _sum_devices(lands, own, me, *, name):
    na = len(lands)

    def body(me_ref, *refs):
        mine = me_ref[0]
        for a in range(na):
            l_ref, o_ref = refs[a], refs[na + a]
            acc = None
            for k in range(N_DEV):
                term = jnp.where(mine == k, o_ref[...], l_ref[k]).astype(F32)
                acc = term if acc is None else acc + term
            refs[2 * na + a][...] = acc

    half = lambda x: x.shape[1] // 2
    land_specs = [pl.BlockSpec((N_DEV, half(x), x.shape[2]), lambda i, me_ref: (0, i, 0)) for x in lands]
    own_specs = [pl.BlockSpec((None, half(x), x.shape[2]), lambda i, me_ref: (me_ref[0], i, 0)) for x in lands]
    out_specs = tuple(pl.BlockSpec((half(x), x.shape[2]), lambda i, me_ref: (i, 0)) for x in lands)
    blk = sum(_nbytes((N_DEV + 3, half(x), x.shape[2]), BF16) for x in lands)
    grid_spec = pltpu.PrefetchScalarGridSpec(num_scalar_prefetch=1, grid=(2,), in_specs=land_specs + own_specs,
                                             out_specs=out_specs)
    params = pltpu.CompilerParams(dimension_semantics=("parallel",), vmem_limit_bytes=_vmem_limit(blk))
    return pl.pallas_call(body, name=name, out_shape=tuple(_sds(x.shape[1:], F32) for x in lands), grid_spec=grid_spec,
                          compiler_params=params)(me, *lands, *own)


def _reduce_layer(grads, l):
    n = lambda s: f"l{l}_{s}"
    views = [g.reshape(4, 2, g.shape[0] // N_DEV, g.shape[1]) for g in grads]
    recv = _pair_swap(views, name=n("reduce_pair"))
    core = lax.axis_index("c").astype(jnp.int32).reshape(1)
    chip_sum = _add_slabs(views, recv, core, name=n("reduce_pair_add"))
    from_chips = _chip_exchange(chip_sum, name=n("reduce_chips"))
    return _sum_chips(from_chips, name=n("reduce_chips_add"))


def _adamw(w, g, m, v, *, name):
    lead, (r, c) = w.shape[:-2], w.shape[-2:]
    tr = _pick(r, (512, 256, 192, 128, 64, 32, 16, 8))
    c1 = 1.0 / (1.0 - ADAM_B1 ** ADAM_STEP)
    c2 = 1.0 / (1.0 - ADAM_B2 ** ADAM_STEP)

    def body(w_ref, g_ref, m_ref, v_ref, d_ref, nm_ref, nv_ref):
        gv = g_ref[...]
        nm = ADAM_B1 * m_ref[...] + (1.0 - ADAM_B1) * gv
        nv = ADAM_B2 * v_ref[...] + (1.0 - ADAM_B2) * jnp.square(gv)
        d_ref[...] = -ADAM_LR * ((nm * c1) / (jnp.sqrt(nv * c2) + ADAM_EPS) + ADAM_WD * w_ref[...])
        nm_ref[...] = nm
        nv_ref[...] = nv

    if lead:
        blk = pl.BlockSpec((None, tr, c), lambda k, i: (k, i, 0))
        grid, sem = (lead[0], r // tr), ("parallel", "parallel")
    else:
        blk = pl.BlockSpec((tr, c), lambda i: (i, 0))
        grid, sem = (r // tr,), ("parallel",)
    out = _sds(w.shape, F32)
    return _pcall(body, name=name, out_shape=(out, out, out), grid=grid, in_specs=[blk] * 4,
                  out_specs=(blk, blk, blk), semantics=sem, block_bytes=7 * _nbytes((tr, c), F32))(w, g, m, v)


def _pack_flat(arrs, rows, cols=1024):
    flat = jnp.concatenate([a.reshape(-1).astype(F32) for a in arrs])
    pad = rows * cols - flat.shape[0]
    return jnp.pad(flat, (0, pad)).reshape(rows, cols)


def _unpack_flat(buf, shapes):
    flat = buf.reshape(-1)
    out, off = [], 0
    for shp in shapes:
        n = 1
        for s in shp:
            n *= s
        out.append(flat[off:off + n].reshape(shp))
        off += n
    return out


def _flat_rows(shapes, cols=1024):
    n = sum(functools.reduce(lambda a, b: a * b, shp, 1) for shp in shapes)
    rows = -(-n // cols)
    return -(-rows // 64) * 64


def _block_diag(w):
    eye = jnp.eye(N_HEADS, dtype=w.dtype)
    return (w[:, :, :, None, :] * eye[None, :, None, :, None]).reshape(w.shape[0], W_GRP, W_GRP)


def _diag_blocks(w):
    w5 = w.reshape(w.shape[0], N_HEADS, HEAD_DIM, N_HEADS, HEAD_DIM)
    return jnp.stack([w5[:, h, :, h, :] for h in range(N_HEADS)], axis=1)


def _stacked_params(w, lbs):
    tril = jnp.tril(jnp.ones((GMLP_CHUNK, GMLP_CHUNK), bool))
    row = lambda a: a.reshape(DEPTH, 1, -1)
    return dict(
        g1=row(w['norm1_g']), g2=row(w['norm2_g']), g3=row(w['norm3_g']),
        a_ln_g=row(w['a_ln_g']), a_ln_b=row(w['a_ln_b']),
        a_wcat=jnp.where(tril, w['a_ws'], 0.0).reshape(DEPTH, N_HEADS * GMLP_CHUNK, GMLP_CHUNK),
        a_bfull=jnp.repeat(jnp.swapaxes(w['a_bs'], 1, 2), HEAD_DIM, axis=2),
        b_cw=w['b_conv_w_full'], b_cb=row(w['b_conv_b']), b_wa=_block_diag(w['b_wa']), b_ba=row(w['b_ba']),
        b_wx=_block_diag(w['b_wx']), b_bx=row(w['b_bx']), b_lam=row(w['b_lam']),
        c_lb=row(lbs), c_ngf=row(jnp.tile(w['c_norm_g'], (1, N_HEADS))),
        d_wd=_block_diag(w['d_w']), d_scale=row(w['d_scale']),
        f_cw=w['ffn_conv_w_full'], f_cb=row(w['ffn_conv_b']),
    )


B_PRM = ('b_cw', 'b_cb', 'b_wa', 'b_ba', 'b_wx', 'b_bx', 'b_lam')


def _layer_fwd(x, p_bf, wb, sp, l):
    n = lambda s: f"l{l}_{s}"
    h = _rms_fwd(x, sp['g1'], name=n("norm1"))
    z = _matmul(h, wb['w_in'], nt=True, name=n("proj_in"))
    mix = _gmlp_fwd(z, sp['a_ln_g'], sp['a_ln_b'], sp['a_wcat'], sp['a_bfull'], name=n("gmlp"))
    mix, h0s = _rglru_fwd(z, [sp[k] for k in B_PRM], mix, name=n("rglru"))
    mix, sts = _hgrn_fwd(z, sp['c_lb'], sp['c_ngf'], mix, name=n("hgrn"))
    mix = _pool_fwd(z, sp['d_wd'], sp['d_scale'], mix, name=n("pool"))
    x1 = _matmul(mix, wb['w_out'], res=x, name=n("proj_out"))
    h2 = _rms_fwd(x1, sp['g2'], name=n("norm2"))
    hg = _matmul(h2, wb['w_up_g'], nt=True, name=n("up_gate"))
    hv = _matmul(h2, wb['w_up_v'], nt=True, name=n("up_val"))
    a = _ffn_fwd(hg, hv, sp['f_cw'], sp['f_cb'], name=n("ffn_gate"))
    x2 = _matmul(a, wb['w_down'], res=x1, name=n("down"))
    h3 = _rms_fwd(x2, sp['g3'], name=n("norm3"))
    gl = _matmul(h3, wb['w_pg'], name=n("ple_gate"))
    pe = _matmul(p_bf, wb['w_pe'], nt=True, name=n("ple_emb"))
    x3 = _ple_fwd(x2, gl, pe, name=n("ple"))
    saved = dict(x=x, h=h, z=z, h0s=h0s, sts=sts, mix=mix, x1=x1, h2=h2, hg=hg, hv=hv, a=a, x2=x2, h3=h3, gl=gl, pe=pe)
    return x3, saved


def _layer_bwd(dx3, sv, p_bf, wb, sp, l, mid=None):
    n = lambda s: f"l{l}_{s}_bwd"
    gb, gs = {}, {}
    dpe, dgl = _ple_bwd(dx3, sv['gl'], sv['pe'], name=n("ple"))
    gb['w_pe'] = _matmul_tn(dpe, p_bf, name=n("ple_emb_w"))
    gb['w_pg'] = _matmul_tn(sv['h3'], dgl, name=n("ple_gate_w"))
    dh3 = _matmul(dgl, wb['w_pg'], nt=True, name=n("ple_gate_x"))
    dx2, dx2b, gs['norm3_g'] = _rms_bwd(sv['x2'], sp['g3'], dh3, dx3, name=n("norm3"))
    da = _matmul(dx2b, wb['w_down'], nt=True, name=n("down_x"))
    gb['w_down'] = _matmul_tn(sv['a'], dx2b, name=n("down_w"))
    dhg, dhv, gs['f_dwg'], gs['f_dwv'] = _ffn_bwd(sv['hg'], sv['hv'], da, sp['f_cw'], sp['f_cb'], name=n("ffn_gate"))
    gate_rows = _matmul_tn(dhg, sv['h2'], name=n("up_gate_w"), out_rows=2 * D_FF)
    gb['w_up'] = _matmul_tn(dhv, sv['h2'], name=n("up_val_w"), out_rows=2 * D_FF, row_off=D_FF, into=gate_rows)
    if mid is not None:
        sp = mid(gb, sp)
    dh2 = _matmul(dhg, wb['w_up_g'], name=n("up_gate_x"))
    dh2 = _matmul(dhv, wb['w_up_v'], res=dh2, name=n("up_val_x"))
    dx1, dx1b, gs['norm2_g'] = _rms_bwd(sv['x1'], sp['g2'], dh2, dx2, name=n("norm2"))
    dmix = _matmul(dx1b, wb['w_out'], nt=True, name=n("proj_out_x"))
    gb['w_out'] = _matmul_tn(sv['mix'], dx1b, name=n("proj_out_w"))
    z = sv['z']
    dz, gs['a_ln_g'], gs['a_ln_b'], gs['a_wcat'], gs['a_bfull'] = _gmlp_bwd(
        z, dmix, sp['a_ln_g'], sp['a_ln_b'], sp['a_wcat'], sp['a_bfull'], name=n("gmlp"))
    dz, *dbp = _rglru_bwd(z, dmix, sv['h0s'], [sp[k] for k in B_PRM], dz, name=n("rglru"))
    gs.update(zip(B_PRM, dbp))
    dz, gs['c_lb'], gs['c_ngf'] = _hgrn_bwd(z, dmix, sv['sts'], sp['c_lb'], sp['c_ngf'], dz, name=n("hgrn"))
    dz, gs['d_wd'], gs['d_scale'] = _pool_bwd(z, dmix, sp['d_wd'], sp['d_scale'], dz, name=n("pool"))
    gb['w_in'] = _matmul_tn(dz, sv['h'], name=n("proj_in_w"))
    dh = _matmul(dz, wb['w_in'], name=n("proj_in_x"))
    dx0, _, gs['norm1_g'] = _rms_bwd(sv['x'], sp['g1'], dh, dx1, name=n("norm1"))
    return dx0, gb, gs


SMALL_NAMES = [nm for nm in WEIGHT_NAMES if nm not in BIG_NAMES]
COL_SHARDED = ('w_in', 'w_up', 'w_pe')


def _comm_shards(w):
    return [(jnp.swapaxes(w[nm], 1, 2) if nm in COL_SHARDED else w[nm]).astype(BF16) for nm, _, _ in BIG_COMM]


def _full_weights(gathered):
    out = {nm: g.reshape(N_DEV * r, c) for g, (nm, r, c) in zip(gathered, BIG_COMM)}
    halves = out.pop('w_up').reshape(2, D_FF, D_MODEL)
    out['w_up_g'], out['w_up_v'] = _Sel(halves, 0), _Sel(halves, 1)
    return out


def _small_grads(raw):
    nl = len(raw)
    st = {k: jnp.stack([r[k] for r in raw]) for k in raw[0]}
    tril = jnp.tril(jnp.ones((GMLP_CHUNK, GMLP_CHUNK), bool))
    vec = lambda a: a.reshape(nl, -1)
    out = {nm: vec(st[k]) for nm, k in (('norm1_g', 'norm1_g'), ('norm2_g', 'norm2_g'), ('norm3_g', 'norm3_g'),
                                        ('a_ln_g', 'a_ln_g'), ('a_ln_b', 'a_ln_b'), ('b_conv_b', 'b_cb'),
                                        ('b_ba', 'b_ba'), ('b_bx', 'b_bx'), ('b_lam', 'b_lam'), ('c_lb', 'c_lb'),
                                        ('d_scale', 'd_scale'))}
    out['a_ws'] = jnp.where(tril, st['a_wcat'].reshape(nl, N_HEADS, GMLP_CHUNK, GMLP_CHUNK), 0.0)
    out['a_bs'] = jnp.swapaxes(st['a_bfull'].reshape(nl, GMLP_CHUNK, N_HEADS, HEAD_DIM).sum(-1), 1, 2)
    out['b_conv_w'] = st['b_cw']
    out['b_wa'], out['b_wx'], out['d_w'] = _diag_blocks(st['b_wa']), _diag_blocks(st['b_wx']), _diag_blocks(st['d_wd'])
    out['c_norm_g'] = st['c_ngf'].reshape(nl, N_HEADS, HEAD_DIM).sum(1)
    out['ffn_conv_w'] = jnp.concatenate([st['f_dwg'][:, 0:3], st['f_dwv'][:, 0:3]], axis=2)
    out['ffn_conv_b'] = jnp.concatenate([st['f_dwg'][:, 3], st['f_dwv'][:, 3]], axis=1)
    return out


def _step(w, m, v, x, p, target):
    s = x.shape[1]
    dev = 4 * lax.axis_index("x") + 2 * lax.axis_index("y") + lax.axis_index("c")
    xs = x.reshape(s, D_MODEL)

    shards = _comm_shards(w)
    conv_shapes = [w['b_conv_w'].shape, w['ffn_conv_w'].shape]
    conv_rows = _flat_rows(conv_shapes)
    conv_all = _all_gather(_pack_flat([w['b_conv_w'], w['ffn_conv_w']], conv_rows), name="gather_conv_weights")
    parts = [_unpack_flat(conv_all[d], conv_shapes) for d in range(N_DEV)]
    wf = dict(w)
    wf['b_conv_w_full'] = jnp.concatenate([pt[0] for pt in parts], axis=-1)
    wf['ffn_conv_w_full'] = jnp.concatenate([pt[1] for pt in parts], axis=-1)
    lbs = _lbs_fwd(w['c_lb'], name="hgrn_bounds")

    stacked = _stacked_params(wf, lbs)
    p_all = p.reshape(DEPTH, s, PLE_DIM).astype(BF16)
    xl, saved, wbs, sps = xs, [], [], []
    gathered = _gather_layer(shards, 0, name="l0_gather_weights")
    for l in range(DEPTH):
        sp = {k: _Sel(a, l) for k, a in stacked.items()}
        if l + 1 < DEPTH:
            own = [x[l + 1] for x in shards]
            after = [conv_all, *gathered] if l == 0 else [xl]
            lands = _place_own(own, after, name=f"l{l + 1}_gather_place")
            started = _exchange_start(own, lands, name=f"l{l + 1}_gather_start")
            sp['g1'] = stacked['g1'][l] + started[-1][0, 0]
        wb = _full_weights(gathered)
        p_bf = p_all[l]
        xl, sv = _layer_fwd(xl, p_bf, wb, sp, l)
        if l + 1 < DEPTH:
            gathered = _exchange_wait(started, xl, name=f"l{l + 1}_gather_wait")[1]
        saved.append((sv, p_bf))
        wbs.append(wb)
        sps.append(sp)
    loss_part, dx, dfinal = _loss_head(xl, w['final_g'].reshape(1, D_MODEL), target.reshape(s, D_MODEL), name="loss_head")
    loss = lax.psum(loss_part[0, 0], ("x", "y", "c"))

    dev1 = dev.astype(jnp.int32).reshape(1)
    names = [nm for nm, _, _ in BIG_COMM]

    def start_reduce(grads, name):
        views = [g.reshape(N_DEV, g.shape[0] // N_DEV, g.shape[1]) for g in grads]
        return _exchange_start(views, [lax.empty(g.shape, g.dtype) for g in views], name=name, per_peer=True)

    def finish_reduce(started, after, lname):
        own, lands = _exchange_wait(started, after, name=f"{lname}_reduce_wait")
        return _sum_devices(lands, own, dev1, name=f"{lname}_reduce_sum")

    reduced, small = [None] * DEPTH, [None] * DEPTH
    pending = None
    for l in range(DEPTH - 1, 0, -1):
        sv, p_bf = saved[l]
        sp = sps[l]
        if pending is not None:
            sp = dict(sp, g3=stacked['g3'][l] + pending[-1][0, 0])
        dx, gb, small[l] = _layer_bwd(dx, sv, p_bf, wbs[l], sp, l)
        if pending is not None:
            reduced[l + 1] = finish_reduce(pending, dx, f"l{l + 1}")
        pending = start_reduce([gb[nm] for nm in names], f"l{l}_reduce_start")
    early = ('w_up', 'w_down', 'w_pe', 'w_pg')
    mid_started = []

    def mid(gb, sp):
        mid_started.append(start_reduce([gb[nm] for nm in early], "l0_reduce_start"))
        return dict(sp, g2=stacked['g2'][0] + mid_started[0][-1][0, 0])

    upper_names = [nm for nm in SMALL_NAMES if nm != 'final_g']
    low_names = upper_names + ['final_g']
    upper = _small_grads(small[1:])
    upper_shapes = [upper[nm].shape for nm in upper_names]
    upper_packed = [_pack_flat([upper[nm] for nm in upper_names], _flat_rows(upper_shapes))]
    upper_started = _exchange_start(upper_packed, _place_own(upper_packed, [], name="upper_small_grads_place"),
                                    name="upper_small_grads_start")

    sv, p_bf = saved[0]
    g3 = stacked['g3'][0] + pending[-1][0, 0] + upper_started[-1][0, 0]
    dx, gb, small[0] = _layer_bwd(dx, sv, p_bf, wbs[0], dict(sps[0], g3=g3), 0, mid=mid)
    reduced[1] = finish_reduce(pending, dx, "l1")
    late = dict(zip(('w_in', 'w_out'), _reduce_layer([gb['w_in'], gb['w_out']], 0)))
    late.update(zip(early, finish_reduce(mid_started[0], late['w_in'], "l0")))
    reduced[0] = [late[nm] for nm in names]
    grad_x = dx.reshape(1, s, D_MODEL)
    low = _small_grads(small[:1])
    low['final_g'] = dfinal.reshape(D_MODEL)
    low_shapes = [low[nm].shape for nm in low_names]
    low_all = _all_gather(_pack_flat([low[nm] for nm in low_names], _flat_rows(low_shapes)), name="gather_small_grads")
    low_sum = dict(zip(low_names, _unpack_flat(_sum_slots(low_all, name="sum_small_grads"), low_shapes)))
    upper_all = _exchange_wait(upper_started, low_all, name="upper_small_grads_wait")[1][0]
    upper_sum = dict(zip(upper_names, _unpack_flat(_sum_slots(upper_all, name="sum_upper_small_grads"), upper_shapes)))
    gsmall = {nm: jnp.concatenate([low_sum[nm], upper_sum[nm]], axis=0) for nm in upper_names}
    gsmall['c_lb'] = _lbs_bwd(w['c_lb'], gsmall['c_lb'], name="hgrn_bounds_bwd")
    gsmall['final_g'] = low_sum['final_g']
    for nm in ('b_conv_w', 'ffn_conv_w'):
        width = w[nm].shape[-1]
        gsmall[nm] = lax.dynamic_slice_in_dim(gsmall[nm], dev * width, width, axis=2)

    grads, delta, new_m, new_v = {}, {}, {}, {}
    for a, (nm, _, _) in enumerate(BIG_COMM):
        t = (lambda x: jnp.swapaxes(x, 1, 2)) if nm in COL_SHARDED else (lambda x: x)
        g = jnp.stack([reduced[l][a] for l in range(DEPTH)])
        d, nm_, nv_ = _adamw(t(w[nm]), g, t(m[nm]), t(v[nm]), name=f"adamw_{nm}")
        grads[nm], delta[nm], new_m[nm], new_v[nm] = t(g), t(d), t(nm_), t(nv_)

    shapes = [w[nm].shape for nm in SMALL_NAMES]
    rows = _flat_rows(shapes)
    pk = lambda t: _pack_flat([t[nm] for nm in SMALL_NAMES], rows)
    d, nm_, nv_ = _adamw(pk(w), pk(gsmall), pk(m), pk(v), name="adamw_small")
    for nm, dd, mm_, vv_ in zip(SMALL_NAMES, _unpack_flat(d, shapes), _unpack_flat(nm_, shapes), _unpack_flat(nv_, shapes)):
        grads[nm], delta[nm], new_m[nm], new_v[nm] = gsmall[nm], dd, mm_, vv_

    return (loss, grad_x, *[grads[nm] for nm in WEIGHT_NAMES], *[delta[nm] for nm in WEIGHT_NAMES],
            *[new_m[nm] for nm in WEIGHT_NAMES], *[new_v[nm] for nm in WEIGHT_NAMES])


def kernel(x, p, norm1_g, w_in, a_ln_g, a_ln_b, a_ws, a_bs, b_conv_w, b_conv_b, b_wa, b_ba, b_wx, b_bx, b_lam, c_lb, c_norm_g, d_w, d_scale, w_out, norm2_g, w_up, ffn_conv_w, ffn_conv_b, w_down, norm3_g, w_pe, w_pg, final_g, loss_target, m_norm1_g, m_w_in, m_a_ln_g, m_a_ln_b, m_a_ws, m_a_bs, m_b_conv_w, m_b_conv_b, m_b_wa, m_b_ba, m_b_wx, m_b_bx, m_b_lam, m_c_lb, m_c_norm_g, m_d_w, m_d_scale, m_w_out, m_norm2_g, m_w_up, m_ffn_conv_w, m_ffn_conv_b, m_w_down, m_norm3_g, m_w_pe, m_w_pg, m_final_g, v_norm1_g, v_w_in, v_a_ln_g, v_a_ln_b, v_a_ws, v_a_bs, v_b_conv_w, v_b_conv_b, v_b_wa, v_b_ba, v_b_wx, v_b_bx, v_b_lam, v_c_lb, v_c_norm_g, v_d_w, v_d_scale, v_w_out, v_norm2_g, v_w_up, v_ffn_conv_w, v_ffn_conv_b, v_w_down, v_norm3_g, v_w_pe, v_w_pg, v_final_g):
    w = dict(norm1_g=norm1_g, w_in=w_in, a_ln_g=a_ln_g, a_ln_b=a_ln_b, a_ws=a_ws, a_bs=a_bs, b_conv_w=b_conv_w, b_conv_b=b_conv_b, b_wa=b_wa, b_ba=b_ba, b_wx=b_wx, b_bx=b_bx, b_lam=b_lam, c_lb=c_lb, c_norm_g=c_norm_g, d_w=d_w, d_scale=d_scale, w_out=w_out, norm2_g=norm2_g, w_up=w_up, ffn_conv_w=ffn_conv_w, ffn_conv_b=ffn_conv_b, w_down=w_down, norm3_g=norm3_g, w_pe=w_pe, w_pg=w_pg, final_g=final_g)
    m = dict(norm1_g=m_norm1_g, w_in=m_w_in, a_ln_g=m_a_ln_g, a_ln_b=m_a_ln_b, a_ws=m_a_ws, a_bs=m_a_bs, b_conv_w=m_b_conv_w, b_conv_b=m_b_conv_b, b_wa=m_b_wa, b_ba=m_b_ba, b_wx=m_b_wx, b_bx=m_b_bx, b_lam=m_b_lam, c_lb=m_c_lb, c_norm_g=m_c_norm_g, d_w=m_d_w, d_scale=m_d_scale, w_out=m_w_out, norm2_g=m_norm2_g, w_up=m_w_up, ffn_conv_w=m_ffn_conv_w, ffn_conv_b=m_ffn_conv_b, w_down=m_w_down, norm3_g=m_norm3_g, w_pe=m_w_pe, w_pg=m_w_pg, final_g=m_final_g)
    v = dict(norm1_g=v_norm1_g, w_in=v_w_in, a_ln_g=v_a_ln_g, a_ln_b=v_a_ln_b, a_ws=v_a_ws, a_bs=v_a_bs, b_conv_w=v_b_conv_w, b_conv_b=v_b_conv_b, b_wa=v_b_wa, b_ba=v_b_ba, b_wx=v_b_wx, b_bx=v_b_bx, b_lam=v_b_lam, c_lb=v_c_lb, c_norm_g=v_c_norm_g, d_w=v_d_w, d_scale=v_d_scale, w_out=v_w_out, norm2_g=v_norm2_g, w_up=v_w_up, ffn_conv_w=v_ffn_conv_w, ffn_conv_b=v_ffn_conv_b, w_down=v_w_down, norm3_g=v_norm3_g, w_pe=v_w_pe, w_pg=v_w_pg, final_g=v_final_g)
    return _step(w, m, v, x, p, loss_target)
```

```python
import functools

import jax
import jax.numpy as jnp
from jax import lax
from jax.experimental import pallas as pl
from jax.experimental.pallas import tpu as pltpu

F32 = jnp.float32
BF16 = jnp.bfloat16
MESH = pl.DeviceIdType.MESH

D_MODEL = 1024
DEPTH = 4
PLE_DIM = 256
W_GRP = 256
N_HEADS = 4
HEAD_DIM = 64
GMLP_CHUNK = 128
RGLRU_C = 8.0
HGRN_CHUNK = 64
HGRN_SUB = 16
HGRN_STEP_CHUNKS = 4
POOL_WINDOWS = (2, 4, 8, 16)
D_FF = 2816
D_PROJ = 2304
EPS = 1e-6
ADAM_LR = 0.001
ADAM_B1 = 0.9
ADAM_B2 = 0.999
ADAM_EPS = 1e-08
ADAM_WD = 0.01
ADAM_STEP = 10

N_DEV = 8
MIB = 2 ** 20
V7X_VMEM_BYTES = 64 * MIB
HGRN_EXP_CLAMP = 60.0

WEIGHT_NAMES = ['norm1_g', 'w_in', 'a_ln_g', 'a_ln_b', 'a_ws', 'a_bs', 'b_conv_w', 'b_conv_b', 'b_wa', 'b_ba', 'b_wx',
                'b_bx', 'b_lam', 'c_lb', 'c_norm_g', 'd_w', 'd_scale', 'w_out', 'norm2_g', 'w_up', 'ffn_conv_w',
                'ffn_conv_b', 'w_down', 'norm3_g', 'w_pe', 'w_pg', 'final_g']
BIG_NAMES = ('w_in', 'w_out', 'w_up', 'w_down', 'w_pe', 'w_pg')


def _vmem_limit(block_bytes):
    want = 2 * block_bytes + 24 * MIB
    return int(min(max(want, 32 * MIB), V7X_VMEM_BYTES - 8 * MIB))


def _in_hbm(x):
    return pltpu.with_memory_space_constraint(x, pltpu.HBM)


def _out_hbm(s):
    return pltpu.HBM(s.shape, s.dtype)


def _pcall(body, *, name, out_shape, grid=None, in_specs=None, out_specs=None, scratch_shapes=(),
           semantics=None, block_bytes=0, aliases=None, pin=True):
    kw = {} if aliases is None else {"input_output_aliases": aliases}
    if pin:
        out_shape = tuple(_out_hbm(s) for s in out_shape) if isinstance(out_shape, (tuple, list)) else _out_hbm(out_shape)
    if grid is not None:
        kw["grid"] = grid
    if in_specs is not None:
        kw["in_specs"] = in_specs
    if out_specs is not None:
        kw["out_specs"] = out_specs
    params = pltpu.CompilerParams(dimension_semantics=semantics, vmem_limit_bytes=_vmem_limit(block_bytes))
    call = pl.pallas_call(body, name=name, out_shape=out_shape, scratch_shapes=list(scratch_shapes),
                          compiler_params=params, **kw)
    return (lambda *args: call(*[_in_hbm(a) for a in args])) if pin else call


def _pick(n, cands):
    for c in cands:
        if n % c == 0:
            return c
    return n


def _nbytes(shape, dtype):
    n = 1
    for s in shape:
        n *= s
    return n * jnp.dtype(dtype).itemsize


def _sds(shape, dtype):
    return jax.ShapeDtypeStruct(tuple(shape), dtype)


class _Sel:
    def __init__(self, arr, *idx):
        self.arr, self.idx = arr, tuple(idx)
        self.shape = arr.shape[len(idx):]
        self.ndim = len(self.shape)
        self.dtype = arr.dtype


def _arr(a):
    return a.arr if isinstance(a, _Sel) else a


def _spec(a, block=None, index=None):
    block = tuple(a.shape) if block is None else tuple(block)
    index = (lambda *g: (0,) * len(block)) if index is None else index
    if isinstance(a, _Sel):
        lead = a.idx
        return pl.BlockSpec((None,) * len(lead) + block, lambda *g: lead + tuple(index(*g)))
    return pl.BlockSpec(block, lambda *g: tuple(index(*g)))


def _ospec(a):
    return pl.BlockSpec(tuple(a.shape), lambda *g: (0,) * a.ndim)


def _rows_of(shape):
    return lax.broadcasted_iota(jnp.int32, shape, 0)


def _lanes_of(shape):
    return lax.broadcasted_iota(jnp.int32, shape, 1)


def _sdn(x, k, fill):
    n = x.shape[0]
    return jnp.where(_rows_of(x.shape) >= k, pltpu.roll(x, k % n, 0), fill)


def _sup(x, k, fill):
    n = x.shape[0]
    return jnp.where(_rows_of(x.shape) < n - k, pltpu.roll(x, (n - k) % n, 0), fill)


@functools.partial(jax.custom_vjp, nondiff_argnums=(1,))
def _shift_dn(x, k):
    return pltpu.roll(x, k, 0)


def _shift_dn_fwd(x, k):
    return pltpu.roll(x, k, 0), None


def _shift_dn_bwd(k, _, g):
    return (pltpu.roll(g, g.shape[0] - k, 0),)


_shift_dn.defvjp(_shift_dn_fwd, _shift_dn_bwd)


def _lin_scan_impl(a, b, h0):
    n = a.shape[0]
    aa, bb = a, b
    k = 1
    while k < n:
        bb = aa * _sdn(bb, k, 0.0) + bb
        aa = aa * _sdn(aa, k, 1.0)
        k *= 2
    return bb + aa * h0


@jax.custom_vjp
def _lin_scan(a, b, h0):
    return _lin_scan_impl(a, b, h0)


def _lin_scan_fwd(a, b, h0):
    h = _lin_scan_impl(a, b, h0)
    return h, (a, h, h0)


def _lin_scan_bwd(res, g):
    a, h, h0 = res
    n = a.shape[0]
    cc, gg = _sup(a, 1, 0.0), g
    k = 1
    while k < n:
        gg = gg + cc * _sup(gg, k, 0.0)
        cc = cc * _sup(cc, k, 1.0)
        k *= 2
    first = _rows_of(a.shape) == 0
    hprev = jnp.where(first, h0, _sdn(h, 1, 0.0))
    dh0 = jnp.sum(jnp.where(first, a * gg, 0.0), axis=0, keepdims=True)
    return gg * hprev, gg, dh0


_lin_scan.defvjp(_lin_scan_fwd, _lin_scan_bwd)


def _cumsum_sub_impl(x):
    pos = _rows_of(x.shape) % HGRN_SUB
    k = 1
    while k < HGRN_SUB:
        x = x + jnp.where(pos >= k, pltpu.roll(x, k, 0), 0.0)
        k *= 2
    return x


@jax.custom_vjp
def _cumsum_sub(x):
    return _cumsum_sub_impl(x)


def _cumsum_sub_fwd(x):
    return _cumsum_sub_impl(x), None


def _cumsum_sub_bwd(_, g):
    n = g.shape[0]
    pos = _rows_of(g.shape) % HGRN_SUB
    k = 1
    while k < HGRN_SUB:
        g = g + jnp.where(pos < HGRN_SUB - k, pltpu.roll(g, n - k, 0), 0.0)
        k *= 2
    return (g,)


_cumsum_sub.defvjp(_cumsum_sub_fwd, _cumsum_sub_bwd)


def _dot(a, b, ca, cb):
    return lax.dot_general(a.astype(BF16), b.astype(BF16), (((ca,), (cb,)), ((), ())), preferred_element_type=F32)


@jax.custom_vjp
def _mm(a, b):
    return _dot(a, b, 1, 0)


def _mm_fwd(a, b):
    return _dot(a, b, 1, 0), (a, b)


def _mm_bwd(res, g):
    a, b = res
    return _dot(g, b, 1, 1), _dot(a, g, 0, 0)


_mm.defvjp(_mm_fwd, _mm_bwd)


@jax.custom_vjp
def _mm_nt(a, b):
    return _dot(a, b, 1, 1)


def _mm_nt_fwd(a, b):
    return _dot(a, b, 1, 1), (a, b)


def _mm_nt_bwd(res, g):
    a, b = res
    return _dot(g, b, 1, 0), _dot(g, a, 0, 0)


_mm_nt.defvjp(_mm_nt_fwd, _mm_nt_bwd)


@jax.custom_vjp
def _mm_tn(a, b):
    return _dot(a, b, 0, 0)


def _mm_tn_fwd(a, b):
    return _dot(a, b, 0, 0), (a, b)


def _mm_tn_bwd(res, g):
    a, b = res
    return _dot(b, g, 1, 1), _dot(a, g, 1, 0)


_mm_tn.defvjp(_mm_tn_fwd, _mm_tn_bwd)


def _head_mask(shape, h):
    return (_lanes_of(shape) // HEAD_DIM) == h


def _stack_heads(x):
    return jnp.concatenate([jnp.where(_head_mask(x.shape, h), x, 0.0) for h in range(N_HEADS)], axis=0)


def _unstack_heads(p):
    r = p.shape[0] // N_HEADS
    out = None
    for h in range(N_HEADS):
        blk = p[h * r:(h + 1) * r]
        term = jnp.where(_head_mask(blk.shape, h), blk, 0.0)
        out = term if out is None else out + term
    return out


def _segmean_impl(x):
    n = x.shape[1]
    same = (lax.broadcasted_iota(jnp.int32, (n, n), 0) // HEAD_DIM) == (lax.broadcasted_iota(jnp.int32, (n, n), 1) // HEAD_DIM)
    m = jnp.where(same, 1.0 / HEAD_DIM, 0.0).astype(BF16)
    hi = x.astype(BF16)
    lo = (x - hi.astype(F32)).astype(BF16)
    dn = (((1,), (0,)), ((), ()))
    return (lax.dot_general(hi, m, dn, preferred_element_type=F32)
            + lax.dot_general(lo, m, dn, preferred_element_type=F32))


@jax.custom_vjp
def _segmean(x):
    return _segmean_impl(x)


def _segmean_fwd(x):
    return _segmean_impl(x), None


def _segmean_bwd(_, g):
    return (_segmean_impl(g),)


_segmean.defvjp(_segmean_fwd, _segmean_bwd)


def _log1p(u):
    w = 1.0 + u
    return jnp.where(w == 1.0, u, jnp.log(w) * (u / (w - 1.0)))


def _softplus(y):
    return jnp.maximum(y, 0.0) + _log1p(jnp.exp(-jnp.abs(y)))


def _rms(x, g):
    return x * lax.rsqrt(jnp.mean(x * x, axis=-1, keepdims=True) + EPS) * g


def _gmlp_chunk(zu, zv, ln_g, ln_b, wcat, bfull):
    u = jax.nn.gelu(zu)
    v = jax.nn.gelu(zv)
    mu = jnp.mean(v, axis=-1, keepdims=True)
    var = jnp.mean(jnp.square(v - mu), axis=-1, keepdims=True)
    vn = (v - mu) * lax.rsqrt(var + EPS) * ln_g + ln_b
    sv = _unstack_heads(_mm(wcat, vn)) + bfull
    return u * sv


def _rglru_tile(xb_ext, gb, h0, cw, cb, wa, ba, wx, bx, lam):
    xc = (cb + cw[0:1] * _shift_dn(xb_ext, 3) + cw[1:2] * _shift_dn(xb_ext, 2) + cw[2:3] * _shift_dn(xb_ext, 1)
          + cw[3:4] * xb_ext)[8:]
    r = jax.nn.sigmoid(_mm(xc, wa) + ba)
    i = jax.nn.sigmoid(_mm(xc, wx) + bx)
    log_a = (-RGLRU_C) * r * _softplus(-lam)
    a = jnp.exp(log_a)
    mult = jnp.sqrt(-jnp.tanh(log_a) * (a * a + 1.0))
    h = _lin_scan(a, mult * (i * xc), h0)
    y = h * jax.nn.gelu(gb)
    h_last = jnp.sum(jnp.where(_rows_of(h.shape) == h.shape[0] - 1, h, 0.0), axis=0, keepdims=True)
    return y, h_last


def _pool_tile(xd_ext, inv, wd, scale):
    s1 = xd_ext + _shift_dn(xd_ext, 1)
    s2 = s1 + _shift_dn(s1, 2)
    s3 = s2 + _shift_dn(s2, 4)
    s4 = s3 + _shift_dn(s3, 8)
    grp = _lanes_of(xd_ext.shape) // HEAD_DIM
    win = jnp.where(grp == 0, s1, jnp.where(grp == 1, s2, jnp.where(grp == 2, s3, s4)))
    pooled = win[16:] * inv - xd_ext[16:]
    return _mm(pooled, wd) * scale


def _hgrn_chunk(q, f, i, g, st, lb, ngf):
    n = q.shape[0]
    nsub = n // HGRN_SUB
    qs = jax.nn.silu(q)
    fg = lb + (1.0 - lb) * jax.nn.sigmoid(f)
    lf = jnp.log(fg)
    k = 1.0 - fg
    bl = _cumsum_sub(lf)
    row = _rows_of(q.shape)
    blk = row // HGRN_SUB
    betas = [jnp.zeros_like(lb)]
    for s in range(nsub):
        tot = jnp.sum(jnp.where(row == s * HGRN_SUB + HGRN_SUB - 1, bl, 0.0), axis=0, keepdims=True)
        betas.append(betas[-1] + tot)
    b_end = betas[nsub]
    beta_full = jnp.zeros_like(q)
    for s in range(1, nsub):
        beta_full = jnp.where(blk == s, betas[s], beta_full)
    qh = qs * jnp.exp(bl)
    qt = qh * jnp.exp(beta_full)
    b_all = beta_full + bl
    kt = k * jnp.exp(b_end - b_all)
    outs = []
    for s in range(nsub):
        kh = k * jnp.exp(jnp.minimum(betas[s] - b_all, HGRN_EXP_CLAMP))
        qstk = _stack_heads(qh[s * HGRN_SUB:(s + 1) * HGRN_SUB])
        att = _mm_nt(qstk, kh)
        ar = _rows_of(att.shape) % HGRN_SUB + s * HGRN_SUB
        att = jnp.where(_lanes_of(att.shape) <= ar, att, 0.0)
        outs.append(_unstack_heads(_mm(att, i)))
    o = jnp.concatenate(outs, axis=0) + _mm_nt(qt, st)
    same = (_rows_of(st.shape) // HEAD_DIM) == (_lanes_of(st.shape) // HEAD_DIM)
    st_new = st * jnp.exp(b_end) + jnp.where(same, _mm_tn(i, kt), 0.0)
    on = o * lax.rsqrt(_segmean(o * o) + EPS) * ngf
    return on * jax.nn.silu(g), st_new


def _ffn_tile(eg, ev, wg, bg, wv, bv):
    gt = (bg + wg[0:1] * _shift_dn(eg, 2) + wg[1:2] * _shift_dn(eg, 1) + wg[2:3] * eg)[8:]
    val = (bv + wv[0:1] * _shift_dn(ev, 2) + wv[1:2] * _shift_dn(ev, 1) + wv[2:3] * ev)[8:]
    return jax.nn.gelu(gt) * val


MXU_WIDTH = 256
MATMUL_BLOCK_BUDGET = 18 * MIB


def _matmul_tiles(m, k, n, a_dtype, b_dtype, out_dtype, has_res):
    best = None
    for tm in (2048, 1024, 512, 256):
        if m % tm:
            continue
        for tn in (1024, 768, 1408, 512, 256, 128):
            if n % tn:
                continue
            blk = (_nbytes((tm, k), a_dtype) + _nbytes((k, tn), b_dtype) + _nbytes((tm, tn), out_dtype)
                   + (_nbytes((tm, tn), F32) if has_res else 0))
            if blk > MATMUL_BLOCK_BUDGET:
                continue
            waste = -(-tn // MXU_WIDTH) * MXU_WIDTH / tn
            cost = (m // tm) * (n // tn) + 64 * (waste - 1.0)
            if best is None or cost < best[0]:
                best = (cost, tm, tn, blk)
    assert best is not None, (m, k, n)
    return best[1:]


def _matmul(a, b, *, name, nt=False, res=None, out_dtype=F32):
    m, k = a.shape
    n = b.shape[0] if nt else b.shape[1]
    tm, tn, blk = _matmul_tiles(m, k, n, a.dtype, b.dtype, out_dtype, res is not None)
    dims = (((1,), (1,)), ((), ())) if nt else (((1,), (0,)), ((), ()))

    def body(*refs):
        if res is None:
            a_ref, b_ref, o_ref = refs
        else:
            a_ref, b_ref, r_ref, o_ref = refs
        acc = lax.dot_general(a_ref[...], b_ref[...], dims, preferred_element_type=F32)
        if res is not None:
            acc = acc + r_ref[...]
        o_ref[...] = acc.astype(out_dtype)

    in_specs = [pl.BlockSpec((tm, k), lambda i, j: (i, 0)),
                _spec(b, (tn, k), lambda i, j: (j, 0)) if nt else _spec(b, (k, tn), lambda i, j: (0, j))]
    args = [a, _arr(b)]
    if res is not None:
        in_specs.append(pl.BlockSpec((tm, tn), lambda i, j: (i, j)))
        args.append(res)
    return _pcall(body, name=name, out_shape=_sds((m, n), out_dtype), grid=(m // tm, n // tn), in_specs=in_specs,
                  out_specs=pl.BlockSpec((tm, tn), lambda i, j: (i, j)), semantics=("parallel", "parallel"),
                  block_bytes=blk + _nbytes((tm, tn), F32))(*args)


def _matmul_rms_bwd(a, b, x, g, dres, *, name, nt=False, res=None):
    m, k = a.shape
    n = b.shape[0] if nt else b.shape[1]
    tm = _pick(m, (512, 256))
    dims = (((1,), (1,)), ((), ())) if nt else (((1,), (0,)), ((), ()))

    def body(*refs):
        a_ref, b_ref, x_ref, g_ref, dr_ref = refs[:5]
        dx_ref, dxb_ref, dg_ref = refs[-3:]
        dh = lax.dot_general(a_ref[...], b_ref[...], dims, preferred_element_type=F32)
        if res is not None:
            dh = dh + refs[5][...]
        _, vjp = jax.vjp(_rms, x_ref[...], g_ref[...])
        dxn, dg = vjp(dh)
        dx = dr_ref[...] + dxn
        dx_ref[...] = dx
        dxb_ref[...] = dx.astype(BF16)
        _acc_out(dg_ref, dg, pl.program_id(0) == 0)

    row = pl.BlockSpec((tm, n), lambda i: (i, 0))
    vec = pl.BlockSpec((1, n), lambda i: (0, 0))
    in_specs = [pl.BlockSpec((tm, k), lambda i: (i, 0)),
                _spec(b, (n, k), lambda i: (0, 0)) if nt else _spec(b, (k, n), lambda i: (0, 0)), row, _spec(g), row]
    args = [a, _arr(b), x, _arr(g), dres]
    if res is not None:
        in_specs.append(row)
        args.append(res)
    blk = _nbytes((tm, k), a.dtype) + _nbytes((k, n), b.dtype) + 6 * _nbytes((tm, n), F32)
    return _pcall(body, name=name, out_shape=(_sds((m, n), F32), _sds((m, n), BF16), _sds((1, n), F32)), grid=(m // tm,),
                  in_specs=in_specs, out_specs=(row, row, vec), semantics=("arbitrary",), block_bytes=blk)(*args)


def _matmul_tn(a, b, *, name, out_dtype=BF16, out_rows=None, row_off=0, into=None):
    m, k1 = a.shape
    n = b.shape[1]
    tk = _pick(k1, (512, 256, 128))
    off = row_off // tk
    assert off * tk == row_off

    def body(a_ref, b_ref, *rest):
        rest[-1][...] = lax.dot_general(a_ref[...], b_ref[...], (((0,), (0,)), ((), ())),
                                        preferred_element_type=F32).astype(out_dtype)

    blk = 2 * _nbytes((m, tk), a.dtype) + _nbytes((m, n), b.dtype) + _nbytes((tk, n), F32)
    in_specs = [pl.BlockSpec((m, tk), lambda i: (0, i)), pl.BlockSpec((m, n), lambda i: (0, 0))]
    args = [a, b]
    if into is not None:
        in_specs.append(HBM_SPEC)
        args.append(into)
    return _pcall(body, name=name, out_shape=_sds((out_rows or k1, n), out_dtype), grid=(k1 // tk,), in_specs=in_specs,
                  out_specs=pl.BlockSpec((tk, n), lambda i: (i + off, 0)), semantics=("parallel",), block_bytes=blk,
                  aliases=None if into is None else {2: 0})(*args)


def _rms_fwd(x, g, *, name):
    s, d = x.shape
    tm = _pick(s, (512, 256))

    def body(x_ref, g_ref, o_ref):
        o_ref[...] = _rms(x_ref[...], g_ref[...]).astype(BF16)

    return _pcall(body, name=name, out_shape=_sds((s, d), BF16), grid=(s // tm,),
                  in_specs=[pl.BlockSpec((tm, d), lambda i: (i, 0)), _spec(g)],
                  out_specs=pl.BlockSpec((tm, d), lambda i: (i, 0)), semantics=("parallel",),
                  block_bytes=3 * _nbytes((tm, d), F32))(x, _arr(g))


def _ple_fwd(x, gl, pe, *, name):
    s, d = x.shape
    tm = _pick(s, (512, 256))

    def body(x_ref, gl_ref, pe_ref, o_ref):
        o_ref[...] = x_ref[...] + pe_ref[...] * jax.nn.sigmoid(gl_ref[...])

    row = pl.BlockSpec((tm, d), lambda i: (i, 0))
    return _pcall(body, name=name, out_shape=_sds((s, d), F32), grid=(s // tm,), in_specs=[row, row, row],
                  out_specs=row, semantics=("parallel",), block_bytes=4 * _nbytes((tm, d), F32))(x, gl, pe)


def _ple_bwd(dx, gl, pe, *, name):
    s, d = dx.shape
    tm = _pick(s, (512, 256))

    def body(dx_ref, gl_ref, pe_ref, dpe_ref, dgl_ref):
        gate = jax.nn.sigmoid(gl_ref[...])
        dxv = dx_ref[...]
        dpe_ref[...] = (dxv * gate).astype(BF16)
        dgl_ref[...] = (dxv * pe_ref[...] * gate * (1.0 - gate)).astype(BF16)

    row = pl.BlockSpec((tm, d), lambda i: (i, 0))
    return _pcall(body, name=name, out_shape=(_sds((s, d), BF16), _sds((s, d), BF16)), grid=(s // tm,),
                  in_specs=[row, row, row], out_specs=(row, row), semantics=("parallel",),
                  block_bytes=5 * _nbytes((tm, d), F32))(dx, gl, pe)


def _loss_head(x, g, target, *, name):
    s, d = x.shape
    tm = _pick(s, (256, 128))

    def tile_loss(xv, gv, tv):
        err = jnp.square(_rms(xv, gv) - tv)
        return 0.5 * jnp.sum(jnp.mean(err, axis=-1, keepdims=True), axis=0, keepdims=True)

    def body(x_ref, g_ref, t_ref, l_ref, dx_ref, dg_ref):
        lv, vjp = jax.vjp(tile_loss, x_ref[...], g_ref[...], t_ref[...])
        dxv, dgv, _ = vjp(jnp.ones((1, 1), F32))
        dx_ref[...] = dxv

        @pl.when(pl.program_id(0) == 0)
        def _():
            l_ref[...] = jnp.zeros_like(l_ref)
            dg_ref[...] = jnp.zeros_like(dg_ref)

        l_ref[...] += jnp.broadcast_to(lv, l_ref.shape)
        dg_ref[...] += dgv

    row = pl.BlockSpec((tm, d), lambda i: (i, 0))
    vec = pl.BlockSpec((1, d), lambda i: (0, 0))
    return _pcall(body, name=name, out_shape=(_sds((8, 128), F32), _sds((s, d), F32), _sds((1, d), F32)),
                  grid=(s // tm,), in_specs=[row, vec, row],
                  out_specs=(pl.BlockSpec((8, 128), lambda i: (0, 0)), row, vec), semantics=("arbitrary",),
                  block_bytes=8 * _nbytes((tm, d), F32))(x, g, target)


def _acc_out(ref, val, first):
    @pl.when(first)
    def _():
        ref[...] = jnp.zeros_like(ref)

    ref[...] += val


def _gmlp_fwd(z, ln_g, ln_b, wcat, bfull, *, name):
    s = z.shape[0]
    t = _pick(s, (512, 256, 128))
    nch = t // GMLP_CHUNK

    def body(zu_ref, zv_ref, g_ref, b_ref, w_ref, bf_ref, o_ref):
        for c in range(nch):
            rows = pl.ds(c * GMLP_CHUNK, GMLP_CHUNK)
            o_ref[rows, :] = _gmlp_chunk(zu_ref[rows, :], zv_ref[rows, :], g_ref[...], b_ref[...], w_ref[...],
                                         bf_ref[...]).astype(BF16)

    col = lambda c: pl.BlockSpec((t, W_GRP), lambda i: (i, c))
    params = (ln_g, ln_b, wcat, bfull)
    return _pcall(body, name=name, out_shape=_sds((s, D_MODEL), BF16), grid=(s // t,),
                  in_specs=[col(0), col(1)] + [_spec(a) for a in params],
                  out_specs=pl.BlockSpec((t, W_GRP), lambda i: (i, 0)), semantics=("parallel",),
                  block_bytes=4 * _nbytes((t, W_GRP), F32))(z, z, *[_arr(a) for a in params])


def _gmlp_bwd(z, dmix, ln_g, ln_b, wcat, bfull, *, name):
    s = z.shape[0]
    t = _pick(s, (512, 256, 128))
    nch = t // GMLP_CHUNK

    def body(zu_ref, zv_ref, dy_ref, g_ref, b_ref, w_ref, bf_ref, dz_ref, dg_ref, db_ref, dw_ref, dbf_ref):
        acc = None
        for c in range(nch):
            rows = pl.ds(c * GMLP_CHUNK, GMLP_CHUNK)
            _, vjp = jax.vjp(_gmlp_chunk, zu_ref[rows, :], zv_ref[rows, :], g_ref[...], b_ref[...], w_ref[...],
                             bf_ref[...])
            du, dv, *dps = vjp(dy_ref[rows, :])
            dz_ref[rows, :] = jnp.concatenate([du, dv], axis=1).astype(BF16)
            acc = dps if acc is None else [x + y for x, y in zip(acc, dps)]
        first = pl.program_id(0) == 0
        for ref, val in zip((dg_ref, db_ref, dw_ref, dbf_ref), acc):
            _acc_out(ref, val, first)

    col = lambda c: pl.BlockSpec((t, W_GRP), lambda i: (i, c))
    params = (ln_g, ln_b, wcat, bfull)
    return _pcall(body, name=name,
                  out_shape=(_sds((s, D_PROJ), BF16),) + tuple(_sds(a.shape, F32) for a in params),
                  grid=(s // t,), in_specs=[col(0), col(1), col(0)] + [_spec(a) for a in params],
                  out_specs=(pl.BlockSpec((t, 2 * W_GRP), lambda i: (i, 0)),) + tuple(_ospec(a) for a in params),
                  semantics=("arbitrary",),
                  block_bytes=8 * _nbytes((t, W_GRP), F32))(z, z, dmix, *[_arr(a) for a in params])


def _rglru_fwd(z, prm, mix, *, name):
    s = z.shape[0]
    t = _pick(s, (512, 256, 128))
    nt = s // t

    def body(xb_ref, halo_ref, gb_ref, *rest):
        prm_refs, (y_ref, h0s_ref, h_scr) = rest[:len(prm)], rest[len(prm) + 1:]
        i = pl.program_id(0)

        @pl.when(i == 0)
        def _():
            h_scr[...] = jnp.zeros_like(h_scr)

        halo = jnp.where(i == 0, 0.0, halo_ref[...])
        h0 = h_scr[...]
        y, h_last = _rglru_tile(jnp.concatenate([halo, xb_ref[...]], axis=0), gb_ref[...], h0,
                                *[r[...] for r in prm_refs])
        y_ref[...] = y.astype(BF16)
        h0s_ref[...] = jnp.broadcast_to(h0, h0s_ref.shape)
        h_scr[...] = h_last

    in_specs = [pl.BlockSpec((t, W_GRP), lambda i: (i, 2)),
                pl.BlockSpec((8, W_GRP), lambda i: (jnp.maximum(i * (t // 8) - 1, 0), 2)),
                pl.BlockSpec((t, W_GRP), lambda i: (i, 3))] + [_spec(a) for a in prm] + [HBM_SPEC]
    return _pcall(body, name=name, out_shape=(_sds(mix.shape, BF16), _sds((nt, 8, W_GRP), F32)), grid=(nt,),
                  in_specs=in_specs,
                  out_specs=(pl.BlockSpec((t, W_GRP), lambda i: (i, 1)), pl.BlockSpec((None, 8, W_GRP), lambda i: (i, 0, 0))),
                  scratch_shapes=[pltpu.VMEM((1, W_GRP), F32)], semantics=("arbitrary",),
                  block_bytes=24 * _nbytes((t, W_GRP), F32), aliases={3 + len(prm): 0})(
                      z, z, z, *[_arr(a) for a in prm], mix)


def _rglru_bwd(z, dmix, h0s, prm, dz, *, name):
    s = z.shape[0]
    t = _pick(s, (512, 256, 128))
    nt = s // t
    npm = len(prm)

    def body(xb_ref, halo_ref, gb_ref, dy_ref, h0s_ref, *rest):
        prm_refs = rest[:npm]
        dz_ref = rest[npm + 1]
        dprm_refs = rest[npm + 2:2 * npm + 2]
        dh_scr, dhalo_scr = rest[2 * npm + 2:]
        i = pl.program_id(0)
        r = nt - 1 - i

        @pl.when(i == 0)
        def _():
            dh_scr[...] = jnp.zeros_like(dh_scr)
            dhalo_scr[...] = jnp.zeros_like(dhalo_scr)

        halo = jnp.where(r == 0, 0.0, halo_ref[...])
        h0 = h0s_ref[0:1, :]
        _, vjp = jax.vjp(_rglru_tile, jnp.concatenate([halo, xb_ref[...]], axis=0), gb_ref[...], h0,
                         *[p[...] for p in prm_refs])
        dext, dgb, _dh0, *dps = vjp((dy_ref[...], dh_scr[...]))
        dmain = dext[8:]
        dxb = jnp.concatenate([dmain[:t - 8], dmain[t - 8:] + dhalo_scr[...]], axis=0)
        dz_ref[...] = jnp.concatenate([dxb, dgb], axis=1).astype(BF16)
        dh_scr[...] = _dh0
        dhalo_scr[...] = dext[:8]
        for ref, val in zip(dprm_refs, dps):
            _acc_out(ref, val, i == 0)

    rev = lambda c: pl.BlockSpec((t, W_GRP), lambda i: (nt - 1 - i, c))
    in_specs = [rev(2), pl.BlockSpec((8, W_GRP), lambda i: (jnp.maximum((nt - 1 - i) * (t // 8) - 1, 0), 2)), rev(3),
                rev(1), pl.BlockSpec((None, 8, W_GRP), lambda i: (nt - 1 - i, 0, 0))] + [_spec(a) for a in prm] + [HBM_SPEC]
    return _pcall(body, name=name,
                  out_shape=(_sds(dz.shape, BF16),) + tuple(_sds(a.shape, F32) for a in prm),
                  grid=(nt,), in_specs=in_specs,
                  out_specs=(pl.BlockSpec((t, 2 * W_GRP), lambda i: (nt - 1 - i, 1)),) + tuple(_ospec(a) for a in prm),
                  scratch_shapes=[pltpu.VMEM((1, W_GRP), F32), pltpu.VMEM((8, W_GRP), F32)],
                  semantics=("arbitrary",), block_bytes=40 * _nbytes((t, W_GRP), F32), aliases={5 + npm: 0})(
                      z, z, z, dmix, h0s, *[_arr(a) for a in prm], dz)


def _pool_inv(i, t):
    pos = (_rows_of((t, W_GRP)) + i * t + 1).astype(F32)
    grp = _lanes_of((t, W_GRP)) // HEAD_DIM
    win = jnp.where(grp == 0, float(POOL_WINDOWS[0]), jnp.where(grp == 1, float(POOL_WINDOWS[1]),
                    jnp.where(grp == 2, float(POOL_WINDOWS[2]), float(POOL_WINDOWS[3]))))
    return 1.0 / jnp.minimum(pos, win)


def _pool_fwd(z, wd, scale, mix, *, name):
    s = z.shape[0]
    t = _pick(s, (512, 256, 128))

    def body(x_ref, halo_ref, wd_ref, sc_ref, _, y_ref):
        i = pl.program_id(0)
        halo = jnp.where(i == 0, 0.0, halo_ref[...])
        y = _pool_tile(jnp.concatenate([halo, x_ref[...]], axis=0), _pool_inv(i, t), wd_ref[...], sc_ref[...])
        y_ref[...] = y.astype(BF16)

    in_specs = [pl.BlockSpec((t, W_GRP), lambda i: (i, 8)),
                pl.BlockSpec((16, W_GRP), lambda i: (jnp.maximum(i * (t // 16) - 1, 0), 8)), _spec(wd), _spec(scale),
                HBM_SPEC]
    return _pcall(body, name=name, out_shape=_sds(mix.shape, BF16), grid=(s // t,), in_specs=in_specs,
                  out_specs=pl.BlockSpec((t, W_GRP), lambda i: (i, 3)), semantics=("parallel",),
                  block_bytes=12 * _nbytes((t, W_GRP), F32), aliases={4: 0})(z, z, _arr(wd), _arr(scale), mix)


def _pool_bwd(z, dmix, wd, scale, dz, *, name):
    s = z.shape[0]
    t = _pick(s, (512, 256, 128))
    nt = s // t

    def body(x_ref, halo_ref, dy_ref, wd_ref, sc_ref, _, dx_ref, dwd_ref, dsc_ref, dhalo_scr):
        i = pl.program_id(0)
        r = nt - 1 - i

        @pl.when(i == 0)
        def _():
            dhalo_scr[...] = jnp.zeros_like(dhalo_scr)

        halo = jnp.where(r == 0, 0.0, halo_ref[...])
        inv = _pool_inv(r, t)
        _, vjp = jax.vjp(lambda e, w, sc: _pool_tile(e, inv, w, sc), jnp.concatenate([halo, x_ref[...]], axis=0),
                         wd_ref[...], sc_ref[...])
        dext, dwd, dsc = vjp(dy_ref[...])
        dmain = dext[16:]
        dx = jnp.concatenate([dmain[:t - 16], dmain[t - 16:] + dhalo_scr[...]], axis=0)
        dx_ref[...] = dx.astype(BF16)
        dhalo_scr[...] = dext[:16]
        _acc_out(dwd_ref, dwd, i == 0)
        _acc_out(dsc_ref, dsc, i == 0)

    rev = lambda c: pl.BlockSpec((t, W_GRP), lambda i: (nt - 1 - i, c))
    in_specs = [rev(8), pl.BlockSpec((16, W_GRP), lambda i: (jnp.maximum((nt - 1 - i) * (t // 16) - 1, 0), 8)), rev(3),
                _spec(wd), _spec(scale), HBM_SPEC]
    return _pcall(body, name=name, out_shape=(_sds(dz.shape, BF16), _sds(wd.shape, F32), _sds(scale.shape, F32)),
                  grid=(nt,), in_specs=in_specs, out_specs=(rev(8), _ospec(wd), _ospec(scale)),
                  scratch_shapes=[pltpu.VMEM((16, W_GRP), F32)], semantics=("arbitrary",),
                  block_bytes=20 * _nbytes((t, W_GRP), F32), aliases={5: 0})(z, z, dmix, _arr(wd), _arr(scale), dz)


def _hgrn_fwd(z, lb, ngf, mix, *, name):
    s = z.shape[0]
    c = HGRN_CHUNK
    per = HGRN_STEP_CHUNKS
    ns = s // (c * per)

    def body(q_ref, f_ref, i_ref, g_ref, lb_ref, ng_ref, _, y_ref, sts_ref, st_scr):
        @pl.when(pl.program_id(0) == 0)
        def _():
            st_scr[...] = jnp.zeros_like(st_scr)

        st = st_scr[...]
        for k in range(per):
            rows = pl.ds(k * c, c)
            sts_ref[k] = st
            y, st = _hgrn_chunk(q_ref[rows, :], f_ref[rows, :], i_ref[rows, :], g_ref[rows, :], st, lb_ref[...],
                                ng_ref[...])
            y_ref[rows, :] = y.astype(BF16)
        st_scr[...] = st

    col = lambda k: pl.BlockSpec((per * c, W_GRP), lambda i: (i, k))
    return _pcall(body, name=name, out_shape=(_sds(mix.shape, BF16), _sds((ns * per, W_GRP, W_GRP), F32)), grid=(ns,),
                  in_specs=[col(4), col(5), col(6), col(7), _spec(lb), _spec(ngf), HBM_SPEC],
                  out_specs=(pl.BlockSpec((per * c, W_GRP), lambda i: (i, 2)),
                             pl.BlockSpec((per, W_GRP, W_GRP), lambda i: (i, 0, 0))),
                  scratch_shapes=[pltpu.VMEM((W_GRP, W_GRP), F32)], semantics=("arbitrary",),
                  block_bytes=16 * per * _nbytes((W_GRP, W_GRP), F32), aliases={6: 0})(
                      z, z, z, z, _arr(lb), _arr(ngf), mix)


def _hgrn_bwd(z, dmix, sts, lb, ngf, dz, *, name):
    s = z.shape[0]
    c = HGRN_CHUNK
    per = HGRN_STEP_CHUNKS
    ns = s // (c * per)

    def body(q_ref, f_ref, i_ref, g_ref, dy_ref, st_ref, lb_ref, ng_ref, _, dz_ref, dlb_ref, dng_ref, dst_scr):
        i = pl.program_id(0)

        @pl.when(i == 0)
        def _():
            dst_scr[...] = jnp.zeros_like(dst_scr)

        dst = dst_scr[...]
        dlb_sum = dng_sum = None
        for k in range(per - 1, -1, -1):
            rows = pl.ds(k * c, c)
            _, vjp = jax.vjp(_hgrn_chunk, q_ref[rows, :], f_ref[rows, :], i_ref[rows, :], g_ref[rows, :], st_ref[k],
                             lb_ref[...], ng_ref[...])
            dq, df, di, dg, dst, dlb, dng = vjp((dy_ref[rows, :], dst))
            dz_ref[rows, :] = jnp.concatenate([dq, df, di, dg], axis=1).astype(BF16)
            dlb_sum = dlb if dlb_sum is None else dlb_sum + dlb
            dng_sum = dng if dng_sum is None else dng_sum + dng
        dst_scr[...] = dst
        _acc_out(dlb_ref, dlb_sum, i == 0)
        _acc_out(dng_ref, dng_sum, i == 0)

    rev = lambda k: pl.BlockSpec((per * c, W_GRP), lambda i: (ns - 1 - i, k))
    vec = pl.BlockSpec((1, W_GRP), lambda i: (0, 0))
    return _pcall(body, name=name, out_shape=(_sds(dz.shape, BF16), _sds((1, W_GRP), F32), _sds((1, W_GRP), F32)),
                  grid=(ns,),
                  in_specs=[rev(4), rev(5), rev(6), rev(7), rev(2),
                            pl.BlockSpec((per, W_GRP, W_GRP), lambda i: (ns - 1 - i, 0, 0)), _spec(lb), _spec(ngf),
                            HBM_SPEC],
                  out_specs=(pl.BlockSpec((per * c, 4 * W_GRP), lambda i: (ns - 1 - i, 1)), vec, vec),
                  scratch_shapes=[pltpu.VMEM((W_GRP, W_GRP), F32)], semantics=("arbitrary",),
                  block_bytes=32 * per * _nbytes((W_GRP, W_GRP), F32), aliases={8: 0})(
                      z, z, z, z, dmix, sts, _arr(lb), _arr(ngf), dz)


def _lbs_fwd(c_lb, *, name):
    def body(c_ref, o_ref):
        c = c_ref[...]
        e = jnp.exp(c - jnp.max(c, axis=0, keepdims=True))
        sm = e / jnp.sum(e, axis=0, keepdims=True)
        run = jnp.zeros((1, W_GRP), F32)
        o_ref[0:1, :] = run
        for l in range(1, DEPTH):
            run = run + sm[l:l + 1]
            o_ref[l:l + 1, :] = run

    return _pcall(body, name=name, out_shape=_sds((DEPTH, W_GRP), F32), pin=False)(c_lb)


def _lbs_bwd(c_lb, dlbs, *, name):
    def body(c_ref, d_ref, o_ref):
        c = c_ref[...]
        e = jnp.exp(c - jnp.max(c, axis=0, keepdims=True))
        sm = e / jnp.sum(e, axis=0, keepdims=True)
        d = d_ref[...]
        dsm = [None] * DEPTH
        run = jnp.zeros((1, W_GRP), F32)
        for l in range(DEPTH - 1, 0, -1):
            run = run + d[l:l + 1]
            dsm[l] = run
        dsm[0] = jnp.zeros((1, W_GRP), F32)
        inner = sum(sm[l:l + 1] * dsm[l] for l in range(DEPTH))
        for l in range(DEPTH):
            o_ref[l:l + 1, :] = sm[l:l + 1] * (dsm[l] - inner)

    return _pcall(body, name=name, out_shape=_sds((DEPTH, W_GRP), F32), pin=False)(c_lb, dlbs)


def _ffn_fwd(hg, hv, cwf, cbf, *, name):
    s, n = hg.shape
    t = _pick(s, (256, 128))
    cw = _pick(n, (1408, 256, 128))
    nj = n // cw

    def body(g_ref, gh_ref, v_ref, vh_ref, wg_ref, bg_ref, wv_ref, bv_ref, o_ref):
        first = pl.program_id(1) == 0
        eg = jnp.concatenate([jnp.where(first, 0.0, gh_ref[...]), g_ref[...]], axis=0)
        ev = jnp.concatenate([jnp.where(first, 0.0, vh_ref[...]), v_ref[...]], axis=0)
        o_ref[...] = _ffn_tile(eg, ev, wg_ref[...], bg_ref[...], wv_ref[...], bv_ref[...]).astype(BF16)

    main = pl.BlockSpec((t, cw), lambda j, i: (i, j))
    halo = pl.BlockSpec((8, cw), lambda j, i: (jnp.maximum(i * (t // 8) - 1, 0), j))
    taps = lambda off: _spec(cwf, (3, cw), lambda j, i: (0, j + off))
    bias = lambda off: _spec(cbf, (1, cw), lambda j, i: (0, j + off))
    return _pcall(body, name=name, out_shape=_sds((s, n), BF16), grid=(nj, s // t),
                  in_specs=[main, halo, main, halo, taps(0), bias(0), taps(nj), bias(nj)], out_specs=main,
                  semantics=("parallel", "parallel"), block_bytes=12 * _nbytes((t, cw), F32))(
                      hg, hg, hv, hv, _arr(cwf), _arr(cbf), _arr(cwf), _arr(cbf))


def _ffn_bwd(hg, hv, da, cwf, cbf, *, name):
    s, n = hg.shape
    t = _pick(s, (256, 128))
    cw = _pick(n, (1408, 256, 128))
    nt = s // t
    nj = n // cw

    def body(g_ref, gh_ref, v_ref, vh_ref, da_ref, wg_ref, bg_ref, wv_ref, bv_ref, dg_ref, dv_ref, dwg_ref, dwv_ref,
             cg_scr, cv_scr):
        i = pl.program_id(1)
        r = nt - 1 - i

        @pl.when(i == 0)
        def _():
            cg_scr[...] = jnp.zeros_like(cg_scr)
            cv_scr[...] = jnp.zeros_like(cv_scr)

        eg = jnp.concatenate([jnp.where(r == 0, 0.0, gh_ref[...]), g_ref[...]], axis=0)
        ev = jnp.concatenate([jnp.where(r == 0, 0.0, vh_ref[...]), v_ref[...]], axis=0)
        _, vjp = jax.vjp(_ffn_tile, eg, ev, wg_ref[...], bg_ref[...], wv_ref[...], bv_ref[...])
        deg, dev, dwg, dbg, dwv, dbv = vjp(da_ref[...])
        for dext, scr, ref in ((deg, cg_scr, dg_ref), (dev, cv_scr, dv_ref)):
            dmain = dext[8:]
            ref[...] = jnp.concatenate([dmain[:t - 8], dmain[t - 8:] + scr[...]], axis=0).astype(BF16)
            scr[...] = dext[:8]
        zeros = jnp.zeros((4, cw), F32)
        _acc_out(dwg_ref, jnp.concatenate([dwg, dbg, zeros], axis=0), i == 0)
        _acc_out(dwv_ref, jnp.concatenate([dwv, dbv, zeros], axis=0), i == 0)

    main = pl.BlockSpec((t, cw), lambda j, i: (nt - 1 - i, j))
    halo = pl.BlockSpec((8, cw), lambda j, i: (jnp.maximum((nt - 1 - i) * (t // 8) - 1, 0), j))
    taps = lambda off: _spec(cwf, (3, cw), lambda j, i: (0, j + off))
    bias = lambda off: _spec(cbf, (1, cw), lambda j, i: (0, j + off))
    w8 = pl.BlockSpec((8, cw), lambda j, i: (0, j))
    return _pcall(body, name=name,
                  out_shape=(_sds((s, n), BF16), _sds((s, n), BF16), _sds((8, n), F32), _sds((8, n), F32)),
                  grid=(nj, nt), in_specs=[main, halo, main, halo, main, taps(0), bias(0), taps(nj), bias(nj)],
                  out_specs=(main, main, w8, w8),
                  scratch_shapes=[pltpu.VMEM((8, cw), F32), pltpu.VMEM((8, cw), F32)],
                  semantics=("parallel", "arbitrary"), block_bytes=24 * _nbytes((t, cw), F32))(
                      hg, hg, hv, hv, da, _arr(cwf), _arr(cbf), _arr(cwf), _arr(cbf))


def _all_gather(x, *, name):
    r, c = x.shape

    def body(x_ref, out_ref, send_sems, recv_sems, local_sem):
        mx, my, mc = lax.axis_index("x"), lax.axis_index("y"), lax.axis_index("c")
        me, sibling = (mx, my, mc), (mx, my, 1 - mc)
        chips = [(1 - mx, my), (mx, 1 - my), (1 - mx, 1 - my)]

        def slot(px, py, pc):
            return out_ref.at[4 * px + 2 * py + pc]

        def copy(k, block, to, src=None):
            return pltpu.make_async_remote_copy(src_ref=slot(*block) if src is None else src, dst_ref=slot(*block),
                                                send_sem=send_sems.at[k], recv_sem=recv_sems.at[k],
                                                device_id=to, device_id_type=MESH)

        mine = pltpu.make_async_copy(x_ref, slot(*me), local_sem)
        mine.start()
        first = [copy(0, me, sibling, src=x_ref)]
        first += [copy(1 + j, me, (*chip, mc), src=x_ref) for j, chip in enumerate(chips)]
        for cp in first:
            cp.start()
        passed = [copy(4 + j, (*chip, mc), sibling) for j, chip in enumerate(chips)]
        for j, chip in enumerate(chips):
            copy(1 + j, (*chip, mc), me).wait_recv()
            passed[j].start()
        copy(0, sibling, me).wait_recv()
        for j, chip in enumerate(chips):
            copy(4 + j, (*chip, 1 - mc), me).wait_recv()
        for cp in first + passed:
            cp.wait_send()
        mine.wait()

    hbm = pl.BlockSpec(memory_space=pl.ANY)
    return _pcall(body, name=name, out_shape=_sds((N_DEV, r, c), x.dtype), in_specs=[hbm], out_specs=hbm,
                  scratch_shapes=[pltpu.SemaphoreType.DMA((7,)), pltpu.SemaphoreType.DMA((7,)),
                                  pltpu.SemaphoreType.DMA(())])(x)


def _sum_slots(p, *, name):
    q, r, c = p.shape
    tr = _pick(r, (544, 408, 272, 192, 136, 64, 32, 16, 8))

    def body(p_ref, o_ref):
        acc = p_ref[0].astype(F32)
        for k in range(1, q):
            acc = acc + p_ref[k].astype(F32)
        o_ref[...] = acc

    return _pcall(body, name=name, out_shape=_sds((r, c), F32), grid=(r // tr,),
                  in_specs=[pl.BlockSpec((q, tr, c), lambda i: (0, i, 0))],
                  out_specs=pl.BlockSpec((tr, c), lambda i: (i, 0)), semantics=("parallel",),
                  block_bytes=(q + 2) * _nbytes((tr, c), F32))(p)


BIG_COMM = (('w_in', 288, D_MODEL), ('w_out', 128, D_MODEL), ('w_up', 704, D_MODEL), ('w_down', 352, D_MODEL),
            ('w_pe', 128, PLE_DIM), ('w_pg', 128, D_MODEL))
HBM_SPEC = pl.BlockSpec(memory_space=pl.ANY)


def _gather_layer(shards, l, *, name):
    na = len(shards)

    def body(*refs):
        x_refs, out_refs = refs[:na], refs[na:2 * na]
        send_sems, recv_sems, local_sems = refs[2 * na:]
        mx, my, mc = lax.axis_index("x"), lax.axis_index("y"), lax.axis_index("c")
        me, sibling = (mx, my, mc), (mx, my, 1 - mc)
        chips = [(1 - mx, my), (mx, 1 - my), (1 - mx, 1 - my)]

        def slot(a, px, py, pc):
            return out_refs[a].at[4 * px + 2 * py + pc]

        def copy(k, a, block, to, own=False):
            return pltpu.make_async_remote_copy(src_ref=x_refs[a].at[l] if own else slot(a, *block),
                                                dst_ref=slot(a, *block), send_sem=send_sems.at[k, a],
                                                recv_sem=recv_sems.at[k, a], device_id=to, device_id_type=MESH)

        mine = [pltpu.make_async_copy(x_refs[a].at[l], slot(a, *me), local_sems.at[a]) for a in range(na)]
        for cp in mine:
            cp.start()
        first = []
        for a in range(na):
            first.append(copy(0, a, me, sibling, own=True))
            first += [copy(1 + j, a, me, (*chip, mc), own=True) for j, chip in enumerate(chips)]
        for cp in first:
            cp.start()
        passed = []
        for j, chip in enumerate(chips):
            for a in range(na):
                copy(1 + j, a, (*chip, mc), me).wait_recv()
                fwd = copy(4 + j, a, (*chip, mc), sibling)
                fwd.start()
                passed.append(fwd)
        for a in range(na):
            copy(0, a, sibling, me).wait_recv()
        for j, chip in enumerate(chips):
            for a in range(na):
                copy(4 + j, a, (*chip, 1 - mc), me).wait_recv()
        for cp in first + passed:
            cp.wait_send()
        for cp in mine:
            cp.wait()

    return _pcall(body, name=name, out_shape=tuple(_sds((N_DEV,) + x.shape[1:], x.dtype) for x in shards),
                  in_specs=[HBM_SPEC] * na, out_specs=(HBM_SPEC,) * na,
                  scratch_shapes=[pltpu.SemaphoreType.DMA((7, na)), pltpu.SemaphoreType.DMA((7, na)),
                                  pltpu.SemaphoreType.DMA((na,))])(*shards)


SEM_SPEC = pl.BlockSpec(memory_space=pltpu.SEMAPHORE)
DATAFLOW_EFFECT = pltpu.SideEffectType.DATAFLOW_SIDE_EFFECTING


def _place_own(srcs, after, *, name):
    na = len(srcs)

    def body(*refs):
        x_refs, land_refs, sems = refs[:na], refs[na + len(after):2 * na + len(after)], refs[-1]
        me = 4 * lax.axis_index("x") + 2 * lax.axis_index("y") + lax.axis_index("c")
        cps = [pltpu.make_async_copy(x_refs[a], land_refs[a].at[me], sems.at[a]) for a in range(na)]
        for cp in cps:
            cp.start()
        for cp in cps:
            cp.wait()

    return _pcall(body, name=name, out_shape=tuple(_sds((N_DEV,) + x.shape, x.dtype) for x in srcs),
                  in_specs=[HBM_SPEC] * (na + len(after)), out_specs=(HBM_SPEC,) * na,
                  scratch_shapes=[pltpu.SemaphoreType.DMA((na,))], pin=False)(*srcs, *after)


def _exchange_start(srcs, lands, *, name, per_peer=False):
    na = len(srcs)

    def body(*refs):
        x_refs, land_refs = refs[:na], refs[na:2 * na]
        send_sems, recv_sems = refs[2 * na], refs[2 * na + 1]
        token = refs[-1]
        mx, my, mc = lax.axis_index("x"), lax.axis_index("y"), lax.axis_index("c")
        me = 4 * mx + 2 * my + mc
        peers = [(mx, my, 1 - mc)]
        for px, py in ((1 - mx, my), (mx, 1 - my), (1 - mx, 1 - my)):
            peers += [(px, py, mc), (px, py, 1 - mc)]
        for a in range(na):
            for peer in peers:
                src = x_refs[a].at[4 * peer[0] + 2 * peer[1] + peer[2]] if per_peer else x_refs[a]
                pltpu.make_async_remote_copy(src_ref=src, dst_ref=land_refs[a].at[me], send_sem=send_sems.at[a],
                                             recv_sem=recv_sems.at[a], device_id=peer, device_id_type=MESH).start()
        token[...] = jnp.zeros_like(token)

    hbm = lambda x: pltpu.HBM(x.shape, x.dtype)
    out_shape = ((pltpu.SemaphoreType.DMA((na,)), pltpu.SemaphoreType.DMA((na,))) + tuple(hbm(x) for x in srcs)
                 + tuple(hbm(x) for x in lands) + (_sds((8, 128), F32),))
    params = pltpu.CompilerParams(has_side_effects=DATAFLOW_EFFECT)
    pin = lambda x: pltpu.with_memory_space_constraint(x, pltpu.HBM)
    return pl.pallas_call(body, name=name, out_shape=out_shape, in_specs=[HBM_SPEC] * (2 * na),
                          out_specs=(SEM_SPEC, SEM_SPEC) + (HBM_SPEC,) * (2 * na) + (pl.BlockSpec(memory_space=pltpu.VMEM),),
                          input_output_aliases={i: 2 + i for i in range(2 * na)}, compiler_params=params)(
                              *[pin(x) for x in srcs], *[pin(x) for x in lands])


def _exchange_wait(started, after, *, name):
    send_sems, recv_sems, *bufs, _ = started
    na = len(bufs) // 2

    def body(*refs):
        land_refs = refs[na:2 * na]
        s_sems, r_sems = refs[2 * na], refs[2 * na + 1]
        me = (lax.axis_index("x"), lax.axis_index("y"), lax.axis_index("c"))
        for a in range(na):
            seven = land_refs[a].at[pl.ds(0, N_DEV - 1)]
            cp = pltpu.make_async_remote_copy(src_ref=seven, dst_ref=seven, send_sem=s_sems.at[a], recv_sem=r_sems.at[a],
                                              device_id=me, device_id_type=MESH)
            cp.wait_send()
            cp.wait_recv()

    hbm = lambda x: pltpu.HBM(x.shape, x.dtype)
    params = pltpu.CompilerParams(has_side_effects=DATAFLOW_EFFECT)
    outs = pl.pallas_call(body, name=name, out_shape=tuple(hbm(x) for x in bufs),
                          in_specs=[HBM_SPEC] * (2 * na) + [SEM_SPEC, SEM_SPEC, HBM_SPEC],
                          out_specs=(HBM_SPEC,) * (2 * na), input_output_aliases={i: i for i in range(2 * na)},
                          compiler_params=params)(*bufs, send_sems, recv_sems, after)
    return outs[:na], outs[na:]


def _pair_swap(grads, *, name):
    na = len(grads)

    def body(*refs):
        g_refs, recv_refs = refs[:na], refs[na:2 * na]
        send_sems, recv_sems = refs[2 * na:]
        mx, my, mc = lax.axis_index("x"), lax.axis_index("y"), lax.axis_index("c")
        sibling = (mx, my, 1 - mc)
        for a in range(na):
            for q in range(4):
                pltpu.make_async_remote_copy(src_ref=g_refs[a].at[q, 1 - mc], dst_ref=recv_refs[a].at[q],
                                             send_sem=send_sems.at[a], recv_sem=recv_sems.at[a],
                                             device_id=sibling, device_id_type=MESH).start()
        for a in range(na):
            pltpu.make_async_remote_copy(src_ref=recv_refs[a], dst_ref=recv_refs[a], send_sem=send_sems.at[a],
                                         recv_sem=recv_sems.at[a], device_id=sibling, device_id_type=MESH).wait()

    half = tuple(_sds((4,) + g.shape[2:], g.dtype) for g in grads)
    return _pcall(body, name=name, out_shape=half, in_specs=[HBM_SPEC] * na, out_specs=(HBM_SPEC,) * na,
                  scratch_shapes=[pltpu.SemaphoreType.DMA((na,)), pltpu.SemaphoreType.DMA((na,))])(*grads)


def _add_slabs(grads, recv, core, *, name):
    na = len(grads)

    def body(core_ref, *refs):
        for a in range(na):
            refs[2 * na + a][...] = (refs[a][...].astype(F32) + refs[na + a][...].astype(F32)).astype(BF16)

    own_specs = [pl.BlockSpec((None, None) + x.shape[2:], lambda q, core_ref: (q, core_ref[0], 0, 0)) for x in grads]
    specs = [pl.BlockSpec((None,) + x.shape[1:], lambda q, core_ref: (q, 0, 0)) for x in recv]
    blk = sum(_nbytes(x.shape[1:], F32) for x in recv)
    grid_spec = pltpu.PrefetchScalarGridSpec(num_scalar_prefetch=1, grid=(4,), in_specs=own_specs + specs,
                                             out_specs=tuple(specs))
    params = pltpu.CompilerParams(dimension_semantics=("parallel",), vmem_limit_bytes=_vmem_limit(2 * blk))
    return pl.pallas_call(body, name=name, out_shape=tuple(_sds(x.shape, BF16) for x in recv), grid_spec=grid_spec,
                          compiler_params=params)(core, *grads, *recv)


def _chip_exchange(parts, *, name):
    na = len(parts)

    def body(*refs):
        p_refs, out_refs = refs[:na], refs[na:2 * na]
        send_sems, recv_sems, local_sems = refs[2 * na:]
        mx, my, mc = lax.axis_index("x"), lax.axis_index("y"), lax.axis_index("c")
        mine_q = 2 * mx + my
        chips = [(1 - mx, my), (mx, 1 - my), (1 - mx, 1 - my)]
        owns = [pltpu.make_async_copy(p_refs[a].at[mine_q], out_refs[a].at[mine_q], local_sems.at[a]) for a in range(na)]
        for cp in owns:
            cp.start()
        sends = []
        for a in range(na):
            for k, chip in enumerate(chips):
                sends.append(pltpu.make_async_remote_copy(
                    src_ref=p_refs[a].at[2 * chip[0] + chip[1]], dst_ref=out_refs[a].at[mine_q],
                    send_sem=send_sems.at[k, a], recv_sem=recv_sems.at[k, a], device_id=(*chip, mc), device_id_type=MESH))
        for cp in sends:
            cp.start()
        for a in range(na):
            for k, chip in enumerate(chips):
                pltpu.make_async_remote_copy(
                    src_ref=p_refs[a].at[mine_q], dst_ref=out_refs[a].at[2 * chip[0] + chip[1]],
                    send_sem=send_sems.at[k, a], recv_sem=recv_sems.at[k, a], device_id=(*chip, mc),
                    device_id_type=MESH).wait_recv()
        for cp in sends:
            cp.wait_send()
        for cp in owns:
            cp.wait()

    return _pcall(body, name=name, out_shape=tuple(_sds(x.shape, x.dtype) for x in parts), in_specs=[HBM_SPEC] * na,
                  out_specs=(HBM_SPEC,) * na,
                  scratch_shapes=[pltpu.SemaphoreType.DMA((3, na)), pltpu.SemaphoreType.DMA((3, na)),
                                  pltpu.SemaphoreType.DMA((na,))])(*parts)


def _sum_chips(parts, *, name):
    na = len(parts)

    def body(*refs):
        for a in range(na):
            p_ref = refs[a]
            acc = p_ref[0].astype(F32)
            for k in range(1, p_ref.shape[0]):
                acc = acc + p_ref[k].astype(F32)
            refs[na + a][...] = acc

    half = lambda x: x.shape[1] // 2
    in_specs = [pl.BlockSpec((x.shape[0], half(x), x.shape[2]), lambda i: (0, i, 0)) for x in parts]
    out_specs = tuple(pl.BlockSpec((half(x), x.shape[2]), lambda i: (i, 0)) for x in parts)
    blk = sum(_nbytes((x.shape[0] + 2, half(x), x.shape[2]), BF16) for x in parts)
    return _pcall(body, name=name, out_shape=tuple(_sds(x.shape[1:], F32) for x in parts), grid=(2,),
                  in_specs=in_specs, out_specs=out_specs, semantics=("parallel",), block_bytes=blk)(*parts)


def _sum_devices(lands, own, me, *, name):
    na = len(lands)

    def body(me_ref, *refs):
        mine = me_ref[0]
        for a in range(na):
            l_ref, o_ref = refs[a], refs[na + a]
            acc = None
            for k in range(N_DEV):
                term = jnp.where(mine == k, o_ref[...], l_ref[k]).astype(F32)
                acc = term if acc is None else acc + term
            refs[2 * na + a][...] = acc

    half = lambda x: x.shape[1] // 2
    land_specs = [pl.BlockSpec((N_DEV, half(x), x.shape[2]), lambda i, me_ref: (0, i, 0)) for x in lands]
    own_specs = [pl.BlockSpec((None, half(x), x.shape[2]), lambda i, me_ref: (me_ref[0], i, 0)) for x in lands]
    out_specs = tuple(pl.BlockSpec((half(x), x.shape[2]), lambda i, me_ref: (i, 0)) for x in lands)
    blk = sum(_nbytes((N_DEV + 3, half(x), x.shape[2]), BF16) for x in lands)
    grid_spec = pltpu.PrefetchScalarGridSpec(num_scalar_prefetch=1, grid=(2,), in_specs=land_specs + own_specs,
                                             out_specs=out_specs)
    params = pltpu.CompilerParams(dimension_semantics=("parallel",), vmem_limit_bytes=_vmem_limit(blk))
    return pl.pallas_call(body, name=name, out_shape=tuple(_sds(x.shape[1:], F32) for x in lands), grid_spec=grid_spec,
                          compiler_params=params)(me, *lands, *own)


def _reduce_layer(grads, l):
    n = lambda s: f"l{l}_{s}"
    views = [g.reshape(4, 2, g.shape[0] // N_DEV, g.shape[1]) for g in grads]
    recv = _pair_swap(views, name=n("reduce_pair"))
    core = lax.axis_index("c").astype(jnp.int32).reshape(1)
    chip_sum = _add_slabs(views, recv, core, name=n("reduce_pair_add"))
    from_chips = _chip_exchange(chip_sum, name=n("reduce_chips"))
    return _sum_chips(from_chips, name=n("reduce_chips_add"))


def _adamw(w, g, m, v, *, name):
    lead, (r, c) = w.shape[:-2], w.shape[-2:]
    tr = _pick(r, (512, 352, 288, 256, 192, 128, 64, 32, 16, 8))
    c1 = 1.0 / (1.0 - ADAM_B1 ** ADAM_STEP)
    c2 = 1.0 / (1.0 - ADAM_B2 ** ADAM_STEP)

    def body(w_ref, g_ref, m_ref, v_ref, d_ref, nm_ref, nv_ref):
        gv = g_ref[...]
        nm = ADAM_B1 * m_ref[...] + (1.0 - ADAM_B1) * gv
        nv = ADAM_B2 * v_ref[...] + (1.0 - ADAM_B2) * jnp.square(gv)
        d_ref[...] = -ADAM_LR * ((nm * c1) / (jnp.sqrt(nv * c2) + ADAM_EPS) + ADAM_WD * w_ref[...])
        nm_ref[...] = nm
        nv_ref[...] = nv

    if lead:
        blk = pl.BlockSpec((None, tr, c), lambda k, i: (k, i, 0))
        grid, sem = (lead[0], r // tr), ("parallel", "parallel")
    else:
        blk = pl.BlockSpec((tr, c), lambda i: (i, 0))
        grid, sem = (r // tr,), ("parallel",)
    out = _sds(w.shape, F32)
    return _pcall(body, name=name, out_shape=(out, out, out), grid=grid, in_specs=[blk] * 4,
                  out_specs=(blk, blk, blk), semantics=sem, block_bytes=7 * _nbytes((tr, c), F32))(w, g, m, v)


def _pack_flat(arrs, rows, cols=1024):
    flat = jnp.concatenate([a.reshape(-1).astype(F32) for a in arrs])
    pad = rows * cols - flat.shape[0]
    return jnp.pad(flat, (0, pad)).reshape(rows, cols)


def _unpack_flat(buf, shapes):
    flat = buf.reshape(-1)
    out, off = [], 0
    for shp in shapes:
        n = 1
        for s in shp:
            n *= s
        out.append(flat[off:off + n].reshape(shp))
        off += n
    return out


def _flat_rows(shapes, cols=1024):
    n = sum(functools.reduce(lambda a, b: a * b, shp, 1) for shp in shapes)
    rows = -(-n // cols)
    return -(-rows // 64) * 64


def _block_diag(w):
    eye = jnp.eye(N_HEADS, dtype=w.dtype)
    return (w[:, :, :, None, :] * eye[None, :, None, :, None]).reshape(w.shape[0], W_GRP, W_GRP)


def _diag_blocks(w):
    w5 = w.reshape(w.shape[0], N_HEADS, HEAD_DIM, N_HEADS, HEAD_DIM)
    return jnp.stack([w5[:, h, :, h, :] for h in range(N_HEADS)], axis=1)


def _stacked_params(w, lbs):
    tril = jnp.tril(jnp.ones((GMLP_CHUNK, GMLP_CHUNK), bool))
    row = lambda a: a.reshape(DEPTH, 1, -1)
    return dict(
        g1=row(w['norm1_g']), g2=row(w['norm2_g']), g3=row(w['norm3_g']),
        a_ln_g=row(w['a_ln_g']), a_ln_b=row(w['a_ln_b']),
        a_wcat=jnp.where(tril, w['a_ws'], 0.0).reshape(DEPTH, N_HEADS * GMLP_CHUNK, GMLP_CHUNK),
        a_bfull=jnp.repeat(jnp.swapaxes(w['a_bs'], 1, 2), HEAD_DIM, axis=2),
        b_cw=w['b_conv_w_full'], b_cb=row(w['b_conv_b']), b_wa=_block_diag(w['b_wa']), b_ba=row(w['b_ba']),
        b_wx=_block_diag(w['b_wx']), b_bx=row(w['b_bx']), b_lam=row(w['b_lam']),
        c_lb=row(lbs), c_ngf=row(jnp.tile(w['c_norm_g'], (1, N_HEADS))),
        d_wd=_block_diag(w['d_w']), d_scale=row(w['d_scale']),
        f_cw=w['ffn_conv_w_full'], f_cb=row(w['ffn_conv_b']),
    )


B_PRM = ('b_cw', 'b_cb', 'b_wa', 'b_ba', 'b_wx', 'b_bx', 'b_lam')


def _layer_fwd(x, p_bf, wb, sp, l):
    n = lambda s: f"l{l}_{s}"
    h = _rms_fwd(x, sp['g1'], name=n("norm1"))
    z = _matmul(h, wb['w_in'], nt=True, name=n("proj_in"))
    mix = _gmlp_fwd(z, sp['a_ln_g'], sp['a_ln_b'], sp['a_wcat'], sp['a_bfull'], name=n("gmlp"))
    mix, h0s = _rglru_fwd(z, [sp[k] for k in B_PRM], mix, name=n("rglru"))
    mix, sts = _hgrn_fwd(z, sp['c_lb'], sp['c_ngf'], mix, name=n("hgrn"))
    mix = _pool_fwd(z, sp['d_wd'], sp['d_scale'], mix, name=n("pool"))
    x1 = _matmul(mix, wb['w_out'], res=x, name=n("proj_out"))
    h2 = _rms_fwd(x1, sp['g2'], name=n("norm2"))
    hg = _matmul(h2, wb['w_up_g'], nt=True, name=n("up_gate"))
    hv = _matmul(h2, wb['w_up_v'], nt=True, name=n("up_val"))
    a = _ffn_fwd(hg, hv, sp['f_cw'], sp['f_cb'], name=n("ffn_gate"))
    x2 = _matmul(a, wb['w_down'], res=x1, name=n("down"))
    h3 = _rms_fwd(x2, sp['g3'], name=n("norm3"))
    gl = _matmul(h3, wb['w_pg'], name=n("ple_gate"))
    pe = _matmul(p_bf, wb['w_pe'], nt=True, name=n("ple_emb"))
    x3 = _ple_fwd(x2, gl, pe, name=n("ple"))
    saved = dict(x=x, h=h, z=z, h0s=h0s, sts=sts, mix=mix, x1=x1, h2=h2, hg=hg, hv=hv, a=a, x2=x2, h3=h3, gl=gl, pe=pe)
    return x3, saved


def _layer_bwd(dx3, sv, p_bf, wb, sp, l, mid=None):
    n = lambda s: f"l{l}_{s}_bwd"
    gb, gs = {}, {}
    dpe, dgl = _ple_bwd(dx3, sv['gl'], sv['pe'], name=n("ple"))
    gb['w_pe'] = _matmul_tn(dpe, p_bf, name=n("ple_emb_w"))
    gb['w_pg'] = _matmul_tn(sv['h3'], dgl, name=n("ple_gate_w"))
    dx2, dx2b, gs['norm3_g'] = _matmul_rms_bwd(dgl, wb['w_pg'], sv['x2'], sp['g3'], dx3, nt=True, name=n("ple_gate_x"))
    da = _matmul(dx2b, wb['w_down'], nt=True, name=n("down_x"))
    gb['w_down'] = _matmul_tn(sv['a'], dx2b, name=n("down_w"))
    dhg, dhv, gs['f_dwg'], gs['f_dwv'] = _ffn_bwd(sv['hg'], sv['hv'], da, sp['f_cw'], sp['f_cb'], name=n("ffn_gate"))
    gate_rows = _matmul_tn(dhg, sv['h2'], name=n("up_gate_w"), out_rows=2 * D_FF)
    gb['w_up'] = _matmul_tn(dhv, sv['h2'], name=n("up_val_w"), out_rows=2 * D_FF, row_off=D_FF, into=gate_rows)
    if mid is not None:
        sp = mid(gb, sp)
    dh2 = _matmul(dhg, wb['w_up_g'], name=n("up_gate_x"))
    dx1, dx1b, gs['norm2_g'] = _matmul_rms_bwd(dhv, wb['w_up_v'], sv['x1'], sp['g2'], dx2, res=dh2, name=n("up_val_x"))
    dmix = _matmul(dx1b, wb['w_out'], nt=True, name=n("proj_out_x"))
    gb['w_out'] = _matmul_tn(sv['mix'], dx1b, name=n("proj_out_w"))
    z = sv['z']
    dz, gs['a_ln_g'], gs['a_ln_b'], gs['a_wcat'], gs['a_bfull'] = _gmlp_bwd(
        z, dmix, sp['a_ln_g'], sp['a_ln_b'], sp['a_wcat'], sp['a_bfull'], name=n("gmlp"))
    dz, *dbp = _rglru_bwd(z, dmix, sv['h0s'], [sp[k] for k in B_PRM], dz, name=n("rglru"))
    gs.update(zip(B_PRM, dbp))
    dz, gs['c_lb'], gs['c_ngf'] = _hgrn_bwd(z, dmix, sv['sts'], sp['c_lb'], sp['c_ngf'], dz, name=n("hgrn"))
    dz, gs['d_wd'], gs['d_scale'] = _pool_bwd(z, dmix, sp['d_wd'], sp['d_scale'], dz, name=n("pool"))
    gb['w_in'] = _matmul_tn(dz, sv['h'], name=n("proj_in_w"))
    dx0, _, gs['norm1_g'] = _matmul_rms_bwd(dz, wb['w_in'], sv['x'], sp['g1'], dx1, name=n("proj_in_x"))
    return dx0, gb, gs


SMALL_NAMES = [nm for nm in WEIGHT_NAMES if nm not in BIG_NAMES]
COL_SHARDED = ('w_in', 'w_up', 'w_pe')


def _comm_shards(w):
    return [(jnp.swapaxes(w[nm], 1, 2) if nm in COL_SHARDED else w[nm]).astype(BF16) for nm, _, _ in BIG_COMM]


def _full_weights(gathered):
    out = {nm: g.reshape(N_DEV * r, c) for g, (nm, r, c) in zip(gathered, BIG_COMM)}
    halves = out.pop('w_up').reshape(2, D_FF, D_MODEL)
    out['w_up_g'], out['w_up_v'] = _Sel(halves, 0), _Sel(halves, 1)
    return out


def _small_grads(raw):
    nl = len(raw)
    st = {k: jnp.stack([r[k] for r in raw]) for k in raw[0]}
    tril = jnp.tril(jnp.ones((GMLP_CHUNK, GMLP_CHUNK), bool))
    vec = lambda a: a.reshape(nl, -1)
    out = {nm: vec(st[k]) for nm, k in (('norm1_g', 'norm1_g'), ('norm2_g', 'norm2_g'), ('norm3_g', 'norm3_g'),
                                        ('a_ln_g', 'a_ln_g'), ('a_ln_b', 'a_ln_b'), ('b_conv_b', 'b_cb'),
                                        ('b_ba', 'b_ba'), ('b_bx', 'b_bx'), ('b_lam', 'b_lam'), ('c_lb', 'c_lb'),
                                        ('d_scale', 'd_scale'))}
    out['a_ws'] = jnp.where(tril, st['a_wcat'].reshape(nl, N_HEADS, GMLP_CHUNK, GMLP_CHUNK), 0.0)
    out['a_bs'] = jnp.swapaxes(st['a_bfull'].reshape(nl, GMLP_CHUNK, N_HEADS, HEAD_DIM).sum(-1), 1, 2)
    out['b_conv_w'] = st['b_cw']
    out['b_wa'], out['b_wx'], out['d_w'] = _diag_blocks(st['b_wa']), _diag_blocks(st['b_wx']), _diag_blocks(st['d_wd'])
    out['c_norm_g'] = st['c_ngf'].reshape(nl, N_HEADS, HEAD_DIM).sum(1)
    out['ffn_conv_w'] = jnp.concatenate([st['f_dwg'][:, 0:3], st['f_dwv'][:, 0:3]], axis=2)
    out['ffn_conv_b'] = jnp.concatenate([st['f_dwg'][:, 3], st['f_dwv'][:, 3]], axis=1)
    return out


def _step(w, m, v, x, p, target):
    s = x.shape[1]
    dev = 4 * lax.axis_index("x") + 2 * lax.axis_index("y") + lax.axis_index("c")
    xs = x.reshape(s, D_MODEL)

    shards = _comm_shards(w)
    conv_shapes = [w['b_conv_w'].shape, w['ffn_conv_w'].shape]
    conv_rows = _flat_rows(conv_shapes)
    conv_all = _all_gather(_pack_flat([w['b_conv_w'], w['ffn_conv_w']], conv_rows), name="gather_conv_weights")
    parts = [_unpack_flat(conv_all[d], conv_shapes) for d in range(N_DEV)]
    wf = dict(w)
    wf['b_conv_w_full'] = jnp.concatenate([pt[0] for pt in parts], axis=-1)
    wf['ffn_conv_w_full'] = jnp.concatenate([pt[1] for pt in parts], axis=-1)
    lbs = _lbs_fwd(w['c_lb'], name="hgrn_bounds")

    stacked = _stacked_params(wf, lbs)
    p_all = p.reshape(DEPTH, s, PLE_DIM).astype(BF16)
    xl, saved, wbs, sps = xs, [], [], []
    gathered = _gather_layer(shards, 0, name="l0_gather_weights")
    for l in range(DEPTH):
        sp = {k: _Sel(a, l) for k, a in stacked.items()}
        if l + 1 < DEPTH:
            own = [x[l + 1] for x in shards]
            after = [conv_all, *gathered] if l == 0 else [xl]
            lands = _place_own(own, after, name=f"l{l + 1}_gather_place")
            started = _exchange_start(own, lands, name=f"l{l + 1}_gather_start")
            sp['g1'] = stacked['g1'][l] + started[-1][0, 0]
        wb = _full_weights(gathered)
        p_bf = p_all[l]
        xl, sv = _layer_fwd(xl, p_bf, wb, sp, l)
        if l + 1 < DEPTH:
            gathered = _exchange_wait(started, xl, name=f"l{l + 1}_gather_wait")[1]
        saved.append((sv, p_bf))
        wbs.append(wb)
        sps.append(sp)
    loss_part, dx, dfinal = _loss_head(xl, w['final_g'].reshape(1, D_MODEL), target.reshape(s, D_MODEL), name="loss_head")
    loss = lax.psum(loss_part[0, 0], ("x", "y", "c"))

    dev1 = dev.astype(jnp.int32).reshape(1)
    names = [nm for nm, _, _ in BIG_COMM]

    def start_reduce(grads, name):
        views = [g.reshape(N_DEV, g.shape[0] // N_DEV, g.shape[1]) for g in grads]
        return _exchange_start(views, [lax.empty(g.shape, g.dtype) for g in views], name=name, per_peer=True)

    def finish_reduce(started, after, lname):
        own, lands = _exchange_wait(started, after, name=f"{lname}_reduce_wait")
        return _sum_devices(lands, own, dev1, name=f"{lname}_reduce_sum")

    reduced, small = [None] * DEPTH, [None] * DEPTH
    pending = None
    for l in range(DEPTH - 1, 0, -1):
        sv, p_bf = saved[l]
        sp = sps[l]
        if pending is not None:
            sp = dict(sp, g3=stacked['g3'][l] + pending[-1][0, 0])
        dx, gb, small[l] = _layer_bwd(dx, sv, p_bf, wbs[l], sp, l)
        if pending is not None:
            reduced[l + 1] = finish_reduce(pending, dx, f"l{l + 1}")
        pending = start_reduce([gb[nm] for nm in names], f"l{l}_reduce_start")
    early = ('w_up', 'w_down', 'w_pe', 'w_pg')
    mid_started = []

    def mid(gb, sp):
        mid_started.append(start_reduce([gb[nm] for nm in early], "l0_reduce_start"))
        return dict(sp, g2=stacked['g2'][0] + mid_started[0][-1][0, 0])

    upper_names = [nm for nm in SMALL_NAMES if nm != 'final_g']
    low_names = upper_names + ['final_g']
    upper = _small_grads(small[1:])
    upper_shapes = [upper[nm].shape for nm in upper_names]
    upper_packed = [_pack_flat([upper[nm] for nm in upper_names], _flat_rows(upper_shapes))]
    upper_started = _exchange_start(upper_packed, _place_own(upper_packed, [], name="upper_small_grads_place"),
                                    name="upper_small_grads_start")

    sv, p_bf = saved[0]
    g3 = stacked['g3'][0] + pending[-1][0, 0] + upper_started[-1][0, 0]
    dx, gb, small[0] = _layer_bwd(dx, sv, p_bf, wbs[0], dict(sps[0], g3=g3), 0, mid=mid)
    reduced[1] = finish_reduce(pending, dx, "l1")
    late = dict(zip(('w_in', 'w_out'), _reduce_layer([gb['w_in'], gb['w_out']], 0)))
    late.update(zip(early, finish_reduce(mid_started[0], late['w_in'], "l0")))
    reduced[0] = [late[nm] for nm in names]
    grad_x = dx.reshape(1, s, D_MODEL)
    low = _small_grads(small[:1])
    low['final_g'] = dfinal.reshape(D_MODEL)
    low_shapes = [low[nm].shape for nm in low_names]
    low_all = _all_gather(_pack_flat([low[nm] for nm in low_names], _flat_rows(low_shapes)), name="gather_small_grads")
    low_sum = dict(zip(low_names, _unpack_flat(_sum_slots(low_all, name="sum_small_grads"), low_shapes)))
    upper_all = _exchange_wait(upper_started, low_all, name="upper_small_grads_wait")[1][0]
    upper_sum = dict(zip(upper_names, _unpack_flat(_sum_slots(upper_all, name="sum_upper_small_grads"), upper_shapes)))
    gsmall = {nm: jnp.concatenate([low_sum[nm], upper_sum[nm]], axis=0) for nm in upper_names}
    gsmall['c_lb'] = _lbs_bwd(w['c_lb'], gsmall['c_lb'], name="hgrn_bounds_bwd")
    gsmall['final_g'] = low_sum['final_g']
    for nm in ('b_conv_w', 'ffn_conv_w'):
        width = w[nm].shape[-1]
        gsmall[nm] = lax.dynamic_slice_in_dim(gsmall[nm], dev * width, width, axis=2)

    grads, delta, new_m, new_v = {}, {}, {}, {}
    for a, (nm, _, _) in enumerate(BIG_COMM):
        t = (lambda x: jnp.swapaxes(x, 1, 2)) if nm in COL_SHARDED else (lambda x: x)
        g = jnp.stack([reduced[l][a] for l in range(DEPTH)])
        d, nm_, nv_ = _adamw(t(w[nm]), g, t(m[nm]), t(v[nm]), name=f"adamw_{nm}")
        grads[nm], delta[nm], new_m[nm], new_v[nm] = t(g), t(d), t(nm_), t(nv_)

    shapes = [w[nm].shape for nm in SMALL_NAMES]
    rows = _flat_rows(shapes)
    pk = lambda t: _pack_flat([t[nm] for nm in SMALL_NAMES], rows)
    d, nm_, nv_ = _adamw(pk(w), pk(gsmall), pk(m), pk(v), name="adamw_small")
    for nm, dd, mm_, vv_ in zip(SMALL_NAMES, _unpack_flat(d, shapes), _unpack_flat(nm_, shapes), _unpack_flat(nv_, shapes)):
        grads[nm], delta[nm], new_m[nm], new_v[nm] = gsmall[nm], dd, mm_, vv_

    return (loss, grad_x, *[grads[nm] for nm in WEIGHT_NAMES], *[delta[nm] for nm in WEIGHT_NAMES],
            *[new_m[nm] for nm in WEIGHT_NAMES], *[new_v[nm] for nm in WEIGHT_NAMES])


def kernel(x, p, norm1_g, w_in, a_ln_g, a_ln_b, a_ws, a_bs, b_conv_w, b_conv_b, b_wa, b_ba, b_wx, b_bx, b_lam, c_lb, c_norm_g, d_w, d_scale, w_out, norm2_g, w_up, ffn_conv_w, ffn_conv_b, w_down, norm3_g, w_pe, w_pg, final_g, loss_target, m_norm1_g, m_w_in, m_a_ln_g, m_a_ln_b, m_a_ws, m_a_bs, m_b_conv_w, m_b_conv_b, m_b_wa, m_b_ba, m_b_wx, m_b_bx, m_b_lam, m_c_lb, m_c_norm_g, m_d_w, m_d_scale, m_w_out, m_norm2_g, m_w_up, m_ffn_conv_w, m_ffn_conv_b, m_w_down, m_norm3_g, m_w_pe, m_w_pg, m_final_g, v_norm1_g, v_w_in, v_a_ln_g, v_a_ln_b, v_a_ws, v_a_bs, v_b_conv_w, v_b_conv_b, v_b_wa, v_b_ba, v_b_wx, v_b_bx, v_b_lam, v_c_lb, v_c_norm_g, v_d_w, v_d_scale, v_w_out, v_norm2_g, v_w_up, v_ffn_conv_w, v_ffn_conv_b, v_w_down, v_norm3_g, v_w_pe, v_w_pg, v_final_g):
    w = dict(norm1_g=norm1_g, w_in=w_in, a_ln_g=a_ln_g, a_ln_b=a_ln_b, a_ws=a_ws, a_bs=a_bs, b_conv_w=b_conv_w, b_conv_b=b_conv_b, b_wa=b_wa, b_ba=b_ba, b_wx=b_wx, b_bx=b_bx, b_lam=b_lam, c_lb=c_lb, c_norm_g=c_norm_g, d_w=d_w, d_scale=d_scale, w_out=w_out, norm2_g=norm2_g, w_up=w_up, ffn_conv_w=ffn_conv_w, ffn_conv_b=ffn_conv_b, w_down=w_down, norm3_g=norm3_g, w_pe=w_pe, w_pg=w_pg, final_g=final_g)
    m = dict(norm1_g=m_norm1_g, w_in=m_w_in, a_ln_g=m_a_ln_g, a_ln_b=m_a_ln_b, a_ws=m_a_ws, a_bs=m_a_bs, b_conv_w=m_b_conv_w, b_conv_b=m_b_conv_b, b_wa=m_b_wa, b_ba=m_b_ba, b_wx=m_b_wx, b_bx=m_b_bx, b_lam=m_b_lam, c_lb=m_c_lb, c_norm_g=m_c_norm_g, d_w=m_d_w, d_scale=m_d_scale, w_out=m_w_out, norm2_g=m_norm2_g, w_up=m_w_up, ffn_conv_w=m_ffn_conv_w, ffn_conv_b=m_ffn_conv_b, w_down=m_w_down, norm3_g=m_norm3_g, w_pe=m_w_pe, w_pg=m_w_pg, final_g=m_final_g)
    v = dict(norm1_g=v_norm1_g, w_in=v_w_in, a_ln_g=v_a_ln_g, a_ln_b=v_a_ln_b, a_ws=v_a_ws, a_bs=v_a_bs, b_conv_w=v_b_conv_w, b_conv_b=v_b_conv_b, b_wa=v_b_wa, b_ba=v_b_ba, b_wx=v_b_wx, b_bx=v_b_bx, b_lam=v_b_lam, c_lb=v_c_lb, c_norm_g=v_c_norm_g, d_w=v_d_w, d_scale=v_d_scale, w_out=v_w_out, norm2_g=v_norm2_g, w_up=v_w_up, ffn_conv_w=v_ffn_conv_w, ffn_conv_b=v_ffn_conv_b, w_down=v_w_down, norm3_g=v_norm3_g, w_pe=v_w_pe, w_pg=v_w_pg, final_g=v_final_g)
    return _step(w, m, v, x, p, loss_target)
```

```python
import functools

import jax
import jax.numpy as jnp
from jax import lax
from jax.experimental import pallas as pl
from jax.experimental.pallas import tpu as pltpu

F32 = jnp.float32
BF16 = jnp.bfloat16
MESH = pl.DeviceIdType.MESH

D_MODEL = 1024
DEPTH = 4
PLE_DIM = 256
W_GRP = 256
N_HEADS = 4
HEAD_DIM = 64
GMLP_CHUNK = 128
RGLRU_C = 8.0
HGRN_CHUNK = 64
HGRN_SUB = 16
HGRN_STEP_CHUNKS = 4
POOL_WINDOWS = (2, 4, 8, 16)
D_FF = 2816
D_PROJ = 2304
EPS = 1e-6
ADAM_LR = 0.001
ADAM_B1 = 0.9
ADAM_B2 = 0.999
ADAM_EPS = 1e-08
ADAM_WD = 0.01
ADAM_STEP = 10

N_DEV = 8
MIB = 2 ** 20
V7X_VMEM_BYTES = 64 * MIB
HGRN_EXP_CLAMP = 60.0

WEIGHT_NAMES = ['norm1_g', 'w_in', 'a_ln_g', 'a_ln_b', 'a_ws', 'a_bs', 'b_conv_w', 'b_conv_b', 'b_wa', 'b_ba', 'b_wx',
                'b_bx', 'b_lam', 'c_lb', 'c_norm_g', 'd_w', 'd_scale', 'w_out', 'norm2_g', 'w_up', 'ffn_conv_w',
                'ffn_conv_b', 'w_down', 'norm3_g', 'w_pe', 'w_pg', 'final_g']
BIG_NAMES = ('w_in', 'w_out', 'w_up', 'w_down', 'w_pe', 'w_pg')


def _vmem_limit(block_bytes):
    want = 2 * block_bytes + 24 * MIB
    return int(min(max(want, 32 * MIB), V7X_VMEM_BYTES - 8 * MIB))


def _in_hbm(x):
    return pltpu.with_memory_space_constraint(x, pltpu.HBM)


def _out_hbm(s):
    return pltpu.HBM(s.shape, s.dtype)


def _pcall(body, *, name, out_shape, grid=None, in_specs=None, out_specs=None, scratch_shapes=(),
           semantics=None, block_bytes=0, aliases=None, pin=True):
    kw = {} if aliases is None else {"input_output_aliases": aliases}
    if pin:
        out_shape = tuple(_out_hbm(s) for s in out_shape) if isinstance(out_shape, (tuple, list)) else _out_hbm(out_shape)
    if grid is not None:
        kw["grid"] = grid
    if in_specs is not None:
        kw["in_specs"] = in_specs
    if out_specs is not None:
        kw["out_specs"] = out_specs
    params = pltpu.CompilerParams(dimension_semantics=semantics, vmem_limit_bytes=_vmem_limit(block_bytes))
    call = pl.pallas_call(body, name=name, out_shape=out_shape, scratch_shapes=list(scratch_shapes),
                          compiler_params=params, **kw)
    return (lambda *args: call(*[_in_hbm(a) for a in args])) if pin else call


def _pick(n, cands):
    for c in cands:
        if n % c == 0:
            return c
    return n


def _nbytes(shape, dtype):
    n = 1
    for s in shape:
        n *= s
    return n * jnp.dtype(dtype).itemsize


def _sds(shape, dtype):
    return jax.ShapeDtypeStruct(tuple(shape), dtype)


class _Sel:
    def __init__(self, arr, *idx):
        self.arr, self.idx = arr, tuple(idx)
        self.shape = arr.shape[len(idx):]
        self.ndim = len(self.shape)
        self.dtype = arr.dtype


def _arr(a):
    return a.arr if isinstance(a, _Sel) else a


def _spec(a, block=None, index=None):
    block = tuple(a.shape) if block is None else tuple(block)
    index = (lambda *g: (0,) * len(block)) if index is None else index
    if isinstance(a, _Sel):
        lead = a.idx
        return pl.BlockSpec((None,) * len(lead) + block, lambda *g: lead + tuple(index(*g)))
    return pl.BlockSpec(block, lambda *g: tuple(index(*g)))


def _ospec(a):
    return pl.BlockSpec(tuple(a.shape), lambda *g: (0,) * a.ndim)


def _rows_of(shape):
    return lax.broadcasted_iota(jnp.int32, shape, 0)


def _lanes_of(shape):
    return lax.broadcasted_iota(jnp.int32, shape, 1)


def _sdn(x, k, fill):
    n = x.shape[0]
    return jnp.where(_rows_of(x.shape) >= k, pltpu.roll(x, k % n, 0), fill)


def _sup(x, k, fill):
    n = x.shape[0]
    return jnp.where(_rows_of(x.shape) < n - k, pltpu.roll(x, (n - k) % n, 0), fill)


@functools.partial(jax.custom_vjp, nondiff_argnums=(1,))
def _shift_dn(x, k):
    return pltpu.roll(x, k, 0)


def _shift_dn_fwd(x, k):
    return pltpu.roll(x, k, 0), None


def _shift_dn_bwd(k, _, g):
    return (pltpu.roll(g, g.shape[0] - k, 0),)


_shift_dn.defvjp(_shift_dn_fwd, _shift_dn_bwd)


def _lin_scan_impl(a, b, h0):
    n = a.shape[0]
    aa, bb = a, b
    k = 1
    while k < n:
        bb = aa * _sdn(bb, k, 0.0) + bb
        aa = aa * _sdn(aa, k, 1.0)
        k *= 2
    return bb + aa * h0


@jax.custom_vjp
def _lin_scan(a, b, h0):
    return _lin_scan_impl(a, b, h0)


def _lin_scan_fwd(a, b, h0):
    h = _lin_scan_impl(a, b, h0)
    return h, (a, h, h0)


def _lin_scan_bwd(res, g):
    a, h, h0 = res
    n = a.shape[0]
    cc, gg = _sup(a, 1, 0.0), g
    k = 1
    while k < n:
        gg = gg + cc * _sup(gg, k, 0.0)
        cc = cc * _sup(cc, k, 1.0)
        k *= 2
    first = _rows_of(a.shape) == 0
    hprev = jnp.where(first, h0, _sdn(h, 1, 0.0))
    dh0 = jnp.sum(jnp.where(first, a * gg, 0.0), axis=0, keepdims=True)
    return gg * hprev, gg, dh0


_lin_scan.defvjp(_lin_scan_fwd, _lin_scan_bwd)


def _cumsum_sub_impl(x):
    pos = _rows_of(x.shape) % HGRN_SUB
    k = 1
    while k < HGRN_SUB:
        x = x + jnp.where(pos >= k, pltpu.roll(x, k, 0), 0.0)
        k *= 2
    return x


@jax.custom_vjp
def _cumsum_sub(x):
    return _cumsum_sub_impl(x)


def _cumsum_sub_fwd(x):
    return _cumsum_sub_impl(x), None


def _cumsum_sub_bwd(_, g):
    n = g.shape[0]
    pos = _rows_of(g.shape) % HGRN_SUB
    k = 1
    while k < HGRN_SUB:
        g = g + jnp.where(pos < HGRN_SUB - k, pltpu.roll(g, n - k, 0), 0.0)
        k *= 2
    return (g,)


_cumsum_sub.defvjp(_cumsum_sub_fwd, _cumsum_sub_bwd)


def _dot(a, b, ca, cb):
    return lax.dot_general(a.astype(BF16), b.astype(BF16), (((ca,), (cb,)), ((), ())), preferred_element_type=F32)


@jax.custom_vjp
def _mm(a, b):
    return _dot(a, b, 1, 0)


def _mm_fwd(a, b):
    return _dot(a, b, 1, 0), (a, b)


def _mm_bwd(res, g):
    a, b = res
    return _dot(g, b, 1, 1), _dot(a, g, 0, 0)


_mm.defvjp(_mm_fwd, _mm_bwd)


@jax.custom_vjp
def _mm_nt(a, b):
    return _dot(a, b, 1, 1)


def _mm_nt_fwd(a, b):
    return _dot(a, b, 1, 1), (a, b)


def _mm_nt_bwd(res, g):
    a, b = res
    return _dot(g, b, 1, 0), _dot(g, a, 0, 0)


_mm_nt.defvjp(_mm_nt_fwd, _mm_nt_bwd)


@jax.custom_vjp
def _mm_tn(a, b):
    return _dot(a, b, 0, 0)


def _mm_tn_fwd(a, b):
    return _dot(a, b, 0, 0), (a, b)


def _mm_tn_bwd(res, g):
    a, b = res
    return _dot(b, g, 1, 1), _dot(a, g, 1, 0)


_mm_tn.defvjp(_mm_tn_fwd, _mm_tn_bwd)


def _head_mask(shape, h):
    return (_lanes_of(shape) // HEAD_DIM) == h


def _stack_heads(x):
    return jnp.concatenate([jnp.where(_head_mask(x.shape, h), x, 0.0) for h in range(N_HEADS)], axis=0)


def _unstack_heads(p):
    r = p.shape[0] // N_HEADS
    out = None
    for h in range(N_HEADS):
        blk = p[h * r:(h + 1) * r]
        term = jnp.where(_head_mask(blk.shape, h), blk, 0.0)
        out = term if out is None else out + term
    return out


def _segmean_impl(x):
    n = x.shape[1]
    same = (lax.broadcasted_iota(jnp.int32, (n, n), 0) // HEAD_DIM) == (lax.broadcasted_iota(jnp.int32, (n, n), 1) // HEAD_DIM)
    m = jnp.where(same, 1.0 / HEAD_DIM, 0.0).astype(BF16)
    hi = x.astype(BF16)
    lo = (x - hi.astype(F32)).astype(BF16)
    dn = (((1,), (0,)), ((), ()))
    return (lax.dot_general(hi, m, dn, preferred_element_type=F32)
            + lax.dot_general(lo, m, dn, preferred_element_type=F32))


@jax.custom_vjp
def _segmean(x):
    return _segmean_impl(x)


def _segmean_fwd(x):
    return _segmean_impl(x), None


def _segmean_bwd(_, g):
    return (_segmean_impl(g),)


_segmean.defvjp(_segmean_fwd, _segmean_bwd)


def _log1p(u):
    w = 1.0 + u
    return jnp.where(w == 1.0, u, jnp.log(w) * (u / (w - 1.0)))


def _softplus(y):
    return jnp.maximum(y, 0.0) + _log1p(jnp.exp(-jnp.abs(y)))


def _rms(x, g):
    return x * lax.rsqrt(jnp.mean(x * x, axis=-1, keepdims=True) + EPS) * g


def _gmlp_chunk(zu, zv, ln_g, ln_b, wcat, bfull):
    u = jax.nn.gelu(zu)
    v = jax.nn.gelu(zv)
    mu = jnp.mean(v, axis=-1, keepdims=True)
    var = jnp.mean(jnp.square(v - mu), axis=-1, keepdims=True)
    vn = (v - mu) * lax.rsqrt(var + EPS) * ln_g + ln_b
    sv = _unstack_heads(_mm(wcat, vn)) + bfull
    return u * sv


def _rglru_tile(xb_ext, gb, h0, cw, cb, wa, ba, wx, bx, lam):
    xc = (cb + cw[0:1] * _shift_dn(xb_ext, 3) + cw[1:2] * _shift_dn(xb_ext, 2) + cw[2:3] * _shift_dn(xb_ext, 1)
          + cw[3:4] * xb_ext)[8:]
    r = jax.nn.sigmoid(_mm(xc, wa) + ba)
    i = jax.nn.sigmoid(_mm(xc, wx) + bx)
    log_a = (-RGLRU_C) * r * _softplus(-lam)
    a = jnp.exp(log_a)
    mult = jnp.sqrt(-jnp.tanh(log_a) * (a * a + 1.0))
    h = _lin_scan(a, mult * (i * xc), h0)
    y = h * jax.nn.gelu(gb)
    h_last = jnp.sum(jnp.where(_rows_of(h.shape) == h.shape[0] - 1, h, 0.0), axis=0, keepdims=True)
    return y, h_last


def _pool_tile(xd_ext, inv, wd, scale):
    s1 = xd_ext + _shift_dn(xd_ext, 1)
    s2 = s1 + _shift_dn(s1, 2)
    s3 = s2 + _shift_dn(s2, 4)
    s4 = s3 + _shift_dn(s3, 8)
    grp = _lanes_of(xd_ext.shape) // HEAD_DIM
    win = jnp.where(grp == 0, s1, jnp.where(grp == 1, s2, jnp.where(grp == 2, s3, s4)))
    pooled = win[16:] * inv - xd_ext[16:]
    return _mm(pooled, wd) * scale


def _hgrn_chunk(q, f, i, g, st, lb, ngf):
    n = q.shape[0]
    nsub = n // HGRN_SUB
    qs = jax.nn.silu(q)
    fg = lb + (1.0 - lb) * jax.nn.sigmoid(f)
    lf = jnp.log(fg)
    k = 1.0 - fg
    bl = _cumsum_sub(lf)
    row = _rows_of(q.shape)
    blk = row // HGRN_SUB
    betas = [jnp.zeros_like(lb)]
    for s in range(nsub):
        tot = jnp.sum(jnp.where(row == s * HGRN_SUB + HGRN_SUB - 1, bl, 0.0), axis=0, keepdims=True)
        betas.append(betas[-1] + tot)
    b_end = betas[nsub]
    beta_full = jnp.zeros_like(q)
    for s in range(1, nsub):
        beta_full = jnp.where(blk == s, betas[s], beta_full)
    qh = qs * jnp.exp(bl)
    qt = qh * jnp.exp(beta_full)
    b_all = beta_full + bl
    kt = k * jnp.exp(b_end - b_all)
    outs = []
    for s in range(nsub):
        kh = k * jnp.exp(jnp.minimum(betas[s] - b_all, HGRN_EXP_CLAMP))
        qstk = _stack_heads(qh[s * HGRN_SUB:(s + 1) * HGRN_SUB])
        att = _mm_nt(qstk, kh)
        ar = _rows_of(att.shape) % HGRN_SUB + s * HGRN_SUB
        att = jnp.where(_lanes_of(att.shape) <= ar, att, 0.0)
        outs.append(_unstack_heads(_mm(att, i)))
    o = jnp.concatenate(outs, axis=0) + _mm_nt(qt, st)
    same = (_rows_of(st.shape) // HEAD_DIM) == (_lanes_of(st.shape) // HEAD_DIM)
    st_new = st * jnp.exp(b_end) + jnp.where(same, _mm_tn(i, kt), 0.0)
    on = o * lax.rsqrt(_segmean(o * o) + EPS) * ngf
    return on * jax.nn.silu(g), st_new


def _ffn_tile(eg, ev, wg, bg, wv, bv):
    gt = (bg + wg[0:1] * _shift_dn(eg, 2) + wg[1:2] * _shift_dn(eg, 1) + wg[2:3] * eg)[8:]
    val = (bv + wv[0:1] * _shift_dn(ev, 2) + wv[1:2] * _shift_dn(ev, 1) + wv[2:3] * ev)[8:]
    return jax.nn.gelu(gt) * val


MXU_WIDTH = 256
MATMUL_BLOCK_BUDGET = 18 * MIB


def _matmul_tiles(m, k, n, a_dtype, b_dtype, out_dtype, has_res):
    best = None
    for tm in (2048, 1024, 512, 256):
        if m % tm:
            continue
        for tn in (1024, 768, 1408, 512, 256, 128):
            if n % tn:
                continue
            blk = (_nbytes((tm, k), a_dtype) + _nbytes((k, tn), b_dtype) + _nbytes((tm, tn), out_dtype)
                   + (_nbytes((tm, tn), F32) if has_res else 0))
            if blk > MATMUL_BLOCK_BUDGET:
                continue
            waste = -(-tn // MXU_WIDTH) * MXU_WIDTH / tn
            cost = (m // tm) * (n // tn) + 64 * (waste - 1.0)
            if best is None or cost < best[0]:
                best = (cost, tm, tn, blk)
    assert best is not None, (m, k, n)
    return best[1:]


def _matmul(a, b, *, name, nt=False, res=None, out_dtype=F32):
    m, k = a.shape
    n = b.shape[0] if nt else b.shape[1]
    tm, tn, blk = _matmul_tiles(m, k, n, a.dtype, b.dtype, out_dtype, res is not None)
    dims = (((1,), (1,)), ((), ())) if nt else (((1,), (0,)), ((), ()))

    def body(*refs):
        if res is None:
            a_ref, b_ref, o_ref = refs
        else:
            a_ref, b_ref, r_ref, o_ref = refs
        acc = lax.dot_general(a_ref[...], b_ref[...], dims, preferred_element_type=F32)
        if res is not None:
            acc = acc + r_ref[...]
        o_ref[...] = acc.astype(out_dtype)

    in_specs = [pl.BlockSpec((tm, k), lambda i, j: (i, 0)),
                _spec(b, (tn, k), lambda i, j: (j, 0)) if nt else _spec(b, (k, tn), lambda i, j: (0, j))]
    args = [a, _arr(b)]
    if res is not None:
        in_specs.append(pl.BlockSpec((tm, tn), lambda i, j: (i, j)))
        args.append(res)
    return _pcall(body, name=name, out_shape=_sds((m, n), out_dtype), grid=(m // tm, n // tn), in_specs=in_specs,
                  out_specs=pl.BlockSpec((tm, tn), lambda i, j: (i, j)), semantics=("parallel", "parallel"),
                  block_bytes=blk + _nbytes((tm, tn), F32))(*args)


def _matmul_rms_bwd(a, b, x, g, dres, *, name, nt=False, res=None):
    m, k = a.shape
    n = b.shape[0] if nt else b.shape[1]
    tm = _pick(m, (512, 256))
    dims = (((1,), (1,)), ((), ())) if nt else (((1,), (0,)), ((), ()))

    def body(*refs):
        a_ref, b_ref, x_ref, g_ref, dr_ref = refs[:5]
        dx_ref, dxb_ref, dg_ref = refs[-3:]
        dh = lax.dot_general(a_ref[...], b_ref[...], dims, preferred_element_type=F32)
        if res is not None:
            dh = dh + refs[5][...]
        _, vjp = jax.vjp(_rms, x_ref[...], g_ref[...])
        dxn, dg = vjp(dh)
        dx = dr_ref[...] + dxn
        dx_ref[...] = dx
        dxb_ref[...] = dx.astype(BF16)
        _acc_out(dg_ref, dg, pl.program_id(0) == 0)

    row = pl.BlockSpec((tm, n), lambda i: (i, 0))
    vec = pl.BlockSpec((1, n), lambda i: (0, 0))
    in_specs = [pl.BlockSpec((tm, k), lambda i: (i, 0)),
                _spec(b, (n, k), lambda i: (0, 0)) if nt else _spec(b, (k, n), lambda i: (0, 0)), row, _spec(g), row]
    args = [a, _arr(b), x, _arr(g), dres]
    if res is not None:
        in_specs.append(row)
        args.append(res)
    blk = _nbytes((tm, k), a.dtype) + _nbytes((k, n), b.dtype) + 6 * _nbytes((tm, n), F32)
    return _pcall(body, name=name, out_shape=(_sds((m, n), F32), _sds((m, n), BF16), _sds((1, n), F32)), grid=(m // tm,),
                  in_specs=in_specs, out_specs=(row, row, vec), semantics=("arbitrary",), block_bytes=blk)(*args)


def _matmul_tn(a, b, *, name, out_dtype=BF16, out_rows=None, row_off=0, into=None):
    m, k1 = a.shape
    n = b.shape[1]
    tk = _pick(k1, (512, 256, 128))
    off = row_off // tk
    assert off * tk == row_off

    def body(a_ref, b_ref, *rest):
        rest[-1][...] = lax.dot_general(a_ref[...], b_ref[...], (((0,), (0,)), ((), ())),
                                        preferred_element_type=F32).astype(out_dtype)

    blk = 2 * _nbytes((m, tk), a.dtype) + _nbytes((m, n), b.dtype) + _nbytes((tk, n), F32)
    in_specs = [pl.BlockSpec((m, tk), lambda i: (0, i)), pl.BlockSpec((m, n), lambda i: (0, 0))]
    args = [a, b]
    if into is not None:
        in_specs.append(HBM_SPEC)
        args.append(into)
    return _pcall(body, name=name, out_shape=_sds((out_rows or k1, n), out_dtype), grid=(k1 // tk,), in_specs=in_specs,
                  out_specs=pl.BlockSpec((tk, n), lambda i: (i + off, 0)), semantics=("parallel",), block_bytes=blk,
                  aliases=None if into is None else {2: 0})(*args)


def _rms_matmul(x, g, bs, *, name, nt=False, ple=None):
    m, d = x.shape
    n = bs[0].shape[0] if nt else bs[0].shape[1]
    nb = len(bs)
    tm = _pick(m, (1024, 512, 256))
    tn = _pick(n, (768, 512, 256, 128))
    dims = (((1,), (1,)), ((), ())) if nt else (((1,), (0,)), ((), ()))

    def body(*refs):
        x_ref, g_ref, b_refs = refs[0], refs[1], refs[2:2 + nb]
        rest = refs[2 + nb:]
        h_scr = rest[-1]
        j = pl.program_id(1)

        @pl.when(j == 0)
        def _():
            h = _rms(x_ref[...], g_ref[...]).astype(BF16)
            h_scr[...] = h
            rest[-2 - nb - (2 if ple else 0)][...] = h

        h = h_scr[...]
        if ple is None:
            for k in range(nb):
                rest[-1 - nb + k][...] = lax.dot_general(h, b_refs[k][...], dims, preferred_element_type=F32)
        else:
            p_ref, wpe_ref, xt_ref = rest[0], rest[1], rest[2]
            gl_ref, pe_ref, out_ref = rest[-4], rest[-3], rest[-2]
            gl = lax.dot_general(h, b_refs[0][...], dims, preferred_element_type=F32)
            pe = lax.dot_general(p_ref[...], wpe_ref[...], (((1,), (1,)), ((), ())), preferred_element_type=F32)
            gl_ref[...] = gl
            pe_ref[...] = pe
            out_ref[...] = xt_ref[...] + pe * jax.nn.sigmoid(gl)

    row = pl.BlockSpec((tm, d), lambda i, j: (i, 0))
    tile = pl.BlockSpec((tm, tn), lambda i, j: (i, j))
    in_specs = [row, _spec(g)] + [_spec(b, (tn, d), lambda i, j: (j, 0)) if nt else _spec(b, (d, tn), lambda i, j: (0, j))
                                  for b in bs]
    args = [x, _arr(g)] + [_arr(b) for b in bs]
    out_shape, out_specs = [_sds((m, d), BF16)], [row]
    if ple is None:
        out_shape += [_sds((m, n), F32)] * nb
        out_specs += [tile] * nb
    else:
        p, wpe = ple
        in_specs += [pl.BlockSpec((tm, p.shape[1]), lambda i, j: (i, 0)), _spec(wpe, (tn, p.shape[1]), lambda i, j: (j, 0)),
                     tile]
        args += [p, _arr(wpe), x]
        out_shape += [_sds((m, n), F32)] * 3
        out_specs += [tile] * 3
    blk = (3 * _nbytes((tm, d), F32) + nb * _nbytes((d, tn), BF16) + (nb + 3) * _nbytes((tm, tn), F32))
    outs = _pcall(body, name=name, out_shape=tuple(out_shape), grid=(m // tm, n // tn), in_specs=in_specs,
                  out_specs=tuple(out_specs), scratch_shapes=[pltpu.VMEM((tm, d), BF16)],
                  semantics=("parallel", "arbitrary"), block_bytes=blk)(*args)
    return outs[0], list(outs[1:])


def _ple_bwd(dx, gl, pe, *, name):
    s, d = dx.shape
    tm = _pick(s, (512, 256))

    def body(dx_ref, gl_ref, pe_ref, dpe_ref, dgl_ref):
        gate = jax.nn.sigmoid(gl_ref[...])
        dxv = dx_ref[...]
        dpe_ref[...] = (dxv * gate).astype(BF16)
        dgl_ref[...] = (dxv * pe_ref[...] * gate * (1.0 - gate)).astype(BF16)

    row = pl.BlockSpec((tm, d), lambda i: (i, 0))
    return _pcall(body, name=name, out_shape=(_sds((s, d), BF16), _sds((s, d), BF16)), grid=(s // tm,),
                  in_specs=[row, row, row], out_specs=(row, row), semantics=("parallel",),
                  block_bytes=5 * _nbytes((tm, d), F32))(dx, gl, pe)


def _loss_head(x, g, target, *, name):
    s, d = x.shape
    tm = _pick(s, (256, 128))

    def tile_loss(xv, gv, tv):
        err = jnp.square(_rms(xv, gv) - tv)
        return 0.5 * jnp.sum(jnp.mean(err, axis=-1, keepdims=True), axis=0, keepdims=True)

    def body(x_ref, g_ref, t_ref, l_ref, dx_ref, dg_ref):
        lv, vjp = jax.vjp(tile_loss, x_ref[...], g_ref[...], t_ref[...])
        dxv, dgv, _ = vjp(jnp.ones((1, 1), F32))
        dx_ref[...] = dxv

        @pl.when(pl.program_id(0) == 0)
        def _():
            l_ref[...] = jnp.zeros_like(l_ref)
            dg_ref[...] = jnp.zeros_like(dg_ref)

        l_ref[...] += jnp.broadcast_to(lv, l_ref.shape)
        dg_ref[...] += dgv

    row = pl.BlockSpec((tm, d), lambda i: (i, 0))
    vec = pl.BlockSpec((1, d), lambda i: (0, 0))
    return _pcall(body, name=name, out_shape=(_sds((8, 128), F32), _sds((s, d), F32), _sds((1, d), F32)),
                  grid=(s // tm,), in_specs=[row, vec, row],
                  out_specs=(pl.BlockSpec((8, 128), lambda i: (0, 0)), row, vec), semantics=("arbitrary",),
                  block_bytes=8 * _nbytes((tm, d), F32))(x, g, target)


def _acc_out(ref, val, first):
    @pl.when(first)
    def _():
        ref[...] = jnp.zeros_like(ref)

    ref[...] += val


def _gmlp_fwd(z, ln_g, ln_b, wcat, bfull, *, name):
    s = z.shape[0]
    t = _pick(s, (512, 256, 128))
    nch = t // GMLP_CHUNK

    def body(zu_ref, zv_ref, g_ref, b_ref, w_ref, bf_ref, o_ref):
        for c in range(nch):
            rows = pl.ds(c * GMLP_CHUNK, GMLP_CHUNK)
            o_ref[rows, :] = _gmlp_chunk(zu_ref[rows, :], zv_ref[rows, :], g_ref[...], b_ref[...], w_ref[...],
                                         bf_ref[...]).astype(BF16)

    col = lambda c: pl.BlockSpec((t, W_GRP), lambda i: (i, c))
    params = (ln_g, ln_b, wcat, bfull)
    return _pcall(body, name=name, out_shape=_sds((s, D_MODEL), BF16), grid=(s // t,),
                  in_specs=[col(0), col(1)] + [_spec(a) for a in params],
                  out_specs=pl.BlockSpec((t, W_GRP), lambda i: (i, 0)), semantics=("parallel",),
                  block_bytes=4 * _nbytes((t, W_GRP), F32))(z, z, *[_arr(a) for a in params])


def _gmlp_bwd(z, dmix, ln_g, ln_b, wcat, bfull, *, name):
    s = z.shape[0]
    t = _pick(s, (512, 256, 128))
    nch = t // GMLP_CHUNK

    def body(zu_ref, zv_ref, dy_ref, g_ref, b_ref, w_ref, bf_ref, dz_ref, dg_ref, db_ref, dw_ref, dbf_ref):
        acc = None
        for c in range(nch):
            rows = pl.ds(c * GMLP_CHUNK, GMLP_CHUNK)
            _, vjp = jax.vjp(_gmlp_chunk, zu_ref[rows, :], zv_ref[rows, :], g_ref[...], b_ref[...], w_ref[...],
                             bf_ref[...])
            du, dv, *dps = vjp(dy_ref[rows, :])
            dz_ref[rows, :] = jnp.concatenate([du, dv], axis=1).astype(BF16)
            acc = dps if acc is None else [x + y for x, y in zip(acc, dps)]
        first = pl.program_id(0) == 0
        for ref, val in zip((dg_ref, db_ref, dw_ref, dbf_ref), acc):
            _acc_out(ref, val, first)

    col = lambda c: pl.BlockSpec((t, W_GRP), lambda i: (i, c))
    params = (ln_g, ln_b, wcat, bfull)
    return _pcall(body, name=name,
                  out_shape=(_sds((s, D_PROJ), BF16),) + tuple(_sds(a.shape, F32) for a in params),
                  grid=(s // t,), in_specs=[col(0), col(1), col(0)] + [_spec(a) for a in params],
                  out_specs=(pl.BlockSpec((t, 2 * W_GRP), lambda i: (i, 0)),) + tuple(_ospec(a) for a in params),
                  semantics=("arbitrary",),
                  block_bytes=8 * _nbytes((t, W_GRP), F32))(z, z, dmix, *[_arr(a) for a in params])


def _rglru_fwd(z, prm, mix, *, name):
    s = z.shape[0]
    t = _pick(s, (512, 256, 128))
    nt = s // t

    def body(xb_ref, halo_ref, gb_ref, *rest):
        prm_refs, (y_ref, h0s_ref, h_scr) = rest[:len(prm)], rest[len(prm) + 1:]
        i = pl.program_id(0)

        @pl.when(i == 0)
        def _():
            h_scr[...] = jnp.zeros_like(h_scr)

        halo = jnp.where(i == 0, 0.0, halo_ref[...])
        h0 = h_scr[...]
        y, h_last = _rglru_tile(jnp.concatenate([halo, xb_ref[...]], axis=0), gb_ref[...], h0,
                                *[r[...] for r in prm_refs])
        y_ref[...] = y.astype(BF16)
        h0s_ref[...] = jnp.broadcast_to(h0, h0s_ref.shape)
        h_scr[...] = h_last

    in_specs = [pl.BlockSpec((t, W_GRP), lambda i: (i, 2)),
                pl.BlockSpec((8, W_GRP), lambda i: (jnp.maximum(i * (t // 8) - 1, 0), 2)),
                pl.BlockSpec((t, W_GRP), lambda i: (i, 3))] + [_spec(a) for a in prm] + [HBM_SPEC]
    return _pcall(body, name=name, out_shape=(_sds(mix.shape, BF16), _sds((nt, 8, W_GRP), F32)), grid=(nt,),
                  in_specs=in_specs,
                  out_specs=(pl.BlockSpec((t, W_GRP), lambda i: (i, 1)), pl.BlockSpec((None, 8, W_GRP), lambda i: (i, 0, 0))),
                  scratch_shapes=[pltpu.VMEM((1, W_GRP), F32)], semantics=("arbitrary",),
                  block_bytes=24 * _nbytes((t, W_GRP), F32), aliases={3 + len(prm): 0})(
                      z, z, z, *[_arr(a) for a in prm], mix)


def _rglru_bwd(z, dmix, h0s, prm, dz, *, name):
    s = z.shape[0]
    t = _pick(s, (512, 256, 128))
    nt = s // t
    npm = len(prm)

    def body(xb_ref, halo_ref, gb_ref, dy_ref, h0s_ref, *rest):
        prm_refs = rest[:npm]
        dz_ref = rest[npm + 1]
        dprm_refs = rest[npm + 2:2 * npm + 2]
        dh_scr, dhalo_scr = rest[2 * npm + 2:]
        i = pl.program_id(0)
        r = nt - 1 - i

        @pl.when(i == 0)
        def _():
            dh_scr[...] = jnp.zeros_like(dh_scr)
            dhalo_scr[...] = jnp.zeros_like(dhalo_scr)

        halo = jnp.where(r == 0, 0.0, halo_ref[...])
        h0 = h0s_ref[0:1, :]
        _, vjp = jax.vjp(_rglru_tile, jnp.concatenate([halo, xb_ref[...]], axis=0), gb_ref[...], h0,
                         *[p[...] for p in prm_refs])
        dext, dgb, _dh0, *dps = vjp((dy_ref[...], dh_scr[...]))
        dmain = dext[8:]
        dxb = jnp.concatenate([dmain[:t - 8], dmain[t - 8:] + dhalo_scr[...]], axis=0)
        dz_ref[...] = jnp.concatenate([dxb, dgb], axis=1).astype(BF16)
        dh_scr[...] = _dh0
        dhalo_scr[...] = dext[:8]
        for ref, val in zip(dprm_refs, dps):
            _acc_out(ref, val, i == 0)

    rev = lambda c: pl.BlockSpec((t, W_GRP), lambda i: (nt - 1 - i, c))
    in_specs = [rev(2), pl.BlockSpec((8, W_GRP), lambda i: (jnp.maximum((nt - 1 - i) * (t // 8) - 1, 0), 2)), rev(3),
                rev(1), pl.BlockSpec((None, 8, W_GRP), lambda i: (nt - 1 - i, 0, 0))] + [_spec(a) for a in prm] + [HBM_SPEC]
    return _pcall(body, name=name,
                  out_shape=(_sds(dz.shape, BF16),) + tuple(_sds(a.shape, F32) for a in prm),
                  grid=(nt,), in_specs=in_specs,
                  out_specs=(pl.BlockSpec((t, 2 * W_GRP), lambda i: (nt - 1 - i, 1)),) + tuple(_ospec(a) for a in prm),
                  scratch_shapes=[pltpu.VMEM((1, W_GRP), F32), pltpu.VMEM((8, W_GRP), F32)],
                  semantics=("arbitrary",), block_bytes=40 * _nbytes((t, W_GRP), F32), aliases={5 + npm: 0})(
                      z, z, z, dmix, h0s, *[_arr(a) for a in prm], dz)


def _pool_inv(i, t):
    pos = (_rows_of((t, W_GRP)) + i * t + 1).astype(F32)
    grp = _lanes_of((t, W_GRP)) // HEAD_DIM
    win = jnp.where(grp == 0, float(POOL_WINDOWS[0]), jnp.where(grp == 1, float(POOL_WINDOWS[1]),
                    jnp.where(grp == 2, float(POOL_WINDOWS[2]), float(POOL_WINDOWS[3]))))
    return 1.0 / jnp.minimum(pos, win)


def _pool_fwd(z, wd, scale, mix, *, name):
    s = z.shape[0]
    t = _pick(s, (512, 256, 128))

    def body(x_ref, halo_ref, wd_ref, sc_ref, _, y_ref):
        i = pl.program_id(0)
        halo = jnp.where(i == 0, 0.0, halo_ref[...])
        y = _pool_tile(jnp.concatenate([halo, x_ref[...]], axis=0), _pool_inv(i, t), wd_ref[...], sc_ref[...])
        y_ref[...] = y.astype(BF16)

    in_specs = [pl.BlockSpec((t, W_GRP), lambda i: (i, 8)),
                pl.BlockSpec((16, W_GRP), lambda i: (jnp.maximum(i * (t // 16) - 1, 0), 8)), _spec(wd), _spec(scale),
                HBM_SPEC]
    return _pcall(body, name=name, out_shape=_sds(mix.shape, BF16), grid=(s // t,), in_specs=in_specs,
                  out_specs=pl.BlockSpec((t, W_GRP), lambda i: (i, 3)), semantics=("parallel",),
                  block_bytes=12 * _nbytes((t, W_GRP), F32), aliases={4: 0})(z, z, _arr(wd), _arr(scale), mix)


def _pool_bwd(z, dmix, wd, scale, dz, *, name):
    s = z.shape[0]
    t = _pick(s, (512, 256, 128))
    nt = s // t

    def body(x_ref, halo_ref, dy_ref, wd_ref, sc_ref, _, dx_ref, dwd_ref, dsc_ref, dhalo_scr):
        i = pl.program_id(0)
        r = nt - 1 - i

        @pl.when(i == 0)
        def _():
            dhalo_scr[...] = jnp.zeros_like(dhalo_scr)

        halo = jnp.where(r == 0, 0.0, halo_ref[...])
        inv = _pool_inv(r, t)
        _, vjp = jax.vjp(lambda e, w, sc: _pool_tile(e, inv, w, sc), jnp.concatenate([halo, x_ref[...]], axis=0),
                         wd_ref[...], sc_ref[...])
        dext, dwd, dsc = vjp(dy_ref[...])
        dmain = dext[16:]
        dx = jnp.concatenate([dmain[:t - 16], dmain[t - 16:] + dhalo_scr[...]], axis=0)
        dx_ref[...] = dx.astype(BF16)
        dhalo_scr[...] = dext[:16]
        _acc_out(dwd_ref, dwd, i == 0)
        _acc_out(dsc_ref, dsc, i == 0)

    rev = lambda c: pl.BlockSpec((t, W_GRP), lambda i: (nt - 1 - i, c))
    in_specs = [rev(8), pl.BlockSpec((16, W_GRP), lambda i: (jnp.maximum((nt - 1 - i) * (t // 16) - 1, 0), 8)), rev(3),
                _spec(wd), _spec(scale), HBM_SPEC]
    return _pcall(body, name=name, out_shape=(_sds(dz.shape, BF16), _sds(wd.shape, F32), _sds(scale.shape, F32)),
                  grid=(nt,), in_specs=in_specs, out_specs=(rev(8), _ospec(wd), _ospec(scale)),
                  scratch_shapes=[pltpu.VMEM((16, W_GRP), F32)], semantics=("arbitrary",),
                  block_bytes=20 * _nbytes((t, W_GRP), F32), aliases={5: 0})(z, z, dmix, _arr(wd), _arr(scale), dz)


def _hgrn_fwd(z, lb, ngf, mix, *, name):
    s = z.shape[0]
    c = HGRN_CHUNK
    per = HGRN_STEP_CHUNKS
    ns = s // (c * per)

    def body(q_ref, f_ref, i_ref, g_ref, lb_ref, ng_ref, _, y_ref, sts_ref, st_scr):
        @pl.when(pl.program_id(0) == 0)
        def _():
            st_scr[...] = jnp.zeros_like(st_scr)

        st = st_scr[...]
        for k in range(per):
            rows = pl.ds(k * c, c)
            sts_ref[k] = st
            y, st = _hgrn_chunk(q_ref[rows, :], f_ref[rows, :], i_ref[rows, :], g_ref[rows, :], st, lb_ref[...],
                                ng_ref[...])
            y_ref[rows, :] = y.astype(BF16)
        st_scr[...] = st

    col = lambda k: pl.BlockSpec((per * c, W_GRP), lambda i: (i, k))
    return _pcall(body, name=name, out_shape=(_sds(mix.shape, BF16), _sds((ns * per, W_GRP, W_GRP), F32)), grid=(ns,),
                  in_specs=[col(4), col(5), col(6), col(7), _spec(lb), _spec(ngf), HBM_SPEC],
                  out_specs=(pl.BlockSpec((per * c, W_GRP), lambda i: (i, 2)),
                             pl.BlockSpec((per, W_GRP, W_GRP), lambda i: (i, 0, 0))),
                  scratch_shapes=[pltpu.VMEM((W_GRP, W_GRP), F32)], semantics=("arbitrary",),
                  block_bytes=16 * per * _nbytes((W_GRP, W_GRP), F32), aliases={6: 0})(
                      z, z, z, z, _arr(lb), _arr(ngf), mix)


def _hgrn_bwd(z, dmix, sts, lb, ngf, dz, *, name):
    s = z.shape[0]
    c = HGRN_CHUNK
    per = HGRN_STEP_CHUNKS
    ns = s // (c * per)

    def body(q_ref, f_ref, i_ref, g_ref, dy_ref, st_ref, lb_ref, ng_ref, _, dz_ref, dlb_ref, dng_ref, dst_scr):
        i = pl.program_id(0)

        @pl.when(i == 0)
        def _():
            dst_scr[...] = jnp.zeros_like(dst_scr)

        dst = dst_scr[...]
        dlb_sum = dng_sum = None
        for k in range(per - 1, -1, -1):
            rows = pl.ds(k * c, c)
            _, vjp = jax.vjp(_hgrn_chunk, q_ref[rows, :], f_ref[rows, :], i_ref[rows, :], g_ref[rows, :], st_ref[k],
                             lb_ref[...], ng_ref[...])
            dq, df, di, dg, dst, dlb, dng = vjp((dy_ref[rows, :], dst))
            dz_ref[rows, :] = jnp.concatenate([dq, df, di, dg], axis=1).astype(BF16)
            dlb_sum = dlb if dlb_sum is None else dlb_sum + dlb
            dng_sum = dng if dng_sum is None else dng_sum + dng
        dst_scr[...] = dst
        _acc_out(dlb_ref, dlb_sum, i == 0)
        _acc_out(dng_ref, dng_sum, i == 0)

    rev = lambda k: pl.BlockSpec((per * c, W_GRP), lambda i: (ns - 1 - i, k))
    vec = pl.BlockSpec((1, W_GRP), lambda i: (0, 0))
    return _pcall(body, name=name, out_shape=(_sds(dz.shape, BF16), _sds((1, W_GRP), F32), _sds((1, W_GRP), F32)),
                  grid=(ns,),
                  in_specs=[rev(4), rev(5), rev(6), rev(7), rev(2),
                            pl.BlockSpec((per, W_GRP, W_GRP), lambda i: (ns - 1 - i, 0, 0)), _spec(lb), _spec(ngf),
                            HBM_SPEC],
                  out_specs=(pl.BlockSpec((per * c, 4 * W_GRP), lambda i: (ns - 1 - i, 1)), vec, vec),
                  scratch_shapes=[pltpu.VMEM((W_GRP, W_GRP), F32)], semantics=("arbitrary",),
                  block_bytes=32 * per * _nbytes((W_GRP, W_GRP), F32), aliases={8: 0})(
                      z, z, z, z, dmix, sts, _arr(lb), _arr(ngf), dz)


def _lbs_fwd(c_lb, *, name):
    def body(c_ref, o_ref):
        c = c_ref[...]
        e = jnp.exp(c - jnp.max(c, axis=0, keepdims=True))
        sm = e / jnp.sum(e, axis=0, keepdims=True)
        run = jnp.zeros((1, W_GRP), F32)
        o_ref[0:1, :] = run
        for l in range(1, DEPTH):
            run = run + sm[l:l + 1]
            o_ref[l:l + 1, :] = run

    return _pcall(body, name=name, out_shape=_sds((DEPTH, W_GRP), F32), pin=False)(c_lb)


def _lbs_bwd(c_lb, dlbs, *, name):
    def body(c_ref, d_ref, o_ref):
        c = c_ref[...]
        e = jnp.exp(c - jnp.max(c, axis=0, keepdims=True))
        sm = e / jnp.sum(e, axis=0, keepdims=True)
        d = d_ref[...]
        dsm = [None] * DEPTH
        run = jnp.zeros((1, W_GRP), F32)
        for l in range(DEPTH - 1, 0, -1):
            run = run + d[l:l + 1]
            dsm[l] = run
        dsm[0] = jnp.zeros((1, W_GRP), F32)
        inner = sum(sm[l:l + 1] * dsm[l] for l in range(DEPTH))
        for l in range(DEPTH):
            o_ref[l:l + 1, :] = sm[l:l + 1] * (dsm[l] - inner)

    return _pcall(body, name=name, out_shape=_sds((DEPTH, W_GRP), F32), pin=False)(c_lb, dlbs)


def _ffn_fwd(hg, hv, cwf, cbf, *, name):
    s, n = hg.shape
    t = _pick(s, (256, 128))
    cw = _pick(n, (1408, 256, 128))
    nj = n // cw

    def body(g_ref, gh_ref, v_ref, vh_ref, wg_ref, bg_ref, wv_ref, bv_ref, o_ref):
        first = pl.program_id(1) == 0
        eg = jnp.concatenate([jnp.where(first, 0.0, gh_ref[...]), g_ref[...]], axis=0)
        ev = jnp.concatenate([jnp.where(first, 0.0, vh_ref[...]), v_ref[...]], axis=0)
        o_ref[...] = _ffn_tile(eg, ev, wg_ref[...], bg_ref[...], wv_ref[...], bv_ref[...]).astype(BF16)

    main = pl.BlockSpec((t, cw), lambda j, i: (i, j))
    halo = pl.BlockSpec((8, cw), lambda j, i: (jnp.maximum(i * (t // 8) - 1, 0), j))
    taps = lambda off: _spec(cwf, (3, cw), lambda j, i: (0, j + off))
    bias = lambda off: _spec(cbf, (1, cw), lambda j, i: (0, j + off))
    return _pcall(body, name=name, out_shape=_sds((s, n), BF16), grid=(nj, s // t),
                  in_specs=[main, halo, main, halo, taps(0), bias(0), taps(nj), bias(nj)], out_specs=main,
                  semantics=("parallel", "parallel"), block_bytes=12 * _nbytes((t, cw), F32))(
                      hg, hg, hv, hv, _arr(cwf), _arr(cbf), _arr(cwf), _arr(cbf))


def _ffn_bwd(hg, hv, da, cwf, cbf, *, name):
    s, n = hg.shape
    t = _pick(s, (256, 128))
    cw = _pick(n, (1408, 256, 128))
    nt = s // t
    nj = n // cw

    def body(g_ref, gh_ref, v_ref, vh_ref, da_ref, wg_ref, bg_ref, wv_ref, bv_ref, dg_ref, dv_ref, dwg_ref, dwv_ref,
             cg_scr, cv_scr):
        i = pl.program_id(1)
        r = nt - 1 - i

        @pl.when(i == 0)
        def _():
            cg_scr[...] = jnp.zeros_like(cg_scr)
            cv_scr[...] = jnp.zeros_like(cv_scr)

        eg = jnp.concatenate([jnp.where(r == 0, 0.0, gh_ref[...]), g_ref[...]], axis=0)
        ev = jnp.concatenate([jnp.where(r == 0, 0.0, vh_ref[...]), v_ref[...]], axis=0)
        _, vjp = jax.vjp(_ffn_tile, eg, ev, wg_ref[...], bg_ref[...], wv_ref[...], bv_ref[...])
        deg, dev, dwg, dbg, dwv, dbv = vjp(da_ref[...])
        for dext, scr, ref in ((deg, cg_scr, dg_ref), (dev, cv_scr, dv_ref)):
            dmain = dext[8:]
            ref[...] = jnp.concatenate([dmain[:t - 8], dmain[t - 8:] + scr[...]], axis=0).astype(BF16)
            scr[...] = dext[:8]
        zeros = jnp.zeros((4, cw), F32)
        _acc_out(dwg_ref, jnp.concatenate([dwg, dbg, zeros], axis=0), i == 0)
        _acc_out(dwv_ref, jnp.concatenate([dwv, dbv, zeros], axis=0), i == 0)

    main = pl.BlockSpec((t, cw), lambda j, i: (nt - 1 - i, j))
    halo = pl.BlockSpec((8, cw), lambda j, i: (jnp.maximum((nt - 1 - i) * (t // 8) - 1, 0), j))
    taps = lambda off: _spec(cwf, (3, cw), lambda j, i: (0, j + off))
    bias = lambda off: _spec(cbf, (1, cw), lambda j, i: (0, j + off))
    w8 = pl.BlockSpec((8, cw), lambda j, i: (0, j))
    return _pcall(body, name=name,
                  out_shape=(_sds((s, n), BF16), _sds((s, n), BF16), _sds((8, n), F32), _sds((8, n), F32)),
                  grid=(nj, nt), in_specs=[main, halo, main, halo, main, taps(0), bias(0), taps(nj), bias(nj)],
                  out_specs=(main, main, w8, w8),
                  scratch_shapes=[pltpu.VMEM((8, cw), F32), pltpu.VMEM((8, cw), F32)],
                  semantics=("parallel", "arbitrary"), block_bytes=24 * _nbytes((t, cw), F32))(
                      hg, hg, hv, hv, da, _arr(cwf), _arr(cbf), _arr(cwf), _arr(cbf))


def _all_gather(x, *, name):
    r, c = x.shape

    def body(x_ref, out_ref, send_sems, recv_sems, local_sem):
        mx, my, mc = lax.axis_index("x"), lax.axis_index("y"), lax.axis_index("c")
        me, sibling = (mx, my, mc), (mx, my, 1 - mc)
        chips = [(1 - mx, my), (mx, 1 - my), (1 - mx, 1 - my)]

        def slot(px, py, pc):
            return out_ref.at[4 * px + 2 * py + pc]

        def copy(k, block, to, src=None):
            return pltpu.make_async_remote_copy(src_ref=slot(*block) if src is None else src, dst_ref=slot(*block),
                                                send_sem=send_sems.at[k], recv_sem=recv_sems.at[k],
                                                device_id=to, device_id_type=MESH)

        mine = pltpu.make_async_copy(x_ref, slot(*me), local_sem)
        mine.start()
        first = [copy(0, me, sibling, src=x_ref)]
        first += [copy(1 + j, me, (*chip, mc), src=x_ref) for j, chip in enumerate(chips)]
        for cp in first:
            cp.start()
        passed = [copy(4 + j, (*chip, mc), sibling) for j, chip in enumerate(chips)]
        for j, chip in enumerate(chips):
            copy(1 + j, (*chip, mc), me).wait_recv()
            passed[j].start()
        copy(0, sibling, me).wait_recv()
        for j, chip in enumerate(chips):
            copy(4 + j, (*chip, 1 - mc), me).wait_recv()
        for cp in first + passed:
            cp.wait_send()
        mine.wait()

    hbm = pl.BlockSpec(memory_space=pl.ANY)
    return _pcall(body, name=name, out_shape=_sds((N_DEV, r, c), x.dtype), in_specs=[hbm], out_specs=hbm,
                  scratch_shapes=[pltpu.SemaphoreType.DMA((7,)), pltpu.SemaphoreType.DMA((7,)),
                                  pltpu.SemaphoreType.DMA(())])(x)


def _sum_slots(p, *, name):
    q, r, c = p.shape
    tr = _pick(r, (544, 408, 272, 192, 136, 64, 32, 16, 8))

    def body(p_ref, o_ref):
        acc = p_ref[0].astype(F32)
        for k in range(1, q):
            acc = acc + p_ref[k].astype(F32)
        o_ref[...] = acc

    return _pcall(body, name=name, out_shape=_sds((r, c), F32), grid=(r // tr,),
                  in_specs=[pl.BlockSpec((q, tr, c), lambda i: (0, i, 0))],
                  out_specs=pl.BlockSpec((tr, c), lambda i: (i, 0)), semantics=("parallel",),
                  block_bytes=(q + 2) * _nbytes((tr, c), F32))(p)


BIG_COMM = (('w_in', 288, D_MODEL), ('w_out', 128, D_MODEL), ('w_up', 704, D_MODEL), ('w_down', 352, D_MODEL),
            ('w_pe', 128, PLE_DIM), ('w_pg', 128, D_MODEL))
HBM_SPEC = pl.BlockSpec(memory_space=pl.ANY)


def _gather_layer(shards, l, *, name):
    na = len(shards)

    def body(*refs):
        x_refs, out_refs = refs[:na], refs[na:2 * na]
        send_sems, recv_sems, local_sems = refs[2 * na:]
        mx, my, mc = lax.axis_index("x"), lax.axis_index("y"), lax.axis_index("c")
        me, sibling = (mx, my, mc), (mx, my, 1 - mc)
        chips = [(1 - mx, my), (mx, 1 - my), (1 - mx, 1 - my)]

        def slot(a, px, py, pc):
            return out_refs[a].at[4 * px + 2 * py + pc]

        def copy(k, a, block, to, own=False):
            return pltpu.make_async_remote_copy(src_ref=x_refs[a].at[l] if own else slot(a, *block),
                                                dst_ref=slot(a, *block), send_sem=send_sems.at[k, a],
                                                recv_sem=recv_sems.at[k, a], device_id=to, device_id_type=MESH)

        mine = [pltpu.make_async_copy(x_refs[a].at[l], slot(a, *me), local_sems.at[a]) for a in range(na)]
        for cp in mine:
            cp.start()
        first = []
        for a in range(na):
            first.append(copy(0, a, me, sibling, own=True))
            first += [copy(1 + j, a, me, (*chip, mc), own=True) for j, chip in enumerate(chips)]
        for cp in first:
            cp.start()
        passed = []
        for j, chip in enumerate(chips):
            for a in range(na):
                copy(1 + j, a, (*chip, mc), me).wait_recv()
                fwd = copy(4 + j, a, (*chip, mc), sibling)
                fwd.start()
                passed.append(fwd)
        for a in range(na):
            copy(0, a, sibling, me).wait_recv()
        for j, chip in enumerate(chips):
            for a in range(na):
                copy(4 + j, a, (*chip, 1 - mc), me).wait_recv()
        for cp in first + passed:
            cp.wait_send()
        for cp in mine:
            cp.wait()

    return _pcall(body, name=name, out_shape=tuple(_sds((N_DEV,) + x.shape[1:], x.dtype) for x in shards),
                  in_specs=[HBM_SPEC] * na, out_specs=(HBM_SPEC,) * na,
                  scratch_shapes=[pltpu.SemaphoreType.DMA((7, na)), pltpu.SemaphoreType.DMA((7, na)),
                                  pltpu.SemaphoreType.DMA((na,))])(*shards)


SEM_SPEC = pl.BlockSpec(memory_space=pltpu.SEMAPHORE)
DATAFLOW_EFFECT = pltpu.SideEffectType.DATAFLOW_SIDE_EFFECTING


def _place_own(srcs, after, *, name):
    na = len(srcs)

    def body(*refs):
        x_refs, land_refs, sems = refs[:na], refs[na + len(after):2 * na + len(after)], refs[-1]
        me = 4 * lax.axis_index("x") + 2 * lax.axis_index("y") + lax.axis_index("c")
        cps = [pltpu.make_async_copy(x_refs[a], land_refs[a].at[me], sems.at[a]) for a in range(na)]
        for cp in cps:
            cp.start()
        for cp in cps:
            cp.wait()

    return _pcall(body, name=name, out_shape=tuple(_sds((N_DEV,) + x.shape, x.dtype) for x in srcs),
                  in_specs=[HBM_SPEC] * (na + len(after)), out_specs=(HBM_SPEC,) * na,
                  scratch_shapes=[pltpu.SemaphoreType.DMA((na,))], pin=False)(*srcs, *after)


def _exchange_start(srcs, lands, *, name, per_peer=False):
    na = len(srcs)

    def body(*refs):
        x_refs, land_refs = refs[:na], refs[na:2 * na]
        send_sems, recv_sems = refs[2 * na], refs[2 * na + 1]
        token = refs[-1]
        mx, my, mc = lax.axis_index("x"), lax.axis_index("y"), lax.axis_index("c")
        me = 4 * mx + 2 * my + mc
        peers = [(mx, my, 1 - mc)]
        for px, py in ((1 - mx, my), (mx, 1 - my), (1 - mx, 1 - my)):
            peers += [(px, py, mc), (px, py, 1 - mc)]
        for a in range(na):
            for peer in peers:
                src = x_refs[a].at[4 * peer[0] + 2 * peer[1] + peer[2]] if per_peer else x_refs[a]
                pltpu.make_async_remote_copy(src_ref=src, dst_ref=land_refs[a].at[me], send_sem=send_sems.at[a],
                                             recv_sem=recv_sems.at[a], device_id=peer, device_id_type=MESH).start()
        token[...] = jnp.zeros_like(token)

    hbm = lambda x: pltpu.HBM(x.shape, x.dtype)
    out_shape = ((pltpu.SemaphoreType.DMA((na,)), pltpu.SemaphoreType.DMA((na,))) + tuple(hbm(x) for x in srcs)
                 + tuple(hbm(x) for x in lands) + (_sds((8, 128), F32),))
    params = pltpu.CompilerParams(has_side_effects=DATAFLOW_EFFECT)
    pin = lambda x: pltpu.with_memory_space_constraint(x, pltpu.HBM)
    return pl.pallas_call(body, name=name, out_shape=out_shape, in_specs=[HBM_SPEC] * (2 * na),
                          out_specs=(SEM_SPEC, SEM_SPEC) + (HBM_SPEC,) * (2 * na) + (pl.BlockSpec(memory_space=pltpu.VMEM),),
                          input_output_aliases={i: 2 + i for i in range(2 * na)}, compiler_params=params)(
                              *[pin(x) for x in srcs], *[pin(x) for x in lands])


def _exchange_wait(started, after, *, name):
    send_sems, recv_sems, *bufs, _ = started
    na = len(bufs) // 2

    def body(*refs):
        land_refs = refs[na:2 * na]
        s_sems, r_sems = refs[2 * na], refs[2 * na + 1]
        me = (lax.axis_index("x"), lax.axis_index("y"), lax.axis_index("c"))
        for a in range(na):
            seven = land_refs[a].at[pl.ds(0, N_DEV - 1)]
            cp = pltpu.make_async_remote_copy(src_ref=seven, dst_ref=seven, send_sem=s_sems.at[a], recv_sem=r_sems.at[a],
                                              device_id=me, device_id_type=MESH)
            cp.wait_send()
            cp.wait_recv()

    hbm = lambda x: pltpu.HBM(x.shape, x.dtype)
    params = pltpu.CompilerParams(has_side_effects=DATAFLOW_EFFECT)
    outs = pl.pallas_call(body, name=name, out_shape=tuple(hbm(x) for x in bufs),
                          in_specs=[HBM_SPEC] * (2 * na) + [SEM_SPEC, SEM_SPEC, HBM_SPEC],
                          out_specs=(HBM_SPEC,) * (2 * na), input_output_aliases={i: i for i in range(2 * na)},
                          compiler_params=params)(*bufs, send_sems, recv_sems, after)
    return outs[:na], outs[na:]


def _pair_swap(grads, *, name):
    na = len(grads)

    def body(*refs):
        g_refs, recv_refs = refs[:na], refs[na:2 * na]
        send_sems, recv_sems = refs[2 * na:]
        mx, my, mc = lax.axis_index("x"), lax.axis_index("y"), lax.axis_index("c")
        sibling = (mx, my, 1 - mc)
        for a in range(na):
            for q in range(4):
                pltpu.make_async_remote_copy(src_ref=g_refs[a].at[q, 1 - mc], dst_ref=recv_refs[a].at[q],
                                             send_sem=send_sems.at[a], recv_sem=recv_sems.at[a],
                                             device_id=sibling, device_id_type=MESH).start()
        for a in range(na):
            pltpu.make_async_remote_copy(src_ref=recv_refs[a], dst_ref=recv_refs[a], send_sem=send_sems.at[a],
                                         recv_sem=recv_sems.at[a], device_id=sibling, device_id_type=MESH).wait()

    half = tuple(_sds((4,) + g.shape[2:], g.dtype) for g in grads)
    return _pcall(body, name=name, out_shape=half, in_specs=[HBM_SPEC] * na, out_specs=(HBM_SPEC,) * na,
                  scratch_shapes=[pltpu.SemaphoreType.DMA((na,)), pltpu.SemaphoreType.DMA((na,))])(*grads)


def _add_slabs(grads, recv, core, *, name):
    na = len(grads)

    def body(core_ref, *refs):
        for a in range(na):
            refs[2 * na + a][...] = (refs[a][...].astype(F32) + refs[na + a][...].astype(F32)).astype(BF16)

    own_specs = [pl.BlockSpec((None, None) + x.shape[2:], lambda q, core_ref: (q, core_ref[0], 0, 0)) for x in grads]
    specs = [pl.BlockSpec((None,) + x.shape[1:], lambda q, core_ref: (q, 0, 0)) for x in recv]
    blk = sum(_nbytes(x.shape[1:], F32) for x in recv)
    grid_spec = pltpu.PrefetchScalarGridSpec(num_scalar_prefetch=1, grid=(4,), in_specs=own_specs + specs,
                                             out_specs=tuple(specs))
    params = pltpu.CompilerParams(dimension_semantics=("parallel",), vmem_limit_bytes=_vmem_limit(2 * blk))
    return pl.pallas_call(body, name=name, out_shape=tuple(_sds(x.shape, BF16) for x in recv), grid_spec=grid_spec,
                          compiler_params=params)(core, *grads, *recv)


def _chip_exchange(parts, *, name):
    na = len(parts)

    def body(*refs):
        p_refs, out_refs = refs[:na], refs[na:2 * na]
        send_sems, recv_sems, local_sems = refs[2 * na:]
        mx, my, mc = lax.axis_index("x"), lax.axis_index("y"), lax.axis_index("c")
        mine_q = 2 * mx + my
        chips = [(1 - mx, my), (mx, 1 - my), (1 - mx, 1 - my)]
        owns = [pltpu.make_async_copy(p_refs[a].at[mine_q], out_refs[a].at[mine_q], local_sems.at[a]) for a in range(na)]
        for cp in owns:
            cp.start()
        sends = []
        for a in range(na):
            for k, chip in enumerate(chips):
                sends.append(pltpu.make_async_remote_copy(
                    src_ref=p_refs[a].at[2 * chip[0] + chip[1]], dst_ref=out_refs[a].at[mine_q],
                    send_sem=send_sems.at[k, a], recv_sem=recv_sems.at[k, a], device_id=(*chip, mc), device_id_type=MESH))
        for cp in sends:
            cp.start()
        for a in range(na):
            for k, chip in enumerate(chips):
                pltpu.make_async_remote_copy(
                    src_ref=p_refs[a].at[mine_q], dst_ref=out_refs[a].at[2 * chip[0] + chip[1]],
                    send_sem=send_sems.at[k, a], recv_sem=recv_sems.at[k, a], device_id=(*chip, mc),
                    device_id_type=MESH).wait_recv()
        for cp in sends:
            cp.wait_send()
        for cp in owns:
            cp.wait()

    return _pcall(body, name=name, out_shape=tuple(_sds(x.shape, x.dtype) for x in parts), in_specs=[HBM_SPEC] * na,
                  out_specs=(HBM_SPEC,) * na,
                  scratch_shapes=[pltpu.SemaphoreType.DMA((3, na)), pltpu.SemaphoreType.DMA((3, na)),
                                  pltpu.SemaphoreType.DMA((na,))])(*parts)


def _sum_chips(parts, *, name):
    na = len(parts)

    def body(*refs):
        for a in range(na):
            p_ref = refs[a]
            acc = p_ref[0].astype(F32)
            for k in range(1, p_ref.shape[0]):
                acc = acc + p_ref[k].astype(F32)
            refs[na + a][...] = acc

    half = lambda x: x.shape[1] // 2
    in_specs = [pl.BlockSpec((x.shape[0], half(x), x.shape[2]), lambda i: (0, i, 0)) for x in parts]
    out_specs = tuple(pl.BlockSpec((half(x), x.shape[2]), lambda i: (i, 0)) for x in parts)
    blk = sum(_nbytes((x.shape[0] + 2, half(x), x.shape[2]), BF16) for x in parts)
    return _pcall(body, name=name, out_shape=tuple(_sds(x.shape[1:], F32) for x in parts), grid=(2,),
                  in_specs=in_specs, out_specs=out_specs, semantics=("parallel",), block_bytes=blk)(*parts)


def _sum_devices(lands, own, me, *, name):
    na = len(lands)

    def body(me_ref, *refs):
        mine = me_ref[0]
        for a in range(na):
            l_ref, o_ref = refs[a], refs[na + a]
            acc = None
            for k in range(N_DEV):
                term = jnp.where(mine == k, o_ref[...], l_ref[k]).astype(F32)
                acc = term if acc is None else acc + term
            refs[2 * na + a][...] = acc

    half = lambda x: x.shape[1] // 2
    land_specs = [pl.BlockSpec((N_DEV, half(x), x.shape[2]), lambda i, me_ref: (0, i, 0)) for x in lands]
    own_specs = [pl.BlockSpec((None, half(x), x.shape[2]), lambda i, me_ref: (me_ref[0], i, 0)) for x in lands]
    out_specs = tuple(pl.BlockSpec((half(x), x.shape[2]), lambda i, me_ref: (i, 0)) for x in lands)
    blk = sum(_nbytes((N_DEV + 3, half(x), x.shape[2]), BF16) for x in lands)
    grid_spec = pltpu.PrefetchScalarGridSpec(num_scalar_prefetch=1, grid=(2,), in_specs=land_specs + own_specs,
                                             out_specs=out_specs)
    params = pltpu.CompilerParams(dimension_semantics=("parallel",), vmem_limit_bytes=_vmem_limit(blk))
    return pl.pallas_call(body, name=name, out_shape=tuple(_sds(x.shape[1:], F32) for x in lands), grid_spec=grid_spec,
                          compiler_params=params)(me, *lands, *own)


def _reduce_layer(grads, l):
    n = lambda s: f"l{l}_{s}"
    views = [g.reshape(4, 2, g.shape[0] // N_DEV, g.shape[1]) for g in grads]
    recv = _pair_swap(views, name=n("reduce_pair"))
    core = lax.axis_index("c").astype(jnp.int32).reshape(1)
    chip_sum = _add_slabs(views, recv, core, name=n("reduce_pair_add"))
    from_chips = _chip_exchange(chip_sum, name=n("reduce_chips"))
    return _sum_chips(from_chips, name=n("reduce_chips_add"))


def _adamw(w, g, m, v, *, name):
    lead, (r, c) = w.shape[:-2], w.shape[-2:]
    tr = _pick(r, (512, 352, 288, 256, 192, 128, 64, 32, 16, 8))
    c1 = 1.0 / (1.0 - ADAM_B1 ** ADAM_STEP)
    c2 = 1.0 / (1.0 - ADAM_B2 ** ADAM_STEP)

    def body(w_ref, g_ref, m_ref, v_ref, d_ref, nm_ref, nv_ref):
        gv = g_ref[...]
        nm = ADAM_B1 * m_ref[...] + (1.0 - ADAM_B1) * gv
        nv = ADAM_B2 * v_ref[...] + (1.0 - ADAM_B2) * jnp.square(gv)
        d_ref[...] = -ADAM_LR * ((nm * c1) / (jnp.sqrt(nv * c2) + ADAM_EPS) + ADAM_WD * w_ref[...])
        nm_ref[...] = nm
        nv_ref[...] = nv

    if lead:
        blk = pl.BlockSpec((None, tr, c), lambda k, i: (k, i, 0))
        grid, sem = (lead[0], r // tr), ("parallel", "parallel")
    else:
        blk = pl.BlockSpec((tr, c), lambda i: (i, 0))
        grid, sem = (r // tr,), ("parallel",)
    out = _sds(w.shape, F32)
    return _pcall(body, name=name, out_shape=(out, out, out), grid=grid, in_specs=[blk] * 4,
                  out_specs=(blk, blk, blk), semantics=sem, block_bytes=7 * _nbytes((tr, c), F32))(w, g, m, v)


def _pack_flat(arrs, rows, cols=1024):
    flat = jnp.concatenate([a.reshape(-1).astype(F32) for a in arrs])
    pad = rows * cols - flat.shape[0]
    return jnp.pad(flat, (0, pad)).reshape(rows, cols)


def _unpack_flat(buf, shapes):
    flat = buf.reshape(-1)
    out, off = [], 0
    for shp in shapes:
        n = 1
        for s in shp:
            n *= s
        out.append(flat[off:off + n].reshape(shp))
        off += n
    return out


def _flat_rows(shapes, cols=1024):
    n = sum(functools.reduce(lambda a, b: a * b, shp, 1) for shp in shapes)
    rows = -(-n // cols)
    return -(-rows // 64) * 64


def _block_diag(w):
    eye = jnp.eye(N_HEADS, dtype=w.dtype)
    return (w[:, :, :, None, :] * eye[None, :, None, :, None]).reshape(w.shape[0], W_GRP, W_GRP)


def _diag_blocks(w):
    w5 = w.reshape(w.shape[0], N_HEADS, HEAD_DIM, N_HEADS, HEAD_DIM)
    return jnp.stack([w5[:, h, :, h, :] for h in range(N_HEADS)], axis=1)


def _stacked_params(w, lbs):
    tril = jnp.tril(jnp.ones((GMLP_CHUNK, GMLP_CHUNK), bool))
    row = lambda a: a.reshape(DEPTH, 1, -1)
    return dict(
        g1=row(w['norm1_g']), g2=row(w['norm2_g']), g3=row(w['norm3_g']),
        a_ln_g=row(w['a_ln_g']), a_ln_b=row(w['a_ln_b']),
        a_wcat=jnp.where(tril, w['a_ws'], 0.0).reshape(DEPTH, N_HEADS * GMLP_CHUNK, GMLP_CHUNK),
        a_bfull=jnp.repeat(jnp.swapaxes(w['a_bs'], 1, 2), HEAD_DIM, axis=2),
        b_cw=w['b_conv_w_full'], b_cb=row(w['b_conv_b']), b_wa=_block_diag(w['b_wa']), b_ba=row(w['b_ba']),
        b_wx=_block_diag(w['b_wx']), b_bx=row(w['b_bx']), b_lam=row(w['b_lam']),
        c_lb=row(lbs), c_ngf=row(jnp.tile(w['c_norm_g'], (1, N_HEADS))),
        d_wd=_block_diag(w['d_w']), d_scale=row(w['d_scale']),
        f_cw=w['ffn_conv_w_full'], f_cb=row(w['ffn_conv_b']),
    )


B_PRM = ('b_cw', 'b_cb', 'b_wa', 'b_ba', 'b_wx', 'b_bx', 'b_lam')


def _layer_fwd(x, p_bf, wb, sp, l):
    n = lambda s: f"l{l}_{s}"
    h, (z,) = _rms_matmul(x, sp['g1'], [wb['w_in']], nt=True, name=n("proj_in"))
    mix = _gmlp_fwd(z, sp['a_ln_g'], sp['a_ln_b'], sp['a_wcat'], sp['a_bfull'], name=n("gmlp"))
    mix, h0s = _rglru_fwd(z, [sp[k] for k in B_PRM], mix, name=n("rglru"))
    mix, sts = _hgrn_fwd(z, sp['c_lb'], sp['c_ngf'], mix, name=n("hgrn"))
    mix = _pool_fwd(z, sp['d_wd'], sp['d_scale'], mix, name=n("pool"))
    x1 = _matmul(mix, wb['w_out'], res=x, name=n("proj_out"))
    h2, (hg, hv) = _rms_matmul(x1, sp['g2'], [wb['w_up_g'], wb['w_up_v']], nt=True, name=n("up"))
    a = _ffn_fwd(hg, hv, sp['f_cw'], sp['f_cb'], name=n("ffn_gate"))
    x2 = _matmul(a, wb['w_down'], res=x1, name=n("down"))
    h3, (gl, pe, x3) = _rms_matmul(x2, sp['g3'], [wb['w_pg']], ple=(p_bf, wb['w_pe']), name=n("ple"))
    saved = dict(x=x, h=h, z=z, h0s=h0s, sts=sts, mix=mix, x1=x1, h2=h2, hg=hg, hv=hv, a=a, x2=x2, h3=h3, gl=gl, pe=pe)
    return x3, saved


def _layer_bwd(dx3, sv, p_bf, wb, sp, l, mid=None):
    n = lambda s: f"l{l}_{s}_bwd"
    gb, gs = {}, {}
    dpe, dgl = _ple_bwd(dx3, sv['gl'], sv['pe'], name=n("ple"))
    gb['w_pe'] = _matmul_tn(dpe, p_bf, name=n("ple_emb_w"))
    gb['w_pg'] = _matmul_tn(sv['h3'], dgl, name=n("ple_gate_w"))
    dx2, dx2b, gs['norm3_g'] = _matmul_rms_bwd(dgl, wb['w_pg'], sv['x2'], sp['g3'], dx3, nt=True, name=n("ple_gate_x"))
    da = _matmul(dx2b, wb['w_down'], nt=True, name=n("down_x"))
    gb['w_down'] = _matmul_tn(sv['a'], dx2b, name=n("down_w"))
    dhg, dhv, gs['f_dwg'], gs['f_dwv'] = _ffn_bwd(sv['hg'], sv['hv'], da, sp['f_cw'], sp['f_cb'], name=n("ffn_gate"))
    gate_rows = _matmul_tn(dhg, sv['h2'], name=n("up_gate_w"), out_rows=2 * D_FF)
    gb['w_up'] = _matmul_tn(dhv, sv['h2'], name=n("up_val_w"), out_rows=2 * D_FF, row_off=D_FF, into=gate_rows)
    if mid is not None:
        sp = mid(gb, sp)
    dh2 = _matmul(dhg, wb['w_up_g'], name=n("up_gate_x"))
    dx1, dx1b, gs['norm2_g'] = _matmul_rms_bwd(dhv, wb['w_up_v'], sv['x1'], sp['g2'], dx2, res=dh2, name=n("up_val_x"))
    dmix = _matmul(dx1b, wb['w_out'], nt=True, name=n("proj_out_x"))
    gb['w_out'] = _matmul_tn(sv['mix'], dx1b, name=n("proj_out_w"))
    z = sv['z']
    dz, gs['a_ln_g'], gs['a_ln_b'], gs['a_wcat'], gs['a_bfull'] = _gmlp_bwd(
        z, dmix, sp['a_ln_g'], sp['a_ln_b'], sp['a_wcat'], sp['a_bfull'], name=n("gmlp"))
    dz, *dbp = _rglru_bwd(z, dmix, sv['h0s'], [sp[k] for k in B_PRM], dz, name=n("rglru"))
    gs.update(zip(B_PRM, dbp))
    dz, gs['c_lb'], gs['c_ngf'] = _hgrn_bwd(z, dmix, sv['sts'], sp['c_lb'], sp['c_ngf'], dz, name=n("hgrn"))
    dz, gs['d_wd'], gs['d_scale'] = _pool_bwd(z, dmix, sp['d_wd'], sp['d_scale'], dz, name=n("pool"))
    gb['w_in'] = _matmul_tn(dz, sv['h'], name=n("proj_in_w"))
    dx0, _, gs['norm1_g'] = _matmul_rms_bwd(dz, wb['w_in'], sv['x'], sp['g1'], dx1, name=n("proj_in_x"))
    return dx0, gb, gs


SMALL_NAMES = [nm for nm in WEIGHT_NAMES if nm not in BIG_NAMES]
COL_SHARDED = ('w_in', 'w_up', 'w_pe')


def _comm_shards(w):
    return [(jnp.swapaxes(w[nm], 1, 2) if nm in COL_SHARDED else w[nm]).astype(BF16) for nm, _, _ in BIG_COMM]


def _full_weights(gathered):
    out = {nm: g.reshape(N_DEV * r, c) for g, (nm, r, c) in zip(gathered, BIG_COMM)}
    halves = out.pop('w_up').reshape(2, D_FF, D_MODEL)
    out['w_up_g'], out['w_up_v'] = _Sel(halves, 0), _Sel(halves, 1)
    return out


def _small_grads(raw):
    nl = len(raw)
    st = {k: jnp.stack([r[k] for r in raw]) for k in raw[0]}
    tril = jnp.tril(jnp.ones((GMLP_CHUNK, GMLP_CHUNK), bool))
    vec = lambda a: a.reshape(nl, -1)
    out = {nm: vec(st[k]) for nm, k in (('norm1_g', 'norm1_g'), ('norm2_g', 'norm2_g'), ('norm3_g', 'norm3_g'),
                                        ('a_ln_g', 'a_ln_g'), ('a_ln_b', 'a_ln_b'), ('b_conv_b', 'b_cb'),
                                        ('b_ba', 'b_ba'), ('b_bx', 'b_bx'), ('b_lam', 'b_lam'), ('c_lb', 'c_lb'),
                                        ('d_scale', 'd_scale'))}
    out['a_ws'] = jnp.where(tril, st['a_wcat'].reshape(nl, N_HEADS, GMLP_CHUNK, GMLP_CHUNK), 0.0)
    out['a_bs'] = jnp.swapaxes(st['a_bfull'].reshape(nl, GMLP_CHUNK, N_HEADS, HEAD_DIM).sum(-1), 1, 2)
    out['b_conv_w'] = st['b_cw']
    out['b_wa'], out['b_wx'], out['d_w'] = _diag_blocks(st['b_wa']), _diag_blocks(st['b_wx']), _diag_blocks(st['d_wd'])
    out['c_norm_g'] = st['c_ngf'].reshape(nl, N_HEADS, HEAD_DIM).sum(1)
    out['ffn_conv_w'] = jnp.concatenate([st['f_dwg'][:, 0:3], st['f_dwv'][:, 0:3]], axis=2)
    out['ffn_conv_b'] = jnp.concatenate([st['f_dwg'][:, 3], st['f_dwv'][:, 3]], axis=1)
    return out


def _step(w, m, v, x, p, target):
    s = x.shape[1]
    dev = 4 * lax.axis_index("x") + 2 * lax.axis_index("y") + lax.axis_index("c")
    xs = x.reshape(s, D_MODEL)

    shards = _comm_shards(w)
    conv_shapes = [w['b_conv_w'].shape, w['ffn_conv_w'].shape]
    conv_rows = _flat_rows(conv_shapes)
    conv_all = _all_gather(_pack_flat([w['b_conv_w'], w['ffn_conv_w']], conv_rows), name="gather_conv_weights")
    parts = [_unpack_flat(conv_all[d], conv_shapes) for d in range(N_DEV)]
    wf = dict(w)
    wf['b_conv_w_full'] = jnp.concatenate([pt[0] for pt in parts], axis=-1)
    wf['ffn_conv_w_full'] = jnp.concatenate([pt[1] for pt in parts], axis=-1)
    lbs = _lbs_fwd(w['c_lb'], name="hgrn_bounds")

    stacked = _stacked_params(wf, lbs)
    p_all = p.reshape(DEPTH, s, PLE_DIM).astype(BF16)
    xl, saved, wbs, sps = xs, [], [], []
    gathered = _gather_layer(shards, 0, name="l0_gather_weights")
    for l in range(DEPTH):
        sp = {k: _Sel(a, l) for k, a in stacked.items()}
        if l + 1 < DEPTH:
            own = [x[l + 1] for x in shards]
            after = [conv_all, *gathered] if l == 0 else [xl]
            lands = _place_own(own, after, name=f"l{l + 1}_gather_place")
            started = _exchange_start(own, lands, name=f"l{l + 1}_gather_start")
            sp['g1'] = stacked['g1'][l] + started[-1][0, 0]
        wb = _full_weights(gathered)
        p_bf = p_all[l]
        xl, sv = _layer_fwd(xl, p_bf, wb, sp, l)
        if l + 1 < DEPTH:
            gathered = _exchange_wait(started, xl, name=f"l{l + 1}_gather_wait")[1]
        saved.append((sv, p_bf))
        wbs.append(wb)
        sps.append(sp)
    loss_part, dx, dfinal = _loss_head(xl, w['final_g'].reshape(1, D_MODEL), target.reshape(s, D_MODEL), name="loss_head")
    loss = lax.psum(loss_part[0, 0], ("x", "y", "c"))

    dev1 = dev.astype(jnp.int32).reshape(1)
    names = [nm for nm, _, _ in BIG_COMM]

    def start_reduce(grads, name):
        views = [g.reshape(N_DEV, g.shape[0] // N_DEV, g.shape[1]) for g in grads]
        return _exchange_start(views, [lax.empty(g.shape, g.dtype) for g in views], name=name, per_peer=True)

    def finish_reduce(started, after, lname):
        own, lands = _exchange_wait(started, after, name=f"{lname}_reduce_wait")
        return _sum_devices(lands, own, dev1, name=f"{lname}_reduce_sum")

    reduced, small = [None] * DEPTH, [None] * DEPTH
    pending = None
    for l in range(DEPTH - 1, 0, -1):
        sv, p_bf = saved[l]
        sp = sps[l]
        if pending is not None:
            sp = dict(sp, g3=stacked['g3'][l] + pending[-1][0, 0])
        dx, gb, small[l] = _layer_bwd(dx, sv, p_bf, wbs[l], sp, l)
        if pending is not None:
            reduced[l + 1] = finish_reduce(pending, dx, f"l{l + 1}")
        pending = start_reduce([gb[nm] for nm in names], f"l{l}_reduce_start")
    early = ('w_up', 'w_down', 'w_pe', 'w_pg')
    mid_started = []

    def mid(gb, sp):
        mid_started.append(start_reduce([gb[nm] for nm in early], "l0_reduce_start"))
        return dict(sp, g2=stacked['g2'][0] + mid_started[0][-1][0, 0])

    upper_names = [nm for nm in SMALL_NAMES if nm != 'final_g']
    low_names = upper_names + ['final_g']
    upper = _small_grads(small[1:])
    upper_shapes = [upper[nm].shape for nm in upper_names]
    upper_packed = [_pack_flat([upper[nm] for nm in upper_names], _flat_rows(upper_shapes))]
    upper_started = _exchange_start(upper_packed, _place_own(upper_packed, [], name="upper_small_grads_place"),
                                    name="upper_small_grads_start")

    sv, p_bf = saved[0]
    g3 = stacked['g3'][0] + pending[-1][0, 0] + upper_started[-1][0, 0]
    dx, gb, small[0] = _layer_bwd(dx, sv, p_bf, wbs[0], dict(sps[0], g3=g3), 0, mid=mid)
    reduced[1] = finish_reduce(pending, dx, "l1")
    late = dict(zip(('w_in', 'w_out'), _reduce_layer([gb['w_in'], gb['w_out']], 0)))
    late.update(zip(early, finish_reduce(mid_started[0], late['w_in'], "l0")))
    reduced[0] = [late[nm] for nm in names]
    grad_x = dx.reshape(1, s, D_MODEL)
    low = _small_grads(small[:1])
    low['final_g'] = dfinal.reshape(D_MODEL)
    low_shapes = [low[nm].shape for nm in low_names]
    low_all = _all_gather(_pack_flat([low[nm] for nm in low_names], _flat_rows(low_shapes)), name="gather_small_grads")
    low_sum = dict(zip(low_names, _unpack_flat(_sum_slots(low_all, name="sum_small_grads"), low_shapes)))
    upper_all = _exchange_wait(upper_started, low_all, name="upper_small_grads_wait")[1][0]
    upper_sum = dict(zip(upper_names, _unpack_flat(_sum_slots(upper_all, name="sum_upper_small_grads"), upper_shapes)))
    gsmall = {nm: jnp.concatenate([low_sum[nm], upper_sum[nm]], axis=0) for nm in upper_names}
    gsmall['c_lb'] = _lbs_bwd(w['c_lb'], gsmall['c_lb'], name="hgrn_bounds_bwd")
    gsmall['final_g'] = low_sum['final_g']
    for nm in ('b_conv_w', 'ffn_conv_w'):
        width = w[nm].shape[-1]
        gsmall[nm] = lax.dynamic_slice_in_dim(gsmall[nm], dev * width, width, axis=2)

    grads, delta, new_m, new_v = {}, {}, {}, {}
    for a, (nm, _, _) in enumerate(BIG_COMM):
        t = (lambda x: jnp.swapaxes(x, 1, 2)) if nm in COL_SHARDED else (lambda x: x)
        g = jnp.stack([reduced[l][a] for l in range(DEPTH)])
        d, nm_, nv_ = _adamw(t(w[nm]), g, t(m[nm]), t(v[nm]), name=f"adamw_{nm}")
        grads[nm], delta[nm], new_m[nm], new_v[nm] = t(g), t(d), t(nm_), t(nv_)

    shapes = [w[nm].shape for nm in SMALL_NAMES]
    rows = _flat_rows(shapes)
    pk = lambda t: _pack_flat([t[nm] for nm in SMALL_NAMES], rows)
    d, nm_, nv_ = _adamw(pk(w), pk(gsmall), pk(m), pk(v), name="adamw_small")
    for nm, dd, mm_, vv_ in zip(SMALL_NAMES, _unpack_flat(d, shapes), _unpack_flat(nm_, shapes), _unpack_flat(nv_, shapes)):
        grads[nm], delta[nm], new_m[nm], new_v[nm] = gsmall[nm], dd, mm_, vv_

    return (loss, grad_x, *[grads[nm] for nm in WEIGHT_NAMES], *[delta[nm] for nm in WEIGHT_NAMES],
            *[new_m[nm] for nm in WEIGHT_NAMES], *[new_v[nm] for nm in WEIGHT_NAMES])


def kernel(x, p, norm1_g, w_in, a_ln_g, a_ln_b, a_ws, a_bs, b_conv_w, b_conv_b, b_wa, b_ba, b_wx, b_bx, b_lam, c_lb, c_norm_g, d_w, d_scale, w_out, norm2_g, w_up, ffn_conv_w, ffn_conv_b, w_down, norm3_g, w_pe, w_pg, final_g, loss_target, m_norm1_g, m_w_in, m_a_ln_g, m_a_ln_b, m_a_ws, m_a_bs, m_b_conv_w, m_b_conv_b, m_b_wa, m_b_ba, m_b_wx, m_b_bx, m_b_lam, m_c_lb, m_c_norm_g, m_d_w, m_d_scale, m_w_out, m_norm2_g, m_w_up, m_ffn_conv_w, m_ffn_conv_b, m_w_down, m_norm3_g, m_w_pe, m_w_pg, m_final_g, v_norm1_g, v_w_in, v_a_ln_g, v_a_ln_b, v_a_ws, v_a_bs, v_b_conv_w, v_b_conv_b, v_b_wa, v_b_ba, v_b_wx, v_b_bx, v_b_lam, v_c_lb, v_c_norm_g, v_d_w, v_d_scale, v_w_out, v_norm2_g, v_w_up, v_ffn_conv_w, v_ffn_conv_b, v_w_down, v_norm3_g, v_w_pe, v_w_pg, v_final_g):
    w = dict(norm1_g=norm1_g, w_in=w_in, a_ln_g=a_ln_g, a_ln_b=a_ln_b, a_ws=a_ws, a_bs=a_bs, b_conv_w=b_conv_w, b_conv_b=b_conv_b, b_wa=b_wa, b_ba=b_ba, b_wx=b_wx, b_bx=b_bx, b_lam=b_lam, c_lb=c_lb, c_norm_g=c_norm_g, d_w=d_w, d_scale=d_scale, w_out=w_out, norm2_g=norm2_g, w_up=w_up, ffn_conv_w=ffn_conv_w, ffn_conv_b=ffn_conv_b, w_down=w_down, norm3_g=norm3_g, w_pe=w_pe, w_pg=w_pg, final_g=final_g)
    m = dict(norm1_g=m_norm1_g, w_in=m_w_in, a_ln_g=m_a_ln_g, a_ln_b=m_a_ln_b, a_ws=m_a_ws, a_bs=m_a_bs, b_conv_w=m_b_conv_w, b_conv_b=m_b_conv_b, b_wa=m_b_wa, b_ba=m_b_ba, b_wx=m_b_wx, b_bx=m_b_bx, b_lam=m_b_lam, c_lb=m_c_lb, c_norm_g=m_c_norm_g, d_w=m_d_w, d_scale=m_d_scale, w_out=m_w_out, norm2_g=m_norm2_g, w_up=m_w_up, ffn_conv_w=m_ffn_conv_w, ffn_conv_b=m_ffn_conv_b, w_down=m_w_down, norm3_g=m_norm3_g, w_pe=m_w_pe, w_pg=m_w_pg, final_g=m_final_g)
    v = dict(norm1_g=v_norm1_g, w_in=v_w_in, a_ln_g=v_a_ln_g, a_ln_b=v_a_ln_b, a_ws=v_a_ws, a_bs=v_a_bs, b_conv_w=v_b_conv_w, b_conv_b=v_b_conv_b, b_wa=v_b_wa, b_ba=v_b_ba, b_wx=v_b_wx, b_bx=v_b_bx, b_lam=v_b_lam, c_lb=v_c_lb, c_norm_g=v_c_norm_g, d_w=v_d_w, d_scale=v_d_scale, w_out=v_w_out, norm2_g=v_norm2_g, w_up=v_w_up, ffn_conv_w=v_ffn_conv_w, ffn_conv_b=v_ffn_conv_b, w_down=v_w_down, norm3_g=v_norm3_g, w_pe=v_w_pe, w_pg=v_w_pg, final_g=v_final_g)
    return _step(w, m, v, x, p, loss_target)
```

```python
import functools

import jax
import jax.numpy as jnp
from jax import lax
from jax.experimental import pallas as pl
from jax.experimental.pallas import tpu as pltpu

F32 = jnp.float32
BF16 = jnp.bfloat16
MESH = pl.DeviceIdType.MESH

D_MODEL = 1024
DEPTH = 4
PLE_DIM = 256
W_GRP = 256
N_HEADS = 4
HEAD_DIM = 64
GMLP_CHUNK = 128
RGLRU_C = 8.0
HGRN_CHUNK = 64
HGRN_SUB = 16
HGRN_STEP_CHUNKS = 4
POOL_WINDOWS = (2, 4, 8, 16)
D_FF = 2816
D_PROJ = 2304
EPS = 1e-6
ADAM_LR = 0.001
ADAM_B1 = 0.9
ADAM_B2 = 0.999
ADAM_EPS = 1e-08
ADAM_WD = 0.01
ADAM_STEP = 10

N_DEV = 8
MIB = 2 ** 20
V7X_VMEM_BYTES = 64 * MIB
HGRN_EXP_CLAMP = 60.0

WEIGHT_NAMES = ['norm1_g', 'w_in', 'a_ln_g', 'a_ln_b', 'a_ws', 'a_bs', 'b_conv_w', 'b_conv_b', 'b_wa', 'b_ba', 'b_wx',
                'b_bx', 'b_lam', 'c_lb', 'c_norm_g', 'd_w', 'd_scale', 'w_out', 'norm2_g', 'w_up', 'ffn_conv_w',
                'ffn_conv_b', 'w_down', 'norm3_g', 'w_pe', 'w_pg', 'final_g']
BIG_NAMES = ('w_in', 'w_out', 'w_up', 'w_down', 'w_pe', 'w_pg')


def _vmem_limit(block_bytes):
    want = 2 * block_bytes + 24 * MIB
    return int(min(max(want, 32 * MIB), V7X_VMEM_BYTES - 8 * MIB))


def _in_hbm(x):
    return pltpu.with_memory_space_constraint(x, pltpu.HBM)


def _out_hbm(s):
    return pltpu.HBM(s.shape, s.dtype)


def _pcall(body, *, name, out_shape, grid=None, in_specs=None, out_specs=None, scratch_shapes=(),
           semantics=None, block_bytes=0, aliases=None, pin=True):
    kw = {} if aliases is None else {"input_output_aliases": aliases}
    if pin:
        out_shape = tuple(_out_hbm(s) for s in out_shape) if isinstance(out_shape, (tuple, list)) else _out_hbm(out_shape)
    if grid is not None:
        kw["grid"] = grid
    if in_specs is not None:
        kw["in_specs"] = in_specs
    if out_specs is not None:
        kw["out_specs"] = out_specs
    params = pltpu.CompilerParams(dimension_semantics=semantics, vmem_limit_bytes=_vmem_limit(block_bytes))
    call = pl.pallas_call(body, name=name, out_shape=out_shape, scratch_shapes=list(scratch_shapes),
                          compiler_params=params, **kw)
    return (lambda *args: call(*[_in_hbm(a) for a in args])) if pin else call


def _pick(n, cands):
    for c in cands:
        if n % c == 0:
            return c
    return n


def _nbytes(shape, dtype):
    n = 1
    for s in shape:
        n *= s
    return n * jnp.dtype(dtype).itemsize


def _sds(shape, dtype):
    return jax.ShapeDtypeStruct(tuple(shape), dtype)


class _Sel:
    def __init__(self, arr, *idx):
        self.arr, self.idx = arr, tuple(idx)
        self.shape = arr.shape[len(idx):]
        self.ndim = len(self.shape)
        self.dtype = arr.dtype


def _arr(a):
    return a.arr if isinstance(a, _Sel) else a


def _spec(a, block=None, index=None):
    block = tuple(a.shape) if block is None else tuple(block)
    index = (lambda *g: (0,) * len(block)) if index is None else index
    if isinstance(a, _Sel):
        lead = a.idx
        return pl.BlockSpec((None,) * len(lead) + block, lambda *g: lead + tuple(index(*g)))
    return pl.BlockSpec(block, lambda *g: tuple(index(*g)))


def _ospec(a):
    return pl.BlockSpec(tuple(a.shape), lambda *g: (0,) * a.ndim)


def _rows_of(shape):
    return lax.broadcasted_iota(jnp.int32, shape, 0)


def _lanes_of(shape):
    return lax.broadcasted_iota(jnp.int32, shape, 1)


def _sdn(x, k, fill):
    n = x.shape[0]
    return jnp.where(_rows_of(x.shape) >= k, pltpu.roll(x, k % n, 0), fill)


def _sup(x, k, fill):
    n = x.shape[0]
    return jnp.where(_rows_of(x.shape) < n - k, pltpu.roll(x, (n - k) % n, 0), fill)


@functools.partial(jax.custom_vjp, nondiff_argnums=(1,))
def _shift_dn(x, k):
    return pltpu.roll(x, k, 0)


def _shift_dn_fwd(x, k):
    return pltpu.roll(x, k, 0), None


def _shift_dn_bwd(k, _, g):
    return (pltpu.roll(g, g.shape[0] - k, 0),)


_shift_dn.defvjp(_shift_dn_fwd, _shift_dn_bwd)


def _lin_scan_impl(a, b, h0):
    n = a.shape[0]
    aa, bb = a, b
    k = 1
    while k < n:
        bb = aa * _sdn(bb, k, 0.0) + bb
        aa = aa * _sdn(aa, k, 1.0)
        k *= 2
    return bb + aa * h0


@jax.custom_vjp
def _lin_scan(a, b, h0):
    return _lin_scan_impl(a, b, h0)


def _lin_scan_fwd(a, b, h0):
    h = _lin_scan_impl(a, b, h0)
    return h, (a, h, h0)


def _lin_scan_bwd(res, g):
    a, h, h0 = res
    n = a.shape[0]
    cc, gg = _sup(a, 1, 0.0), g
    k = 1
    while k < n:
        gg = gg + cc * _sup(gg, k, 0.0)
        cc = cc * _sup(cc, k, 1.0)
        k *= 2
    first = _rows_of(a.shape) == 0
    hprev = jnp.where(first, h0, _sdn(h, 1, 0.0))
    dh0 = jnp.sum(jnp.where(first, a * gg, 0.0), axis=0, keepdims=True)
    return gg * hprev, gg, dh0


_lin_scan.defvjp(_lin_scan_fwd, _lin_scan_bwd)


def _cumsum_sub_impl(x):
    pos = _rows_of(x.shape) % HGRN_SUB
    k = 1
    while k < HGRN_SUB:
        x = x + jnp.where(pos >= k, pltpu.roll(x, k, 0), 0.0)
        k *= 2
    return x


@jax.custom_vjp
def _cumsum_sub(x):
    return _cumsum_sub_impl(x)


def _cumsum_sub_fwd(x):
    return _cumsum_sub_impl(x), None


def _cumsum_sub_bwd(_, g):
    n = g.shape[0]
    pos = _rows_of(g.shape) % HGRN_SUB
    k = 1
    while k < HGRN_SUB:
        g = g + jnp.where(pos < HGRN_SUB - k, pltpu.roll(g, n - k, 0), 0.0)
        k *= 2
    return (g,)


_cumsum_sub.defvjp(_cumsum_sub_fwd, _cumsum_sub_bwd)


def _dot(a, b, ca, cb):
    return lax.dot_general(a.astype(BF16), b.astype(BF16), (((ca,), (cb,)), ((), ())), preferred_element_type=F32)


@jax.custom_vjp
def _mm(a, b):
    return _dot(a, b, 1, 0)


def _mm_fwd(a, b):
    return _dot(a, b, 1, 0), (a, b)


def _mm_bwd(res, g):
    a, b = res
    return _dot(g, b, 1, 1), _dot(a, g, 0, 0)


_mm.defvjp(_mm_fwd, _mm_bwd)


@jax.custom_vjp
def _mm_nt(a, b):
    return _dot(a, b, 1, 1)


def _mm_nt_fwd(a, b):
    return _dot(a, b, 1, 1), (a, b)


def _mm_nt_bwd(res, g):
    a, b = res
    return _dot(g, b, 1, 0), _dot(g, a, 0, 0)


_mm_nt.defvjp(_mm_nt_fwd, _mm_nt_bwd)


@jax.custom_vjp
def _mm_tn(a, b):
    return _dot(a, b, 0, 0)


def _mm_tn_fwd(a, b):
    return _dot(a, b, 0, 0), (a, b)


def _mm_tn_bwd(res, g):
    a, b = res
    return _dot(b, g, 1, 1), _dot(a, g, 1, 0)


_mm_tn.defvjp(_mm_tn_fwd, _mm_tn_bwd)


def _head_mask(shape, h):
    return (_lanes_of(shape) // HEAD_DIM) == h


def _stack_heads(x):
    return jnp.concatenate([jnp.where(_head_mask(x.shape, h), x, 0.0) for h in range(N_HEADS)], axis=0)


def _unstack_heads(p):
    r = p.shape[0] // N_HEADS
    out = None
    for h in range(N_HEADS):
        blk = p[h * r:(h + 1) * r]
        term = jnp.where(_head_mask(blk.shape, h), blk, 0.0)
        out = term if out is None else out + term
    return out


def _segmean_impl(x):
    n = x.shape[1]
    same = (lax.broadcasted_iota(jnp.int32, (n, n), 0) // HEAD_DIM) == (lax.broadcasted_iota(jnp.int32, (n, n), 1) // HEAD_DIM)
    m = jnp.where(same, 1.0 / HEAD_DIM, 0.0).astype(BF16)
    hi = x.astype(BF16)
    lo = (x - hi.astype(F32)).astype(BF16)
    dn = (((1,), (0,)), ((), ()))
    return (lax.dot_general(hi, m, dn, preferred_element_type=F32)
            + lax.dot_general(lo, m, dn, preferred_element_type=F32))


@jax.custom_vjp
def _segmean(x):
    return _segmean_impl(x)


def _segmean_fwd(x):
    return _segmean_impl(x), None


def _segmean_bwd(_, g):
    return (_segmean_impl(g),)


_segmean.defvjp(_segmean_fwd, _segmean_bwd)


def _log1p(u):
    w = 1.0 + u
    return jnp.where(w == 1.0, u, jnp.log(w) * (u / (w - 1.0)))


def _softplus(y):
    return jnp.maximum(y, 0.0) + _log1p(jnp.exp(-jnp.abs(y)))


def _rms(x, g):
    return x * lax.rsqrt(jnp.mean(x * x, axis=-1, keepdims=True) + EPS) * g


def _gmlp_chunk(zu, zv, ln_g, ln_b, wcat, bfull):
    u = jax.nn.gelu(zu)
    v = jax.nn.gelu(zv)
    mu = jnp.mean(v, axis=-1, keepdims=True)
    var = jnp.mean(jnp.square(v - mu), axis=-1, keepdims=True)
    vn = (v - mu) * lax.rsqrt(var + EPS) * ln_g + ln_b
    sv = _unstack_heads(_mm(wcat, vn)) + bfull
    return u * sv


def _rglru_tile(xb_ext, gb, h0, cw, cb, wa, ba, wx, bx, lam):
    xc = (cb + cw[0:1] * _shift_dn(xb_ext, 3) + cw[1:2] * _shift_dn(xb_ext, 2) + cw[2:3] * _shift_dn(xb_ext, 1)
          + cw[3:4] * xb_ext)[8:]
    r = jax.nn.sigmoid(_mm(xc, wa) + ba)
    i = jax.nn.sigmoid(_mm(xc, wx) + bx)
    log_a = (-RGLRU_C) * r * _softplus(-lam)
    a = jnp.exp(log_a)
    mult = jnp.sqrt(-jnp.tanh(log_a) * (a * a + 1.0))
    h = _lin_scan(a, mult * (i * xc), h0)
    y = h * jax.nn.gelu(gb)
    h_last = jnp.sum(jnp.where(_rows_of(h.shape) == h.shape[0] - 1, h, 0.0), axis=0, keepdims=True)
    return y, h_last


def _pool_tile(xd_ext, inv, wd, scale):
    s1 = xd_ext + _shift_dn(xd_ext, 1)
    s2 = s1 + _shift_dn(s1, 2)
    s3 = s2 + _shift_dn(s2, 4)
    s4 = s3 + _shift_dn(s3, 8)
    grp = _lanes_of(xd_ext.shape) // HEAD_DIM
    win = jnp.where(grp == 0, s1, jnp.where(grp == 1, s2, jnp.where(grp == 2, s3, s4)))
    pooled = win[16:] * inv - xd_ext[16:]
    return _mm(pooled, wd) * scale


def _hgrn_chunk(q, f, i, g, st, lb, ngf):
    n = q.shape[0]
    nsub = n // HGRN_SUB
    qs = jax.nn.silu(q)
    fg = lb + (1.0 - lb) * jax.nn.sigmoid(f)
    lf = jnp.log(fg)
    k = 1.0 - fg
    bl = _cumsum_sub(lf)
    row = _rows_of(q.shape)
    blk = row // HGRN_SUB
    betas = [jnp.zeros_like(lb)]
    for s in range(nsub):
        tot = jnp.sum(jnp.where(row == s * HGRN_SUB + HGRN_SUB - 1, bl, 0.0), axis=0, keepdims=True)
        betas.append(betas[-1] + tot)
    b_end = betas[nsub]
    beta_full = jnp.zeros_like(q)
    for s in range(1, nsub):
        beta_full = jnp.where(blk == s, betas[s], beta_full)
    qh = qs * jnp.exp(bl)
    qt = qh * jnp.exp(beta_full)
    b_all = beta_full + bl
    kt = k * jnp.exp(b_end - b_all)
    outs = []
    for s in range(nsub):
        kh = k * jnp.exp(jnp.minimum(betas[s] - b_all, HGRN_EXP_CLAMP))
        qstk = _stack_heads(qh[s * HGRN_SUB:(s + 1) * HGRN_SUB])
        att = _mm_nt(qstk, kh)
        ar = _rows_of(att.shape) % HGRN_SUB + s * HGRN_SUB
        att = jnp.where(_lanes_of(att.shape) <= ar, att, 0.0)
        outs.append(_unstack_heads(_mm(att, i)))
    o = jnp.concatenate(outs, axis=0) + _mm_nt(qt, st)
    same = (_rows_of(st.shape) // HEAD_DIM) == (_lanes_of(st.shape) // HEAD_DIM)
    st_new = st * jnp.exp(b_end) + jnp.where(same, _mm_tn(i, kt), 0.0)
    on = o * lax.rsqrt(_segmean(o * o) + EPS) * ngf
    return on * jax.nn.silu(g), st_new


def _ffn_tile(eg, ev, wg, bg, wv, bv):
    gt = (bg + wg[0:1] * _shift_dn(eg, 2) + wg[1:2] * _shift_dn(eg, 1) + wg[2:3] * eg)[8:]
    val = (bv + wv[0:1] * _shift_dn(ev, 2) + wv[1:2] * _shift_dn(ev, 1) + wv[2:3] * ev)[8:]
    return jax.nn.gelu(gt) * val


MXU_WIDTH = 256
MATMUL_BLOCK_BUDGET = 18 * MIB


def _matmul_tiles(m, k, n, a_dtype, b_dtype, out_dtype, has_res):
    best = None
    for tm in (2048, 1024, 512, 256):
        if m % tm:
            continue
        for tn in (1024, 768, 1408, 512, 256, 128):
            if n % tn:
                continue
            blk = (_nbytes((tm, k), a_dtype) + _nbytes((k, tn), b_dtype) + _nbytes((tm, tn), out_dtype)
                   + (_nbytes((tm, tn), F32) if has_res else 0))
            if blk > MATMUL_BLOCK_BUDGET:
                continue
            waste = -(-tn // MXU_WIDTH) * MXU_WIDTH / tn
            cost = (m // tm) * (n // tn) + 64 * (waste - 1.0)
            if best is None or cost < best[0]:
                best = (cost, tm, tn, blk)
    assert best is not None, (m, k, n)
    return best[1:]


def _matmul(a, b, *, name, nt=False, res=None, out_dtype=F32):
    m, k = a.shape
    n = b.shape[0] if nt else b.shape[1]
    tm, tn, blk = _matmul_tiles(m, k, n, a.dtype, b.dtype, out_dtype, res is not None)
    dims = (((1,), (1,)), ((), ())) if nt else (((1,), (0,)), ((), ()))

    def body(*refs):
        if res is None:
            a_ref, b_ref, o_ref = refs
        else:
            a_ref, b_ref, r_ref, o_ref = refs
        acc = lax.dot_general(a_ref[...], b_ref[...], dims, preferred_element_type=F32)
        if res is not None:
            acc = acc + r_ref[...]
        o_ref[...] = acc.astype(out_dtype)

    in_specs = [pl.BlockSpec((tm, k), lambda i, j: (i, 0)),
                _spec(b, (tn, k), lambda i, j: (j, 0)) if nt else _spec(b, (k, tn), lambda i, j: (0, j))]
    args = [a, _arr(b)]
    if res is not None:
        in_specs.append(pl.BlockSpec((tm, tn), lambda i, j: (i, j)))
        args.append(res)
    return _pcall(body, name=name, out_shape=_sds((m, n), out_dtype), grid=(m // tm, n // tn), in_specs=in_specs,
                  out_specs=pl.BlockSpec((tm, tn), lambda i, j: (i, j)), semantics=("parallel", "parallel"),
                  block_bytes=blk + _nbytes((tm, tn), F32))(*args)


def _matmul_rms_bwd(a, b, x, g, dres, *, name, nt=False, res=None):
    m, k = a.shape
    n = b.shape[0] if nt else b.shape[1]
    tm = _pick(m, (512, 256))
    dims = (((1,), (1,)), ((), ())) if nt else (((1,), (0,)), ((), ()))

    def body(*refs):
        a_ref, b_ref, x_ref, g_ref, dr_ref = refs[:5]
        dx_ref, dxb_ref, dg_ref = refs[-3:]
        dh = lax.dot_general(a_ref[...], b_ref[...], dims, preferred_element_type=F32)
        if res is not None:
            dh = dh + refs[5][...]
        _, vjp = jax.vjp(_rms, x_ref[...], g_ref[...])
        dxn, dg = vjp(dh)
        dx = dr_ref[...] + dxn
        dx_ref[...] = dx
        dxb_ref[...] = dx.astype(BF16)
        _acc_out(dg_ref, dg, pl.program_id(0) == 0)

    row = pl.BlockSpec((tm, n), lambda i: (i, 0))
    vec = pl.BlockSpec((1, n), lambda i: (0, 0))
    in_specs = [pl.BlockSpec((tm, k), lambda i: (i, 0)),
                _spec(b, (n, k), lambda i: (0, 0)) if nt else _spec(b, (k, n), lambda i: (0, 0)), row, _spec(g), row]
    args = [a, _arr(b), x, _arr(g), dres]
    if res is not None:
        in_specs.append(row)
        args.append(res)
    blk = _nbytes((tm, k), a.dtype) + _nbytes((k, n), b.dtype) + 6 * _nbytes((tm, n), F32)
    return _pcall(body, name=name, out_shape=(_sds((m, n), F32), _sds((m, n), BF16), _sds((1, n), F32)), grid=(m // tm,),
                  in_specs=in_specs, out_specs=(row, row, vec), semantics=("arbitrary",), block_bytes=blk)(*args)


def _matmul_tn(a, b, *, name, out_dtype=BF16, out_rows=None, row_off=0, into=None):
    m, k1 = a.shape
    n = b.shape[1]
    tk = _pick(k1, (512, 256, 128))
    off = row_off // tk
    assert off * tk == row_off

    def body(a_ref, b_ref, *rest):
        rest[-1][...] = lax.dot_general(a_ref[...], b_ref[...], (((0,), (0,)), ((), ())),
                                        preferred_element_type=F32).astype(out_dtype)

    blk = 2 * _nbytes((m, tk), a.dtype) + _nbytes((m, n), b.dtype) + _nbytes((tk, n), F32)
    in_specs = [pl.BlockSpec((m, tk), lambda i: (0, i)), pl.BlockSpec((m, n), lambda i: (0, 0))]
    args = [a, b]
    if into is not None:
        in_specs.append(HBM_SPEC)
        args.append(into)
    return _pcall(body, name=name, out_shape=_sds((out_rows or k1, n), out_dtype), grid=(k1 // tk,), in_specs=in_specs,
                  out_specs=pl.BlockSpec((tk, n), lambda i: (i + off, 0)), semantics=("parallel",), block_bytes=blk,
                  aliases=None if into is None else {2: 0})(*args)


def _rms_matmul(x, g, bs, *, name, nt=False, ple=None):
    m, d = x.shape
    n = bs[0].shape[0] if nt else bs[0].shape[1]
    nb = len(bs)
    nout = nb if ple is None else 3
    best = None
    for tm_c in (1024, 512, 256):
        for tn_c in (1408, 1024, 768, 512, 256, 128):
            if m % tm_c or n % tn_c:
                continue
            blk_c = (_nbytes((tm_c, d), F32) + 2 * _nbytes((tm_c, d), BF16) + nb * _nbytes((d, tn_c), BF16)
                     + (nout + 1) * _nbytes((tm_c, tn_c), F32))
            steps = (m // tm_c) * (n // tn_c)
            if blk_c <= MATMUL_BLOCK_BUDGET and (best is None or steps < best[0]):
                best = (steps, tm_c, tn_c, blk_c)
    _, tm, tn, blk = best
    dims = (((1,), (1,)), ((), ())) if nt else (((1,), (0,)), ((), ()))

    def body(*refs):
        x_ref, g_ref, b_refs = refs[0], refs[1], refs[2:2 + nb]
        rest = refs[2 + nb:]
        h_scr = rest[-1]
        j = pl.program_id(1)

        @pl.when(j == 0)
        def _():
            h = _rms(x_ref[...], g_ref[...]).astype(BF16)
            h_scr[...] = h
            rest[-2 - nb - (2 if ple else 0)][...] = h

        h = h_scr[...]
        if ple is None:
            for k in range(nb):
                rest[-1 - nb + k][...] = lax.dot_general(h, b_refs[k][...], dims, preferred_element_type=F32)
        else:
            p_ref, wpe_ref, xt_ref = rest[0], rest[1], rest[2]
            gl_ref, pe_ref, out_ref = rest[-4], rest[-3], rest[-2]
            gl = lax.dot_general(h, b_refs[0][...], dims, preferred_element_type=F32)
            pe = lax.dot_general(p_ref[...], wpe_ref[...], (((1,), (1,)), ((), ())), preferred_element_type=F32)
            gl_ref[...] = gl
            pe_ref[...] = pe
            out_ref[...] = xt_ref[...] + pe * jax.nn.sigmoid(gl)

    row = pl.BlockSpec((tm, d), lambda i, j: (i, 0))
    tile = pl.BlockSpec((tm, tn), lambda i, j: (i, j))
    in_specs = [row, _spec(g)] + [_spec(b, (tn, d), lambda i, j: (j, 0)) if nt else _spec(b, (d, tn), lambda i, j: (0, j))
                                  for b in bs]
    args = [x, _arr(g)] + [_arr(b) for b in bs]
    out_shape, out_specs = [_sds((m, d), BF16)], [row]
    if ple is None:
        out_shape += [_sds((m, n), F32)] * nb
        out_specs += [tile] * nb
    else:
        p, wpe = ple
        in_specs += [pl.BlockSpec((tm, p.shape[1]), lambda i, j: (i, 0)), _spec(wpe, (tn, p.shape[1]), lambda i, j: (j, 0)),
                     tile]
        args += [p, _arr(wpe), x]
        out_shape += [_sds((m, n), F32)] * 3
        out_specs += [tile] * 3
    outs = _pcall(body, name=name, out_shape=tuple(out_shape), grid=(m // tm, n // tn), in_specs=in_specs,
                  out_specs=tuple(out_specs), scratch_shapes=[pltpu.VMEM((tm, d), BF16)],
                  semantics=("parallel", "arbitrary"), block_bytes=blk)(*args)
    return outs[0], list(outs[1:])


def _up_ffn_fwd(x, g, wg, wv, cwf, cbf, *, name):
    m, d = x.shape
    n = wg.shape[0]
    tm = _pick(m, (256, 128))
    tn = _pick(n, (1408, 256, 128))
    nj = n // tn
    dims = (((1,), (1,)), ((), ()))

    def body(x_ref, g_ref, wg_ref, wv_ref, tg_ref, bg_ref, tv_ref, bv_ref, h_ref, hg_ref, hv_ref, a_ref, cg_scr, cv_scr):
        i = pl.program_id(1)
        h = _rms(x_ref[...], g_ref[...]).astype(BF16)
        h_ref[...] = h
        hg = lax.dot_general(h, wg_ref[...], dims, preferred_element_type=F32)
        hv = lax.dot_general(h, wv_ref[...], dims, preferred_element_type=F32)
        hg_ref[...] = hg
        hv_ref[...] = hv
        eg = jnp.concatenate([jnp.where(i == 0, 0.0, cg_scr[...]), hg], axis=0)
        ev = jnp.concatenate([jnp.where(i == 0, 0.0, cv_scr[...]), hv], axis=0)
        a_ref[...] = _ffn_tile(eg, ev, tg_ref[...], bg_ref[...], tv_ref[...], bv_ref[...]).astype(BF16)
        cg_scr[...] = hg[tm - 8:]
        cv_scr[...] = hv[tm - 8:]

    row = pl.BlockSpec((tm, d), lambda j, i: (i, 0))
    hrow = pl.BlockSpec((tm, d), lambda j, i: (j * (m // tm) + i, 0))
    tile = pl.BlockSpec((tm, tn), lambda j, i: (i, j))
    wspec = lambda w: _spec(w, (tn, d), lambda j, i: (j, 0))
    taps = lambda off: _spec(cwf, (3, tn), lambda j, i: (0, j + off))
    bias = lambda off: _spec(cbf, (1, tn), lambda j, i: (0, j + off))
    blk = (_nbytes((tm, d), F32) + _nbytes((tm, d), BF16) + 2 * _nbytes((tn, d), BF16) + 12 * _nbytes((tm, tn), F32))
    return _pcall(body, name=name,
                  out_shape=(_sds((nj * m, d), BF16), _sds((m, n), F32), _sds((m, n), F32), _sds((m, n), BF16)),
                  grid=(nj, m // tm),
                  in_specs=[row, _spec(g), wspec(wg), wspec(wv), taps(0), bias(0), taps(nj), bias(nj)],
                  out_specs=(hrow, tile, tile, tile),
                  scratch_shapes=[pltpu.VMEM((8, tn), F32), pltpu.VMEM((8, tn), F32)],
                  semantics=("arbitrary", "arbitrary"), block_bytes=blk)(
                      x, _arr(g), _arr(wg), _arr(wv), _arr(cwf), _arr(cbf), _arr(cwf), _arr(cbf))


def _ple_bwd(dx, gl, pe, *, name):
    s, d = dx.shape
    tm = _pick(s, (512, 256))

    def body(dx_ref, gl_ref, pe_ref, dpe_ref, dgl_ref):
        gate = jax.nn.sigmoid(gl_ref[...])
        dxv = dx_ref[...]
        dpe_ref[...] = (dxv * gate).astype(BF16)
        dgl_ref[...] = (dxv * pe_ref[...] * gate * (1.0 - gate)).astype(BF16)

    row = pl.BlockSpec((tm, d), lambda i: (i, 0))
    return _pcall(body, name=name, out_shape=(_sds((s, d), BF16), _sds((s, d), BF16)), grid=(s // tm,),
                  in_specs=[row, row, row], out_specs=(row, row), semantics=("parallel",),
                  block_bytes=5 * _nbytes((tm, d), F32))(dx, gl, pe)


def _loss_head(x, g, target, *, name):
    s, d = x.shape
    tm = _pick(s, (256, 128))

    def tile_loss(xv, gv, tv):
        err = jnp.square(_rms(xv, gv) - tv)
        return 0.5 * jnp.sum(jnp.mean(err, axis=-1, keepdims=True), axis=0, keepdims=True)

    def body(x_ref, g_ref, t_ref, l_ref, dx_ref, dg_ref):
        lv, vjp = jax.vjp(tile_loss, x_ref[...], g_ref[...], t_ref[...])
        dxv, dgv, _ = vjp(jnp.ones((1, 1), F32))
        dx_ref[...] = dxv

        @pl.when(pl.program_id(0) == 0)
        def _():
            l_ref[...] = jnp.zeros_like(l_ref)
            dg_ref[...] = jnp.zeros_like(dg_ref)

        l_ref[...] += jnp.broadcast_to(lv, l_ref.shape)
        dg_ref[...] += dgv

    row = pl.BlockSpec((tm, d), lambda i: (i, 0))
    vec = pl.BlockSpec((1, d), lambda i: (0, 0))
    return _pcall(body, name=name, out_shape=(_sds((8, 128), F32), _sds((s, d), F32), _sds((1, d), F32)),
                  grid=(s // tm,), in_specs=[row, vec, row],
                  out_specs=(pl.BlockSpec((8, 128), lambda i: (0, 0)), row, vec), semantics=("arbitrary",),
                  block_bytes=8 * _nbytes((tm, d), F32))(x, g, target)


def _acc_out(ref, val, first):
    @pl.when(first)
    def _():
        ref[...] = jnp.zeros_like(ref)

    ref[...] += val


def _gmlp_fwd(z, ln_g, ln_b, wcat, bfull, *, name):
    s = z.shape[0]
    t = _pick(s, (512, 256, 128))
    nch = t // GMLP_CHUNK

    def body(zu_ref, zv_ref, g_ref, b_ref, w_ref, bf_ref, o_ref):
        for c in range(nch):
            rows = pl.ds(c * GMLP_CHUNK, GMLP_CHUNK)
            o_ref[rows, :] = _gmlp_chunk(zu_ref[rows, :], zv_ref[rows, :], g_ref[...], b_ref[...], w_ref[...],
                                         bf_ref[...]).astype(BF16)

    col = lambda c: pl.BlockSpec((t, W_GRP), lambda i: (i, c))
    params = (ln_g, ln_b, wcat, bfull)
    return _pcall(body, name=name, out_shape=_sds((s, D_MODEL), BF16), grid=(s // t,),
                  in_specs=[col(0), col(1)] + [_spec(a) for a in params],
                  out_specs=pl.BlockSpec((t, W_GRP), lambda i: (i, 0)), semantics=("parallel",),
                  block_bytes=4 * _nbytes((t, W_GRP), F32))(z, z, *[_arr(a) for a in params])


def _gmlp_bwd(z, dmix, ln_g, ln_b, wcat, bfull, *, name):
    s = z.shape[0]
    t = _pick(s, (512, 256, 128))
    nch = t // GMLP_CHUNK

    def body(zu_ref, zv_ref, dy_ref, g_ref, b_ref, w_ref, bf_ref, dz_ref, dg_ref, db_ref, dw_ref, dbf_ref):
        acc = None
        for c in range(nch):
            rows = pl.ds(c * GMLP_CHUNK, GMLP_CHUNK)
            _, vjp = jax.vjp(_gmlp_chunk, zu_ref[rows, :], zv_ref[rows, :], g_ref[...], b_ref[...], w_ref[...],
                             bf_ref[...])
            du, dv, *dps = vjp(dy_ref[rows, :])
            dz_ref[rows, :] = jnp.concatenate([du, dv], axis=1).astype(BF16)
            acc = dps if acc is None else [x + y for x, y in zip(acc, dps)]
        first = pl.program_id(0) == 0
        for ref, val in zip((dg_ref, db_ref, dw_ref, dbf_ref), acc):
            _acc_out(ref, val, first)

    col = lambda c: pl.BlockSpec((t, W_GRP), lambda i: (i, c))
    params = (ln_g, ln_b, wcat, bfull)
    return _pcall(body, name=name,
                  out_shape=(_sds((s, D_PROJ), BF16),) + tuple(_sds(a.shape, F32) for a in params),
                  grid=(s // t,), in_specs=[col(0), col(1), col(0)] + [_spec(a) for a in params],
                  out_specs=(pl.BlockSpec((t, 2 * W_GRP), lambda i: (i, 0)),) + tuple(_ospec(a) for a in params),
                  semantics=("arbitrary",),
                  block_bytes=8 * _nbytes((t, W_GRP), F32))(z, z, dmix, *[_arr(a) for a in params])


def _rglru_fwd(z, prm, mix, *, name):
    s = z.shape[0]
    t = _pick(s, (512, 256, 128))
    nt = s // t

    def body(xb_ref, halo_ref, gb_ref, *rest):
        prm_refs, (y_ref, h0s_ref, h_scr) = rest[:len(prm)], rest[len(prm) + 1:]
        i = pl.program_id(0)

        @pl.when(i == 0)
        def _():
            h_scr[...] = jnp.zeros_like(h_scr)

        halo = jnp.where(i == 0, 0.0, halo_ref[...])
        h0 = h_scr[...]
        y, h_last = _rglru_tile(jnp.concatenate([halo, xb_ref[...]], axis=0), gb_ref[...], h0,
                                *[r[...] for r in prm_refs])
        y_ref[...] = y.astype(BF16)
        h0s_ref[...] = jnp.broadcast_to(h0, h0s_ref.shape)
        h_scr[...] = h_last

    in_specs = [pl.BlockSpec((t, W_GRP), lambda i: (i, 2)),
                pl.BlockSpec((8, W_GRP), lambda i: (jnp.maximum(i * (t // 8) - 1, 0), 2)),
                pl.BlockSpec((t, W_GRP), lambda i: (i, 3))] + [_spec(a) for a in prm] + [HBM_SPEC]
    return _pcall(body, name=name, out_shape=(_sds(mix.shape, BF16), _sds((nt, 8, W_GRP), F32)), grid=(nt,),
                  in_specs=in_specs,
                  out_specs=(pl.BlockSpec((t, W_GRP), lambda i: (i, 1)), pl.BlockSpec((None, 8, W_GRP), lambda i: (i, 0, 0))),
                  scratch_shapes=[pltpu.VMEM((1, W_GRP), F32)], semantics=("arbitrary",),
                  block_bytes=24 * _nbytes((t, W_GRP), F32), aliases={3 + len(prm): 0})(
                      z, z, z, *[_arr(a) for a in prm], mix)


def _rglru_bwd(z, dmix, h0s, prm, dz, *, name):
    s = z.shape[0]
    t = _pick(s, (512, 256, 128))
    nt = s // t
    npm = len(prm)

    def body(xb_ref, halo_ref, gb_ref, dy_ref, h0s_ref, *rest):
        prm_refs = rest[:npm]
        dz_ref = rest[npm + 1]
        dprm_refs = rest[npm + 2:2 * npm + 2]
        dh_scr, dhalo_scr = rest[2 * npm + 2:]
        i = pl.program_id(0)
        r = nt - 1 - i

        @pl.when(i == 0)
        def _():
            dh_scr[...] = jnp.zeros_like(dh_scr)
            dhalo_scr[...] = jnp.zeros_like(dhalo_scr)

        halo = jnp.where(r == 0, 0.0, halo_ref[...])
        h0 = h0s_ref[0:1, :]
        _, vjp = jax.vjp(_rglru_tile, jnp.concatenate([halo, xb_ref[...]], axis=0), gb_ref[...], h0,
                         *[p[...] for p in prm_refs])
        dext, dgb, _dh0, *dps = vjp((dy_ref[...], dh_scr[...]))
        dmain = dext[8:]
        dxb = jnp.concatenate([dmain[:t - 8], dmain[t - 8:] + dhalo_scr[...]], axis=0)
        dz_ref[...] = jnp.concatenate([dxb, dgb], axis=1).astype(BF16)
        dh_scr[...] = _dh0
        dhalo_scr[...] = dext[:8]
        for ref, val in zip(dprm_refs, dps):
            _acc_out(ref, val, i == 0)

    rev = lambda c: pl.BlockSpec((t, W_GRP), lambda i: (nt - 1 - i, c))
    in_specs = [rev(2), pl.BlockSpec((8, W_GRP), lambda i: (jnp.maximum((nt - 1 - i) * (t // 8) - 1, 0), 2)), rev(3),
                rev(1), pl.BlockSpec((None, 8, W_GRP), lambda i: (nt - 1 - i, 0, 0))] + [_spec(a) for a in prm] + [HBM_SPEC]
    return _pcall(body, name=name,
                  out_shape=(_sds(dz.shape, BF16),) + tuple(_sds(a.shape, F32) for a in prm),
                  grid=(nt,), in_specs=in_specs,
                  out_specs=(pl.BlockSpec((t, 2 * W_GRP), lambda i: (nt - 1 - i, 1)),) + tuple(_ospec(a) for a in prm),
                  scratch_shapes=[pltpu.VMEM((1, W_GRP), F32), pltpu.VMEM((8, W_GRP), F32)],
                  semantics=("arbitrary",), block_bytes=40 * _nbytes((t, W_GRP), F32), aliases={5 + npm: 0})(
                      z, z, z, dmix, h0s, *[_arr(a) for a in prm], dz)


def _pool_inv(i, t):
    pos = (_rows_of((t, W_GRP)) + i * t + 1).astype(F32)
    grp = _lanes_of((t, W_GRP)) // HEAD_DIM
    win = jnp.where(grp == 0, float(POOL_WINDOWS[0]), jnp.where(grp == 1, float(POOL_WINDOWS[1]),
                    jnp.where(grp == 2, float(POOL_WINDOWS[2]), float(POOL_WINDOWS[3]))))
    return 1.0 / jnp.minimum(pos, win)


def _pool_fwd(z, wd, scale, mix, *, name):
    s = z.shape[0]
    t = _pick(s, (512, 256, 128))

    def body(x_ref, halo_ref, wd_ref, sc_ref, _, y_ref):
        i = pl.program_id(0)
        halo = jnp.where(i == 0, 0.0, halo_ref[...])
        y = _pool_tile(jnp.concatenate([halo, x_ref[...]], axis=0), _pool_inv(i, t), wd_ref[...], sc_ref[...])
        y_ref[...] = y.astype(BF16)

    in_specs = [pl.BlockSpec((t, W_GRP), lambda i: (i, 8)),
                pl.BlockSpec((16, W_GRP), lambda i: (jnp.maximum(i * (t // 16) - 1, 0), 8)), _spec(wd), _spec(scale),
                HBM_SPEC]
    return _pcall(body, name=name, out_shape=_sds(mix.shape, BF16), grid=(s // t,), in_specs=in_specs,
                  out_specs=pl.BlockSpec((t, W_GRP), lambda i: (i, 3)), semantics=("parallel",),
                  block_bytes=12 * _nbytes((t, W_GRP), F32), aliases={4: 0})(z, z, _arr(wd), _arr(scale), mix)


def _pool_bwd(z, dmix, wd, scale, dz, *, name):
    s = z.shape[0]
    t = _pick(s, (512, 256, 128))
    nt = s // t

    def body(x_ref, halo_ref, dy_ref, wd_ref, sc_ref, _, dx_ref, dwd_ref, dsc_ref, dhalo_scr):
        i = pl.program_id(0)
        r = nt - 1 - i

        @pl.when(i == 0)
        def _():
            dhalo_scr[...] = jnp.zeros_like(dhalo_scr)

        halo = jnp.where(r == 0, 0.0, halo_ref[...])
        inv = _pool_inv(r, t)
        _, vjp = jax.vjp(lambda e, w, sc: _pool_tile(e, inv, w, sc), jnp.concatenate([halo, x_ref[...]], axis=0),
                         wd_ref[...], sc_ref[...])
        dext, dwd, dsc = vjp(dy_ref[...])
        dmain = dext[16:]
        dx = jnp.concatenate([dmain[:t - 16], dmain[t - 16:] + dhalo_scr[...]], axis=0)
        dx_ref[...] = dx.astype(BF16)
        dhalo_scr[...] = dext[:16]
        _acc_out(dwd_ref, dwd, i == 0)
        _acc_out(dsc_ref, dsc, i == 0)

    rev = lambda c: pl.BlockSpec((t, W_GRP), lambda i: (nt - 1 - i, c))
    in_specs = [rev(8), pl.BlockSpec((16, W_GRP), lambda i: (jnp.maximum((nt - 1 - i) * (t // 16) - 1, 0), 8)), rev(3),
                _spec(wd), _spec(scale), HBM_SPEC]
    return _pcall(body, name=name, out_shape=(_sds(dz.shape, BF16), _sds(wd.shape, F32), _sds(scale.shape, F32)),
                  grid=(nt,), in_specs=in_specs, out_specs=(rev(8), _ospec(wd), _ospec(scale)),
                  scratch_shapes=[pltpu.VMEM((16, W_GRP), F32)], semantics=("arbitrary",),
                  block_bytes=20 * _nbytes((t, W_GRP), F32), aliases={5: 0})(z, z, dmix, _arr(wd), _arr(scale), dz)


def _hgrn_fwd(z, lb, ngf, mix, *, name):
    s = z.shape[0]
    c = HGRN_CHUNK
    per = HGRN_STEP_CHUNKS
    ns = s // (c * per)

    def body(q_ref, f_ref, i_ref, g_ref, lb_ref, ng_ref, _, y_ref, sts_ref, st_scr):
        @pl.when(pl.program_id(0) == 0)
        def _():
            st_scr[...] = jnp.zeros_like(st_scr)

        st = st_scr[...]
        for k in range(per):
            rows = pl.ds(k * c, c)
            sts_ref[k] = st
            y, st = _hgrn_chunk(q_ref[rows, :], f_ref[rows, :], i_ref[rows, :], g_ref[rows, :], st, lb_ref[...],
                                ng_ref[...])
            y_ref[rows, :] = y.astype(BF16)
        st_scr[...] = st

    col = lambda k: pl.BlockSpec((per * c, W_GRP), lambda i: (i, k))
    return _pcall(body, name=name, out_shape=(_sds(mix.shape, BF16), _sds((ns * per, W_GRP, W_GRP), F32)), grid=(ns,),
                  in_specs=[col(4), col(5), col(6), col(7), _spec(lb), _spec(ngf), HBM_SPEC],
                  out_specs=(pl.BlockSpec((per * c, W_GRP), lambda i: (i, 2)),
                             pl.BlockSpec((per, W_GRP, W_GRP), lambda i: (i, 0, 0))),
                  scratch_shapes=[pltpu.VMEM((W_GRP, W_GRP), F32)], semantics=("arbitrary",),
                  block_bytes=16 * per * _nbytes((W_GRP, W_GRP), F32), aliases={6: 0})(
                      z, z, z, z, _arr(lb), _arr(ngf), mix)


def _hgrn_bwd(z, dmix, sts, lb, ngf, dz, *, name):
    s = z.shape[0]
    c = HGRN_CHUNK
    per = HGRN_STEP_CHUNKS
    ns = s // (c * per)

    def body(q_ref, f_ref, i_ref, g_ref, dy_ref, st_ref, lb_ref, ng_ref, _, dz_ref, dlb_ref, dng_ref, dst_scr):
        i = pl.program_id(0)

        @pl.when(i == 0)
        def _():
            dst_scr[...] = jnp.zeros_like(dst_scr)

        dst = dst_scr[...]
        dlb_sum = dng_sum = None
        for k in range(per - 1, -1, -1):
            rows = pl.ds(k * c, c)
            _, vjp = jax.vjp(_hgrn_chunk, q_ref[rows, :], f_ref[rows, :], i_ref[rows, :], g_ref[rows, :], st_ref[k],
                             lb_ref[...], ng_ref[...])
            dq, df, di, dg, dst, dlb, dng = vjp((dy_ref[rows, :], dst))
            dz_ref[rows, :] = jnp.concatenate([dq, df, di, dg], axis=1).astype(BF16)
            dlb_sum = dlb if dlb_sum is None else dlb_sum + dlb
            dng_sum = dng if dng_sum is None else dng_sum + dng
        dst_scr[...] = dst
        _acc_out(dlb_ref, dlb_sum, i == 0)
        _acc_out(dng_ref, dng_sum, i == 0)

    rev = lambda k: pl.BlockSpec((per * c, W_GRP), lambda i: (ns - 1 - i, k))
    vec = pl.BlockSpec((1, W_GRP), lambda i: (0, 0))
    return _pcall(body, name=name, out_shape=(_sds(dz.shape, BF16), _sds((1, W_GRP), F32), _sds((1, W_GRP), F32)),
                  grid=(ns,),
                  in_specs=[rev(4), rev(5), rev(6), rev(7), rev(2),
                            pl.BlockSpec((per, W_GRP, W_GRP), lambda i: (ns - 1 - i, 0, 0)), _spec(lb), _spec(ngf),
                            HBM_SPEC],
                  out_specs=(pl.BlockSpec((per * c, 4 * W_GRP), lambda i: (ns - 1 - i, 1)), vec, vec),
                  scratch_shapes=[pltpu.VMEM((W_GRP, W_GRP), F32)], semantics=("arbitrary",),
                  block_bytes=32 * per * _nbytes((W_GRP, W_GRP), F32), aliases={8: 0})(
                      z, z, z, z, dmix, sts, _arr(lb), _arr(ngf), dz)


def _lbs_fwd(c_lb, *, name):
    def body(c_ref, o_ref):
        c = c_ref[...]
        e = jnp.exp(c - jnp.max(c, axis=0, keepdims=True))
        sm = e / jnp.sum(e, axis=0, keepdims=True)
        run = jnp.zeros((1, W_GRP), F32)
        o_ref[0:1, :] = run
        for l in range(1, DEPTH):
            run = run + sm[l:l + 1]
            o_ref[l:l + 1, :] = run

    return _pcall(body, name=name, out_shape=_sds((DEPTH, W_GRP), F32), pin=False)(c_lb)


def _lbs_bwd(c_lb, dlbs, *, name):
    def body(c_ref, d_ref, o_ref):
        c = c_ref[...]
        e = jnp.exp(c - jnp.max(c, axis=0, keepdims=True))
        sm = e / jnp.sum(e, axis=0, keepdims=True)
        d = d_ref[...]
        dsm = [None] * DEPTH
        run = jnp.zeros((1, W_GRP), F32)
        for l in range(DEPTH - 1, 0, -1):
            run = run + d[l:l + 1]
            dsm[l] = run
        dsm[0] = jnp.zeros((1, W_GRP), F32)
        inner = sum(sm[l:l + 1] * dsm[l] for l in range(DEPTH))
        for l in range(DEPTH):
            o_ref[l:l + 1, :] = sm[l:l + 1] * (dsm[l] - inner)

    return _pcall(body, name=name, out_shape=_sds((DEPTH, W_GRP), F32), pin=False)(c_lb, dlbs)


def _ffn_bwd(hg, hv, dx, w_down, cwf, cbf, *, name):
    s, n = hg.shape
    t = _pick(s, (256, 128))
    cw = _pick(n, (1408, 256, 128))
    nt = s // t
    nj = n // cw

    def body(g_ref, gh_ref, v_ref, vh_ref, dx_ref, wd_ref, wg_ref, bg_ref, wv_ref, bv_ref, dg_ref, dv_ref, dwg_ref,
             dwv_ref, cg_scr, cv_scr):
        i = pl.program_id(1)
        r = nt - 1 - i

        @pl.when(i == 0)
        def _():
            cg_scr[...] = jnp.zeros_like(cg_scr)
            cv_scr[...] = jnp.zeros_like(cv_scr)

        da = lax.dot_general(dx_ref[...], wd_ref[...], (((1,), (1,)), ((), ())), preferred_element_type=F32)
        eg = jnp.concatenate([jnp.where(r == 0, 0.0, gh_ref[...]), g_ref[...]], axis=0)
        ev = jnp.concatenate([jnp.where(r == 0, 0.0, vh_ref[...]), v_ref[...]], axis=0)
        _, vjp = jax.vjp(_ffn_tile, eg, ev, wg_ref[...], bg_ref[...], wv_ref[...], bv_ref[...])
        deg, dev, dwg, dbg, dwv, dbv = vjp(da)
        for dext, scr, ref in ((deg, cg_scr, dg_ref), (dev, cv_scr, dv_ref)):
            dmain = dext[8:]
            ref[...] = jnp.concatenate([dmain[:t - 8], dmain[t - 8:] + scr[...]], axis=0).astype(BF16)
            scr[...] = dext[:8]
        zeros = jnp.zeros((4, cw), F32)
        _acc_out(dwg_ref, jnp.concatenate([dwg, dbg, zeros], axis=0), i == 0)
        _acc_out(dwv_ref, jnp.concatenate([dwv, dbv, zeros], axis=0), i == 0)

    main = pl.BlockSpec((t, cw), lambda j, i: (nt - 1 - i, j))
    halo = pl.BlockSpec((8, cw), lambda j, i: (jnp.maximum((nt - 1 - i) * (t // 8) - 1, 0), j))
    taps = lambda off: _spec(cwf, (3, cw), lambda j, i: (0, j + off))
    bias = lambda off: _spec(cbf, (1, cw), lambda j, i: (0, j + off))
    w8 = pl.BlockSpec((8, cw), lambda j, i: (0, j))
    d = dx.shape[1]
    in_specs = [main, halo, main, halo, pl.BlockSpec((t, d), lambda j, i: (nt - 1 - i, 0)),
                _spec(w_down, (cw, d), lambda j, i: (j, 0)), taps(0), bias(0), taps(nj), bias(nj)]
    return _pcall(body, name=name,
                  out_shape=(_sds((s, n), BF16), _sds((s, n), BF16), _sds((8, n), F32), _sds((8, n), F32)),
                  grid=(nj, nt), in_specs=in_specs, out_specs=(main, main, w8, w8),
                  scratch_shapes=[pltpu.VMEM((8, cw), F32), pltpu.VMEM((8, cw), F32)],
                  semantics=("parallel", "arbitrary"),
                  block_bytes=24 * _nbytes((t, cw), F32) + _nbytes((cw, d), BF16))(
                      hg, hg, hv, hv, dx, _arr(w_down), _arr(cwf), _arr(cbf), _arr(cwf), _arr(cbf))


def _all_gather(x, *, name):
    r, c = x.shape

    def body(x_ref, out_ref, send_sems, recv_sems, local_sem):
        mx, my, mc = lax.axis_index("x"), lax.axis_index("y"), lax.axis_index("c")
        me, sibling = (mx, my, mc), (mx, my, 1 - mc)
        chips = [(1 - mx, my), (mx, 1 - my), (1 - mx, 1 - my)]

        def slot(px, py, pc):
            return out_ref.at[4 * px + 2 * py + pc]

        def copy(k, block, to, src=None):
            return pltpu.make_async_remote_copy(src_ref=slot(*block) if src is None else src, dst_ref=slot(*block),
                                                send_sem=send_sems.at[k], recv_sem=recv_sems.at[k],
                                                device_id=to, device_id_type=MESH)

        mine = pltpu.make_async_copy(x_ref, slot(*me), local_sem)
        mine.start()
        first = [copy(0, me, sibling, src=x_ref)]
        first += [copy(1 + j, me, (*chip, mc), src=x_ref) for j, chip in enumerate(chips)]
        for cp in first:
            cp.start()
        passed = [copy(4 + j, (*chip, mc), sibling) for j, chip in enumerate(chips)]
        for j, chip in enumerate(chips):
            copy(1 + j, (*chip, mc), me).wait_recv()
            passed[j].start()
        copy(0, sibling, me).wait_recv()
        for j, chip in enumerate(chips):
            copy(4 + j, (*chip, 1 - mc), me).wait_recv()
        for cp in first + passed:
            cp.wait_send()
        mine.wait()

    hbm = pl.BlockSpec(memory_space=pl.ANY)
    return _pcall(body, name=name, out_shape=_sds((N_DEV, r, c), x.dtype), in_specs=[hbm], out_specs=hbm,
                  scratch_shapes=[pltpu.SemaphoreType.DMA((7,)), pltpu.SemaphoreType.DMA((7,)),
                                  pltpu.SemaphoreType.DMA(())])(x)


def _sum_slots(p, *, name):
    q, r, c = p.shape
    tr = _pick(r, (544, 408, 272, 192, 136, 64, 32, 16, 8))

    def body(p_ref, o_ref):
        acc = p_ref[0].astype(F32)
        for k in range(1, q):
            acc = acc + p_ref[k].astype(F32)
        o_ref[...] = acc

    return _pcall(body, name=name, out_shape=_sds((r, c), F32), grid=(r // tr,),
                  in_specs=[pl.BlockSpec((q, tr, c), lambda i: (0, i, 0))],
                  out_specs=pl.BlockSpec((tr, c), lambda i: (i, 0)), semantics=("parallel",),
                  block_bytes=(q + 2) * _nbytes((tr, c), F32))(p)


BIG_COMM = (('w_in', 288, D_MODEL), ('w_out', 128, D_MODEL), ('w_up', 704, D_MODEL), ('w_down', 352, D_MODEL),
            ('w_pe', 128, PLE_DIM), ('w_pg', 128, D_MODEL))
HBM_SPEC = pl.BlockSpec(memory_space=pl.ANY)


def _gather_layer(shards, l, *, name):
    na = len(shards)

    def body(*refs):
        x_refs, out_refs = refs[:na], refs[na:2 * na]
        send_sems, recv_sems, local_sems = refs[2 * na:]
        mx, my, mc = lax.axis_index("x"), lax.axis_index("y"), lax.axis_index("c")
        me, sibling = (mx, my, mc), (mx, my, 1 - mc)
        chips = [(1 - mx, my), (mx, 1 - my), (1 - mx, 1 - my)]

        def slot(a, px, py, pc):
            return out_refs[a].at[4 * px + 2 * py + pc]

        def copy(k, a, block, to, own=False):
            return pltpu.make_async_remote_copy(src_ref=x_refs[a].at[l] if own else slot(a, *block),
                                                dst_ref=slot(a, *block), send_sem=send_sems.at[k, a],
                                                recv_sem=recv_sems.at[k, a], device_id=to, device_id_type=MESH)

        mine = [pltpu.make_async_copy(x_refs[a].at[l], slot(a, *me), local_sems.at[a]) for a in range(na)]
        for cp in mine:
            cp.start()
        first = []
        for a in range(na):
            first.append(copy(0, a, me, sibling, own=True))
            first += [copy(1 + j, a, me, (*chip, mc), own=True) for j, chip in enumerate(chips)]
        for cp in first:
            cp.start()
        passed = []
        for j, chip in enumerate(chips):
            for a in range(na):
                copy(1 + j, a, (*chip, mc), me).wait_recv()
                fwd = copy(4 + j, a, (*chip, mc), sibling)
                fwd.start()
                passed.append(fwd)
        for a in range(na):
            copy(0, a, sibling, me).wait_recv()
        for j, chip in enumerate(chips):
            for a in range(na):
                copy(4 + j, a, (*chip, 1 - mc), me).wait_recv()
        for cp in first + passed:
            cp.wait_send()
        for cp in mine:
            cp.wait()

    return _pcall(body, name=name, out_shape=tuple(_sds((N_DEV,) + x.shape[1:], x.dtype) for x in shards),
                  in_specs=[HBM_SPEC] * na, out_specs=(HBM_SPEC,) * na,
                  scratch_shapes=[pltpu.SemaphoreType.DMA((7, na)), pltpu.SemaphoreType.DMA((7, na)),
                                  pltpu.SemaphoreType.DMA((na,))])(*shards)


SEM_SPEC = pl.BlockSpec(memory_space=pltpu.SEMAPHORE)
DATAFLOW_EFFECT = pltpu.SideEffectType.DATAFLOW_SIDE_EFFECTING


def _place_own(srcs, after, *, name):
    na = len(srcs)

    def body(*refs):
        x_refs, land_refs, sems = refs[:na], refs[na + len(after):2 * na + len(after)], refs[-1]
        me = 4 * lax.axis_index("x") + 2 * lax.axis_index("y") + lax.axis_index("c")
        cps = [pltpu.make_async_copy(x_refs[a], land_refs[a].at[me], sems.at[a]) for a in range(na)]
        for cp in cps:
            cp.start()
        for cp in cps:
            cp.wait()

    return _pcall(body, name=name, out_shape=tuple(_sds((N_DEV,) + x.shape, x.dtype) for x in srcs),
                  in_specs=[HBM_SPEC] * (na + len(after)), out_specs=(HBM_SPEC,) * na,
                  scratch_shapes=[pltpu.SemaphoreType.DMA((na,))], pin=False)(*srcs, *after)


def _exchange_start(srcs, lands, *, name, per_peer=False):
    na = len(srcs)

    def body(*refs):
        x_refs, land_refs = refs[:na], refs[na:2 * na]
        send_sems, recv_sems = refs[2 * na], refs[2 * na + 1]
        token = refs[-1]
        mx, my, mc = lax.axis_index("x"), lax.axis_index("y"), lax.axis_index("c")
        me = 4 * mx + 2 * my + mc
        peers = [(mx, my, 1 - mc)]
        for px, py in ((1 - mx, my), (mx, 1 - my), (1 - mx, 1 - my)):
            peers += [(px, py, mc), (px, py, 1 - mc)]
        for a in range(na):
            for peer in peers:
                src = x_refs[a].at[4 * peer[0] + 2 * peer[1] + peer[2]] if per_peer else x_refs[a]
                pltpu.make_async_remote_copy(src_ref=src, dst_ref=land_refs[a].at[me], send_sem=send_sems.at[a],
                                             recv_sem=recv_sems.at[a], device_id=peer, device_id_type=MESH).start()
        token[...] = jnp.zeros_like(token)

    hbm = lambda x: pltpu.HBM(x.shape, x.dtype)
    out_shape = ((pltpu.SemaphoreType.DMA((na,)), pltpu.SemaphoreType.DMA((na,))) + tuple(hbm(x) for x in srcs)
                 + tuple(hbm(x) for x in lands) + (_sds((8, 128), F32),))
    params = pltpu.CompilerParams(has_side_effects=DATAFLOW_EFFECT)
    pin = lambda x: pltpu.with_memory_space_constraint(x, pltpu.HBM)
    return pl.pallas_call(body, name=name, out_shape=out_shape, in_specs=[HBM_SPEC] * (2 * na),
                          out_specs=(SEM_SPEC, SEM_SPEC) + (HBM_SPEC,) * (2 * na) + (pl.BlockSpec(memory_space=pltpu.VMEM),),
                          input_output_aliases={i: 2 + i for i in range(2 * na)}, compiler_params=params)(
                              *[pin(x) for x in srcs], *[pin(x) for x in lands])


def _exchange_wait(started, after, *, name):
    send_sems, recv_sems, *bufs, _ = started
    na = len(bufs) // 2

    def body(*refs):
        land_refs = refs[na:2 * na]
        s_sems, r_sems = refs[2 * na], refs[2 * na + 1]
        me = (lax.axis_index("x"), lax.axis_index("y"), lax.axis_index("c"))
        for a in range(na):
            seven = land_refs[a].at[pl.ds(0, N_DEV - 1)]
            cp = pltpu.make_async_remote_copy(src_ref=seven, dst_ref=seven, send_sem=s_sems.at[a], recv_sem=r_sems.at[a],
                                              device_id=me, device_id_type=MESH)
            cp.wait_send()
            cp.wait_recv()

    hbm = lambda x: pltpu.HBM(x.shape, x.dtype)
    params = pltpu.CompilerParams(has_side_effects=DATAFLOW_EFFECT)
    outs = pl.pallas_call(body, name=name, out_shape=tuple(hbm(x) for x in bufs),
                          in_specs=[HBM_SPEC] * (2 * na) + [SEM_SPEC, SEM_SPEC, HBM_SPEC],
                          out_specs=(HBM_SPEC,) * (2 * na), input_output_aliases={i: i for i in range(2 * na)},
                          compiler_params=params)(*bufs, send_sems, recv_sems, after)
    return outs[:na], outs[na:]


def _pair_swap(grads, *, name):
    na = len(grads)

    def body(*refs):
        g_refs, recv_refs = refs[:na], refs[na:2 * na]
        send_sems, recv_sems = refs[2 * na:]
        mx, my, mc = lax.axis_index("x"), lax.axis_index("y"), lax.axis_index("c")
        sibling = (mx, my, 1 - mc)
        for a in range(na):
            for q in range(4):
                pltpu.make_async_remote_copy(src_ref=g_refs[a].at[q, 1 - mc], dst_ref=recv_refs[a].at[q],
                                             send_sem=send_sems.at[a], recv_sem=recv_sems.at[a],
                                             device_id=sibling, device_id_type=MESH).start()
        for a in range(na):
            pltpu.make_async_remote_copy(src_ref=recv_refs[a], dst_ref=recv_refs[a], send_sem=send_sems.at[a],
                                         recv_sem=recv_sems.at[a], device_id=sibling, device_id_type=MESH).wait()

    half = tuple(_sds((4,) + g.shape[2:], g.dtype) for g in grads)
    return _pcall(body, name=name, out_shape=half, in_specs=[HBM_SPEC] * na, out_specs=(HBM_SPEC,) * na,
                  scratch_shapes=[pltpu.SemaphoreType.DMA((na,)), pltpu.SemaphoreType.DMA((na,))])(*grads)


def _add_slabs(grads, recv, core, *, name):
    na = len(grads)

    def body(core_ref, *refs):
        for a in range(na):
            refs[2 * na + a][...] = (refs[a][...].astype(F32) + refs[na + a][...].astype(F32)).astype(BF16)

    own_specs = [pl.BlockSpec((None, None) + x.shape[2:], lambda q, core_ref: (q, core_ref[0], 0, 0)) for x in grads]
    specs = [pl.BlockSpec((None,) + x.shape[1:], lambda q, core_ref: (q, 0, 0)) for x in recv]
    blk = sum(_nbytes(x.shape[1:], F32) for x in recv)
    grid_spec = pltpu.PrefetchScalarGridSpec(num_scalar_prefetch=1, grid=(4,), in_specs=own_specs + specs,
                                             out_specs=tuple(specs))
    params = pltpu.CompilerParams(dimension_semantics=("parallel",), vmem_limit_bytes=_vmem_limit(2 * blk))
    return pl.pallas_call(body, name=name, out_shape=tuple(_sds(x.shape, BF16) for x in recv), grid_spec=grid_spec,
                          compiler_params=params)(core, *grads, *recv)


def _chip_exchange(parts, *, name):
    na = len(parts)

    def body(*refs):
        p_refs, out_refs = refs[:na], refs[na:2 * na]
        send_sems, recv_sems, local_sems = refs[2 * na:]
        mx, my, mc = lax.axis_index("x"), lax.axis_index("y"), lax.axis_index("c")
        mine_q = 2 * mx + my
        chips = [(1 - mx, my), (mx, 1 - my), (1 - mx, 1 - my)]
        owns = [pltpu.make_async_copy(p_refs[a].at[mine_q], out_refs[a].at[mine_q], local_sems.at[a]) for a in range(na)]
        for cp in owns:
            cp.start()
        sends = []
        for a in range(na):
            for k, chip in enumerate(chips):
                sends.append(pltpu.make_async_remote_copy(
                    src_ref=p_refs[a].at[2 * chip[0] + chip[1]], dst_ref=out_refs[a].at[mine_q],
                    send_sem=send_sems.at[k, a], recv_sem=recv_sems.at[k, a], device_id=(*chip, mc), device_id_type=MESH))
        for cp in sends:
            cp.start()
        for a in range(na):
            for k, chip in enumerate(chips):
                pltpu.make_async_remote_copy(
                    src_ref=p_refs[a].at[mine_q], dst_ref=out_refs[a].at[2 * chip[0] + chip[1]],
                    send_sem=send_sems.at[k, a], recv_sem=recv_sems.at[k, a], device_id=(*chip, mc),
                    device_id_type=MESH).wait_recv()
        for cp in sends:
            cp.wait_send()
        for cp in owns:
            cp.wait()

    return _pcall(body, name=name, out_shape=tuple(_sds(x.shape, x.dtype) for x in parts), in_specs=[HBM_SPEC] * na,
                  out_specs=(HBM_SPEC,) * na,
                  scratch_shapes=[pltpu.SemaphoreType.DMA((3, na)), pltpu.SemaphoreType.DMA((3, na)),
                                  pltpu.SemaphoreType.DMA((na,))])(*parts)


def _sum_chips(parts, *, name):
    na = len(parts)

    def body(*refs):
        for a in range(na):
            p_ref = refs[a]
            acc = p_ref[0].astype(F32)
            for k in range(1, p_ref.shape[0]):
                acc = acc + p_ref[k].astype(F32)
            refs[na + a][...] = acc

    half = lambda x: x.shape[1] // 2
    in_specs = [pl.BlockSpec((x.shape[0], half(x), x.shape[2]), lambda i: (0, i, 0)) for x in parts]
    out_specs = tuple(pl.BlockSpec((half(x), x.shape[2]), lambda i: (i, 0)) for x in parts)
    blk = sum(_nbytes((x.shape[0] + 2, half(x), x.shape[2]), BF16) for x in parts)
    return _pcall(body, name=name, out_shape=tuple(_sds(x.shape[1:], F32) for x in parts), grid=(2,),
                  in_specs=in_specs, out_specs=out_specs, semantics=("parallel",), block_bytes=blk)(*parts)


def _sum_devices(lands, own, me, *, name):
    na = len(lands)

    def body(me_ref, *refs):
        mine = me_ref[0]
        for a in range(na):
            l_ref, o_ref = refs[a], refs[na + a]
            acc = None
            for k in range(N_DEV):
                term = jnp.where(mine == k, o_ref[...], l_ref[k]).astype(F32)
                acc = term if acc is None else acc + term
            refs[2 * na + a][...] = acc

    half = lambda x: x.shape[1] // 2
    land_specs = [pl.BlockSpec((N_DEV, half(x), x.shape[2]), lambda i, me_ref: (0, i, 0)) for x in lands]
    own_specs = [pl.BlockSpec((None, half(x), x.shape[2]), lambda i, me_ref: (me_ref[0], i, 0)) for x in lands]
    out_specs = tuple(pl.BlockSpec((half(x), x.shape[2]), lambda i, me_ref: (i, 0)) for x in lands)
    blk = sum(_nbytes((N_DEV + 3, half(x), x.shape[2]), BF16) for x in lands)
    grid_spec = pltpu.PrefetchScalarGridSpec(num_scalar_prefetch=1, grid=(2,), in_specs=land_specs + own_specs,
                                             out_specs=out_specs)
    params = pltpu.CompilerParams(dimension_semantics=("parallel",), vmem_limit_bytes=_vmem_limit(blk))
    return pl.pallas_call(body, name=name, out_shape=tuple(_sds(x.shape[1:], F32) for x in lands), grid_spec=grid_spec,
                          compiler_params=params)(me, *lands, *own)


def _reduce_layer(grads, l):
    n = lambda s: f"l{l}_{s}"
    views = [g.reshape(4, 2, g.shape[0] // N_DEV, g.shape[1]) for g in grads]
    recv = _pair_swap(views, name=n("reduce_pair"))
    core = lax.axis_index("c").astype(jnp.int32).reshape(1)
    chip_sum = _add_slabs(views, recv, core, name=n("reduce_pair_add"))
    from_chips = _chip_exchange(chip_sum, name=n("reduce_chips"))
    return _sum_chips(from_chips, name=n("reduce_chips_add"))


def _adamw(w, g, m, v, *, name):
    lead, (r, c) = w.shape[:-2], w.shape[-2:]
    tr = _pick(r, (512, 352, 288, 256, 192, 128, 64, 32, 16, 8))
    c1 = 1.0 / (1.0 - ADAM_B1 ** ADAM_STEP)
    c2 = 1.0 / (1.0 - ADAM_B2 ** ADAM_STEP)

    def body(w_ref, g_ref, m_ref, v_ref, d_ref, nm_ref, nv_ref):
        gv = g_ref[...]
        nm = ADAM_B1 * m_ref[...] + (1.0 - ADAM_B1) * gv
        nv = ADAM_B2 * v_ref[...] + (1.0 - ADAM_B2) * jnp.square(gv)
        d_ref[...] = -ADAM_LR * ((nm * c1) / (jnp.sqrt(nv * c2) + ADAM_EPS) + ADAM_WD * w_ref[...])
        nm_ref[...] = nm
        nv_ref[...] = nv

    if lead:
        blk = pl.BlockSpec((None, tr, c), lambda k, i: (k, i, 0))
        grid, sem = (lead[0], r // tr), ("parallel", "parallel")
    else:
        blk = pl.BlockSpec((tr, c), lambda i: (i, 0))
        grid, sem = (r // tr,), ("parallel",)
    out = _sds(w.shape, F32)
    return _pcall(body, name=name, out_shape=(out, out, out), grid=grid, in_specs=[blk] * 4,
                  out_specs=(blk, blk, blk), semantics=sem, block_bytes=7 * _nbytes((tr, c), F32))(w, g, m, v)


def _pack_flat(arrs, rows, cols=1024):
    flat = jnp.concatenate([a.reshape(-1).astype(F32) for a in arrs])
    pad = rows * cols - flat.shape[0]
    return jnp.pad(flat, (0, pad)).reshape(rows, cols)


def _unpack_flat(buf, shapes):
    flat = buf.reshape(-1)
    out, off = [], 0
    for shp in shapes:
        n = 1
        for s in shp:
            n *= s
        out.append(flat[off:off + n].reshape(shp))
        off += n
    return out


def _flat_rows(shapes, cols=1024):
    n = sum(functools.reduce(lambda a, b: a * b, shp, 1) for shp in shapes)
    rows = -(-n // cols)
    return -(-rows // 64) * 64


def _block_diag(w):
    eye = jnp.eye(N_HEADS, dtype=w.dtype)
    return (w[:, :, :, None, :] * eye[None, :, None, :, None]).reshape(w.shape[0], W_GRP, W_GRP)


def _diag_blocks(w):
    w5 = w.reshape(w.shape[0], N_HEADS, HEAD_DIM, N_HEADS, HEAD_DIM)
    return jnp.stack([w5[:, h, :, h, :] for h in range(N_HEADS)], axis=1)


def _stacked_params(w, lbs):
    tril = jnp.tril(jnp.ones((GMLP_CHUNK, GMLP_CHUNK), bool))
    row = lambda a: a.reshape(DEPTH, 1, -1)
    return dict(
        g1=row(w['norm1_g']), g2=row(w['norm2_g']), g3=row(w['norm3_g']),
        a_ln_g=row(w['a_ln_g']), a_ln_b=row(w['a_ln_b']),
        a_wcat=jnp.where(tril, w['a_ws'], 0.0).reshape(DEPTH, N_HEADS * GMLP_CHUNK, GMLP_CHUNK),
        a_bfull=jnp.repeat(jnp.swapaxes(w['a_bs'], 1, 2), HEAD_DIM, axis=2),
        b_cw=w['b_conv_w_full'], b_cb=row(w['b_conv_b']), b_wa=_block_diag(w['b_wa']), b_ba=row(w['b_ba']),
        b_wx=_block_diag(w['b_wx']), b_bx=row(w['b_bx']), b_lam=row(w['b_lam']),
        c_lb=row(lbs), c_ngf=row(jnp.tile(w['c_norm_g'], (1, N_HEADS))),
        d_wd=_block_diag(w['d_w']), d_scale=row(w['d_scale']),
        f_cw=w['ffn_conv_w_full'], f_cb=row(w['ffn_conv_b']),
    )


B_PRM = ('b_cw', 'b_cb', 'b_wa', 'b_ba', 'b_wx', 'b_bx', 'b_lam')


def _layer_fwd(x, p_bf, wb, sp, l):
    n = lambda s: f"l{l}_{s}"
    h, (z,) = _rms_matmul(x, sp['g1'], [wb['w_in']], nt=True, name=n("proj_in"))
    mix = _gmlp_fwd(z, sp['a_ln_g'], sp['a_ln_b'], sp['a_wcat'], sp['a_bfull'], name=n("gmlp"))
    mix, h0s = _rglru_fwd(z, [sp[k] for k in B_PRM], mix, name=n("rglru"))
    mix, sts = _hgrn_fwd(z, sp['c_lb'], sp['c_ngf'], mix, name=n("hgrn"))
    mix = _pool_fwd(z, sp['d_wd'], sp['d_scale'], mix, name=n("pool"))
    x1 = _matmul(mix, wb['w_out'], res=x, name=n("proj_out"))
    h2, hg, hv, a = _up_ffn_fwd(x1, sp['g2'], wb['w_up_g'], wb['w_up_v'], sp['f_cw'], sp['f_cb'], name=n("up_ffn"))
    x2 = _matmul(a, wb['w_down'], res=x1, name=n("down"))
    h3, (gl, pe, x3) = _rms_matmul(x2, sp['g3'], [wb['w_pg']], ple=(p_bf, wb['w_pe']), name=n("ple"))
    saved = dict(x=x, h=h, z=z, h0s=h0s, sts=sts, mix=mix, x1=x1, h2=h2, hg=hg, hv=hv, a=a, x2=x2, h3=h3, gl=gl, pe=pe)
    return x3, saved


def _layer_bwd(dx3, sv, p_bf, wb, sp, l, mid=None):
    n = lambda s: f"l{l}_{s}_bwd"
    gb, gs = {}, {}
    dpe, dgl = _ple_bwd(dx3, sv['gl'], sv['pe'], name=n("ple"))
    gb['w_pe'] = _matmul_tn(dpe, p_bf, name=n("ple_emb_w"))
    gb['w_pg'] = _matmul_tn(sv['h3'], dgl, name=n("ple_gate_w"))
    dx2, dx2b, gs['norm3_g'] = _matmul_rms_bwd(dgl, wb['w_pg'], sv['x2'], sp['g3'], dx3, nt=True, name=n("ple_gate_x"))
    gb['w_down'] = _matmul_tn(sv['a'], dx2b, name=n("down_w"))
    dhg, dhv, gs['f_dwg'], gs['f_dwv'] = _ffn_bwd(sv['hg'], sv['hv'], dx2b, wb['w_down'], sp['f_cw'], sp['f_cb'],
                                                  name=n("ffn_gate"))
    gate_rows = _matmul_tn(dhg, sv['h2'], name=n("up_gate_w"), out_rows=2 * D_FF)
    gb['w_up'] = _matmul_tn(dhv, sv['h2'], name=n("up_val_w"), out_rows=2 * D_FF, row_off=D_FF, into=gate_rows)
    if mid is not None:
        sp = mid(gb, sp)
    dh2 = _matmul(dhg, wb['w_up_g'], name=n("up_gate_x"))
    dx1, dx1b, gs['norm2_g'] = _matmul_rms_bwd(dhv, wb['w_up_v'], sv['x1'], sp['g2'], dx2, res=dh2, name=n("up_val_x"))
    dmix = _matmul(dx1b, wb['w_out'], nt=True, name=n("proj_out_x"))
    gb['w_out'] = _matmul_tn(sv['mix'], dx1b, name=n("proj_out_w"))
    z = sv['z']
    dz, gs['a_ln_g'], gs['a_ln_b'], gs['a_wcat'], gs['a_bfull'] = _gmlp_bwd(
        z, dmix, sp['a_ln_g'], sp['a_ln_b'], sp['a_wcat'], sp['a_bfull'], name=n("gmlp"))
    dz, *dbp = _rglru_bwd(z, dmix, sv['h0s'], [sp[k] for k in B_PRM], dz, name=n("rglru"))
    gs.update(zip(B_PRM, dbp))
    dz, gs['c_lb'], gs['c_ngf'] = _hgrn_bwd(z, dmix, sv['sts'], sp['c_lb'], sp['c_ngf'], dz, name=n("hgrn"))
    dz, gs['d_wd'], gs['d_scale'] = _pool_bwd(z, dmix, sp['d_wd'], sp['d_scale'], dz, name=n("pool"))
    gb['w_in'] = _matmul_tn(dz, sv['h'], name=n("proj_in_w"))
    dx0, _, gs['norm1_g'] = _matmul_rms_bwd(dz, wb['w_in'], sv['x'], sp['g1'], dx1, name=n("proj_in_x"))
    return dx0, gb, gs


SMALL_NAMES = [nm for nm in WEIGHT_NAMES if nm not in BIG_NAMES]
COL_SHARDED = ('w_in', 'w_up', 'w_pe')


def _comm_shards(w):
    return [(jnp.swapaxes(w[nm], 1, 2) if nm in COL_SHARDED else w[nm]).astype(BF16) for nm, _, _ in BIG_COMM]


def _full_weights(gathered):
    out = {nm: g.reshape(N_DEV * r, c) for g, (nm, r, c) in zip(gathered, BIG_COMM)}
    halves = out.pop('w_up').reshape(2, D_FF, D_MODEL)
    out['w_up_g'], out['w_up_v'] = _Sel(halves, 0), _Sel(halves, 1)
    return out


def _small_grads(raw):
    nl = len(raw)
    st = {k: jnp.stack([r[k] for r in raw]) for k in raw[0]}
    tril = jnp.tril(jnp.ones((GMLP_CHUNK, GMLP_CHUNK), bool))
    vec = lambda a: a.reshape(nl, -1)
    out = {nm: vec(st[k]) for nm, k in (('norm1_g', 'norm1_g'), ('norm2_g', 'norm2_g'), ('norm3_g', 'norm3_g'),
                                        ('a_ln_g', 'a_ln_g'), ('a_ln_b', 'a_ln_b'), ('b_conv_b', 'b_cb'),
                                        ('b_ba', 'b_ba'), ('b_bx', 'b_bx'), ('b_lam', 'b_lam'), ('c_lb', 'c_lb'),
                                        ('d_scale', 'd_scale'))}
    out['a_ws'] = jnp.where(tril, st['a_wcat'].reshape(nl, N_HEADS, GMLP_CHUNK, GMLP_CHUNK), 0.0)
    out['a_bs'] = jnp.swapaxes(st['a_bfull'].reshape(nl, GMLP_CHUNK, N_HEADS, HEAD_DIM).sum(-1), 1, 2)
    out['b_conv_w'] = st['b_cw']
    out['b_wa'], out['b_wx'], out['d_w'] = _diag_blocks(st['b_wa']), _diag_blocks(st['b_wx']), _diag_blocks(st['d_wd'])
    out['c_norm_g'] = st['c_ngf'].reshape(nl, N_HEADS, HEAD_DIM).sum(1)
    out['ffn_conv_w'] = jnp.concatenate([st['f_dwg'][:, 0:3], st['f_dwv'][:, 0:3]], axis=2)
    out['ffn_conv_b'] = jnp.concatenate([st['f_dwg'][:, 3], st['f_dwv'][:, 3]], axis=1)
    return out


def _step(w, m, v, x, p, target):
    s = x.shape[1]
    dev = 4 * lax.axis_index("x") + 2 * lax.axis_index("y") + lax.axis_index("c")
    xs = x.reshape(s, D_MODEL)

    shards = _comm_shards(w)
    conv_shapes = [w['b_conv_w'].shape, w['ffn_conv_w'].shape]
    conv_rows = _flat_rows(conv_shapes)
    conv_all = _all_gather(_pack_flat([w['b_conv_w'], w['ffn_conv_w']], conv_rows), name="gather_conv_weights")
    parts = [_unpack_flat(conv_all[d], conv_shapes) for d in range(N_DEV)]
    wf = dict(w)
    wf['b_conv_w_full'] = jnp.concatenate([pt[0] for pt in parts], axis=-1)
    wf['ffn_conv_w_full'] = jnp.concatenate([pt[1] for pt in parts], axis=-1)
    lbs = _lbs_fwd(w['c_lb'], name="hgrn_bounds")

    stacked = _stacked_params(wf, lbs)
    p_all = p.reshape(DEPTH, s, PLE_DIM).astype(BF16)
    xl, saved, wbs, sps = xs, [], [], []
    gathered = _gather_layer(shards, 0, name="l0_gather_weights")
    for l in range(DEPTH):
        sp = {k: _Sel(a, l) for k, a in stacked.items()}
        if l + 1 < DEPTH:
            own = [x[l + 1] for x in shards]
            after = [conv_all, *gathered] if l == 0 else [xl]
            lands = _place_own(own, after, name=f"l{l + 1}_gather_place")
            started = _exchange_start(own, lands, name=f"l{l + 1}_gather_start")
            sp['g1'] = stacked['g1'][l] + started[-1][0, 0]
        wb = _full_weights(gathered)
        p_bf = p_all[l]
        xl, sv = _layer_fwd(xl, p_bf, wb, sp, l)
        if l + 1 < DEPTH:
            gathered = _exchange_wait(started, xl, name=f"l{l + 1}_gather_wait")[1]
        saved.append((sv, p_bf))
        wbs.append(wb)
        sps.append(sp)
    loss_part, dx, dfinal = _loss_head(xl, w['final_g'].reshape(1, D_MODEL), target.reshape(s, D_MODEL), name="loss_head")
    loss = lax.psum(loss_part[0, 0], ("x", "y", "c"))

    dev1 = dev.astype(jnp.int32).reshape(1)
    names = [nm for nm, _, _ in BIG_COMM]

    def start_reduce(grads, name):
        views = [g.reshape(N_DEV, g.shape[0] // N_DEV, g.shape[1]) for g in grads]
        return _exchange_start(views, [lax.empty(g.shape, g.dtype) for g in views], name=name, per_peer=True)

    def finish_reduce(started, after, lname):
        own, lands = _exchange_wait(started, after, name=f"{lname}_reduce_wait")
        return _sum_devices(lands, own, dev1, name=f"{lname}_reduce_sum")

    reduced, small = [None] * DEPTH, [None] * DEPTH
    pending = None
    for l in range(DEPTH - 1, 0, -1):
        sv, p_bf = saved[l]
        sp = sps[l]
        if pending is not None:
            sp = dict(sp, g3=stacked['g3'][l] + pending[-1][0, 0])
        dx, gb, small[l] = _layer_bwd(dx, sv, p_bf, wbs[l], sp, l)
        if pending is not None:
            reduced[l + 1] = finish_reduce(pending, dx, f"l{l + 1}")
        pending = start_reduce([gb[nm] for nm in names], f"l{l}_reduce_start")
    early = ('w_up', 'w_down', 'w_pe', 'w_pg')
    mid_started = []

    def mid(gb, sp):
        mid_started.append(start_reduce([gb[nm] for nm in early], "l0_reduce_start"))
        return dict(sp, g2=stacked['g2'][0] + mid_started[0][-1][0, 0])

    upper_names = [nm for nm in SMALL_NAMES if nm != 'final_g']
    low_names = upper_names + ['final_g']
    upper = _small_grads(small[1:])
    upper_shapes = [upper[nm].shape for nm in upper_names]
    upper_packed = [_pack_flat([upper[nm] for nm in upper_names], _flat_rows(upper_shapes))]
    upper_started = _exchange_start(upper_packed, _place_own(upper_packed, [], name="upper_small_grads_place"),
                                    name="upper_small_grads_start")

    sv, p_bf = saved[0]
    g3 = stacked['g3'][0] + pending[-1][0, 0] + upper_started[-1][0, 0]
    dx, gb, small[0] = _layer_bwd(dx, sv, p_bf, wbs[0], dict(sps[0], g3=g3), 0, mid=mid)
    reduced[1] = finish_reduce(pending, dx, "l1")
    late = dict(zip(('w_in', 'w_out'), _reduce_layer([gb['w_in'], gb['w_out']], 0)))
    late.update(zip(early, finish_reduce(mid_started[0], late['w_in'], "l0")))
    reduced[0] = [late[nm] for nm in names]
    grad_x = dx.reshape(1, s, D_MODEL)
    low = _small_grads(small[:1])
    low['final_g'] = dfinal.reshape(D_MODEL)
    low_shapes = [low[nm].shape for nm in low_names]
    low_all = _all_gather(_pack_flat([low[nm] for nm in low_names], _flat_rows(low_shapes)), name="gather_small_grads")
    low_sum = dict(zip(low_names, _unpack_flat(_sum_slots(low_all, name="sum_small_grads"), low_shapes)))
    upper_all = _exchange_wait(upper_started, low_all, name="upper_small_grads_wait")[1][0]
    upper_sum = dict(zip(upper_names, _unpack_flat(_sum_slots(upper_all, name="sum_upper_small_grads"), upper_shapes)))
    gsmall = {nm: jnp.concatenate([low_sum[nm], upper_sum[nm]], axis=0) for nm in upper_names}
    gsmall['c_lb'] = _lbs_bwd(w['c_lb'], gsmall['c_lb'], name="hgrn_bounds_bwd")
    gsmall['final_g'] = low_sum['final_g']
    for nm in ('b_conv_w', 'ffn_conv_w'):
        width = w[nm].shape[-1]
        gsmall[nm] = lax.dynamic_slice_in_dim(gsmall[nm], dev * width, width, axis=2)

    grads, delta, new_m, new_v = {}, {}, {}, {}
    for a, (nm, _, _) in enumerate(BIG_COMM):
        t = (lambda x: jnp.swapaxes(x, 1, 2)) if nm in COL_SHARDED else (lambda x: x)
        g = jnp.stack([reduced[l][a] for l in range(DEPTH)])
        d, nm_, nv_ = _adamw(t(w[nm]), g, t(m[nm]), t(v[nm]), name=f"adamw_{nm}")
        grads[nm], delta[nm], new_m[nm], new_v[nm] = t(g), t(d), t(nm_), t(nv_)

    shapes = [w[nm].shape for nm in SMALL_NAMES]
    rows = _flat_rows(shapes)
    pk = lambda t: _pack_flat([t[nm] for nm in SMALL_NAMES], rows)
    d, nm_, nv_ = _adamw(pk(w), pk(gsmall), pk(m), pk(v), name="adamw_small")
    for nm, dd, mm_, vv_ in zip(SMALL_NAMES, _unpack_flat(d, shapes), _unpack_flat(nm_, shapes), _unpack_flat(nv_, shapes)):
        grads[nm], delta[nm], new_m[nm], new_v[nm] = gsmall[nm], dd, mm_, vv_

    return (loss, grad_x, *[grads[nm] for nm in WEIGHT_NAMES], *[delta[nm] for nm in WEIGHT_NAMES],
            *[new_m[nm] for nm in WEIGHT_NAMES], *[new_v[nm] for nm in WEIGHT_NAMES])


def kernel(x, p, norm1_g, w_in, a_ln_g, a_ln_b, a_ws, a_bs, b_conv_w, b_conv_b, b_wa, b_ba, b_wx, b_bx, b_lam, c_lb, c_norm_g, d_w, d_scale, w_out, norm2_g, w_up, ffn_conv_w, ffn_conv_b, w_down, norm3_g, w_pe, w_pg, final_g, loss_target, m_norm1_g, m_w_in, m_a_ln_g, m_a_ln_b, m_a_ws, m_a_bs, m_b_conv_w, m_b_conv_b, m_b_wa, m_b_ba, m_b_wx, m_b_bx, m_b_lam, m_c_lb, m_c_norm_g, m_d_w, m_d_scale, m_w_out, m_norm2_g, m_w_up, m_ffn_conv_w, m_ffn_conv_b, m_w_down, m_norm3_g, m_w_pe, m_w_pg, m_final_g, v_norm1_g, v_w_in, v_a_ln_g, v_a_ln_b, v_a_ws, v_a_bs, v_b_conv_w, v_b_conv_b, v_b_wa, v_b_ba, v_b_wx, v_b_bx, v_b_lam, v_c_lb, v_c_norm_g, v_d_w, v_d_scale, v_w_out, v_norm2_g, v_w_up, v_ffn_conv_w, v_ffn_conv_b, v_w_down, v_norm3_g, v_w_pe, v_w_pg, v_final_g):
    w = dict(norm1_g=norm1_g, w_in=w_in, a_ln_g=a_ln_g, a_ln_b=a_ln_b, a_ws=a_ws, a_bs=a_bs, b_conv_w=b_conv_w, b_conv_b=b_conv_b, b_wa=b_wa, b_ba=b_ba, b_wx=b_wx, b_bx=b_bx, b_lam=b_lam, c_lb=c_lb, c_norm_g=c_norm_g, d_w=d_w, d_scale=d_scale, w_out=w_out, norm2_g=norm2_g, w_up=w_up, ffn_conv_w=ffn_conv_w, ffn_conv_b=ffn_conv_b, w_down=w_down, norm3_g=norm3_g, w_pe=w_pe, w_pg=w_pg, final_g=final_g)
    m = dict(norm1_g=m_norm1_g, w_in=m_w_in, a_ln_g=m_a_ln_g, a_ln_b=m_a_ln_b, a_ws=m_a_ws, a_bs=m_a_bs, b_conv_w=m_b_conv_w, b_conv_b=m_b_conv_b, b_wa=m_b_wa, b_ba=m_b_ba, b_wx=m_b_wx, b_bx=m_b_bx, b_lam=m_b_lam, c_lb=m_c_lb, c_norm_g=m_c_norm_g, d_w=m_d_w, d_scale=m_d_scale, w_out=m_w_out, norm2_g=m_norm2_g, w_up=m_w_up, ffn_conv_w=m_ffn_conv_w, ffn_conv_b=m_ffn_conv_b, w_down=m_w_down, norm3_g=m_norm3_g, w_pe=m_w_pe, w_pg=m_w_pg, final_g=m_final_g)
    v = dict(norm1_g=v_norm1_g, w_in=v_w_in, a_ln_g=v_a_ln_g, a_ln_b=v_a_ln_b, a_ws=v_a_ws, a_bs=v_a_bs, b_conv_w=v_b_conv_w, b_conv_b=v_b_conv_b, b_wa=v_b_wa, b_ba=v_b_ba, b_wx=v_b_wx, b_bx=v_b_bx, b_lam=v_b_lam, c_lb=v_c_lb, c_norm_g=v_c_norm_g, d_w=v_d_w, d_scale=v_d_scale, w_out=v_w_out, norm2_g=v_norm2_g, w_up=v_w_up, ffn_conv_w=v_ffn_conv_w, ffn_conv_b=v_ffn_conv_b, w_down=v_w_down, norm3_g=v_norm3_g, w_pe=v_w_pe, w_pg=v_w_pg, final_g=v_final_g)
    return _step(w, m, v, x, p, loss_target)
```

```python
import functools

import jax
import jax.numpy as jnp
from jax import lax
from jax.experimental import pallas as pl
from jax.experimental.pallas import tpu as pltpu

F32 = jnp.float32
BF16 = jnp.bfloat16
MESH = pl.DeviceIdType.MESH

D_MODEL = 1024
DEPTH = 4
PLE_DIM = 256
W_GRP = 256
N_HEADS = 4
HEAD_DIM = 64
GMLP_CHUNK = 128
RGLRU_C = 8.0
HGRN_CHUNK = 64
HGRN_SUB = 16
HGRN_STEP_CHUNKS = 8
POOL_WINDOWS = (2, 4, 8, 16)
D_FF = 2816
D_PROJ = 2304
EPS = 1e-6
ADAM_LR = 0.001
ADAM_B1 = 0.9
ADAM_B2 = 0.999
ADAM_EPS = 1e-08
ADAM_WD = 0.01
ADAM_STEP = 10

N_DEV = 8
MIB = 2 ** 20
V7X_VMEM_BYTES = 64 * MIB
HGRN_EXP_CLAMP = 60.0

WEIGHT_NAMES = ['norm1_g', 'w_in', 'a_ln_g', 'a_ln_b', 'a_ws', 'a_bs', 'b_conv_w', 'b_conv_b', 'b_wa', 'b_ba', 'b_wx',
                'b_bx', 'b_lam', 'c_lb', 'c_norm_g', 'd_w', 'd_scale', 'w_out', 'norm2_g', 'w_up', 'ffn_conv_w',
                'ffn_conv_b', 'w_down', 'norm3_g', 'w_pe', 'w_pg', 'final_g']
BIG_NAMES = ('w_in', 'w_out', 'w_up', 'w_down', 'w_pe', 'w_pg')


def _vmem_limit(block_bytes):
    want = 2 * block_bytes + 24 * MIB
    return int(min(max(want, 32 * MIB), V7X_VMEM_BYTES - 8 * MIB))


def _in_hbm(x):
    return pltpu.with_memory_space_constraint(x, pltpu.HBM)


def _out_hbm(s):
    return pltpu.HBM(s.shape, s.dtype)


def _pcall(body, *, name, out_shape, grid=None, in_specs=None, out_specs=None, scratch_shapes=(),
           semantics=None, block_bytes=0, aliases=None, pin=True):
    kw = {} if aliases is None else {"input_output_aliases": aliases}
    if pin:
        out_shape = tuple(_out_hbm(s) for s in out_shape) if isinstance(out_shape, (tuple, list)) else _out_hbm(out_shape)
    if grid is not None:
        kw["grid"] = grid
    if in_specs is not None:
        kw["in_specs"] = in_specs
    if out_specs is not None:
        kw["out_specs"] = out_specs
    params = pltpu.CompilerParams(dimension_semantics=semantics, vmem_limit_bytes=_vmem_limit(block_bytes))
    call = pl.pallas_call(body, name=name, out_shape=out_shape, scratch_shapes=list(scratch_shapes),
                          compiler_params=params, **kw)
    return (lambda *args: call(*[_in_hbm(a) for a in args])) if pin else call


def _pick(n, cands):
    for c in cands:
        if n % c == 0:
            return c
    return n


def _nbytes(shape, dtype):
    n = 1
    for s in shape:
        n *= s
    return n * jnp.dtype(dtype).itemsize


def _sds(shape, dtype):
    return jax.ShapeDtypeStruct(tuple(shape), dtype)


class _Sel:
    def __init__(self, arr, *idx):
        self.arr, self.idx = arr, tuple(idx)
        self.shape = arr.shape[len(idx):]
        self.ndim = len(self.shape)
        self.dtype = arr.dtype


def _arr(a):
    return a.arr if isinstance(a, _Sel) else a


def _spec(a, block=None, index=None):
    block = tuple(a.shape) if block is None else tuple(block)
    index = (lambda *g: (0,) * len(block)) if index is None else index
    if isinstance(a, _Sel):
        lead = a.idx
        return pl.BlockSpec((None,) * len(lead) + block, lambda *g: lead + tuple(index(*g)))
    return pl.BlockSpec(block, lambda *g: tuple(index(*g)))


def _ospec(a):
    return pl.BlockSpec(tuple(a.shape), lambda *g: (0,) * a.ndim)


def _rows_of(shape):
    return lax.broadcasted_iota(jnp.int32, shape, 0)


def _lanes_of(shape):
    return lax.broadcasted_iota(jnp.int32, shape, 1)


def _sdn(x, k, fill):
    n = x.shape[0]
    return jnp.where(_rows_of(x.shape) >= k, pltpu.roll(x, k % n, 0), fill)


def _sup(x, k, fill):
    n = x.shape[0]
    return jnp.where(_rows_of(x.shape) < n - k, pltpu.roll(x, (n - k) % n, 0), fill)


@functools.partial(jax.custom_vjp, nondiff_argnums=(1,))
def _shift_dn(x, k):
    return pltpu.roll(x, k, 0)


def _shift_dn_fwd(x, k):
    return pltpu.roll(x, k, 0), None


def _shift_dn_bwd(k, _, g):
    return (pltpu.roll(g, g.shape[0] - k, 0),)


_shift_dn.defvjp(_shift_dn_fwd, _shift_dn_bwd)


SUBLANES = 8


def _lin_scan_impl(a, b, h0):
    n = a.shape[0]
    pos = _rows_of(a.shape) % SUBLANES
    aa, bb = a, b
    k = 1
    while k < SUBLANES:
        keep = pos >= k
        bb = bb + jnp.where(keep, aa * pltpu.roll(bb, k, 0), 0.0)
        aa = aa * jnp.where(keep, pltpu.roll(aa, k, 0), 1.0)
        k *= 2
    out, carry = [], h0
    for r in range(n // SUBLANES):
        rows = slice(r * SUBLANES, (r + 1) * SUBLANES)
        hr = bb[rows] + aa[rows] * carry
        out.append(hr)
        carry = hr[SUBLANES - 1:]
    return jnp.concatenate(out, axis=0)


@jax.custom_vjp
def _lin_scan(a, b, h0):
    return _lin_scan_impl(a, b, h0)


def _lin_scan_fwd(a, b, h0):
    h = _lin_scan_impl(a, b, h0)
    return h, (a, h, h0)


def _lin_scan_bwd(res, g):
    a, h, h0 = res
    n = a.shape[0]
    pos = _rows_of(a.shape) % SUBLANES
    cc, gg = _sup(a, 1, 0.0), g
    k = 1
    while k < SUBLANES:
        keep = pos < SUBLANES - k
        gg = gg + jnp.where(keep, cc * pltpu.roll(gg, n - k, 0), 0.0)
        cc = cc * jnp.where(keep, pltpu.roll(cc, n - k, 0), 1.0)
        k *= 2
    out, carry = [], jnp.zeros_like(h0)
    for r in range(n // SUBLANES - 1, -1, -1):
        rows = slice(r * SUBLANES, (r + 1) * SUBLANES)
        gr = gg[rows] + cc[rows] * carry
        out.append(gr)
        carry = gr[:1]
    gg = jnp.concatenate(out[::-1], axis=0)
    first = _rows_of(a.shape) == 0
    hprev = jnp.where(first, h0, _sdn(h, 1, 0.0))
    dh0 = jnp.sum(jnp.where(first, a * gg, 0.0), axis=0, keepdims=True)
    return gg * hprev, gg, dh0


_lin_scan.defvjp(_lin_scan_fwd, _lin_scan_bwd)


def _cumsum_sub_impl(x):
    pos = _rows_of(x.shape) % HGRN_SUB
    k = 1
    while k < HGRN_SUB:
        x = x + jnp.where(pos >= k, pltpu.roll(x, k, 0), 0.0)
        k *= 2
    return x


@jax.custom_vjp
def _cumsum_sub(x):
    return _cumsum_sub_impl(x)


def _cumsum_sub_fwd(x):
    return _cumsum_sub_impl(x), None


def _cumsum_sub_bwd(_, g):
    n = g.shape[0]
    pos = _rows_of(g.shape) % HGRN_SUB
    k = 1
    while k < HGRN_SUB:
        g = g + jnp.where(pos < HGRN_SUB - k, pltpu.roll(g, n - k, 0), 0.0)
        k *= 2
    return (g,)


_cumsum_sub.defvjp(_cumsum_sub_fwd, _cumsum_sub_bwd)


def _dot(a, b, ca, cb):
    return lax.dot_general(a.astype(BF16), b.astype(BF16), (((ca,), (cb,)), ((), ())), preferred_element_type=F32)


@jax.custom_vjp
def _mm(a, b):
    return _dot(a, b, 1, 0)


def _mm_fwd(a, b):
    return _dot(a, b, 1, 0), (a, b)


def _mm_bwd(res, g):
    a, b = res
    return _dot(g, b, 1, 1), _dot(a, g, 0, 0)


_mm.defvjp(_mm_fwd, _mm_bwd)


@jax.custom_vjp
def _mm_nt(a, b):
    return _dot(a, b, 1, 1)


def _mm_nt_fwd(a, b):
    return _dot(a, b, 1, 1), (a, b)


def _mm_nt_bwd(res, g):
    a, b = res
    return _dot(g, b, 1, 0), _dot(g, a, 0, 0)


_mm_nt.defvjp(_mm_nt_fwd, _mm_nt_bwd)


@jax.custom_vjp
def _mm_tn(a, b):
    return _dot(a, b, 0, 0)


def _mm_tn_fwd(a, b):
    return _dot(a, b, 0, 0), (a, b)


def _mm_tn_bwd(res, g):
    a, b = res
    return _dot(b, g, 1, 1), _dot(a, g, 1, 0)


_mm_tn.defvjp(_mm_tn_fwd, _mm_tn_bwd)


def _head_mask(shape, h):
    return (_lanes_of(shape) // HEAD_DIM) == h


def _stack_heads(x):
    return jnp.concatenate([jnp.where(_head_mask(x.shape, h), x, 0.0) for h in range(N_HEADS)], axis=0)


def _unstack_heads(p):
    r = p.shape[0] // N_HEADS
    out = None
    for h in range(N_HEADS):
        blk = p[h * r:(h + 1) * r]
        term = jnp.where(_head_mask(blk.shape, h), blk, 0.0)
        out = term if out is None else out + term
    return out


def _segmean_impl(x):
    n = x.shape[1]
    same = (lax.broadcasted_iota(jnp.int32, (n, n), 0) // HEAD_DIM) == (lax.broadcasted_iota(jnp.int32, (n, n), 1) // HEAD_DIM)
    m = jnp.where(same, 1.0 / HEAD_DIM, 0.0).astype(BF16)
    hi = x.astype(BF16)
    lo = (x - hi.astype(F32)).astype(BF16)
    dn = (((1,), (0,)), ((), ()))
    return (lax.dot_general(hi, m, dn, preferred_element_type=F32)
            + lax.dot_general(lo, m, dn, preferred_element_type=F32))


@jax.custom_vjp
def _segmean(x):
    return _segmean_impl(x)


def _segmean_fwd(x):
    return _segmean_impl(x), None


def _segmean_bwd(_, g):
    return (_segmean_impl(g),)


_segmean.defvjp(_segmean_fwd, _segmean_bwd)


GELU_C = 0.7978845608028654
GELU_A = 0.044715


@jax.custom_vjp
def _gelu(x):
    return 0.5 * x * (1.0 + jnp.tanh(GELU_C * x * (1.0 + GELU_A * (x * x))))


def _gelu_fwd(x):
    x2 = x * x
    t = jnp.tanh(GELU_C * x * (1.0 + GELU_A * x2))
    return 0.5 * x * (1.0 + t), (x, x2, t)


def _gelu_bwd(res, g):
    x, x2, t = res
    half = 0.5 * (1.0 + t)
    return (g * (half + (0.5 * GELU_C) * x * (1.0 - t * t) * (1.0 + (3.0 * GELU_A) * x2)),)


_gelu.defvjp(_gelu_fwd, _gelu_bwd)


def _log1p(u):
    w = 1.0 + u
    return jnp.where(w == 1.0, u, jnp.log(w) * (u / (w - 1.0)))


def _softplus(y):
    return jnp.maximum(y, 0.0) + _log1p(jnp.exp(-jnp.abs(y)))


def _rms(x, g):
    return x * lax.rsqrt(jnp.mean(x * x, axis=-1, keepdims=True) + EPS) * g


def _gmlp_chunk(zu, zv, ln_g, ln_b, wcat, bfull):
    u = _gelu(zu)
    v = _gelu(zv)
    mu = jnp.mean(v, axis=-1, keepdims=True)
    var = jnp.mean(jnp.square(v - mu), axis=-1, keepdims=True)
    vn = (v - mu) * lax.rsqrt(var + EPS) * ln_g + ln_b
    sv = _unstack_heads(_mm(wcat, vn)) + bfull
    return u * sv


def _rglru_tile(xb_ext, gb, h0, cw, cb, wa, ba, wx, bx, lam):
    xc = (cb + cw[0:1] * _shift_dn(xb_ext, 3) + cw[1:2] * _shift_dn(xb_ext, 2) + cw[2:3] * _shift_dn(xb_ext, 1)
          + cw[3:4] * xb_ext)[8:]
    r = jax.nn.sigmoid(_mm(xc, wa) + ba)
    i = jax.nn.sigmoid(_mm(xc, wx) + bx)
    log_a = (-RGLRU_C) * r * _softplus(-lam)
    a = jnp.exp(log_a)
    mult = jnp.sqrt(-jnp.tanh(log_a) * (a * a + 1.0))
    h = _lin_scan(a, mult * (i * xc), h0)
    y = h * _gelu(gb)
    h_last = jnp.sum(jnp.where(_rows_of(h.shape) == h.shape[0] - 1, h, 0.0), axis=0, keepdims=True)
    return y, h_last


def _pool_tile(xd_ext, inv, wd, scale):
    s1 = xd_ext + _shift_dn(xd_ext, 1)
    s2 = s1 + _shift_dn(s1, 2)
    s3 = s2 + _shift_dn(s2, 4)
    s4 = s3 + _shift_dn(s3, 8)
    grp = _lanes_of(xd_ext.shape) // HEAD_DIM
    win = jnp.where(grp == 0, s1, jnp.where(grp == 1, s2, jnp.where(grp == 2, s3, s4)))
    pooled = win[16:] * inv - xd_ext[16:]
    return _mm(pooled, wd) * scale


def _hgrn_chunk(q, f, i, g, st, lb, ngf):
    n = q.shape[0]
    nsub = n // HGRN_SUB
    qs = jax.nn.silu(q)
    fg = lb + (1.0 - lb) * jax.nn.sigmoid(f)
    lf = jnp.log(fg)
    k = 1.0 - fg
    bl = _cumsum_sub(lf)
    row = _rows_of(q.shape)
    blk = row // HGRN_SUB
    betas = [jnp.zeros_like(lb)]
    for s in range(nsub):
        tot = jnp.sum(jnp.where(row == s * HGRN_SUB + HGRN_SUB - 1, bl, 0.0), axis=0, keepdims=True)
        betas.append(betas[-1] + tot)
    b_end = betas[nsub]
    beta_full = jnp.zeros_like(q)
    for s in range(1, nsub):
        beta_full = jnp.where(blk == s, betas[s], beta_full)
    qh = qs * jnp.exp(bl)
    qt = qh * jnp.exp(beta_full)
    b_all = beta_full + bl
    kt = k * jnp.exp(b_end - b_all)
    outs = []
    for s in range(nsub):
        kh = k * jnp.exp(jnp.minimum(betas[s] - b_all, HGRN_EXP_CLAMP))
        qstk = _stack_heads(qh[s * HGRN_SUB:(s + 1) * HGRN_SUB])
        att = _mm_nt(qstk, kh)
        ar = _rows_of(att.shape) % HGRN_SUB + s * HGRN_SUB
        att = jnp.where(_lanes_of(att.shape) <= ar, att, 0.0)
        outs.append(_unstack_heads(_mm(att, i)))
    o = jnp.concatenate(outs, axis=0) + _mm_nt(qt, st)
    same = (_rows_of(st.shape) // HEAD_DIM) == (_lanes_of(st.shape) // HEAD_DIM)
    st_new = st * jnp.exp(b_end) + jnp.where(same, _mm_tn(i, kt), 0.0)
    on = o * lax.rsqrt(_segmean(o * o) + EPS) * ngf
    return on * jax.nn.silu(g), st_new


def _ffn_tile(eg, ev, wg, bg, wv, bv):
    gt = (bg + wg[0:1] * _shift_dn(eg, 2) + wg[1:2] * _shift_dn(eg, 1) + wg[2:3] * eg)[8:]
    val = (bv + wv[0:1] * _shift_dn(ev, 2) + wv[1:2] * _shift_dn(ev, 1) + wv[2:3] * ev)[8:]
    return _gelu(gt) * val


MXU_WIDTH = 256
MATMUL_BLOCK_BUDGET = 18 * MIB


def _matmul_tiles(m, k, n, a_dtype, b_dtype, out_dtype, has_res):
    best = None
    for tm in (2048, 1024, 512, 256):
        if m % tm:
            continue
        for tn in (1024, 768, 1408, 512, 256, 128):
            if n % tn:
                continue
            blk = (_nbytes((tm, k), a_dtype) + _nbytes((k, tn), b_dtype) + _nbytes((tm, tn), out_dtype)
                   + (_nbytes((tm, tn), F32) if has_res else 0))
            if blk > MATMUL_BLOCK_BUDGET:
                continue
            waste = -(-tn // MXU_WIDTH) * MXU_WIDTH / tn
            cost = (m // tm) * (n // tn) + 64 * (waste - 1.0)
            if best is None or cost < best[0]:
                best = (cost, tm, tn, blk)
    assert best is not None, (m, k, n)
    return best[1:]


def _matmul(a, b, *, name, nt=False, res=None, out_dtype=F32):
    m, k = a.shape
    n = b.shape[0] if nt else b.shape[1]
    tm, tn, blk = _matmul_tiles(m, k, n, a.dtype, b.dtype, out_dtype, res is not None)
    dims = (((1,), (1,)), ((), ())) if nt else (((1,), (0,)), ((), ()))

    def body(*refs):
        if res is None:
            a_ref, b_ref, o_ref = refs
        else:
            a_ref, b_ref, r_ref, o_ref = refs
        acc = lax.dot_general(a_ref[...], b_ref[...], dims, preferred_element_type=F32)
        if res is not None:
            acc = acc + r_ref[...]
        o_ref[...] = acc.astype(out_dtype)

    in_specs = [pl.BlockSpec((tm, k), lambda i, j: (i, 0)),
                _spec(b, (tn, k), lambda i, j: (j, 0)) if nt else _spec(b, (k, tn), lambda i, j: (0, j))]
    args = [a, _arr(b)]
    if res is not None:
        in_specs.append(pl.BlockSpec((tm, tn), lambda i, j: (i, j)))
        args.append(res)
    return _pcall(body, name=name, out_shape=_sds((m, n), out_dtype), grid=(m // tm, n // tn), in_specs=in_specs,
                  out_specs=pl.BlockSpec((tm, tn), lambda i, j: (i, j)), semantics=("parallel", "parallel"),
                  block_bytes=blk + _nbytes((tm, tn), F32))(*args)


def _matmul_rms_bwd(a, b, x, g, dres, *, name, nt=False, res=None):
    m, k = a.shape
    n = b.shape[0] if nt else b.shape[1]
    tm = _pick(m, (512, 256))
    dims = (((1,), (1,)), ((), ())) if nt else (((1,), (0,)), ((), ()))

    def body(*refs):
        a_ref, b_ref, x_ref, g_ref, dr_ref = refs[:5]
        dx_ref, dxb_ref, dg_ref = refs[-3:]
        dh = lax.dot_general(a_ref[...], b_ref[...], dims, preferred_element_type=F32)
        if res is not None:
            dh = dh + refs[5][...]
        _, vjp = jax.vjp(_rms, x_ref[...], g_ref[...])
        dxn, dg = vjp(dh)
        dx = dr_ref[...] + dxn
        dx_ref[...] = dx
        dxb_ref[...] = dx.astype(BF16)
        _acc_out(dg_ref, dg, pl.program_id(0) == 0)

    row = pl.BlockSpec((tm, n), lambda i: (i, 0))
    vec = pl.BlockSpec((1, n), lambda i: (0, 0))
    in_specs = [pl.BlockSpec((tm, k), lambda i: (i, 0)),
                _spec(b, (n, k), lambda i: (0, 0)) if nt else _spec(b, (k, n), lambda i: (0, 0)), row, _spec(g), row]
    args = [a, _arr(b), x, _arr(g), dres]
    if res is not None:
        in_specs.append(row)
        args.append(res)
    blk = _nbytes((tm, k), a.dtype) + _nbytes((k, n), b.dtype) + 6 * _nbytes((tm, n), F32)
    return _pcall(body, name=name, out_shape=(_sds((m, n), F32), _sds((m, n), BF16), _sds((1, n), F32)), grid=(m // tm,),
                  in_specs=in_specs, out_specs=(row, row, vec), semantics=("arbitrary",), block_bytes=blk)(*args)


def _matmul_tn(a, b, *, name, out_dtype=BF16, out_rows=None, row_off=0, into=None):
    m, k1 = a.shape
    n = b.shape[1]
    tk = _pick(k1, (512, 256, 128))
    off = row_off // tk
    assert off * tk == row_off

    def body(a_ref, b_ref, *rest):
        rest[-1][...] = lax.dot_general(a_ref[...], b_ref[...], (((0,), (0,)), ((), ())),
                                        preferred_element_type=F32).astype(out_dtype)

    blk = 2 * _nbytes((m, tk), a.dtype) + _nbytes((m, n), b.dtype) + _nbytes((tk, n), F32)
    in_specs = [pl.BlockSpec((m, tk), lambda i: (0, i)), pl.BlockSpec((m, n), lambda i: (0, 0))]
    args = [a, b]
    if into is not None:
        in_specs.append(HBM_SPEC)
        args.append(into)
    return _pcall(body, name=name, out_shape=_sds((out_rows or k1, n), out_dtype), grid=(k1 // tk,), in_specs=in_specs,
                  out_specs=pl.BlockSpec((tk, n), lambda i: (i + off, 0)), semantics=("parallel",), block_bytes=blk,
                  aliases=None if into is None else {2: 0})(*args)


def _rms_matmul(x, g, bs, *, name, nt=False, ple=None):
    m, d = x.shape
    n = bs[0].shape[0] if nt else bs[0].shape[1]
    nb = len(bs)
    nout = nb if ple is None else 3
    best = None
    for tm_c in (1024, 512, 256):
        for tn_c in (1408, 1024, 768, 512, 256, 128):
            if m % tm_c or n % tn_c:
                continue
            blk_c = (_nbytes((tm_c, d), F32) + 2 * _nbytes((tm_c, d), BF16) + nb * _nbytes((d, tn_c), BF16)
                     + (nout + 1) * _nbytes((tm_c, tn_c), F32))
            steps = (m // tm_c) * (n // tn_c)
            if blk_c <= MATMUL_BLOCK_BUDGET and (best is None or steps < best[0]):
                best = (steps, tm_c, tn_c, blk_c)
    _, tm, tn, blk = best
    dims = (((1,), (1,)), ((), ())) if nt else (((1,), (0,)), ((), ()))

    def body(*refs):
        x_ref, g_ref, b_refs = refs[0], refs[1], refs[2:2 + nb]
        rest = refs[2 + nb:]
        h_scr = rest[-1]
        j = pl.program_id(1)

        @pl.when(j == 0)
        def _():
            h = _rms(x_ref[...], g_ref[...]).astype(BF16)
            h_scr[...] = h
            rest[-2 - nb - (2 if ple else 0)][...] = h

        h = h_scr[...]
        if ple is None:
            for k in range(nb):
                rest[-1 - nb + k][...] = lax.dot_general(h, b_refs[k][...], dims, preferred_element_type=F32)
        else:
            p_ref, wpe_ref, xt_ref = rest[0], rest[1], rest[2]
            gl_ref, pe_ref, out_ref = rest[-4], rest[-3], rest[-2]
            gl = lax.dot_general(h, b_refs[0][...], dims, preferred_element_type=F32)
            pe = lax.dot_general(p_ref[...], wpe_ref[...], (((1,), (1,)), ((), ())), preferred_element_type=F32)
            gl_ref[...] = gl
            pe_ref[...] = pe
            out_ref[...] = xt_ref[...] + pe * jax.nn.sigmoid(gl)

    row = pl.BlockSpec((tm, d), lambda i, j: (i, 0))
    tile = pl.BlockSpec((tm, tn), lambda i, j: (i, j))
    in_specs = [row, _spec(g)] + [_spec(b, (tn, d), lambda i, j: (j, 0)) if nt else _spec(b, (d, tn), lambda i, j: (0, j))
                                  for b in bs]
    args = [x, _arr(g)] + [_arr(b) for b in bs]
    out_shape, out_specs = [_sds((m, d), BF16)], [row]
    if ple is None:
        out_shape += [_sds((m, n), F32)] * nb
        out_specs += [tile] * nb
    else:
        p, wpe = ple
        in_specs += [pl.BlockSpec((tm, p.shape[1]), lambda i, j: (i, 0)), _spec(wpe, (tn, p.shape[1]), lambda i, j: (j, 0)),
                     tile]
        args += [p, _arr(wpe), x]
        out_shape += [_sds((m, n), F32)] * 3
        out_specs += [tile] * 3
    outs = _pcall(body, name=name, out_shape=tuple(out_shape), grid=(m // tm, n // tn), in_specs=in_specs,
                  out_specs=tuple(out_specs), scratch_shapes=[pltpu.VMEM((tm, d), BF16)],
                  semantics=("parallel", "arbitrary"), block_bytes=blk)(*args)
    return outs[0], list(outs[1:])


def _up_ffn_fwd(x, g, wg, wv, cwf, cbf, *, name):
    m, d = x.shape
    n = wg.shape[0]
    tm = _pick(m, (256, 128))
    tn = _pick(n, (1408, 256, 128))
    nj = n // tn
    dims = (((1,), (1,)), ((), ()))

    def body(x_ref, g_ref, wg_ref, wv_ref, tg_ref, bg_ref, tv_ref, bv_ref, h_ref, hg_ref, hv_ref, a_ref, cg_scr, cv_scr):
        i = pl.program_id(1)
        h = _rms(x_ref[...], g_ref[...]).astype(BF16)
        h_ref[...] = h
        hg = lax.dot_general(h, wg_ref[...], dims, preferred_element_type=F32)
        hv = lax.dot_general(h, wv_ref[...], dims, preferred_element_type=F32)
        hg_ref[...] = hg
        hv_ref[...] = hv
        eg = jnp.concatenate([jnp.where(i == 0, 0.0, cg_scr[...]), hg], axis=0)
        ev = jnp.concatenate([jnp.where(i == 0, 0.0, cv_scr[...]), hv], axis=0)
        a_ref[...] = _ffn_tile(eg, ev, tg_ref[...], bg_ref[...], tv_ref[...], bv_ref[...]).astype(BF16)
        cg_scr[...] = hg[tm - 8:]
        cv_scr[...] = hv[tm - 8:]

    row = pl.BlockSpec((tm, d), lambda j, i: (i, 0))
    hrow = pl.BlockSpec((tm, d), lambda j, i: (j * (m // tm) + i, 0))
    tile = pl.BlockSpec((tm, tn), lambda j, i: (i, j))
    wspec = lambda w: _spec(w, (tn, d), lambda j, i: (j, 0))
    taps = lambda off: _spec(cwf, (3, tn), lambda j, i: (0, j + off))
    bias = lambda off: _spec(cbf, (1, tn), lambda j, i: (0, j + off))
    blk = (_nbytes((tm, d), F32) + _nbytes((tm, d), BF16) + 2 * _nbytes((tn, d), BF16) + 12 * _nbytes((tm, tn), F32))
    return _pcall(body, name=name,
                  out_shape=(_sds((nj * m, d), BF16), _sds((m, n), F32), _sds((m, n), F32), _sds((m, n), BF16)),
                  grid=(nj, m // tm),
                  in_specs=[row, _spec(g), wspec(wg), wspec(wv), taps(0), bias(0), taps(nj), bias(nj)],
                  out_specs=(hrow, tile, tile, tile),
                  scratch_shapes=[pltpu.VMEM((8, tn), F32), pltpu.VMEM((8, tn), F32)],
                  semantics=("arbitrary", "arbitrary"), block_bytes=blk)(
                      x, _arr(g), _arr(wg), _arr(wv), _arr(cwf), _arr(cbf), _arr(cwf), _arr(cbf))


def _ple_bwd(dx, gl, pe, *, name):
    s, d = dx.shape
    tm = _pick(s, (512, 256))

    def body(dx_ref, gl_ref, pe_ref, dpe_ref, dgl_ref):
        gate = jax.nn.sigmoid(gl_ref[...])
        dxv = dx_ref[...]
        dpe_ref[...] = (dxv * gate).astype(BF16)
        dgl_ref[...] = (dxv * pe_ref[...] * gate * (1.0 - gate)).astype(BF16)

    row = pl.BlockSpec((tm, d), lambda i: (i, 0))
    return _pcall(body, name=name, out_shape=(_sds((s, d), BF16), _sds((s, d), BF16)), grid=(s // tm,),
                  in_specs=[row, row, row], out_specs=(row, row), semantics=("parallel",),
                  block_bytes=5 * _nbytes((tm, d), F32))(dx, gl, pe)


def _loss_head(x, g, target, *, name):
    s, d = x.shape
    tm = _pick(s, (256, 128))

    def tile_loss(xv, gv, tv):
        err = jnp.square(_rms(xv, gv) - tv)
        return 0.5 * jnp.sum(jnp.mean(err, axis=-1, keepdims=True), axis=0, keepdims=True)

    def body(x_ref, g_ref, t_ref, l_ref, dx_ref, dg_ref):
        lv, vjp = jax.vjp(tile_loss, x_ref[...], g_ref[...], t_ref[...])
        dxv, dgv, _ = vjp(jnp.ones((1, 1), F32))
        dx_ref[...] = dxv

        @pl.when(pl.program_id(0) == 0)
        def _():
            l_ref[...] = jnp.zeros_like(l_ref)
            dg_ref[...] = jnp.zeros_like(dg_ref)

        l_ref[...] += jnp.broadcast_to(lv, l_ref.shape)
        dg_ref[...] += dgv

    row = pl.BlockSpec((tm, d), lambda i: (i, 0))
    vec = pl.BlockSpec((1, d), lambda i: (0, 0))
    return _pcall(body, name=name, out_shape=(_sds((8, 128), F32), _sds((s, d), F32), _sds((1, d), F32)),
                  grid=(s // tm,), in_specs=[row, vec, row],
                  out_specs=(pl.BlockSpec((8, 128), lambda i: (0, 0)), row, vec), semantics=("arbitrary",),
                  block_bytes=8 * _nbytes((tm, d), F32))(x, g, target)


def _acc_out(ref, val, first):
    @pl.when(first)
    def _():
        ref[...] = jnp.zeros_like(ref)

    ref[...] += val


def _gmlp_fwd(z, ln_g, ln_b, wcat, bfull, *, name):
    s = z.shape[0]
    t = _pick(s, (512, 256, 128))
    nch = t // GMLP_CHUNK

    def body(zu_ref, zv_ref, g_ref, b_ref, w_ref, bf_ref, o_ref):
        for c in range(nch):
            rows = pl.ds(c * GMLP_CHUNK, GMLP_CHUNK)
            o_ref[rows, :] = _gmlp_chunk(zu_ref[rows, :], zv_ref[rows, :], g_ref[...], b_ref[...], w_ref[...],
                                         bf_ref[...]).astype(BF16)

    col = lambda c: pl.BlockSpec((t, W_GRP), lambda i: (i, c))
    params = (ln_g, ln_b, wcat, bfull)
    return _pcall(body, name=name, out_shape=_sds((s, D_MODEL), BF16), grid=(s // t,),
                  in_specs=[col(0), col(1)] + [_spec(a) for a in params],
                  out_specs=pl.BlockSpec((t, W_GRP), lambda i: (i, 0)), semantics=("parallel",),
                  block_bytes=4 * _nbytes((t, W_GRP), F32))(z, z, *[_arr(a) for a in params])


def _gmlp_bwd(z, dmix, ln_g, ln_b, wcat, bfull, *, name):
    s = z.shape[0]
    t = _pick(s, (512, 256, 128))
    nch = t // GMLP_CHUNK

    def body(zu_ref, zv_ref, dy_ref, g_ref, b_ref, w_ref, bf_ref, dz_ref, dg_ref, db_ref, dw_ref, dbf_ref):
        acc = None
        for c in range(nch):
            rows = pl.ds(c * GMLP_CHUNK, GMLP_CHUNK)
            _, vjp = jax.vjp(_gmlp_chunk, zu_ref[rows, :], zv_ref[rows, :], g_ref[...], b_ref[...], w_ref[...],
                             bf_ref[...])
            du, dv, *dps = vjp(dy_ref[rows, :])
            dz_ref[rows, :] = jnp.concatenate([du, dv], axis=1).astype(BF16)
            acc = dps if acc is None else [x + y for x, y in zip(acc, dps)]
        first = pl.program_id(0) == 0
        for ref, val in zip((dg_ref, db_ref, dw_ref, dbf_ref), acc):
            _acc_out(ref, val, first)

    col = lambda c: pl.BlockSpec((t, W_GRP), lambda i: (i, c))
    params = (ln_g, ln_b, wcat, bfull)
    return _pcall(body, name=name,
                  out_shape=(_sds((s, D_PROJ), BF16),) + tuple(_sds(a.shape, F32) for a in params),
                  grid=(s // t,), in_specs=[col(0), col(1), col(0)] + [_spec(a) for a in params],
                  out_specs=(pl.BlockSpec((t, 2 * W_GRP), lambda i: (i, 0)),) + tuple(_ospec(a) for a in params),
                  semantics=("arbitrary",),
                  block_bytes=8 * _nbytes((t, W_GRP), F32))(z, z, dmix, *[_arr(a) for a in params])


def _rglru_fwd(z, prm, mix, *, name):
    s = z.shape[0]
    t = _pick(s, (512, 256, 128))
    nt = s // t

    def body(xb_ref, halo_ref, gb_ref, *rest):
        prm_refs, (y_ref, h0s_ref, h_scr) = rest[:len(prm)], rest[len(prm) + 1:]
        i = pl.program_id(0)

        @pl.when(i == 0)
        def _():
            h_scr[...] = jnp.zeros_like(h_scr)

        halo = jnp.where(i == 0, 0.0, halo_ref[...])
        h0 = h_scr[...]
        y, h_last = _rglru_tile(jnp.concatenate([halo, xb_ref[...]], axis=0), gb_ref[...], h0,
                                *[r[...] for r in prm_refs])
        y_ref[...] = y.astype(BF16)
        h0s_ref[...] = jnp.broadcast_to(h0, h0s_ref.shape)
        h_scr[...] = h_last

    in_specs = [pl.BlockSpec((t, W_GRP), lambda i: (i, 2)),
                pl.BlockSpec((8, W_GRP), lambda i: (jnp.maximum(i * (t // 8) - 1, 0), 2)),
                pl.BlockSpec((t, W_GRP), lambda i: (i, 3))] + [_spec(a) for a in prm] + [HBM_SPEC]
    return _pcall(body, name=name, out_shape=(_sds(mix.shape, BF16), _sds((nt, 8, W_GRP), F32)), grid=(nt,),
                  in_specs=in_specs,
                  out_specs=(pl.BlockSpec((t, W_GRP), lambda i: (i, 1)), pl.BlockSpec((None, 8, W_GRP), lambda i: (i, 0, 0))),
                  scratch_shapes=[pltpu.VMEM((1, W_GRP), F32)], semantics=("arbitrary",),
                  block_bytes=24 * _nbytes((t, W_GRP), F32), aliases={3 + len(prm): 0})(
                      z, z, z, *[_arr(a) for a in prm], mix)


def _rglru_bwd(z, dmix, h0s, prm, dz, *, name):
    s = z.shape[0]
    t = _pick(s, (512, 256, 128))
    nt = s // t
    npm = len(prm)

    def body(xb_ref, halo_ref, gb_ref, dy_ref, h0s_ref, *rest):
        prm_refs = rest[:npm]
        dz_ref = rest[npm + 1]
        dprm_refs = rest[npm + 2:2 * npm + 2]
        dh_scr, dhalo_scr = rest[2 * npm + 2:]
        i = pl.program_id(0)
        r = nt - 1 - i

        @pl.when(i == 0)
        def _():
            dh_scr[...] = jnp.zeros_like(dh_scr)
            dhalo_scr[...] = jnp.zeros_like(dhalo_scr)

        halo = jnp.where(r == 0, 0.0, halo_ref[...])
        h0 = h0s_ref[0:1, :]
        _, vjp = jax.vjp(_rglru_tile, jnp.concatenate([halo, xb_ref[...]], axis=0), gb_ref[...], h0,
                         *[p[...] for p in prm_refs])
        dext, dgb, _dh0, *dps = vjp((dy_ref[...], dh_scr[...]))
        dmain = dext[8:]
        dxb = jnp.concatenate([dmain[:t - 8], dmain[t - 8:] + dhalo_scr[...]], axis=0)
        dz_ref[...] = jnp.concatenate([dxb, dgb], axis=1).astype(BF16)
        dh_scr[...] = _dh0
        dhalo_scr[...] = dext[:8]
        for ref, val in zip(dprm_refs, dps):
            _acc_out(ref, val, i == 0)

    rev = lambda c: pl.BlockSpec((t, W_GRP), lambda i: (nt - 1 - i, c))
    in_specs = [rev(2), pl.BlockSpec((8, W_GRP), lambda i: (jnp.maximum((nt - 1 - i) * (t // 8) - 1, 0), 2)), rev(3),
                rev(1), pl.BlockSpec((None, 8, W_GRP), lambda i: (nt - 1 - i, 0, 0))] + [_spec(a) for a in prm] + [HBM_SPEC]
    return _pcall(body, name=name,
                  out_shape=(_sds(dz.shape, BF16),) + tuple(_sds(a.shape, F32) for a in prm),
                  grid=(nt,), in_specs=in_specs,
                  out_specs=(pl.BlockSpec((t, 2 * W_GRP), lambda i: (nt - 1 - i, 1)),) + tuple(_ospec(a) for a in prm),
                  scratch_shapes=[pltpu.VMEM((1, W_GRP), F32), pltpu.VMEM((8, W_GRP), F32)],
                  semantics=("arbitrary",), block_bytes=40 * _nbytes((t, W_GRP), F32), aliases={5 + npm: 0})(
                      z, z, z, dmix, h0s, *[_arr(a) for a in prm], dz)


def _pool_inv(i, t):
    pos = (_rows_of((t, W_GRP)) + i * t + 1).astype(F32)
    grp = _lanes_of((t, W_GRP)) // HEAD_DIM
    win = jnp.where(grp == 0, float(POOL_WINDOWS[0]), jnp.where(grp == 1, float(POOL_WINDOWS[1]),
                    jnp.where(grp == 2, float(POOL_WINDOWS[2]), float(POOL_WINDOWS[3]))))
    return 1.0 / jnp.minimum(pos, win)


def _pool_fwd(z, wd, scale, mix, *, name):
    s = z.shape[0]
    t = _pick(s, (512, 256, 128))

    def body(x_ref, halo_ref, wd_ref, sc_ref, _, y_ref):
        i = pl.program_id(0)
        halo = jnp.where(i == 0, 0.0, halo_ref[...])
        y = _pool_tile(jnp.concatenate([halo, x_ref[...]], axis=0), _pool_inv(i, t), wd_ref[...], sc_ref[...])
        y_ref[...] = y.astype(BF16)

    in_specs = [pl.BlockSpec((t, W_GRP), lambda i: (i, 8)),
                pl.BlockSpec((16, W_GRP), lambda i: (jnp.maximum(i * (t // 16) - 1, 0), 8)), _spec(wd), _spec(scale),
                HBM_SPEC]
    return _pcall(body, name=name, out_shape=_sds(mix.shape, BF16), grid=(s // t,), in_specs=in_specs,
                  out_specs=pl.BlockSpec((t, W_GRP), lambda i: (i, 3)), semantics=("parallel",),
                  block_bytes=12 * _nbytes((t, W_GRP), F32), aliases={4: 0})(z, z, _arr(wd), _arr(scale), mix)


def _pool_bwd(z, dmix, wd, scale, dz, *, name):
    s = z.shape[0]
    t = _pick(s, (512, 256, 128))
    nt = s // t

    def body(x_ref, halo_ref, dy_ref, wd_ref, sc_ref, _, dx_ref, dwd_ref, dsc_ref, dhalo_scr):
        i = pl.program_id(0)
        r = nt - 1 - i

        @pl.when(i == 0)
        def _():
            dhalo_scr[...] = jnp.zeros_like(dhalo_scr)

        halo = jnp.where(r == 0, 0.0, halo_ref[...])
        inv = _pool_inv(r, t)
        _, vjp = jax.vjp(lambda e, w, sc: _pool_tile(e, inv, w, sc), jnp.concatenate([halo, x_ref[...]], axis=0),
                         wd_ref[...], sc_ref[...])
        dext, dwd, dsc = vjp(dy_ref[...])
        dmain = dext[16:]
        dx = jnp.concatenate([dmain[:t - 16], dmain[t - 16:] + dhalo_scr[...]], axis=0)
        dx_ref[...] = dx.astype(BF16)
        dhalo_scr[...] = dext[:16]
        _acc_out(dwd_ref, dwd, i == 0)
        _acc_out(dsc_ref, dsc, i == 0)

    rev = lambda c: pl.BlockSpec((t, W_GRP), lambda i: (nt - 1 - i, c))
    in_specs = [rev(8), pl.BlockSpec((16, W_GRP), lambda i: (jnp.maximum((nt - 1 - i) * (t // 16) - 1, 0), 8)), rev(3),
                _spec(wd), _spec(scale), HBM_SPEC]
    return _pcall(body, name=name, out_shape=(_sds(dz.shape, BF16), _sds(wd.shape, F32), _sds(scale.shape, F32)),
                  grid=(nt,), in_specs=in_specs, out_specs=(rev(8), _ospec(wd), _ospec(scale)),
                  scratch_shapes=[pltpu.VMEM((16, W_GRP), F32)], semantics=("arbitrary",),
                  block_bytes=20 * _nbytes((t, W_GRP), F32), aliases={5: 0})(z, z, dmix, _arr(wd), _arr(scale), dz)


def _hgrn_fwd(z, lb, ngf, mix, *, name):
    s = z.shape[0]
    c = HGRN_CHUNK
    per = HGRN_STEP_CHUNKS
    ns = s // (c * per)

    def body(q_ref, f_ref, i_ref, g_ref, lb_ref, ng_ref, _, y_ref, sts_ref, st_scr):
        @pl.when(pl.program_id(0) == 0)
        def _():
            st_scr[...] = jnp.zeros_like(st_scr)

        st = st_scr[...]
        for k in range(per):
            rows = pl.ds(k * c, c)
            sts_ref[k] = st
            y, st = _hgrn_chunk(q_ref[rows, :], f_ref[rows, :], i_ref[rows, :], g_ref[rows, :], st, lb_ref[...],
                                ng_ref[...])
            y_ref[rows, :] = y.astype(BF16)
        st_scr[...] = st

    col = lambda k: pl.BlockSpec((per * c, W_GRP), lambda i: (i, k))
    return _pcall(body, name=name, out_shape=(_sds(mix.shape, BF16), _sds((ns * per, W_GRP, W_GRP), F32)), grid=(ns,),
                  in_specs=[col(4), col(5), col(6), col(7), _spec(lb), _spec(ngf), HBM_SPEC],
                  out_specs=(pl.BlockSpec((per * c, W_GRP), lambda i: (i, 2)),
                             pl.BlockSpec((per, W_GRP, W_GRP), lambda i: (i, 0, 0))),
                  scratch_shapes=[pltpu.VMEM((W_GRP, W_GRP), F32)], semantics=("arbitrary",),
                  block_bytes=16 * per * _nbytes((W_GRP, W_GRP), F32), aliases={6: 0})(
                      z, z, z, z, _arr(lb), _arr(ngf), mix)


def _hgrn_bwd(z, dmix, sts, lb, ngf, dz, *, name):
    s = z.shape[0]
    c = HGRN_CHUNK
    per = HGRN_STEP_CHUNKS
    ns = s // (c * per)

    def body(q_ref, f_ref, i_ref, g_ref, dy_ref, st_ref, lb_ref, ng_ref, _, dz_ref, dlb_ref, dng_ref, dst_scr):
        i = pl.program_id(0)

        @pl.when(i == 0)
        def _():
            dst_scr[...] = jnp.zeros_like(dst_scr)

        dst = dst_scr[...]
        dlb_sum = dng_sum = None
        for k in range(per - 1, -1, -1):
            rows = pl.ds(k * c, c)
            _, vjp = jax.vjp(_hgrn_chunk, q_ref[rows, :], f_ref[rows, :], i_ref[rows, :], g_ref[rows, :], st_ref[k],
                             lb_ref[...], ng_ref[...])
            dq, df, di, dg, dst, dlb, dng = vjp((dy_ref[rows, :], dst))
            dz_ref[rows, :] = jnp.concatenate([dq, df, di, dg], axis=1).astype(BF16)
            dlb_sum = dlb if dlb_sum is None else dlb_sum + dlb
            dng_sum = dng if dng_sum is None else dng_sum + dng
        dst_scr[...] = dst
        _acc_out(dlb_ref, dlb_sum, i == 0)
        _acc_out(dng_ref, dng_sum, i == 0)

    rev = lambda k: pl.BlockSpec((per * c, W_GRP), lambda i: (ns - 1 - i, k))
    vec = pl.BlockSpec((1, W_GRP), lambda i: (0, 0))
    return _pcall(body, name=name, out_shape=(_sds(dz.shape, BF16), _sds((1, W_GRP), F32), _sds((1, W_GRP), F32)),
                  grid=(ns,),
                  in_specs=[rev(4), rev(5), rev(6), rev(7), rev(2),
                            pl.BlockSpec((per, W_GRP, W_GRP), lambda i: (ns - 1 - i, 0, 0)), _spec(lb), _spec(ngf),
                            HBM_SPEC],
                  out_specs=(pl.BlockSpec((per * c, 4 * W_GRP), lambda i: (ns - 1 - i, 1)), vec, vec),
                  scratch_shapes=[pltpu.VMEM((W_GRP, W_GRP), F32)], semantics=("arbitrary",),
                  block_bytes=32 * per * _nbytes((W_GRP, W_GRP), F32), aliases={8: 0})(
                      z, z, z, z, dmix, sts, _arr(lb), _arr(ngf), dz)


def _lbs_fwd(c_lb, *, name):
    def body(c_ref, o_ref):
        c = c_ref[...]
        e = jnp.exp(c - jnp.max(c, axis=0, keepdims=True))
        sm = e / jnp.sum(e, axis=0, keepdims=True)
        run = jnp.zeros((1, W_GRP), F32)
        o_ref[0:1, :] = run
        for l in range(1, DEPTH):
            run = run + sm[l:l + 1]
            o_ref[l:l + 1, :] = run

    return _pcall(body, name=name, out_shape=_sds((DEPTH, W_GRP), F32), pin=False)(c_lb)


def _lbs_bwd(c_lb, dlbs, *, name):
    def body(c_ref, d_ref, o_ref):
        c = c_ref[...]
        e = jnp.exp(c - jnp.max(c, axis=0, keepdims=True))
        sm = e / jnp.sum(e, axis=0, keepdims=True)
        d = d_ref[...]
        dsm = [None] * DEPTH
        run = jnp.zeros((1, W_GRP), F32)
        for l in range(DEPTH - 1, 0, -1):
            run = run + d[l:l + 1]
            dsm[l] = run
        dsm[0] = jnp.zeros((1, W_GRP), F32)
        inner = sum(sm[l:l + 1] * dsm[l] for l in range(DEPTH))
        for l in range(DEPTH):
            o_ref[l:l + 1, :] = sm[l:l + 1] * (dsm[l] - inner)

    return _pcall(body, name=name, out_shape=_sds((DEPTH, W_GRP), F32), pin=False)(c_lb, dlbs)


def _ffn_bwd(hg, hv, dx, w_down, cwf, cbf, *, name):
    s, n = hg.shape
    t = _pick(s, (256, 128))
    cw = _pick(n, (1408, 256, 128))
    nt = s // t
    nj = n // cw

    def body(g_ref, gh_ref, v_ref, vh_ref, dx_ref, wd_ref, wg_ref, bg_ref, wv_ref, bv_ref, dg_ref, dv_ref, dwg_ref,
             dwv_ref, cg_scr, cv_scr):
        i = pl.program_id(1)
        r = nt - 1 - i

        @pl.when(i == 0)
        def _():
            cg_scr[...] = jnp.zeros_like(cg_scr)
            cv_scr[...] = jnp.zeros_like(cv_scr)

        da = lax.dot_general(dx_ref[...], wd_ref[...], (((1,), (1,)), ((), ())), preferred_element_type=F32)
        eg = jnp.concatenate([jnp.where(r == 0, 0.0, gh_ref[...]), g_ref[...]], axis=0)
        ev = jnp.concatenate([jnp.where(r == 0, 0.0, vh_ref[...]), v_ref[...]], axis=0)
        _, vjp = jax.vjp(_ffn_tile, eg, ev, wg_ref[...], bg_ref[...], wv_ref[...], bv_ref[...])
        deg, dev, dwg, dbg, dwv, dbv = vjp(da)
        for dext, scr, ref in ((deg, cg_scr, dg_ref), (dev, cv_scr, dv_ref)):
            dmain = dext[8:]
            ref[...] = jnp.concatenate([dmain[:t - 8], dmain[t - 8:] + scr[...]], axis=0).astype(BF16)
            scr[...] = dext[:8]
        zeros = jnp.zeros((4, cw), F32)
        _acc_out(dwg_ref, jnp.concatenate([dwg, dbg, zeros], axis=0), i == 0)
        _acc_out(dwv_ref, jnp.concatenate([dwv, dbv, zeros], axis=0), i == 0)

    main = pl.BlockSpec((t, cw), lambda j, i: (nt - 1 - i, j))
    halo = pl.BlockSpec((8, cw), lambda j, i: (jnp.maximum((nt - 1 - i) * (t // 8) - 1, 0), j))
    taps = lambda off: _spec(cwf, (3, cw), lambda j, i: (0, j + off))
    bias = lambda off: _spec(cbf, (1, cw), lambda j, i: (0, j + off))
    w8 = pl.BlockSpec((8, cw), lambda j, i: (0, j))
    d = dx.shape[1]
    in_specs = [main, halo, main, halo, pl.BlockSpec((t, d), lambda j, i: (nt - 1 - i, 0)),
                _spec(w_down, (cw, d), lambda j, i: (j, 0)), taps(0), bias(0), taps(nj), bias(nj)]
    return _pcall(body, name=name,
                  out_shape=(_sds((s, n), BF16), _sds((s, n), BF16), _sds((8, n), F32), _sds((8, n), F32)),
                  grid=(nj, nt), in_specs=in_specs, out_specs=(main, main, w8, w8),
                  scratch_shapes=[pltpu.VMEM((8, cw), F32), pltpu.VMEM((8, cw), F32)],
                  semantics=("parallel", "arbitrary"),
                  block_bytes=24 * _nbytes((t, cw), F32) + _nbytes((cw, d), BF16))(
                      hg, hg, hv, hv, dx, _arr(w_down), _arr(cwf), _arr(cbf), _arr(cwf), _arr(cbf))


def _all_gather(x, *, name):
    r, c = x.shape

    def body(x_ref, out_ref, send_sems, recv_sems, local_sem):
        mx, my, mc = lax.axis_index("x"), lax.axis_index("y"), lax.axis_index("c")
        me, sibling = (mx, my, mc), (mx, my, 1 - mc)
        chips = [(1 - mx, my), (mx, 1 - my), (1 - mx, 1 - my)]

        def slot(px, py, pc):
            return out_ref.at[4 * px + 2 * py + pc]

        def copy(k, block, to, src=None):
            return pltpu.make_async_remote_copy(src_ref=slot(*block) if src is None else src, dst_ref=slot(*block),
                                                send_sem=send_sems.at[k], recv_sem=recv_sems.at[k],
                                                device_id=to, device_id_type=MESH)

        mine = pltpu.make_async_copy(x_ref, slot(*me), local_sem)
        mine.start()
        first = [copy(0, me, sibling, src=x_ref)]
        first += [copy(1 + j, me, (*chip, mc), src=x_ref) for j, chip in enumerate(chips)]
        for cp in first:
            cp.start()
        passed = [copy(4 + j, (*chip, mc), sibling) for j, chip in enumerate(chips)]
        for j, chip in enumerate(chips):
            copy(1 + j, (*chip, mc), me).wait_recv()
            passed[j].start()
        copy(0, sibling, me).wait_recv()
        for j, chip in enumerate(chips):
            copy(4 + j, (*chip, 1 - mc), me).wait_recv()
        for cp in first + passed:
            cp.wait_send()
        mine.wait()

    hbm = pl.BlockSpec(memory_space=pl.ANY)
    return _pcall(body, name=name, out_shape=_sds((N_DEV, r, c), x.dtype), in_specs=[hbm], out_specs=hbm,
                  scratch_shapes=[pltpu.SemaphoreType.DMA((7,)), pltpu.SemaphoreType.DMA((7,)),
                                  pltpu.SemaphoreType.DMA(())])(x)


def _sum_slots(p, *, name):
    q, r, c = p.shape
    tr = _pick(r, (544, 408, 272, 192, 136, 64, 32, 16, 8))

    def body(p_ref, o_ref):
        acc = p_ref[0].astype(F32)
        for k in range(1, q):
            acc = acc + p_ref[k].astype(F32)
        o_ref[...] = acc

    return _pcall(body, name=name, out_shape=_sds((r, c), F32), grid=(r // tr,),
                  in_specs=[pl.BlockSpec((q, tr, c), lambda i: (0, i, 0))],
                  out_specs=pl.BlockSpec((tr, c), lambda i: (i, 0)), semantics=("parallel",),
                  block_bytes=(q + 2) * _nbytes((tr, c), F32))(p)


BIG_COMM = (('w_in', 288, D_MODEL), ('w_out', 128, D_MODEL), ('w_up', 704, D_MODEL), ('w_down', 352, D_MODEL),
            ('w_pe', 128, PLE_DIM), ('w_pg', 128, D_MODEL))
HBM_SPEC = pl.BlockSpec(memory_space=pl.ANY)


def _gather_layer(shards, l, *, name):
    na = len(shards)

    def body(*refs):
        x_refs, out_refs = refs[:na], refs[na:2 * na]
        send_sems, recv_sems, local_sems = refs[2 * na:]
        mx, my, mc = lax.axis_index("x"), lax.axis_index("y"), lax.axis_index("c")
        me, sibling = (mx, my, mc), (mx, my, 1 - mc)
        chips = [(1 - mx, my), (mx, 1 - my), (1 - mx, 1 - my)]

        def slot(a, px, py, pc):
            return out_refs[a].at[4 * px + 2 * py + pc]

        def copy(k, a, block, to, own=False):
            return pltpu.make_async_remote_copy(src_ref=x_refs[a].at[l] if own else slot(a, *block),
                                                dst_ref=slot(a, *block), send_sem=send_sems.at[k, a],
                                                recv_sem=recv_sems.at[k, a], device_id=to, device_id_type=MESH)

        mine = [pltpu.make_async_copy(x_refs[a].at[l], slot(a, *me), local_sems.at[a]) for a in range(na)]
        for cp in mine:
            cp.start()
        first = []
        for a in range(na):
            first.append(copy(0, a, me, sibling, own=True))
            first += [copy(1 + j, a, me, (*chip, mc), own=True) for j, chip in enumerate(chips)]
        for cp in first:
            cp.start()
        passed = []
        for j, chip in enumerate(chips):
            for a in range(na):
                copy(1 + j, a, (*chip, mc), me).wait_recv()
                fwd = copy(4 + j, a, (*chip, mc), sibling)
                fwd.start()
                passed.append(fwd)
        for a in range(na):
            copy(0, a, sibling, me).wait_recv()
        for j, chip in enumerate(chips):
            for a in range(na):
                copy(4 + j, a, (*chip, 1 - mc), me).wait_recv()
        for cp in first + passed:
            cp.wait_send()
        for cp in mine:
            cp.wait()

    return _pcall(body, name=name, out_shape=tuple(_sds((N_DEV,) + x.shape[1:], x.dtype) for x in shards),
                  in_specs=[HBM_SPEC] * na, out_specs=(HBM_SPEC,) * na,
                  scratch_shapes=[pltpu.SemaphoreType.DMA((7, na)), pltpu.SemaphoreType.DMA((7, na)),
                                  pltpu.SemaphoreType.DMA((na,))])(*shards)


SEM_SPEC = pl.BlockSpec(memory_space=pltpu.SEMAPHORE)
DATAFLOW_EFFECT = pltpu.SideEffectType.DATAFLOW_SIDE_EFFECTING


def _place_own(srcs, after, *, name):
    na = len(srcs)

    def body(*refs):
        x_refs, land_refs, sems = refs[:na], refs[na + len(after):2 * na + len(after)], refs[-1]
        me = 4 * lax.axis_index("x") + 2 * lax.axis_index("y") + lax.axis_index("c")
        cps = [pltpu.make_async_copy(x_refs[a], land_refs[a].at[me], sems.at[a]) for a in range(na)]
        for cp in cps:
            cp.start()
        for cp in cps:
            cp.wait()

    return _pcall(body, name=name, out_shape=tuple(_sds((N_DEV,) + x.shape, x.dtype) for x in srcs),
                  in_specs=[HBM_SPEC] * (na + len(after)), out_specs=(HBM_SPEC,) * na,
                  scratch_shapes=[pltpu.SemaphoreType.DMA((na,))], pin=False)(*srcs, *after)


def _exchange_start(srcs, lands, *, name, per_peer=False):
    na = len(srcs)

    def body(*refs):
        x_refs, land_refs = refs[:na], refs[na:2 * na]
        send_sems, recv_sems = refs[2 * na], refs[2 * na + 1]
        token = refs[-1]
        mx, my, mc = lax.axis_index("x"), lax.axis_index("y"), lax.axis_index("c")
        me = 4 * mx + 2 * my + mc
        peers = [(mx, my, 1 - mc)]
        for px, py in ((1 - mx, my), (mx, 1 - my), (1 - mx, 1 - my)):
            peers += [(px, py, mc), (px, py, 1 - mc)]
        for a in range(na):
            for peer in peers:
                src = x_refs[a].at[4 * peer[0] + 2 * peer[1] + peer[2]] if per_peer else x_refs[a]
                pltpu.make_async_remote_copy(src_ref=src, dst_ref=land_refs[a].at[me], send_sem=send_sems.at[a],
                                             recv_sem=recv_sems.at[a], device_id=peer, device_id_type=MESH).start()
        token[...] = jnp.zeros_like(token)

    hbm = lambda x: pltpu.HBM(x.shape, x.dtype)
    out_shape = ((pltpu.SemaphoreType.DMA((na,)), pltpu.SemaphoreType.DMA((na,))) + tuple(hbm(x) for x in srcs)
                 + tuple(hbm(x) for x in lands) + (_sds((8, 128), F32),))
    params = pltpu.CompilerParams(has_side_effects=DATAFLOW_EFFECT)
    pin = lambda x: pltpu.with_memory_space_constraint(x, pltpu.HBM)
    return pl.pallas_call(body, name=name, out_shape=out_shape, in_specs=[HBM_SPEC] * (2 * na),
                          out_specs=(SEM_SPEC, SEM_SPEC) + (HBM_SPEC,) * (2 * na) + (pl.BlockSpec(memory_space=pltpu.VMEM),),
                          input_output_aliases={i: 2 + i for i in range(2 * na)}, compiler_params=params)(
                              *[pin(x) for x in srcs], *[pin(x) for x in lands])


def _exchange_wait(started, after, *, name):
    send_sems, recv_sems, *bufs, _ = started
    na = len(bufs) // 2

    def body(*refs):
        land_refs = refs[na:2 * na]
        s_sems, r_sems = refs[2 * na], refs[2 * na + 1]
        me = (lax.axis_index("x"), lax.axis_index("y"), lax.axis_index("c"))
        for a in range(na):
            seven = land_refs[a].at[pl.ds(0, N_DEV - 1)]
            cp = pltpu.make_async_remote_copy(src_ref=seven, dst_ref=seven, send_sem=s_sems.at[a], recv_sem=r_sems.at[a],
                                              device_id=me, device_id_type=MESH)
            cp.wait_send()
            cp.wait_recv()

    hbm = lambda x: pltpu.HBM(x.shape, x.dtype)
    params = pltpu.CompilerParams(has_side_effects=DATAFLOW_EFFECT)
    outs = pl.pallas_call(body, name=name, out_shape=tuple(hbm(x) for x in bufs),
                          in_specs=[HBM_SPEC] * (2 * na) + [SEM_SPEC, SEM_SPEC, HBM_SPEC],
                          out_specs=(HBM_SPEC,) * (2 * na), input_output_aliases={i: i for i in range(2 * na)},
                          compiler_params=params)(*bufs, send_sems, recv_sems, after)
    return outs[:na], outs[na:]


def _pair_swap(grads, *, name):
    na = len(grads)

    def body(*refs):
        g_refs, recv_refs = refs[:na], refs[na:2 * na]
        send_sems, recv_sems = refs[2 * na:]
        mx, my, mc = lax.axis_index("x"), lax.axis_index("y"), lax.axis_index("c")
        sibling = (mx, my, 1 - mc)
        for a in range(na):
            for q in range(4):
                pltpu.make_async_remote_copy(src_ref=g_refs[a].at[q, 1 - mc], dst_ref=recv_refs[a].at[q],
                                             send_sem=send_sems.at[a], recv_sem=recv_sems.at[a],
                                             device_id=sibling, device_id_type=MESH).start()
        for a in range(na):
            pltpu.make_async_remote_copy(src_ref=recv_refs[a], dst_ref=recv_refs[a], send_sem=send_sems.at[a],
                                         recv_sem=recv_sems.at[a], device_id=sibling, device_id_type=MESH).wait()

    half = tuple(_sds((4,) + g.shape[2:], g.dtype) for g in grads)
    return _pcall(body, name=name, out_shape=half, in_specs=[HBM_SPEC] * na, out_specs=(HBM_SPEC,) * na,
                  scratch_shapes=[pltpu.SemaphoreType.DMA((na,)), pltpu.SemaphoreType.DMA((na,))])(*grads)


def _add_slabs(grads, recv, core, *, name):
    na = len(grads)

    def body(core_ref, *refs):
        for a in range(na):
            refs[2 * na + a][...] = (refs[a][...].astype(F32) + refs[na + a][...].astype(F32)).astype(BF16)

    own_specs = [pl.BlockSpec((None, None) + x.shape[2:], lambda q, core_ref: (q, core_ref[0], 0, 0)) for x in grads]
    specs = [pl.BlockSpec((None,) + x.shape[1:], lambda q, core_ref: (q, 0, 0)) for x in recv]
    blk = sum(_nbytes(x.shape[1:], F32) for x in recv)
    grid_spec = pltpu.PrefetchScalarGridSpec(num_scalar_prefetch=1, grid=(4,), in_specs=own_specs + specs,
                                             out_specs=tuple(specs))
    params = pltpu.CompilerParams(dimension_semantics=("parallel",), vmem_limit_bytes=_vmem_limit(2 * blk))
    return pl.pallas_call(body, name=name, out_shape=tuple(_sds(x.shape, BF16) for x in recv), grid_spec=grid_spec,
                          compiler_params=params)(core, *grads, *recv)


def _chip_exchange(parts, *, name):
    na = len(parts)

    def body(*refs):
        p_refs, out_refs = refs[:na], refs[na:2 * na]
        send_sems, recv_sems, local_sems = refs[2 * na:]
        mx, my, mc = lax.axis_index("x"), lax.axis_index("y"), lax.axis_index("c")
        mine_q = 2 * mx + my
        chips = [(1 - mx, my), (mx, 1 - my), (1 - mx, 1 - my)]
        owns = [pltpu.make_async_copy(p_refs[a].at[mine_q], out_refs[a].at[mine_q], local_sems.at[a]) for a in range(na)]
        for cp in owns:
            cp.start()
        sends = []
        for a in range(na):
            for k, chip in enumerate(chips):
                sends.append(pltpu.make_async_remote_copy(
                    src_ref=p_refs[a].at[2 * chip[0] + chip[1]], dst_ref=out_refs[a].at[mine_q],
                    send_sem=send_sems.at[k, a], recv_sem=recv_sems.at[k, a], device_id=(*chip, mc), device_id_type=MESH))
        for cp in sends:
            cp.start()
        for a in range(na):
            for k, chip in enumerate(chips):
                pltpu.make_async_remote_copy(
                    src_ref=p_refs[a].at[mine_q], dst_ref=out_refs[a].at[2 * chip[0] + chip[1]],
                    send_sem=send_sems.at[k, a], recv_sem=recv_sems.at[k, a], device_id=(*chip, mc),
                    device_id_type=MESH).wait_recv()
        for cp in sends:
            cp.wait_send()
        for cp in owns:
            cp.wait()

    return _pcall(body, name=name, out_shape=tuple(_sds(x.shape, x.dtype) for x in parts), in_specs=[HBM_SPEC] * na,
                  out_specs=(HBM_SPEC,) * na,
                  scratch_shapes=[pltpu.SemaphoreType.DMA((3, na)), pltpu.SemaphoreType.DMA((3, na)),
                                  pltpu.SemaphoreType.DMA((na,))])(*parts)


def _sum_chips(parts, *, name):
    na = len(parts)

    def body(*refs):
        for a in range(na):
            p_ref = refs[a]
            acc = p_ref[0].astype(F32)
            for k in range(1, p_ref.shape[0]):
                acc = acc + p_ref[k].astype(F32)
            refs[na + a][...] = acc

    half = lambda x: x.shape[1] // 2
    in_specs = [pl.BlockSpec((x.shape[0], half(x), x.shape[2]), lambda i: (0, i, 0)) for x in parts]
    out_specs = tuple(pl.BlockSpec((half(x), x.shape[2]), lambda i: (i, 0)) for x in parts)
    blk = sum(_nbytes((x.shape[0] + 2, half(x), x.shape[2]), BF16) for x in parts)
    return _pcall(body, name=name, out_shape=tuple(_sds(x.shape[1:], F32) for x in parts), grid=(2,),
                  in_specs=in_specs, out_specs=out_specs, semantics=("parallel",), block_bytes=blk)(*parts)


def _sum_devices(lands, own, me, *, name):
    na = len(lands)

    def body(me_ref, *refs):
        mine = me_ref[0]
        for a in range(na):
            l_ref, o_ref = refs[a], refs[na + a]
            acc = None
            for k in range(N_DEV):
                term = jnp.where(mine == k, o_ref[...], l_ref[k]).astype(F32)
                acc = term if acc is None else acc + term
            refs[2 * na + a][...] = acc

    half = lambda x: x.shape[1] // 2
    land_specs = [pl.BlockSpec((N_DEV, half(x), x.shape[2]), lambda i, me_ref: (0, i, 0)) for x in lands]
    own_specs = [pl.BlockSpec((None, half(x), x.shape[2]), lambda i, me_ref: (me_ref[0], i, 0)) for x in lands]
    out_specs = tuple(pl.BlockSpec((half(x), x.shape[2]), lambda i, me_ref: (i, 0)) for x in lands)
    blk = sum(_nbytes((N_DEV + 3, half(x), x.shape[2]), BF16) for x in lands)
    grid_spec = pltpu.PrefetchScalarGridSpec(num_scalar_prefetch=1, grid=(2,), in_specs=land_specs + own_specs,
                                             out_specs=out_specs)
    params = pltpu.CompilerParams(dimension_semantics=("parallel",), vmem_limit_bytes=_vmem_limit(blk))
    return pl.pallas_call(body, name=name, out_shape=tuple(_sds(x.shape[1:], F32) for x in lands), grid_spec=grid_spec,
                          compiler_params=params)(me, *lands, *own)


def _reduce_layer(grads, l):
    n = lambda s: f"l{l}_{s}"
    views = [g.reshape(4, 2, g.shape[0] // N_DEV, g.shape[1]) for g in grads]
    recv = _pair_swap(views, name=n("reduce_pair"))
    core = lax.axis_index("c").astype(jnp.int32).reshape(1)
    chip_sum = _add_slabs(views, recv, core, name=n("reduce_pair_add"))
    from_chips = _chip_exchange(chip_sum, name=n("reduce_chips"))
    return _sum_chips(from_chips, name=n("reduce_chips_add"))


def _adamw(w, g, m, v, *, name):
    lead, (r, c) = w.shape[:-2], w.shape[-2:]
    tr = _pick(r, (512, 352, 288, 256, 192, 128, 64, 32, 16, 8))
    c1 = 1.0 / (1.0 - ADAM_B1 ** ADAM_STEP)
    c2 = 1.0 / (1.0 - ADAM_B2 ** ADAM_STEP)

    def body(w_ref, g_ref, m_ref, v_ref, d_ref, nm_ref, nv_ref):
        gv = g_ref[...]
        nm = ADAM_B1 * m_ref[...] + (1.0 - ADAM_B1) * gv
        nv = ADAM_B2 * v_ref[...] + (1.0 - ADAM_B2) * jnp.square(gv)
        d_ref[...] = -ADAM_LR * ((nm * c1) / (jnp.sqrt(nv * c2) + ADAM_EPS) + ADAM_WD * w_ref[...])
        nm_ref[...] = nm
        nv_ref[...] = nv

    if lead:
        blk = pl.BlockSpec((None, tr, c), lambda k, i: (k, i, 0))
        grid, sem = (lead[0], r // tr), ("parallel", "parallel")
    else:
        blk = pl.BlockSpec((tr, c), lambda i: (i, 0))
        grid, sem = (r // tr,), ("parallel",)
    out = _sds(w.shape, F32)
    return _pcall(body, name=name, out_shape=(out, out, out), grid=grid, in_specs=[blk] * 4,
                  out_specs=(blk, blk, blk), semantics=sem, block_bytes=7 * _nbytes((tr, c), F32))(w, g, m, v)


def _pack_flat(arrs, rows, cols=1024):
    flat = jnp.concatenate([a.reshape(-1).astype(F32) for a in arrs])
    pad = rows * cols - flat.shape[0]
    return jnp.pad(flat, (0, pad)).reshape(rows, cols)


def _unpack_flat(buf, shapes):
    flat = buf.reshape(-1)
    out, off = [], 0
    for shp in shapes:
        n = 1
        for s in shp:
            n *= s
        out.append(flat[off:off + n].reshape(shp))
        off += n
    return out


def _flat_rows(shapes, cols=1024):
    n = sum(functools.reduce(lambda a, b: a * b, shp, 1) for shp in shapes)
    rows = -(-n // cols)
    return -(-rows // 64) * 64


def _block_diag(w):
    eye = jnp.eye(N_HEADS, dtype=w.dtype)
    return (w[:, :, :, None, :] * eye[None, :, None, :, None]).reshape(w.shape[0], W_GRP, W_GRP)


def _diag_blocks(w):
    w5 = w.reshape(w.shape[0], N_HEADS, HEAD_DIM, N_HEADS, HEAD_DIM)
    return jnp.stack([w5[:, h, :, h, :] for h in range(N_HEADS)], axis=1)


def _stacked_params(w, lbs):
    tril = jnp.tril(jnp.ones((GMLP_CHUNK, GMLP_CHUNK), bool))
    row = lambda a: a.reshape(DEPTH, 1, -1)
    return dict(
        g1=row(w['norm1_g']), g2=row(w['norm2_g']), g3=row(w['norm3_g']),
        a_ln_g=row(w['a_ln_g']), a_ln_b=row(w['a_ln_b']),
        a_wcat=jnp.where(tril, w['a_ws'], 0.0).reshape(DEPTH, N_HEADS * GMLP_CHUNK, GMLP_CHUNK),
        a_bfull=jnp.repeat(jnp.swapaxes(w['a_bs'], 1, 2), HEAD_DIM, axis=2),
        b_cw=w['b_conv_w_full'], b_cb=row(w['b_conv_b']), b_wa=_block_diag(w['b_wa']), b_ba=row(w['b_ba']),
        b_wx=_block_diag(w['b_wx']), b_bx=row(w['b_bx']), b_lam=row(w['b_lam']),
        c_lb=row(lbs), c_ngf=row(jnp.tile(w['c_norm_g'], (1, N_HEADS))),
        d_wd=_block_diag(w['d_w']), d_scale=row(w['d_scale']),
        f_cw=w['ffn_conv_w_full'], f_cb=row(w['ffn_conv_b']),
    )


B_PRM = ('b_cw', 'b_cb', 'b_wa', 'b_ba', 'b_wx', 'b_bx', 'b_lam')


def _layer_fwd(x, p_bf, wb, sp, l):
    n = lambda s: f"l{l}_{s}"
    h, (z,) = _rms_matmul(x, sp['g1'], [wb['w_in']], nt=True, name=n("proj_in"))
    mix = _gmlp_fwd(z, sp['a_ln_g'], sp['a_ln_b'], sp['a_wcat'], sp['a_bfull'], name=n("gmlp"))
    mix, h0s = _rglru_fwd(z, [sp[k] for k in B_PRM], mix, name=n("rglru"))
    mix, sts = _hgrn_fwd(z, sp['c_lb'], sp['c_ngf'], mix, name=n("hgrn"))
    mix = _pool_fwd(z, sp['d_wd'], sp['d_scale'], mix, name=n("pool"))
    x1 = _matmul(mix, wb['w_out'], res=x, name=n("proj_out"))
    h2, hg, hv, a = _up_ffn_fwd(x1, sp['g2'], wb['w_up_g'], wb['w_up_v'], sp['f_cw'], sp['f_cb'], name=n("up_ffn"))
    x2 = _matmul(a, wb['w_down'], res=x1, name=n("down"))
    h3, (gl, pe, x3) = _rms_matmul(x2, sp['g3'], [wb['w_pg']], ple=(p_bf, wb['w_pe']), name=n("ple"))
    saved = dict(x=x, h=h, z=z, h0s=h0s, sts=sts, mix=mix, x1=x1, h2=h2, hg=hg, hv=hv, a=a, x2=x2, h3=h3, gl=gl, pe=pe)
    return x3, saved


def _layer_bwd(dx3, sv, p_bf, wb, sp, l, mid=None):
    n = lambda s: f"l{l}_{s}_bwd"
    gb, gs = {}, {}
    dpe, dgl = _ple_bwd(dx3, sv['gl'], sv['pe'], name=n("ple"))
    gb['w_pe'] = _matmul_tn(dpe, p_bf, name=n("ple_emb_w"))
    gb['w_pg'] = _matmul_tn(sv['h3'], dgl, name=n("ple_gate_w"))
    dx2, dx2b, gs['norm3_g'] = _matmul_rms_bwd(dgl, wb['w_pg'], sv['x2'], sp['g3'], dx3, nt=True, name=n("ple_gate_x"))
    gb['w_down'] = _matmul_tn(sv['a'], dx2b, name=n("down_w"))
    dhg, dhv, gs['f_dwg'], gs['f_dwv'] = _ffn_bwd(sv['hg'], sv['hv'], dx2b, wb['w_down'], sp['f_cw'], sp['f_cb'],
                                                  name=n("ffn_gate"))
    gate_rows = _matmul_tn(dhg, sv['h2'], name=n("up_gate_w"), out_rows=2 * D_FF)
    gb['w_up'] = _matmul_tn(dhv, sv['h2'], name=n("up_val_w"), out_rows=2 * D_FF, row_off=D_FF, into=gate_rows)
    if mid is not None:
        sp = mid(gb, sp)
    dh2 = _matmul(dhg, wb['w_up_g'], name=n("up_gate_x"))
    dx1, dx1b, gs['norm2_g'] = _matmul_rms_bwd(dhv, wb['w_up_v'], sv['x1'], sp['g2'], dx2, res=dh2, name=n("up_val_x"))
    dmix = _matmul(dx1b, wb['w_out'], nt=True, name=n("proj_out_x"))
    gb['w_out'] = _matmul_tn(sv['mix'], dx1b, name=n("proj_out_w"))
    z = sv['z']
    dz, gs['a_ln_g'], gs['a_ln_b'], gs['a_wcat'], gs['a_bfull'] = _gmlp_bwd(
        z, dmix, sp['a_ln_g'], sp['a_ln_b'], sp['a_wcat'], sp['a_bfull'], name=n("gmlp"))
    dz, *dbp = _rglru_bwd(z, dmix, sv['h0s'], [sp[k] for k in B_PRM], dz, name=n("rglru"))
    gs.update(zip(B_PRM, dbp))
    dz, gs['c_lb'], gs['c_ngf'] = _hgrn_bwd(z, dmix, sv['sts'], sp['c_lb'], sp['c_ngf'], dz, name=n("hgrn"))
    dz, gs['d_wd'], gs['d_scale'] = _pool_bwd(z, dmix, sp['d_wd'], sp['d_scale'], dz, name=n("pool"))
    gb['w_in'] = _matmul_tn(dz, sv['h'], name=n("proj_in_w"))
    dx0, _, gs['norm1_g'] = _matmul_rms_bwd(dz, wb['w_in'], sv['x'], sp['g1'], dx1, name=n("proj_in_x"))
    return dx0, gb, gs


SMALL_NAMES = [nm for nm in WEIGHT_NAMES if nm not in BIG_NAMES]
COL_SHARDED = ('w_in', 'w_up', 'w_pe')


def _comm_shards(w):
    return [(jnp.swapaxes(w[nm], 1, 2) if nm in COL_SHARDED else w[nm]).astype(BF16) for nm, _, _ in BIG_COMM]


def _full_weights(gathered):
    out = {nm: g.reshape(N_DEV * r, c) for g, (nm, r, c) in zip(gathered, BIG_COMM)}
    halves = out.pop('w_up').reshape(2, D_FF, D_MODEL)
    out['w_up_g'], out['w_up_v'] = _Sel(halves, 0), _Sel(halves, 1)
    return out


def _small_grads(raw):
    nl = len(raw)
    st = {k: jnp.stack([r[k] for r in raw]) for k in raw[0]}
    tril = jnp.tril(jnp.ones((GMLP_CHUNK, GMLP_CHUNK), bool))
    vec = lambda a: a.reshape(nl, -1)
    out = {nm: vec(st[k]) for nm, k in (('norm1_g', 'norm1_g'), ('norm2_g', 'norm2_g'), ('norm3_g', 'norm3_g'),
                                        ('a_ln_g', 'a_ln_g'), ('a_ln_b', 'a_ln_b'), ('b_conv_b', 'b_cb'),
                                        ('b_ba', 'b_ba'), ('b_bx', 'b_bx'), ('b_lam', 'b_lam'), ('c_lb', 'c_lb'),
                                        ('d_scale', 'd_scale'))}
    out['a_ws'] = jnp.where(tril, st['a_wcat'].reshape(nl, N_HEADS, GMLP_CHUNK, GMLP_CHUNK), 0.0)
    out['a_bs'] = jnp.swapaxes(st['a_bfull'].reshape(nl, GMLP_CHUNK, N_HEADS, HEAD_DIM).sum(-1), 1, 2)
    out['b_conv_w'] = st['b_cw']
    out['b_wa'], out['b_wx'], out['d_w'] = _diag_blocks(st['b_wa']), _diag_blocks(st['b_wx']), _diag_blocks(st['d_wd'])
    out['c_norm_g'] = st['c_ngf'].reshape(nl, N_HEADS, HEAD_DIM).sum(1)
    out['ffn_conv_w'] = jnp.concatenate([st['f_dwg'][:, 0:3], st['f_dwv'][:, 0:3]], axis=2)
    out['ffn_conv_b'] = jnp.concatenate([st['f_dwg'][:, 3], st['f_dwv'][:, 3]], axis=1)
    return out


def _step(w, m, v, x, p, target):
    s = x.shape[1]
    dev = 4 * lax.axis_index("x") + 2 * lax.axis_index("y") + lax.axis_index("c")
    xs = x.reshape(s, D_MODEL)

    shards = _comm_shards(w)
    conv_shapes = [w['b_conv_w'].shape, w['ffn_conv_w'].shape]
    conv_rows = _flat_rows(conv_shapes)
    conv_all = _all_gather(_pack_flat([w['b_conv_w'], w['ffn_conv_w']], conv_rows), name="gather_conv_weights")
    parts = [_unpack_flat(conv_all[d], conv_shapes) for d in range(N_DEV)]
    wf = dict(w)
    wf['b_conv_w_full'] = jnp.concatenate([pt[0] for pt in parts], axis=-1)
    wf['ffn_conv_w_full'] = jnp.concatenate([pt[1] for pt in parts], axis=-1)
    lbs = _lbs_fwd(w['c_lb'], name="hgrn_bounds")

    stacked = _stacked_params(wf, lbs)
    p_all = p.reshape(DEPTH, s, PLE_DIM).astype(BF16)
    xl, saved, wbs, sps = xs, [], [], []
    gathered = _gather_layer(shards, 0, name="l0_gather_weights")
    for l in range(DEPTH):
        sp = {k: _Sel(a, l) for k, a in stacked.items()}
        if l + 1 < DEPTH:
            own = [x[l + 1] for x in shards]
            after = [conv_all, *gathered] if l == 0 else [xl]
            lands = _place_own(own, after, name=f"l{l + 1}_gather_place")
            started = _exchange_start(own, lands, name=f"l{l + 1}_gather_start")
            sp['g1'] = stacked['g1'][l] + started[-1][0, 0]
        wb = _full_weights(gathered)
        p_bf = p_all[l]
        xl, sv = _layer_fwd(xl, p_bf, wb, sp, l)
        if l + 1 < DEPTH:
            gathered = _exchange_wait(started, xl, name=f"l{l + 1}_gather_wait")[1]
        saved.append((sv, p_bf))
        wbs.append(wb)
        sps.append(sp)
    loss_part, dx, dfinal = _loss_head(xl, w['final_g'].reshape(1, D_MODEL), target.reshape(s, D_MODEL), name="loss_head")
    loss = lax.psum(loss_part[0, 0], ("x", "y", "c"))

    dev1 = dev.astype(jnp.int32).reshape(1)
    names = [nm for nm, _, _ in BIG_COMM]

    def start_reduce(grads, name):
        views = [g.reshape(N_DEV, g.shape[0] // N_DEV, g.shape[1]) for g in grads]
        return _exchange_start(views, [lax.empty(g.shape, g.dtype) for g in views], name=name, per_peer=True)

    def finish_reduce(started, after, lname):
        own, lands = _exchange_wait(started, after, name=f"{lname}_reduce_wait")
        return _sum_devices(lands, own, dev1, name=f"{lname}_reduce_sum")

    reduced, small = [None] * DEPTH, [None] * DEPTH
    pending = None
    for l in range(DEPTH - 1, 0, -1):
        sv, p_bf = saved[l]
        sp = sps[l]
        if pending is not None:
            sp = dict(sp, g3=stacked['g3'][l] + pending[-1][0, 0])
        dx, gb, small[l] = _layer_bwd(dx, sv, p_bf, wbs[l], sp, l)
        if pending is not None:
            reduced[l + 1] = finish_reduce(pending, dx, f"l{l + 1}")
        pending = start_reduce([gb[nm] for nm in names], f"l{l}_reduce_start")
    early = ('w_up', 'w_down', 'w_pe', 'w_pg')
    mid_started = []

    def mid(gb, sp):
        mid_started.append(start_reduce([gb[nm] for nm in early], "l0_reduce_start"))
        return dict(sp, g2=stacked['g2'][0] + mid_started[0][-1][0, 0])

    upper_names = [nm for nm in SMALL_NAMES if nm != 'final_g']
    low_names = upper_names + ['final_g']
    upper = _small_grads(small[1:])
    upper_shapes = [upper[nm].shape for nm in upper_names]
    upper_packed = [_pack_flat([upper[nm] for nm in upper_names], _flat_rows(upper_shapes))]
    upper_started = _exchange_start(upper_packed, _place_own(upper_packed, [], name="upper_small_grads_place"),
                                    name="upper_small_grads_start")

    sv, p_bf = saved[0]
    g3 = stacked['g3'][0] + pending[-1][0, 0] + upper_started[-1][0, 0]
    dx, gb, small[0] = _layer_bwd(dx, sv, p_bf, wbs[0], dict(sps[0], g3=g3), 0, mid=mid)
    reduced[1] = finish_reduce(pending, dx, "l1")
    late = dict(zip(('w_in', 'w_out'), _reduce_layer([gb['w_in'], gb['w_out']], 0)))
    late.update(zip(early, finish_reduce(mid_started[0], late['w_in'], "l0")))
    reduced[0] = [late[nm] for nm in names]
    grad_x = dx.reshape(1, s, D_MODEL)
    low = _small_grads(small[:1])
    low['final_g'] = dfinal.reshape(D_MODEL)
    low_shapes = [low[nm].shape for nm in low_names]
    low_all = _all_gather(_pack_flat([low[nm] for nm in low_names], _flat_rows(low_shapes)), name="gather_small_grads")
    low_sum = dict(zip(low_names, _unpack_flat(_sum_slots(low_all, name="sum_small_grads"), low_shapes)))
    upper_all = _exchange_wait(upper_started, low_all, name="upper_small_grads_wait")[1][0]
    upper_sum = dict(zip(upper_names, _unpack_flat(_sum_slots(upper_all, name="sum_upper_small_grads"), upper_shapes)))
    gsmall = {nm: jnp.concatenate([low_sum[nm], upper_sum[nm]], axis=0) for nm in upper_names}
    gsmall['c_lb'] = _lbs_bwd(w['c_lb'], gsmall['c_lb'], name="hgrn_bounds_bwd")
    gsmall['final_g'] = low_sum['final_g']
    for nm in ('b_conv_w', 'ffn_conv_w'):
        width = w[nm].shape[-1]
        gsmall[nm] = lax.dynamic_slice_in_dim(gsmall[nm], dev * width, width, axis=2)

    grads, delta, new_m, new_v = {}, {}, {}, {}
    for a, (nm, _, _) in enumerate(BIG_COMM):
        t = (lambda x: jnp.swapaxes(x, 1, 2)) if nm in COL_SHARDED else (lambda x: x)
        g = jnp.stack([reduced[l][a] for l in range(DEPTH)])
        d, nm_, nv_ = _adamw(t(w[nm]), g, t(m[nm]), t(v[nm]), name=f"adamw_{nm}")
        grads[nm], delta[nm], new_m[nm], new_v[nm] = t(g), t(d), t(nm_), t(nv_)

    shapes = [w[nm].shape for nm in SMALL_NAMES]
    rows = _flat_rows(shapes)
    pk = lambda t: _pack_flat([t[nm] for nm in SMALL_NAMES], rows)
    d, nm_, nv_ = _adamw(pk(w), pk(gsmall), pk(m), pk(v), name="adamw_small")
    for nm, dd, mm_, vv_ in zip(SMALL_NAMES, _unpack_flat(d, shapes), _unpack_flat(nm_, shapes), _unpack_flat(nv_, shapes)):
        grads[nm], delta[nm], new_m[nm], new_v[nm] = gsmall[nm], dd, mm_, vv_

    return (loss, grad_x, *[grads[nm] for nm in WEIGHT_NAMES], *[delta[nm] for nm in WEIGHT_NAMES],
            *[new_m[nm] for nm in WEIGHT_NAMES], *[new_v[nm] for nm in WEIGHT_NAMES])


def kernel(x, p, norm1_g, w_in, a_ln_g, a_ln_b, a_ws, a_bs, b_conv_w, b_conv_b, b_wa, b_ba, b_wx, b_bx, b_lam, c_lb, c_norm_g, d_w, d_scale, w_out, norm2_g, w_up, ffn_conv_w, ffn_conv_b, w_down, norm3_g, w_pe, w_pg, final_g, loss_target, m_norm1_g, m_w_in, m_a_ln_g, m_a_ln_b, m_a_ws, m_a_bs, m_b_conv_w, m_b_conv_b, m_b_wa, m_b_ba, m_b_wx, m_b_bx, m_b_lam, m_c_lb, m_c_norm_g, m_d_w, m_d_scale, m_w_out, m_norm2_g, m_w_up, m_ffn_conv_w, m_ffn_conv_b, m_w_down, m_norm3_g, m_w_pe, m_w_pg, m_final_g, v_norm1_g, v_w_in, v_a_ln_g, v_a_ln_b, v_a_ws, v_a_bs, v_b_conv_w, v_b_conv_b, v_b_wa, v_b_ba, v_b_wx, v_b_bx, v_b_lam, v_c_lb, v_c_norm_g, v_d_w, v_d_scale, v_w_out, v_norm2_g, v_w_up, v_ffn_conv_w, v_ffn_conv_b, v_w_down, v_norm3_g, v_w_pe, v_w_pg, v_final_g):
    w = dict(norm1_g=norm1_g, w_in=w_in, a_ln_g=a_ln_g, a_ln_b=a_ln_b, a_ws=a_ws, a_bs=a_bs, b_conv_w=b_conv_w, b_conv_b=b_conv_b, b_wa=b_wa, b_ba=b_ba, b_wx=b_wx, b_bx=b_bx, b_lam=b_lam, c_lb=c_lb, c_norm_g=c_norm_g, d_w=d_w, d_scale=d_scale, w_out=w_out, norm2_g=norm2_g, w_up=w_up, ffn_conv_w=ffn_conv_w, ffn_conv_b=ffn_conv_b, w_down=w_down, norm3_g=norm3_g, w_pe=w_pe, w_pg=w_pg, final_g=final_g)
    m = dict(norm1_g=m_norm1_g, w_in=m_w_in, a_ln_g=m_a_ln_g, a_ln_b=m_a_ln_b, a_ws=m_a_ws, a_bs=m_a_bs, b_conv_w=m_b_conv_w, b_conv_b=m_b_conv_b, b_wa=m_b_wa, b_ba=m_b_ba, b_wx=m_b_wx, b_bx=m_b_bx, b_lam=m_b_lam, c_lb=m_c_lb, c_norm_g=m_c_norm_g, d_w=m_d_w, d_scale=m_d_scale, w_out=m_w_out, norm2_g=m_norm2_g, w_up=m_w_up, ffn_conv_w=m_ffn_conv_w, ffn_conv_b=m_ffn_conv_b, w_down=m_w_down, norm3_g=m_norm3_g, w_pe=m_w_pe, w_pg=m_w_pg, final_g=m_final_g)
    v = dict(norm1_g=v_norm1_g, w_in=v_w_in, a_ln_g=v_a_ln_g, a_ln_b=v_a_ln_b, a_ws=v_a_ws, a_bs=v_a_bs, b_conv_w=v_b_conv_w, b_conv_b=v_b_conv_b, b_wa=v_b_wa, b_ba=v_b_ba, b_wx=v_b_wx, b_bx=v_b_bx, b_lam=v_b_lam, c_lb=v_c_lb, c_norm_g=v_c_norm_g, d_w=v_d_w, d_scale=v_d_scale, w_out=v_w_out, norm2_g=v_norm2_g, w_up=v_w_up, ffn_conv_w=v_ffn_conv_w, ffn_conv_b=v_ffn_conv_b, w_down=v_w_down, norm3_g=v_norm3_g, w_pe=v_w_pe, w_pg=v_w_pg, final_g=v_final_g)
    return _step(w, m, v, x, p, loss_target)
```

```python
import functools

import jax
import jax.numpy as jnp
from jax import lax
from jax.experimental import pallas as pl
from jax.experimental.pallas import tpu as pltpu

F32 = jnp.float32
BF16 = jnp.bfloat16
MESH = pl.DeviceIdType.MESH

D_MODEL = 1024
DEPTH = 4
PLE_DIM = 256
W_GRP = 256
N_HEADS = 4
HEAD_DIM = 64
GMLP_CHUNK = 128
RGLRU_C = 8.0
HGRN_CHUNK = 64
HGRN_SUB = 16
HGRN_STEP_CHUNKS = 8
POOL_WINDOWS = (2, 4, 8, 16)
D_FF = 2816
D_PROJ = 2304
EPS = 1e-6
ADAM_LR = 0.001
ADAM_B1 = 0.9
ADAM_B2 = 0.999
ADAM_EPS = 1e-08
ADAM_WD = 0.01
ADAM_STEP = 10

N_DEV = 8
MIB = 2 ** 20
V7X_VMEM_BYTES = 64 * MIB
HGRN_EXP_CLAMP = 60.0

WEIGHT_NAMES = ['norm1_g', 'w_in', 'a_ln_g', 'a_ln_b', 'a_ws', 'a_bs', 'b_conv_w', 'b_conv_b', 'b_wa', 'b_ba', 'b_wx',
                'b_bx', 'b_lam', 'c_lb', 'c_norm_g', 'd_w', 'd_scale', 'w_out', 'norm2_g', 'w_up', 'ffn_conv_w',
                'ffn_conv_b', 'w_down', 'norm3_g', 'w_pe', 'w_pg', 'final_g']
BIG_NAMES = ('w_in', 'w_out', 'w_up', 'w_down', 'w_pe', 'w_pg')


def _vmem_limit(block_bytes):
    want = 2 * block_bytes + 24 * MIB
    return int(min(max(want, 32 * MIB), V7X_VMEM_BYTES - 8 * MIB))


def _in_hbm(x):
    return pltpu.with_memory_space_constraint(x, pltpu.HBM)


def _out_hbm(s):
    return pltpu.HBM(s.shape, s.dtype)


def _pcall(body, *, name, out_shape, grid=None, in_specs=None, out_specs=None, scratch_shapes=(),
           semantics=None, block_bytes=0, aliases=None, pin=True):
    kw = {} if aliases is None else {"input_output_aliases": aliases}
    if pin:
        out_shape = tuple(_out_hbm(s) for s in out_shape) if isinstance(out_shape, (tuple, list)) else _out_hbm(out_shape)
    if grid is not None:
        kw["grid"] = grid
    if in_specs is not None:
        kw["in_specs"] = in_specs
    if out_specs is not None:
        kw["out_specs"] = out_specs
    params = pltpu.CompilerParams(dimension_semantics=semantics, vmem_limit_bytes=_vmem_limit(block_bytes))
    call = pl.pallas_call(body, name=name, out_shape=out_shape, scratch_shapes=list(scratch_shapes),
                          compiler_params=params, **kw)
    return (lambda *args: call(*[_in_hbm(a) for a in args])) if pin else call


def _pick(n, cands):
    for c in cands:
        if n % c == 0:
            return c
    return n


def _nbytes(shape, dtype):
    n = 1
    for s in shape:
        n *= s
    return n * jnp.dtype(dtype).itemsize


def _sds(shape, dtype):
    return jax.ShapeDtypeStruct(tuple(shape), dtype)


class _Sel:
    def __init__(self, arr, *idx):
        self.arr, self.idx = arr, tuple(idx)
        self.shape = arr.shape[len(idx):]
        self.ndim = len(self.shape)
        self.dtype = arr.dtype


def _arr(a):
    return a.arr if isinstance(a, _Sel) else a


def _spec(a, block=None, index=None):
    block = tuple(a.shape) if block is None else tuple(block)
    index = (lambda *g: (0,) * len(block)) if index is None else index
    if isinstance(a, _Sel):
        lead = a.idx
        return pl.BlockSpec((None,) * len(lead) + block, lambda *g: lead + tuple(index(*g)))
    return pl.BlockSpec(block, lambda *g: tuple(index(*g)))


def _ospec(a):
    return pl.BlockSpec(tuple(a.shape), lambda *g: (0,) * a.ndim)


def _rows_of(shape):
    return lax.broadcasted_iota(jnp.int32, shape, 0)


def _lanes_of(shape):
    return lax.broadcasted_iota(jnp.int32, shape, 1)


def _sdn(x, k, fill):
    n = x.shape[0]
    return jnp.where(_rows_of(x.shape) >= k, pltpu.roll(x, k % n, 0), fill)


def _sup(x, k, fill):
    n = x.shape[0]
    return jnp.where(_rows_of(x.shape) < n - k, pltpu.roll(x, (n - k) % n, 0), fill)


@functools.partial(jax.custom_vjp, nondiff_argnums=(1,))
def _shift_dn(x, k):
    return pltpu.roll(x, k, 0)


def _shift_dn_fwd(x, k):
    return pltpu.roll(x, k, 0), None


def _shift_dn_bwd(k, _, g):
    return (pltpu.roll(g, g.shape[0] - k, 0),)


_shift_dn.defvjp(_shift_dn_fwd, _shift_dn_bwd)


SUBLANES = 8


def _lin_scan_impl(a, b, h0):
    n = a.shape[0]
    pos = _rows_of(a.shape) % SUBLANES
    aa, bb = a, b
    k = 1
    while k < SUBLANES:
        keep = pos >= k
        bb = bb + jnp.where(keep, aa * pltpu.roll(bb, k, 0), 0.0)
        aa = aa * jnp.where(keep, pltpu.roll(aa, k, 0), 1.0)
        k *= 2
    out, carry = [], h0
    for r in range(n // SUBLANES):
        rows = slice(r * SUBLANES, (r + 1) * SUBLANES)
        hr = bb[rows] + aa[rows] * carry
        out.append(hr)
        carry = hr[SUBLANES - 1:]
    return jnp.concatenate(out, axis=0)


@jax.custom_vjp
def _lin_scan(a, b, h0):
    return _lin_scan_impl(a, b, h0)


def _lin_scan_fwd(a, b, h0):
    h = _lin_scan_impl(a, b, h0)
    return h, (a, h, h0)


def _lin_scan_bwd(res, g):
    a, h, h0 = res
    n = a.shape[0]
    pos = _rows_of(a.shape) % SUBLANES
    cc, gg = _sup(a, 1, 0.0), g
    k = 1
    while k < SUBLANES:
        keep = pos < SUBLANES - k
        gg = gg + jnp.where(keep, cc * pltpu.roll(gg, n - k, 0), 0.0)
        cc = cc * jnp.where(keep, pltpu.roll(cc, n - k, 0), 1.0)
        k *= 2
    out, carry = [], jnp.zeros_like(h0)
    for r in range(n // SUBLANES - 1, -1, -1):
        rows = slice(r * SUBLANES, (r + 1) * SUBLANES)
        gr = gg[rows] + cc[rows] * carry
        out.append(gr)
        carry = gr[:1]
    gg = jnp.concatenate(out[::-1], axis=0)
    first = _rows_of(a.shape) == 0
    hprev = jnp.where(first, h0, _sdn(h, 1, 0.0))
    dh0 = jnp.sum(jnp.where(first, a * gg, 0.0), axis=0, keepdims=True)
    return gg * hprev, gg, dh0


_lin_scan.defvjp(_lin_scan_fwd, _lin_scan_bwd)


def _cumsum_sub_impl(x):
    pos = _rows_of(x.shape) % HGRN_SUB
    k = 1
    while k < HGRN_SUB:
        x = x + jnp.where(pos >= k, pltpu.roll(x, k, 0), 0.0)
        k *= 2
    return x


@jax.custom_vjp
def _cumsum_sub(x):
    return _cumsum_sub_impl(x)


def _cumsum_sub_fwd(x):
    return _cumsum_sub_impl(x), None


def _cumsum_sub_bwd(_, g):
    n = g.shape[0]
    pos = _rows_of(g.shape) % HGRN_SUB
    k = 1
    while k < HGRN_SUB:
        g = g + jnp.where(pos < HGRN_SUB - k, pltpu.roll(g, n - k, 0), 0.0)
        k *= 2
    return (g,)


_cumsum_sub.defvjp(_cumsum_sub_fwd, _cumsum_sub_bwd)


def _dot(a, b, ca, cb):
    return lax.dot_general(a.astype(BF16), b.astype(BF16), (((ca,), (cb,)), ((), ())), preferred_element_type=F32)


@jax.custom_vjp
def _mm(a, b):
    return _dot(a, b, 1, 0)


def _mm_fwd(a, b):
    return _dot(a, b, 1, 0), (a, b)


def _mm_bwd(res, g):
    a, b = res
    return _dot(g, b, 1, 1), _dot(a, g, 0, 0)


_mm.defvjp(_mm_fwd, _mm_bwd)


@jax.custom_vjp
def _mm_nt(a, b):
    return _dot(a, b, 1, 1)


def _mm_nt_fwd(a, b):
    return _dot(a, b, 1, 1), (a, b)


def _mm_nt_bwd(res, g):
    a, b = res
    return _dot(g, b, 1, 0), _dot(g, a, 0, 0)


_mm_nt.defvjp(_mm_nt_fwd, _mm_nt_bwd)


@jax.custom_vjp
def _mm_tn(a, b):
    return _dot(a, b, 0, 0)


def _mm_tn_fwd(a, b):
    return _dot(a, b, 0, 0), (a, b)


def _mm_tn_bwd(res, g):
    a, b = res
    return _dot(b, g, 1, 1), _dot(a, g, 1, 0)


_mm_tn.defvjp(_mm_tn_fwd, _mm_tn_bwd)


def _head_mask(shape, h):
    return (_lanes_of(shape) // HEAD_DIM) == h


def _stack_heads(x):
    return jnp.concatenate([jnp.where(_head_mask(x.shape, h), x, 0.0) for h in range(N_HEADS)], axis=0)


def _unstack_heads(p):
    r = p.shape[0] // N_HEADS
    out = None
    for h in range(N_HEADS):
        blk = p[h * r:(h + 1) * r]
        term = jnp.where(_head_mask(blk.shape, h), blk, 0.0)
        out = term if out is None else out + term
    return out


def _segmean_impl(x):
    n = x.shape[1]
    same = (lax.broadcasted_iota(jnp.int32, (n, n), 0) // HEAD_DIM) == (lax.broadcasted_iota(jnp.int32, (n, n), 1) // HEAD_DIM)
    m = jnp.where(same, 1.0 / HEAD_DIM, 0.0).astype(BF16)
    hi = x.astype(BF16)
    lo = (x - hi.astype(F32)).astype(BF16)
    dn = (((1,), (0,)), ((), ()))
    return (lax.dot_general(hi, m, dn, preferred_element_type=F32)
            + lax.dot_general(lo, m, dn, preferred_element_type=F32))


@jax.custom_vjp
def _segmean(x):
    return _segmean_impl(x)


def _segmean_fwd(x):
    return _segmean_impl(x), None


def _segmean_bwd(_, g):
    return (_segmean_impl(g),)


_segmean.defvjp(_segmean_fwd, _segmean_bwd)


GELU_C = 0.7978845608028654
GELU_A = 0.044715


@jax.custom_vjp
def _gelu(x):
    return 0.5 * x * (1.0 + jnp.tanh(GELU_C * x * (1.0 + GELU_A * (x * x))))


def _gelu_fwd(x):
    x2 = x * x
    t = jnp.tanh(GELU_C * x * (1.0 + GELU_A * x2))
    return 0.5 * x * (1.0 + t), (x, x2, t)


def _gelu_bwd(res, g):
    x, x2, t = res
    half = 0.5 * (1.0 + t)
    return (g * (half + (0.5 * GELU_C) * x * (1.0 - t * t) * (1.0 + (3.0 * GELU_A) * x2)),)


_gelu.defvjp(_gelu_fwd, _gelu_bwd)


def _log1p(u):
    w = 1.0 + u
    return jnp.where(w == 1.0, u, jnp.log(w) * (u / (w - 1.0)))


def _softplus(y):
    return jnp.maximum(y, 0.0) + _log1p(jnp.exp(-jnp.abs(y)))


def _rms(x, g):
    return x * lax.rsqrt(jnp.mean(x * x, axis=-1, keepdims=True) + EPS) * g


def _gmlp_chunk(zu, zv, ln_g, ln_b, wcat, bfull):
    u = _gelu(zu)
    v = _gelu(zv)
    mu = jnp.mean(v, axis=-1, keepdims=True)
    var = jnp.mean(jnp.square(v - mu), axis=-1, keepdims=True)
    vn = (v - mu) * lax.rsqrt(var + EPS) * ln_g + ln_b
    sv = _unstack_heads(_mm(wcat, vn)) + bfull
    return u * sv


def _rglru_tile(xb_ext, gb, h0, cw, cb, wa, ba, wx, bx, lam):
    xc = (cb + cw[0:1] * _shift_dn(xb_ext, 3) + cw[1:2] * _shift_dn(xb_ext, 2) + cw[2:3] * _shift_dn(xb_ext, 1)
          + cw[3:4] * xb_ext)[8:]
    r = jax.nn.sigmoid(_mm(xc, wa) + ba)
    i = jax.nn.sigmoid(_mm(xc, wx) + bx)
    log_a = (-RGLRU_C) * r * _softplus(-lam)
    a = jnp.exp(log_a)
    mult = jnp.sqrt(-jnp.tanh(log_a) * (a * a + 1.0))
    h = _lin_scan(a, mult * (i * xc), h0)
    y = h * _gelu(gb)
    h_last = jnp.sum(jnp.where(_rows_of(h.shape) == h.shape[0] - 1, h, 0.0), axis=0, keepdims=True)
    return y, h_last


def _pool_tile(xd_ext, inv, wd, scale):
    s1 = xd_ext + _shift_dn(xd_ext, 1)
    s2 = s1 + _shift_dn(s1, 2)
    s3 = s2 + _shift_dn(s2, 4)
    s4 = s3 + _shift_dn(s3, 8)
    grp = _lanes_of(xd_ext.shape) // HEAD_DIM
    win = jnp.where(grp == 0, s1, jnp.where(grp == 1, s2, jnp.where(grp == 2, s3, s4)))
    pooled = win[16:] * inv - xd_ext[16:]
    return _mm(pooled, wd) * scale


def _hgrn_chunk(q, f, i, g, st, lb, ngf):
    n = q.shape[0]
    nsub = n // HGRN_SUB
    qs = jax.nn.silu(q)
    fg = lb + (1.0 - lb) * jax.nn.sigmoid(f)
    lf = jnp.log(fg)
    k = 1.0 - fg
    bl = _cumsum_sub(lf)
    row = _rows_of(q.shape)
    blk = row // HGRN_SUB
    betas = [jnp.zeros_like(lb)]
    for s in range(nsub):
        tot = jnp.sum(jnp.where(row == s * HGRN_SUB + HGRN_SUB - 1, bl, 0.0), axis=0, keepdims=True)
        betas.append(betas[-1] + tot)
    b_end = betas[nsub]
    beta_full = jnp.zeros_like(q)
    for s in range(1, nsub):
        beta_full = jnp.where(blk == s, betas[s], beta_full)
    qh = qs * jnp.exp(bl)
    qt = qh * jnp.exp(beta_full)
    b_all = beta_full + bl
    kt = k * jnp.exp(b_end - b_all)
    outs = []
    for s in range(nsub):
        kh = k * jnp.exp(jnp.minimum(betas[s] - b_all, HGRN_EXP_CLAMP))
        qstk = _stack_heads(qh[s * HGRN_SUB:(s + 1) * HGRN_SUB])
        att = _mm_nt(qstk, kh)
        ar = _rows_of(att.shape) % HGRN_SUB + s * HGRN_SUB
        att = jnp.where(_lanes_of(att.shape) <= ar, att, 0.0)
        outs.append(_unstack_heads(_mm(att, i)))
    o = jnp.concatenate(outs, axis=0) + _mm_nt(qt, st)
    same = (_rows_of(st.shape) // HEAD_DIM) == (_lanes_of(st.shape) // HEAD_DIM)
    st_new = st * jnp.exp(b_end) + jnp.where(same, _mm_tn(i, kt), 0.0)
    on = o * lax.rsqrt(_segmean(o * o) + EPS) * ngf
    return on * jax.nn.silu(g), st_new


def _ffn_tile(eg, ev, wg, bg, wv, bv, halo=8):
    gt = (bg + wg[0:1] * _shift_dn(eg, 2) + wg[1:2] * _shift_dn(eg, 1) + wg[2:3] * eg)[halo:]
    val = (bv + wv[0:1] * _shift_dn(ev, 2) + wv[1:2] * _shift_dn(ev, 1) + wv[2:3] * ev)[halo:]
    return _gelu(gt) * val


MXU_WIDTH = 256
MATMUL_BLOCK_BUDGET = 18 * MIB


def _matmul_tiles(m, k, n, a_dtype, b_dtype, out_dtype, has_res):
    best = None
    for tm in (2048, 1024, 512, 256):
        if m % tm:
            continue
        for tn in (1024, 768, 1408, 512, 256, 128):
            if n % tn:
                continue
            blk = (_nbytes((tm, k), a_dtype) + _nbytes((k, tn), b_dtype) + _nbytes((tm, tn), out_dtype)
                   + (_nbytes((tm, tn), F32) if has_res else 0))
            if blk > MATMUL_BLOCK_BUDGET:
                continue
            waste = -(-tn // MXU_WIDTH) * MXU_WIDTH / tn
            cost = (m // tm) * (n // tn) + 64 * (waste - 1.0)
            if best is None or cost < best[0]:
                best = (cost, tm, tn, blk)
    assert best is not None, (m, k, n)
    return best[1:]


def _matmul(a, b, *, name, nt=False, res=None, out_dtype=F32):
    m, k = a.shape
    n = b.shape[0] if nt else b.shape[1]
    tm, tn, blk = _matmul_tiles(m, k, n, a.dtype, b.dtype, out_dtype, res is not None)
    dims = (((1,), (1,)), ((), ())) if nt else (((1,), (0,)), ((), ()))

    def body(*refs):
        if res is None:
            a_ref, b_ref, o_ref = refs
        else:
            a_ref, b_ref, r_ref, o_ref = refs
        acc = lax.dot_general(a_ref[...], b_ref[...], dims, preferred_element_type=F32)
        if res is not None:
            acc = acc + r_ref[...]
        o_ref[...] = acc.astype(out_dtype)

    in_specs = [pl.BlockSpec((tm, k), lambda i, j: (i, 0)),
                _spec(b, (tn, k), lambda i, j: (j, 0)) if nt else _spec(b, (k, tn), lambda i, j: (0, j))]
    args = [a, _arr(b)]
    if res is not None:
        in_specs.append(pl.BlockSpec((tm, tn), lambda i, j: (i, j)))
        args.append(res)
    return _pcall(body, name=name, out_shape=_sds((m, n), out_dtype), grid=(m // tm, n // tn), in_specs=in_specs,
                  out_specs=pl.BlockSpec((tm, tn), lambda i, j: (i, j)), semantics=("parallel", "parallel"),
                  block_bytes=blk + _nbytes((tm, tn), F32))(*args)


def _matmul_rms_bwd(a, b, x, g, dres, *, name, nt=False, res=None):
    m, k = a.shape
    n = b.shape[0] if nt else b.shape[1]
    tm = _pick(m, (512, 256))
    dims = (((1,), (1,)), ((), ())) if nt else (((1,), (0,)), ((), ()))

    def body(*refs):
        a_ref, b_ref, x_ref, g_ref, dr_ref = refs[:5]
        dx_ref, dxb_ref, dg_ref = refs[-3:]
        dh = lax.dot_general(a_ref[...], b_ref[...], dims, preferred_element_type=F32)
        if res is not None:
            dh = dh + refs[5][...]
        _, vjp = jax.vjp(_rms, x_ref[...], g_ref[...])
        dxn, dg = vjp(dh)
        dx = dr_ref[...] + dxn
        dx_ref[...] = dx
        dxb_ref[...] = dx.astype(BF16)
        _acc_out(dg_ref, dg, pl.program_id(0) == 0)

    row = pl.BlockSpec((tm, n), lambda i: (i, 0))
    vec = pl.BlockSpec((1, n), lambda i: (0, 0))
    in_specs = [pl.BlockSpec((tm, k), lambda i: (i, 0)),
                _spec(b, (n, k), lambda i: (0, 0)) if nt else _spec(b, (k, n), lambda i: (0, 0)), row, _spec(g), row]
    args = [a, _arr(b), x, _arr(g), dres]
    if res is not None:
        in_specs.append(row)
        args.append(res)
    blk = _nbytes((tm, k), a.dtype) + _nbytes((k, n), b.dtype) + 6 * _nbytes((tm, n), F32)
    return _pcall(body, name=name, out_shape=(_sds((m, n), F32), _sds((m, n), BF16), _sds((1, n), F32)), grid=(m // tm,),
                  in_specs=in_specs, out_specs=(row, row, vec), semantics=("arbitrary",), block_bytes=blk)(*args)


def _matmul_tn(a, b, *, name, out_dtype=BF16, out_rows=None, row_off=0, into=None):
    m, k1 = a.shape
    n = b.shape[1]
    tk = _pick(k1, (512, 256, 128))
    off = row_off // tk
    assert off * tk == row_off

    def body(a_ref, b_ref, *rest):
        rest[-1][...] = lax.dot_general(a_ref[...], b_ref[...], (((0,), (0,)), ((), ())),
                                        preferred_element_type=F32).astype(out_dtype)

    blk = 2 * _nbytes((m, tk), a.dtype) + _nbytes((m, n), b.dtype) + _nbytes((tk, n), F32)
    in_specs = [pl.BlockSpec((m, tk), lambda i: (0, i)), pl.BlockSpec((m, n), lambda i: (0, 0))]
    args = [a, b]
    if into is not None:
        in_specs.append(HBM_SPEC)
        args.append(into)
    return _pcall(body, name=name, out_shape=_sds((out_rows or k1, n), out_dtype), grid=(k1 // tk,), in_specs=in_specs,
                  out_specs=pl.BlockSpec((tk, n), lambda i: (i + off, 0)), semantics=("parallel",), block_bytes=blk,
                  aliases=None if into is None else {2: 0})(*args)


def _rms_matmul(x, g, bs, *, name, nt=False, ple=None):
    m, d = x.shape
    n = bs[0].shape[0] if nt else bs[0].shape[1]
    nb = len(bs)
    nout = nb if ple is None else 3
    best = None
    for tm_c in (1024, 512, 256):
        for tn_c in (1408, 1024, 768, 512, 256, 128):
            if m % tm_c or n % tn_c:
                continue
            blk_c = (_nbytes((tm_c, d), F32) + 2 * _nbytes((tm_c, d), BF16) + nb * _nbytes((d, tn_c), BF16)
                     + (nout + 1) * _nbytes((tm_c, tn_c), F32))
            steps = (m // tm_c) * (n // tn_c)
            if blk_c <= MATMUL_BLOCK_BUDGET and (best is None or steps < best[0]):
                best = (steps, tm_c, tn_c, blk_c)
    _, tm, tn, blk = best
    dims = (((1,), (1,)), ((), ())) if nt else (((1,), (0,)), ((), ()))

    def body(*refs):
        x_ref, g_ref, b_refs = refs[0], refs[1], refs[2:2 + nb]
        rest = refs[2 + nb:]
        h_scr = rest[-1]
        j = pl.program_id(1)

        @pl.when(j == 0)
        def _():
            h = _rms(x_ref[...], g_ref[...]).astype(BF16)
            h_scr[...] = h
            rest[-2 - nb - (2 if ple else 0)][...] = h

        h = h_scr[...]
        if ple is None:
            for k in range(nb):
                rest[-1 - nb + k][...] = lax.dot_general(h, b_refs[k][...], dims, preferred_element_type=F32)
        else:
            p_ref, wpe_ref, xt_ref = rest[0], rest[1], rest[2]
            gl_ref, pe_ref, out_ref = rest[-4], rest[-3], rest[-2]
            gl = lax.dot_general(h, b_refs[0][...], dims, preferred_element_type=F32)
            pe = lax.dot_general(p_ref[...], wpe_ref[...], (((1,), (1,)), ((), ())), preferred_element_type=F32)
            gl_ref[...] = gl
            pe_ref[...] = pe
            out_ref[...] = xt_ref[...] + pe * jax.nn.sigmoid(gl)

    row = pl.BlockSpec((tm, d), lambda i, j: (i, 0))
    tile = pl.BlockSpec((tm, tn), lambda i, j: (i, j))
    in_specs = [row, _spec(g)] + [_spec(b, (tn, d), lambda i, j: (j, 0)) if nt else _spec(b, (d, tn), lambda i, j: (0, j))
                                  for b in bs]
    args = [x, _arr(g)] + [_arr(b) for b in bs]
    out_shape, out_specs = [_sds((m, d), BF16)], [row]
    if ple is None:
        out_shape += [_sds((m, n), F32)] * nb
        out_specs += [tile] * nb
    else:
        p, wpe = ple
        in_specs += [pl.BlockSpec((tm, p.shape[1]), lambda i, j: (i, 0)), _spec(wpe, (tn, p.shape[1]), lambda i, j: (j, 0)),
                     tile]
        args += [p, _arr(wpe), x]
        out_shape += [_sds((m, n), F32)] * 3
        out_specs += [tile] * 3
    outs = _pcall(body, name=name, out_shape=tuple(out_shape), grid=(m // tm, n // tn), in_specs=in_specs,
                  out_specs=tuple(out_specs), scratch_shapes=[pltpu.VMEM((tm, d), BF16)],
                  semantics=("parallel", "arbitrary"), block_bytes=blk)(*args)
    return outs[0], list(outs[1:])


def _up_ffn_fwd(x, g, wg, wv, cwf, cbf, *, name):
    m, d = x.shape
    n = wg.shape[0]
    tm = _pick(m, (256, 128))
    tn = _pick(n, (1408, 256, 128))
    nj = n // tn
    dims = (((1,), (1,)), ((), ()))

    def body(x_ref, g_ref, wg_ref, wv_ref, tg_ref, bg_ref, tv_ref, bv_ref, h_ref, hg_ref, hv_ref, a_ref, cg_scr, cv_scr):
        i = pl.program_id(1)
        h = _rms(x_ref[...], g_ref[...]).astype(BF16)
        h_ref[...] = h
        hg_ref[...] = lax.dot_general(h, wg_ref[...], dims, preferred_element_type=F32).astype(BF16)
        hv_ref[...] = lax.dot_general(h, wv_ref[...], dims, preferred_element_type=F32).astype(BF16)
        hg = hg_ref[...].astype(F32)
        hv = hv_ref[...].astype(F32)
        eg = jnp.concatenate([jnp.where(i == 0, 0.0, cg_scr[...]), hg], axis=0)
        ev = jnp.concatenate([jnp.where(i == 0, 0.0, cv_scr[...]), hv], axis=0)
        a_ref[...] = _ffn_tile(eg, ev, tg_ref[...], bg_ref[...], tv_ref[...], bv_ref[...]).astype(BF16)
        cg_scr[...] = hg[tm - 8:]
        cv_scr[...] = hv[tm - 8:]

    row = pl.BlockSpec((tm, d), lambda j, i: (i, 0))
    hrow = pl.BlockSpec((tm, d), lambda j, i: (j * (m // tm) + i, 0))
    tile = pl.BlockSpec((tm, tn), lambda j, i: (i, j))
    wspec = lambda w: _spec(w, (tn, d), lambda j, i: (j, 0))
    taps = lambda off: _spec(cwf, (3, tn), lambda j, i: (0, j + off))
    bias = lambda off: _spec(cbf, (1, tn), lambda j, i: (0, j + off))
    blk = (_nbytes((tm, d), F32) + _nbytes((tm, d), BF16) + 2 * _nbytes((tn, d), BF16) + 12 * _nbytes((tm, tn), F32))
    return _pcall(body, name=name,
                  out_shape=(_sds((nj * m, d), BF16), _sds((m, n), BF16), _sds((m, n), BF16), _sds((m, n), BF16)),
                  grid=(nj, m // tm),
                  in_specs=[row, _spec(g), wspec(wg), wspec(wv), taps(0), bias(0), taps(nj), bias(nj)],
                  out_specs=(hrow, tile, tile, tile),
                  scratch_shapes=[pltpu.VMEM((8, tn), F32), pltpu.VMEM((8, tn), F32)],
                  semantics=("arbitrary", "arbitrary"), block_bytes=blk)(
                      x, _arr(g), _arr(wg), _arr(wv), _arr(cwf), _arr(cbf), _arr(cwf), _arr(cbf))


def _ple_bwd(dx, gl, pe, *, name):
    s, d = dx.shape
    tm = _pick(s, (512, 256))

    def body(dx_ref, gl_ref, pe_ref, dpe_ref, dgl_ref):
        gate = jax.nn.sigmoid(gl_ref[...])
        dxv = dx_ref[...]
        dpe_ref[...] = (dxv * gate).astype(BF16)
        dgl_ref[...] = (dxv * pe_ref[...] * gate * (1.0 - gate)).astype(BF16)

    row = pl.BlockSpec((tm, d), lambda i: (i, 0))
    return _pcall(body, name=name, out_shape=(_sds((s, d), BF16), _sds((s, d), BF16)), grid=(s // tm,),
                  in_specs=[row, row, row], out_specs=(row, row), semantics=("parallel",),
                  block_bytes=5 * _nbytes((tm, d), F32))(dx, gl, pe)


def _loss_head(x, g, target, *, name):
    s, d = x.shape
    tm = _pick(s, (256, 128))

    def tile_loss(xv, gv, tv):
        err = jnp.square(_rms(xv, gv) - tv)
        return 0.5 * jnp.sum(jnp.mean(err, axis=-1, keepdims=True), axis=0, keepdims=True)

    def body(x_ref, g_ref, t_ref, l_ref, dx_ref, dg_ref):
        lv, vjp = jax.vjp(tile_loss, x_ref[...], g_ref[...], t_ref[...])
        dxv, dgv, _ = vjp(jnp.ones((1, 1), F32))
        dx_ref[...] = dxv

        @pl.when(pl.program_id(0) == 0)
        def _():
            l_ref[...] = jnp.zeros_like(l_ref)
            dg_ref[...] = jnp.zeros_like(dg_ref)

        l_ref[...] += jnp.broadcast_to(lv, l_ref.shape)
        dg_ref[...] += dgv

    row = pl.BlockSpec((tm, d), lambda i: (i, 0))
    vec = pl.BlockSpec((1, d), lambda i: (0, 0))
    return _pcall(body, name=name, out_shape=(_sds((8, 128), F32), _sds((s, d), F32), _sds((1, d), F32)),
                  grid=(s // tm,), in_specs=[row, vec, row],
                  out_specs=(pl.BlockSpec((8, 128), lambda i: (0, 0)), row, vec), semantics=("arbitrary",),
                  block_bytes=8 * _nbytes((tm, d), F32))(x, g, target)


def _acc_out(ref, val, first):
    @pl.when(first)
    def _():
        ref[...] = jnp.zeros_like(ref)

    ref[...] += val


def _gmlp_fwd(z, ln_g, ln_b, wcat, bfull, *, name):
    s = z.shape[0]
    t = _pick(s, (512, 256, 128))
    nch = t // GMLP_CHUNK

    def body(zu_ref, zv_ref, g_ref, b_ref, w_ref, bf_ref, o_ref):
        for c in range(nch):
            rows = pl.ds(c * GMLP_CHUNK, GMLP_CHUNK)
            o_ref[rows, :] = _gmlp_chunk(zu_ref[rows, :], zv_ref[rows, :], g_ref[...], b_ref[...], w_ref[...],
                                         bf_ref[...]).astype(BF16)

    col = lambda c: pl.BlockSpec((t, W_GRP), lambda i: (i, c))
    params = (ln_g, ln_b, wcat, bfull)
    return _pcall(body, name=name, out_shape=_sds((s, D_MODEL), BF16), grid=(s // t,),
                  in_specs=[col(0), col(1)] + [_spec(a) for a in params],
                  out_specs=pl.BlockSpec((t, W_GRP), lambda i: (i, 0)), semantics=("parallel",),
                  block_bytes=4 * _nbytes((t, W_GRP), F32))(z, z, *[_arr(a) for a in params])


def _gmlp_bwd(z, dmix, ln_g, ln_b, wcat, bfull, *, name):
    s = z.shape[0]
    t = _pick(s, (512, 256, 128))
    nch = t // GMLP_CHUNK

    def body(zu_ref, zv_ref, dy_ref, g_ref, b_ref, w_ref, bf_ref, dz_ref, dg_ref, db_ref, dw_ref, dbf_ref):
        acc = None
        for c in range(nch):
            rows = pl.ds(c * GMLP_CHUNK, GMLP_CHUNK)
            _, vjp = jax.vjp(_gmlp_chunk, zu_ref[rows, :], zv_ref[rows, :], g_ref[...], b_ref[...], w_ref[...],
                             bf_ref[...])
            du, dv, *dps = vjp(dy_ref[rows, :])
            dz_ref[rows, :] = jnp.concatenate([du, dv], axis=1).astype(BF16)
            acc = dps if acc is None else [x + y for x, y in zip(acc, dps)]
        first = pl.program_id(0) == 0
        for ref, val in zip((dg_ref, db_ref, dw_ref, dbf_ref), acc):
            _acc_out(ref, val, first)

    col = lambda c: pl.BlockSpec((t, W_GRP), lambda i: (i, c))
    params = (ln_g, ln_b, wcat, bfull)
    return _pcall(body, name=name,
                  out_shape=(_sds((s, D_PROJ), BF16),) + tuple(_sds(a.shape, F32) for a in params),
                  grid=(s // t,), in_specs=[col(0), col(1), col(0)] + [_spec(a) for a in params],
                  out_specs=(pl.BlockSpec((t, 2 * W_GRP), lambda i: (i, 0)),) + tuple(_ospec(a) for a in params),
                  semantics=("arbitrary",),
                  block_bytes=8 * _nbytes((t, W_GRP), F32))(z, z, dmix, *[_arr(a) for a in params])


def _rglru_fwd(z, prm, mix, *, name):
    s = z.shape[0]
    t = _pick(s, (512, 256, 128))
    nt = s // t

    def body(xb_ref, halo_ref, gb_ref, *rest):
        prm_refs, (y_ref, h0s_ref, h_scr) = rest[:len(prm)], rest[len(prm) + 1:]
        i = pl.program_id(0)

        @pl.when(i == 0)
        def _():
            h_scr[...] = jnp.zeros_like(h_scr)

        halo = jnp.where(i == 0, 0.0, halo_ref[...])
        h0 = h_scr[...]
        y, h_last = _rglru_tile(jnp.concatenate([halo, xb_ref[...]], axis=0), gb_ref[...], h0,
                                *[r[...] for r in prm_refs])
        y_ref[...] = y.astype(BF16)
        h0s_ref[...] = jnp.broadcast_to(h0, h0s_ref.shape)
        h_scr[...] = h_last

    in_specs = [pl.BlockSpec((t, W_GRP), lambda i: (i, 2)),
                pl.BlockSpec((8, W_GRP), lambda i: (jnp.maximum(i * (t // 8) - 1, 0), 2)),
                pl.BlockSpec((t, W_GRP), lambda i: (i, 3))] + [_spec(a) for a in prm] + [HBM_SPEC]
    return _pcall(body, name=name, out_shape=(_sds(mix.shape, BF16), _sds((nt, 8, W_GRP), F32)), grid=(nt,),
                  in_specs=in_specs,
                  out_specs=(pl.BlockSpec((t, W_GRP), lambda i: (i, 1)), pl.BlockSpec((None, 8, W_GRP), lambda i: (i, 0, 0))),
                  scratch_shapes=[pltpu.VMEM((1, W_GRP), F32)], semantics=("arbitrary",),
                  block_bytes=24 * _nbytes((t, W_GRP), F32), aliases={3 + len(prm): 0})(
                      z, z, z, *[_arr(a) for a in prm], mix)


def _rglru_bwd(z, dmix, h0s, prm, dz, *, name):
    s = z.shape[0]
    t = _pick(s, (512, 256, 128))
    nt = s // t
    npm = len(prm)

    def body(xb_ref, halo_ref, gb_ref, dy_ref, h0s_ref, *rest):
        prm_refs = rest[:npm]
        dz_ref = rest[npm + 1]
        dprm_refs = rest[npm + 2:2 * npm + 2]
        dh_scr, dhalo_scr = rest[2 * npm + 2:]
        i = pl.program_id(0)
        r = nt - 1 - i

        @pl.when(i == 0)
        def _():
            dh_scr[...] = jnp.zeros_like(dh_scr)
            dhalo_scr[...] = jnp.zeros_like(dhalo_scr)

        halo = jnp.where(r == 0, 0.0, halo_ref[...])
        h0 = h0s_ref[0:1, :]
        _, vjp = jax.vjp(_rglru_tile, jnp.concatenate([halo, xb_ref[...]], axis=0), gb_ref[...], h0,
                         *[p[...] for p in prm_refs])
        dext, dgb, _dh0, *dps = vjp((dy_ref[...], dh_scr[...]))
        dmain = dext[8:]
        dxb = jnp.concatenate([dmain[:t - 8], dmain[t - 8:] + dhalo_scr[...]], axis=0)
        dz_ref[...] = jnp.concatenate([dxb, dgb], axis=1).astype(BF16)
        dh_scr[...] = _dh0
        dhalo_scr[...] = dext[:8]
        for ref, val in zip(dprm_refs, dps):
            _acc_out(ref, val, i == 0)

    rev = lambda c: pl.BlockSpec((t, W_GRP), lambda i: (nt - 1 - i, c))
    in_specs = [rev(2), pl.BlockSpec((8, W_GRP), lambda i: (jnp.maximum((nt - 1 - i) * (t // 8) - 1, 0), 2)), rev(3),
                rev(1), pl.BlockSpec((None, 8, W_GRP), lambda i: (nt - 1 - i, 0, 0))] + [_spec(a) for a in prm] + [HBM_SPEC]
    return _pcall(body, name=name,
                  out_shape=(_sds(dz.shape, BF16),) + tuple(_sds(a.shape, F32) for a in prm),
                  grid=(nt,), in_specs=in_specs,
                  out_specs=(pl.BlockSpec((t, 2 * W_GRP), lambda i: (nt - 1 - i, 1)),) + tuple(_ospec(a) for a in prm),
                  scratch_shapes=[pltpu.VMEM((1, W_GRP), F32), pltpu.VMEM((8, W_GRP), F32)],
                  semantics=("arbitrary",), block_bytes=40 * _nbytes((t, W_GRP), F32), aliases={5 + npm: 0})(
                      z, z, z, dmix, h0s, *[_arr(a) for a in prm], dz)


def _pool_inv(i, t):
    pos = (_rows_of((t, W_GRP)) + i * t + 1).astype(F32)
    grp = _lanes_of((t, W_GRP)) // HEAD_DIM
    win = jnp.where(grp == 0, float(POOL_WINDOWS[0]), jnp.where(grp == 1, float(POOL_WINDOWS[1]),
                    jnp.where(grp == 2, float(POOL_WINDOWS[2]), float(POOL_WINDOWS[3]))))
    return 1.0 / jnp.minimum(pos, win)


def _pool_fwd(z, wd, scale, mix, *, name):
    s = z.shape[0]
    t = _pick(s, (512, 256, 128))

    def body(x_ref, halo_ref, wd_ref, sc_ref, _, y_ref):
        i = pl.program_id(0)
        halo = jnp.where(i == 0, 0.0, halo_ref[...])
        y = _pool_tile(jnp.concatenate([halo, x_ref[...]], axis=0), _pool_inv(i, t), wd_ref[...], sc_ref[...])
        y_ref[...] = y.astype(BF16)

    in_specs = [pl.BlockSpec((t, W_GRP), lambda i: (i, 8)),
                pl.BlockSpec((16, W_GRP), lambda i: (jnp.maximum(i * (t // 16) - 1, 0), 8)), _spec(wd), _spec(scale),
                HBM_SPEC]
    return _pcall(body, name=name, out_shape=_sds(mix.shape, BF16), grid=(s // t,), in_specs=in_specs,
                  out_specs=pl.BlockSpec((t, W_GRP), lambda i: (i, 3)), semantics=("parallel",),
                  block_bytes=12 * _nbytes((t, W_GRP), F32), aliases={4: 0})(z, z, _arr(wd), _arr(scale), mix)


def _pool_bwd(z, dmix, wd, scale, dz, *, name):
    s = z.shape[0]
    t = _pick(s, (512, 256, 128))
    nt = s // t

    def body(x_ref, halo_ref, dy_ref, wd_ref, sc_ref, _, dx_ref, dwd_ref, dsc_ref, dhalo_scr):
        i = pl.program_id(0)
        r = nt - 1 - i

        @pl.when(i == 0)
        def _():
            dhalo_scr[...] = jnp.zeros_like(dhalo_scr)

        halo = jnp.where(r == 0, 0.0, halo_ref[...])
        inv = _pool_inv(r, t)
        _, vjp = jax.vjp(lambda e, w, sc: _pool_tile(e, inv, w, sc), jnp.concatenate([halo, x_ref[...]], axis=0),
                         wd_ref[...], sc_ref[...])
        dext, dwd, dsc = vjp(dy_ref[...])
        dmain = dext[16:]
        dx = jnp.concatenate([dmain[:t - 16], dmain[t - 16:] + dhalo_scr[...]], axis=0)
        dx_ref[...] = dx.astype(BF16)
        dhalo_scr[...] = dext[:16]
        _acc_out(dwd_ref, dwd, i == 0)
        _acc_out(dsc_ref, dsc, i == 0)

    rev = lambda c: pl.BlockSpec((t, W_GRP), lambda i: (nt - 1 - i, c))
    in_specs = [rev(8), pl.BlockSpec((16, W_GRP), lambda i: (jnp.maximum((nt - 1 - i) * (t // 16) - 1, 0), 8)), rev(3),
                _spec(wd), _spec(scale), HBM_SPEC]
    return _pcall(body, name=name, out_shape=(_sds(dz.shape, BF16), _sds(wd.shape, F32), _sds(scale.shape, F32)),
                  grid=(nt,), in_specs=in_specs, out_specs=(rev(8), _ospec(wd), _ospec(scale)),
                  scratch_shapes=[pltpu.VMEM((16, W_GRP), F32)], semantics=("arbitrary",),
                  block_bytes=20 * _nbytes((t, W_GRP), F32), aliases={5: 0})(z, z, dmix, _arr(wd), _arr(scale), dz)


def _hgrn_fwd(z, lb, ngf, mix, *, name):
    s = z.shape[0]
    c = HGRN_CHUNK
    per = HGRN_STEP_CHUNKS
    ns = s // (c * per)

    def body(q_ref, f_ref, i_ref, g_ref, lb_ref, ng_ref, _, y_ref, sts_ref, st_scr):
        @pl.when(pl.program_id(0) == 0)
        def _():
            st_scr[...] = jnp.zeros_like(st_scr)

        st = st_scr[...]
        for k in range(per):
            rows = pl.ds(k * c, c)
            sts_ref[k] = st
            y, st = _hgrn_chunk(q_ref[rows, :], f_ref[rows, :], i_ref[rows, :], g_ref[rows, :], st, lb_ref[...],
                                ng_ref[...])
            y_ref[rows, :] = y.astype(BF16)
        st_scr[...] = st

    col = lambda k: pl.BlockSpec((per * c, W_GRP), lambda i: (i, k))
    return _pcall(body, name=name, out_shape=(_sds(mix.shape, BF16), _sds((ns * per, W_GRP, W_GRP), F32)), grid=(ns,),
                  in_specs=[col(4), col(5), col(6), col(7), _spec(lb), _spec(ngf), HBM_SPEC],
                  out_specs=(pl.BlockSpec((per * c, W_GRP), lambda i: (i, 2)),
                             pl.BlockSpec((per, W_GRP, W_GRP), lambda i: (i, 0, 0))),
                  scratch_shapes=[pltpu.VMEM((W_GRP, W_GRP), F32)], semantics=("arbitrary",),
                  block_bytes=16 * per * _nbytes((W_GRP, W_GRP), F32), aliases={6: 0})(
                      z, z, z, z, _arr(lb), _arr(ngf), mix)


def _hgrn_bwd(z, dmix, sts, lb, ngf, dz, *, name):
    s = z.shape[0]
    c = HGRN_CHUNK
    per = HGRN_STEP_CHUNKS
    ns = s // (c * per)

    def body(q_ref, f_ref, i_ref, g_ref, dy_ref, st_ref, lb_ref, ng_ref, _, dz_ref, dlb_ref, dng_ref, dst_scr):
        i = pl.program_id(0)

        @pl.when(i == 0)
        def _():
            dst_scr[...] = jnp.zeros_like(dst_scr)

        dst = dst_scr[...]
        dlb_sum = dng_sum = None
        for k in range(per - 1, -1, -1):
            rows = pl.ds(k * c, c)
            _, vjp = jax.vjp(_hgrn_chunk, q_ref[rows, :], f_ref[rows, :], i_ref[rows, :], g_ref[rows, :], st_ref[k],
                             lb_ref[...], ng_ref[...])
            dq, df, di, dg, dst, dlb, dng = vjp((dy_ref[rows, :], dst))
            dz_ref[rows, :] = jnp.concatenate([dq, df, di, dg], axis=1).astype(BF16)
            dlb_sum = dlb if dlb_sum is None else dlb_sum + dlb
            dng_sum = dng if dng_sum is None else dng_sum + dng
        dst_scr[...] = dst
        _acc_out(dlb_ref, dlb_sum, i == 0)
        _acc_out(dng_ref, dng_sum, i == 0)

    rev = lambda k: pl.BlockSpec((per * c, W_GRP), lambda i: (ns - 1 - i, k))
    vec = pl.BlockSpec((1, W_GRP), lambda i: (0, 0))
    return _pcall(body, name=name, out_shape=(_sds(dz.shape, BF16), _sds((1, W_GRP), F32), _sds((1, W_GRP), F32)),
                  grid=(ns,),
                  in_specs=[rev(4), rev(5), rev(6), rev(7), rev(2),
                            pl.BlockSpec((per, W_GRP, W_GRP), lambda i: (ns - 1 - i, 0, 0)), _spec(lb), _spec(ngf),
                            HBM_SPEC],
                  out_specs=(pl.BlockSpec((per * c, 4 * W_GRP), lambda i: (ns - 1 - i, 1)), vec, vec),
                  scratch_shapes=[pltpu.VMEM((W_GRP, W_GRP), F32)], semantics=("arbitrary",),
                  block_bytes=32 * per * _nbytes((W_GRP, W_GRP), F32), aliases={8: 0})(
                      z, z, z, z, dmix, sts, _arr(lb), _arr(ngf), dz)


def _lbs_fwd(c_lb, *, name):
    def body(c_ref, o_ref):
        c = c_ref[...]
        e = jnp.exp(c - jnp.max(c, axis=0, keepdims=True))
        sm = e / jnp.sum(e, axis=0, keepdims=True)
        run = jnp.zeros((1, W_GRP), F32)
        o_ref[0:1, :] = run
        for l in range(1, DEPTH):
            run = run + sm[l:l + 1]
            o_ref[l:l + 1, :] = run

    return _pcall(body, name=name, out_shape=_sds((DEPTH, W_GRP), F32), pin=False)(c_lb)


def _lbs_bwd(c_lb, dlbs, *, name):
    def body(c_ref, d_ref, o_ref):
        c = c_ref[...]
        e = jnp.exp(c - jnp.max(c, axis=0, keepdims=True))
        sm = e / jnp.sum(e, axis=0, keepdims=True)
        d = d_ref[...]
        dsm = [None] * DEPTH
        run = jnp.zeros((1, W_GRP), F32)
        for l in range(DEPTH - 1, 0, -1):
            run = run + d[l:l + 1]
            dsm[l] = run
        dsm[0] = jnp.zeros((1, W_GRP), F32)
        inner = sum(sm[l:l + 1] * dsm[l] for l in range(DEPTH))
        for l in range(DEPTH):
            o_ref[l:l + 1, :] = sm[l:l + 1] * (dsm[l] - inner)

    return _pcall(body, name=name, out_shape=_sds((DEPTH, W_GRP), F32), pin=False)(c_lb, dlbs)


def _ffn_bwd(hg, hv, dx, w_down, cwf, cbf, *, name):
    s, n = hg.shape
    t = _pick(s, (256, 128))
    cw = _pick(n, (1408, 256, 128))
    nt = s // t
    nj = n // cw
    hr = 16

    def body(g_ref, gh_ref, v_ref, vh_ref, dx_ref, wd_ref, wg_ref, bg_ref, wv_ref, bv_ref, dg_ref, dv_ref, dwg_ref,
             dwv_ref, cg_scr, cv_scr):
        i = pl.program_id(1)
        r = nt - 1 - i

        @pl.when(i == 0)
        def _():
            cg_scr[...] = jnp.zeros_like(cg_scr)
            cv_scr[...] = jnp.zeros_like(cv_scr)

        da = lax.dot_general(dx_ref[...], wd_ref[...], (((1,), (1,)), ((), ())), preferred_element_type=F32)
        eg = jnp.concatenate([jnp.where(r == 0, 0.0, gh_ref[...].astype(F32)), g_ref[...].astype(F32)], axis=0)
        ev = jnp.concatenate([jnp.where(r == 0, 0.0, vh_ref[...].astype(F32)), v_ref[...].astype(F32)], axis=0)
        _, vjp = jax.vjp(functools.partial(_ffn_tile, halo=hr), eg, ev, wg_ref[...], bg_ref[...], wv_ref[...],
                         bv_ref[...])
        deg, dev, dwg, dbg, dwv, dbv = vjp(da)
        for dext, scr, ref in ((deg, cg_scr, dg_ref), (dev, cv_scr, dv_ref)):
            dmain = dext[hr:]
            ref[...] = jnp.concatenate([dmain[:t - hr], dmain[t - hr:] + scr[...]], axis=0).astype(BF16)
            scr[...] = dext[:hr]
        zeros = jnp.zeros((4, cw), F32)
        _acc_out(dwg_ref, jnp.concatenate([dwg, dbg, zeros], axis=0), i == 0)
        _acc_out(dwv_ref, jnp.concatenate([dwv, dbv, zeros], axis=0), i == 0)

    main = pl.BlockSpec((t, cw), lambda j, i: (nt - 1 - i, j))
    halo = pl.BlockSpec((hr, cw), lambda j, i: (jnp.maximum((nt - 1 - i) * (t // hr) - 1, 0), j))
    taps = lambda off: _spec(cwf, (3, cw), lambda j, i: (0, j + off))
    bias = lambda off: _spec(cbf, (1, cw), lambda j, i: (0, j + off))
    w8 = pl.BlockSpec((8, cw), lambda j, i: (0, j))
    d = dx.shape[1]
    in_specs = [main, halo, main, halo, pl.BlockSpec((t, d), lambda j, i: (nt - 1 - i, 0)),
                _spec(w_down, (cw, d), lambda j, i: (j, 0)), taps(0), bias(0), taps(nj), bias(nj)]
    return _pcall(body, name=name,
                  out_shape=(_sds((s, n), BF16), _sds((s, n), BF16), _sds((8, n), F32), _sds((8, n), F32)),
                  grid=(nj, nt), in_specs=in_specs, out_specs=(main, main, w8, w8),
                  scratch_shapes=[pltpu.VMEM((hr, cw), F32), pltpu.VMEM((hr, cw), F32)],
                  semantics=("parallel", "arbitrary"),
                  block_bytes=24 * _nbytes((t, cw), F32) + _nbytes((cw, d), BF16))(
                      hg, hg, hv, hv, dx, _arr(w_down), _arr(cwf), _arr(cbf), _arr(cwf), _arr(cbf))


def _all_gather(x, *, name):
    r, c = x.shape

    def body(x_ref, out_ref, send_sems, recv_sems, local_sem):
        mx, my, mc = lax.axis_index("x"), lax.axis_index("y"), lax.axis_index("c")
        me, sibling = (mx, my, mc), (mx, my, 1 - mc)
        chips = [(1 - mx, my), (mx, 1 - my), (1 - mx, 1 - my)]

        def slot(px, py, pc):
            return out_ref.at[4 * px + 2 * py + pc]

        def copy(k, block, to, src=None):
            return pltpu.make_async_remote_copy(src_ref=slot(*block) if src is None else src, dst_ref=slot(*block),
                                                send_sem=send_sems.at[k], recv_sem=recv_sems.at[k],
                                                device_id=to, device_id_type=MESH)

        mine = pltpu.make_async_copy(x_ref, slot(*me), local_sem)
        mine.start()
        first = [copy(0, me, sibling, src=x_ref)]
        first += [copy(1 + j, me, (*chip, mc), src=x_ref) for j, chip in enumerate(chips)]
        for cp in first:
            cp.start()
        passed = [copy(4 + j, (*chip, mc), sibling) for j, chip in enumerate(chips)]
        for j, chip in enumerate(chips):
            copy(1 + j, (*chip, mc), me).wait_recv()
            passed[j].start()
        copy(0, sibling, me).wait_recv()
        for j, chip in enumerate(chips):
            copy(4 + j, (*chip, 1 - mc), me).wait_recv()
        for cp in first + passed:
            cp.wait_send()
        mine.wait()

    hbm = pl.BlockSpec(memory_space=pl.ANY)
    return _pcall(body, name=name, out_shape=_sds((N_DEV, r, c), x.dtype), in_specs=[hbm], out_specs=hbm,
                  scratch_shapes=[pltpu.SemaphoreType.DMA((7,)), pltpu.SemaphoreType.DMA((7,)),
                                  pltpu.SemaphoreType.DMA(())])(x)


def _sum_slots(p, *, name):
    q, r, c = p.shape
    tr = _pick(r, (544, 408, 272, 192, 136, 64, 32, 16, 8))

    def body(p_ref, o_ref):
        acc = p_ref[0].astype(F32)
        for k in range(1, q):
            acc = acc + p_ref[k].astype(F32)
        o_ref[...] = acc

    return _pcall(body, name=name, out_shape=_sds((r, c), F32), grid=(r // tr,),
                  in_specs=[pl.BlockSpec((q, tr, c), lambda i: (0, i, 0))],
                  out_specs=pl.BlockSpec((tr, c), lambda i: (i, 0)), semantics=("parallel",),
                  block_bytes=(q + 2) * _nbytes((tr, c), F32))(p)


BIG_COMM = (('w_in', 288, D_MODEL), ('w_out', 128, D_MODEL), ('w_up', 704, D_MODEL), ('w_down', 352, D_MODEL),
            ('w_pe', 128, PLE_DIM), ('w_pg', 128, D_MODEL))
HBM_SPEC = pl.BlockSpec(memory_space=pl.ANY)


def _gather_layer(shards, l, *, name):
    na = len(shards)

    def body(*refs):
        x_refs, out_refs = refs[:na], refs[na:2 * na]
        send_sems, recv_sems, local_sems = refs[2 * na:]
        mx, my, mc = lax.axis_index("x"), lax.axis_index("y"), lax.axis_index("c")
        me, sibling = (mx, my, mc), (mx, my, 1 - mc)
        chips = [(1 - mx, my), (mx, 1 - my), (1 - mx, 1 - my)]

        def slot(a, px, py, pc):
            return out_refs[a].at[4 * px + 2 * py + pc]

        def copy(k, a, block, to, own=False):
            return pltpu.make_async_remote_copy(src_ref=x_refs[a].at[l] if own else slot(a, *block),
                                                dst_ref=slot(a, *block), send_sem=send_sems.at[k, a],
                                                recv_sem=recv_sems.at[k, a], device_id=to, device_id_type=MESH)

        mine = [pltpu.make_async_copy(x_refs[a].at[l], slot(a, *me), local_sems.at[a]) for a in range(na)]
        for cp in mine:
            cp.start()
        first = []
        for a in range(na):
            first.append(copy(0, a, me, sibling, own=True))
            first += [copy(1 + j, a, me, (*chip, mc), own=True) for j, chip in enumerate(chips)]
        for cp in first:
            cp.start()
        passed = []
        for j, chip in enumerate(chips):
            for a in range(na):
                copy(1 + j, a, (*chip, mc), me).wait_recv()
                fwd = copy(4 + j, a, (*chip, mc), sibling)
                fwd.start()
                passed.append(fwd)
        for a in range(na):
            copy(0, a, sibling, me).wait_recv()
        for j, chip in enumerate(chips):
            for a in range(na):
                copy(4 + j, a, (*chip, 1 - mc), me).wait_recv()
        for cp in first + passed:
            cp.wait_send()
        for cp in mine:
            cp.wait()

    return _pcall(body, name=name, out_shape=tuple(_sds((N_DEV,) + x.shape[1:], x.dtype) for x in shards),
                  in_specs=[HBM_SPEC] * na, out_specs=(HBM_SPEC,) * na,
                  scratch_shapes=[pltpu.SemaphoreType.DMA((7, na)), pltpu.SemaphoreType.DMA((7, na)),
                                  pltpu.SemaphoreType.DMA((na,))])(*shards)


SEM_SPEC = pl.BlockSpec(memory_space=pltpu.SEMAPHORE)
DATAFLOW_EFFECT = pltpu.SideEffectType.DATAFLOW_SIDE_EFFECTING


def _place_own(srcs, after, *, name):
    na = len(srcs)

    def body(*refs):
        x_refs, land_refs, sems = refs[:na], refs[na + len(after):2 * na + len(after)], refs[-1]
        me = 4 * lax.axis_index("x") + 2 * lax.axis_index("y") + lax.axis_index("c")
        cps = [pltpu.make_async_copy(x_refs[a], land_refs[a].at[me], sems.at[a]) for a in range(na)]
        for cp in cps:
            cp.start()
        for cp in cps:
            cp.wait()

    return _pcall(body, name=name, out_shape=tuple(_sds((N_DEV,) + x.shape, x.dtype) for x in srcs),
                  in_specs=[HBM_SPEC] * (na + len(after)), out_specs=(HBM_SPEC,) * na,
                  scratch_shapes=[pltpu.SemaphoreType.DMA((na,))], pin=False)(*srcs, *after)


def _exchange_start(srcs, lands, *, name, per_peer=False):
    na = len(srcs)

    def body(*refs):
        x_refs, land_refs = refs[:na], refs[na:2 * na]
        send_sems, recv_sems = refs[2 * na], refs[2 * na + 1]
        token = refs[-1]
        mx, my, mc = lax.axis_index("x"), lax.axis_index("y"), lax.axis_index("c")
        me = 4 * mx + 2 * my + mc
        peers = [(mx, my, 1 - mc)]
        for px, py in ((1 - mx, my), (mx, 1 - my), (1 - mx, 1 - my)):
            peers += [(px, py, mc), (px, py, 1 - mc)]
        for a in range(na):
            for peer in peers:
                src = x_refs[a].at[4 * peer[0] + 2 * peer[1] + peer[2]] if per_peer else x_refs[a]
                pltpu.make_async_remote_copy(src_ref=src, dst_ref=land_refs[a].at[me], send_sem=send_sems.at[a],
                                             recv_sem=recv_sems.at[a], device_id=peer, device_id_type=MESH).start()
        token[...] = jnp.zeros_like(token)

    hbm = lambda x: pltpu.HBM(x.shape, x.dtype)
    out_shape = ((pltpu.SemaphoreType.DMA((na,)), pltpu.SemaphoreType.DMA((na,))) + tuple(hbm(x) for x in srcs)
                 + tuple(hbm(x) for x in lands) + (_sds((8, 128), F32),))
    params = pltpu.CompilerParams(has_side_effects=DATAFLOW_EFFECT)
    pin = lambda x: pltpu.with_memory_space_constraint(x, pltpu.HBM)
    return pl.pallas_call(body, name=name, out_shape=out_shape, in_specs=[HBM_SPEC] * (2 * na),
                          out_specs=(SEM_SPEC, SEM_SPEC) + (HBM_SPEC,) * (2 * na) + (pl.BlockSpec(memory_space=pltpu.VMEM),),
                          input_output_aliases={i: 2 + i for i in range(2 * na)}, compiler_params=params)(
                              *[pin(x) for x in srcs], *[pin(x) for x in lands])


def _exchange_wait(started, after, *, name):
    send_sems, recv_sems, *bufs, _ = started
    na = len(bufs) // 2

    def body(*refs):
        land_refs = refs[na:2 * na]
        s_sems, r_sems = refs[2 * na], refs[2 * na + 1]
        me = (lax.axis_index("x"), lax.axis_index("y"), lax.axis_index("c"))
        for a in range(na):
            seven = land_refs[a].at[pl.ds(0, N_DEV - 1)]
            cp = pltpu.make_async_remote_copy(src_ref=seven, dst_ref=seven, send_sem=s_sems.at[a], recv_sem=r_sems.at[a],
                                              device_id=me, device_id_type=MESH)
            cp.wait_send()
            cp.wait_recv()

    hbm = lambda x: pltpu.HBM(x.shape, x.dtype)
    params = pltpu.CompilerParams(has_side_effects=DATAFLOW_EFFECT)
    outs = pl.pallas_call(body, name=name, out_shape=tuple(hbm(x) for x in bufs),
                          in_specs=[HBM_SPEC] * (2 * na) + [SEM_SPEC, SEM_SPEC, HBM_SPEC],
                          out_specs=(HBM_SPEC,) * (2 * na), input_output_aliases={i: i for i in range(2 * na)},
                          compiler_params=params)(*bufs, send_sems, recv_sems, after)
    return outs[:na], outs[na:]


def _pair_swap(grads, *, name):
    na = len(grads)

    def body(*refs):
        g_refs, recv_refs = refs[:na], refs[na:2 * na]
        send_sems, recv_sems = refs[2 * na:]
        mx, my, mc = lax.axis_index("x"), lax.axis_index("y"), lax.axis_index("c")
        sibling = (mx, my, 1 - mc)
        for a in range(na):
            for q in range(4):
                pltpu.make_async_remote_copy(src_ref=g_refs[a].at[q, 1 - mc], dst_ref=recv_refs[a].at[q],
                                             send_sem=send_sems.at[a], recv_sem=recv_sems.at[a],
                                             device_id=sibling, device_id_type=MESH).start()
        for a in range(na):
            pltpu.make_async_remote_copy(src_ref=recv_refs[a], dst_ref=recv_refs[a], send_sem=send_sems.at[a],
                                         recv_sem=recv_sems.at[a], device_id=sibling, device_id_type=MESH).wait()

    half = tuple(_sds((4,) + g.shape[2:], g.dtype) for g in grads)
    return _pcall(body, name=name, out_shape=half, in_specs=[HBM_SPEC] * na, out_specs=(HBM_SPEC,) * na,
                  scratch_shapes=[pltpu.SemaphoreType.DMA((na,)), pltpu.SemaphoreType.DMA((na,))])(*grads)


def _add_slabs(grads, recv, core, *, name):
    na = len(grads)

    def body(core_ref, *refs):
        for a in range(na):
            refs[2 * na + a][...] = (refs[a][...].astype(F32) + refs[na + a][...].astype(F32)).astype(BF16)

    own_specs = [pl.BlockSpec((None, None) + x.shape[2:], lambda q, core_ref: (q, core_ref[0], 0, 0)) for x in grads]
    specs = [pl.BlockSpec((None,) + x.shape[1:], lambda q, core_ref: (q, 0, 0)) for x in recv]
    blk = sum(_nbytes(x.shape[1:], F32) for x in recv)
    grid_spec = pltpu.PrefetchScalarGridSpec(num_scalar_prefetch=1, grid=(4,), in_specs=own_specs + specs,
                                             out_specs=tuple(specs))
    params = pltpu.CompilerParams(dimension_semantics=("parallel",), vmem_limit_bytes=_vmem_limit(2 * blk))
    return pl.pallas_call(body, name=name, out_shape=tuple(_sds(x.shape, BF16) for x in recv), grid_spec=grid_spec,
                          compiler_params=params)(core, *grads, *recv)


def _chip_exchange(parts, *, name):
    na = len(parts)

    def body(*refs):
        p_refs, out_refs = refs[:na], refs[na:2 * na]
        send_sems, recv_sems, local_sems = refs[2 * na:]
        mx, my, mc = lax.axis_index("x"), lax.axis_index("y"), lax.axis_index("c")
        mine_q = 2 * mx + my
        chips = [(1 - mx, my), (mx, 1 - my), (1 - mx, 1 - my)]
        owns = [pltpu.make_async_copy(p_refs[a].at[mine_q], out_refs[a].at[mine_q], local_sems.at[a]) for a in range(na)]
        for cp in owns:
            cp.start()
        sends = []
        for a in range(na):
            for k, chip in enumerate(chips):
                sends.append(pltpu.make_async_remote_copy(
                    src_ref=p_refs[a].at[2 * chip[0] + chip[1]], dst_ref=out_refs[a].at[mine_q],
                    send_sem=send_sems.at[k, a], recv_sem=recv_sems.at[k, a], device_id=(*chip, mc), device_id_type=MESH))
        for cp in sends:
            cp.start()
        for a in range(na):
            for k, chip in enumerate(chips):
                pltpu.make_async_remote_copy(
                    src_ref=p_refs[a].at[mine_q], dst_ref=out_refs[a].at[2 * chip[0] + chip[1]],
                    send_sem=send_sems.at[k, a], recv_sem=recv_sems.at[k, a], device_id=(*chip, mc),
                    device_id_type=MESH).wait_recv()
        for cp in sends:
            cp.wait_send()
        for cp in owns:
            cp.wait()

    return _pcall(body, name=name, out_shape=tuple(_sds(x.shape, x.dtype) for x in parts), in_specs=[HBM_SPEC] * na,
                  out_specs=(HBM_SPEC,) * na,
                  scratch_shapes=[pltpu.SemaphoreType.DMA((3, na)), pltpu.SemaphoreType.DMA((3, na)),
                                  pltpu.SemaphoreType.DMA((na,))])(*parts)


def _sum_chips(parts, *, name):
    na = len(parts)

    def body(*refs):
        for a in range(na):
            p_ref = refs[a]
            acc = p_ref[0].astype(F32)
            for k in range(1, p_ref.shape[0]):
                acc = acc + p_ref[k].astype(F32)
            refs[na + a][...] = acc

    half = lambda x: x.shape[1] // 2
    in_specs = [pl.BlockSpec((x.shape[0], half(x), x.shape[2]), lambda i: (0, i, 0)) for x in parts]
    out_specs = tuple(pl.BlockSpec((half(x), x.shape[2]), lambda i: (i, 0)) for x in parts)
    blk = sum(_nbytes((x.shape[0] + 2, half(x), x.shape[2]), BF16) for x in parts)
    return _pcall(body, name=name, out_shape=tuple(_sds(x.shape[1:], F32) for x in parts), grid=(2,),
                  in_specs=in_specs, out_specs=out_specs, semantics=("parallel",), block_bytes=blk)(*parts)


def _sum_devices(lands, own, me, *, name):
    na = len(lands)

    def body(me_ref, *refs):
        mine = me_ref[0]
        for a in range(na):
            l_ref, o_ref = refs[a], refs[na + a]
            acc = None
            for k in range(N_DEV):
                term = jnp.where(mine == k, o_ref[...], l_ref[k]).astype(F32)
                acc = term if acc is None else acc + term
            refs[2 * na + a][...] = acc

    half = lambda x: x.shape[1] // 2
    land_specs = [pl.BlockSpec((N_DEV, half(x), x.shape[2]), lambda i, me_ref: (0, i, 0)) for x in lands]
    own_specs = [pl.BlockSpec((None, half(x), x.shape[2]), lambda i, me_ref: (me_ref[0], i, 0)) for x in lands]
    out_specs = tuple(pl.BlockSpec((half(x), x.shape[2]), lambda i, me_ref: (i, 0)) for x in lands)
    blk = sum(_nbytes((N_DEV + 3, half(x), x.shape[2]), BF16) for x in lands)
    grid_spec = pltpu.PrefetchScalarGridSpec(num_scalar_prefetch=1, grid=(2,), in_specs=land_specs + own_specs,
                                             out_specs=out_specs)
    params = pltpu.CompilerParams(dimension_semantics=("parallel",), vmem_limit_bytes=_vmem_limit(blk))
    return pl.pallas_call(body, name=name, out_shape=tuple(_sds(x.shape[1:], F32) for x in lands), grid_spec=grid_spec,
                          compiler_params=params)(me, *lands, *own)


def _reduce_layer(grads, l):
    n = lambda s: f"l{l}_{s}"
    views = [g.reshape(4, 2, g.shape[0] // N_DEV, g.shape[1]) for g in grads]
    recv = _pair_swap(views, name=n("reduce_pair"))
    core = lax.axis_index("c").astype(jnp.int32).reshape(1)
    chip_sum = _add_slabs(views, recv, core, name=n("reduce_pair_add"))
    from_chips = _chip_exchange(chip_sum, name=n("reduce_chips"))
    return _sum_chips(from_chips, name=n("reduce_chips_add"))


def _adamw(w, g, m, v, *, name):
    lead, (r, c) = w.shape[:-2], w.shape[-2:]
    tr = _pick(r, (512, 352, 288, 256, 192, 128, 64, 32, 16, 8))
    c1 = 1.0 / (1.0 - ADAM_B1 ** ADAM_STEP)
    c2 = 1.0 / (1.0 - ADAM_B2 ** ADAM_STEP)

    def body(w_ref, g_ref, m_ref, v_ref, d_ref, nm_ref, nv_ref):
        gv = g_ref[...]
        nm = ADAM_B1 * m_ref[...] + (1.0 - ADAM_B1) * gv
        nv = ADAM_B2 * v_ref[...] + (1.0 - ADAM_B2) * jnp.square(gv)
        d_ref[...] = -ADAM_LR * ((nm * c1) / (jnp.sqrt(nv * c2) + ADAM_EPS) + ADAM_WD * w_ref[...])
        nm_ref[...] = nm
        nv_ref[...] = nv

    if lead:
        blk = pl.BlockSpec((None, tr, c), lambda k, i: (k, i, 0))
        grid, sem = (lead[0], r // tr), ("parallel", "parallel")
    else:
        blk = pl.BlockSpec((tr, c), lambda i: (i, 0))
        grid, sem = (r // tr,), ("parallel",)
    out = _sds(w.shape, F32)
    return _pcall(body, name=name, out_shape=(out, out, out), grid=grid, in_specs=[blk] * 4,
                  out_specs=(blk, blk, blk), semantics=sem, block_bytes=7 * _nbytes((tr, c), F32))(w, g, m, v)


def _pack_flat(arrs, rows, cols=1024):
    flat = jnp.concatenate([a.reshape(-1).astype(F32) for a in arrs])
    pad = rows * cols - flat.shape[0]
    return jnp.pad(flat, (0, pad)).reshape(rows, cols)


def _unpack_flat(buf, shapes):
    flat = buf.reshape(-1)
    out, off = [], 0
    for shp in shapes:
        n = 1
        for s in shp:
            n *= s
        out.append(flat[off:off + n].reshape(shp))
        off += n
    return out


def _flat_rows(shapes, cols=1024):
    n = sum(functools.reduce(lambda a, b: a * b, shp, 1) for shp in shapes)
    rows = -(-n // cols)
    return -(-rows // 64) * 64


def _block_diag(w):
    eye = jnp.eye(N_HEADS, dtype=w.dtype)
    return (w[:, :, :, None, :] * eye[None, :, None, :, None]).reshape(w.shape[0], W_GRP, W_GRP)


def _diag_blocks(w):
    w5 = w.reshape(w.shape[0], N_HEADS, HEAD_DIM, N_HEADS, HEAD_DIM)
    return jnp.stack([w5[:, h, :, h, :] for h in range(N_HEADS)], axis=1)


def _stacked_params(w, lbs):
    tril = jnp.tril(jnp.ones((GMLP_CHUNK, GMLP_CHUNK), bool))
    row = lambda a: a.reshape(DEPTH, 1, -1)
    return dict(
        g1=row(w['norm1_g']), g2=row(w['norm2_g']), g3=row(w['norm3_g']),
        a_ln_g=row(w['a_ln_g']), a_ln_b=row(w['a_ln_b']),
        a_wcat=jnp.where(tril, w['a_ws'], 0.0).reshape(DEPTH, N_HEADS * GMLP_CHUNK, GMLP_CHUNK),
        a_bfull=jnp.repeat(jnp.swapaxes(w['a_bs'], 1, 2), HEAD_DIM, axis=2),
        b_cw=w['b_conv_w_full'], b_cb=row(w['b_conv_b']), b_wa=_block_diag(w['b_wa']), b_ba=row(w['b_ba']),
        b_wx=_block_diag(w['b_wx']), b_bx=row(w['b_bx']), b_lam=row(w['b_lam']),
        c_lb=row(lbs), c_ngf=row(jnp.tile(w['c_norm_g'], (1, N_HEADS))),
        d_wd=_block_diag(w['d_w']), d_scale=row(w['d_scale']),
        f_cw=w['ffn_conv_w_full'], f_cb=row(w['ffn_conv_b']),
    )


B_PRM = ('b_cw', 'b_cb', 'b_wa', 'b_ba', 'b_wx', 'b_bx', 'b_lam')


def _layer_fwd(x, p_bf, wb, sp, l):
    n = lambda s: f"l{l}_{s}"
    h, (z,) = _rms_matmul(x, sp['g1'], [wb['w_in']], nt=True, name=n("proj_in"))
    mix = _gmlp_fwd(z, sp['a_ln_g'], sp['a_ln_b'], sp['a_wcat'], sp['a_bfull'], name=n("gmlp"))
    mix, h0s = _rglru_fwd(z, [sp[k] for k in B_PRM], mix, name=n("rglru"))
    mix, sts = _hgrn_fwd(z, sp['c_lb'], sp['c_ngf'], mix, name=n("hgrn"))
    mix = _pool_fwd(z, sp['d_wd'], sp['d_scale'], mix, name=n("pool"))
    x1 = _matmul(mix, wb['w_out'], res=x, name=n("proj_out"))
    h2, hg, hv, a = _up_ffn_fwd(x1, sp['g2'], wb['w_up_g'], wb['w_up_v'], sp['f_cw'], sp['f_cb'], name=n("up_ffn"))
    x2 = _matmul(a, wb['w_down'], res=x1, name=n("down"))
    h3, (gl, pe, x3) = _rms_matmul(x2, sp['g3'], [wb['w_pg']], ple=(p_bf, wb['w_pe']), name=n("ple"))
    saved = dict(x=x, h=h, z=z, h0s=h0s, sts=sts, mix=mix, x1=x1, h2=h2, hg=hg, hv=hv, a=a, x2=x2, h3=h3, gl=gl, pe=pe)
    return x3, saved


def _layer_bwd(dx3, sv, p_bf, wb, sp, l, mid=None):
    n = lambda s: f"l{l}_{s}_bwd"
    gb, gs = {}, {}
    dpe, dgl = _ple_bwd(dx3, sv['gl'], sv['pe'], name=n("ple"))
    gb['w_pe'] = _matmul_tn(dpe, p_bf, name=n("ple_emb_w"))
    gb['w_pg'] = _matmul_tn(sv['h3'], dgl, name=n("ple_gate_w"))
    dx2, dx2b, gs['norm3_g'] = _matmul_rms_bwd(dgl, wb['w_pg'], sv['x2'], sp['g3'], dx3, nt=True, name=n("ple_gate_x"))
    gb['w_down'] = _matmul_tn(sv['a'], dx2b, name=n("down_w"))
    dhg, dhv, gs['f_dwg'], gs['f_dwv'] = _ffn_bwd(sv['hg'], sv['hv'], dx2b, wb['w_down'], sp['f_cw'], sp['f_cb'],
                                                  name=n("ffn_gate"))
    gate_rows = _matmul_tn(dhg, sv['h2'], name=n("up_gate_w"), out_rows=2 * D_FF)
    gb['w_up'] = _matmul_tn(dhv, sv['h2'], name=n("up_val_w"), out_rows=2 * D_FF, row_off=D_FF, into=gate_rows)
    if mid is not None:
        sp = mid(gb, sp)
    dh2 = _matmul(dhg, wb['w_up_g'], name=n("up_gate_x"))
    dx1, dx1b, gs['norm2_g'] = _matmul_rms_bwd(dhv, wb['w_up_v'], sv['x1'], sp['g2'], dx2, res=dh2, name=n("up_val_x"))
    dmix = _matmul(dx1b, wb['w_out'], nt=True, name=n("proj_out_x"))
    gb['w_out'] = _matmul_tn(sv['mix'], dx1b, name=n("proj_out_w"))
    z = sv['z']
    dz, gs['a_ln_g'], gs['a_ln_b'], gs['a_wcat'], gs['a_bfull'] = _gmlp_bwd(
        z, dmix, sp['a_ln_g'], sp['a_ln_b'], sp['a_wcat'], sp['a_bfull'], name=n("gmlp"))
    dz, *dbp = _rglru_bwd(z, dmix, sv['h0s'], [sp[k] for k in B_PRM], dz, name=n("rglru"))
    gs.update(zip(B_PRM, dbp))
    dz, gs['c_lb'], gs['c_ngf'] = _hgrn_bwd(z, dmix, sv['sts'], sp['c_lb'], sp['c_ngf'], dz, name=n("hgrn"))
    dz, gs['d_wd'], gs['d_scale'] = _pool_bwd(z, dmix, sp['d_wd'], sp['d_scale'], dz, name=n("pool"))
    gb['w_in'] = _matmul_tn(dz, sv['h'], name=n("proj_in_w"))
    dx0, _, gs['norm1_g'] = _matmul_rms_bwd(dz, wb['w_in'], sv['x'], sp['g1'], dx1, name=n("proj_in_x"))
    return dx0, gb, gs


SMALL_NAMES = [nm for nm in WEIGHT_NAMES if nm not in BIG_NAMES]
COL_SHARDED = ('w_in', 'w_up', 'w_pe')


def _comm_shards(w):
    return [(jnp.swapaxes(w[nm], 1, 2) if nm in COL_SHARDED else w[nm]).astype(BF16) for nm, _, _ in BIG_COMM]


def _full_weights(gathered):
    out = {nm: g.reshape(N_DEV * r, c) for g, (nm, r, c) in zip(gathered, BIG_COMM)}
    halves = out.pop('w_up').reshape(2, D_FF, D_MODEL)
    out['w_up_g'], out['w_up_v'] = _Sel(halves, 0), _Sel(halves, 1)
    return out


def _small_grads(raw):
    nl = len(raw)
    st = {k: jnp.stack([r[k] for r in raw]) for k in raw[0]}
    tril = jnp.tril(jnp.ones((GMLP_CHUNK, GMLP_CHUNK), bool))
    vec = lambda a: a.reshape(nl, -1)
    out = {nm: vec(st[k]) for nm, k in (('norm1_g', 'norm1_g'), ('norm2_g', 'norm2_g'), ('norm3_g', 'norm3_g'),
                                        ('a_ln_g', 'a_ln_g'), ('a_ln_b', 'a_ln_b'), ('b_conv_b', 'b_cb'),
                                        ('b_ba', 'b_ba'), ('b_bx', 'b_bx'), ('b_lam', 'b_lam'), ('c_lb', 'c_lb'),
                                        ('d_scale', 'd_scale'))}
    out['a_ws'] = jnp.where(tril, st['a_wcat'].reshape(nl, N_HEADS, GMLP_CHUNK, GMLP_CHUNK), 0.0)
    out['a_bs'] = jnp.swapaxes(st['a_bfull'].reshape(nl, GMLP_CHUNK, N_HEADS, HEAD_DIM).sum(-1), 1, 2)
    out['b_conv_w'] = st['b_cw']
    out['b_wa'], out['b_wx'], out['d_w'] = _diag_blocks(st['b_wa']), _diag_blocks(st['b_wx']), _diag_blocks(st['d_wd'])
    out['c_norm_g'] = st['c_ngf'].reshape(nl, N_HEADS, HEAD_DIM).sum(1)
    out['ffn_conv_w'] = jnp.concatenate([st['f_dwg'][:, 0:3], st['f_dwv'][:, 0:3]], axis=2)
    out['ffn_conv_b'] = jnp.concatenate([st['f_dwg'][:, 3], st['f_dwv'][:, 3]], axis=1)
    return out


def _step(w, m, v, x, p, target):
    s = x.shape[1]
    dev = 4 * lax.axis_index("x") + 2 * lax.axis_index("y") + lax.axis_index("c")
    xs = x.reshape(s, D_MODEL)

    shards = _comm_shards(w)
    conv_shapes = [w['b_conv_w'].shape, w['ffn_conv_w'].shape]
    conv_rows = _flat_rows(conv_shapes)
    conv_all = _all_gather(_pack_flat([w['b_conv_w'], w['ffn_conv_w']], conv_rows), name="gather_conv_weights")
    parts = [_unpack_flat(conv_all[d], conv_shapes) for d in range(N_DEV)]
    wf = dict(w)
    wf['b_conv_w_full'] = jnp.concatenate([pt[0] for pt in parts], axis=-1)
    wf['ffn_conv_w_full'] = jnp.concatenate([pt[1] for pt in parts], axis=-1)
    lbs = _lbs_fwd(w['c_lb'], name="hgrn_bounds")

    stacked = _stacked_params(wf, lbs)
    p_all = p.reshape(DEPTH, s, PLE_DIM).astype(BF16)
    xl, saved, wbs, sps = xs, [], [], []
    gathered = _gather_layer(shards, 0, name="l0_gather_weights")
    for l in range(DEPTH):
        sp = {k: _Sel(a, l) for k, a in stacked.items()}
        if l + 1 < DEPTH:
            own = [x[l + 1] for x in shards]
            after = [conv_all, *gathered] if l == 0 else [xl]
            lands = _place_own(own, after, name=f"l{l + 1}_gather_place")
            started = _exchange_start(own, lands, name=f"l{l + 1}_gather_start")
            sp['g1'] = stacked['g1'][l] + started[-1][0, 0]
        wb = _full_weights(gathered)
        p_bf = p_all[l]
        xl, sv = _layer_fwd(xl, p_bf, wb, sp, l)
        if l + 1 < DEPTH:
            gathered = _exchange_wait(started, xl, name=f"l{l + 1}_gather_wait")[1]
        saved.append((sv, p_bf))
        wbs.append(wb)
        sps.append(sp)
    loss_part, dx, dfinal = _loss_head(xl, w['final_g'].reshape(1, D_MODEL), target.reshape(s, D_MODEL), name="loss_head")
    loss = lax.psum(loss_part[0, 0], ("x", "y", "c"))

    dev1 = dev.astype(jnp.int32).reshape(1)
    names = [nm for nm, _, _ in BIG_COMM]

    def start_reduce(grads, name):
        views = [g.reshape(N_DEV, g.shape[0] // N_DEV, g.shape[1]) for g in grads]
        return _exchange_start(views, [lax.empty(g.shape, g.dtype) for g in views], name=name, per_peer=True)

    def finish_reduce(started, after, lname):
        own, lands = _exchange_wait(started, after, name=f"{lname}_reduce_wait")
        return _sum_devices(lands, own, dev1, name=f"{lname}_reduce_sum")

    reduced, small = [None] * DEPTH, [None] * DEPTH
    pending = None
    for l in range(DEPTH - 1, 0, -1):
        sv, p_bf = saved[l]
        sp = sps[l]
        if pending is not None:
            sp = dict(sp, g3=stacked['g3'][l] + pending[-1][0, 0])
        dx, gb, small[l] = _layer_bwd(dx, sv, p_bf, wbs[l], sp, l)
        if pending is not None:
            reduced[l + 1] = finish_reduce(pending, dx, f"l{l + 1}")
        pending = start_reduce([gb[nm] for nm in names], f"l{l}_reduce_start")
    early = ('w_up', 'w_down', 'w_pe', 'w_pg')
    mid_started = []

    def mid(gb, sp):
        mid_started.append(start_reduce([gb[nm] for nm in early], "l0_reduce_start"))
        return dict(sp, g2=stacked['g2'][0] + mid_started[0][-1][0, 0])

    upper_names = [nm for nm in SMALL_NAMES if nm != 'final_g']
    low_names = upper_names + ['final_g']
    upper = _small_grads(small[1:])
    upper_shapes = [upper[nm].shape for nm in upper_names]
    upper_packed = [_pack_flat([upper[nm] for nm in upper_names], _flat_rows(upper_shapes))]
    upper_started = _exchange_start(upper_packed, _place_own(upper_packed, [], name="upper_small_grads_place"),
                                    name="upper_small_grads_start")

    sv, p_bf = saved[0]
    g3 = stacked['g3'][0] + pending[-1][0, 0] + upper_started[-1][0, 0]
    dx, gb, small[0] = _layer_bwd(dx, sv, p_bf, wbs[0], dict(sps[0], g3=g3), 0, mid=mid)
    reduced[1] = finish_reduce(pending, dx, "l1")
    late = dict(zip(('w_in', 'w_out'), _reduce_layer([gb['w_in'], gb['w_out']], 0)))
    late.update(zip(early, finish_reduce(mid_started[0], late['w_in'], "l0")))
    reduced[0] = [late[nm] for nm in names]
    grad_x = dx.reshape(1, s, D_MODEL)
    low = _small_grads(small[:1])
    low['final_g'] = dfinal.reshape(D_MODEL)
    low_shapes = [low[nm].shape for nm in low_names]
    low_all = _all_gather(_pack_flat([low[nm] for nm in low_names], _flat_rows(low_shapes)), name="gather_small_grads")
    low_sum = dict(zip(low_names, _unpack_flat(_sum_slots(low_all, name="sum_small_grads"), low_shapes)))
    upper_all = _exchange_wait(upper_started, low_all, name="upper_small_grads_wait")[1][0]
    upper_sum = dict(zip(upper_names, _unpack_flat(_sum_slots(upper_all, name="sum_upper_small_grads"), upper_shapes)))
    gsmall = {nm: jnp.concatenate([low_sum[nm], upper_sum[nm]], axis=0) for nm in upper_names}
    gsmall['c_lb'] = _lbs_bwd(w['c_lb'], gsmall['c_lb'], name="hgrn_bounds_bwd")
    gsmall['final_g'] = low_sum['final_g']
    for nm in ('b_conv_w', 'ffn_conv_w'):
        width = w[nm].shape[-1]
        gsmall[nm] = lax.dynamic_slice_in_dim(gsmall[nm], dev * width, width, axis=2)

    grads, delta, new_m, new_v = {}, {}, {}, {}
    for a, (nm, _, _) in enumerate(BIG_COMM):
        t = (lambda x: jnp.swapaxes(x, 1, 2)) if nm in COL_SHARDED else (lambda x: x)
        g = jnp.stack([reduced[l][a] for l in range(DEPTH)])
        d, nm_, nv_ = _adamw(t(w[nm]), g, t(m[nm]), t(v[nm]), name=f"adamw_{nm}")
        grads[nm], delta[nm], new_m[nm], new_v[nm] = t(g), t(d), t(nm_), t(nv_)

    shapes = [w[nm].shape for nm in SMALL_NAMES]
    rows = _flat_rows(shapes)
    pk = lambda t: _pack_flat([t[nm] for nm in SMALL_NAMES], rows)
    d, nm_, nv_ = _adamw(pk(w), pk(gsmall), pk(m), pk(v), name="adamw_small")
    for nm, dd, mm_, vv_ in zip(SMALL_NAMES, _unpack_flat(d, shapes), _unpack_flat(nm_, shapes), _unpack_flat(nv_, shapes)):
        grads[nm], delta[nm], new_m[nm], new_v[nm] = gsmall[nm], dd, mm_, vv_

    return (loss, grad_x, *[grads[nm] for nm in WEIGHT_NAMES], *[delta[nm] for nm in WEIGHT_NAMES],
            *[new_m[nm] for nm in WEIGHT_NAMES], *[new_v[nm] for nm in WEIGHT_NAMES])


def kernel(x, p, norm1_g, w_in, a_ln_g, a_ln_b, a_ws, a_bs, b_conv_w, b_conv_b, b_wa, b_ba, b_wx, b_bx, b_lam, c_lb, c_norm_g, d_w, d_scale, w_out, norm2_g, w_up, ffn_conv_w, ffn_conv_b, w_down, norm3_g, w_pe, w_pg, final_g, loss_target, m_norm1_g, m_w_in, m_a_ln_g, m_a_ln_b, m_a_ws, m_a_bs, m_b_conv_w, m_b_conv_b, m_b_wa, m_b_ba, m_b_wx, m_b_bx, m_b_lam, m_c_lb, m_c_norm_g, m_d_w, m_d_scale, m_w_out, m_norm2_g, m_w_up, m_ffn_conv_w, m_ffn_conv_b, m_w_down, m_norm3_g, m_w_pe, m_w_pg, m_final_g, v_norm1_g, v_w_in, v_a_ln_g, v_a_ln_b, v_a_ws, v_a_bs, v_b_conv_w, v_b_conv_b, v_b_wa, v_b_ba, v_b_wx, v_b_bx, v_b_lam, v_c_lb, v_c_norm_g, v_d_w, v_d_scale, v_w_out, v_norm2_g, v_w_up, v_ffn_conv_w, v_ffn_conv_b, v_w_down, v_norm3_g, v_w_pe, v_w_pg, v_final_g):
    w = dict(norm1_g=norm1_g, w_in=w_in, a_ln_g=a_ln_g, a_ln_b=a_ln_b, a_ws=a_ws, a_bs=a_bs, b_conv_w=b_conv_w, b_conv_b=b_conv_b, b_wa=b_wa, b_ba=b_ba, b_wx=b_wx, b_bx=b_bx, b_lam=b_lam, c_lb=c_lb, c_norm_g=c_norm_g, d_w=d_w, d_scale=d_scale, w_out=w_out, norm2_g=norm2_g, w_up=w_up, ffn_conv_w=ffn_conv_w, ffn_conv_b=ffn_conv_b, w_down=w_down, norm3_g=norm3_g, w_pe=w_pe, w_pg=w_pg, final_g=final_g)
    m = dict(norm1_g=m_norm1_g, w_in=m_w_in, a_ln_g=m_a_ln_g, a_ln_b=m_a_ln_b, a_ws=m_a_ws, a_bs=m_a_bs, b_conv_w=m_b_conv_w, b_conv_b=m_b_conv_b, b_wa=m_b_wa, b_ba=m_b_ba, b_wx=m_b_wx, b_bx=m_b_bx, b_lam=m_b_lam, c_lb=m_c_lb, c_norm_g=m_c_norm_g, d_w=m_d_w, d_scale=m_d_scale, w_out=m_w_out, norm2_g=m_norm2_g, w_up=m_w_up, ffn_conv_w=m_ffn_conv_w, ffn_conv_b=m_ffn_conv_b, w_down=m_w_down, norm3_g=m_norm3_g, w_pe=m_w_pe, w_pg=m_w_pg, final_g=m_final_g)
    v = dict(norm1_g=v_norm1_g, w_in=v_w_in, a_ln_g=v_a_ln_g, a_ln_b=v_a_ln_b, a_ws=v_a_ws, a_bs=v_a_bs, b_conv_w=v_b_conv_w, b_conv_b=v_b_conv_b, b_wa=v_b_wa, b_ba=v_b_ba, b_wx=v_b_wx, b_bx=v_b_bx, b_lam=v_b_lam, c_lb=v_c_lb, c_norm_g=v_c_norm_g, d_w=v_d_w, d_scale=v_d_scale, w_out=v_w_out, norm2_g=v_norm2_g, w_up=v_w_up, ffn_conv_w=v_ffn_conv_w, ffn_conv_b=v_ffn_conv_b, w_down=v_w_down, norm3_g=v_norm3_g, w_pe=v_w_pe, w_pg=v_w_pg, final_g=v_final_g)
    return _step(w, m, v, x, p, loss_target)
```

```python
import functools

import jax
import jax.numpy as jnp
from jax import lax
from jax.experimental import pallas as pl
from jax.experimental.pallas import tpu as pltpu

F32 = jnp.float32
BF16 = jnp.bfloat16
MESH = pl.DeviceIdType.MESH

D_MODEL = 1024
DEPTH = 4
PLE_DIM = 256
W_GRP = 256
N_HEADS = 4
HEAD_DIM = 64
GMLP_CHUNK = 128
RGLRU_C = 8.0
HGRN_CHUNK = 64
HGRN_SUB = 16
HGRN_STEP_CHUNKS = 8
POOL_WINDOWS = (2, 4, 8, 16)
D_FF = 2816
D_PROJ = 2304
EPS = 1e-6
ADAM_LR = 0.001
ADAM_B1 = 0.9
ADAM_B2 = 0.999
ADAM_EPS = 1e-08
ADAM_WD = 0.01
ADAM_STEP = 10

N_DEV = 8
MIB = 2 ** 20
V7X_VMEM_BYTES = 64 * MIB
HGRN_EXP_CLAMP = 60.0

WEIGHT_NAMES = ['norm1_g', 'w_in', 'a_ln_g', 'a_ln_b', 'a_ws', 'a_bs', 'b_conv_w', 'b_conv_b', 'b_wa', 'b_ba', 'b_wx',
                'b_bx', 'b_lam', 'c_lb', 'c_norm_g', 'd_w', 'd_scale', 'w_out', 'norm2_g', 'w_up', 'ffn_conv_w',
                'ffn_conv_b', 'w_down', 'norm3_g', 'w_pe', 'w_pg', 'final_g']
BIG_NAMES = ('w_in', 'w_out', 'w_up', 'w_down', 'w_pe', 'w_pg')


def _vmem_limit(block_bytes):
    want = 2 * block_bytes + 24 * MIB
    return int(min(max(want, 32 * MIB), V7X_VMEM_BYTES - 8 * MIB))


def _in_hbm(x):
    return pltpu.with_memory_space_constraint(x, pltpu.HBM)


def _out_hbm(s):
    return pltpu.HBM(s.shape, s.dtype)


def _pcall(body, *, name, out_shape, grid=None, in_specs=None, out_specs=None, scratch_shapes=(),
           semantics=None, block_bytes=0, aliases=None, pin=True):
    kw = {} if aliases is None else {"input_output_aliases": aliases}
    if pin:
        out_shape = tuple(_out_hbm(s) for s in out_shape) if isinstance(out_shape, (tuple, list)) else _out_hbm(out_shape)
    if grid is not None:
        kw["grid"] = grid
    if in_specs is not None:
        kw["in_specs"] = in_specs
    if out_specs is not None:
        kw["out_specs"] = out_specs
    params = pltpu.CompilerParams(dimension_semantics=semantics, vmem_limit_bytes=_vmem_limit(block_bytes))
    call = pl.pallas_call(body, name=name, out_shape=out_shape, scratch_shapes=list(scratch_shapes),
                          compiler_params=params, **kw)
    return (lambda *args: call(*[_in_hbm(a) for a in args])) if pin else call


def _pick(n, cands):
    for c in cands:
        if n % c == 0:
            return c
    return n


def _nbytes(shape, dtype):
    n = 1
    for s in shape:
        n *= s
    return n * jnp.dtype(dtype).itemsize


def _sds(shape, dtype):
    return jax.ShapeDtypeStruct(tuple(shape), dtype)


class _Sel:
    def __init__(self, arr, *idx):
        self.arr, self.idx = arr, tuple(idx)
        self.shape = arr.shape[len(idx):]
        self.ndim = len(self.shape)
        self.dtype = arr.dtype


def _arr(a):
    return a.arr if isinstance(a, _Sel) else a


def _spec(a, block=None, index=None):
    block = tuple(a.shape) if block is None else tuple(block)
    index = (lambda *g: (0,) * len(block)) if index is None else index
    if isinstance(a, _Sel):
        lead = a.idx
        return pl.BlockSpec((None,) * len(lead) + block, lambda *g: lead + tuple(index(*g)))
    return pl.BlockSpec(block, lambda *g: tuple(index(*g)))


def _ospec(a):
    return pl.BlockSpec(tuple(a.shape), lambda *g: (0,) * a.ndim)


def _rows_of(shape):
    return lax.broadcasted_iota(jnp.int32, shape, 0)


def _lanes_of(shape):
    return lax.broadcasted_iota(jnp.int32, shape, 1)


def _sdn(x, k, fill):
    n = x.shape[0]
    return jnp.where(_rows_of(x.shape) >= k, pltpu.roll(x, k % n, 0), fill)


def _sup(x, k, fill):
    n = x.shape[0]
    return jnp.where(_rows_of(x.shape) < n - k, pltpu.roll(x, (n - k) % n, 0), fill)


@functools.partial(jax.custom_vjp, nondiff_argnums=(1,))
def _shift_dn(x, k):
    return pltpu.roll(x, k, 0)


def _shift_dn_fwd(x, k):
    return pltpu.roll(x, k, 0), None


def _shift_dn_bwd(k, _, g):
    return (pltpu.roll(g, g.shape[0] - k, 0),)


_shift_dn.defvjp(_shift_dn_fwd, _shift_dn_bwd)


SUBLANES = 8


def _lin_scan_impl(a, b, h0):
    n = a.shape[0]
    pos = _rows_of(a.shape) % SUBLANES
    aa, bb = a, b
    k = 1
    while k < SUBLANES:
        keep = pos >= k
        bb = bb + jnp.where(keep, aa * pltpu.roll(bb, k, 0), 0.0)
        aa = aa * jnp.where(keep, pltpu.roll(aa, k, 0), 1.0)
        k *= 2
    out, carry = [], h0
    for r in range(n // SUBLANES):
        rows = slice(r * SUBLANES, (r + 1) * SUBLANES)
        hr = bb[rows] + aa[rows] * carry
        out.append(hr)
        carry = hr[SUBLANES - 1:]
    return jnp.concatenate(out, axis=0)


@jax.custom_vjp
def _lin_scan(a, b, h0):
    return _lin_scan_impl(a, b, h0)


def _lin_scan_fwd(a, b, h0):
    h = _lin_scan_impl(a, b, h0)
    return h, (a, h, h0)


def _lin_scan_bwd(res, g):
    a, h, h0 = res
    n = a.shape[0]
    pos = _rows_of(a.shape) % SUBLANES
    cc, gg = _sup(a, 1, 0.0), g
    k = 1
    while k < SUBLANES:
        keep = pos < SUBLANES - k
        gg = gg + jnp.where(keep, cc * pltpu.roll(gg, n - k, 0), 0.0)
        cc = cc * jnp.where(keep, pltpu.roll(cc, n - k, 0), 1.0)
        k *= 2
    out, carry = [], jnp.zeros_like(h0)
    for r in range(n // SUBLANES - 1, -1, -1):
        rows = slice(r * SUBLANES, (r + 1) * SUBLANES)
        gr = gg[rows] + cc[rows] * carry
        out.append(gr)
        carry = gr[:1]
    gg = jnp.concatenate(out[::-1], axis=0)
    first = _rows_of(a.shape) == 0
    hprev = jnp.where(first, h0, _sdn(h, 1, 0.0))
    dh0 = jnp.sum(jnp.where(first, a * gg, 0.0), axis=0, keepdims=True)
    return gg * hprev, gg, dh0


_lin_scan.defvjp(_lin_scan_fwd, _lin_scan_bwd)


def _cumsum_sub_impl(x):
    pos = _rows_of(x.shape) % HGRN_SUB
    k = 1
    while k < HGRN_SUB:
        x = x + jnp.where(pos >= k, pltpu.roll(x, k, 0), 0.0)
        k *= 2
    return x


@jax.custom_vjp
def _cumsum_sub(x):
    return _cumsum_sub_impl(x)


def _cumsum_sub_fwd(x):
    return _cumsum_sub_impl(x), None


def _cumsum_sub_bwd(_, g):
    n = g.shape[0]
    pos = _rows_of(g.shape) % HGRN_SUB
    k = 1
    while k < HGRN_SUB:
        g = g + jnp.where(pos < HGRN_SUB - k, pltpu.roll(g, n - k, 0), 0.0)
        k *= 2
    return (g,)


_cumsum_sub.defvjp(_cumsum_sub_fwd, _cumsum_sub_bwd)


def _dot(a, b, ca, cb):
    return lax.dot_general(a.astype(BF16), b.astype(BF16), (((ca,), (cb,)), ((), ())), preferred_element_type=F32)


@jax.custom_vjp
def _mm(a, b):
    return _dot(a, b, 1, 0)


def _mm_fwd(a, b):
    return _dot(a, b, 1, 0), (a, b)


def _mm_bwd(res, g):
    a, b = res
    return _dot(g, b, 1, 1), _dot(a, g, 0, 0)


_mm.defvjp(_mm_fwd, _mm_bwd)


@jax.custom_vjp
def _mm_nt(a, b):
    return _dot(a, b, 1, 1)


def _mm_nt_fwd(a, b):
    return _dot(a, b, 1, 1), (a, b)


def _mm_nt_bwd(res, g):
    a, b = res
    return _dot(g, b, 1, 0), _dot(g, a, 0, 0)


_mm_nt.defvjp(_mm_nt_fwd, _mm_nt_bwd)


@jax.custom_vjp
def _mm_tn(a, b):
    return _dot(a, b, 0, 0)


def _mm_tn_fwd(a, b):
    return _dot(a, b, 0, 0), (a, b)


def _mm_tn_bwd(res, g):
    a, b = res
    return _dot(b, g, 1, 1), _dot(a, g, 1, 0)


_mm_tn.defvjp(_mm_tn_fwd, _mm_tn_bwd)


def _head_mask(shape, h):
    return (_lanes_of(shape) // HEAD_DIM) == h


def _stack_heads(x):
    return jnp.concatenate([jnp.where(_head_mask(x.shape, h), x, 0.0) for h in range(N_HEADS)], axis=0)


def _unstack_heads(p):
    r = p.shape[0] // N_HEADS
    out = None
    for h in range(N_HEADS):
        blk = p[h * r:(h + 1) * r]
        term = jnp.where(_head_mask(blk.shape, h), blk, 0.0)
        out = term if out is None else out + term
    return out


def _segmean_impl(x):
    n = x.shape[1]
    same = (lax.broadcasted_iota(jnp.int32, (n, n), 0) // HEAD_DIM) == (lax.broadcasted_iota(jnp.int32, (n, n), 1) // HEAD_DIM)
    m = jnp.where(same, 1.0 / HEAD_DIM, 0.0).astype(BF16)
    hi = x.astype(BF16)
    lo = (x - hi.astype(F32)).astype(BF16)
    dn = (((1,), (0,)), ((), ()))
    return (lax.dot_general(hi, m, dn, preferred_element_type=F32)
            + lax.dot_general(lo, m, dn, preferred_element_type=F32))


@jax.custom_vjp
def _segmean(x):
    return _segmean_impl(x)


def _segmean_fwd(x):
    return _segmean_impl(x), None


def _segmean_bwd(_, g):
    return (_segmean_impl(g),)


_segmean.defvjp(_segmean_fwd, _segmean_bwd)


GELU_C = 0.7978845608028654
GELU_A = 0.044715


@jax.custom_vjp
def _gelu(x):
    return 0.5 * x * (1.0 + jnp.tanh(GELU_C * x * (1.0 + GELU_A * (x * x))))


def _gelu_fwd(x):
    x2 = x * x
    t = jnp.tanh(GELU_C * x * (1.0 + GELU_A * x2))
    return 0.5 * x * (1.0 + t), (x, x2, t)


def _gelu_bwd(res, g):
    x, x2, t = res
    half = 0.5 * (1.0 + t)
    return (g * (half + (0.5 * GELU_C) * x * (1.0 - t * t) * (1.0 + (3.0 * GELU_A) * x2)),)


_gelu.defvjp(_gelu_fwd, _gelu_bwd)


def _log1p(u):
    w = 1.0 + u
    return jnp.where(w == 1.0, u, jnp.log(w) * (u / (w - 1.0)))


def _softplus(y):
    return jnp.maximum(y, 0.0) + _log1p(jnp.exp(-jnp.abs(y)))


def _rms(x, g):
    return x * lax.rsqrt(jnp.mean(x * x, axis=-1, keepdims=True) + EPS) * g


def _gmlp_chunk(zu, zv, ln_g, ln_b, wcat, bfull):
    u = _gelu(zu)
    v = _gelu(zv)
    mu = jnp.mean(v, axis=-1, keepdims=True)
    var = jnp.mean(jnp.square(v - mu), axis=-1, keepdims=True)
    vn = (v - mu) * lax.rsqrt(var + EPS) * ln_g + ln_b
    sv = _unstack_heads(_mm(wcat, vn)) + bfull
    return u * sv


def _rglru_tile(xb_ext, gb, h0, cw, cb, wa, ba, wx, bx, lam):
    xc = (cb + cw[0:1] * _shift_dn(xb_ext, 3) + cw[1:2] * _shift_dn(xb_ext, 2) + cw[2:3] * _shift_dn(xb_ext, 1)
          + cw[3:4] * xb_ext)[8:]
    r = jax.nn.sigmoid(_mm(xc, wa) + ba)
    i = jax.nn.sigmoid(_mm(xc, wx) + bx)
    log_a = (-RGLRU_C) * r * _softplus(-lam)
    a = jnp.exp(log_a)
    mult = jnp.sqrt(-jnp.tanh(log_a) * (a * a + 1.0))
    h = _lin_scan(a, mult * (i * xc), h0)
    y = h * _gelu(gb)
    h_last = jnp.sum(jnp.where(_rows_of(h.shape) == h.shape[0] - 1, h, 0.0), axis=0, keepdims=True)
    return y, h_last


def _pool_tile(xd_ext, inv, wd, scale):
    s1 = xd_ext + _shift_dn(xd_ext, 1)
    s2 = s1 + _shift_dn(s1, 2)
    s3 = s2 + _shift_dn(s2, 4)
    s4 = s3 + _shift_dn(s3, 8)
    grp = _lanes_of(xd_ext.shape) // HEAD_DIM
    win = jnp.where(grp == 0, s1, jnp.where(grp == 1, s2, jnp.where(grp == 2, s3, s4)))
    pooled = win[16:] * inv - xd_ext[16:]
    return _mm(pooled, wd) * scale


def _hgrn_chunk(q, f, i, g, st, lb, ngf):
    n = q.shape[0]
    nsub = n // HGRN_SUB
    qs = jax.nn.silu(q)
    fg = lb + (1.0 - lb) * jax.nn.sigmoid(f)
    lf = jnp.log(fg)
    k = 1.0 - fg
    bl = _cumsum_sub(lf)
    row = _rows_of(q.shape)
    blk = row // HGRN_SUB
    betas = [jnp.zeros_like(lb)]
    for s in range(nsub):
        tot = jnp.sum(jnp.where(row == s * HGRN_SUB + HGRN_SUB - 1, bl, 0.0), axis=0, keepdims=True)
        betas.append(betas[-1] + tot)
    b_end = betas[nsub]
    beta_full = jnp.zeros_like(q)
    for s in range(1, nsub):
        beta_full = jnp.where(blk == s, betas[s], beta_full)
    qh = qs * jnp.exp(bl)
    qt = qh * jnp.exp(beta_full)
    b_all = beta_full + bl
    kt = k * jnp.exp(b_end - b_all)
    outs = []
    for s in range(nsub):
        kh = k * jnp.exp(jnp.minimum(betas[s] - b_all, HGRN_EXP_CLAMP))
        qstk = _stack_heads(qh[s * HGRN_SUB:(s + 1) * HGRN_SUB])
        att = _mm_nt(qstk, kh)
        ar = _rows_of(att.shape) % HGRN_SUB + s * HGRN_SUB
        att = jnp.where(_lanes_of(att.shape) <= ar, att, 0.0)
        outs.append(_unstack_heads(_mm(att, i)))
    o = jnp.concatenate(outs, axis=0) + _mm_nt(qt, st)
    same = (_rows_of(st.shape) // HEAD_DIM) == (_lanes_of(st.shape) // HEAD_DIM)
    st_new = st * jnp.exp(b_end) + jnp.where(same, _mm_tn(i, kt), 0.0)
    on = o * lax.rsqrt(_segmean(o * o) + EPS) * ngf
    return on * jax.nn.silu(g), st_new


def _ffn_tile(eg, ev, wg, bg, wv, bv):
    gt = (bg + wg[0:1] * _shift_dn(eg, 2) + wg[1:2] * _shift_dn(eg, 1) + wg[2:3] * eg)[8:]
    val = (bv + wv[0:1] * _shift_dn(ev, 2) + wv[1:2] * _shift_dn(ev, 1) + wv[2:3] * ev)[8:]
    return _gelu(gt) * val


MXU_WIDTH = 256
MATMUL_BLOCK_BUDGET = 18 * MIB


def _matmul_tiles(m, k, n, a_dtype, b_dtype, out_dtype, has_res):
    best = None
    for tm in (2048, 1024, 512, 256):
        if m % tm:
            continue
        for tn in (1024, 768, 1408, 512, 256, 128):
            if n % tn:
                continue
            blk = (_nbytes((tm, k), a_dtype) + _nbytes((k, tn), b_dtype) + _nbytes((tm, tn), out_dtype)
                   + (_nbytes((tm, tn), F32) if has_res else 0))
            if blk > MATMUL_BLOCK_BUDGET:
                continue
            waste = -(-tn // MXU_WIDTH) * MXU_WIDTH / tn
            cost = (m // tm) * (n // tn) + 64 * (waste - 1.0)
            if best is None or cost < best[0]:
                best = (cost, tm, tn, blk)
    assert best is not None, (m, k, n)
    return best[1:]


def _matmul(a, b, *, name, nt=False, res=None, out_dtype=F32):
    m, k = a.shape
    n = b.shape[0] if nt else b.shape[1]
    tm, tn, blk = _matmul_tiles(m, k, n, a.dtype, b.dtype, out_dtype, res is not None)
    dims = (((1,), (1,)), ((), ())) if nt else (((1,), (0,)), ((), ()))

    def body(*refs):
        if res is None:
            a_ref, b_ref, o_ref = refs
        else:
            a_ref, b_ref, r_ref, o_ref = refs
        acc = lax.dot_general(a_ref[...], b_ref[...], dims, preferred_element_type=F32)
        if res is not None:
            acc = acc + r_ref[...]
        o_ref[...] = acc.astype(out_dtype)

    in_specs = [pl.BlockSpec((tm, k), lambda i, j: (i, 0)),
                _spec(b, (tn, k), lambda i, j: (j, 0)) if nt else _spec(b, (k, tn), lambda i, j: (0, j))]
    args = [a, _arr(b)]
    if res is not None:
        in_specs.append(pl.BlockSpec((tm, tn), lambda i, j: (i, j)))
        args.append(res)
    return _pcall(body, name=name, out_shape=_sds((m, n), out_dtype), grid=(m // tm, n // tn), in_specs=in_specs,
                  out_specs=pl.BlockSpec((tm, tn), lambda i, j: (i, j)), semantics=("parallel", "parallel"),
                  block_bytes=blk + _nbytes((tm, tn), F32))(*args)


def _matmul_rms_bwd(a, b, x, g, dres, *, name, nt=False, res=None):
    m, k = a.shape
    n = b.shape[0] if nt else b.shape[1]
    tm = _pick(m, (512, 256))
    dims = (((1,), (1,)), ((), ())) if nt else (((1,), (0,)), ((), ()))

    def body(*refs):
        a_ref, b_ref, x_ref, g_ref, dr_ref = refs[:5]
        dx_ref, dxb_ref, dg_ref = refs[-3:]
        dh = lax.dot_general(a_ref[...], b_ref[...], dims, preferred_element_type=F32)
        if res is not None:
            dh = dh + refs[5][...]
        _, vjp = jax.vjp(_rms, x_ref[...], g_ref[...])
        dxn, dg = vjp(dh)
        dx = dr_ref[...] + dxn
        dx_ref[...] = dx
        dxb_ref[...] = dx.astype(BF16)
        _acc_out(dg_ref, dg, pl.program_id(0) == 0)

    row = pl.BlockSpec((tm, n), lambda i: (i, 0))
    vec = pl.BlockSpec((1, n), lambda i: (0, 0))
    in_specs = [pl.BlockSpec((tm, k), lambda i: (i, 0)),
                _spec(b, (n, k), lambda i: (0, 0)) if nt else _spec(b, (k, n), lambda i: (0, 0)), row, _spec(g), row]
    args = [a, _arr(b), x, _arr(g), dres]
    if res is not None:
        in_specs.append(row)
        args.append(res)
    blk = _nbytes((tm, k), a.dtype) + _nbytes((k, n), b.dtype) + 6 * _nbytes((tm, n), F32)
    return _pcall(body, name=name, out_shape=(_sds((m, n), F32), _sds((m, n), BF16), _sds((1, n), F32)), grid=(m // tm,),
                  in_specs=in_specs, out_specs=(row, row, vec), semantics=("arbitrary",), block_bytes=blk)(*args)


def _ple_rms_bwd(dx3, gl, pe, w_pg, x, g, *, name):
    m, n = dx3.shape
    tm = _pick(m, (512, 256))

    def body(d3_ref, gl_ref, pe_ref, w_ref, x_ref, g_ref, dx_ref, dxb_ref, dg_ref, dpe_ref, dgl_ref):
        gate = jax.nn.sigmoid(gl_ref[...])
        d3 = d3_ref[...]
        dpe_ref[...] = (d3 * gate).astype(BF16)
        dgl = (d3 * pe_ref[...] * gate * (1.0 - gate)).astype(BF16)
        dgl_ref[...] = dgl
        dh = lax.dot_general(dgl, w_ref[...], (((1,), (1,)), ((), ())), preferred_element_type=F32)
        _, vjp = jax.vjp(_rms, x_ref[...], g_ref[...])
        dxn, dg = vjp(dh)
        dx = d3 + dxn
        dx_ref[...] = dx
        dxb_ref[...] = dx.astype(BF16)
        _acc_out(dg_ref, dg, pl.program_id(0) == 0)

    row = pl.BlockSpec((tm, n), lambda i: (i, 0))
    vec = pl.BlockSpec((1, n), lambda i: (0, 0))
    blk = _nbytes((n, n), BF16) + 9 * _nbytes((tm, n), F32)
    return _pcall(body, name=name,
                  out_shape=(_sds((m, n), F32), _sds((m, n), BF16), _sds((1, n), F32), _sds((m, n), BF16), _sds((m, n), BF16)),
                  grid=(m // tm,), in_specs=[row, row, row, _spec(w_pg, (n, n), lambda i: (0, 0)), row, _spec(g)],
                  out_specs=(row, row, vec, row, row), semantics=("arbitrary",), block_bytes=blk)(
                      dx3, gl, pe, _arr(w_pg), x, _arr(g))


def _matmul_tn(a, b, *, name, out_dtype=BF16, out_rows=None, row_off=0, into=None):
    m, k1 = a.shape
    n = b.shape[1]
    tk = _pick(k1, (512, 256, 128))
    off = row_off // tk
    assert off * tk == row_off

    def body(a_ref, b_ref, *rest):
        rest[-1][...] = lax.dot_general(a_ref[...], b_ref[...], (((0,), (0,)), ((), ())),
                                        preferred_element_type=F32).astype(out_dtype)

    blk = 2 * _nbytes((m, tk), a.dtype) + _nbytes((m, n), b.dtype) + _nbytes((tk, n), F32)
    in_specs = [pl.BlockSpec((m, tk), lambda i: (0, i)), pl.BlockSpec((m, n), lambda i: (0, 0))]
    args = [a, b]
    if into is not None:
        in_specs.append(HBM_SPEC)
        args.append(into)
    return _pcall(body, name=name, out_shape=_sds((out_rows or k1, n), out_dtype), grid=(k1 // tk,), in_specs=in_specs,
                  out_specs=pl.BlockSpec((tk, n), lambda i: (i + off, 0)), semantics=("parallel",), block_bytes=blk,
                  aliases=None if into is None else {2: 0})(*args)


def _rms_matmul(x, g, bs, *, name, nt=False, ple=None):
    m, d = x.shape
    n = bs[0].shape[0] if nt else bs[0].shape[1]
    nb = len(bs)
    nout = nb if ple is None else 3
    best = None
    for tm_c in (1024, 512, 256):
        for tn_c in (1408, 1024, 768, 512, 256, 128):
            if m % tm_c or n % tn_c:
                continue
            blk_c = (_nbytes((tm_c, d), F32) + 2 * _nbytes((tm_c, d), BF16) + nb * _nbytes((d, tn_c), BF16)
                     + (nout + 1) * _nbytes((tm_c, tn_c), F32))
            steps = (m // tm_c) * (n // tn_c)
            if blk_c <= MATMUL_BLOCK_BUDGET and (best is None or steps < best[0]):
                best = (steps, tm_c, tn_c, blk_c)
    _, tm, tn, blk = best
    dims = (((1,), (1,)), ((), ())) if nt else (((1,), (0,)), ((), ()))

    def body(*refs):
        x_ref, g_ref, b_refs = refs[0], refs[1], refs[2:2 + nb]
        rest = refs[2 + nb:]
        h_scr = rest[-1]
        j = pl.program_id(1)

        @pl.when(j == 0)
        def _():
            h = _rms(x_ref[...], g_ref[...]).astype(BF16)
            h_scr[...] = h
            rest[-2 - nb - (2 if ple else 0)][...] = h

        h = h_scr[...]
        if ple is None:
            for k in range(nb):
                rest[-1 - nb + k][...] = lax.dot_general(h, b_refs[k][...], dims, preferred_element_type=F32)
        else:
            p_ref, wpe_ref, xt_ref = rest[0], rest[1], rest[2]
            gl_ref, pe_ref, out_ref = rest[-4], rest[-3], rest[-2]
            gl = lax.dot_general(h, b_refs[0][...], dims, preferred_element_type=F32)
            pe = lax.dot_general(p_ref[...], wpe_ref[...], (((1,), (1,)), ((), ())), preferred_element_type=F32)
            gl_ref[...] = gl
            pe_ref[...] = pe
            out_ref[...] = xt_ref[...] + pe * jax.nn.sigmoid(gl)

    row = pl.BlockSpec((tm, d), lambda i, j: (i, 0))
    tile = pl.BlockSpec((tm, tn), lambda i, j: (i, j))
    in_specs = [row, _spec(g)] + [_spec(b, (tn, d), lambda i, j: (j, 0)) if nt else _spec(b, (d, tn), lambda i, j: (0, j))
                                  for b in bs]
    args = [x, _arr(g)] + [_arr(b) for b in bs]
    out_shape, out_specs = [_sds((m, d), BF16)], [row]
    if ple is None:
        out_shape += [_sds((m, n), F32)] * nb
        out_specs += [tile] * nb
    else:
        p, wpe = ple
        in_specs += [pl.BlockSpec((tm, p.shape[1]), lambda i, j: (i, 0)), _spec(wpe, (tn, p.shape[1]), lambda i, j: (j, 0)),
                     tile]
        args += [p, _arr(wpe), x]
        out_shape += [_sds((m, n), F32)] * 3
        out_specs += [tile] * 3
    outs = _pcall(body, name=name, out_shape=tuple(out_shape), grid=(m // tm, n // tn), in_specs=in_specs,
                  out_specs=tuple(out_specs), scratch_shapes=[pltpu.VMEM((tm, d), BF16)],
                  semantics=("parallel", "arbitrary"), block_bytes=blk)(*args)
    return outs[0], list(outs[1:])


def _up_ffn_fwd(x, g, wg, wv, cwf, cbf, *, name):
    m, d = x.shape
    n = wg.shape[0]
    tm = _pick(m, (256, 128))
    tn = _pick(n, (1408, 256, 128))
    nj = n // tn
    dims = (((1,), (1,)), ((), ()))

    def body(x_ref, g_ref, wg_ref, wv_ref, tg_ref, bg_ref, tv_ref, bv_ref, h_ref, hg_ref, hv_ref, a_ref, cg_scr, cv_scr):
        i = pl.program_id(1)
        h = _rms(x_ref[...], g_ref[...]).astype(BF16)
        h_ref[...] = h
        hg = lax.dot_general(h, wg_ref[...], dims, preferred_element_type=F32)
        hv = lax.dot_general(h, wv_ref[...], dims, preferred_element_type=F32)
        hg_ref[...] = hg
        hv_ref[...] = hv
        eg = jnp.concatenate([jnp.where(i == 0, 0.0, cg_scr[...]), hg], axis=0)
        ev = jnp.concatenate([jnp.where(i == 0, 0.0, cv_scr[...]), hv], axis=0)
        a_ref[...] = _ffn_tile(eg, ev, tg_ref[...], bg_ref[...], tv_ref[...], bv_ref[...]).astype(BF16)
        cg_scr[...] = hg[tm - 8:]
        cv_scr[...] = hv[tm - 8:]

    row = pl.BlockSpec((tm, d), lambda j, i: (i, 0))
    hrow = pl.BlockSpec((tm, d), lambda j, i: (j * (m // tm) + i, 0))
    tile = pl.BlockSpec((tm, tn), lambda j, i: (i, j))
    wspec = lambda w: _spec(w, (tn, d), lambda j, i: (j, 0))
    taps = lambda off: _spec(cwf, (3, tn), lambda j, i: (0, j + off))
    bias = lambda off: _spec(cbf, (1, tn), lambda j, i: (0, j + off))
    blk = (_nbytes((tm, d), F32) + _nbytes((tm, d), BF16) + 2 * _nbytes((tn, d), BF16) + 12 * _nbytes((tm, tn), F32))
    return _pcall(body, name=name,
                  out_shape=(_sds((nj * m, d), BF16), _sds((m, n), F32), _sds((m, n), F32), _sds((m, n), BF16)),
                  grid=(nj, m // tm),
                  in_specs=[row, _spec(g), wspec(wg), wspec(wv), taps(0), bias(0), taps(nj), bias(nj)],
                  out_specs=(hrow, tile, tile, tile),
                  scratch_shapes=[pltpu.VMEM((8, tn), F32), pltpu.VMEM((8, tn), F32)],
                  semantics=("arbitrary", "arbitrary"), block_bytes=blk)(
                      x, _arr(g), _arr(wg), _arr(wv), _arr(cwf), _arr(cbf), _arr(cwf), _arr(cbf))


def _loss_head(x, g, target, *, name):
    s, d = x.shape
    tm = _pick(s, (256, 128))

    def tile_loss(xv, gv, tv):
        err = jnp.square(_rms(xv, gv) - tv)
        return 0.5 * jnp.sum(jnp.mean(err, axis=-1, keepdims=True), axis=0, keepdims=True)

    def body(x_ref, g_ref, t_ref, l_ref, dx_ref, dg_ref):
        lv, vjp = jax.vjp(tile_loss, x_ref[...], g_ref[...], t_ref[...])
        dxv, dgv, _ = vjp(jnp.ones((1, 1), F32))
        dx_ref[...] = dxv

        @pl.when(pl.program_id(0) == 0)
        def _():
            l_ref[...] = jnp.zeros_like(l_ref)
            dg_ref[...] = jnp.zeros_like(dg_ref)

        l_ref[...] += jnp.broadcast_to(lv, l_ref.shape)
        dg_ref[...] += dgv

    row = pl.BlockSpec((tm, d), lambda i: (i, 0))
    vec = pl.BlockSpec((1, d), lambda i: (0, 0))
    return _pcall(body, name=name, out_shape=(_sds((8, 128), F32), _sds((s, d), F32), _sds((1, d), F32)),
                  grid=(s // tm,), in_specs=[row, vec, row],
                  out_specs=(pl.BlockSpec((8, 128), lambda i: (0, 0)), row, vec), semantics=("arbitrary",),
                  block_bytes=8 * _nbytes((tm, d), F32))(x, g, target)


def _acc_out(ref, val, first):
    @pl.when(first)
    def _():
        ref[...] = jnp.zeros_like(ref)

    ref[...] += val


def _gmlp_fwd(z, ln_g, ln_b, wcat, bfull, *, name):
    s = z.shape[0]
    t = _pick(s, (512, 256, 128))
    nch = t // GMLP_CHUNK

    def body(zu_ref, zv_ref, g_ref, b_ref, w_ref, bf_ref, o_ref):
        for c in range(nch):
            rows = pl.ds(c * GMLP_CHUNK, GMLP_CHUNK)
            o_ref[rows, :] = _gmlp_chunk(zu_ref[rows, :], zv_ref[rows, :], g_ref[...], b_ref[...], w_ref[...],
                                         bf_ref[...]).astype(BF16)

    col = lambda c: pl.BlockSpec((t, W_GRP), lambda i: (i, c))
    params = (ln_g, ln_b, wcat, bfull)
    return _pcall(body, name=name, out_shape=_sds((s, D_MODEL), BF16), grid=(s // t,),
                  in_specs=[col(0), col(1)] + [_spec(a) for a in params],
                  out_specs=pl.BlockSpec((t, W_GRP), lambda i: (i, 0)), semantics=("parallel",),
                  block_bytes=4 * _nbytes((t, W_GRP), F32))(z, z, *[_arr(a) for a in params])


def _gmlp_bwd(z, dmix, ln_g, ln_b, wcat, bfull, *, name):
    s = z.shape[0]
    t = _pick(s, (512, 256, 128))
    nch = t // GMLP_CHUNK

    def body(zu_ref, zv_ref, dy_ref, g_ref, b_ref, w_ref, bf_ref, dz_ref, dg_ref, db_ref, dw_ref, dbf_ref):
        acc = None
        for c in range(nch):
            rows = pl.ds(c * GMLP_CHUNK, GMLP_CHUNK)
            _, vjp = jax.vjp(_gmlp_chunk, zu_ref[rows, :], zv_ref[rows, :], g_ref[...], b_ref[...], w_ref[...],
                             bf_ref[...])
            du, dv, *dps = vjp(dy_ref[rows, :])
            dz_ref[rows, :] = jnp.concatenate([du, dv], axis=1).astype(BF16)
            acc = dps if acc is None else [x + y for x, y in zip(acc, dps)]
        first = pl.program_id(0) == 0
        for ref, val in zip((dg_ref, db_ref, dw_ref, dbf_ref), acc):
            _acc_out(ref, val, first)

    col = lambda c: pl.BlockSpec((t, W_GRP), lambda i: (i, c))
    params = (ln_g, ln_b, wcat, bfull)
    return _pcall(body, name=name,
                  out_shape=(_sds((s, D_PROJ), BF16),) + tuple(_sds(a.shape, F32) for a in params),
                  grid=(s // t,), in_specs=[col(0), col(1), col(0)] + [_spec(a) for a in params],
                  out_specs=(pl.BlockSpec((t, 2 * W_GRP), lambda i: (i, 0)),) + tuple(_ospec(a) for a in params),
                  semantics=("arbitrary",),
                  block_bytes=8 * _nbytes((t, W_GRP), F32))(z, z, dmix, *[_arr(a) for a in params])


def _rglru_fwd(z, prm, mix, *, name):
    s = z.shape[0]
    t = _pick(s, (512, 256, 128))
    nt = s // t

    def body(xb_ref, halo_ref, gb_ref, *rest):
        prm_refs, (y_ref, h0s_ref, h_scr) = rest[:len(prm)], rest[len(prm) + 1:]
        i = pl.program_id(0)

        @pl.when(i == 0)
        def _():
            h_scr[...] = jnp.zeros_like(h_scr)

        halo = jnp.where(i == 0, 0.0, halo_ref[...])
        h0 = h_scr[...]
        y, h_last = _rglru_tile(jnp.concatenate([halo, xb_ref[...]], axis=0), gb_ref[...], h0,
                                *[r[...] for r in prm_refs])
        y_ref[...] = y.astype(BF16)
        h0s_ref[...] = jnp.broadcast_to(h0, h0s_ref.shape)
        h_scr[...] = h_last

    in_specs = [pl.BlockSpec((t, W_GRP), lambda i: (i, 2)),
                pl.BlockSpec((8, W_GRP), lambda i: (jnp.maximum(i * (t // 8) - 1, 0), 2)),
                pl.BlockSpec((t, W_GRP), lambda i: (i, 3))] + [_spec(a) for a in prm] + [HBM_SPEC]
    return _pcall(body, name=name, out_shape=(_sds(mix.shape, BF16), _sds((nt, 8, W_GRP), F32)), grid=(nt,),
                  in_specs=in_specs,
                  out_specs=(pl.BlockSpec((t, W_GRP), lambda i: (i, 1)), pl.BlockSpec((None, 8, W_GRP), lambda i: (i, 0, 0))),
                  scratch_shapes=[pltpu.VMEM((1, W_GRP), F32)], semantics=("arbitrary",),
                  block_bytes=24 * _nbytes((t, W_GRP), F32), aliases={3 + len(prm): 0})(
                      z, z, z, *[_arr(a) for a in prm], mix)


def _rglru_bwd(z, dmix, h0s, prm, dz, *, name):
    s = z.shape[0]
    t = _pick(s, (512, 256, 128))
    nt = s // t
    npm = len(prm)

    def body(xb_ref, halo_ref, gb_ref, dy_ref, h0s_ref, *rest):
        prm_refs = rest[:npm]
        dz_ref = rest[npm + 1]
        dprm_refs = rest[npm + 2:2 * npm + 2]
        dh_scr, dhalo_scr = rest[2 * npm + 2:]
        i = pl.program_id(0)
        r = nt - 1 - i

        @pl.when(i == 0)
        def _():
            dh_scr[...] = jnp.zeros_like(dh_scr)
            dhalo_scr[...] = jnp.zeros_like(dhalo_scr)

        halo = jnp.where(r == 0, 0.0, halo_ref[...])
        h0 = h0s_ref[0:1, :]
        _, vjp = jax.vjp(_rglru_tile, jnp.concatenate([halo, xb_ref[...]], axis=0), gb_ref[...], h0,
                         *[p[...] for p in prm_refs])
        dext, dgb, _dh0, *dps = vjp((dy_ref[...], dh_scr[...]))
        dmain = dext[8:]
        dxb = jnp.concatenate([dmain[:t - 8], dmain[t - 8:] + dhalo_scr[...]], axis=0)
        dz_ref[...] = jnp.concatenate([dxb, dgb], axis=1).astype(BF16)
        dh_scr[...] = _dh0
        dhalo_scr[...] = dext[:8]
        for ref, val in zip(dprm_refs, dps):
            _acc_out(ref, val, i == 0)

    rev = lambda c: pl.BlockSpec((t, W_GRP), lambda i: (nt - 1 - i, c))
    in_specs = [rev(2), pl.BlockSpec((8, W_GRP), lambda i: (jnp.maximum((nt - 1 - i) * (t // 8) - 1, 0), 2)), rev(3),
                rev(1), pl.BlockSpec((None, 8, W_GRP), lambda i: (nt - 1 - i, 0, 0))] + [_spec(a) for a in prm] + [HBM_SPEC]
    return _pcall(body, name=name,
                  out_shape=(_sds(dz.shape, BF16),) + tuple(_sds(a.shape, F32) for a in prm),
                  grid=(nt,), in_specs=in_specs,
                  out_specs=(pl.BlockSpec((t, 2 * W_GRP), lambda i: (nt - 1 - i, 1)),) + tuple(_ospec(a) for a in prm),
                  scratch_shapes=[pltpu.VMEM((1, W_GRP), F32), pltpu.VMEM((8, W_GRP), F32)],
                  semantics=("arbitrary",), block_bytes=40 * _nbytes((t, W_GRP), F32), aliases={5 + npm: 0})(
                      z, z, z, dmix, h0s, *[_arr(a) for a in prm], dz)


def _pool_inv(i, t):
    pos = (_rows_of((t, W_GRP)) + i * t + 1).astype(F32)
    grp = _lanes_of((t, W_GRP)) // HEAD_DIM
    win = jnp.where(grp == 0, float(POOL_WINDOWS[0]), jnp.where(grp == 1, float(POOL_WINDOWS[1]),
                    jnp.where(grp == 2, float(POOL_WINDOWS[2]), float(POOL_WINDOWS[3]))))
    return 1.0 / jnp.minimum(pos, win)


def _pool_fwd(z, wd, scale, mix, *, name):
    s = z.shape[0]
    t = _pick(s, (512, 256, 128))

    def body(x_ref, halo_ref, wd_ref, sc_ref, _, y_ref):
        i = pl.program_id(0)
        halo = jnp.where(i == 0, 0.0, halo_ref[...])
        y = _pool_tile(jnp.concatenate([halo, x_ref[...]], axis=0), _pool_inv(i, t), wd_ref[...], sc_ref[...])
        y_ref[...] = y.astype(BF16)

    in_specs = [pl.BlockSpec((t, W_GRP), lambda i: (i, 8)),
                pl.BlockSpec((16, W_GRP), lambda i: (jnp.maximum(i * (t // 16) - 1, 0), 8)), _spec(wd), _spec(scale),
                HBM_SPEC]
    return _pcall(body, name=name, out_shape=_sds(mix.shape, BF16), grid=(s // t,), in_specs=in_specs,
                  out_specs=pl.BlockSpec((t, W_GRP), lambda i: (i, 3)), semantics=("parallel",),
                  block_bytes=12 * _nbytes((t, W_GRP), F32), aliases={4: 0})(z, z, _arr(wd), _arr(scale), mix)


def _pool_bwd(z, dmix, wd, scale, dz, *, name):
    s = z.shape[0]
    t = _pick(s, (512, 256, 128))
    nt = s // t

    def body(x_ref, halo_ref, dy_ref, wd_ref, sc_ref, _, dx_ref, dwd_ref, dsc_ref, dhalo_scr):
        i = pl.program_id(0)
        r = nt - 1 - i

        @pl.when(i == 0)
        def _():
            dhalo_scr[...] = jnp.zeros_like(dhalo_scr)

        halo = jnp.where(r == 0, 0.0, halo_ref[...])
        inv = _pool_inv(r, t)
        _, vjp = jax.vjp(lambda e, w, sc: _pool_tile(e, inv, w, sc), jnp.concatenate([halo, x_ref[...]], axis=0),
                         wd_ref[...], sc_ref[...])
        dext, dwd, dsc = vjp(dy_ref[...])
        dmain = dext[16:]
        dx = jnp.concatenate([dmain[:t - 16], dmain[t - 16:] + dhalo_scr[...]], axis=0)
        dx_ref[...] = dx.astype(BF16)
        dhalo_scr[...] = dext[:16]
        _acc_out(dwd_ref, dwd, i == 0)
        _acc_out(dsc_ref, dsc, i == 0)

    rev = lambda c: pl.BlockSpec((t, W_GRP), lambda i: (nt - 1 - i, c))
    in_specs = [rev(8), pl.BlockSpec((16, W_GRP), lambda i: (jnp.maximum((nt - 1 - i) * (t // 16) - 1, 0), 8)), rev(3),
                _spec(wd), _spec(scale), HBM_SPEC]
    return _pcall(body, name=name, out_shape=(_sds(dz.shape, BF16), _sds(wd.shape, F32), _sds(scale.shape, F32)),
                  grid=(nt,), in_specs=in_specs, out_specs=(rev(8), _ospec(wd), _ospec(scale)),
                  scratch_shapes=[pltpu.VMEM((16, W_GRP), F32)], semantics=("arbitrary",),
                  block_bytes=20 * _nbytes((t, W_GRP), F32), aliases={5: 0})(z, z, dmix, _arr(wd), _arr(scale), dz)


def _hgrn_fwd(z, lb, ngf, mix, *, name):
    s = z.shape[0]
    c = HGRN_CHUNK
    per = HGRN_STEP_CHUNKS
    ns = s // (c * per)

    def body(q_ref, f_ref, i_ref, g_ref, lb_ref, ng_ref, _, y_ref, sts_ref, st_scr):
        @pl.when(pl.program_id(0) == 0)
        def _():
            st_scr[...] = jnp.zeros_like(st_scr)

        st = st_scr[...]
        for k in range(per):
            rows = pl.ds(k * c, c)
            sts_ref[k] = st
            y, st = _hgrn_chunk(q_ref[rows, :], f_ref[rows, :], i_ref[rows, :], g_ref[rows, :], st, lb_ref[...],
                                ng_ref[...])
            y_ref[rows, :] = y.astype(BF16)
        st_scr[...] = st

    col = lambda k: pl.BlockSpec((per * c, W_GRP), lambda i: (i, k))
    return _pcall(body, name=name, out_shape=(_sds(mix.shape, BF16), _sds((ns * per, W_GRP, W_GRP), F32)), grid=(ns,),
                  in_specs=[col(4), col(5), col(6), col(7), _spec(lb), _spec(ngf), HBM_SPEC],
                  out_specs=(pl.BlockSpec((per * c, W_GRP), lambda i: (i, 2)),
                             pl.BlockSpec((per, W_GRP, W_GRP), lambda i: (i, 0, 0))),
                  scratch_shapes=[pltpu.VMEM((W_GRP, W_GRP), F32)], semantics=("arbitrary",),
                  block_bytes=16 * per * _nbytes((W_GRP, W_GRP), F32), aliases={6: 0})(
                      z, z, z, z, _arr(lb), _arr(ngf), mix)


def _hgrn_bwd(z, dmix, sts, lb, ngf, dz, *, name):
    s = z.shape[0]
    c = HGRN_CHUNK
    per = HGRN_STEP_CHUNKS
    ns = s // (c * per)

    def body(q_ref, f_ref, i_ref, g_ref, dy_ref, st_ref, lb_ref, ng_ref, _, dz_ref, dlb_ref, dng_ref, dst_scr):
        i = pl.program_id(0)

        @pl.when(i == 0)
        def _():
            dst_scr[...] = jnp.zeros_like(dst_scr)

        dst = dst_scr[...]
        dlb_sum = dng_sum = None
        for k in range(per - 1, -1, -1):
            rows = pl.ds(k * c, c)
            _, vjp = jax.vjp(_hgrn_chunk, q_ref[rows, :], f_ref[rows, :], i_ref[rows, :], g_ref[rows, :], st_ref[k],
                             lb_ref[...], ng_ref[...])
            dq, df, di, dg, dst, dlb, dng = vjp((dy_ref[rows, :], dst))
            dz_ref[rows, :] = jnp.concatenate([dq, df, di, dg], axis=1).astype(BF16)
            dlb_sum = dlb if dlb_sum is None else dlb_sum + dlb
            dng_sum = dng if dng_sum is None else dng_sum + dng
        dst_scr[...] = dst
        _acc_out(dlb_ref, dlb_sum, i == 0)
        _acc_out(dng_ref, dng_sum, i == 0)

    rev = lambda k: pl.BlockSpec((per * c, W_GRP), lambda i: (ns - 1 - i, k))
    vec = pl.BlockSpec((1, W_GRP), lambda i: (0, 0))
    return _pcall(body, name=name, out_shape=(_sds(dz.shape, BF16), _sds((1, W_GRP), F32), _sds((1, W_GRP), F32)),
                  grid=(ns,),
                  in_specs=[rev(4), rev(5), rev(6), rev(7), rev(2),
                            pl.BlockSpec((per, W_GRP, W_GRP), lambda i: (ns - 1 - i, 0, 0)), _spec(lb), _spec(ngf),
                            HBM_SPEC],
                  out_specs=(pl.BlockSpec((per * c, 4 * W_GRP), lambda i: (ns - 1 - i, 1)), vec, vec),
                  scratch_shapes=[pltpu.VMEM((W_GRP, W_GRP), F32)], semantics=("arbitrary",),
                  block_bytes=32 * per * _nbytes((W_GRP, W_GRP), F32), aliases={8: 0})(
                      z, z, z, z, dmix, sts, _arr(lb), _arr(ngf), dz)


def _lbs_fwd(c_lb, *, name):
    def body(c_ref, o_ref):
        c = c_ref[...]
        e = jnp.exp(c - jnp.max(c, axis=0, keepdims=True))
        sm = e / jnp.sum(e, axis=0, keepdims=True)
        run = jnp.zeros((1, W_GRP), F32)
        o_ref[0:1, :] = run
        for l in range(1, DEPTH):
            run = run + sm[l:l + 1]
            o_ref[l:l + 1, :] = run

    return _pcall(body, name=name, out_shape=_sds((DEPTH, W_GRP), F32), pin=False)(c_lb)


def _lbs_bwd(c_lb, dlbs, *, name):
    def body(c_ref, d_ref, o_ref):
        c = c_ref[...]
        e = jnp.exp(c - jnp.max(c, axis=0, keepdims=True))
        sm = e / jnp.sum(e, axis=0, keepdims=True)
        d = d_ref[...]
        dsm = [None] * DEPTH
        run = jnp.zeros((1, W_GRP), F32)
        for l in range(DEPTH - 1, 0, -1):
            run = run + d[l:l + 1]
            dsm[l] = run
        dsm[0] = jnp.zeros((1, W_GRP), F32)
        inner = sum(sm[l:l + 1] * dsm[l] for l in range(DEPTH))
        for l in range(DEPTH):
            o_ref[l:l + 1, :] = sm[l:l + 1] * (dsm[l] - inner)

    return _pcall(body, name=name, out_shape=_sds((DEPTH, W_GRP), F32), pin=False)(c_lb, dlbs)


def _ffn_bwd(hg, hv, dx, w_down, cwf, cbf, *, name):
    s, n = hg.shape
    t = _pick(s, (256, 128))
    cw = _pick(n, (1408, 256, 128))
    nt = s // t
    nj = n // cw

    def body(g_ref, gh_ref, v_ref, vh_ref, dx_ref, wd_ref, wg_ref, bg_ref, wv_ref, bv_ref, dg_ref, dv_ref, dwg_ref,
             dwv_ref, cg_scr, cv_scr):
        i = pl.program_id(1)
        r = nt - 1 - i

        @pl.when(i == 0)
        def _():
            cg_scr[...] = jnp.zeros_like(cg_scr)
            cv_scr[...] = jnp.zeros_like(cv_scr)

        da = lax.dot_general(dx_ref[...], wd_ref[...], (((1,), (1,)), ((), ())), preferred_element_type=F32)
        eg = jnp.concatenate([jnp.where(r == 0, 0.0, gh_ref[...]), g_ref[...]], axis=0)
        ev = jnp.concatenate([jnp.where(r == 0, 0.0, vh_ref[...]), v_ref[...]], axis=0)
        _, vjp = jax.vjp(_ffn_tile, eg, ev, wg_ref[...], bg_ref[...], wv_ref[...], bv_ref[...])
        deg, dev, dwg, dbg, dwv, dbv = vjp(da)
        for dext, scr, ref in ((deg, cg_scr, dg_ref), (dev, cv_scr, dv_ref)):
            dmain = dext[8:]
            ref[...] = jnp.concatenate([dmain[:t - 8], dmain[t - 8:] + scr[...]], axis=0).astype(BF16)
            scr[...] = dext[:8]
        zeros = jnp.zeros((4, cw), F32)
        _acc_out(dwg_ref, jnp.concatenate([dwg, dbg, zeros], axis=0), i == 0)
        _acc_out(dwv_ref, jnp.concatenate([dwv, dbv, zeros], axis=0), i == 0)

    main = pl.BlockSpec((t, cw), lambda j, i: (nt - 1 - i, j))
    halo = pl.BlockSpec((8, cw), lambda j, i: (jnp.maximum((nt - 1 - i) * (t // 8) - 1, 0), j))
    taps = lambda off: _spec(cwf, (3, cw), lambda j, i: (0, j + off))
    bias = lambda off: _spec(cbf, (1, cw), lambda j, i: (0, j + off))
    w8 = pl.BlockSpec((8, cw), lambda j, i: (0, j))
    d = dx.shape[1]
    in_specs = [main, halo, main, halo, pl.BlockSpec((t, d), lambda j, i: (nt - 1 - i, 0)),
                _spec(w_down, (cw, d), lambda j, i: (j, 0)), taps(0), bias(0), taps(nj), bias(nj)]
    return _pcall(body, name=name,
                  out_shape=(_sds((s, n), BF16), _sds((s, n), BF16), _sds((8, n), F32), _sds((8, n), F32)),
                  grid=(nj, nt), in_specs=in_specs, out_specs=(main, main, w8, w8),
                  scratch_shapes=[pltpu.VMEM((8, cw), F32), pltpu.VMEM((8, cw), F32)],
                  semantics=("parallel", "arbitrary"),
                  block_bytes=24 * _nbytes((t, cw), F32) + _nbytes((cw, d), BF16))(
                      hg, hg, hv, hv, dx, _arr(w_down), _arr(cwf), _arr(cbf), _arr(cwf), _arr(cbf))


def _all_gather(x, *, name):
    r, c = x.shape

    def body(x_ref, out_ref, send_sems, recv_sems, local_sem):
        mx, my, mc = lax.axis_index("x"), lax.axis_index("y"), lax.axis_index("c")
        me, sibling = (mx, my, mc), (mx, my, 1 - mc)
        chips = [(1 - mx, my), (mx, 1 - my), (1 - mx, 1 - my)]

        def slot(px, py, pc):
            return out_ref.at[4 * px + 2 * py + pc]

        def copy(k, block, to, src=None):
            return pltpu.make_async_remote_copy(src_ref=slot(*block) if src is None else src, dst_ref=slot(*block),
                                                send_sem=send_sems.at[k], recv_sem=recv_sems.at[k],
                                                device_id=to, device_id_type=MESH)

        mine = pltpu.make_async_copy(x_ref, slot(*me), local_sem)
        mine.start()
        first = [copy(0, me, sibling, src=x_ref)]
        first += [copy(1 + j, me, (*chip, mc), src=x_ref) for j, chip in enumerate(chips)]
        for cp in first:
            cp.start()
        passed = [copy(4 + j, (*chip, mc), sibling) for j, chip in enumerate(chips)]
        for j, chip in enumerate(chips):
            copy(1 + j, (*chip, mc), me).wait_recv()
            passed[j].start()
        copy(0, sibling, me).wait_recv()
        for j, chip in enumerate(chips):
            copy(4 + j, (*chip, 1 - mc), me).wait_recv()
        for cp in first + passed:
            cp.wait_send()
        mine.wait()

    hbm = pl.BlockSpec(memory_space=pl.ANY)
    return _pcall(body, name=name, out_shape=_sds((N_DEV, r, c), x.dtype), in_specs=[hbm], out_specs=hbm,
                  scratch_shapes=[pltpu.SemaphoreType.DMA((7,)), pltpu.SemaphoreType.DMA((7,)),
                                  pltpu.SemaphoreType.DMA(())])(x)


def _sum_slots(p, *, name):
    q, r, c = p.shape
    tr = _pick(r, (544, 408, 272, 192, 136, 64, 32, 16, 8))

    def body(p_ref, o_ref):
        acc = p_ref[0].astype(F32)
        for k in range(1, q):
            acc = acc + p_ref[k].astype(F32)
        o_ref[...] = acc

    return _pcall(body, name=name, out_shape=_sds((r, c), F32), grid=(r // tr,),
                  in_specs=[pl.BlockSpec((q, tr, c), lambda i: (0, i, 0))],
                  out_specs=pl.BlockSpec((tr, c), lambda i: (i, 0)), semantics=("parallel",),
                  block_bytes=(q + 2) * _nbytes((tr, c), F32))(p)


BIG_COMM = (('w_in', 288, D_MODEL), ('w_out', 128, D_MODEL), ('w_up', 704, D_MODEL), ('w_down', 352, D_MODEL),
            ('w_pe', 128, PLE_DIM), ('w_pg', 128, D_MODEL))
HBM_SPEC = pl.BlockSpec(memory_space=pl.ANY)


def _gather_layer(shards, l, *, name):
    na = len(shards)

    def body(*refs):
        x_refs, out_refs = refs[:na], refs[na:2 * na]
        send_sems, recv_sems, local_sems = refs[2 * na:]
        mx, my, mc = lax.axis_index("x"), lax.axis_index("y"), lax.axis_index("c")
        me, sibling = (mx, my, mc), (mx, my, 1 - mc)
        chips = [(1 - mx, my), (mx, 1 - my), (1 - mx, 1 - my)]

        def slot(a, px, py, pc):
            return out_refs[a].at[4 * px + 2 * py + pc]

        def copy(k, a, block, to, own=False):
            return pltpu.make_async_remote_copy(src_ref=x_refs[a].at[l] if own else slot(a, *block),
                                                dst_ref=slot(a, *block), send_sem=send_sems.at[k, a],
                                                recv_sem=recv_sems.at[k, a], device_id=to, device_id_type=MESH)

        mine = [pltpu.make_async_copy(x_refs[a].at[l], slot(a, *me), local_sems.at[a]) for a in range(na)]
        for cp in mine:
            cp.start()
        first = []
        for a in range(na):
            first.append(copy(0, a, me, sibling, own=True))
            first += [copy(1 + j, a, me, (*chip, mc), own=True) for j, chip in enumerate(chips)]
        for cp in first:
            cp.start()
        passed = []
        for j, chip in enumerate(chips):
            for a in range(na):
                copy(1 + j, a, (*chip, mc), me).wait_recv()
                fwd = copy(4 + j, a, (*chip, mc), sibling)
                fwd.start()
                passed.append(fwd)
        for a in range(na):
            copy(0, a, sibling, me).wait_recv()
        for j, chip in enumerate(chips):
            for a in range(na):
                copy(4 + j, a, (*chip, 1 - mc), me).wait_recv()
        for cp in first + passed:
            cp.wait_send()
        for cp in mine:
            cp.wait()

    return _pcall(body, name=name, out_shape=tuple(_sds((N_DEV,) + x.shape[1:], x.dtype) for x in shards),
                  in_specs=[HBM_SPEC] * na, out_specs=(HBM_SPEC,) * na,
                  scratch_shapes=[pltpu.SemaphoreType.DMA((7, na)), pltpu.SemaphoreType.DMA((7, na)),
                                  pltpu.SemaphoreType.DMA((na,))])(*shards)


SEM_SPEC = pl.BlockSpec(memory_space=pltpu.SEMAPHORE)
DATAFLOW_EFFECT = pltpu.SideEffectType.DATAFLOW_SIDE_EFFECTING


def _place_own(srcs, after, *, name):
    na = len(srcs)

    def body(*refs):
        x_refs, land_refs, sems = refs[:na], refs[na + len(after):2 * na + len(after)], refs[-1]
        me = 4 * lax.axis_index("x") + 2 * lax.axis_index("y") + lax.axis_index("c")
        cps = [pltpu.make_async_copy(x_refs[a], land_refs[a].at[me], sems.at[a]) for a in range(na)]
        for cp in cps:
            cp.start()
        for cp in cps:
            cp.wait()

    return _pcall(body, name=name, out_shape=tuple(_sds((N_DEV,) + x.shape, x.dtype) for x in srcs),
                  in_specs=[HBM_SPEC] * (na + len(after)), out_specs=(HBM_SPEC,) * na,
                  scratch_shapes=[pltpu.SemaphoreType.DMA((na,))], pin=False)(*srcs, *after)


def _exchange_start(srcs, lands, *, name, per_peer=False):
    na = len(srcs)

    def body(*refs):
        x_refs, land_refs = refs[:na], refs[na:2 * na]
        send_sems, recv_sems = refs[2 * na], refs[2 * na + 1]
        token = refs[-1]
        mx, my, mc = lax.axis_index("x"), lax.axis_index("y"), lax.axis_index("c")
        me = 4 * mx + 2 * my + mc
        peers = [(mx, my, 1 - mc)]
        for px, py in ((1 - mx, my), (mx, 1 - my), (1 - mx, 1 - my)):
            peers += [(px, py, mc), (px, py, 1 - mc)]
        for a in range(na):
            for peer in peers:
                src = x_refs[a].at[4 * peer[0] + 2 * peer[1] + peer[2]] if per_peer else x_refs[a]
                pltpu.make_async_remote_copy(src_ref=src, dst_ref=land_refs[a].at[me], send_sem=send_sems.at[a],
                                             recv_sem=recv_sems.at[a], device_id=peer, device_id_type=MESH).start()
        token[...] = jnp.zeros_like(token)

    hbm = lambda x: pltpu.HBM(x.shape, x.dtype)
    out_shape = ((pltpu.SemaphoreType.DMA((na,)), pltpu.SemaphoreType.DMA((na,))) + tuple(hbm(x) for x in srcs)
                 + tuple(hbm(x) for x in lands) + (_sds((8, 128), F32),))
    params = pltpu.CompilerParams(has_side_effects=DATAFLOW_EFFECT)
    pin = lambda x: pltpu.with_memory_space_constraint(x, pltpu.HBM)
    return pl.pallas_call(body, name=name, out_shape=out_shape, in_specs=[HBM_SPEC] * (2 * na),
                          out_specs=(SEM_SPEC, SEM_SPEC) + (HBM_SPEC,) * (2 * na) + (pl.BlockSpec(memory_space=pltpu.VMEM),),
                          input_output_aliases={i: 2 + i for i in range(2 * na)}, compiler_params=params)(
                              *[pin(x) for x in srcs], *[pin(x) for x in lands])


def _exchange_wait(started, after, *, name):
    send_sems, recv_sems, *bufs, _ = started
    na = len(bufs) // 2

    def body(*refs):
        land_refs = refs[na:2 * na]
        s_sems, r_sems = refs[2 * na], refs[2 * na + 1]
        me = (lax.axis_index("x"), lax.axis_index("y"), lax.axis_index("c"))
        for a in range(na):
            seven = land_refs[a].at[pl.ds(0, N_DEV - 1)]
            cp = pltpu.make_async_remote_copy(src_ref=seven, dst_ref=seven, send_sem=s_sems.at[a], recv_sem=r_sems.at[a],
                                              device_id=me, device_id_type=MESH)
            cp.wait_send()
            cp.wait_recv()

    hbm = lambda x: pltpu.HBM(x.shape, x.dtype)
    params = pltpu.CompilerParams(has_side_effects=DATAFLOW_EFFECT)
    outs = pl.pallas_call(body, name=name, out_shape=tuple(hbm(x) for x in bufs),
                          in_specs=[HBM_SPEC] * (2 * na) + [SEM_SPEC, SEM_SPEC, HBM_SPEC],
                          out_specs=(HBM_SPEC,) * (2 * na), input_output_aliases={i: i for i in range(2 * na)},
                          compiler_params=params)(*bufs, send_sems, recv_sems, after)
    return outs[:na], outs[na:]


def _pair_swap(grads, *, name):
    na = len(grads)

    def body(*refs):
        g_refs, recv_refs = refs[:na], refs[na:2 * na]
        send_sems, recv_sems = refs[2 * na:]
        mx, my, mc = lax.axis_index("x"), lax.axis_index("y"), lax.axis_index("c")
        sibling = (mx, my, 1 - mc)
        for a in range(na):
            for q in range(4):
                pltpu.make_async_remote_copy(src_ref=g_refs[a].at[q, 1 - mc], dst_ref=recv_refs[a].at[q],
                                             send_sem=send_sems.at[a], recv_sem=recv_sems.at[a],
                                             device_id=sibling, device_id_type=MESH).start()
        for a in range(na):
            pltpu.make_async_remote_copy(src_ref=recv_refs[a], dst_ref=recv_refs[a], send_sem=send_sems.at[a],
                                         recv_sem=recv_sems.at[a], device_id=sibling, device_id_type=MESH).wait()

    half = tuple(_sds((4,) + g.shape[2:], g.dtype) for g in grads)
    return _pcall(body, name=name, out_shape=half, in_specs=[HBM_SPEC] * na, out_specs=(HBM_SPEC,) * na,
                  scratch_shapes=[pltpu.SemaphoreType.DMA((na,)), pltpu.SemaphoreType.DMA((na,))])(*grads)


def _add_slabs(grads, recv, core, *, name):
    na = len(grads)

    def body(core_ref, *refs):
        for a in range(na):
            refs[2 * na + a][...] = (refs[a][...].astype(F32) + refs[na + a][...].astype(F32)).astype(BF16)

    own_specs = [pl.BlockSpec((None, None) + x.shape[2:], lambda q, core_ref: (q, core_ref[0], 0, 0)) for x in grads]
    specs = [pl.BlockSpec((None,) + x.shape[1:], lambda q, core_ref: (q, 0, 0)) for x in recv]
    blk = sum(_nbytes(x.shape[1:], F32) for x in recv)
    grid_spec = pltpu.PrefetchScalarGridSpec(num_scalar_prefetch=1, grid=(4,), in_specs=own_specs + specs,
                                             out_specs=tuple(specs))
    params = pltpu.CompilerParams(dimension_semantics=("parallel",), vmem_limit_bytes=_vmem_limit(2 * blk))
    return pl.pallas_call(body, name=name, out_shape=tuple(_sds(x.shape, BF16) for x in recv), grid_spec=grid_spec,
                          compiler_params=params)(core, *grads, *recv)


def _chip_exchange(parts, *, name):
    na = len(parts)

    def body(*refs):
        p_refs, out_refs = refs[:na], refs[na:2 * na]
        send_sems, recv_sems, local_sems = refs[2 * na:]
        mx, my, mc = lax.axis_index("x"), lax.axis_index("y"), lax.axis_index("c")
        mine_q = 2 * mx + my
        chips = [(1 - mx, my), (mx, 1 - my), (1 - mx, 1 - my)]
        owns = [pltpu.make_async_copy(p_refs[a].at[mine_q], out_refs[a].at[mine_q], local_sems.at[a]) for a in range(na)]
        for cp in owns:
            cp.start()
        sends = []
        for a in range(na):
            for k, chip in enumerate(chips):
                sends.append(pltpu.make_async_remote_copy(
                    src_ref=p_refs[a].at[2 * chip[0] + chip[1]], dst_ref=out_refs[a].at[mine_q],
                    send_sem=send_sems.at[k, a], recv_sem=recv_sems.at[k, a], device_id=(*chip, mc), device_id_type=MESH))
        for cp in sends:
            cp.start()
        for a in range(na):
            for k, chip in enumerate(chips):
                pltpu.make_async_remote_copy(
                    src_ref=p_refs[a].at[mine_q], dst_ref=out_refs[a].at[2 * chip[0] + chip[1]],
                    send_sem=send_sems.at[k, a], recv_sem=recv_sems.at[k, a], device_id=(*chip, mc),
                    device_id_type=MESH).wait_recv()
        for cp in sends:
            cp.wait_send()
        for cp in owns:
            cp.wait()

    return _pcall(body, name=name, out_shape=tuple(_sds(x.shape, x.dtype) for x in parts), in_specs=[HBM_SPEC] * na,
                  out_specs=(HBM_SPEC,) * na,
                  scratch_shapes=[pltpu.SemaphoreType.DMA((3, na)), pltpu.SemaphoreType.DMA((3, na)),
                                  pltpu.SemaphoreType.DMA((na,))])(*parts)


def _sum_chips(parts, *, name):
    na = len(parts)

    def body(*refs):
        for a in range(na):
            p_ref = refs[a]
            acc = p_ref[0].astype(F32)
            for k in range(1, p_ref.shape[0]):
                acc = acc + p_ref[k].astype(F32)
            refs[na + a][...] = acc

    half = lambda x: x.shape[1] // 2
    in_specs = [pl.BlockSpec((x.shape[0], half(x), x.shape[2]), lambda i: (0, i, 0)) for x in parts]
    out_specs = tuple(pl.BlockSpec((half(x), x.shape[2]), lambda i: (i, 0)) for x in parts)
    blk = sum(_nbytes((x.shape[0] + 2, half(x), x.shape[2]), BF16) for x in parts)
    return _pcall(body, name=name, out_shape=tuple(_sds(x.shape[1:], F32) for x in parts), grid=(2,),
                  in_specs=in_specs, out_specs=out_specs, semantics=("parallel",), block_bytes=blk)(*parts)


def _sum_devices(lands, own, me, *, name):
    na = len(lands)

    def body(me_ref, *refs):
        mine = me_ref[0]
        for a in range(na):
            l_ref, o_ref = refs[a], refs[na + a]
            acc = None
            for k in range(N_DEV):
                term = jnp.where(mine == k, o_ref[...], l_ref[k]).astype(F32)
                acc = term if acc is None else acc + term
            refs[2 * na + a][...] = acc

    half = lambda x: x.shape[1] // 2
    land_specs = [pl.BlockSpec((N_DEV, half(x), x.shape[2]), lambda i, me_ref: (0, i, 0)) for x in lands]
    own_specs = [pl.BlockSpec((None, half(x), x.shape[2]), lambda i, me_ref: (me_ref[0], i, 0)) for x in lands]
    out_specs = tuple(pl.BlockSpec((half(x), x.shape[2]), lambda i, me_ref: (i, 0)) for x in lands)
    blk = sum(_nbytes((N_DEV + 3, half(x), x.shape[2]), BF16) for x in lands)
    grid_spec = pltpu.PrefetchScalarGridSpec(num_scalar_prefetch=1, grid=(2,), in_specs=land_specs + own_specs,
                                             out_specs=out_specs)
    params = pltpu.CompilerParams(dimension_semantics=("parallel",), vmem_limit_bytes=_vmem_limit(blk))
    return pl.pallas_call(body, name=name, out_shape=tuple(_sds(x.shape[1:], F32) for x in lands), grid_spec=grid_spec,
                          compiler_params=params)(me, *lands, *own)


def _reduce_layer(grads, l):
    n = lambda s: f"l{l}_{s}"
    views = [g.reshape(4, 2, g.shape[0] // N_DEV, g.shape[1]) for g in grads]
    recv = _pair_swap(views, name=n("reduce_pair"))
    core = lax.axis_index("c").astype(jnp.int32).reshape(1)
    chip_sum = _add_slabs(views, recv, core, name=n("reduce_pair_add"))
    from_chips = _chip_exchange(chip_sum, name=n("reduce_chips"))
    return _sum_chips(from_chips, name=n("reduce_chips_add"))


def _adamw(w, g, m, v, *, name):
    lead, (r, c) = w.shape[:-2], w.shape[-2:]
    tr = _pick(r, (512, 352, 288, 256, 192, 128, 64, 32, 16, 8))
    c1 = 1.0 / (1.0 - ADAM_B1 ** ADAM_STEP)
    c2 = 1.0 / (1.0 - ADAM_B2 ** ADAM_STEP)

    def body(w_ref, g_ref, m_ref, v_ref, d_ref, nm_ref, nv_ref):
        gv = g_ref[...]
        nm = ADAM_B1 * m_ref[...] + (1.0 - ADAM_B1) * gv
        nv = ADAM_B2 * v_ref[...] + (1.0 - ADAM_B2) * jnp.square(gv)
        d_ref[...] = -ADAM_LR * ((nm * c1) / (jnp.sqrt(nv * c2) + ADAM_EPS) + ADAM_WD * w_ref[...])
        nm_ref[...] = nm
        nv_ref[...] = nv

    if lead:
        blk = pl.BlockSpec((None, tr, c), lambda k, i: (k, i, 0))
        grid, sem = (lead[0], r // tr), ("parallel", "parallel")
    else:
        blk = pl.BlockSpec((tr, c), lambda i: (i, 0))
        grid, sem = (r // tr,), ("parallel",)
    out = _sds(w.shape, F32)
    return _pcall(body, name=name, out_shape=(out, out, out), grid=grid, in_specs=[blk] * 4,
                  out_specs=(blk, blk, blk), semantics=sem, block_bytes=7 * _nbytes((tr, c), F32))(w, g, m, v)


def _pack_flat(arrs, rows, cols=1024):
    flat = jnp.concatenate([a.reshape(-1).astype(F32) for a in arrs])
    pad = rows * cols - flat.shape[0]
    return jnp.pad(flat, (0, pad)).reshape(rows, cols)


def _unpack_flat(buf, shapes):
    flat = buf.reshape(-1)
    out, off = [], 0
    for shp in shapes:
        n = 1
        for s in shp:
            n *= s
        out.append(flat[off:off + n].reshape(shp))
        off += n
    return out


def _flat_rows(shapes, cols=1024):
    n = sum(functools.reduce(lambda a, b: a * b, shp, 1) for shp in shapes)
    rows = -(-n // cols)
    return -(-rows // 64) * 64


def _block_diag(w):
    eye = jnp.eye(N_HEADS, dtype=w.dtype)
    return (w[:, :, :, None, :] * eye[None, :, None, :, None]).reshape(w.shape[0], W_GRP, W_GRP)


def _diag_blocks(w):
    w5 = w.reshape(w.shape[0], N_HEADS, HEAD_DIM, N_HEADS, HEAD_DIM)
    return jnp.stack([w5[:, h, :, h, :] for h in range(N_HEADS)], axis=1)


def _stacked_params(w, lbs):
    tril = jnp.tril(jnp.ones((GMLP_CHUNK, GMLP_CHUNK), bool))
    row = lambda a: a.reshape(DEPTH, 1, -1)
    return dict(
        g1=row(w['norm1_g']), g2=row(w['norm2_g']), g3=row(w['norm3_g']),
        a_ln_g=row(w['a_ln_g']), a_ln_b=row(w['a_ln_b']),
        a_wcat=jnp.where(tril, w['a_ws'], 0.0).reshape(DEPTH, N_HEADS * GMLP_CHUNK, GMLP_CHUNK),
        a_bfull=jnp.repeat(jnp.swapaxes(w['a_bs'], 1, 2), HEAD_DIM, axis=2),
        b_cw=w['b_conv_w_full'], b_cb=row(w['b_conv_b']), b_wa=_block_diag(w['b_wa']), b_ba=row(w['b_ba']),
        b_wx=_block_diag(w['b_wx']), b_bx=row(w['b_bx']), b_lam=row(w['b_lam']),
        c_lb=row(lbs), c_ngf=row(jnp.tile(w['c_norm_g'], (1, N_HEADS))),
        d_wd=_block_diag(w['d_w']), d_scale=row(w['d_scale']),
        f_cw=w['ffn_conv_w_full'], f_cb=row(w['ffn_conv_b']),
    )


B_PRM = ('b_cw', 'b_cb', 'b_wa', 'b_ba', 'b_wx', 'b_bx', 'b_lam')


def _layer_fwd(x, p_bf, wb, sp, l):
    n = lambda s: f"l{l}_{s}"
    h, (z,) = _rms_matmul(x, sp['g1'], [wb['w_in']], nt=True, name=n("proj_in"))
    mix = _gmlp_fwd(z, sp['a_ln_g'], sp['a_ln_b'], sp['a_wcat'], sp['a_bfull'], name=n("gmlp"))
    mix, h0s = _rglru_fwd(z, [sp[k] for k in B_PRM], mix, name=n("rglru"))
    mix, sts = _hgrn_fwd(z, sp['c_lb'], sp['c_ngf'], mix, name=n("hgrn"))
    mix = _pool_fwd(z, sp['d_wd'], sp['d_scale'], mix, name=n("pool"))
    x1 = _matmul(mix, wb['w_out'], res=x, name=n("proj_out"))
    h2, hg, hv, a = _up_ffn_fwd(x1, sp['g2'], wb['w_up_g'], wb['w_up_v'], sp['f_cw'], sp['f_cb'], name=n("up_ffn"))
    x2 = _matmul(a, wb['w_down'], res=x1, name=n("down"))
    h3, (gl, pe, x3) = _rms_matmul(x2, sp['g3'], [wb['w_pg']], ple=(p_bf, wb['w_pe']), name=n("ple"))
    saved = dict(x=x, h=h, z=z, h0s=h0s, sts=sts, mix=mix, x1=x1, h2=h2, hg=hg, hv=hv, a=a, x2=x2, h3=h3, gl=gl, pe=pe)
    return x3, saved


def _layer_bwd(dx3, sv, p_bf, wb, sp, l, mid=None):
    n = lambda s: f"l{l}_{s}_bwd"
    gb, gs = {}, {}
    dx2, dx2b, gs['norm3_g'], dpe, dgl = _ple_rms_bwd(dx3, sv['gl'], sv['pe'], wb['w_pg'], sv['x2'], sp['g3'],
                                                      name=n("ple"))
    gb['w_pe'] = _matmul_tn(dpe, p_bf, name=n("ple_emb_w"))
    gb['w_pg'] = _matmul_tn(sv['h3'], dgl, name=n("ple_gate_w"))
    gb['w_down'] = _matmul_tn(sv['a'], dx2b, name=n("down_w"))
    dhg, dhv, gs['f_dwg'], gs['f_dwv'] = _ffn_bwd(sv['hg'], sv['hv'], dx2b, wb['w_down'], sp['f_cw'], sp['f_cb'],
                                                  name=n("ffn_gate"))
    gate_rows = _matmul_tn(dhg, sv['h2'], name=n("up_gate_w"), out_rows=2 * D_FF)
    gb['w_up'] = _matmul_tn(dhv, sv['h2'], name=n("up_val_w"), out_rows=2 * D_FF, row_off=D_FF, into=gate_rows)
    if mid is not None:
        sp = mid(gb, sp)
    dh2 = _matmul(dhg, wb['w_up_g'], name=n("up_gate_x"))
    dx1, dx1b, gs['norm2_g'] = _matmul_rms_bwd(dhv, wb['w_up_v'], sv['x1'], sp['g2'], dx2, res=dh2, name=n("up_val_x"))
    dmix = _matmul(dx1b, wb['w_out'], nt=True, name=n("proj_out_x"))
    gb['w_out'] = _matmul_tn(sv['mix'], dx1b, name=n("proj_out_w"))
    z = sv['z']
    dz, gs['a_ln_g'], gs['a_ln_b'], gs['a_wcat'], gs['a_bfull'] = _gmlp_bwd(
        z, dmix, sp['a_ln_g'], sp['a_ln_b'], sp['a_wcat'], sp['a_bfull'], name=n("gmlp"))
    dz, *dbp = _rglru_bwd(z, dmix, sv['h0s'], [sp[k] for k in B_PRM], dz, name=n("rglru"))
    gs.update(zip(B_PRM, dbp))
    dz, gs['c_lb'], gs['c_ngf'] = _hgrn_bwd(z, dmix, sv['sts'], sp['c_lb'], sp['c_ngf'], dz, name=n("hgrn"))
    dz, gs['d_wd'], gs['d_scale'] = _pool_bwd(z, dmix, sp['d_wd'], sp['d_scale'], dz, name=n("pool"))
    gb['w_in'] = _matmul_tn(dz, sv['h'], name=n("proj_in_w"))
    dx0, _, gs['norm1_g'] = _matmul_rms_bwd(dz, wb['w_in'], sv['x'], sp['g1'], dx1, name=n("proj_in_x"))
    return dx0, gb, gs


SMALL_NAMES = [nm for nm in WEIGHT_NAMES if nm not in BIG_NAMES]
COL_SHARDED = ('w_in', 'w_up', 'w_pe')


def _comm_shards(w):
    return [(jnp.swapaxes(w[nm], 1, 2) if nm in COL_SHARDED else w[nm]).astype(BF16) for nm, _, _ in BIG_COMM]


def _full_weights(gathered):
    out = {nm: g.reshape(N_DEV * r, c) for g, (nm, r, c) in zip(gathered, BIG_COMM)}
    halves = out.pop('w_up').reshape(2, D_FF, D_MODEL)
    out['w_up_g'], out['w_up_v'] = _Sel(halves, 0), _Sel(halves, 1)
    return out


def _small_grads(raw):
    nl = len(raw)
    st = {k: jnp.stack([r[k] for r in raw]) for k in raw[0]}
    tril = jnp.tril(jnp.ones((GMLP_CHUNK, GMLP_CHUNK), bool))
    vec = lambda a: a.reshape(nl, -1)
    out = {nm: vec(st[k]) for nm, k in (('norm1_g', 'norm1_g'), ('norm2_g', 'norm2_g'), ('norm3_g', 'norm3_g'),
                                        ('a_ln_g', 'a_ln_g'), ('a_ln_b', 'a_ln_b'), ('b_conv_b', 'b_cb'),
                                        ('b_ba', 'b_ba'), ('b_bx', 'b_bx'), ('b_lam', 'b_lam'), ('c_lb', 'c_lb'),
                                        ('d_scale', 'd_scale'))}
    out['a_ws'] = jnp.where(tril, st['a_wcat'].reshape(nl, N_HEADS, GMLP_CHUNK, GMLP_CHUNK), 0.0)
    out['a_bs'] = jnp.swapaxes(st['a_bfull'].reshape(nl, GMLP_CHUNK, N_HEADS, HEAD_DIM).sum(-1), 1, 2)
    out['b_conv_w'] = st['b_cw']
    out['b_wa'], out['b_wx'], out['d_w'] = _diag_blocks(st['b_wa']), _diag_blocks(st['b_wx']), _diag_blocks(st['d_wd'])
    out['c_norm_g'] = st['c_ngf'].reshape(nl, N_HEADS, HEAD_DIM).sum(1)
    out['ffn_conv_w'] = jnp.concatenate([st['f_dwg'][:, 0:3], st['f_dwv'][:, 0:3]], axis=2)
    out['ffn_conv_b'] = jnp.concatenate([st['f_dwg'][:, 3], st['f_dwv'][:, 3]], axis=1)
    return out


def _step(w, m, v, x, p, target):
    s = x.shape[1]
    dev = 4 * lax.axis_index("x") + 2 * lax.axis_index("y") + lax.axis_index("c")
    xs = x.reshape(s, D_MODEL)

    shards = _comm_shards(w)
    conv_shapes = [w['b_conv_w'].shape, w['ffn_conv_w'].shape]
    conv_rows = _flat_rows(conv_shapes)
    conv_all = _all_gather(_pack_flat([w['b_conv_w'], w['ffn_conv_w']], conv_rows), name="gather_conv_weights")
    parts = [_unpack_flat(conv_all[d], conv_shapes) for d in range(N_DEV)]
    wf = dict(w)
    wf['b_conv_w_full'] = jnp.concatenate([pt[0] for pt in parts], axis=-1)
    wf['ffn_conv_w_full'] = jnp.concatenate([pt[1] for pt in parts], axis=-1)
    lbs = _lbs_fwd(w['c_lb'], name="hgrn_bounds")

    stacked = _stacked_params(wf, lbs)
    p_all = p.reshape(DEPTH, s, PLE_DIM).astype(BF16)
    xl, saved, wbs, sps = xs, [], [], []
    gathered = _gather_layer(shards, 0, name="l0_gather_weights")
    for l in range(DEPTH):
        sp = {k: _Sel(a, l) for k, a in stacked.items()}
        if l + 1 < DEPTH:
            own = [x[l + 1] for x in shards]
            after = [conv_all, *gathered] if l == 0 else [xl]
            lands = _place_own(own, after, name=f"l{l + 1}_gather_place")
            started = _exchange_start(own, lands, name=f"l{l + 1}_gather_start")
            sp['g1'] = stacked['g1'][l] + started[-1][0, 0]
        wb = _full_weights(gathered)
        p_bf = p_all[l]
        xl, sv = _layer_fwd(xl, p_bf, wb, sp, l)
        if l + 1 < DEPTH:
            gathered = _exchange_wait(started, xl, name=f"l{l + 1}_gather_wait")[1]
        saved.append((sv, p_bf))
        wbs.append(wb)
        sps.append(sp)
    loss_part, dx, dfinal = _loss_head(xl, w['final_g'].reshape(1, D_MODEL), target.reshape(s, D_MODEL), name="loss_head")
    loss = lax.psum(loss_part[0, 0], ("x", "y", "c"))

    dev1 = dev.astype(jnp.int32).reshape(1)
    names = [nm for nm, _, _ in BIG_COMM]

    def start_reduce(grads, name):
        views = [g.reshape(N_DEV, g.shape[0] // N_DEV, g.shape[1]) for g in grads]
        return _exchange_start(views, [lax.empty(g.shape, g.dtype) for g in views], name=name, per_peer=True)

    def finish_reduce(started, after, lname):
        own, lands = _exchange_wait(started, after, name=f"{lname}_reduce_wait")
        return _sum_devices(lands, own, dev1, name=f"{lname}_reduce_sum")

    reduced, small = [None] * DEPTH, [None] * DEPTH
    pending = None
    for l in range(DEPTH - 1, 0, -1):
        sv, p_bf = saved[l]
        sp = sps[l]
        if pending is not None:
            sp = dict(sp, g3=stacked['g3'][l] + pending[-1][0, 0])
        dx, gb, small[l] = _layer_bwd(dx, sv, p_bf, wbs[l], sp, l)
        if pending is not None:
            reduced[l + 1] = finish_reduce(pending, dx, f"l{l + 1}")
        pending = start_reduce([gb[nm] for nm in names], f"l{l}_reduce_start")
    early = ('w_up', 'w_down', 'w_pe', 'w_pg')
    mid_started = []

    def mid(gb, sp):
        mid_started.append(start_reduce([gb[nm] for nm in early], "l0_reduce_start"))
        return dict(sp, g2=stacked['g2'][0] + mid_started[0][-1][0, 0])

    upper_names = [nm for nm in SMALL_NAMES if nm != 'final_g']
    low_names = upper_names + ['final_g']
    upper = _small_grads(small[1:])
    upper_shapes = [upper[nm].shape for nm in upper_names]
    upper_packed = [_pack_flat([upper[nm] for nm in upper_names], _flat_rows(upper_shapes))]
    upper_started = _exchange_start(upper_packed, _place_own(upper_packed, [], name="upper_small_grads_place"),
                                    name="upper_small_grads_start")

    sv, p_bf = saved[0]
    g3 = stacked['g3'][0] + pending[-1][0, 0] + upper_started[-1][0, 0]
    dx, gb, small[0] = _layer_bwd(dx, sv, p_bf, wbs[0], dict(sps[0], g3=g3), 0, mid=mid)
    reduced[1] = finish_reduce(pending, dx, "l1")
    late = dict(zip(('w_in', 'w_out'), _reduce_layer([gb['w_in'], gb['w_out']], 0)))
    late.update(zip(early, finish_reduce(mid_started[0], late['w_in'], "l0")))
    reduced[0] = [late[nm] for nm in names]
    grad_x = dx.reshape(1, s, D_MODEL)
    low = _small_grads(small[:1])
    low['final_g'] = dfinal.reshape(D_MODEL)
    low_shapes = [low[nm].shape for nm in low_names]
    low_all = _all_gather(_pack_flat([low[nm] for nm in low_names], _flat_rows(low_shapes)), name="gather_small_grads")
    low_sum = dict(zip(low_names, _unpack_flat(_sum_slots(low_all, name="sum_small_grads"), low_shapes)))
    upper_all = _exchange_wait(upper_started, low_all, name="upper_small_grads_wait")[1][0]
    upper_sum = dict(zip(upper_names, _unpack_flat(_sum_slots(upper_all, name="sum_upper_small_grads"), upper_shapes)))
    gsmall = {nm: jnp.concatenate([low_sum[nm], upper_sum[nm]], axis=0) for nm in upper_names}
    gsmall['c_lb'] = _lbs_bwd(w['c_lb'], gsmall['c_lb'], name="hgrn_bounds_bwd")
    gsmall['final_g'] = low_sum['final_g']
    for nm in ('b_conv_w', 'ffn_conv_w'):
        width = w[nm].shape[-1]
        gsmall[nm] = lax.dynamic_slice_in_dim(gsmall[nm], dev * width, width, axis=2)

    grads, delta, new_m, new_v = {}, {}, {}, {}
    for a, (nm, _, _) in enumerate(BIG_COMM):
        t = (lambda x: jnp.swapaxes(x, 1, 2)) if nm in COL_SHARDED else (lambda x: x)
        g = jnp.stack([reduced[l][a] for l in range(DEPTH)])
        d, nm_, nv_ = _adamw(t(w[nm]), g, t(m[nm]), t(v[nm]), name=f"adamw_{nm}")
        grads[nm], delta[nm], new_m[nm], new_v[nm] = t(g), t(d), t(nm_), t(nv_)

    shapes = [w[nm].shape for nm in SMALL_NAMES]
    rows = _flat_rows(shapes)
    pk = lambda t: _pack_flat([t[nm] for nm in SMALL_NAMES], rows)
    d, nm_, nv_ = _adamw(pk(w), pk(gsmall), pk(m), pk(v), name="adamw_small")
    for nm, dd, mm_, vv_ in zip(SMALL_NAMES, _unpack_flat(d, shapes), _unpack_flat(nm_, shapes), _unpack_flat(nv_, shapes)):
        grads[nm], delta[nm], new_m[nm], new_v[nm] = gsmall[nm], dd, mm_, vv_

    return (loss, grad_x, *[grads[nm] for nm in WEIGHT_NAMES], *[delta[nm] for nm in WEIGHT_NAMES],
            *[new_m[nm] for nm in WEIGHT_NAMES], *[new_v[nm] for nm in WEIGHT_NAMES])


def kernel(x, p, norm1_g, w_in, a_ln_g, a_ln_b, a_ws, a_bs, b_conv_w, b_conv_b, b_wa, b_ba, b_wx, b_bx, b_lam, c_lb, c_norm_g, d_w, d_scale, w_out, norm2_g, w_up, ffn_conv_w, ffn_conv_b, w_down, norm3_g, w_pe, w_pg, final_g, loss_target, m_norm1_g, m_w_in, m_a_ln_g, m_a_ln_b, m_a_ws, m_a_bs, m_b_conv_w, m_b_conv_b, m_b_wa, m_b_ba, m_b_wx, m_b_bx, m_b_lam, m_c_lb, m_c_norm_g, m_d_w, m_d_scale, m_w_out, m_norm2_g, m_w_up, m_ffn_conv_w, m_ffn_conv_b, m_w_down, m_norm3_g, m_w_pe, m_w_pg, m_final_g, v_norm1_g, v_w_in, v_a_ln_g, v_a_ln_b, v_a_ws, v_a_bs, v_b_conv_w, v_b_conv_b, v_b_wa, v_b_ba, v_b_wx, v_b_bx, v_b_lam, v_c_lb, v_c_norm_g, v_d_w, v_d_scale, v_w_out, v_norm2_g, v_w_up, v_ffn_conv_w, v_ffn_conv_b, v_w_down, v_norm3_g, v_w_pe, v_w_pg, v_final_g):
    w = dict(norm1_g=norm1_g, w_in=w_in, a_ln_g=a_ln_g, a_ln_b=a_ln_b, a_ws=a_ws, a_bs=a_bs, b_conv_w=b_conv_w, b_conv_b=b_conv_b, b_wa=b_wa, b_ba=b_ba, b_wx=b_wx, b_bx=b_bx, b_lam=b_lam, c_lb=c_lb, c_norm_g=c_norm_g, d_w=d_w, d_scale=d_scale, w_out=w_out, norm2_g=norm2_g, w_up=w_up, ffn_conv_w=ffn_conv_w, ffn_conv_b=ffn_conv_b, w_down=w_down, norm3_g=norm3_g, w_pe=w_pe, w_pg=w_pg, final_g=final_g)
    m = dict(norm1_g=m_norm1_g, w_in=m_w_in, a_ln_g=m_a_ln_g, a_ln_b=m_a_ln_b, a_ws=m_a_ws, a_bs=m_a_bs, b_conv_w=m_b_conv_w, b_conv_b=m_b_conv_b, b_wa=m_b_wa, b_ba=m_b_ba, b_wx=m_b_wx, b_bx=m_b_bx, b_lam=m_b_lam, c_lb=m_c_lb, c_norm_g=m_c_norm_g, d_w=m_d_w, d_scale=m_d_scale, w_out=m_w_out, norm2_g=m_norm2_g, w_up=m_w_up, ffn_conv_w=m_ffn_conv_w, ffn_conv_b=m_ffn_conv_b, w_down=m_w_down, norm3_g=m_norm3_g, w_pe=m_w_pe, w_pg=m_w_pg, final_g=m_final_g)
    v = dict(norm1_g=v_norm1_g, w_in=v_w_in, a_ln_g=v_a_ln_g, a_ln_b=v_a_ln_b, a_ws=v_a_ws, a_bs=v_a_bs, b_conv_w=v_b_conv_w, b_conv_b=v_b_conv_b, b_wa=v_b_wa, b_ba=v_b_ba, b_wx=v_b_wx, b_bx=v_b_bx, b_lam=v_b_lam, c_lb=v_c_lb, c_norm_g=v_c_norm_g, d_w=v_d_w, d_scale=v_d_scale, w_out=v_w_out, norm2_g=v_norm2_g, w_up=v_w_up, ffn_conv_w=v_ffn_conv_w, ffn_conv_b=v_ffn_conv_b, w_down=v_w_down, norm3_g=v_norm3_g, w_pe=v_w_pe, w_pg=v_w_pg, final_g=v_final_g)
    return _step(w, m, v, x, p, loss_target)
```

```python
import functools

import jax
import jax.numpy as jnp
from jax import lax
from jax.experimental import pallas as pl
from jax.experimental.pallas import tpu as pltpu

F32 = jnp.float32
BF16 = jnp.bfloat16
MESH = pl.DeviceIdType.MESH

D_MODEL = 1024
DEPTH = 4
PLE_DIM = 256
W_GRP = 256
N_HEADS = 4
HEAD_DIM = 64
GMLP_CHUNK = 128
RGLRU_C = 8.0
HGRN_CHUNK = 64
HGRN_SUB = 32
HGRN_STEP_CHUNKS = 8
POOL_WINDOWS = (2, 4, 8, 16)
D_FF = 2816
D_PROJ = 2304
EPS = 1e-6
ADAM_LR = 0.001
ADAM_B1 = 0.9
ADAM_B2 = 0.999
ADAM_EPS = 1e-08
ADAM_WD = 0.01
ADAM_STEP = 10

N_DEV = 8
MIB = 2 ** 20
V7X_VMEM_BYTES = 64 * MIB
HGRN_EXP_CLAMP = 60.0

WEIGHT_NAMES = ['norm1_g', 'w_in', 'a_ln_g', 'a_ln_b', 'a_ws', 'a_bs', 'b_conv_w', 'b_conv_b', 'b_wa', 'b_ba', 'b_wx',
                'b_bx', 'b_lam', 'c_lb', 'c_norm_g', 'd_w', 'd_scale', 'w_out', 'norm2_g', 'w_up', 'ffn_conv_w',
                'ffn_conv_b', 'w_down', 'norm3_g', 'w_pe', 'w_pg', 'final_g']
BIG_NAMES = ('w_in', 'w_out', 'w_up', 'w_down', 'w_pe', 'w_pg')


def _vmem_limit(block_bytes):
    want = 2 * block_bytes + 24 * MIB
    return int(min(max(want, 32 * MIB), V7X_VMEM_BYTES - 8 * MIB))


def _in_hbm(x):
    return pltpu.with_memory_space_constraint(x, pltpu.HBM)


def _out_hbm(s):
    return pltpu.HBM(s.shape, s.dtype)


def _pcall(body, *, name, out_shape, grid=None, in_specs=None, out_specs=None, scratch_shapes=(),
           semantics=None, block_bytes=0, aliases=None, pin=True):
    kw = {} if aliases is None else {"input_output_aliases": aliases}
    if pin:
        out_shape = tuple(_out_hbm(s) for s in out_shape) if isinstance(out_shape, (tuple, list)) else _out_hbm(out_shape)
    if grid is not None:
        kw["grid"] = grid
    if in_specs is not None:
        kw["in_specs"] = in_specs
    if out_specs is not None:
        kw["out_specs"] = out_specs
    params = pltpu.CompilerParams(dimension_semantics=semantics, vmem_limit_bytes=_vmem_limit(block_bytes))
    call = pl.pallas_call(body, name=name, out_shape=out_shape, scratch_shapes=list(scratch_shapes),
                          compiler_params=params, **kw)
    return (lambda *args: call(*[_in_hbm(a) for a in args])) if pin else call


def _pick(n, cands):
    for c in cands:
        if n % c == 0:
            return c
    return n


def _nbytes(shape, dtype):
    n = 1
    for s in shape:
        n *= s
    return n * jnp.dtype(dtype).itemsize


def _sds(shape, dtype):
    return jax.ShapeDtypeStruct(tuple(shape), dtype)


class _Sel:
    def __init__(self, arr, *idx):
        self.arr, self.idx = arr, tuple(idx)
        self.shape = arr.shape[len(idx):]
        self.ndim = len(self.shape)
        self.dtype = arr.dtype


def _arr(a):
    return a.arr if isinstance(a, _Sel) else a


def _spec(a, block=None, index=None):
    block = tuple(a.shape) if block is None else tuple(block)
    index = (lambda *g: (0,) * len(block)) if index is None else index
    if isinstance(a, _Sel):
        lead = a.idx
        return pl.BlockSpec((None,) * len(lead) + block, lambda *g: lead + tuple(index(*g)))
    return pl.BlockSpec(block, lambda *g: tuple(index(*g)))


def _ospec(a):
    return pl.BlockSpec(tuple(a.shape), lambda *g: (0,) * a.ndim)


def _rows_of(shape):
    return lax.broadcasted_iota(jnp.int32, shape, 0)


def _lanes_of(shape):
    return lax.broadcasted_iota(jnp.int32, shape, 1)


def _sdn(x, k, fill):
    n = x.shape[0]
    return jnp.where(_rows_of(x.shape) >= k, pltpu.roll(x, k % n, 0), fill)


def _sup(x, k, fill):
    n = x.shape[0]
    return jnp.where(_rows_of(x.shape) < n - k, pltpu.roll(x, (n - k) % n, 0), fill)


@functools.partial(jax.custom_vjp, nondiff_argnums=(1,))
def _shift_dn(x, k):
    return pltpu.roll(x, k, 0)


def _shift_dn_fwd(x, k):
    return pltpu.roll(x, k, 0), None


def _shift_dn_bwd(k, _, g):
    return (pltpu.roll(g, g.shape[0] - k, 0),)


_shift_dn.defvjp(_shift_dn_fwd, _shift_dn_bwd)


SUBLANES = 8


def _lin_scan_impl(a, b, h0):
    n = a.shape[0]
    pos = _rows_of(a.shape) % SUBLANES
    aa, bb = a, b
    k = 1
    while k < SUBLANES:
        keep = pos >= k
        bb = bb + jnp.where(keep, aa * pltpu.roll(bb, k, 0), 0.0)
        aa = aa * jnp.where(keep, pltpu.roll(aa, k, 0), 1.0)
        k *= 2
    out, carry = [], h0
    for r in range(n // SUBLANES):
        rows = slice(r * SUBLANES, (r + 1) * SUBLANES)
        hr = bb[rows] + aa[rows] * carry
        out.append(hr)
        carry = hr[SUBLANES - 1:]
    return jnp.concatenate(out, axis=0)


@jax.custom_vjp
def _lin_scan(a, b, h0):
    return _lin_scan_impl(a, b, h0)


def _lin_scan_fwd(a, b, h0):
    h = _lin_scan_impl(a, b, h0)
    return h, (a, h, h0)


def _lin_scan_bwd(res, g):
    a, h, h0 = res
    n = a.shape[0]
    pos = _rows_of(a.shape) % SUBLANES
    cc, gg = _sup(a, 1, 0.0), g
    k = 1
    while k < SUBLANES:
        keep = pos < SUBLANES - k
        gg = gg + jnp.where(keep, cc * pltpu.roll(gg, n - k, 0), 0.0)
        cc = cc * jnp.where(keep, pltpu.roll(cc, n - k, 0), 1.0)
        k *= 2
    out, carry = [], jnp.zeros_like(h0)
    for r in range(n // SUBLANES - 1, -1, -1):
        rows = slice(r * SUBLANES, (r + 1) * SUBLANES)
        gr = gg[rows] + cc[rows] * carry
        out.append(gr)
        carry = gr[:1]
    gg = jnp.concatenate(out[::-1], axis=0)
    first = _rows_of(a.shape) == 0
    hprev = jnp.where(first, h0, _sdn(h, 1, 0.0))
    dh0 = jnp.sum(jnp.where(first, a * gg, 0.0), axis=0, keepdims=True)
    return gg * hprev, gg, dh0


_lin_scan.defvjp(_lin_scan_fwd, _lin_scan_bwd)


def _cumsum_sub_impl(x):
    pos = _rows_of(x.shape) % HGRN_SUB
    k = 1
    while k < HGRN_SUB:
        x = x + jnp.where(pos >= k, pltpu.roll(x, k, 0), 0.0)
        k *= 2
    return x


@jax.custom_vjp
def _cumsum_sub(x):
    return _cumsum_sub_impl(x)


def _cumsum_sub_fwd(x):
    return _cumsum_sub_impl(x), None


def _cumsum_sub_bwd(_, g):
    n = g.shape[0]
    pos = _rows_of(g.shape) % HGRN_SUB
    k = 1
    while k < HGRN_SUB:
        g = g + jnp.where(pos < HGRN_SUB - k, pltpu.roll(g, n - k, 0), 0.0)
        k *= 2
    return (g,)


_cumsum_sub.defvjp(_cumsum_sub_fwd, _cumsum_sub_bwd)


def _dot(a, b, ca, cb):
    return lax.dot_general(a.astype(BF16), b.astype(BF16), (((ca,), (cb,)), ((), ())), preferred_element_type=F32)


@jax.custom_vjp
def _mm(a, b):
    return _dot(a, b, 1, 0)


def _mm_fwd(a, b):
    return _dot(a, b, 1, 0), (a, b)


def _mm_bwd(res, g):
    a, b = res
    return _dot(g, b, 1, 1), _dot(a, g, 0, 0)


_mm.defvjp(_mm_fwd, _mm_bwd)


@jax.custom_vjp
def _mm_nt(a, b):
    return _dot(a, b, 1, 1)


def _mm_nt_fwd(a, b):
    return _dot(a, b, 1, 1), (a, b)


def _mm_nt_bwd(res, g):
    a, b = res
    return _dot(g, b, 1, 0), _dot(g, a, 0, 0)


_mm_nt.defvjp(_mm_nt_fwd, _mm_nt_bwd)


@jax.custom_vjp
def _mm_tn(a, b):
    return _dot(a, b, 0, 0)


def _mm_tn_fwd(a, b):
    return _dot(a, b, 0, 0), (a, b)


def _mm_tn_bwd(res, g):
    a, b = res
    return _dot(b, g, 1, 1), _dot(a, g, 1, 0)


_mm_tn.defvjp(_mm_tn_fwd, _mm_tn_bwd)


def _head_mask(shape, h):
    return (_lanes_of(shape) // HEAD_DIM) == h


def _stack_heads(x):
    return jnp.concatenate([jnp.where(_head_mask(x.shape, h), x, 0.0) for h in range(N_HEADS)], axis=0)


def _unstack_heads(p):
    r = p.shape[0] // N_HEADS
    out = None
    for h in range(N_HEADS):
        blk = p[h * r:(h + 1) * r]
        term = jnp.where(_head_mask(blk.shape, h), blk, 0.0)
        out = term if out is None else out + term
    return out


def _segmean_impl(x):
    n = x.shape[1]
    same = (lax.broadcasted_iota(jnp.int32, (n, n), 0) // HEAD_DIM) == (lax.broadcasted_iota(jnp.int32, (n, n), 1) // HEAD_DIM)
    m = jnp.where(same, 1.0 / HEAD_DIM, 0.0).astype(BF16)
    hi = x.astype(BF16)
    lo = (x - hi.astype(F32)).astype(BF16)
    dn = (((1,), (0,)), ((), ()))
    return (lax.dot_general(hi, m, dn, preferred_element_type=F32)
            + lax.dot_general(lo, m, dn, preferred_element_type=F32))


@jax.custom_vjp
def _segmean(x):
    return _segmean_impl(x)


def _segmean_fwd(x):
    return _segmean_impl(x), None


def _segmean_bwd(_, g):
    return (_segmean_impl(g),)


_segmean.defvjp(_segmean_fwd, _segmean_bwd)


GELU_C = 0.7978845608028654
GELU_A = 0.044715


@jax.custom_vjp
def _gelu(x):
    return 0.5 * x * (1.0 + jnp.tanh(GELU_C * x * (1.0 + GELU_A * (x * x))))


def _gelu_fwd(x):
    x2 = x * x
    t = jnp.tanh(GELU_C * x * (1.0 + GELU_A * x2))
    return 0.5 * x * (1.0 + t), (x, x2, t)


def _gelu_bwd(res, g):
    x, x2, t = res
    half = 0.5 * (1.0 + t)
    return (g * (half + (0.5 * GELU_C) * x * (1.0 - t * t) * (1.0 + (3.0 * GELU_A) * x2)),)


_gelu.defvjp(_gelu_fwd, _gelu_bwd)


def _log1p(u):
    w = 1.0 + u
    return jnp.where(w == 1.0, u, jnp.log(w) * (u / (w - 1.0)))


def _softplus(y):
    return jnp.maximum(y, 0.0) + _log1p(jnp.exp(-jnp.abs(y)))


def _rms(x, g):
    return x * lax.rsqrt(jnp.mean(x * x, axis=-1, keepdims=True) + EPS) * g


def _gmlp_chunk(zu, zv, ln_g, ln_b, wcat, bfull):
    u = _gelu(zu)
    v = _gelu(zv)
    mu = jnp.mean(v, axis=-1, keepdims=True)
    var = jnp.mean(jnp.square(v - mu), axis=-1, keepdims=True)
    vn = (v - mu) * lax.rsqrt(var + EPS) * ln_g + ln_b
    sv = _unstack_heads(_mm(wcat, vn)) + bfull
    return u * sv


def _rglru_tile(xb_ext, gb, h0, cw, cb, wa, ba, wx, bx, lam):
    xc = (cb + cw[0:1] * _shift_dn(xb_ext, 3) + cw[1:2] * _shift_dn(xb_ext, 2) + cw[2:3] * _shift_dn(xb_ext, 1)
          + cw[3:4] * xb_ext)[8:]
    r = jax.nn.sigmoid(_mm(xc, wa) + ba)
    i = jax.nn.sigmoid(_mm(xc, wx) + bx)
    log_a = (-RGLRU_C) * r * _softplus(-lam)
    a = jnp.exp(log_a)
    mult = jnp.sqrt(-jnp.tanh(log_a) * (a * a + 1.0))
    h = _lin_scan(a, mult * (i * xc), h0)
    y = h * _gelu(gb)
    h_last = jnp.sum(jnp.where(_rows_of(h.shape) == h.shape[0] - 1, h, 0.0), axis=0, keepdims=True)
    return y, h_last


def _pool_tile(xd_ext, inv, wd, scale):
    s1 = xd_ext + _shift_dn(xd_ext, 1)
    s2 = s1 + _shift_dn(s1, 2)
    s3 = s2 + _shift_dn(s2, 4)
    s4 = s3 + _shift_dn(s3, 8)
    grp = _lanes_of(xd_ext.shape) // HEAD_DIM
    win = jnp.where(grp == 0, s1, jnp.where(grp == 1, s2, jnp.where(grp == 2, s3, s4)))
    pooled = win[16:] * inv - xd_ext[16:]
    return _mm(pooled, wd) * scale


def _hgrn_chunk(q, f, i, g, st, lb, ngf):
    n = q.shape[0]
    nsub = n // HGRN_SUB
    qs = jax.nn.silu(q)
    fg = lb + (1.0 - lb) * jax.nn.sigmoid(f)
    lf = jnp.log(fg)
    k = 1.0 - fg
    bl = _cumsum_sub(lf)
    row = _rows_of(q.shape)
    blk = row // HGRN_SUB
    betas = [jnp.zeros_like(lb)]
    for s in range(nsub):
        tot = jnp.sum(jnp.where(row == s * HGRN_SUB + HGRN_SUB - 1, bl, 0.0), axis=0, keepdims=True)
        betas.append(betas[-1] + tot)
    b_end = betas[nsub]
    beta_full = jnp.zeros_like(q)
    for s in range(1, nsub):
        beta_full = jnp.where(blk == s, betas[s], beta_full)
    qh = qs * jnp.exp(bl)
    qt = qh * jnp.exp(beta_full)
    b_all = beta_full + bl
    kt = k * jnp.exp(b_end - b_all)
    outs = []
    for s in range(nsub):
        kh = k * jnp.exp(jnp.minimum(betas[s] - b_all, HGRN_EXP_CLAMP))
        qstk = _stack_heads(qh[s * HGRN_SUB:(s + 1) * HGRN_SUB])
        att = _mm_nt(qstk, kh)
        ar = _rows_of(att.shape) % HGRN_SUB + s * HGRN_SUB
        att = jnp.where(_lanes_of(att.shape) <= ar, att, 0.0)
        outs.append(_unstack_heads(_mm(att, i)))
    o = jnp.concatenate(outs, axis=0) + _mm_nt(qt, st)
    same = (_rows_of(st.shape) // HEAD_DIM) == (_lanes_of(st.shape) // HEAD_DIM)
    st_new = st * jnp.exp(b_end) + jnp.where(same, _mm_tn(i, kt), 0.0)
    on = o * lax.rsqrt(_segmean(o * o) + EPS) * ngf
    return on * jax.nn.silu(g), st_new


def _ffn_tile(eg, ev, wg, bg, wv, bv):
    gt = (bg + wg[0:1] * _shift_dn(eg, 2) + wg[1:2] * _shift_dn(eg, 1) + wg[2:3] * eg)[8:]
    val = (bv + wv[0:1] * _shift_dn(ev, 2) + wv[1:2] * _shift_dn(ev, 1) + wv[2:3] * ev)[8:]
    return _gelu(gt) * val


MXU_WIDTH = 256
MATMUL_BLOCK_BUDGET = 18 * MIB


def _matmul_tiles(m, k, n, a_dtype, b_dtype, out_dtype, has_res):
    best = None
    for tm in (2048, 1024, 512, 256):
        if m % tm:
            continue
        for tn in (1024, 768, 1408, 512, 256, 128):
            if n % tn:
                continue
            blk = (_nbytes((tm, k), a_dtype) + _nbytes((k, tn), b_dtype) + _nbytes((tm, tn), out_dtype)
                   + (_nbytes((tm, tn), F32) if has_res else 0))
            if blk > MATMUL_BLOCK_BUDGET:
                continue
            waste = -(-tn // MXU_WIDTH) * MXU_WIDTH / tn
            cost = (m // tm) * (n // tn) + 64 * (waste - 1.0) + blk / 2 ** 30
            if best is None or cost < best[0]:
                best = (cost, tm, tn, blk)
    assert best is not None, (m, k, n)
    return best[1:]


def _matmul(a, b, *, name, nt=False, res=None, out_dtype=F32):
    m, k = a.shape
    n = b.shape[0] if nt else b.shape[1]
    tm, tn, blk = _matmul_tiles(m, k, n, a.dtype, b.dtype, out_dtype, res is not None)
    dims = (((1,), (1,)), ((), ())) if nt else (((1,), (0,)), ((), ()))

    def body(*refs):
        if res is None:
            a_ref, b_ref, o_ref = refs
        else:
            a_ref, b_ref, r_ref, o_ref = refs
        acc = lax.dot_general(a_ref[...], b_ref[...], dims, preferred_element_type=F32)
        if res is not None:
            acc = acc + r_ref[...]
        o_ref[...] = acc.astype(out_dtype)

    in_specs = [pl.BlockSpec((tm, k), lambda i, j: (i, 0)),
                _spec(b, (tn, k), lambda i, j: (j, 0)) if nt else _spec(b, (k, tn), lambda i, j: (0, j))]
    args = [a, _arr(b)]
    if res is not None:
        in_specs.append(pl.BlockSpec((tm, tn), lambda i, j: (i, j)))
        args.append(res)
    return _pcall(body, name=name, out_shape=_sds((m, n), out_dtype), grid=(m // tm, n // tn), in_specs=in_specs,
                  out_specs=pl.BlockSpec((tm, tn), lambda i, j: (i, j)), semantics=("parallel", "parallel"),
                  block_bytes=blk + _nbytes((tm, tn), F32))(*args)


def _matmul_rms_bwd(a, b, x, g, dres, *, name, nt=False, res=None):
    m, k = a.shape
    n = b.shape[0] if nt else b.shape[1]
    tm = _pick(m, (512, 256))
    dims = (((1,), (1,)), ((), ())) if nt else (((1,), (0,)), ((), ()))

    def body(*refs):
        a_ref, b_ref, x_ref, g_ref, dr_ref = refs[:5]
        dx_ref, dxb_ref, dg_ref = refs[-3:]
        dh = lax.dot_general(a_ref[...], b_ref[...], dims, preferred_element_type=F32)
        if res is not None:
            dh = dh + refs[5][...]
        _, vjp = jax.vjp(_rms, x_ref[...], g_ref[...])
        dxn, dg = vjp(dh)
        dx = dr_ref[...] + dxn
        dx_ref[...] = dx
        dxb_ref[...] = dx.astype(BF16)
        _acc_out(dg_ref, dg, pl.program_id(0) == 0)

    row = pl.BlockSpec((tm, n), lambda i: (i, 0))
    vec = pl.BlockSpec((1, n), lambda i: (0, 0))
    in_specs = [pl.BlockSpec((tm, k), lambda i: (i, 0)),
                _spec(b, (n, k), lambda i: (0, 0)) if nt else _spec(b, (k, n), lambda i: (0, 0)), row, _spec(g), row]
    args = [a, _arr(b), x, _arr(g), dres]
    if res is not None:
        in_specs.append(row)
        args.append(res)
    blk = _nbytes((tm, k), a.dtype) + _nbytes((k, n), b.dtype) + 6 * _nbytes((tm, n), F32)
    return _pcall(body, name=name, out_shape=(_sds((m, n), F32), _sds((m, n), BF16), _sds((1, n), F32)), grid=(m // tm,),
                  in_specs=in_specs, out_specs=(row, row, vec), semantics=("arbitrary",), block_bytes=blk)(*args)


def _ple_rms_bwd(dx3, gl, pe, w_pg, x, g, *, name):
    m, n = dx3.shape
    tm = _pick(m, (512, 256))

    def body(d3_ref, gl_ref, pe_ref, w_ref, x_ref, g_ref, dx_ref, dxb_ref, dg_ref, dpe_ref, dgl_ref):
        gate = jax.nn.sigmoid(gl_ref[...])
        d3 = d3_ref[...]
        dpe_ref[...] = (d3 * gate).astype(BF16)
        dgl = (d3 * pe_ref[...] * gate * (1.0 - gate)).astype(BF16)
        dgl_ref[...] = dgl
        dh = lax.dot_general(dgl, w_ref[...], (((1,), (1,)), ((), ())), preferred_element_type=F32)
        _, vjp = jax.vjp(_rms, x_ref[...], g_ref[...])
        dxn, dg = vjp(dh)
        dx = d3 + dxn
        dx_ref[...] = dx
        dxb_ref[...] = dx.astype(BF16)
        _acc_out(dg_ref, dg, pl.program_id(0) == 0)

    row = pl.BlockSpec((tm, n), lambda i: (i, 0))
    vec = pl.BlockSpec((1, n), lambda i: (0, 0))
    blk = _nbytes((n, n), BF16) + 9 * _nbytes((tm, n), F32)
    return _pcall(body, name=name,
                  out_shape=(_sds((m, n), F32), _sds((m, n), BF16), _sds((1, n), F32), _sds((m, n), BF16), _sds((m, n), BF16)),
                  grid=(m // tm,), in_specs=[row, row, row, _spec(w_pg, (n, n), lambda i: (0, 0)), row, _spec(g)],
                  out_specs=(row, row, vec, row, row), semantics=("arbitrary",), block_bytes=blk)(
                      dx3, gl, pe, _arr(w_pg), x, _arr(g))


def _matmul_tn(a, b, *, name, out_dtype=BF16, out_rows=None, row_off=0, into=None):
    m, k1 = a.shape
    n = b.shape[1]
    tk = _pick(k1, (512, 256, 128))
    off = row_off // tk
    assert off * tk == row_off

    def body(a_ref, b_ref, *rest):
        rest[-1][...] = lax.dot_general(a_ref[...], b_ref[...], (((0,), (0,)), ((), ())),
                                        preferred_element_type=F32).astype(out_dtype)

    blk = 2 * _nbytes((m, tk), a.dtype) + _nbytes((m, n), b.dtype) + _nbytes((tk, n), F32)
    in_specs = [pl.BlockSpec((m, tk), lambda i: (0, i)), pl.BlockSpec((m, n), lambda i: (0, 0))]
    args = [a, b]
    if into is not None:
        in_specs.append(HBM_SPEC)
        args.append(into)
    return _pcall(body, name=name, out_shape=_sds((out_rows or k1, n), out_dtype), grid=(k1 // tk,), in_specs=in_specs,
                  out_specs=pl.BlockSpec((tk, n), lambda i: (i + off, 0)), semantics=("parallel",), block_bytes=blk,
                  aliases=None if into is None else {2: 0})(*args)


def _rms_matmul(x, g, bs, *, name, nt=False, ple=None):
    m, d = x.shape
    n = bs[0].shape[0] if nt else bs[0].shape[1]
    nb = len(bs)
    nout = nb if ple is None else 3
    best = None
    for tm_c in (1024, 512, 256):
        for tn_c in (1408, 1024, 768, 512, 256, 128):
            if m % tm_c or n % tn_c:
                continue
            blk_c = (_nbytes((tm_c, d), F32) + 2 * _nbytes((tm_c, d), BF16) + nb * _nbytes((d, tn_c), BF16)
                     + (nout + 1) * _nbytes((tm_c, tn_c), F32))
            steps = (m // tm_c) * (n // tn_c)
            if blk_c <= MATMUL_BLOCK_BUDGET and (best is None or steps < best[0]):
                best = (steps, tm_c, tn_c, blk_c)
    _, tm, tn, blk = best
    dims = (((1,), (1,)), ((), ())) if nt else (((1,), (0,)), ((), ()))

    def body(*refs):
        x_ref, g_ref, b_refs = refs[0], refs[1], refs[2:2 + nb]
        rest = refs[2 + nb:]
        h_scr = rest[-1]
        j = pl.program_id(1)

        @pl.when(j == 0)
        def _():
            h = _rms(x_ref[...], g_ref[...]).astype(BF16)
            h_scr[...] = h
            rest[-2 - nb - (2 if ple else 0)][...] = h

        h = h_scr[...]
        if ple is None:
            for k in range(nb):
                rest[-1 - nb + k][...] = lax.dot_general(h, b_refs[k][...], dims, preferred_element_type=F32)
        else:
            p_ref, wpe_ref, xt_ref = rest[0], rest[1], rest[2]
            gl_ref, pe_ref, out_ref = rest[-4], rest[-3], rest[-2]
            gl = lax.dot_general(h, b_refs[0][...], dims, preferred_element_type=F32)
            pe = lax.dot_general(p_ref[...], wpe_ref[...], (((1,), (1,)), ((), ())), preferred_element_type=F32)
            gl_ref[...] = gl
            pe_ref[...] = pe
            out_ref[...] = xt_ref[...] + pe * jax.nn.sigmoid(gl)

    row = pl.BlockSpec((tm, d), lambda i, j: (i, 0))
    tile = pl.BlockSpec((tm, tn), lambda i, j: (i, j))
    in_specs = [row, _spec(g)] + [_spec(b, (tn, d), lambda i, j: (j, 0)) if nt else _spec(b, (d, tn), lambda i, j: (0, j))
                                  for b in bs]
    args = [x, _arr(g)] + [_arr(b) for b in bs]
    out_shape, out_specs = [_sds((m, d), BF16)], [row]
    if ple is None:
        out_shape += [_sds((m, n), F32)] * nb
        out_specs += [tile] * nb
    else:
        p, wpe = ple
        in_specs += [pl.BlockSpec((tm, p.shape[1]), lambda i, j: (i, 0)), _spec(wpe, (tn, p.shape[1]), lambda i, j: (j, 0)),
                     tile]
        args += [p, _arr(wpe), x]
        out_shape += [_sds((m, n), F32)] * 3
        out_specs += [tile] * 3
    outs = _pcall(body, name=name, out_shape=tuple(out_shape), grid=(m // tm, n // tn), in_specs=in_specs,
                  out_specs=tuple(out_specs), scratch_shapes=[pltpu.VMEM((tm, d), BF16)],
                  semantics=("parallel", "arbitrary"), block_bytes=blk)(*args)
    return outs[0], list(outs[1:])


def _up_ffn_fwd(x, g, wg, wv, cwf, cbf, *, name):
    m, d = x.shape
    n = wg.shape[0]
    tm = _pick(m, (256, 128))
    tn = _pick(n, (1408, 256, 128))
    nj = n // tn
    dims = (((1,), (1,)), ((), ()))

    def body(x_ref, g_ref, wg_ref, wv_ref, tg_ref, bg_ref, tv_ref, bv_ref, h_ref, hg_ref, hv_ref, a_ref, cg_scr, cv_scr):
        i = pl.program_id(1)
        h = _rms(x_ref[...], g_ref[...]).astype(BF16)
        h_ref[...] = h
        hg = lax.dot_general(h, wg_ref[...], dims, preferred_element_type=F32)
        hv = lax.dot_general(h, wv_ref[...], dims, preferred_element_type=F32)
        hg_ref[...] = hg
        hv_ref[...] = hv
        eg = jnp.concatenate([jnp.where(i == 0, 0.0, cg_scr[...]), hg], axis=0)
        ev = jnp.concatenate([jnp.where(i == 0, 0.0, cv_scr[...]), hv], axis=0)
        a_ref[...] = _ffn_tile(eg, ev, tg_ref[...], bg_ref[...], tv_ref[...], bv_ref[...]).astype(BF16)
        cg_scr[...] = hg[tm - 8:]
        cv_scr[...] = hv[tm - 8:]

    row = pl.BlockSpec((tm, d), lambda j, i: (i, 0))
    hrow = pl.BlockSpec((tm, d), lambda j, i: (j * (m // tm) + i, 0))
    tile = pl.BlockSpec((tm, tn), lambda j, i: (i, j))
    wspec = lambda w: _spec(w, (tn, d), lambda j, i: (j, 0))
    taps = lambda off: _spec(cwf, (3, tn), lambda j, i: (0, j + off))
    bias = lambda off: _spec(cbf, (1, tn), lambda j, i: (0, j + off))
    blk = (_nbytes((tm, d), F32) + _nbytes((tm, d), BF16) + 2 * _nbytes((tn, d), BF16) + 12 * _nbytes((tm, tn), F32))
    return _pcall(body, name=name,
                  out_shape=(_sds((nj * m, d), BF16), _sds((m, n), F32), _sds((m, n), F32), _sds((m, n), BF16)),
                  grid=(nj, m // tm),
                  in_specs=[row, _spec(g), wspec(wg), wspec(wv), taps(0), bias(0), taps(nj), bias(nj)],
                  out_specs=(hrow, tile, tile, tile),
                  scratch_shapes=[pltpu.VMEM((8, tn), F32), pltpu.VMEM((8, tn), F32)],
                  semantics=("arbitrary", "arbitrary"), block_bytes=blk)(
                      x, _arr(g), _arr(wg), _arr(wv), _arr(cwf), _arr(cbf), _arr(cwf), _arr(cbf))


def _loss_head(x, g, target, *, name):
    s, d = x.shape
    tm = _pick(s, (256, 128))

    def tile_loss(xv, gv, tv):
        err = jnp.square(_rms(xv, gv) - tv)
        return 0.5 * jnp.sum(jnp.mean(err, axis=-1, keepdims=True), axis=0, keepdims=True)

    def body(x_ref, g_ref, t_ref, l_ref, dx_ref, dg_ref):
        lv, vjp = jax.vjp(tile_loss, x_ref[...], g_ref[...], t_ref[...])
        dxv, dgv, _ = vjp(jnp.ones((1, 1), F32))
        dx_ref[...] = dxv

        @pl.when(pl.program_id(0) == 0)
        def _():
            l_ref[...] = jnp.zeros_like(l_ref)
            dg_ref[...] = jnp.zeros_like(dg_ref)

        l_ref[...] += jnp.broadcast_to(lv, l_ref.shape)
        dg_ref[...] += dgv

    row = pl.BlockSpec((tm, d), lambda i: (i, 0))
    vec = pl.BlockSpec((1, d), lambda i: (0, 0))
    return _pcall(body, name=name, out_shape=(_sds((8, 128), F32), _sds((s, d), F32), _sds((1, d), F32)),
                  grid=(s // tm,), in_specs=[row, vec, row],
                  out_specs=(pl.BlockSpec((8, 128), lambda i: (0, 0)), row, vec), semantics=("arbitrary",),
                  block_bytes=8 * _nbytes((tm, d), F32))(x, g, target)


def _acc_out(ref, val, first):
    @pl.when(first)
    def _():
        ref[...] = jnp.zeros_like(ref)

    ref[...] += val


def _gmlp_fwd(z, ln_g, ln_b, wcat, bfull, *, name):
    s = z.shape[0]
    t = _pick(s, (512, 256, 128))
    nch = t // GMLP_CHUNK

    def body(zu_ref, zv_ref, g_ref, b_ref, w_ref, bf_ref, o_ref):
        for c in range(nch):
            rows = pl.ds(c * GMLP_CHUNK, GMLP_CHUNK)
            o_ref[rows, :] = _gmlp_chunk(zu_ref[rows, :], zv_ref[rows, :], g_ref[...], b_ref[...], w_ref[...],
                                         bf_ref[...]).astype(BF16)

    col = lambda c: pl.BlockSpec((t, W_GRP), lambda i: (i, c))
    params = (ln_g, ln_b, wcat, bfull)
    return _pcall(body, name=name, out_shape=_sds((s, D_MODEL), BF16), grid=(s // t,),
                  in_specs=[col(0), col(1)] + [_spec(a) for a in params],
                  out_specs=pl.BlockSpec((t, W_GRP), lambda i: (i, 0)), semantics=("parallel",),
                  block_bytes=4 * _nbytes((t, W_GRP), F32))(z, z, *[_arr(a) for a in params])


def _gmlp_bwd(z, dmix, ln_g, ln_b, wcat, bfull, *, name):
    s = z.shape[0]
    t = _pick(s, (512, 256, 128))
    nch = t // GMLP_CHUNK

    def body(zu_ref, zv_ref, dy_ref, g_ref, b_ref, w_ref, bf_ref, dz_ref, dg_ref, db_ref, dw_ref, dbf_ref):
        acc = None
        for c in range(nch):
            rows = pl.ds(c * GMLP_CHUNK, GMLP_CHUNK)
            _, vjp = jax.vjp(_gmlp_chunk, zu_ref[rows, :], zv_ref[rows, :], g_ref[...], b_ref[...], w_ref[...],
                             bf_ref[...])
            du, dv, *dps = vjp(dy_ref[rows, :])
            dz_ref[rows, :] = jnp.concatenate([du, dv], axis=1).astype(BF16)
            acc = dps if acc is None else [x + y for x, y in zip(acc, dps)]
        first = pl.program_id(0) == 0
        for ref, val in zip((dg_ref, db_ref, dw_ref, dbf_ref), acc):
            _acc_out(ref, val, first)

    col = lambda c: pl.BlockSpec((t, W_GRP), lambda i: (i, c))
    params = (ln_g, ln_b, wcat, bfull)
    return _pcall(body, name=name,
                  out_shape=(_sds((s, D_PROJ), BF16),) + tuple(_sds(a.shape, F32) for a in params),
                  grid=(s // t,), in_specs=[col(0), col(1), col(0)] + [_spec(a) for a in params],
                  out_specs=(pl.BlockSpec((t, 2 * W_GRP), lambda i: (i, 0)),) + tuple(_ospec(a) for a in params),
                  semantics=("arbitrary",),
                  block_bytes=8 * _nbytes((t, W_GRP), F32))(z, z, dmix, *[_arr(a) for a in params])


def _rglru_fwd(z, prm, mix, *, name):
    s = z.shape[0]
    t = _pick(s, (512, 256, 128))
    nt = s // t

    def body(xb_ref, halo_ref, gb_ref, *rest):
        prm_refs, (y_ref, h0s_ref, h_scr) = rest[:len(prm)], rest[len(prm) + 1:]
        i = pl.program_id(0)

        @pl.when(i == 0)
        def _():
            h_scr[...] = jnp.zeros_like(h_scr)

        halo = jnp.where(i == 0, 0.0, halo_ref[...])
        h0 = h_scr[...]
        y, h_last = _rglru_tile(jnp.concatenate([halo, xb_ref[...]], axis=0), gb_ref[...], h0,
                                *[r[...] for r in prm_refs])
        y_ref[...] = y.astype(BF16)
        h0s_ref[...] = jnp.broadcast_to(h0, h0s_ref.shape)
        h_scr[...] = h_last

    in_specs = [pl.BlockSpec((t, W_GRP), lambda i: (i, 2)),
                pl.BlockSpec((8, W_GRP), lambda i: (jnp.maximum(i * (t // 8) - 1, 0), 2)),
                pl.BlockSpec((t, W_GRP), lambda i: (i, 3))] + [_spec(a) for a in prm] + [HBM_SPEC]
    return _pcall(body, name=name, out_shape=(_sds(mix.shape, BF16), _sds((nt, 8, W_GRP), F32)), grid=(nt,),
                  in_specs=in_specs,
                  out_specs=(pl.BlockSpec((t, W_GRP), lambda i: (i, 1)), pl.BlockSpec((None, 8, W_GRP), lambda i: (i, 0, 0))),
                  scratch_shapes=[pltpu.VMEM((1, W_GRP), F32)], semantics=("arbitrary",),
                  block_bytes=24 * _nbytes((t, W_GRP), F32), aliases={3 + len(prm): 0})(
                      z, z, z, *[_arr(a) for a in prm], mix)


def _rglru_bwd(z, dmix, h0s, prm, dz, *, name):
    s = z.shape[0]
    t = _pick(s, (512, 256, 128))
    nt = s // t
    npm = len(prm)

    def body(xb_ref, halo_ref, gb_ref, dy_ref, h0s_ref, *rest):
        prm_refs = rest[:npm]
        dz_ref = rest[npm + 1]
        dprm_refs = rest[npm + 2:2 * npm + 2]
        dh_scr, dhalo_scr = rest[2 * npm + 2:]
        i = pl.program_id(0)
        r = nt - 1 - i

        @pl.when(i == 0)
        def _():
            dh_scr[...] = jnp.zeros_like(dh_scr)
            dhalo_scr[...] = jnp.zeros_like(dhalo_scr)

        halo = jnp.where(r == 0, 0.0, halo_ref[...])
        h0 = h0s_ref[0:1, :]
        _, vjp = jax.vjp(_rglru_tile, jnp.concatenate([halo, xb_ref[...]], axis=0), gb_ref[...], h0,
                         *[p[...] for p in prm_refs])
        dext, dgb, _dh0, *dps = vjp((dy_ref[...], dh_scr[...]))
        dmain = dext[8:]
        dxb = jnp.concatenate([dmain[:t - 8], dmain[t - 8:] + dhalo_scr[...]], axis=0)
        dz_ref[...] = jnp.concatenate([dxb, dgb], axis=1).astype(BF16)
        dh_scr[...] = _dh0
        dhalo_scr[...] = dext[:8]
        for ref, val in zip(dprm_refs, dps):
            _acc_out(ref, val, i == 0)

    rev = lambda c: pl.BlockSpec((t, W_GRP), lambda i: (nt - 1 - i, c))
    in_specs = [rev(2), pl.BlockSpec((8, W_GRP), lambda i: (jnp.maximum((nt - 1 - i) * (t // 8) - 1, 0), 2)), rev(3),
                rev(1), pl.BlockSpec((None, 8, W_GRP), lambda i: (nt - 1 - i, 0, 0))] + [_spec(a) for a in prm] + [HBM_SPEC]
    return _pcall(body, name=name,
                  out_shape=(_sds(dz.shape, BF16),) + tuple(_sds(a.shape, F32) for a in prm),
                  grid=(nt,), in_specs=in_specs,
                  out_specs=(pl.BlockSpec((t, 2 * W_GRP), lambda i: (nt - 1 - i, 1)),) + tuple(_ospec(a) for a in prm),
                  scratch_shapes=[pltpu.VMEM((1, W_GRP), F32), pltpu.VMEM((8, W_GRP), F32)],
                  semantics=("arbitrary",), block_bytes=40 * _nbytes((t, W_GRP), F32), aliases={5 + npm: 0})(
                      z, z, z, dmix, h0s, *[_arr(a) for a in prm], dz)


def _pool_inv(i, t):
    pos = (_rows_of((t, W_GRP)) + i * t + 1).astype(F32)
    grp = _lanes_of((t, W_GRP)) // HEAD_DIM
    win = jnp.where(grp == 0, float(POOL_WINDOWS[0]), jnp.where(grp == 1, float(POOL_WINDOWS[1]),
                    jnp.where(grp == 2, float(POOL_WINDOWS[2]), float(POOL_WINDOWS[3]))))
    return 1.0 / jnp.minimum(pos, win)


def _pool_fwd(z, wd, scale, mix, *, name):
    s = z.shape[0]
    t = _pick(s, (512, 256, 128))

    def body(x_ref, halo_ref, wd_ref, sc_ref, _, y_ref):
        i = pl.program_id(0)
        halo = jnp.where(i == 0, 0.0, halo_ref[...])
        y = _pool_tile(jnp.concatenate([halo, x_ref[...]], axis=0), _pool_inv(i, t), wd_ref[...], sc_ref[...])
        y_ref[...] = y.astype(BF16)

    in_specs = [pl.BlockSpec((t, W_GRP), lambda i: (i, 8)),
                pl.BlockSpec((16, W_GRP), lambda i: (jnp.maximum(i * (t // 16) - 1, 0), 8)), _spec(wd), _spec(scale),
                HBM_SPEC]
    return _pcall(body, name=name, out_shape=_sds(mix.shape, BF16), grid=(s // t,), in_specs=in_specs,
                  out_specs=pl.BlockSpec((t, W_GRP), lambda i: (i, 3)), semantics=("parallel",),
                  block_bytes=12 * _nbytes((t, W_GRP), F32), aliases={4: 0})(z, z, _arr(wd), _arr(scale), mix)


def _pool_bwd(z, dmix, wd, scale, dz, *, name):
    s = z.shape[0]
    t = _pick(s, (512, 256, 128))
    nt = s // t

    def body(x_ref, halo_ref, dy_ref, wd_ref, sc_ref, _, dx_ref, dwd_ref, dsc_ref, dhalo_scr):
        i = pl.program_id(0)
        r = nt - 1 - i

        @pl.when(i == 0)
        def _():
            dhalo_scr[...] = jnp.zeros_like(dhalo_scr)

        halo = jnp.where(r == 0, 0.0, halo_ref[...])
        inv = _pool_inv(r, t)
        _, vjp = jax.vjp(lambda e, w, sc: _pool_tile(e, inv, w, sc), jnp.concatenate([halo, x_ref[...]], axis=0),
                         wd_ref[...], sc_ref[...])
        dext, dwd, dsc = vjp(dy_ref[...])
        dmain = dext[16:]
        dx = jnp.concatenate([dmain[:t - 16], dmain[t - 16:] + dhalo_scr[...]], axis=0)
        dx_ref[...] = dx.astype(BF16)
        dhalo_scr[...] = dext[:16]
        _acc_out(dwd_ref, dwd, i == 0)
        _acc_out(dsc_ref, dsc, i == 0)

    rev = lambda c: pl.BlockSpec((t, W_GRP), lambda i: (nt - 1 - i, c))
    in_specs = [rev(8), pl.BlockSpec((16, W_GRP), lambda i: (jnp.maximum((nt - 1 - i) * (t // 16) - 1, 0), 8)), rev(3),
                _spec(wd), _spec(scale), HBM_SPEC]
    return _pcall(body, name=name, out_shape=(_sds(dz.shape, BF16), _sds(wd.shape, F32), _sds(scale.shape, F32)),
                  grid=(nt,), in_specs=in_specs, out_specs=(rev(8), _ospec(wd), _ospec(scale)),
                  scratch_shapes=[pltpu.VMEM((16, W_GRP), F32)], semantics=("arbitrary",),
                  block_bytes=20 * _nbytes((t, W_GRP), F32), aliases={5: 0})(z, z, dmix, _arr(wd), _arr(scale), dz)


def _hgrn_fwd(z, lb, ngf, mix, *, name):
    s = z.shape[0]
    c = HGRN_CHUNK
    per = HGRN_STEP_CHUNKS
    ns = s // (c * per)

    def body(q_ref, f_ref, i_ref, g_ref, lb_ref, ng_ref, _, y_ref, sts_ref, st_scr):
        @pl.when(pl.program_id(0) == 0)
        def _():
            st_scr[...] = jnp.zeros_like(st_scr)

        st = st_scr[...]
        for k in range(per):
            rows = pl.ds(k * c, c)
            sts_ref[k] = st
            y, st = _hgrn_chunk(q_ref[rows, :], f_ref[rows, :], i_ref[rows, :], g_ref[rows, :], st, lb_ref[...],
                                ng_ref[...])
            y_ref[rows, :] = y.astype(BF16)
        st_scr[...] = st

    col = lambda k: pl.BlockSpec((per * c, W_GRP), lambda i: (i, k))
    return _pcall(body, name=name, out_shape=(_sds(mix.shape, BF16), _sds((ns * per, W_GRP, W_GRP), F32)), grid=(ns,),
                  in_specs=[col(4), col(5), col(6), col(7), _spec(lb), _spec(ngf), HBM_SPEC],
                  out_specs=(pl.BlockSpec((per * c, W_GRP), lambda i: (i, 2)),
                             pl.BlockSpec((per, W_GRP, W_GRP), lambda i: (i, 0, 0))),
                  scratch_shapes=[pltpu.VMEM((W_GRP, W_GRP), F32)], semantics=("arbitrary",),
                  block_bytes=16 * per * _nbytes((W_GRP, W_GRP), F32), aliases={6: 0})(
                      z, z, z, z, _arr(lb), _arr(ngf), mix)


def _hgrn_bwd(z, dmix, sts, lb, ngf, dz, *, name):
    s = z.shape[0]
    c = HGRN_CHUNK
    per = HGRN_STEP_CHUNKS
    ns = s // (c * per)

    def body(q_ref, f_ref, i_ref, g_ref, dy_ref, st_ref, lb_ref, ng_ref, _, dz_ref, dlb_ref, dng_ref, dst_scr):
        i = pl.program_id(0)

        @pl.when(i == 0)
        def _():
            dst_scr[...] = jnp.zeros_like(dst_scr)

        dst = dst_scr[...]
        dlb_sum = dng_sum = None
        for k in range(per - 1, -1, -1):
            rows = pl.ds(k * c, c)
            _, vjp = jax.vjp(_hgrn_chunk, q_ref[rows, :], f_ref[rows, :], i_ref[rows, :], g_ref[rows, :], st_ref[k],
                             lb_ref[...], ng_ref[...])
            dq, df, di, dg, dst, dlb, dng = vjp((dy_ref[rows, :], dst))
            dz_ref[rows, :] = jnp.concatenate([dq, df, di, dg], axis=1).astype(BF16)
            dlb_sum = dlb if dlb_sum is None else dlb_sum + dlb
            dng_sum = dng if dng_sum is None else dng_sum + dng
        dst_scr[...] = dst
        _acc_out(dlb_ref, dlb_sum, i == 0)
        _acc_out(dng_ref, dng_sum, i == 0)

    rev = lambda k: pl.BlockSpec((per * c, W_GRP), lambda i: (ns - 1 - i, k))
    vec = pl.BlockSpec((1, W_GRP), lambda i: (0, 0))
    return _pcall(body, name=name, out_shape=(_sds(dz.shape, BF16), _sds((1, W_GRP), F32), _sds((1, W_GRP), F32)),
                  grid=(ns,),
                  in_specs=[rev(4), rev(5), rev(6), rev(7), rev(2),
                            pl.BlockSpec((per, W_GRP, W_GRP), lambda i: (ns - 1 - i, 0, 0)), _spec(lb), _spec(ngf),
                            HBM_SPEC],
                  out_specs=(pl.BlockSpec((per * c, 4 * W_GRP), lambda i: (ns - 1 - i, 1)), vec, vec),
                  scratch_shapes=[pltpu.VMEM((W_GRP, W_GRP), F32)], semantics=("arbitrary",),
                  block_bytes=32 * per * _nbytes((W_GRP, W_GRP), F32), aliases={8: 0})(
                      z, z, z, z, dmix, sts, _arr(lb), _arr(ngf), dz)


def _lbs_fwd(c_lb, *, name):
    def body(c_ref, o_ref):
        c = c_ref[...]
        e = jnp.exp(c - jnp.max(c, axis=0, keepdims=True))
        sm = e / jnp.sum(e, axis=0, keepdims=True)
        run = jnp.zeros((1, W_GRP), F32)
        o_ref[0:1, :] = run
        for l in range(1, DEPTH):
            run = run + sm[l:l + 1]
            o_ref[l:l + 1, :] = run

    return _pcall(body, name=name, out_shape=_sds((DEPTH, W_GRP), F32), pin=False)(c_lb)


def _lbs_bwd(c_lb, dlbs, *, name):
    def body(c_ref, d_ref, o_ref):
        c = c_ref[...]
        e = jnp.exp(c - jnp.max(c, axis=0, keepdims=True))
        sm = e / jnp.sum(e, axis=0, keepdims=True)
        d = d_ref[...]
        dsm = [None] * DEPTH
        run = jnp.zeros((1, W_GRP), F32)
        for l in range(DEPTH - 1, 0, -1):
            run = run + d[l:l + 1]
            dsm[l] = run
        dsm[0] = jnp.zeros((1, W_GRP), F32)
        inner = sum(sm[l:l + 1] * dsm[l] for l in range(DEPTH))
        for l in range(DEPTH):
            o_ref[l:l + 1, :] = sm[l:l + 1] * (dsm[l] - inner)

    return _pcall(body, name=name, out_shape=_sds((DEPTH, W_GRP), F32), pin=False)(c_lb, dlbs)


def _ffn_bwd(hg, hv, dx, w_down, cwf, cbf, *, name):
    s, n = hg.shape
    t = _pick(s, (256, 128))
    cw = _pick(n, (1408, 256, 128))
    nt = s // t
    nj = n // cw

    def body(g_ref, gh_ref, v_ref, vh_ref, dx_ref, wd_ref, wg_ref, bg_ref, wv_ref, bv_ref, dg_ref, dv_ref, dwg_ref,
             dwv_ref, cg_scr, cv_scr):
        i = pl.program_id(1)
        r = nt - 1 - i

        @pl.when(i == 0)
        def _():
            cg_scr[...] = jnp.zeros_like(cg_scr)
            cv_scr[...] = jnp.zeros_like(cv_scr)

        da = lax.dot_general(dx_ref[...], wd_ref[...], (((1,), (1,)), ((), ())), preferred_element_type=F32)
        eg = jnp.concatenate([jnp.where(r == 0, 0.0, gh_ref[...]), g_ref[...]], axis=0)
        ev = jnp.concatenate([jnp.where(r == 0, 0.0, vh_ref[...]), v_ref[...]], axis=0)
        _, vjp = jax.vjp(_ffn_tile, eg, ev, wg_ref[...], bg_ref[...], wv_ref[...], bv_ref[...])
        deg, dev, dwg, dbg, dwv, dbv = vjp(da)
        for dext, scr, ref in ((deg, cg_scr, dg_ref), (dev, cv_scr, dv_ref)):
            dmain = dext[8:]
            ref[...] = jnp.concatenate([dmain[:t - 8], dmain[t - 8:] + scr[...]], axis=0).astype(BF16)
            scr[...] = dext[:8]
        zeros = jnp.zeros((4, cw), F32)
        _acc_out(dwg_ref, jnp.concatenate([dwg, dbg, zeros], axis=0), i == 0)
        _acc_out(dwv_ref, jnp.concatenate([dwv, dbv, zeros], axis=0), i == 0)

    main = pl.BlockSpec((t, cw), lambda j, i: (nt - 1 - i, j))
    halo = pl.BlockSpec((8, cw), lambda j, i: (jnp.maximum((nt - 1 - i) * (t // 8) - 1, 0), j))
    taps = lambda off: _spec(cwf, (3, cw), lambda j, i: (0, j + off))
    bias = lambda off: _spec(cbf, (1, cw), lambda j, i: (0, j + off))
    w8 = pl.BlockSpec((8, cw), lambda j, i: (0, j))
    d = dx.shape[1]
    in_specs = [main, halo, main, halo, pl.BlockSpec((t, d), lambda j, i: (nt - 1 - i, 0)),
                _spec(w_down, (cw, d), lambda j, i: (j, 0)), taps(0), bias(0), taps(nj), bias(nj)]
    return _pcall(body, name=name,
                  out_shape=(_sds((s, n), BF16), _sds((s, n), BF16), _sds((8, n), F32), _sds((8, n), F32)),
                  grid=(nj, nt), in_specs=in_specs, out_specs=(main, main, w8, w8),
                  scratch_shapes=[pltpu.VMEM((8, cw), F32), pltpu.VMEM((8, cw), F32)],
                  semantics=("parallel", "arbitrary"),
                  block_bytes=24 * _nbytes((t, cw), F32) + _nbytes((cw, d), BF16))(
                      hg, hg, hv, hv, dx, _arr(w_down), _arr(cwf), _arr(cbf), _arr(cwf), _arr(cbf))


def _all_gather(x, *, name):
    r, c = x.shape

    def body(x_ref, out_ref, send_sems, recv_sems, local_sem):
        mx, my, mc = lax.axis_index("x"), lax.axis_index("y"), lax.axis_index("c")
        me, sibling = (mx, my, mc), (mx, my, 1 - mc)
        chips = [(1 - mx, my), (mx, 1 - my), (1 - mx, 1 - my)]

        def slot(px, py, pc):
            return out_ref.at[4 * px + 2 * py + pc]

        def copy(k, block, to, src=None):
            return pltpu.make_async_remote_copy(src_ref=slot(*block) if src is None else src, dst_ref=slot(*block),
                                                send_sem=send_sems.at[k], recv_sem=recv_sems.at[k],
                                                device_id=to, device_id_type=MESH)

        mine = pltpu.make_async_copy(x_ref, slot(*me), local_sem)
        mine.start()
        first = [copy(0, me, sibling, src=x_ref)]
        first += [copy(1 + j, me, (*chip, mc), src=x_ref) for j, chip in enumerate(chips)]
        for cp in first:
            cp.start()
        passed = [copy(4 + j, (*chip, mc), sibling) for j, chip in enumerate(chips)]
        for j, chip in enumerate(chips):
            copy(1 + j, (*chip, mc), me).wait_recv()
            passed[j].start()
        copy(0, sibling, me).wait_recv()
        for j, chip in enumerate(chips):
            copy(4 + j, (*chip, 1 - mc), me).wait_recv()
        for cp in first + passed:
            cp.wait_send()
        mine.wait()

    hbm = pl.BlockSpec(memory_space=pl.ANY)
    return _pcall(body, name=name, out_shape=_sds((N_DEV, r, c), x.dtype), in_specs=[hbm], out_specs=hbm,
                  scratch_shapes=[pltpu.SemaphoreType.DMA((7,)), pltpu.SemaphoreType.DMA((7,)),
                                  pltpu.SemaphoreType.DMA(())])(x)


def _sum_slots(p, *, name):
    q, r, c = p.shape
    tr = _pick(r, (544, 408, 272, 192, 136, 64, 32, 16, 8))

    def body(p_ref, o_ref):
        acc = p_ref[0].astype(F32)
        for k in range(1, q):
            acc = acc + p_ref[k].astype(F32)
        o_ref[...] = acc

    return _pcall(body, name=name, out_shape=_sds((r, c), F32), grid=(r // tr,),
                  in_specs=[pl.BlockSpec((q, tr, c), lambda i: (0, i, 0))],
                  out_specs=pl.BlockSpec((tr, c), lambda i: (i, 0)), semantics=("parallel",),
                  block_bytes=(q + 2) * _nbytes((tr, c), F32))(p)


BIG_COMM = (('w_in', 288, D_MODEL), ('w_out', 128, D_MODEL), ('w_up', 704, D_MODEL), ('w_down', 352, D_MODEL),
            ('w_pe', 128, PLE_DIM), ('w_pg', 128, D_MODEL))
HBM_SPEC = pl.BlockSpec(memory_space=pl.ANY)


def _gather_layer(shards, l, *, name):
    na = len(shards)

    def body(*refs):
        x_refs, out_refs = refs[:na], refs[na:2 * na]
        send_sems, recv_sems, local_sems = refs[2 * na:]
        mx, my, mc = lax.axis_index("x"), lax.axis_index("y"), lax.axis_index("c")
        me, sibling = (mx, my, mc), (mx, my, 1 - mc)
        chips = [(1 - mx, my), (mx, 1 - my), (1 - mx, 1 - my)]

        def slot(a, px, py, pc):
            return out_refs[a].at[4 * px + 2 * py + pc]

        def copy(k, a, block, to, own=False):
            return pltpu.make_async_remote_copy(src_ref=x_refs[a].at[l] if own else slot(a, *block),
                                                dst_ref=slot(a, *block), send_sem=send_sems.at[k, a],
                                                recv_sem=recv_sems.at[k, a], device_id=to, device_id_type=MESH)

        mine = [pltpu.make_async_copy(x_refs[a].at[l], slot(a, *me), local_sems.at[a]) for a in range(na)]
        for cp in mine:
            cp.start()
        first = []
        for a in range(na):
            first.append(copy(0, a, me, sibling, own=True))
            first += [copy(1 + j, a, me, (*chip, mc), own=True) for j, chip in enumerate(chips)]
        for cp in first:
            cp.start()
        passed = []
        for j, chip in enumerate(chips):
            for a in range(na):
                copy(1 + j, a, (*chip, mc), me).wait_recv()
                fwd = copy(4 + j, a, (*chip, mc), sibling)
                fwd.start()
                passed.append(fwd)
        for a in range(na):
            copy(0, a, sibling, me).wait_recv()
        for j, chip in enumerate(chips):
            for a in range(na):
                copy(4 + j, a, (*chip, 1 - mc), me).wait_recv()
        for cp in first + passed:
            cp.wait_send()
        for cp in mine:
            cp.wait()

    return _pcall(body, name=name, out_shape=tuple(_sds((N_DEV,) + x.shape[1:], x.dtype) for x in shards),
                  in_specs=[HBM_SPEC] * na, out_specs=(HBM_SPEC,) * na,
                  scratch_shapes=[pltpu.SemaphoreType.DMA((7, na)), pltpu.SemaphoreType.DMA((7, na)),
                                  pltpu.SemaphoreType.DMA((na,))])(*shards)


SEM_SPEC = pl.BlockSpec(memory_space=pltpu.SEMAPHORE)
DATAFLOW_EFFECT = pltpu.SideEffectType.DATAFLOW_SIDE_EFFECTING


def _place_own(srcs, after, *, name):
    na = len(srcs)

    def body(*refs):
        x_refs, land_refs, sems = refs[:na], refs[na + len(after):2 * na + len(after)], refs[-1]
        me = 4 * lax.axis_index("x") + 2 * lax.axis_index("y") + lax.axis_index("c")
        cps = [pltpu.make_async_copy(x_refs[a], land_refs[a].at[me], sems.at[a]) for a in range(na)]
        for cp in cps:
            cp.start()
        for cp in cps:
            cp.wait()

    return _pcall(body, name=name, out_shape=tuple(_sds((N_DEV,) + x.shape, x.dtype) for x in srcs),
                  in_specs=[HBM_SPEC] * (na + len(after)), out_specs=(HBM_SPEC,) * na,
                  scratch_shapes=[pltpu.SemaphoreType.DMA((na,))], pin=False)(*srcs, *after)


def _exchange_start(srcs, lands, *, name, per_peer=False):
    na = len(srcs)

    def body(*refs):
        x_refs, land_refs = refs[:na], refs[na:2 * na]
        send_sems, recv_sems = refs[2 * na], refs[2 * na + 1]
        token = refs[-1]
        mx, my, mc = lax.axis_index("x"), lax.axis_index("y"), lax.axis_index("c")
        me = 4 * mx + 2 * my + mc
        peers = [(mx, my, 1 - mc)]
        for px, py in ((1 - mx, my), (mx, 1 - my), (1 - mx, 1 - my)):
            peers += [(px, py, mc), (px, py, 1 - mc)]
        for a in range(na):
            for peer in peers:
                src = x_refs[a].at[4 * peer[0] + 2 * peer[1] + peer[2]] if per_peer else x_refs[a]
                pltpu.make_async_remote_copy(src_ref=src, dst_ref=land_refs[a].at[me], send_sem=send_sems.at[a],
                                             recv_sem=recv_sems.at[a], device_id=peer, device_id_type=MESH).start()
        token[...] = jnp.zeros_like(token)

    hbm = lambda x: pltpu.HBM(x.shape, x.dtype)
    out_shape = ((pltpu.SemaphoreType.DMA((na,)), pltpu.SemaphoreType.DMA((na,))) + tuple(hbm(x) for x in srcs)
                 + tuple(hbm(x) for x in lands) + (_sds((8, 128), F32),))
    params = pltpu.CompilerParams(has_side_effects=DATAFLOW_EFFECT)
    pin = lambda x: pltpu.with_memory_space_constraint(x, pltpu.HBM)
    return pl.pallas_call(body, name=name, out_shape=out_shape, in_specs=[HBM_SPEC] * (2 * na),
                          out_specs=(SEM_SPEC, SEM_SPEC) + (HBM_SPEC,) * (2 * na) + (pl.BlockSpec(memory_space=pltpu.VMEM),),
                          input_output_aliases={i: 2 + i for i in range(2 * na)}, compiler_params=params)(
                              *[pin(x) for x in srcs], *[pin(x) for x in lands])


def _exchange_wait(started, after, *, name):
    send_sems, recv_sems, *bufs, _ = started
    na = len(bufs) // 2

    def body(*refs):
        land_refs = refs[na:2 * na]
        s_sems, r_sems = refs[2 * na], refs[2 * na + 1]
        me = (lax.axis_index("x"), lax.axis_index("y"), lax.axis_index("c"))
        for a in range(na):
            seven = land_refs[a].at[pl.ds(0, N_DEV - 1)]
            cp = pltpu.make_async_remote_copy(src_ref=seven, dst_ref=seven, send_sem=s_sems.at[a], recv_sem=r_sems.at[a],
                                              device_id=me, device_id_type=MESH)
            cp.wait_send()
            cp.wait_recv()

    hbm = lambda x: pltpu.HBM(x.shape, x.dtype)
    params = pltpu.CompilerParams(has_side_effects=DATAFLOW_EFFECT)
    outs = pl.pallas_call(body, name=name, out_shape=tuple(hbm(x) for x in bufs),
                          in_specs=[HBM_SPEC] * (2 * na) + [SEM_SPEC, SEM_SPEC, HBM_SPEC],
                          out_specs=(HBM_SPEC,) * (2 * na), input_output_aliases={i: i for i in range(2 * na)},
                          compiler_params=params)(*bufs, send_sems, recv_sems, after)
    return outs[:na], outs[na:]


def _pair_swap(grads, *, name):
    na = len(grads)

    def body(*refs):
        g_refs, recv_refs = refs[:na], refs[na:2 * na]
        send_sems, recv_sems = refs[2 * na:]
        mx, my, mc = lax.axis_index("x"), lax.axis_index("y"), lax.axis_index("c")
        sibling = (mx, my, 1 - mc)
        for a in range(na):
            for q in range(4):
                pltpu.make_async_remote_copy(src_ref=g_refs[a].at[q, 1 - mc], dst_ref=recv_refs[a].at[q],
                                             send_sem=send_sems.at[a], recv_sem=recv_sems.at[a],
                                             device_id=sibling, device_id_type=MESH).start()
        for a in range(na):
            pltpu.make_async_remote_copy(src_ref=recv_refs[a], dst_ref=recv_refs[a], send_sem=send_sems.at[a],
                                         recv_sem=recv_sems.at[a], device_id=sibling, device_id_type=MESH).wait()

    half = tuple(_sds((4,) + g.shape[2:], g.dtype) for g in grads)
    return _pcall(body, name=name, out_shape=half, in_specs=[HBM_SPEC] * na, out_specs=(HBM_SPEC,) * na,
                  scratch_shapes=[pltpu.SemaphoreType.DMA((na,)), pltpu.SemaphoreType.DMA((na,))])(*grads)


def _add_slabs(grads, recv, core, *, name):
    na = len(grads)

    def body(core_ref, *refs):
        for a in range(na):
            refs[2 * na + a][...] = (refs[a][...].astype(F32) + refs[na + a][...].astype(F32)).astype(BF16)

    own_specs = [pl.BlockSpec((None, None) + x.shape[2:], lambda q, core_ref: (q, core_ref[0], 0, 0)) for x in grads]
    specs = [pl.BlockSpec((None,) + x.shape[1:], lambda q, core_ref: (q, 0, 0)) for x in recv]
    blk = sum(_nbytes(x.shape[1:], F32) for x in recv)
    grid_spec = pltpu.PrefetchScalarGridSpec(num_scalar_prefetch=1, grid=(4,), in_specs=own_specs + specs,
                                             out_specs=tuple(specs))
    params = pltpu.CompilerParams(dimension_semantics=("parallel",), vmem_limit_bytes=_vmem_limit(2 * blk))
    return pl.pallas_call(body, name=name, out_shape=tuple(_sds(x.shape, BF16) for x in recv), grid_spec=grid_spec,
                          compiler_params=params)(core, *grads, *recv)


def _chip_exchange(parts, *, name):
    na = len(parts)

    def body(*refs):
        p_refs, out_refs = refs[:na], refs[na:2 * na]
        send_sems, recv_sems, local_sems = refs[2 * na:]
        mx, my, mc = lax.axis_index("x"), lax.axis_index("y"), lax.axis_index("c")
        mine_q = 2 * mx + my
        chips = [(1 - mx, my), (mx, 1 - my), (1 - mx, 1 - my)]
        owns = [pltpu.make_async_copy(p_refs[a].at[mine_q], out_refs[a].at[mine_q], local_sems.at[a]) for a in range(na)]
        for cp in owns:
            cp.start()
        sends = []
        for a in range(na):
            for k, chip in enumerate(chips):
                sends.append(pltpu.make_async_remote_copy(
                    src_ref=p_refs[a].at[2 * chip[0] + chip[1]], dst_ref=out_refs[a].at[mine_q],
                    send_sem=send_sems.at[k, a], recv_sem=recv_sems.at[k, a], device_id=(*chip, mc), device_id_type=MESH))
        for cp in sends:
            cp.start()
        for a in range(na):
            for k, chip in enumerate(chips):
                pltpu.make_async_remote_copy(
                    src_ref=p_refs[a].at[mine_q], dst_ref=out_refs[a].at[2 * chip[0] + chip[1]],
                    send_sem=send_sems.at[k, a], recv_sem=recv_sems.at[k, a], device_id=(*chip, mc),
                    device_id_type=MESH).wait_recv()
        for cp in sends:
            cp.wait_send()
        for cp in owns:
            cp.wait()

    return _pcall(body, name=name, out_shape=tuple(_sds(x.shape, x.dtype) for x in parts), in_specs=[HBM_SPEC] * na,
                  out_specs=(HBM_SPEC,) * na,
                  scratch_shapes=[pltpu.SemaphoreType.DMA((3, na)), pltpu.SemaphoreType.DMA((3, na)),
                                  pltpu.SemaphoreType.DMA((na,))])(*parts)


def _sum_chips(parts, *, name):
    na = len(parts)

    def body(*refs):
        for a in range(na):
            p_ref = refs[a]
            acc = p_ref[0].astype(F32)
            for k in range(1, p_ref.shape[0]):
                acc = acc + p_ref[k].astype(F32)
            refs[na + a][...] = acc

    half = lambda x: x.shape[1] // 2
    in_specs = [pl.BlockSpec((x.shape[0], half(x), x.shape[2]), lambda i: (0, i, 0)) for x in parts]
    out_specs = tuple(pl.BlockSpec((half(x), x.shape[2]), lambda i: (i, 0)) for x in parts)
    blk = sum(_nbytes((x.shape[0] + 2, half(x), x.shape[2]), BF16) for x in parts)
    return _pcall(body, name=name, out_shape=tuple(_sds(x.shape[1:], F32) for x in parts), grid=(2,),
                  in_specs=in_specs, out_specs=out_specs, semantics=("parallel",), block_bytes=blk)(*parts)


def _sum_devices(lands, own, me, *, name):
    na = len(lands)

    def body(me_ref, *refs):
        mine = me_ref[0]
        for a in range(na):
            l_ref, o_ref = refs[a], refs[na + a]
            acc = None
            for k in range(N_DEV):
                term = jnp.where(mine == k, o_ref[...], l_ref[k]).astype(F32)
                acc = term if acc is None else acc + term
            refs[2 * na + a][...] = acc

    half = lambda x: x.shape[1] // 2
    land_specs = [pl.BlockSpec((N_DEV, half(x), x.shape[2]), lambda i, me_ref: (0, i, 0)) for x in lands]
    own_specs = [pl.BlockSpec((None, half(x), x.shape[2]), lambda i, me_ref: (me_ref[0], i, 0)) for x in lands]
    out_specs = tuple(pl.BlockSpec((half(x), x.shape[2]), lambda i, me_ref: (i, 0)) for x in lands)
    blk = sum(_nbytes((N_DEV + 3, half(x), x.shape[2]), BF16) for x in lands)
    grid_spec = pltpu.PrefetchScalarGridSpec(num_scalar_prefetch=1, grid=(2,), in_specs=land_specs + own_specs,
                                             out_specs=out_specs)
    params = pltpu.CompilerParams(dimension_semantics=("parallel",), vmem_limit_bytes=_vmem_limit(blk))
    return pl.pallas_call(body, name=name, out_shape=tuple(_sds(x.shape[1:], F32) for x in lands), grid_spec=grid_spec,
                          compiler_params=params)(me, *lands, *own)


def _reduce_layer(grads, l):
    n = lambda s: f"l{l}_{s}"
    views = [g.reshape(4, 2, g.shape[0] // N_DEV, g.shape[1]) for g in grads]
    recv = _pair_swap(views, name=n("reduce_pair"))
    core = lax.axis_index("c").astype(jnp.int32).reshape(1)
    chip_sum = _add_slabs(views, recv, core, name=n("reduce_pair_add"))
    from_chips = _chip_exchange(chip_sum, name=n("reduce_chips"))
    return _sum_chips(from_chips, name=n("reduce_chips_add"))


def _adamw(w, g, m, v, *, name):
    lead, (r, c) = w.shape[:-2], w.shape[-2:]
    tr = _pick(r, (512, 352, 288, 256, 192, 128, 64, 32, 16, 8))
    c1 = 1.0 / (1.0 - ADAM_B1 ** ADAM_STEP)
    c2 = 1.0 / (1.0 - ADAM_B2 ** ADAM_STEP)

    def body(w_ref, g_ref, m_ref, v_ref, d_ref, nm_ref, nv_ref):
        gv = g_ref[...]
        nm = ADAM_B1 * m_ref[...] + (1.0 - ADAM_B1) * gv
        nv = ADAM_B2 * v_ref[...] + (1.0 - ADAM_B2) * jnp.square(gv)
        d_ref[...] = -ADAM_LR * ((nm * c1) / (jnp.sqrt(nv * c2) + ADAM_EPS) + ADAM_WD * w_ref[...])
        nm_ref[...] = nm
        nv_ref[...] = nv

    if lead:
        blk = pl.BlockSpec((None, tr, c), lambda k, i: (k, i, 0))
        grid, sem = (lead[0], r // tr), ("parallel", "parallel")
    else:
        blk = pl.BlockSpec((tr, c), lambda i: (i, 0))
        grid, sem = (r // tr,), ("parallel",)
    out = _sds(w.shape, F32)
    return _pcall(body, name=name, out_shape=(out, out, out), grid=grid, in_specs=[blk] * 4,
                  out_specs=(blk, blk, blk), semantics=sem, block_bytes=7 * _nbytes((tr, c), F32))(w, g, m, v)


def _pack_flat(arrs, rows, cols=1024):
    flat = jnp.concatenate([a.reshape(-1).astype(F32) for a in arrs])
    pad = rows * cols - flat.shape[0]
    return jnp.pad(flat, (0, pad)).reshape(rows, cols)


def _unpack_flat(buf, shapes):
    flat = buf.reshape(-1)
    out, off = [], 0
    for shp in shapes:
        n = 1
        for s in shp:
            n *= s
        out.append(flat[off:off + n].reshape(shp))
        off += n
    return out


def _flat_rows(shapes, cols=1024):
    n = sum(functools.reduce(lambda a, b: a * b, shp, 1) for shp in shapes)
    rows = -(-n // cols)
    return -(-rows // 64) * 64


def _block_diag(w):
    eye = jnp.eye(N_HEADS, dtype=w.dtype)
    return (w[:, :, :, None, :] * eye[None, :, None, :, None]).reshape(w.shape[0], W_GRP, W_GRP)


def _diag_blocks(w):
    w5 = w.reshape(w.shape[0], N_HEADS, HEAD_DIM, N_HEADS, HEAD_DIM)
    return jnp.stack([w5[:, h, :, h, :] for h in range(N_HEADS)], axis=1)


def _stacked_params(w, lbs):
    tril = jnp.tril(jnp.ones((GMLP_CHUNK, GMLP_CHUNK), bool))
    row = lambda a: a.reshape(DEPTH, 1, -1)
    return dict(
        g1=row(w['norm1_g']), g2=row(w['norm2_g']), g3=row(w['norm3_g']),
        a_ln_g=row(w['a_ln_g']), a_ln_b=row(w['a_ln_b']),
        a_wcat=jnp.where(tril, w['a_ws'], 0.0).reshape(DEPTH, N_HEADS * GMLP_CHUNK, GMLP_CHUNK),
        a_bfull=jnp.repeat(jnp.swapaxes(w['a_bs'], 1, 2), HEAD_DIM, axis=2),
        b_cw=w['b_conv_w_full'], b_cb=row(w['b_conv_b']), b_wa=_block_diag(w['b_wa']), b_ba=row(w['b_ba']),
        b_wx=_block_diag(w['b_wx']), b_bx=row(w['b_bx']), b_lam=row(w['b_lam']),
        c_lb=row(lbs), c_ngf=row(jnp.tile(w['c_norm_g'], (1, N_HEADS))),
        d_wd=_block_diag(w['d_w']), d_scale=row(w['d_scale']),
        f_cw=w['ffn_conv_w_full'], f_cb=row(w['ffn_conv_b']),
    )


B_PRM = ('b_cw', 'b_cb', 'b_wa', 'b_ba', 'b_wx', 'b_bx', 'b_lam')


def _layer_fwd(x, p_bf, wb, sp, l):
    n = lambda s: f"l{l}_{s}"
    h, (z,) = _rms_matmul(x, sp['g1'], [wb['w_in']], nt=True, name=n("proj_in"))
    mix = _gmlp_fwd(z, sp['a_ln_g'], sp['a_ln_b'], sp['a_wcat'], sp['a_bfull'], name=n("gmlp"))
    mix, h0s = _rglru_fwd(z, [sp[k] for k in B_PRM], mix, name=n("rglru"))
    mix, sts = _hgrn_fwd(z, sp['c_lb'], sp['c_ngf'], mix, name=n("hgrn"))
    mix = _pool_fwd(z, sp['d_wd'], sp['d_scale'], mix, name=n("pool"))
    x1 = _matmul(mix, wb['w_out'], res=x, name=n("proj_out"))
    h2, hg, hv, a = _up_ffn_fwd(x1, sp['g2'], wb['w_up_g'], wb['w_up_v'], sp['f_cw'], sp['f_cb'], name=n("up_ffn"))
    x2 = _matmul(a, wb['w_down'], res=x1, name=n("down"))
    h3, (gl, pe, x3) = _rms_matmul(x2, sp['g3'], [wb['w_pg']], ple=(p_bf, wb['w_pe']), name=n("ple"))
    saved = dict(x=x, h=h, z=z, h0s=h0s, sts=sts, mix=mix, x1=x1, h2=h2, hg=hg, hv=hv, a=a, x2=x2, h3=h3, gl=gl, pe=pe)
    return x3, saved


def _layer_bwd(dx3, sv, p_bf, wb, sp, l, mid=None):
    n = lambda s: f"l{l}_{s}_bwd"
    gb, gs = {}, {}
    dx2, dx2b, gs['norm3_g'], dpe, dgl = _ple_rms_bwd(dx3, sv['gl'], sv['pe'], wb['w_pg'], sv['x2'], sp['g3'],
                                                      name=n("ple"))
    gb['w_pe'] = _matmul_tn(dpe, p_bf, name=n("ple_emb_w"))
    gb['w_pg'] = _matmul_tn(sv['h3'], dgl, name=n("ple_gate_w"))
    gb['w_down'] = _matmul_tn(sv['a'], dx2b, name=n("down_w"))
    dhg, dhv, gs['f_dwg'], gs['f_dwv'] = _ffn_bwd(sv['hg'], sv['hv'], dx2b, wb['w_down'], sp['f_cw'], sp['f_cb'],
                                                  name=n("ffn_gate"))
    gate_rows = _matmul_tn(dhg, sv['h2'], name=n("up_gate_w"), out_rows=2 * D_FF)
    gb['w_up'] = _matmul_tn(dhv, sv['h2'], name=n("up_val_w"), out_rows=2 * D_FF, row_off=D_FF, into=gate_rows)
    if mid is not None:
        sp = mid(gb, sp)
    dh2 = _matmul(dhg, wb['w_up_g'], name=n("up_gate_x"))
    dx1, dx1b, gs['norm2_g'] = _matmul_rms_bwd(dhv, wb['w_up_v'], sv['x1'], sp['g2'], dx2, res=dh2, name=n("up_val_x"))
    dmix = _matmul(dx1b, wb['w_out'], nt=True, name=n("proj_out_x"))
    gb['w_out'] = _matmul_tn(sv['mix'], dx1b, name=n("proj_out_w"))
    z = sv['z']
    dz, gs['a_ln_g'], gs['a_ln_b'], gs['a_wcat'], gs['a_bfull'] = _gmlp_bwd(
        z, dmix, sp['a_ln_g'], sp['a_ln_b'], sp['a_wcat'], sp['a_bfull'], name=n("gmlp"))
    dz, *dbp = _rglru_bwd(z, dmix, sv['h0s'], [sp[k] for k in B_PRM], dz, name=n("rglru"))
    gs.update(zip(B_PRM, dbp))
    dz, gs['c_lb'], gs['c_ngf'] = _hgrn_bwd(z, dmix, sv['sts'], sp['c_lb'], sp['c_ngf'], dz, name=n("hgrn"))
    dz, gs['d_wd'], gs['d_scale'] = _pool_bwd(z, dmix, sp['d_wd'], sp['d_scale'], dz, name=n("pool"))
    gb['w_in'] = _matmul_tn(dz, sv['h'], name=n("proj_in_w"))
    dx0, _, gs['norm1_g'] = _matmul_rms_bwd(dz, wb['w_in'], sv['x'], sp['g1'], dx1, name=n("proj_in_x"))
    return dx0, gb, gs


SMALL_NAMES = [nm for nm in WEIGHT_NAMES if nm not in BIG_NAMES]
COL_SHARDED = ('w_in', 'w_up', 'w_pe')


def _comm_shards(w):
    return [(jnp.swapaxes(w[nm], 1, 2) if nm in COL_SHARDED else w[nm]).astype(BF16) for nm, _, _ in BIG_COMM]


def _full_weights(gathered):
    out = {nm: g.reshape(N_DEV * r, c) for g, (nm, r, c) in zip(gathered, BIG_COMM)}
    halves = out.pop('w_up').reshape(2, D_FF, D_MODEL)
    out['w_up_g'], out['w_up_v'] = _Sel(halves, 0), _Sel(halves, 1)
    return out


def _small_grads(raw):
    nl = len(raw)
    st = {k: jnp.stack([r[k] for r in raw]) for k in raw[0]}
    tril = jnp.tril(jnp.ones((GMLP_CHUNK, GMLP_CHUNK), bool))
    vec = lambda a: a.reshape(nl, -1)
    out = {nm: vec(st[k]) for nm, k in (('norm1_g', 'norm1_g'), ('norm2_g', 'norm2_g'), ('norm3_g', 'norm3_g'),
                                        ('a_ln_g', 'a_ln_g'), ('a_ln_b', 'a_ln_b'), ('b_conv_b', 'b_cb'),
                                        ('b_ba', 'b_ba'), ('b_bx', 'b_bx'), ('b_lam', 'b_lam'), ('c_lb', 'c_lb'),
                                        ('d_scale', 'd_scale'))}
    out['a_ws'] = jnp.where(tril, st['a_wcat'].reshape(nl, N_HEADS, GMLP_CHUNK, GMLP_CHUNK), 0.0)
    out['a_bs'] = jnp.swapaxes(st['a_bfull'].reshape(nl, GMLP_CHUNK, N_HEADS, HEAD_DIM).sum(-1), 1, 2)
    out['b_conv_w'] = st['b_cw']
    out['b_wa'], out['b_wx'], out['d_w'] = _diag_blocks(st['b_wa']), _diag_blocks(st['b_wx']), _diag_blocks(st['d_wd'])
    out['c_norm_g'] = st['c_ngf'].reshape(nl, N_HEADS, HEAD_DIM).sum(1)
    out['ffn_conv_w'] = jnp.concatenate([st['f_dwg'][:, 0:3], st['f_dwv'][:, 0:3]], axis=2)
    out['ffn_conv_b'] = jnp.concatenate([st['f_dwg'][:, 3], st['f_dwv'][:, 3]], axis=1)
    return out


def _step(w, m, v, x, p, target):
    s = x.shape[1]
    dev = 4 * lax.axis_index("x") + 2 * lax.axis_index("y") + lax.axis_index("c")
    xs = x.reshape(s, D_MODEL)

    shards = _comm_shards(w)
    conv_shapes = [w['b_conv_w'].shape, w['ffn_conv_w'].shape]
    conv_rows = _flat_rows(conv_shapes)
    conv_all = _all_gather(_pack_flat([w['b_conv_w'], w['ffn_conv_w']], conv_rows), name="gather_conv_weights")
    parts = [_unpack_flat(conv_all[d], conv_shapes) for d in range(N_DEV)]
    wf = dict(w)
    wf['b_conv_w_full'] = jnp.concatenate([pt[0] for pt in parts], axis=-1)
    wf['ffn_conv_w_full'] = jnp.concatenate([pt[1] for pt in parts], axis=-1)
    lbs = _lbs_fwd(w['c_lb'], name="hgrn_bounds")

    stacked = _stacked_params(wf, lbs)
    p_all = p.reshape(DEPTH, s, PLE_DIM).astype(BF16)
    xl, saved, wbs, sps = xs, [], [], []
    gathered = _gather_layer(shards, 0, name="l0_gather_weights")
    for l in range(DEPTH):
        sp = {k: _Sel(a, l) for k, a in stacked.items()}
        if l + 1 < DEPTH:
            own = [x[l + 1] for x in shards]
            after = [conv_all, *gathered] if l == 0 else [xl]
            lands = _place_own(own, after, name=f"l{l + 1}_gather_place")
            started = _exchange_start(own, lands, name=f"l{l + 1}_gather_start")
            sp['g1'] = stacked['g1'][l] + started[-1][0, 0]
        wb = _full_weights(gathered)
        p_bf = p_all[l]
        xl, sv = _layer_fwd(xl, p_bf, wb, sp, l)
        if l + 1 < DEPTH:
            gathered = _exchange_wait(started, xl, name=f"l{l + 1}_gather_wait")[1]
        saved.append((sv, p_bf))
        wbs.append(wb)
        sps.append(sp)
    loss_part, dx, dfinal = _loss_head(xl, w['final_g'].reshape(1, D_MODEL), target.reshape(s, D_MODEL), name="loss_head")
    loss = lax.psum(loss_part[0, 0], ("x", "y", "c"))

    dev1 = dev.astype(jnp.int32).reshape(1)
    names = [nm for nm, _, _ in BIG_COMM]

    def start_reduce(grads, name):
        views = [g.reshape(N_DEV, g.shape[0] // N_DEV, g.shape[1]) for g in grads]
        return _exchange_start(views, [lax.empty(g.shape, g.dtype) for g in views], name=name, per_peer=True)

    def finish_reduce(started, after, lname):
        own, lands = _exchange_wait(started, after, name=f"{lname}_reduce_wait")
        return _sum_devices(lands, own, dev1, name=f"{lname}_reduce_sum")

    reduced, small = [None] * DEPTH, [None] * DEPTH
    pending = None
    for l in range(DEPTH - 1, 0, -1):
        sv, p_bf = saved[l]
        sp = sps[l]
        if pending is not None:
            sp = dict(sp, g3=stacked['g3'][l] + pending[-1][0, 0])
        dx, gb, small[l] = _layer_bwd(dx, sv, p_bf, wbs[l], sp, l)
        if pending is not None:
            reduced[l + 1] = finish_reduce(pending, dx, f"l{l + 1}")
        pending = start_reduce([gb[nm] for nm in names], f"l{l}_reduce_start")
    early = ('w_up', 'w_down', 'w_pe', 'w_pg')
    mid_started = []

    def mid(gb, sp):
        mid_started.append(start_reduce([gb[nm] for nm in early], "l0_reduce_start"))
        return dict(sp, g2=stacked['g2'][0] + mid_started[0][-1][0, 0])

    upper_names = [nm for nm in SMALL_NAMES if nm != 'final_g']
    low_names = upper_names + ['final_g']
    upper = _small_grads(small[1:])
    upper_shapes = [upper[nm].shape for nm in upper_names]
    upper_packed = [_pack_flat([upper[nm] for nm in upper_names], _flat_rows(upper_shapes))]
    upper_started = _exchange_start(upper_packed, _place_own(upper_packed, [], name="upper_small_grads_place"),
                                    name="upper_small_grads_start")

    sv, p_bf = saved[0]
    g3 = stacked['g3'][0] + pending[-1][0, 0] + upper_started[-1][0, 0]
    dx, gb, small[0] = _layer_bwd(dx, sv, p_bf, wbs[0], dict(sps[0], g3=g3), 0, mid=mid)
    reduced[1] = finish_reduce(pending, dx, "l1")
    late = dict(zip(('w_in', 'w_out'), _reduce_layer([gb['w_in'], gb['w_out']], 0)))
    late.update(zip(early, finish_reduce(mid_started[0], late['w_in'], "l0")))
    reduced[0] = [late[nm] for nm in names]
    grad_x = dx.reshape(1, s, D_MODEL)
    low = _small_grads(small[:1])
    low['final_g'] = dfinal.reshape(D_MODEL)
    low_shapes = [low[nm].shape for nm in low_names]
    low_all = _all_gather(_pack_flat([low[nm] for nm in low_names], _flat_rows(low_shapes)), name="gather_small_grads")
    low_sum = dict(zip(low_names, _unpack_flat(_sum_slots(low_all, name="sum_small_grads"), low_shapes)))
    upper_all = _exchange_wait(upper_started, low_all, name="upper_small_grads_wait")[1][0]
    upper_sum = dict(zip(upper_names, _unpack_flat(_sum_slots(upper_all, name="sum_upper_small_grads"), upper_shapes)))
    gsmall = {nm: jnp.concatenate([low_sum[nm], upper_sum[nm]], axis=0) for nm in upper_names}
    gsmall['c_lb'] = _lbs_bwd(w['c_lb'], gsmall['c_lb'], name="hgrn_bounds_bwd")
    gsmall['final_g'] = low_sum['final_g']
    for nm in ('b_conv_w', 'ffn_conv_w'):
        width = w[nm].shape[-1]
        gsmall[nm] = lax.dynamic_slice_in_dim(gsmall[nm], dev * width, width, axis=2)

    grads, delta, new_m, new_v = {}, {}, {}, {}
    for a, (nm, _, _) in enumerate(BIG_COMM):
        t = (lambda x: jnp.swapaxes(x, 1, 2)) if nm in COL_SHARDED else (lambda x: x)
        g = jnp.stack([reduced[l][a] for l in range(DEPTH)])
        d, nm_, nv_ = _adamw(t(w[nm]), g, t(m[nm]), t(v[nm]), name=f"adamw_{nm}")
        grads[nm], delta[nm], new_m[nm], new_v[nm] = t(g), t(d), t(nm_), t(nv_)

    shapes = [w[nm].shape for nm in SMALL_NAMES]
    rows = _flat_rows(shapes)
    pk = lambda t: _pack_flat([t[nm] for nm in SMALL_NAMES], rows)
    d, nm_, nv_ = _adamw(pk(w), pk(gsmall), pk(m), pk(v), name="adamw_small")
    for nm, dd, mm_, vv_ in zip(SMALL_NAMES, _unpack_flat(d, shapes), _unpack_flat(nm_, shapes), _unpack_flat(nv_, shapes)):
        grads[nm], delta[nm], new_m[nm], new_v[nm] = gsmall[nm], dd, mm_, vv_

    return (loss, grad_x, *[grads[nm] for nm in WEIGHT_NAMES], *[delta[nm] for nm in WEIGHT_NAMES],
            *[new_m[nm] for nm in WEIGHT_NAMES], *[new_v[nm] for nm in WEIGHT_NAMES])


def kernel(x, p, norm1_g, w_in, a_ln_g, a_ln_b, a_ws, a_bs, b_conv_w, b_conv_b, b_wa, b_ba, b_wx, b_bx, b_lam, c_lb, c_norm_g, d_w, d_scale, w_out, norm2_g, w_up, ffn_conv_w, ffn_conv_b, w_down, norm3_g, w_pe, w_pg, final_g, loss_target, m_norm1_g, m_w_in, m_a_ln_g, m_a_ln_b, m_a_ws, m_a_bs, m_b_conv_w, m_b_conv_b, m_b_wa, m_b_ba, m_b_wx, m_b_bx, m_b_lam, m_c_lb, m_c_norm_g, m_d_w, m_d_scale, m_w_out, m_norm2_g, m_w_up, m_ffn_conv_w, m_ffn_conv_b, m_w_down, m_norm3_g, m_w_pe, m_w_pg, m_final_g, v_norm1_g, v_w_in, v_a_ln_g, v_a_ln_b, v_a_ws, v_a_bs, v_b_conv_w, v_b_conv_b, v_b_wa, v_b_ba, v_b_wx, v_b_bx, v_b_lam, v_c_lb, v_c_norm_g, v_d_w, v_d_scale, v_w_out, v_norm2_g, v_w_up, v_ffn_conv_w, v_ffn_conv_b, v_w_down, v_norm3_g, v_w_pe, v_w_pg, v_final_g):
    w = dict(norm1_g=norm1_g, w_in=w_in, a_ln_g=a_ln_g, a_ln_b=a_ln_b, a_ws=a_ws, a_bs=a_bs, b_conv_w=b_conv_w, b_conv_b=b_conv_b, b_wa=b_wa, b_ba=b_ba, b_wx=b_wx, b_bx=b_bx, b_lam=b_lam, c_lb=c_lb, c_norm_g=c_norm_g, d_w=d_w, d_scale=d_scale, w_out=w_out, norm2_g=norm2_g, w_up=w_up, ffn_conv_w=ffn_conv_w, ffn_conv_b=ffn_conv_b, w_down=w_down, norm3_g=norm3_g, w_pe=w_pe, w_pg=w_pg, final_g=final_g)
    m = dict(norm1_g=m_norm1_g, w_in=m_w_in, a_ln_g=m_a_ln_g, a_ln_b=m_a_ln_b, a_ws=m_a_ws, a_bs=m_a_bs, b_conv_w=m_b_conv_w, b_conv_b=m_b_conv_b, b_wa=m_b_wa, b_ba=m_b_ba, b_wx=m_b_wx, b_bx=m_b_bx, b_lam=m_b_lam, c_lb=m_c_lb, c_norm_g=m_c_norm_g, d_w=m_d_w, d_scale=m_d_scale, w_out=m_w_out, norm2_g=m_norm2_g, w_up=m_w_up, ffn_conv_w=m_ffn_conv_w, ffn_conv_b=m_ffn_conv_b, w_down=m_w_down, norm3_g=m_norm3_g, w_pe=m_w_pe, w_pg=m_w_pg, final_g=m_final_g)
    v = dict(norm1_g=v_norm1_g, w_in=v_w_in, a_ln_g=v_a_ln_g, a_ln_b=v_a_ln_b, a_ws=v_a_ws, a_bs=v_a_bs, b_conv_w=v_b_conv_w, b_conv_b=v_b_conv_b, b_wa=v_b_wa, b_ba=v_b_ba, b_wx=v_b_wx, b_bx=v_b_bx, b_lam=v_b_lam, c_lb=v_c_lb, c_norm_g=v_c_norm_g, d_w=v_d_w, d_scale=v_d_scale, w_out=v_w_out, norm2_g=v_norm2_g, w_up=v_w_up, ffn_conv_w=v_ffn_conv_w, ffn_conv_b=v_ffn_conv_b, w_down=v_w_down, norm3_g=v_norm3_g, w_pe=v_w_pe, w_pg=v_w_pg, final_g=v_final_g)
    return _step(w, m, v, x, p, loss_target)
```

```python
import functools

import jax
import jax.numpy as jnp
from jax import lax
from jax.experimental import pallas as pl
from jax.experimental.pallas import tpu as pltpu

F32 = jnp.float32
BF16 = jnp.bfloat16
MESH = pl.DeviceIdType.MESH

D_MODEL = 1024
DEPTH = 4
PLE_DIM = 256
W_GRP = 256
N_HEADS = 4
HEAD_DIM = 64
GMLP_CHUNK = 128
RGLRU_C = 8.0
HGRN_CHUNK = 64
HGRN_SUB = 32
HGRN_STEP_CHUNKS = 8
POOL_WINDOWS = (2, 4, 8, 16)
D_FF = 2816
D_PROJ = 2304
EPS = 1e-6
ADAM_LR = 0.001
ADAM_B1 = 0.9
ADAM_B2 = 0.999
ADAM_EPS = 1e-08
ADAM_WD = 0.01
ADAM_STEP = 10

N_DEV = 8
MIB = 2 ** 20
V7X_VMEM_BYTES = 64 * MIB
HGRN_EXP_CLAMP = 60.0

WEIGHT_NAMES = ['norm1_g', 'w_in', 'a_ln_g', 'a_ln_b', 'a_ws', 'a_bs', 'b_conv_w', 'b_conv_b', 'b_wa', 'b_ba', 'b_wx',
                'b_bx', 'b_lam', 'c_lb', 'c_norm_g', 'd_w', 'd_scale', 'w_out', 'norm2_g', 'w_up', 'ffn_conv_w',
                'ffn_conv_b', 'w_down', 'norm3_g', 'w_pe', 'w_pg', 'final_g']
BIG_NAMES = ('w_in', 'w_out', 'w_up', 'w_down', 'w_pe', 'w_pg')


def _vmem_limit(block_bytes):
    want = 2 * block_bytes + 24 * MIB
    return int(min(max(want, 32 * MIB), V7X_VMEM_BYTES - 8 * MIB))


def _in_hbm(x):
    return pltpu.with_memory_space_constraint(x, pltpu.HBM)


def _out_hbm(s):
    return pltpu.HBM(s.shape, s.dtype)


def _pcall(body, *, name, out_shape, grid=None, in_specs=None, out_specs=None, scratch_shapes=(),
           semantics=None, block_bytes=0, aliases=None, pin=True):
    kw = {} if aliases is None else {"input_output_aliases": aliases}
    if pin:
        out_shape = tuple(_out_hbm(s) for s in out_shape) if isinstance(out_shape, (tuple, list)) else _out_hbm(out_shape)
    if grid is not None:
        kw["grid"] = grid
    if in_specs is not None:
        kw["in_specs"] = in_specs
    if out_specs is not None:
        kw["out_specs"] = out_specs
    params = pltpu.CompilerParams(dimension_semantics=semantics, vmem_limit_bytes=_vmem_limit(block_bytes))
    call = pl.pallas_call(body, name=name, out_shape=out_shape, scratch_shapes=list(scratch_shapes),
                          compiler_params=params, **kw)
    return (lambda *args: call(*[_in_hbm(a) for a in args])) if pin else call


def _pick(n, cands):
    for c in cands:
        if n % c == 0:
            return c
    return n


def _nbytes(shape, dtype):
    n = 1
    for s in shape:
        n *= s
    return n * jnp.dtype(dtype).itemsize


def _sds(shape, dtype):
    return jax.ShapeDtypeStruct(tuple(shape), dtype)


class _Sel:
    def __init__(self, arr, *idx):
        self.arr, self.idx = arr, tuple(idx)
        self.shape = arr.shape[len(idx):]
        self.ndim = len(self.shape)
        self.dtype = arr.dtype


def _arr(a):
    return a.arr if isinstance(a, _Sel) else a


def _spec(a, block=None, index=None):
    block = tuple(a.shape) if block is None else tuple(block)
    index = (lambda *g: (0,) * len(block)) if index is None else index
    if isinstance(a, _Sel):
        lead = a.idx
        return pl.BlockSpec((None,) * len(lead) + block, lambda *g: lead + tuple(index(*g)))
    return pl.BlockSpec(block, lambda *g: tuple(index(*g)))


def _ospec(a):
    return pl.BlockSpec(tuple(a.shape), lambda *g: (0,) * a.ndim)


def _rows_of(shape):
    return lax.broadcasted_iota(jnp.int32, shape, 0)


def _lanes_of(shape):
    return lax.broadcasted_iota(jnp.int32, shape, 1)


def _sdn(x, k, fill):
    n = x.shape[0]
    return jnp.where(_rows_of(x.shape) >= k, pltpu.roll(x, k % n, 0), fill)


def _sup(x, k, fill):
    n = x.shape[0]
    return jnp.where(_rows_of(x.shape) < n - k, pltpu.roll(x, (n - k) % n, 0), fill)


@functools.partial(jax.custom_vjp, nondiff_argnums=(1,))
def _shift_dn(x, k):
    return pltpu.roll(x, k, 0)


def _shift_dn_fwd(x, k):
    return pltpu.roll(x, k, 0), None


def _shift_dn_bwd(k, _, g):
    return (pltpu.roll(g, g.shape[0] - k, 0),)


_shift_dn.defvjp(_shift_dn_fwd, _shift_dn_bwd)


SUBLANES = 8


def _lin_scan_impl(a, b, h0):
    n = a.shape[0]
    pos = _rows_of(a.shape) % SUBLANES
    aa, bb = a, b
    k = 1
    while k < SUBLANES:
        keep = pos >= k
        bb = bb + jnp.where(keep, aa * pltpu.roll(bb, k, 0), 0.0)
        aa = aa * jnp.where(keep, pltpu.roll(aa, k, 0), 1.0)
        k *= 2
    out, carry = [], h0
    for r in range(n // SUBLANES):
        rows = slice(r * SUBLANES, (r + 1) * SUBLANES)
        hr = bb[rows] + aa[rows] * carry
        out.append(hr)
        carry = hr[SUBLANES - 1:]
    return jnp.concatenate(out, axis=0)


@jax.custom_vjp
def _lin_scan(a, b, h0):
    return _lin_scan_impl(a, b, h0)


def _lin_scan_fwd(a, b, h0):
    h = _lin_scan_impl(a, b, h0)
    return h, (a, h, h0)


def _lin_scan_bwd(res, g):
    a, h, h0 = res
    n = a.shape[0]
    pos = _rows_of(a.shape) % SUBLANES
    cc, gg = _sup(a, 1, 0.0), g
    k = 1
    while k < SUBLANES:
        keep = pos < SUBLANES - k
        gg = gg + jnp.where(keep, cc * pltpu.roll(gg, n - k, 0), 0.0)
        cc = cc * jnp.where(keep, pltpu.roll(cc, n - k, 0), 1.0)
        k *= 2
    out, carry = [], jnp.zeros_like(h0)
    for r in range(n // SUBLANES - 1, -1, -1):
        rows = slice(r * SUBLANES, (r + 1) * SUBLANES)
        gr = gg[rows] + cc[rows] * carry
        out.append(gr)
        carry = gr[:1]
    gg = jnp.concatenate(out[::-1], axis=0)
    first = _rows_of(a.shape) == 0
    hprev = jnp.where(first, h0, _sdn(h, 1, 0.0))
    dh0 = jnp.sum(jnp.where(first, a * gg, 0.0), axis=0, keepdims=True)
    return gg * hprev, gg, dh0


_lin_scan.defvjp(_lin_scan_fwd, _lin_scan_bwd)


def _cumsum_sub_impl(x):
    pos = _rows_of(x.shape) % HGRN_SUB
    k = 1
    while k < HGRN_SUB:
        x = x + jnp.where(pos >= k, pltpu.roll(x, k, 0), 0.0)
        k *= 2
    return x


@jax.custom_vjp
def _cumsum_sub(x):
    return _cumsum_sub_impl(x)


def _cumsum_sub_fwd(x):
    return _cumsum_sub_impl(x), None


def _cumsum_sub_bwd(_, g):
    n = g.shape[0]
    pos = _rows_of(g.shape) % HGRN_SUB
    k = 1
    while k < HGRN_SUB:
        g = g + jnp.where(pos < HGRN_SUB - k, pltpu.roll(g, n - k, 0), 0.0)
        k *= 2
    return (g,)


_cumsum_sub.defvjp(_cumsum_sub_fwd, _cumsum_sub_bwd)


def _dot(a, b, ca, cb):
    return lax.dot_general(a.astype(BF16), b.astype(BF16), (((ca,), (cb,)), ((), ())), preferred_element_type=F32)


@jax.custom_vjp
def _mm(a, b):
    return _dot(a, b, 1, 0)


def _mm_fwd(a, b):
    return _dot(a, b, 1, 0), (a, b)


def _mm_bwd(res, g):
    a, b = res
    return _dot(g, b, 1, 1), _dot(a, g, 0, 0)


_mm.defvjp(_mm_fwd, _mm_bwd)


@jax.custom_vjp
def _mm_nt(a, b):
    return _dot(a, b, 1, 1)


def _mm_nt_fwd(a, b):
    return _dot(a, b, 1, 1), (a, b)


def _mm_nt_bwd(res, g):
    a, b = res
    return _dot(g, b, 1, 0), _dot(g, a, 0, 0)


_mm_nt.defvjp(_mm_nt_fwd, _mm_nt_bwd)


@jax.custom_vjp
def _mm_tn(a, b):
    return _dot(a, b, 0, 0)


def _mm_tn_fwd(a, b):
    return _dot(a, b, 0, 0), (a, b)


def _mm_tn_bwd(res, g):
    a, b = res
    return _dot(b, g, 1, 1), _dot(a, g, 1, 0)


_mm_tn.defvjp(_mm_tn_fwd, _mm_tn_bwd)


def _head_mask(shape, h):
    return (_lanes_of(shape) // HEAD_DIM) == h


def _stack_heads(x):
    return jnp.concatenate([jnp.where(_head_mask(x.shape, h), x, 0.0) for h in range(N_HEADS)], axis=0)


def _unstack_heads(p):
    r = p.shape[0] // N_HEADS
    out = None
    for h in range(N_HEADS):
        blk = p[h * r:(h + 1) * r]
        term = jnp.where(_head_mask(blk.shape, h), blk, 0.0)
        out = term if out is None else out + term
    return out


def _segmean_impl(x):
    n = x.shape[1]
    same = (lax.broadcasted_iota(jnp.int32, (n, n), 0) // HEAD_DIM) == (lax.broadcasted_iota(jnp.int32, (n, n), 1) // HEAD_DIM)
    m = jnp.where(same, 1.0 / HEAD_DIM, 0.0).astype(BF16)
    hi = x.astype(BF16)
    lo = (x - hi.astype(F32)).astype(BF16)
    dn = (((1,), (0,)), ((), ()))
    return (lax.dot_general(hi, m, dn, preferred_element_type=F32)
            + lax.dot_general(lo, m, dn, preferred_element_type=F32))


@jax.custom_vjp
def _segmean(x):
    return _segmean_impl(x)


def _segmean_fwd(x):
    return _segmean_impl(x), None


def _segmean_bwd(_, g):
    return (_segmean_impl(g),)


_segmean.defvjp(_segmean_fwd, _segmean_bwd)


GELU_C = 0.7978845608028654
GELU_A = 0.044715


@jax.custom_vjp
def _gelu(x):
    return 0.5 * x * (1.0 + jnp.tanh(GELU_C * x * (1.0 + GELU_A * (x * x))))


def _gelu_fwd(x):
    x2 = x * x
    t = jnp.tanh(GELU_C * x * (1.0 + GELU_A * x2))
    return 0.5 * x * (1.0 + t), (x, x2, t)


def _gelu_bwd(res, g):
    x, x2, t = res
    half = 0.5 * (1.0 + t)
    return (g * (half + (0.5 * GELU_C) * x * (1.0 - t * t) * (1.0 + (3.0 * GELU_A) * x2)),)


_gelu.defvjp(_gelu_fwd, _gelu_bwd)


def _log1p(u):
    w = 1.0 + u
    return jnp.where(w == 1.0, u, jnp.log(w) * (u / (w - 1.0)))


def _softplus(y):
    return jnp.maximum(y, 0.0) + _log1p(jnp.exp(-jnp.abs(y)))


def _rms(x, g):
    return x * lax.rsqrt(jnp.mean(x * x, axis=-1, keepdims=True) + EPS) * g


def _gmlp_chunk(zu, zv, ln_g, ln_b, wcat, bfull):
    u = _gelu(zu)
    v = _gelu(zv)
    mu = jnp.mean(v, axis=-1, keepdims=True)
    var = jnp.mean(jnp.square(v - mu), axis=-1, keepdims=True)
    vn = (v - mu) * lax.rsqrt(var + EPS) * ln_g + ln_b
    sv = _unstack_heads(_mm(wcat, vn)) + bfull
    return u * sv


def _rglru_tile(xb_ext, gb, h0, cw, cb, wa, ba, wx, bx, lam):
    xc = (cb + cw[0:1] * _shift_dn(xb_ext, 3) + cw[1:2] * _shift_dn(xb_ext, 2) + cw[2:3] * _shift_dn(xb_ext, 1)
          + cw[3:4] * xb_ext)[8:]
    r = jax.nn.sigmoid(_mm(xc, wa) + ba)
    i = jax.nn.sigmoid(_mm(xc, wx) + bx)
    log_a = (-RGLRU_C) * r * _softplus(-lam)
    a = jnp.exp(log_a)
    mult = jnp.sqrt(-jnp.tanh(log_a) * (a * a + 1.0))
    h = _lin_scan(a, mult * (i * xc), h0)
    y = h * _gelu(gb)
    h_last = jnp.sum(jnp.where(_rows_of(h.shape) == h.shape[0] - 1, h, 0.0), axis=0, keepdims=True)
    return y, h_last


def _pool_tile(xd_ext, inv, wd, scale):
    s1 = xd_ext + _shift_dn(xd_ext, 1)
    s2 = s1 + _shift_dn(s1, 2)
    s3 = s2 + _shift_dn(s2, 4)
    s4 = s3 + _shift_dn(s3, 8)
    grp = _lanes_of(xd_ext.shape) // HEAD_DIM
    win = jnp.where(grp == 0, s1, jnp.where(grp == 1, s2, jnp.where(grp == 2, s3, s4)))
    pooled = win[16:] * inv - xd_ext[16:]
    return _mm(pooled, wd) * scale


def _hgrn_chunk(q, f, i, g, st, lb, ngf):
    n = q.shape[0]
    nsub = n // HGRN_SUB
    qs = jax.nn.silu(q)
    fg = lb + (1.0 - lb) * jax.nn.sigmoid(f)
    lf = jnp.log(fg)
    k = 1.0 - fg
    bl = _cumsum_sub(lf)
    row = _rows_of(q.shape)
    blk = row // HGRN_SUB
    betas = [jnp.zeros_like(lb)]
    for s in range(nsub):
        tot = jnp.sum(jnp.where(row == s * HGRN_SUB + HGRN_SUB - 1, bl, 0.0), axis=0, keepdims=True)
        betas.append(betas[-1] + tot)
    b_end = betas[nsub]
    beta_full = jnp.zeros_like(q)
    for s in range(1, nsub):
        beta_full = jnp.where(blk == s, betas[s], beta_full)
    qh = qs * jnp.exp(bl)
    qt = qh * jnp.exp(beta_full)
    b_all = beta_full + bl
    kt = k * jnp.exp(b_end - b_all)
    outs = []
    for s in range(nsub):
        kh = k * jnp.exp(jnp.minimum(betas[s] - b_all, HGRN_EXP_CLAMP))
        qstk = _stack_heads(qh[s * HGRN_SUB:(s + 1) * HGRN_SUB])
        att = _mm_nt(qstk, kh)
        ar = _rows_of(att.shape) % HGRN_SUB + s * HGRN_SUB
        att = jnp.where(_lanes_of(att.shape) <= ar, att, 0.0)
        outs.append(_unstack_heads(_mm(att, i)))
    o = jnp.concatenate(outs, axis=0) + _mm_nt(qt, st)
    same = (_rows_of(st.shape) // HEAD_DIM) == (_lanes_of(st.shape) // HEAD_DIM)
    st_new = st * jnp.exp(b_end) + jnp.where(same, _mm_tn(i, kt), 0.0)
    on = o * lax.rsqrt(_segmean(o * o) + EPS) * ngf
    return on * jax.nn.silu(g), st_new


def _ffn_tile(eg, ev, wg, bg, wv, bv):
    gt = (bg + wg[0:1] * _shift_dn(eg, 2) + wg[1:2] * _shift_dn(eg, 1) + wg[2:3] * eg)[8:]
    val = (bv + wv[0:1] * _shift_dn(ev, 2) + wv[1:2] * _shift_dn(ev, 1) + wv[2:3] * ev)[8:]
    return _gelu(gt) * val


MXU_WIDTH = 256
MATMUL_BLOCK_BUDGET = 18 * MIB


def _matmul_tiles(m, k, n, a_dtype, b_dtype, out_dtype, has_res):
    best = None
    for tm in (2048, 1024, 512, 256):
        if m % tm:
            continue
        for tn in (1024, 768, 1408, 512, 256, 128):
            if n % tn:
                continue
            blk = (_nbytes((tm, k), a_dtype) + _nbytes((k, tn), b_dtype) + _nbytes((tm, tn), out_dtype)
                   + (_nbytes((tm, tn), F32) if has_res else 0))
            if blk > MATMUL_BLOCK_BUDGET:
                continue
            waste = -(-tn // MXU_WIDTH) * MXU_WIDTH / tn
            cost = (m // tm) * (n // tn) + 64 * (waste - 1.0) + blk / 2 ** 30
            if best is None or cost < best[0]:
                best = (cost, tm, tn, blk)
    assert best is not None, (m, k, n)
    return best[1:]


def _matmul(a, b, *, name, nt=False, res=None, out_dtype=F32):
    m, k = a.shape
    n = b.shape[0] if nt else b.shape[1]
    tm, tn, blk = _matmul_tiles(m, k, n, a.dtype, b.dtype, out_dtype, res is not None)
    dims = (((1,), (1,)), ((), ())) if nt else (((1,), (0,)), ((), ()))

    def body(*refs):
        if res is None:
            a_ref, b_ref, o_ref = refs
        else:
            a_ref, b_ref, r_ref, o_ref = refs
        acc = lax.dot_general(a_ref[...], b_ref[...], dims, preferred_element_type=F32)
        if res is not None:
            acc = acc + r_ref[...]
        o_ref[...] = acc.astype(out_dtype)

    in_specs = [pl.BlockSpec((tm, k), lambda i, j: (i, 0)),
                _spec(b, (tn, k), lambda i, j: (j, 0)) if nt else _spec(b, (k, tn), lambda i, j: (0, j))]
    args = [a, _arr(b)]
    if res is not None:
        in_specs.append(pl.BlockSpec((tm, tn), lambda i, j: (i, j)))
        args.append(res)
    return _pcall(body, name=name, out_shape=_sds((m, n), out_dtype), grid=(m // tm, n // tn), in_specs=in_specs,
                  out_specs=pl.BlockSpec((tm, tn), lambda i, j: (i, j)), semantics=("parallel", "parallel"),
                  block_bytes=blk + _nbytes((tm, tn), F32))(*args)


def _matmul_rms_bwd(a, b, x, g, dres, *, name, nt=False, res=None):
    m, k = a.shape
    n = b.shape[0] if nt else b.shape[1]
    tm = _pick(m, (512, 256))
    dims = (((1,), (1,)), ((), ())) if nt else (((1,), (0,)), ((), ()))

    def body(*refs):
        a_ref, b_ref, x_ref, g_ref, dr_ref = refs[:5]
        dx_ref, dxb_ref, dg_ref = refs[-3:]
        dh = lax.dot_general(a_ref[...], b_ref[...], dims, preferred_element_type=F32)
        if res is not None:
            dh = dh + refs[5][...]
        _, vjp = jax.vjp(_rms, x_ref[...], g_ref[...])
        dxn, dg = vjp(dh)
        dx = dr_ref[...] + dxn
        dx_ref[...] = dx
        dxb_ref[...] = dx.astype(BF16)
        _acc_out(dg_ref, dg, pl.program_id(0) == 0)

    row = pl.BlockSpec((tm, n), lambda i: (i, 0))
    vec = pl.BlockSpec((1, n), lambda i: (0, 0))
    in_specs = [pl.BlockSpec((tm, k), lambda i: (i, 0)),
                _spec(b, (n, k), lambda i: (0, 0)) if nt else _spec(b, (k, n), lambda i: (0, 0)), row, _spec(g), row]
    args = [a, _arr(b), x, _arr(g), dres]
    if res is not None:
        in_specs.append(row)
        args.append(res)
    blk = _nbytes((tm, k), a.dtype) + _nbytes((k, n), b.dtype) + 6 * _nbytes((tm, n), F32)
    return _pcall(body, name=name, out_shape=(_sds((m, n), F32), _sds((m, n), BF16), _sds((1, n), F32)), grid=(m // tm,),
                  in_specs=in_specs, out_specs=(row, row, vec), semantics=("arbitrary",), block_bytes=blk)(*args)


def _ple_rms_bwd(dx3, gl, p, w_pe, w_pg, x, g, *, name):
    m, n = dx3.shape
    tm = _pick(m, (512, 256))

    def body(d3_ref, gl_ref, p_ref, wpe_ref, w_ref, x_ref, g_ref, dx_ref, dxb_ref, dg_ref, dpe_ref, dgl_ref):
        gate = jax.nn.sigmoid(gl_ref[...])
        d3 = d3_ref[...]
        pe = lax.dot_general(p_ref[...], wpe_ref[...], (((1,), (1,)), ((), ())), preferred_element_type=F32)
        dpe_ref[...] = (d3 * gate).astype(BF16)
        dgl = (d3 * pe * gate * (1.0 - gate)).astype(BF16)
        dgl_ref[...] = dgl
        dh = lax.dot_general(dgl, w_ref[...], (((1,), (1,)), ((), ())), preferred_element_type=F32)
        _, vjp = jax.vjp(_rms, x_ref[...], g_ref[...])
        dxn, dg = vjp(dh)
        dx = d3 + dxn
        dx_ref[...] = dx
        dxb_ref[...] = dx.astype(BF16)
        _acc_out(dg_ref, dg, pl.program_id(0) == 0)

    row = pl.BlockSpec((tm, n), lambda i: (i, 0))
    vec = pl.BlockSpec((1, n), lambda i: (0, 0))
    blk = _nbytes((n, n), BF16) + 9 * _nbytes((tm, n), F32)
    return _pcall(body, name=name,
                  out_shape=(_sds((m, n), F32), _sds((m, n), BF16), _sds((1, n), F32), _sds((m, n), BF16), _sds((m, n), BF16)),
                  grid=(m // tm,),
                  in_specs=[row, row, pl.BlockSpec((tm, p.shape[1]), lambda i: (i, 0)), _spec(w_pe), _spec(w_pg), row,
                            _spec(g)],
                  out_specs=(row, row, vec, row, row), semantics=("arbitrary",), block_bytes=blk)(
                      dx3, gl, p, _arr(w_pe), _arr(w_pg), x, _arr(g))


def _matmul_tn(a, b, *, name, out_dtype=BF16, out_rows=None, row_off=0, into=None):
    m, k1 = a.shape
    n = b.shape[1]
    tk = _pick(k1, (512, 256, 128))
    off = row_off // tk
    assert off * tk == row_off

    def body(a_ref, b_ref, *rest):
        rest[-1][...] = lax.dot_general(a_ref[...], b_ref[...], (((0,), (0,)), ((), ())),
                                        preferred_element_type=F32).astype(out_dtype)

    blk = 2 * _nbytes((m, tk), a.dtype) + _nbytes((m, n), b.dtype) + _nbytes((tk, n), F32)
    in_specs = [pl.BlockSpec((m, tk), lambda i: (0, i)), pl.BlockSpec((m, n), lambda i: (0, 0))]
    args = [a, b]
    if into is not None:
        in_specs.append(HBM_SPEC)
        args.append(into)
    return _pcall(body, name=name, out_shape=_sds((out_rows or k1, n), out_dtype), grid=(k1 // tk,), in_specs=in_specs,
                  out_specs=pl.BlockSpec((tk, n), lambda i: (i + off, 0)), semantics=("parallel",), block_bytes=blk,
                  aliases=None if into is None else {2: 0})(*args)


def _rms_matmul(x, g, bs, *, name, nt=False, ple=None):
    m, d = x.shape
    n = bs[0].shape[0] if nt else bs[0].shape[1]
    nb = len(bs)
    nout = nb if ple is None else 2
    best = None
    for tm_c in (1024, 512, 256):
        for tn_c in (1408, 1024, 768, 512, 256, 128):
            if m % tm_c or n % tn_c:
                continue
            blk_c = (_nbytes((tm_c, d), F32) + 2 * _nbytes((tm_c, d), BF16) + nb * _nbytes((d, tn_c), BF16)
                     + (nout + 1) * _nbytes((tm_c, tn_c), F32))
            steps = (m // tm_c) * (n // tn_c)
            if blk_c <= MATMUL_BLOCK_BUDGET and (best is None or steps < best[0]):
                best = (steps, tm_c, tn_c, blk_c)
    _, tm, tn, blk = best
    dims = (((1,), (1,)), ((), ())) if nt else (((1,), (0,)), ((), ()))

    def body(*refs):
        x_ref, g_ref, b_refs = refs[0], refs[1], refs[2:2 + nb]
        rest = refs[2 + nb:]
        h_scr = rest[-1]
        j = pl.program_id(1)

        @pl.when(j == 0)
        def _():
            h = _rms(x_ref[...], g_ref[...]).astype(BF16)
            h_scr[...] = h
            rest[-2 - nout][...] = h

        h = h_scr[...]
        if ple is None:
            for k in range(nb):
                rest[-1 - nb + k][...] = lax.dot_general(h, b_refs[k][...], dims, preferred_element_type=F32)
        else:
            p_ref, wpe_ref = rest[0], rest[1]
            gl_ref, out_ref = rest[-3], rest[-2]
            gl = lax.dot_general(h, b_refs[0][...], dims, preferred_element_type=F32)
            pe = lax.dot_general(p_ref[...], wpe_ref[...], (((1,), (1,)), ((), ())), preferred_element_type=F32)
            gl_ref[...] = gl
            upd = pe * jax.nn.sigmoid(gl)
            for jj in range(n // tn):
                @pl.when(j == jj)
                def _():
                    out_ref[...] = x_ref[:, jj * tn:(jj + 1) * tn] + upd

    row = pl.BlockSpec((tm, d), lambda i, j: (i, 0))
    tile = pl.BlockSpec((tm, tn), lambda i, j: (i, j))
    in_specs = [row, _spec(g)] + [_spec(b, (tn, d), lambda i, j: (j, 0)) if nt else _spec(b, (d, tn), lambda i, j: (0, j))
                                  for b in bs]
    args = [x, _arr(g)] + [_arr(b) for b in bs]
    out_shape, out_specs = [_sds((m, d), BF16)], [row]
    if ple is None:
        out_shape += [_sds((m, n), F32)] * nb
        out_specs += [tile] * nb
    else:
        p, wpe = ple
        in_specs += [pl.BlockSpec((tm, p.shape[1]), lambda i, j: (i, 0)), _spec(wpe, (tn, p.shape[1]), lambda i, j: (j, 0))]
        args += [p, _arr(wpe)]
        out_shape += [_sds((m, n), F32)] * 2
        out_specs += [tile] * 2
    outs = _pcall(body, name=name, out_shape=tuple(out_shape), grid=(m // tm, n // tn), in_specs=in_specs,
                  out_specs=tuple(out_specs), scratch_shapes=[pltpu.VMEM((tm, d), BF16)],
                  semantics=("parallel", "arbitrary"), block_bytes=blk)(*args)
    return outs[0], list(outs[1:])


def _up_ffn_fwd(x, g, wg, wv, cwf, cbf, *, name):
    m, d = x.shape
    n = wg.shape[0]
    tm = _pick(m, (256, 128))
    tn = _pick(n, (1408, 256, 128))
    nj = n // tn
    dims = (((1,), (1,)), ((), ()))

    def body(x_ref, g_ref, wg_ref, wv_ref, tg_ref, bg_ref, tv_ref, bv_ref, h_ref, hg_ref, hv_ref, a_ref, cg_scr, cv_scr):
        i = pl.program_id(1)
        h = _rms(x_ref[...], g_ref[...]).astype(BF16)
        h_ref[...] = h
        hg = lax.dot_general(h, wg_ref[...], dims, preferred_element_type=F32)
        hv = lax.dot_general(h, wv_ref[...], dims, preferred_element_type=F32)
        hg_ref[...] = hg
        hv_ref[...] = hv
        eg = jnp.concatenate([jnp.where(i == 0, 0.0, cg_scr[...]), hg], axis=0)
        ev = jnp.concatenate([jnp.where(i == 0, 0.0, cv_scr[...]), hv], axis=0)
        a_ref[...] = _ffn_tile(eg, ev, tg_ref[...], bg_ref[...], tv_ref[...], bv_ref[...]).astype(BF16)
        cg_scr[...] = hg[tm - 8:]
        cv_scr[...] = hv[tm - 8:]

    row = pl.BlockSpec((tm, d), lambda j, i: (i, 0))
    hrow = pl.BlockSpec((tm, d), lambda j, i: (j * (m // tm) + i, 0))
    tile = pl.BlockSpec((tm, tn), lambda j, i: (i, j))
    wspec = lambda w: _spec(w, (tn, d), lambda j, i: (j, 0))
    taps = lambda off: _spec(cwf, (3, tn), lambda j, i: (0, j + off))
    bias = lambda off: _spec(cbf, (1, tn), lambda j, i: (0, j + off))
    blk = (_nbytes((tm, d), F32) + _nbytes((tm, d), BF16) + 2 * _nbytes((tn, d), BF16) + 12 * _nbytes((tm, tn), F32))
    return _pcall(body, name=name,
                  out_shape=(_sds((nj * m, d), BF16), _sds((m, n), F32), _sds((m, n), F32), _sds((m, n), BF16)),
                  grid=(nj, m // tm),
                  in_specs=[row, _spec(g), wspec(wg), wspec(wv), taps(0), bias(0), taps(nj), bias(nj)],
                  out_specs=(hrow, tile, tile, tile),
                  scratch_shapes=[pltpu.VMEM((8, tn), F32), pltpu.VMEM((8, tn), F32)],
                  semantics=("arbitrary", "arbitrary"), block_bytes=blk)(
                      x, _arr(g), _arr(wg), _arr(wv), _arr(cwf), _arr(cbf), _arr(cwf), _arr(cbf))


def _loss_head(x, g, target, *, name):
    s, d = x.shape
    tm = _pick(s, (256, 128))

    def tile_loss(xv, gv, tv):
        err = jnp.square(_rms(xv, gv) - tv)
        return 0.5 * jnp.sum(jnp.mean(err, axis=-1, keepdims=True), axis=0, keepdims=True)

    def body(x_ref, g_ref, t_ref, l_ref, dx_ref, dg_ref):
        lv, vjp = jax.vjp(tile_loss, x_ref[...], g_ref[...], t_ref[...])
        dxv, dgv, _ = vjp(jnp.ones((1, 1), F32))
        dx_ref[...] = dxv

        @pl.when(pl.program_id(0) == 0)
        def _():
            l_ref[...] = jnp.zeros_like(l_ref)
            dg_ref[...] = jnp.zeros_like(dg_ref)

        l_ref[...] += jnp.broadcast_to(lv, l_ref.shape)
        dg_ref[...] += dgv

    row = pl.BlockSpec((tm, d), lambda i: (i, 0))
    vec = pl.BlockSpec((1, d), lambda i: (0, 0))
    return _pcall(body, name=name, out_shape=(_sds((8, 128), F32), _sds((s, d), F32), _sds((1, d), F32)),
                  grid=(s // tm,), in_specs=[row, vec, row],
                  out_specs=(pl.BlockSpec((8, 128), lambda i: (0, 0)), row, vec), semantics=("arbitrary",),
                  block_bytes=8 * _nbytes((tm, d), F32))(x, g, target)


def _acc_out(ref, val, first):
    @pl.when(first)
    def _():
        ref[...] = jnp.zeros_like(ref)

    ref[...] += val


def _gmlp_fwd(z, ln_g, ln_b, wcat, bfull, *, name):
    s = z.shape[0]
    t = _pick(s, (512, 256, 128))
    nch = t // GMLP_CHUNK

    def body(zu_ref, zv_ref, g_ref, b_ref, w_ref, bf_ref, o_ref):
        for c in range(nch):
            rows = pl.ds(c * GMLP_CHUNK, GMLP_CHUNK)
            o_ref[rows, :] = _gmlp_chunk(zu_ref[rows, :], zv_ref[rows, :], g_ref[...], b_ref[...], w_ref[...],
                                         bf_ref[...]).astype(BF16)

    col = lambda c: pl.BlockSpec((t, W_GRP), lambda i: (i, c))
    params = (ln_g, ln_b, wcat, bfull)
    return _pcall(body, name=name, out_shape=_sds((s, D_MODEL), BF16), grid=(s // t,),
                  in_specs=[col(0), col(1)] + [_spec(a) for a in params],
                  out_specs=pl.BlockSpec((t, W_GRP), lambda i: (i, 0)), semantics=("parallel",),
                  block_bytes=4 * _nbytes((t, W_GRP), F32))(z, z, *[_arr(a) for a in params])


def _gmlp_bwd(z, dmix, ln_g, ln_b, wcat, bfull, *, name):
    s = z.shape[0]
    t = _pick(s, (512, 256, 128))
    nch = t // GMLP_CHUNK

    def body(zu_ref, zv_ref, dy_ref, g_ref, b_ref, w_ref, bf_ref, dz_ref, dg_ref, db_ref, dw_ref, dbf_ref):
        acc = None
        for c in range(nch):
            rows = pl.ds(c * GMLP_CHUNK, GMLP_CHUNK)
            _, vjp = jax.vjp(_gmlp_chunk, zu_ref[rows, :], zv_ref[rows, :], g_ref[...], b_ref[...], w_ref[...],
                             bf_ref[...])
            du, dv, *dps = vjp(dy_ref[rows, :])
            dz_ref[rows, :] = jnp.concatenate([du, dv], axis=1).astype(BF16)
            acc = dps if acc is None else [x + y for x, y in zip(acc, dps)]
        first = pl.program_id(0) == 0
        for ref, val in zip((dg_ref, db_ref, dw_ref, dbf_ref), acc):
            _acc_out(ref, val, first)

    col = lambda c: pl.BlockSpec((t, W_GRP), lambda i: (i, c))
    params = (ln_g, ln_b, wcat, bfull)
    return _pcall(body, name=name,
                  out_shape=(_sds((s, D_PROJ), BF16),) + tuple(_sds(a.shape, F32) for a in params),
                  grid=(s // t,), in_specs=[col(0), col(1), col(0)] + [_spec(a) for a in params],
                  out_specs=(pl.BlockSpec((t, 2 * W_GRP), lambda i: (i, 0)),) + tuple(_ospec(a) for a in params),
                  semantics=("arbitrary",),
                  block_bytes=8 * _nbytes((t, W_GRP), F32))(z, z, dmix, *[_arr(a) for a in params])


def _rglru_fwd(z, prm, mix, *, name):
    s = z.shape[0]
    t = _pick(s, (512, 256, 128))
    nt = s // t

    def body(xb_ref, halo_ref, gb_ref, *rest):
        prm_refs, (y_ref, h0s_ref, h_scr) = rest[:len(prm)], rest[len(prm) + 1:]
        i = pl.program_id(0)

        @pl.when(i == 0)
        def _():
            h_scr[...] = jnp.zeros_like(h_scr)

        halo = jnp.where(i == 0, 0.0, halo_ref[...])
        h0 = h_scr[...]
        y, h_last = _rglru_tile(jnp.concatenate([halo, xb_ref[...]], axis=0), gb_ref[...], h0,
                                *[r[...] for r in prm_refs])
        y_ref[...] = y.astype(BF16)
        h0s_ref[...] = jnp.broadcast_to(h0, h0s_ref.shape)
        h_scr[...] = h_last

    in_specs = [pl.BlockSpec((t, W_GRP), lambda i: (i, 2)),
                pl.BlockSpec((8, W_GRP), lambda i: (jnp.maximum(i * (t // 8) - 1, 0), 2)),
                pl.BlockSpec((t, W_GRP), lambda i: (i, 3))] + [_spec(a) for a in prm] + [HBM_SPEC]
    return _pcall(body, name=name, out_shape=(_sds(mix.shape, BF16), _sds((nt, 8, W_GRP), F32)), grid=(nt,),
                  in_specs=in_specs,
                  out_specs=(pl.BlockSpec((t, W_GRP), lambda i: (i, 1)), pl.BlockSpec((None, 8, W_GRP), lambda i: (i, 0, 0))),
                  scratch_shapes=[pltpu.VMEM((1, W_GRP), F32)], semantics=("arbitrary",),
                  block_bytes=24 * _nbytes((t, W_GRP), F32), aliases={3 + len(prm): 0})(
                      z, z, z, *[_arr(a) for a in prm], mix)


def _rglru_bwd(z, dmix, h0s, prm, dz, *, name):
    s = z.shape[0]
    t = _pick(s, (512, 256, 128))
    nt = s // t
    npm = len(prm)

    def body(xb_ref, halo_ref, gb_ref, dy_ref, h0s_ref, *rest):
        prm_refs = rest[:npm]
        dz_ref = rest[npm + 1]
        dprm_refs = rest[npm + 2:2 * npm + 2]
        dh_scr, dhalo_scr = rest[2 * npm + 2:]
        i = pl.program_id(0)
        r = nt - 1 - i

        @pl.when(i == 0)
        def _():
            dh_scr[...] = jnp.zeros_like(dh_scr)
            dhalo_scr[...] = jnp.zeros_like(dhalo_scr)

        halo = jnp.where(r == 0, 0.0, halo_ref[...])
        h0 = h0s_ref[0:1, :]
        _, vjp = jax.vjp(_rglru_tile, jnp.concatenate([halo, xb_ref[...]], axis=0), gb_ref[...], h0,
                         *[p[...] for p in prm_refs])
        dext, dgb, _dh0, *dps = vjp((dy_ref[...], dh_scr[...]))
        dmain = dext[8:]
        dxb = jnp.concatenate([dmain[:t - 8], dmain[t - 8:] + dhalo_scr[...]], axis=0)
        dz_ref[...] = jnp.concatenate([dxb, dgb], axis=1).astype(BF16)
        dh_scr[...] = _dh0
        dhalo_scr[...] = dext[:8]
        for ref, val in zip(dprm_refs, dps):
            _acc_out(ref, val, i == 0)

    rev = lambda c: pl.BlockSpec((t, W_GRP), lambda i: (nt - 1 - i, c))
    in_specs = [rev(2), pl.BlockSpec((8, W_GRP), lambda i: (jnp.maximum((nt - 1 - i) * (t // 8) - 1, 0), 2)), rev(3),
                rev(1), pl.BlockSpec((None, 8, W_GRP), lambda i: (nt - 1 - i, 0, 0))] + [_spec(a) for a in prm] + [HBM_SPEC]
    return _pcall(body, name=name,
                  out_shape=(_sds(dz.shape, BF16),) + tuple(_sds(a.shape, F32) for a in prm),
                  grid=(nt,), in_specs=in_specs,
                  out_specs=(pl.BlockSpec((t, 2 * W_GRP), lambda i: (nt - 1 - i, 1)),) + tuple(_ospec(a) for a in prm),
                  scratch_shapes=[pltpu.VMEM((1, W_GRP), F32), pltpu.VMEM((8, W_GRP), F32)],
                  semantics=("arbitrary",), block_bytes=40 * _nbytes((t, W_GRP), F32), aliases={5 + npm: 0})(
                      z, z, z, dmix, h0s, *[_arr(a) for a in prm], dz)


def _pool_inv(i, t):
    pos = (_rows_of((t, W_GRP)) + i * t + 1).astype(F32)
    grp = _lanes_of((t, W_GRP)) // HEAD_DIM
    win = jnp.where(grp == 0, float(POOL_WINDOWS[0]), jnp.where(grp == 1, float(POOL_WINDOWS[1]),
                    jnp.where(grp == 2, float(POOL_WINDOWS[2]), float(POOL_WINDOWS[3]))))
    return 1.0 / jnp.minimum(pos, win)


def _pool_fwd(z, wd, scale, mix, *, name):
    s = z.shape[0]
    t = _pick(s, (512, 256, 128))

    def body(x_ref, halo_ref, wd_ref, sc_ref, _, y_ref):
        i = pl.program_id(0)
        halo = jnp.where(i == 0, 0.0, halo_ref[...])
        y = _pool_tile(jnp.concatenate([halo, x_ref[...]], axis=0), _pool_inv(i, t), wd_ref[...], sc_ref[...])
        y_ref[...] = y.astype(BF16)

    in_specs = [pl.BlockSpec((t, W_GRP), lambda i: (i, 8)),
                pl.BlockSpec((16, W_GRP), lambda i: (jnp.maximum(i * (t // 16) - 1, 0), 8)), _spec(wd), _spec(scale),
                HBM_SPEC]
    return _pcall(body, name=name, out_shape=_sds(mix.shape, BF16), grid=(s // t,), in_specs=in_specs,
                  out_specs=pl.BlockSpec((t, W_GRP), lambda i: (i, 3)), semantics=("parallel",),
                  block_bytes=12 * _nbytes((t, W_GRP), F32), aliases={4: 0})(z, z, _arr(wd), _arr(scale), mix)


def _pool_bwd(z, dmix, wd, scale, dz, *, name):
    s = z.shape[0]
    t = _pick(s, (512, 256, 128))
    nt = s // t

    def body(x_ref, halo_ref, dy_ref, wd_ref, sc_ref, _, dx_ref, dwd_ref, dsc_ref, dhalo_scr):
        i = pl.program_id(0)
        r = nt - 1 - i

        @pl.when(i == 0)
        def _():
            dhalo_scr[...] = jnp.zeros_like(dhalo_scr)

        halo = jnp.where(r == 0, 0.0, halo_ref[...])
        inv = _pool_inv(r, t)
        _, vjp = jax.vjp(lambda e, w, sc: _pool_tile(e, inv, w, sc), jnp.concatenate([halo, x_ref[...]], axis=0),
                         wd_ref[...], sc_ref[...])
        dext, dwd, dsc = vjp(dy_ref[...])
        dmain = dext[16:]
        dx = jnp.concatenate([dmain[:t - 16], dmain[t - 16:] + dhalo_scr[...]], axis=0)
        dx_ref[...] = dx.astype(BF16)
        dhalo_scr[...] = dext[:16]
        _acc_out(dwd_ref, dwd, i == 0)
        _acc_out(dsc_ref, dsc, i == 0)

    rev = lambda c: pl.BlockSpec((t, W_GRP), lambda i: (nt - 1 - i, c))
    in_specs = [rev(8), pl.BlockSpec((16, W_GRP), lambda i: (jnp.maximum((nt - 1 - i) * (t // 16) - 1, 0), 8)), rev(3),
                _spec(wd), _spec(scale), HBM_SPEC]
    return _pcall(body, name=name, out_shape=(_sds(dz.shape, BF16), _sds(wd.shape, F32), _sds(scale.shape, F32)),
                  grid=(nt,), in_specs=in_specs, out_specs=(rev(8), _ospec(wd), _ospec(scale)),
                  scratch_shapes=[pltpu.VMEM((16, W_GRP), F32)], semantics=("arbitrary",),
                  block_bytes=20 * _nbytes((t, W_GRP), F32), aliases={5: 0})(z, z, dmix, _arr(wd), _arr(scale), dz)


def _hgrn_fwd(z, lb, ngf, mix, *, name):
    s = z.shape[0]
    c = HGRN_CHUNK
    per = HGRN_STEP_CHUNKS
    ns = s // (c * per)

    def body(q_ref, f_ref, i_ref, g_ref, lb_ref, ng_ref, _, y_ref, sts_ref, st_scr):
        @pl.when(pl.program_id(0) == 0)
        def _():
            st_scr[...] = jnp.zeros_like(st_scr)

        st = st_scr[...]
        for k in range(per):
            rows = pl.ds(k * c, c)
            sts_ref[k] = st
            y, st = _hgrn_chunk(q_ref[rows, :], f_ref[rows, :], i_ref[rows, :], g_ref[rows, :], st, lb_ref[...],
                                ng_ref[...])
            y_ref[rows, :] = y.astype(BF16)
        st_scr[...] = st

    col = lambda k: pl.BlockSpec((per * c, W_GRP), lambda i: (i, k))
    return _pcall(body, name=name, out_shape=(_sds(mix.shape, BF16), _sds((ns * per, W_GRP, W_GRP), F32)), grid=(ns,),
                  in_specs=[col(4), col(5), col(6), col(7), _spec(lb), _spec(ngf), HBM_SPEC],
                  out_specs=(pl.BlockSpec((per * c, W_GRP), lambda i: (i, 2)),
                             pl.BlockSpec((per, W_GRP, W_GRP), lambda i: (i, 0, 0))),
                  scratch_shapes=[pltpu.VMEM((W_GRP, W_GRP), F32)], semantics=("arbitrary",),
                  block_bytes=16 * per * _nbytes((W_GRP, W_GRP), F32), aliases={6: 0})(
                      z, z, z, z, _arr(lb), _arr(ngf), mix)


def _hgrn_bwd(z, dmix, sts, lb, ngf, dz, *, name):
    s = z.shape[0]
    c = HGRN_CHUNK
    per = HGRN_STEP_CHUNKS
    ns = s // (c * per)

    def body(q_ref, f_ref, i_ref, g_ref, dy_ref, st_ref, lb_ref, ng_ref, _, dz_ref, dlb_ref, dng_ref, dst_scr):
        i = pl.program_id(0)

        @pl.when(i == 0)
        def _():
            dst_scr[...] = jnp.zeros_like(dst_scr)

        dst = dst_scr[...]
        dlb_sum = dng_sum = None
        for k in range(per - 1, -1, -1):
            rows = pl.ds(k * c, c)
            _, vjp = jax.vjp(_hgrn_chunk, q_ref[rows, :], f_ref[rows, :], i_ref[rows, :], g_ref[rows, :], st_ref[k],
                             lb_ref[...], ng_ref[...])
            dq, df, di, dg, dst, dlb, dng = vjp((dy_ref[rows, :], dst))
            dz_ref[rows, :] = jnp.concatenate([dq, df, di, dg], axis=1).astype(BF16)
            dlb_sum = dlb if dlb_sum is None else dlb_sum + dlb
            dng_sum = dng if dng_sum is None else dng_sum + dng
        dst_scr[...] = dst
        _acc_out(dlb_ref, dlb_sum, i == 0)
        _acc_out(dng_ref, dng_sum, i == 0)

    rev = lambda k: pl.BlockSpec((per * c, W_GRP), lambda i: (ns - 1 - i, k))
    vec = pl.BlockSpec((1, W_GRP), lambda i: (0, 0))
    return _pcall(body, name=name, out_shape=(_sds(dz.shape, BF16), _sds((1, W_GRP), F32), _sds((1, W_GRP), F32)),
                  grid=(ns,),
                  in_specs=[rev(4), rev(5), rev(6), rev(7), rev(2),
                            pl.BlockSpec((per, W_GRP, W_GRP), lambda i: (ns - 1 - i, 0, 0)), _spec(lb), _spec(ngf),
                            HBM_SPEC],
                  out_specs=(pl.BlockSpec((per * c, 4 * W_GRP), lambda i: (ns - 1 - i, 1)), vec, vec),
                  scratch_shapes=[pltpu.VMEM((W_GRP, W_GRP), F32)], semantics=("arbitrary",),
                  block_bytes=32 * per * _nbytes((W_GRP, W_GRP), F32), aliases={8: 0})(
                      z, z, z, z, dmix, sts, _arr(lb), _arr(ngf), dz)


def _lbs_fwd(c_lb, *, name):
    def body(c_ref, o_ref):
        c = c_ref[...]
        e = jnp.exp(c - jnp.max(c, axis=0, keepdims=True))
        sm = e / jnp.sum(e, axis=0, keepdims=True)
        run = jnp.zeros((1, W_GRP), F32)
        o_ref[0:1, :] = run
        for l in range(1, DEPTH):
            run = run + sm[l:l + 1]
            o_ref[l:l + 1, :] = run

    return _pcall(body, name=name, out_shape=_sds((DEPTH, W_GRP), F32), pin=False)(c_lb)


def _lbs_bwd(c_lb, dlbs, *, name):
    def body(c_ref, d_ref, o_ref):
        c = c_ref[...]
        e = jnp.exp(c - jnp.max(c, axis=0, keepdims=True))
        sm = e / jnp.sum(e, axis=0, keepdims=True)
        d = d_ref[...]
        dsm = [None] * DEPTH
        run = jnp.zeros((1, W_GRP), F32)
        for l in range(DEPTH - 1, 0, -1):
            run = run + d[l:l + 1]
            dsm[l] = run
        dsm[0] = jnp.zeros((1, W_GRP), F32)
        inner = sum(sm[l:l + 1] * dsm[l] for l in range(DEPTH))
        for l in range(DEPTH):
            o_ref[l:l + 1, :] = sm[l:l + 1] * (dsm[l] - inner)

    return _pcall(body, name=name, out_shape=_sds((DEPTH, W_GRP), F32), pin=False)(c_lb, dlbs)


def _ffn_bwd(hg, hv, dx, w_down, cwf, cbf, *, name):
    s, n = hg.shape
    t = _pick(s, (256, 128))
    cw = _pick(n, (1408, 256, 128))
    nt = s // t
    nj = n // cw

    def body(g_ref, gh_ref, v_ref, vh_ref, dx_ref, wd_ref, wg_ref, bg_ref, wv_ref, bv_ref, dg_ref, dv_ref, dwg_ref,
             dwv_ref, cg_scr, cv_scr):
        i = pl.program_id(1)
        r = nt - 1 - i

        @pl.when(i == 0)
        def _():
            cg_scr[...] = jnp.zeros_like(cg_scr)
            cv_scr[...] = jnp.zeros_like(cv_scr)

        da = lax.dot_general(dx_ref[...], wd_ref[...], (((1,), (1,)), ((), ())), preferred_element_type=F32)
        eg = jnp.concatenate([jnp.where(r == 0, 0.0, gh_ref[...]), g_ref[...]], axis=0)
        ev = jnp.concatenate([jnp.where(r == 0, 0.0, vh_ref[...]), v_ref[...]], axis=0)
        _, vjp = jax.vjp(_ffn_tile, eg, ev, wg_ref[...], bg_ref[...], wv_ref[...], bv_ref[...])
        deg, dev, dwg, dbg, dwv, dbv = vjp(da)
        for dext, scr, ref in ((deg, cg_scr, dg_ref), (dev, cv_scr, dv_ref)):
            dmain = dext[8:]
            ref[...] = jnp.concatenate([dmain[:t - 8], dmain[t - 8:] + scr[...]], axis=0).astype(BF16)
            scr[...] = dext[:8]
        zeros = jnp.zeros((4, cw), F32)
        _acc_out(dwg_ref, jnp.concatenate([dwg, dbg, zeros], axis=0), i == 0)
        _acc_out(dwv_ref, jnp.concatenate([dwv, dbv, zeros], axis=0), i == 0)

    main = pl.BlockSpec((t, cw), lambda j, i: (nt - 1 - i, j))
    halo = pl.BlockSpec((8, cw), lambda j, i: (jnp.maximum((nt - 1 - i) * (t // 8) - 1, 0), j))
    taps = lambda off: _spec(cwf, (3, cw), lambda j, i: (0, j + off))
    bias = lambda off: _spec(cbf, (1, cw), lambda j, i: (0, j + off))
    w8 = pl.BlockSpec((8, cw), lambda j, i: (0, j))
    d = dx.shape[1]
    in_specs = [main, halo, main, halo, pl.BlockSpec((t, d), lambda j, i: (nt - 1 - i, 0)),
                _spec(w_down, (cw, d), lambda j, i: (j, 0)), taps(0), bias(0), taps(nj), bias(nj)]
    return _pcall(body, name=name,
                  out_shape=(_sds((s, n), BF16), _sds((s, n), BF16), _sds((8, n), F32), _sds((8, n), F32)),
                  grid=(nj, nt), in_specs=in_specs, out_specs=(main, main, w8, w8),
                  scratch_shapes=[pltpu.VMEM((8, cw), F32), pltpu.VMEM((8, cw), F32)],
                  semantics=("parallel", "arbitrary"),
                  block_bytes=24 * _nbytes((t, cw), F32) + _nbytes((cw, d), BF16))(
                      hg, hg, hv, hv, dx, _arr(w_down), _arr(cwf), _arr(cbf), _arr(cwf), _arr(cbf))


def _all_gather(x, *, name):
    r, c = x.shape

    def body(x_ref, out_ref, send_sems, recv_sems, local_sem):
        mx, my, mc = lax.axis_index("x"), lax.axis_index("y"), lax.axis_index("c")
        me, sibling = (mx, my, mc), (mx, my, 1 - mc)
        chips = [(1 - mx, my), (mx, 1 - my), (1 - mx, 1 - my)]

        def slot(px, py, pc):
            return out_ref.at[4 * px + 2 * py + pc]

        def copy(k, block, to, src=None):
            return pltpu.make_async_remote_copy(src_ref=slot(*block) if src is None else src, dst_ref=slot(*block),
                                                send_sem=send_sems.at[k], recv_sem=recv_sems.at[k],
                                                device_id=to, device_id_type=MESH)

        mine = pltpu.make_async_copy(x_ref, slot(*me), local_sem)
        mine.start()
        first = [copy(0, me, sibling, src=x_ref)]
        first += [copy(1 + j, me, (*chip, mc), src=x_ref) for j, chip in enumerate(chips)]
        for cp in first:
            cp.start()
        passed = [copy(4 + j, (*chip, mc), sibling) for j, chip in enumerate(chips)]
        for j, chip in enumerate(chips):
            copy(1 + j, (*chip, mc), me).wait_recv()
            passed[j].start()
        copy(0, sibling, me).wait_recv()
        for j, chip in enumerate(chips):
            copy(4 + j, (*chip, 1 - mc), me).wait_recv()
        for cp in first + passed:
            cp.wait_send()
        mine.wait()

    hbm = pl.BlockSpec(memory_space=pl.ANY)
    return _pcall(body, name=name, out_shape=_sds((N_DEV, r, c), x.dtype), in_specs=[hbm], out_specs=hbm,
                  scratch_shapes=[pltpu.SemaphoreType.DMA((7,)), pltpu.SemaphoreType.DMA((7,)),
                                  pltpu.SemaphoreType.DMA(())])(x)


def _sum_slots(p, *, name):
    q, r, c = p.shape
    tr = _pick(r, (544, 408, 272, 192, 136, 64, 32, 16, 8))

    def body(p_ref, o_ref):
        acc = p_ref[0].astype(F32)
        for k in range(1, q):
            acc = acc + p_ref[k].astype(F32)
        o_ref[...] = acc

    return _pcall(body, name=name, out_shape=_sds((r, c), F32), grid=(r // tr,),
                  in_specs=[pl.BlockSpec((q, tr, c), lambda i: (0, i, 0))],
                  out_specs=pl.BlockSpec((tr, c), lambda i: (i, 0)), semantics=("parallel",),
                  block_bytes=(q + 2) * _nbytes((tr, c), F32))(p)


BIG_COMM = (('w_in', 288, D_MODEL), ('w_out', 128, D_MODEL), ('w_up', 704, D_MODEL), ('w_down', 352, D_MODEL),
            ('w_pe', 128, PLE_DIM), ('w_pg', 128, D_MODEL))
HBM_SPEC = pl.BlockSpec(memory_space=pl.ANY)


def _gather_layer(shards, l, *, name):
    na = len(shards)

    def body(*refs):
        x_refs, out_refs = refs[:na], refs[na:2 * na]
        send_sems, recv_sems, local_sems = refs[2 * na:]
        mx, my, mc = lax.axis_index("x"), lax.axis_index("y"), lax.axis_index("c")
        me, sibling = (mx, my, mc), (mx, my, 1 - mc)
        chips = [(1 - mx, my), (mx, 1 - my), (1 - mx, 1 - my)]

        def slot(a, px, py, pc):
            return out_refs[a].at[4 * px + 2 * py + pc]

        def copy(k, a, block, to, own=False):
            return pltpu.make_async_remote_copy(src_ref=x_refs[a].at[l] if own else slot(a, *block),
                                                dst_ref=slot(a, *block), send_sem=send_sems.at[k, a],
                                                recv_sem=recv_sems.at[k, a], device_id=to, device_id_type=MESH)

        mine = [pltpu.make_async_copy(x_refs[a].at[l], slot(a, *me), local_sems.at[a]) for a in range(na)]
        for cp in mine:
            cp.start()
        first = []
        for a in range(na):
            first.append(copy(0, a, me, sibling, own=True))
            first += [copy(1 + j, a, me, (*chip, mc), own=True) for j, chip in enumerate(chips)]
        for cp in first:
            cp.start()
        passed = []
        for j, chip in enumerate(chips):
            for a in range(na):
                copy(1 + j, a, (*chip, mc), me).wait_recv()
                fwd = copy(4 + j, a, (*chip, mc), sibling)
                fwd.start()
                passed.append(fwd)
        for a in range(na):
            copy(0, a, sibling, me).wait_recv()
        for j, chip in enumerate(chips):
            for a in range(na):
                copy(4 + j, a, (*chip, 1 - mc), me).wait_recv()
        for cp in first + passed:
            cp.wait_send()
        for cp in mine:
            cp.wait()

    return _pcall(body, name=name, out_shape=tuple(_sds((N_DEV,) + x.shape[1:], x.dtype) for x in shards),
                  in_specs=[HBM_SPEC] * na, out_specs=(HBM_SPEC,) * na,
                  scratch_shapes=[pltpu.SemaphoreType.DMA((7, na)), pltpu.SemaphoreType.DMA((7, na)),
                                  pltpu.SemaphoreType.DMA((na,))])(*shards)


SEM_SPEC = pl.BlockSpec(memory_space=pltpu.SEMAPHORE)
DATAFLOW_EFFECT = pltpu.SideEffectType.DATAFLOW_SIDE_EFFECTING


def _place_own(srcs, after, *, name):
    na = len(srcs)

    def body(*refs):
        x_refs, land_refs, sems = refs[:na], refs[na + len(after):2 * na + len(after)], refs[-1]
        me = 4 * lax.axis_index("x") + 2 * lax.axis_index("y") + lax.axis_index("c")
        cps = [pltpu.make_async_copy(x_refs[a], land_refs[a].at[me], sems.at[a]) for a in range(na)]
        for cp in cps:
            cp.start()
        for cp in cps:
            cp.wait()

    return _pcall(body, name=name, out_shape=tuple(_sds((N_DEV,) + x.shape, x.dtype) for x in srcs),
                  in_specs=[HBM_SPEC] * (na + len(after)), out_specs=(HBM_SPEC,) * na,
                  scratch_shapes=[pltpu.SemaphoreType.DMA((na,))], pin=False)(*srcs, *after)


def _exchange_start(srcs, lands, *, name, per_peer=False):
    na = len(srcs)

    def body(*refs):
        x_refs, land_refs = refs[:na], refs[na:2 * na]
        send_sems, recv_sems = refs[2 * na], refs[2 * na + 1]
        token = refs[-1]
        mx, my, mc = lax.axis_index("x"), lax.axis_index("y"), lax.axis_index("c")
        me = 4 * mx + 2 * my + mc
        peers = [(mx, my, 1 - mc)]
        for px, py in ((1 - mx, my), (mx, 1 - my), (1 - mx, 1 - my)):
            peers += [(px, py, mc), (px, py, 1 - mc)]
        for a in range(na):
            for peer in peers:
                src = x_refs[a].at[4 * peer[0] + 2 * peer[1] + peer[2]] if per_peer else x_refs[a]
                pltpu.make_async_remote_copy(src_ref=src, dst_ref=land_refs[a].at[me], send_sem=send_sems.at[a],
                                             recv_sem=recv_sems.at[a], device_id=peer, device_id_type=MESH).start()
        token[...] = jnp.zeros_like(token)

    hbm = lambda x: pltpu.HBM(x.shape, x.dtype)
    out_shape = ((pltpu.SemaphoreType.DMA((na,)), pltpu.SemaphoreType.DMA((na,))) + tuple(hbm(x) for x in srcs)
                 + tuple(hbm(x) for x in lands) + (_sds((8, 128), F32),))
    params = pltpu.CompilerParams(has_side_effects=DATAFLOW_EFFECT)
    pin = lambda x: pltpu.with_memory_space_constraint(x, pltpu.HBM)
    return pl.pallas_call(body, name=name, out_shape=out_shape, in_specs=[HBM_SPEC] * (2 * na),
                          out_specs=(SEM_SPEC, SEM_SPEC) + (HBM_SPEC,) * (2 * na) + (pl.BlockSpec(memory_space=pltpu.VMEM),),
                          input_output_aliases={i: 2 + i for i in range(2 * na)}, compiler_params=params)(
                              *[pin(x) for x in srcs], *[pin(x) for x in lands])


def _exchange_wait(started, after, *, name):
    send_sems, recv_sems, *bufs, _ = started
    na = len(bufs) // 2

    def body(*refs):
        land_refs = refs[na:2 * na]
        s_sems, r_sems = refs[2 * na], refs[2 * na + 1]
        me = (lax.axis_index("x"), lax.axis_index("y"), lax.axis_index("c"))
        for a in range(na):
            seven = land_refs[a].at[pl.ds(0, N_DEV - 1)]
            cp = pltpu.make_async_remote_copy(src_ref=seven, dst_ref=seven, send_sem=s_sems.at[a], recv_sem=r_sems.at[a],
                                              device_id=me, device_id_type=MESH)
            cp.wait_send()
            cp.wait_recv()

    hbm = lambda x: pltpu.HBM(x.shape, x.dtype)
    params = pltpu.CompilerParams(has_side_effects=DATAFLOW_EFFECT)
    outs = pl.pallas_call(body, name=name, out_shape=tuple(hbm(x) for x in bufs),
                          in_specs=[HBM_SPEC] * (2 * na) + [SEM_SPEC, SEM_SPEC, HBM_SPEC],
                          out_specs=(HBM_SPEC,) * (2 * na), input_output_aliases={i: i for i in range(2 * na)},
                          compiler_params=params)(*bufs, send_sems, recv_sems, after)
    return outs[:na], outs[na:]


def _pair_swap(grads, *, name):
    na = len(grads)

    def body(*refs):
        g_refs, recv_refs = refs[:na], refs[na:2 * na]
        send_sems, recv_sems = refs[2 * na:]
        mx, my, mc = lax.axis_index("x"), lax.axis_index("y"), lax.axis_index("c")
        sibling = (mx, my, 1 - mc)
        for a in range(na):
            for q in range(4):
                pltpu.make_async_remote_copy(src_ref=g_refs[a].at[q, 1 - mc], dst_ref=recv_refs[a].at[q],
                                             send_sem=send_sems.at[a], recv_sem=recv_sems.at[a],
                                             device_id=sibling, device_id_type=MESH).start()
        for a in range(na):
            pltpu.make_async_remote_copy(src_ref=recv_refs[a], dst_ref=recv_refs[a], send_sem=send_sems.at[a],
                                         recv_sem=recv_sems.at[a], device_id=sibling, device_id_type=MESH).wait()

    half = tuple(_sds((4,) + g.shape[2:], g.dtype) for g in grads)
    return _pcall(body, name=name, out_shape=half, in_specs=[HBM_SPEC] * na, out_specs=(HBM_SPEC,) * na,
                  scratch_shapes=[pltpu.SemaphoreType.DMA((na,)), pltpu.SemaphoreType.DMA((na,))])(*grads)


def _add_slabs(grads, recv, core, *, name):
    na = len(grads)

    def body(core_ref, *refs):
        for a in range(na):
            refs[2 * na + a][...] = (refs[a][...].astype(F32) + refs[na + a][...].astype(F32)).astype(BF16)

    own_specs = [pl.BlockSpec((None, None) + x.shape[2:], lambda q, core_ref: (q, core_ref[0], 0, 0)) for x in grads]
    specs = [pl.BlockSpec((None,) + x.shape[1:], lambda q, core_ref: (q, 0, 0)) for x in recv]
    blk = sum(_nbytes(x.shape[1:], F32) for x in recv)
    grid_spec = pltpu.PrefetchScalarGridSpec(num_scalar_prefetch=1, grid=(4,), in_specs=own_specs + specs,
                                             out_specs=tuple(specs))
    params = pltpu.CompilerParams(dimension_semantics=("parallel",), vmem_limit_bytes=_vmem_limit(2 * blk))
    return pl.pallas_call(body, name=name, out_shape=tuple(_sds(x.shape, BF16) for x in recv), grid_spec=grid_spec,
                          compiler_params=params)(core, *grads, *recv)


def _chip_exchange(parts, *, name):
    na = len(parts)

    def body(*refs):
        p_refs, out_refs = refs[:na], refs[na:2 * na]
        send_sems, recv_sems, local_sems = refs[2 * na:]
        mx, my, mc = lax.axis_index("x"), lax.axis_index("y"), lax.axis_index("c")
        mine_q = 2 * mx + my
        chips = [(1 - mx, my), (mx, 1 - my), (1 - mx, 1 - my)]
        owns = [pltpu.make_async_copy(p_refs[a].at[mine_q], out_refs[a].at[mine_q], local_sems.at[a]) for a in range(na)]
        for cp in owns:
            cp.start()
        sends = []
        for a in range(na):
            for k, chip in enumerate(chips):
                sends.append(pltpu.make_async_remote_copy(
                    src_ref=p_refs[a].at[2 * chip[0] + chip[1]], dst_ref=out_refs[a].at[mine_q],
                    send_sem=send_sems.at[k, a], recv_sem=recv_sems.at[k, a], device_id=(*chip, mc), device_id_type=MESH))
        for cp in sends:
            cp.start()
        for a in range(na):
            for k, chip in enumerate(chips):
                pltpu.make_async_remote_copy(
                    src_ref=p_refs[a].at[mine_q], dst_ref=out_refs[a].at[2 * chip[0] + chip[1]],
                    send_sem=send_sems.at[k, a], recv_sem=recv_sems.at[k, a], device_id=(*chip, mc),
                    device_id_type=MESH).wait_recv()
        for cp in sends:
            cp.wait_send()
        for cp in owns:
            cp.wait()

    return _pcall(body, name=name, out_shape=tuple(_sds(x.shape, x.dtype) for x in parts), in_specs=[HBM_SPEC] * na,
                  out_specs=(HBM_SPEC,) * na,
                  scratch_shapes=[pltpu.SemaphoreType.DMA((3, na)), pltpu.SemaphoreType.DMA((3, na)),
                                  pltpu.SemaphoreType.DMA((na,))])(*parts)


def _sum_chips(parts, *, name):
    na = len(parts)

    def body(*refs):
        for a in range(na):
            p_ref = refs[a]
            acc = p_ref[0].astype(F32)
            for k in range(1, p_ref.shape[0]):
                acc = acc + p_ref[k].astype(F32)
            refs[na + a][...] = acc

    half = lambda x: x.shape[1] // 2
    in_specs = [pl.BlockSpec((x.shape[0], half(x), x.shape[2]), lambda i: (0, i, 0)) for x in parts]
    out_specs = tuple(pl.BlockSpec((half(x), x.shape[2]), lambda i: (i, 0)) for x in parts)
    blk = sum(_nbytes((x.shape[0] + 2, half(x), x.shape[2]), BF16) for x in parts)
    return _pcall(body, name=name, out_shape=tuple(_sds(x.shape[1:], F32) for x in parts), grid=(2,),
                  in_specs=in_specs, out_specs=out_specs, semantics=("parallel",), block_bytes=blk)(*parts)


def _sum_devices(lands, own, me, *, name):
    na = len(lands)

    def body(me_ref, *refs):
        mine = me_ref[0]
        for a in range(na):
            l_ref, o_ref = refs[a], refs[na + a]
            acc = None
            for k in range(N_DEV):
                term = jnp.where(mine == k, o_ref[...], l_ref[k]).astype(F32)
                acc = term if acc is None else acc + term
            refs[2 * na + a][...] = acc

    half = lambda x: x.shape[1] // 2
    land_specs = [pl.BlockSpec((N_DEV, half(x), x.shape[2]), lambda i, me_ref: (0, i, 0)) for x in lands]
    own_specs = [pl.BlockSpec((None, half(x), x.shape[2]), lambda i, me_ref: (me_ref[0], i, 0)) for x in lands]
    out_specs = tuple(pl.BlockSpec((half(x), x.shape[2]), lambda i, me_ref: (i, 0)) for x in lands)
    blk = sum(_nbytes((N_DEV + 3, half(x), x.shape[2]), BF16) for x in lands)
    grid_spec = pltpu.PrefetchScalarGridSpec(num_scalar_prefetch=1, grid=(2,), in_specs=land_specs + own_specs,
                                             out_specs=out_specs)
    params = pltpu.CompilerParams(dimension_semantics=("parallel",), vmem_limit_bytes=_vmem_limit(blk))
    return pl.pallas_call(body, name=name, out_shape=tuple(_sds(x.shape[1:], F32) for x in lands), grid_spec=grid_spec,
                          compiler_params=params)(me, *lands, *own)


def _reduce_layer(grads, l):
    n = lambda s: f"l{l}_{s}"
    views = [g.reshape(4, 2, g.shape[0] // N_DEV, g.shape[1]) for g in grads]
    recv = _pair_swap(views, name=n("reduce_pair"))
    core = lax.axis_index("c").astype(jnp.int32).reshape(1)
    chip_sum = _add_slabs(views, recv, core, name=n("reduce_pair_add"))
    from_chips = _chip_exchange(chip_sum, name=n("reduce_chips"))
    return _sum_chips(from_chips, name=n("reduce_chips_add"))


def _adamw(w, g, m, v, *, name):
    lead, (r, c) = w.shape[:-2], w.shape[-2:]
    tr = _pick(r, (512, 352, 288, 256, 192, 128, 64, 32, 16, 8))
    c1 = 1.0 / (1.0 - ADAM_B1 ** ADAM_STEP)
    c2 = 1.0 / (1.0 - ADAM_B2 ** ADAM_STEP)

    def body(w_ref, g_ref, m_ref, v_ref, d_ref, nm_ref, nv_ref):
        gv = g_ref[...]
        nm = ADAM_B1 * m_ref[...] + (1.0 - ADAM_B1) * gv
        nv = ADAM_B2 * v_ref[...] + (1.0 - ADAM_B2) * jnp.square(gv)
        d_ref[...] = -ADAM_LR * ((nm * c1) / (jnp.sqrt(nv * c2) + ADAM_EPS) + ADAM_WD * w_ref[...])
        nm_ref[...] = nm
        nv_ref[...] = nv

    if lead:
        blk = pl.BlockSpec((None, tr, c), lambda k, i: (k, i, 0))
        grid, sem = (lead[0], r // tr), ("parallel", "parallel")
    else:
        blk = pl.BlockSpec((tr, c), lambda i: (i, 0))
        grid, sem = (r // tr,), ("parallel",)
    out = _sds(w.shape, F32)
    return _pcall(body, name=name, out_shape=(out, out, out), grid=grid, in_specs=[blk] * 4,
                  out_specs=(blk, blk, blk), semantics=sem, block_bytes=7 * _nbytes((tr, c), F32))(w, g, m, v)


def _pack_flat(arrs, rows, cols=1024):
    flat = jnp.concatenate([a.reshape(-1).astype(F32) for a in arrs])
    pad = rows * cols - flat.shape[0]
    return jnp.pad(flat, (0, pad)).reshape(rows, cols)


def _unpack_flat(buf, shapes):
    flat = buf.reshape(-1)
    out, off = [], 0
    for shp in shapes:
        n = 1
        for s in shp:
            n *= s
        out.append(flat[off:off + n].reshape(shp))
        off += n
    return out


def _flat_rows(shapes, cols=1024):
    n = sum(functools.reduce(lambda a, b: a * b, shp, 1) for shp in shapes)
    rows = -(-n // cols)
    return -(-rows // 64) * 64


def _block_diag(w):
    eye = jnp.eye(N_HEADS, dtype=w.dtype)
    return (w[:, :, :, None, :] * eye[None, :, None, :, None]).reshape(w.shape[0], W_GRP, W_GRP)


def _diag_blocks(w):
    w5 = w.reshape(w.shape[0], N_HEADS, HEAD_DIM, N_HEADS, HEAD_DIM)
    return jnp.stack([w5[:, h, :, h, :] for h in range(N_HEADS)], axis=1)


def _stacked_params(w, lbs):
    tril = jnp.tril(jnp.ones((GMLP_CHUNK, GMLP_CHUNK), bool))
    row = lambda a: a.reshape(DEPTH, 1, -1)
    return dict(
        g1=row(w['norm1_g']), g2=row(w['norm2_g']), g3=row(w['norm3_g']),
        a_ln_g=row(w['a_ln_g']), a_ln_b=row(w['a_ln_b']),
        a_wcat=jnp.where(tril, w['a_ws'], 0.0).reshape(DEPTH, N_HEADS * GMLP_CHUNK, GMLP_CHUNK),
        a_bfull=jnp.repeat(jnp.swapaxes(w['a_bs'], 1, 2), HEAD_DIM, axis=2),
        b_cw=w['b_conv_w_full'], b_cb=row(w['b_conv_b']), b_wa=_block_diag(w['b_wa']), b_ba=row(w['b_ba']),
        b_wx=_block_diag(w['b_wx']), b_bx=row(w['b_bx']), b_lam=row(w['b_lam']),
        c_lb=row(lbs), c_ngf=row(jnp.tile(w['c_norm_g'], (1, N_HEADS))),
        d_wd=_block_diag(w['d_w']), d_scale=row(w['d_scale']),
        f_cw=w['ffn_conv_w_full'], f_cb=row(w['ffn_conv_b']),
    )


B_PRM = ('b_cw', 'b_cb', 'b_wa', 'b_ba', 'b_wx', 'b_bx', 'b_lam')


def _layer_fwd(x, p_bf, wb, sp, l):
    n = lambda s: f"l{l}_{s}"
    h, (z,) = _rms_matmul(x, sp['g1'], [wb['w_in']], nt=True, name=n("proj_in"))
    mix = _gmlp_fwd(z, sp['a_ln_g'], sp['a_ln_b'], sp['a_wcat'], sp['a_bfull'], name=n("gmlp"))
    mix, h0s = _rglru_fwd(z, [sp[k] for k in B_PRM], mix, name=n("rglru"))
    mix, sts = _hgrn_fwd(z, sp['c_lb'], sp['c_ngf'], mix, name=n("hgrn"))
    mix = _pool_fwd(z, sp['d_wd'], sp['d_scale'], mix, name=n("pool"))
    x1 = _matmul(mix, wb['w_out'], res=x, name=n("proj_out"))
    h2, hg, hv, a = _up_ffn_fwd(x1, sp['g2'], wb['w_up_g'], wb['w_up_v'], sp['f_cw'], sp['f_cb'], name=n("up_ffn"))
    x2 = _matmul(a, wb['w_down'], res=x1, name=n("down"))
    h3, (gl, x3) = _rms_matmul(x2, sp['g3'], [wb['w_pg']], ple=(p_bf, wb['w_pe']), name=n("ple"))
    saved = dict(x=x, h=h, z=z, h0s=h0s, sts=sts, mix=mix, x1=x1, h2=h2, hg=hg, hv=hv, a=a, x2=x2, h3=h3, gl=gl)
    return x3, saved


def _layer_bwd(dx3, sv, p_bf, wb, sp, l, mid=None):
    n = lambda s: f"l{l}_{s}_bwd"
    gb, gs = {}, {}
    dx2, dx2b, gs['norm3_g'], dpe, dgl = _ple_rms_bwd(dx3, sv['gl'], p_bf, wb['w_pe'], wb['w_pg'], sv['x2'], sp['g3'],
                                                      name=n("ple"))
    gb['w_pe'] = _matmul_tn(dpe, p_bf, name=n("ple_emb_w"))
    gb['w_pg'] = _matmul_tn(sv['h3'], dgl, name=n("ple_gate_w"))
    gb['w_down'] = _matmul_tn(sv['a'], dx2b, name=n("down_w"))
    dhg, dhv, gs['f_dwg'], gs['f_dwv'] = _ffn_bwd(sv['hg'], sv['hv'], dx2b, wb['w_down'], sp['f_cw'], sp['f_cb'],
                                                  name=n("ffn_gate"))
    gate_rows = _matmul_tn(dhg, sv['h2'], name=n("up_gate_w"), out_rows=2 * D_FF)
    gb['w_up'] = _matmul_tn(dhv, sv['h2'], name=n("up_val_w"), out_rows=2 * D_FF, row_off=D_FF, into=gate_rows)
    if mid is not None:
        sp = mid(gb, sp)
    dh2 = _matmul(dhg, wb['w_up_g'], name=n("up_gate_x"))
    dx1, dx1b, gs['norm2_g'] = _matmul_rms_bwd(dhv, wb['w_up_v'], sv['x1'], sp['g2'], dx2, res=dh2, name=n("up_val_x"))
    dmix = _matmul(dx1b, wb['w_out'], nt=True, name=n("proj_out_x"))
    gb['w_out'] = _matmul_tn(sv['mix'], dx1b, name=n("proj_out_w"))
    z = sv['z']
    dz, gs['a_ln_g'], gs['a_ln_b'], gs['a_wcat'], gs['a_bfull'] = _gmlp_bwd(
        z, dmix, sp['a_ln_g'], sp['a_ln_b'], sp['a_wcat'], sp['a_bfull'], name=n("gmlp"))
    dz, *dbp = _rglru_bwd(z, dmix, sv['h0s'], [sp[k] for k in B_PRM], dz, name=n("rglru"))
    gs.update(zip(B_PRM, dbp))
    dz, gs['c_lb'], gs['c_ngf'] = _hgrn_bwd(z, dmix, sv['sts'], sp['c_lb'], sp['c_ngf'], dz, name=n("hgrn"))
    dz, gs['d_wd'], gs['d_scale'] = _pool_bwd(z, dmix, sp['d_wd'], sp['d_scale'], dz, name=n("pool"))
    gb['w_in'] = _matmul_tn(dz, sv['h'], name=n("proj_in_w"))
    dx0, _, gs['norm1_g'] = _matmul_rms_bwd(dz, wb['w_in'], sv['x'], sp['g1'], dx1, name=n("proj_in_x"))
    return dx0, gb, gs


SMALL_NAMES = [nm for nm in WEIGHT_NAMES if nm not in BIG_NAMES]
COL_SHARDED = ('w_in', 'w_up', 'w_pe')


def _comm_shards(w):
    return [(jnp.swapaxes(w[nm], 1, 2) if nm in COL_SHARDED else w[nm]).astype(BF16) for nm, _, _ in BIG_COMM]


def _full_weights(gathered):
    out = {nm: g.reshape(N_DEV * r, c) for g, (nm, r, c) in zip(gathered, BIG_COMM)}
    halves = out.pop('w_up').reshape(2, D_FF, D_MODEL)
    out['w_up_g'], out['w_up_v'] = _Sel(halves, 0), _Sel(halves, 1)
    return out


def _small_grads(raw):
    nl = len(raw)
    st = {k: jnp.stack([r[k] for r in raw]) for k in raw[0]}
    tril = jnp.tril(jnp.ones((GMLP_CHUNK, GMLP_CHUNK), bool))
    vec = lambda a: a.reshape(nl, -1)
    out = {nm: vec(st[k]) for nm, k in (('norm1_g', 'norm1_g'), ('norm2_g', 'norm2_g'), ('norm3_g', 'norm3_g'),
                                        ('a_ln_g', 'a_ln_g'), ('a_ln_b', 'a_ln_b'), ('b_conv_b', 'b_cb'),
                                        ('b_ba', 'b_ba'), ('b_bx', 'b_bx'), ('b_lam', 'b_lam'), ('c_lb', 'c_lb'),
                                        ('d_scale', 'd_scale'))}
    out['a_ws'] = jnp.where(tril, st['a_wcat'].reshape(nl, N_HEADS, GMLP_CHUNK, GMLP_CHUNK), 0.0)
    out['a_bs'] = jnp.swapaxes(st['a_bfull'].reshape(nl, GMLP_CHUNK, N_HEADS, HEAD_DIM).sum(-1), 1, 2)
    out['b_conv_w'] = st['b_cw']
    out['b_wa'], out['b_wx'], out['d_w'] = _diag_blocks(st['b_wa']), _diag_blocks(st['b_wx']), _diag_blocks(st['d_wd'])
    out['c_norm_g'] = st['c_ngf'].reshape(nl, N_HEADS, HEAD_DIM).sum(1)
    out['ffn_conv_w'] = jnp.concatenate([st['f_dwg'][:, 0:3], st['f_dwv'][:, 0:3]], axis=2)
    out['ffn_conv_b'] = jnp.concatenate([st['f_dwg'][:, 3], st['f_dwv'][:, 3]], axis=1)
    return out


def _step(w, m, v, x, p, target):
    s = x.shape[1]
    dev = 4 * lax.axis_index("x") + 2 * lax.axis_index("y") + lax.axis_index("c")
    xs = x.reshape(s, D_MODEL)

    shards = _comm_shards(w)
    conv_shapes = [w['b_conv_w'].shape, w['ffn_conv_w'].shape]
    conv_rows = _flat_rows(conv_shapes)
    conv_all = _all_gather(_pack_flat([w['b_conv_w'], w['ffn_conv_w']], conv_rows), name="gather_conv_weights")
    parts = [_unpack_flat(conv_all[d], conv_shapes) for d in range(N_DEV)]
    wf = dict(w)
    wf['b_conv_w_full'] = jnp.concatenate([pt[0] for pt in parts], axis=-1)
    wf['ffn_conv_w_full'] = jnp.concatenate([pt[1] for pt in parts], axis=-1)
    lbs = _lbs_fwd(w['c_lb'], name="hgrn_bounds")

    stacked = _stacked_params(wf, lbs)
    p_all = p.reshape(DEPTH, s, PLE_DIM).astype(BF16)
    xl, saved, wbs, sps = xs, [], [], []
    gathered = _gather_layer(shards, 0, name="l0_gather_weights")
    for l in range(DEPTH):
        sp = {k: _Sel(a, l) for k, a in stacked.items()}
        if l + 1 < DEPTH:
            own = [x[l + 1] for x in shards]
            after = [conv_all, *gathered] if l == 0 else [xl]
            lands = _place_own(own, after, name=f"l{l + 1}_gather_place")
            started = _exchange_start(own, lands, name=f"l{l + 1}_gather_start")
            sp['g1'] = stacked['g1'][l] + started[-1][0, 0]
        wb = _full_weights(gathered)
        p_bf = p_all[l]
        xl, sv = _layer_fwd(xl, p_bf, wb, sp, l)
        if l + 1 < DEPTH:
            gathered = _exchange_wait(started, xl, name=f"l{l + 1}_gather_wait")[1]
        saved.append((sv, p_bf))
        wbs.append(wb)
        sps.append(sp)
    loss_part, dx, dfinal = _loss_head(xl, w['final_g'].reshape(1, D_MODEL), target.reshape(s, D_MODEL), name="loss_head")
    loss = lax.psum(loss_part[0, 0], ("x", "y", "c"))

    dev1 = dev.astype(jnp.int32).reshape(1)
    names = [nm for nm, _, _ in BIG_COMM]

    def start_reduce(grads, name):
        views = [g.reshape(N_DEV, g.shape[0] // N_DEV, g.shape[1]) for g in grads]
        return _exchange_start(views, [lax.empty(g.shape, g.dtype) for g in views], name=name, per_peer=True)

    def finish_reduce(started, after, lname):
        own, lands = _exchange_wait(started, after, name=f"{lname}_reduce_wait")
        return _sum_devices(lands, own, dev1, name=f"{lname}_reduce_sum")

    reduced, small = [None] * DEPTH, [None] * DEPTH
    pending = None
    for l in range(DEPTH - 1, 0, -1):
        sv, p_bf = saved[l]
        sp = sps[l]
        if pending is not None:
            sp = dict(sp, g3=stacked['g3'][l] + pending[-1][0, 0])
        dx, gb, small[l] = _layer_bwd(dx, sv, p_bf, wbs[l], sp, l)
        if pending is not None:
            reduced[l + 1] = finish_reduce(pending, dx, f"l{l + 1}")
        pending = start_reduce([gb[nm] for nm in names], f"l{l}_reduce_start")
    early = ('w_up', 'w_down', 'w_pe', 'w_pg')
    mid_started = []

    def mid(gb, sp):
        mid_started.append(start_reduce([gb[nm] for nm in early], "l0_reduce_start"))
        return dict(sp, g2=stacked['g2'][0] + mid_started[0][-1][0, 0])

    upper_names = [nm for nm in SMALL_NAMES if nm != 'final_g']
    low_names = upper_names + ['final_g']
    upper = _small_grads(small[1:])
    upper_shapes = [upper[nm].shape for nm in upper_names]
    upper_packed = [_pack_flat([upper[nm] for nm in upper_names], _flat_rows(upper_shapes))]
    upper_started = _exchange_start(upper_packed, _place_own(upper_packed, [], name="upper_small_grads_place"),
                                    name="upper_small_grads_start")

    sv, p_bf = saved[0]
    g3 = stacked['g3'][0] + pending[-1][0, 0] + upper_started[-1][0, 0]
    dx, gb, small[0] = _layer_bwd(dx, sv, p_bf, wbs[0], dict(sps[0], g3=g3), 0, mid=mid)
    reduced[1] = finish_reduce(pending, dx, "l1")
    late = dict(zip(('w_in', 'w_out'), _reduce_layer([gb['w_in'], gb['w_out']], 0)))
    late.update(zip(early, finish_reduce(mid_started[0], late['w_in'], "l0")))
    reduced[0] = [late[nm] for nm in names]
    grad_x = dx.reshape(1, s, D_MODEL)
    low = _small_grads(small[:1])
    low['final_g'] = dfinal.reshape(D_MODEL)
    low_shapes = [low[nm].shape for nm in low_names]
    low_all = _all_gather(_pack_flat([low[nm] for nm in low_names], _flat_rows(low_shapes)), name="gather_small_grads")
    low_sum = dict(zip(low_names, _unpack_flat(_sum_slots(low_all, name="sum_small_grads"), low_shapes)))
    upper_all = _exchange_wait(upper_started, low_all, name="upper_small_grads_wait")[1][0]
    upper_sum = dict(zip(upper_names, _unpack_flat(_sum_slots(upper_all, name="sum_upper_small_grads"), upper_shapes)))
    gsmall = {nm: jnp.concatenate([low_sum[nm], upper_sum[nm]], axis=0) for nm in upper_names}
    gsmall['c_lb'] = _lbs_bwd(w['c_lb'], gsmall['c_lb'], name="hgrn_bounds_bwd")
    gsmall['final_g'] = low_sum['final_g']
    for nm in ('b_conv_w', 'ffn_conv_w'):
        width = w[nm].shape[-1]
        gsmall[nm] = lax.dynamic_slice_in_dim(gsmall[nm], dev * width, width, axis=2)

    grads, delta, new_m, new_v = {}, {}, {}, {}
    for a, (nm, _, _) in enumerate(BIG_COMM):
        t = (lambda x: jnp.swapaxes(x, 1, 2)) if nm in COL_SHARDED else (lambda x: x)
        g = jnp.stack([reduced[l][a] for l in range(DEPTH)])
        d, nm_, nv_ = _adamw(t(w[nm]), g, t(m[nm]), t(v[nm]), name=f"adamw_{nm}")
        grads[nm], delta[nm], new_m[nm], new_v[nm] = t(g), t(d), t(nm_), t(nv_)

    shapes = [w[nm].shape for nm in SMALL_NAMES]
    rows = _flat_rows(shapes)
    pk = lambda t: _pack_flat([t[nm] for nm in SMALL_NAMES], rows)
    d, nm_, nv_ = _adamw(pk(w), pk(gsmall), pk(m), pk(v), name="adamw_small")
    for nm, dd, mm_, vv_ in zip(SMALL_NAMES, _unpack_flat(d, shapes), _unpack_flat(nm_, shapes), _unpack_flat(nv_, shapes)):
        grads[nm], delta[nm], new_m[nm], new_v[nm] = gsmall[nm], dd, mm_, vv_

    return (loss, grad_x, *[grads[nm] for nm in WEIGHT_NAMES], *[delta[nm] for nm in WEIGHT_NAMES],
            *[new_m[nm] for nm in WEIGHT_NAMES], *[new_v[nm] for nm in WEIGHT_NAMES])


def kernel(x, p, norm1_g, w_in, a_ln_g, a_ln_b, a_ws, a_bs, b_conv_w, b_conv_b, b_wa, b_ba, b_wx, b_bx, b_lam, c_lb, c_norm_g, d_w, d_scale, w_out, norm2_g, w_up, ffn_conv_w, ffn_conv_b, w_down, norm3_g, w_pe, w_pg, final_g, loss_target, m_norm1_g, m_w_in, m_a_ln_g, m_a_ln_b, m_a_ws, m_a_bs, m_b_conv_w, m_b_conv_b, m_b_wa, m_b_ba, m_b_wx, m_b_bx, m_b_lam, m_c_lb, m_c_norm_g, m_d_w, m_d_scale, m_w_out, m_norm2_g, m_w_up, m_ffn_conv_w, m_ffn_conv_b, m_w_down, m_norm3_g, m_w_pe, m_w_pg, m_final_g, v_norm1_g, v_w_in, v_a_ln_g, v_a_ln_b, v_a_ws, v_a_bs, v_b_conv_w, v_b_conv_b, v_b_wa, v_b_ba, v_b_wx, v_b_bx, v_b_lam, v_c_lb, v_c_norm_g, v_d_w, v_d_scale, v_w_out, v_norm2_g, v_w_up, v_ffn_conv_w, v_ffn_conv_b, v_w_down, v_norm3_g, v_w_pe, v_w_pg, v_final_g):
    w = dict(norm1_g=norm1_g, w_in=w_in, a_ln_g=a_ln_g, a_ln_b=a_ln_b, a_ws=a_ws, a_bs=a_bs, b_conv_w=b_conv_w, b_conv_b=b_conv_b, b_wa=b_wa, b_ba=b_ba, b_wx=b_wx, b_bx=b_bx, b_lam=b_lam, c_lb=c_lb, c_norm_g=c_norm_g, d_w=d_w, d_scale=d_scale, w_out=w_out, norm2_g=norm2_g, w_up=w_up, ffn_conv_w=ffn_conv_w, ffn_conv_b=ffn_conv_b, w_down=w_down, norm3_g=norm3_g, w_pe=w_pe, w_pg=w_pg, final_g=final_g)
    m = dict(norm1_g=m_norm1_g, w_in=m_w_in, a_ln_g=m_a_ln_g, a_ln_b=m_a_ln_b, a_ws=m_a_ws, a_bs=m_a_bs, b_conv_w=m_b_conv_w, b_conv_b=m_b_conv_b, b_wa=m_b_wa, b_ba=m_b_ba, b_wx=m_b_wx, b_bx=m_b_bx, b_lam=m_b_lam, c_lb=m_c_lb, c_norm_g=m_c_norm_g, d_w=m_d_w, d_scale=m_d_scale, w_out=m_w_out, norm2_g=m_norm2_g, w_up=m_w_up, ffn_conv_w=m_ffn_conv_w, ffn_conv_b=m_ffn_conv_b, w_down=m_w_down, norm3_g=m_norm3_g, w_pe=m_w_pe, w_pg=m_w_pg, final_g=m_final_g)
    v = dict(norm1_g=v_norm1_g, w_in=v_w_in, a_ln_g=v_a_ln_g, a_ln_b=v_a_ln_b, a_ws=v_a_ws, a_bs=v_a_bs, b_conv_w=v_b_conv_w, b_conv_b=v_b_conv_b, b_wa=v_b_wa, b_ba=v_b_ba, b_wx=v_b_wx, b_bx=v_b_bx, b_lam=v_b_lam, c_lb=v_c_lb, c_norm_g=v_c_norm_g, d_w=v_d_w, d_scale=v_d_scale, w_out=v_w_out, norm2_g=v_norm2_g, w_up=v_w_up, ffn_conv_w=v_ffn_conv_w, ffn_conv_b=v_ffn_conv_b, w_down=v_w_down, norm3_g=v_norm3_g, w_pe=v_w_pe, w_pg=v_w_pg, final_g=v_final_g)
    return _step(w, m, v, x, p, loss_target)
```

```python
import functools

import jax
import jax.numpy as jnp
from jax import lax
from jax.experimental import pallas as pl
from jax.experimental.pallas import tpu as pltpu

F32 = jnp.float32
BF16 = jnp.bfloat16
MESH = pl.DeviceIdType.MESH

D_MODEL = 1024
DEPTH = 4
PLE_DIM = 256
W_GRP = 256
N_HEADS = 4
HEAD_DIM = 64
GMLP_CHUNK = 128
RGLRU_C = 8.0
HGRN_CHUNK = 64
HGRN_SUB = 32
HGRN_STEP_CHUNKS = 8
POOL_WINDOWS = (2, 4, 8, 16)
D_FF = 2816
D_PROJ = 2304
EPS = 1e-6
ADAM_LR = 0.001
ADAM_B1 = 0.9
ADAM_B2 = 0.999
ADAM_EPS = 1e-08
ADAM_WD = 0.01
ADAM_STEP = 10

N_DEV = 8
MIB = 2 ** 20
V7X_VMEM_BYTES = 64 * MIB
HGRN_EXP_CLAMP = 60.0

WEIGHT_NAMES = ['norm1_g', 'w_in', 'a_ln_g', 'a_ln_b', 'a_ws', 'a_bs', 'b_conv_w', 'b_conv_b', 'b_wa', 'b_ba', 'b_wx',
                'b_bx', 'b_lam', 'c_lb', 'c_norm_g', 'd_w', 'd_scale', 'w_out', 'norm2_g', 'w_up', 'ffn_conv_w',
                'ffn_conv_b', 'w_down', 'norm3_g', 'w_pe', 'w_pg', 'final_g']
BIG_NAMES = ('w_in', 'w_out', 'w_up', 'w_down', 'w_pe', 'w_pg')


def _vmem_limit(block_bytes):
    want = 2 * block_bytes + 24 * MIB
    return int(min(max(want, 32 * MIB), V7X_VMEM_BYTES - 8 * MIB))


def _in_hbm(x):
    return pltpu.with_memory_space_constraint(x, pltpu.HBM)


def _out_hbm(s):
    return pltpu.HBM(s.shape, s.dtype)


def _pcall(body, *, name, out_shape, grid=None, in_specs=None, out_specs=None, scratch_shapes=(),
           semantics=None, block_bytes=0, aliases=None, pin=True):
    kw = {} if aliases is None else {"input_output_aliases": aliases}
    if pin:
        out_shape = tuple(_out_hbm(s) for s in out_shape) if isinstance(out_shape, (tuple, list)) else _out_hbm(out_shape)
    if grid is not None:
        kw["grid"] = grid
    if in_specs is not None:
        kw["in_specs"] = in_specs
    if out_specs is not None:
        kw["out_specs"] = out_specs
    params = pltpu.CompilerParams(dimension_semantics=semantics, vmem_limit_bytes=_vmem_limit(block_bytes))
    call = pl.pallas_call(body, name=name, out_shape=out_shape, scratch_shapes=list(scratch_shapes),
                          compiler_params=params, **kw)
    return (lambda *args: call(*[_in_hbm(a) for a in args])) if pin else call


def _pick(n, cands):
    for c in cands:
        if n % c == 0:
            return c
    return n


def _nbytes(shape, dtype):
    n = 1
    for s in shape:
        n *= s
    return n * jnp.dtype(dtype).itemsize


def _sds(shape, dtype):
    return jax.ShapeDtypeStruct(tuple(shape), dtype)


class _Sel:
    def __init__(self, arr, *idx):
        self.arr, self.idx = arr, tuple(idx)
        self.shape = arr.shape[len(idx):]
        self.ndim = len(self.shape)
        self.dtype = arr.dtype


def _arr(a):
    return a.arr if isinstance(a, _Sel) else a


def _spec(a, block=None, index=None):
    block = tuple(a.shape) if block is None else tuple(block)
    index = (lambda *g: (0,) * len(block)) if index is None else index
    if isinstance(a, _Sel):
        lead = a.idx
        return pl.BlockSpec((None,) * len(lead) + block, lambda *g: lead + tuple(index(*g)))
    return pl.BlockSpec(block, lambda *g: tuple(index(*g)))


def _ospec(a):
    return pl.BlockSpec(tuple(a.shape), lambda *g: (0,) * a.ndim)


def _rows_of(shape):
    return lax.broadcasted_iota(jnp.int32, shape, 0)


def _lanes_of(shape):
    return lax.broadcasted_iota(jnp.int32, shape, 1)


def _sdn(x, k, fill):
    n = x.shape[0]
    return jnp.where(_rows_of(x.shape) >= k, pltpu.roll(x, k % n, 0), fill)


def _sup(x, k, fill):
    n = x.shape[0]
    return jnp.where(_rows_of(x.shape) < n - k, pltpu.roll(x, (n - k) % n, 0), fill)


@functools.partial(jax.custom_vjp, nondiff_argnums=(1,))
def _shift_dn(x, k):
    return pltpu.roll(x, k, 0)


def _shift_dn_fwd(x, k):
    return pltpu.roll(x, k, 0), None


def _shift_dn_bwd(k, _, g):
    return (pltpu.roll(g, g.shape[0] - k, 0),)


_shift_dn.defvjp(_shift_dn_fwd, _shift_dn_bwd)


SUBLANES = 8


def _lin_scan_impl(a, b, h0):
    n = a.shape[0]
    pos = _rows_of(a.shape) % SUBLANES
    aa, bb = a, b
    k = 1
    while k < SUBLANES:
        keep = pos >= k
        bb = bb + jnp.where(keep, aa * pltpu.roll(bb, k, 0), 0.0)
        aa = aa * jnp.where(keep, pltpu.roll(aa, k, 0), 1.0)
        k *= 2
    out, carry = [], h0
    for r in range(n // SUBLANES):
        rows = slice(r * SUBLANES, (r + 1) * SUBLANES)
        hr = bb[rows] + aa[rows] * carry
        out.append(hr)
        carry = hr[SUBLANES - 1:]
    return jnp.concatenate(out, axis=0)


@jax.custom_vjp
def _lin_scan(a, b, h0):
    return _lin_scan_impl(a, b, h0)


def _lin_scan_fwd(a, b, h0):
    h = _lin_scan_impl(a, b, h0)
    return h, (a, h, h0)


def _lin_scan_bwd(res, g):
    a, h, h0 = res
    n = a.shape[0]
    pos = _rows_of(a.shape) % SUBLANES
    cc, gg = _sup(a, 1, 0.0), g
    k = 1
    while k < SUBLANES:
        keep = pos < SUBLANES - k
        gg = gg + jnp.where(keep, cc * pltpu.roll(gg, n - k, 0), 0.0)
        cc = cc * jnp.where(keep, pltpu.roll(cc, n - k, 0), 1.0)
        k *= 2
    out, carry = [], jnp.zeros_like(h0)
    for r in range(n // SUBLANES - 1, -1, -1):
        rows = slice(r * SUBLANES, (r + 1) * SUBLANES)
        gr = gg[rows] + cc[rows] * carry
        out.append(gr)
        carry = gr[:1]
    gg = jnp.concatenate(out[::-1], axis=0)
    first = _rows_of(a.shape) == 0
    hprev = jnp.where(first, h0, _sdn(h, 1, 0.0))
    dh0 = jnp.sum(jnp.where(first, a * gg, 0.0), axis=0, keepdims=True)
    return gg * hprev, gg, dh0


_lin_scan.defvjp(_lin_scan_fwd, _lin_scan_bwd)


def _cumsum_sub_impl(x):
    pos = _rows_of(x.shape) % HGRN_SUB
    k = 1
    while k < HGRN_SUB:
        x = x + jnp.where(pos >= k, pltpu.roll(x, k, 0), 0.0)
        k *= 2
    return x


@jax.custom_vjp
def _cumsum_sub(x):
    return _cumsum_sub_impl(x)


def _cumsum_sub_fwd(x):
    return _cumsum_sub_impl(x), None


def _cumsum_sub_bwd(_, g):
    n = g.shape[0]
    pos = _rows_of(g.shape) % HGRN_SUB
    k = 1
    while k < HGRN_SUB:
        g = g + jnp.where(pos < HGRN_SUB - k, pltpu.roll(g, n - k, 0), 0.0)
        k *= 2
    return (g,)


_cumsum_sub.defvjp(_cumsum_sub_fwd, _cumsum_sub_bwd)


def _dot(a, b, ca, cb):
    return lax.dot_general(a.astype(BF16), b.astype(BF16), (((ca,), (cb,)), ((), ())), preferred_element_type=F32)


@jax.custom_vjp
def _mm(a, b):
    return _dot(a, b, 1, 0)


def _mm_fwd(a, b):
    return _dot(a, b, 1, 0), (a, b)


def _mm_bwd(res, g):
    a, b = res
    return _dot(g, b, 1, 1), _dot(a, g, 0, 0)


_mm.defvjp(_mm_fwd, _mm_bwd)


@jax.custom_vjp
def _mm_nt(a, b):
    return _dot(a, b, 1, 1)


def _mm_nt_fwd(a, b):
    return _dot(a, b, 1, 1), (a, b)


def _mm_nt_bwd(res, g):
    a, b = res
    return _dot(g, b, 1, 0), _dot(g, a, 0, 0)


_mm_nt.defvjp(_mm_nt_fwd, _mm_nt_bwd)


@jax.custom_vjp
def _mm_tn(a, b):
    return _dot(a, b, 0, 0)


def _mm_tn_fwd(a, b):
    return _dot(a, b, 0, 0), (a, b)


def _mm_tn_bwd(res, g):
    a, b = res
    return _dot(b, g, 1, 1), _dot(a, g, 1, 0)


_mm_tn.defvjp(_mm_tn_fwd, _mm_tn_bwd)


def _head_mask(shape, h):
    return (_lanes_of(shape) // HEAD_DIM) == h


def _stack_heads(x):
    return jnp.concatenate([jnp.where(_head_mask(x.shape, h), x, 0.0) for h in range(N_HEADS)], axis=0)


def _unstack_heads(p):
    r = p.shape[0] // N_HEADS
    out = None
    for h in range(N_HEADS):
        blk = p[h * r:(h + 1) * r]
        term = jnp.where(_head_mask(blk.shape, h), blk, 0.0)
        out = term if out is None else out + term
    return out


def _segmean_impl(x):
    n = x.shape[1]
    same = (lax.broadcasted_iota(jnp.int32, (n, n), 0) // HEAD_DIM) == (lax.broadcasted_iota(jnp.int32, (n, n), 1) // HEAD_DIM)
    m = jnp.where(same, 1.0 / HEAD_DIM, 0.0).astype(BF16)
    hi = x.astype(BF16)
    lo = (x - hi.astype(F32)).astype(BF16)
    dn = (((1,), (0,)), ((), ()))
    return (lax.dot_general(hi, m, dn, preferred_element_type=F32)
            + lax.dot_general(lo, m, dn, preferred_element_type=F32))


@jax.custom_vjp
def _segmean(x):
    return _segmean_impl(x)


def _segmean_fwd(x):
    return _segmean_impl(x), None


def _segmean_bwd(_, g):
    return (_segmean_impl(g),)


_segmean.defvjp(_segmean_fwd, _segmean_bwd)


GELU_C = 0.7978845608028654
GELU_A = 0.044715


@jax.custom_vjp
def _gelu(x):
    return 0.5 * x * (1.0 + jnp.tanh(GELU_C * x * (1.0 + GELU_A * (x * x))))


def _gelu_fwd(x):
    x2 = x * x
    t = jnp.tanh(GELU_C * x * (1.0 + GELU_A * x2))
    return 0.5 * x * (1.0 + t), (x, x2, t)


def _gelu_bwd(res, g):
    x, x2, t = res
    half = 0.5 * (1.0 + t)
    return (g * (half + (0.5 * GELU_C) * x * (1.0 - t * t) * (1.0 + (3.0 * GELU_A) * x2)),)


_gelu.defvjp(_gelu_fwd, _gelu_bwd)


def _log1p(u):
    w = 1.0 + u
    return jnp.where(w == 1.0, u, jnp.log(w) * (u / (w - 1.0)))


def _softplus(y):
    return jnp.maximum(y, 0.0) + _log1p(jnp.exp(-jnp.abs(y)))


def _rms(x, g):
    return x * lax.rsqrt(jnp.mean(x * x, axis=-1, keepdims=True) + EPS) * g


def _gmlp_chunk(zu, zv, ln_g, ln_b, wcat, bfull):
    u = _gelu(zu)
    v = _gelu(zv)
    mu = jnp.mean(v, axis=-1, keepdims=True)
    var = jnp.mean(jnp.square(v - mu), axis=-1, keepdims=True)
    vn = (v - mu) * lax.rsqrt(var + EPS) * ln_g + ln_b
    sv = _unstack_heads(_mm(wcat, vn)) + bfull
    return u * sv


def _rglru_tile(xb_ext, gb, h0, cw, cb, wa, ba, wx, bx, lam):
    xc = (cb + cw[0:1] * _shift_dn(xb_ext, 3) + cw[1:2] * _shift_dn(xb_ext, 2) + cw[2:3] * _shift_dn(xb_ext, 1)
          + cw[3:4] * xb_ext)[8:]
    r = jax.nn.sigmoid(_mm(xc, wa) + ba)
    i = jax.nn.sigmoid(_mm(xc, wx) + bx)
    log_a = (-RGLRU_C) * r * _softplus(-lam)
    a = jnp.exp(log_a)
    mult = jnp.sqrt(-jnp.tanh(log_a) * (a * a + 1.0))
    h = _lin_scan(a, mult * (i * xc), h0)
    y = h * _gelu(gb)
    h_last = jnp.sum(jnp.where(_rows_of(h.shape) == h.shape[0] - 1, h, 0.0), axis=0, keepdims=True)
    return y, h_last


def _pool_tile(xd_ext, inv, wd, scale):
    s1 = xd_ext + _shift_dn(xd_ext, 1)
    s2 = s1 + _shift_dn(s1, 2)
    s3 = s2 + _shift_dn(s2, 4)
    s4 = s3 + _shift_dn(s3, 8)
    grp = _lanes_of(xd_ext.shape) // HEAD_DIM
    win = jnp.where(grp == 0, s1, jnp.where(grp == 1, s2, jnp.where(grp == 2, s3, s4)))
    pooled = win[16:] * inv - xd_ext[16:]
    return _mm(pooled, wd) * scale


def _hgrn_chunk(q, f, i, g, st, lb, ngf):
    n = q.shape[0]
    nsub = n // HGRN_SUB
    qs = jax.nn.silu(q)
    fg = lb + (1.0 - lb) * jax.nn.sigmoid(f)
    lf = jnp.log(fg)
    k = 1.0 - fg
    bl = _cumsum_sub(lf)
    row = _rows_of(q.shape)
    blk = row // HGRN_SUB
    betas = [jnp.zeros_like(lb)]
    for s in range(nsub):
        tot = jnp.sum(jnp.where(row == s * HGRN_SUB + HGRN_SUB - 1, bl, 0.0), axis=0, keepdims=True)
        betas.append(betas[-1] + tot)
    b_end = betas[nsub]
    beta_full = jnp.zeros_like(q)
    for s in range(1, nsub):
        beta_full = jnp.where(blk == s, betas[s], beta_full)
    qh = qs * jnp.exp(bl)
    qt = qh * jnp.exp(beta_full)
    b_all = beta_full + bl
    kt = k * jnp.exp(b_end - b_all)
    outs = []
    for s in range(nsub):
        kh = k * jnp.exp(jnp.minimum(betas[s] - b_all, HGRN_EXP_CLAMP))
        qstk = _stack_heads(qh[s * HGRN_SUB:(s + 1) * HGRN_SUB])
        att = _mm_nt(qstk, kh)
        ar = _rows_of(att.shape) % HGRN_SUB + s * HGRN_SUB
        att = jnp.where(_lanes_of(att.shape) <= ar, att, 0.0)
        outs.append(_unstack_heads(_mm(att, i)))
    o = jnp.concatenate(outs, axis=0) + _mm_nt(qt, st)
    same = (_rows_of(st.shape) // HEAD_DIM) == (_lanes_of(st.shape) // HEAD_DIM)
    st_new = st * jnp.exp(b_end) + jnp.where(same, _mm_tn(i, kt), 0.0)
    on = o * lax.rsqrt(_segmean(o * o) + EPS) * ngf
    return on * jax.nn.silu(g), st_new


def _ffn_tile(eg, ev, wg, bg, wv, bv):
    gt = (bg + wg[0:1] * _shift_dn(eg, 2) + wg[1:2] * _shift_dn(eg, 1) + wg[2:3] * eg)[8:]
    val = (bv + wv[0:1] * _shift_dn(ev, 2) + wv[1:2] * _shift_dn(ev, 1) + wv[2:3] * ev)[8:]
    return _gelu(gt) * val


MXU_WIDTH = 256
MATMUL_BLOCK_BUDGET = 18 * MIB


def _matmul_tiles(m, k, n, a_dtype, b_dtype, out_dtype, has_res):
    best = None
    for tm in (2048, 1024, 512, 256):
        if m % tm:
            continue
        for tn in (1024, 768, 1408, 512, 256, 128):
            if n % tn:
                continue
            blk = (_nbytes((tm, k), a_dtype) + _nbytes((k, tn), b_dtype) + _nbytes((tm, tn), out_dtype)
                   + (_nbytes((tm, tn), F32) if has_res else 0))
            if blk > MATMUL_BLOCK_BUDGET:
                continue
            waste = -(-tn // MXU_WIDTH) * MXU_WIDTH / tn
            cost = (m // tm) * (n // tn) + 64 * (waste - 1.0) + 1e-3 * (n // tn) + blk / 2 ** 40
            if best is None or cost < best[0]:
                best = (cost, tm, tn, blk)
    assert best is not None, (m, k, n)
    return best[1:]


def _matmul(a, b, *, name, nt=False, res=None, out_dtype=F32):
    m, k = a.shape
    n = b.shape[0] if nt else b.shape[1]
    tm, tn, blk = _matmul_tiles(m, k, n, a.dtype, b.dtype, out_dtype, res is not None)
    dims = (((1,), (1,)), ((), ())) if nt else (((1,), (0,)), ((), ()))

    def body(*refs):
        if res is None:
            a_ref, b_ref, o_ref = refs
        else:
            a_ref, b_ref, r_ref, o_ref = refs
        acc = lax.dot_general(a_ref[...], b_ref[...], dims, preferred_element_type=F32)
        if res is not None:
            acc = acc + r_ref[...]
        o_ref[...] = acc.astype(out_dtype)

    in_specs = [pl.BlockSpec((tm, k), lambda i, j: (i, 0)),
                _spec(b, (tn, k), lambda i, j: (j, 0)) if nt else _spec(b, (k, tn), lambda i, j: (0, j))]
    args = [a, _arr(b)]
    if res is not None:
        in_specs.append(pl.BlockSpec((tm, tn), lambda i, j: (i, j)))
        args.append(res)
    return _pcall(body, name=name, out_shape=_sds((m, n), out_dtype), grid=(m // tm, n // tn), in_specs=in_specs,
                  out_specs=pl.BlockSpec((tm, tn), lambda i, j: (i, j)), semantics=("parallel", "parallel"),
                  block_bytes=blk + _nbytes((tm, tn), F32))(*args)


def _matmul_rms_bwd(a, b, x, g, dres, *, name, nt=False, res=None):
    m, k = a.shape
    n = b.shape[0] if nt else b.shape[1]
    tm = _pick(m, (512, 256))
    dims = (((1,), (1,)), ((), ())) if nt else (((1,), (0,)), ((), ()))

    def body(*refs):
        a_ref, b_ref, x_ref, g_ref, dr_ref = refs[:5]
        dx_ref, dxb_ref, dg_ref = refs[-3:]
        dh = lax.dot_general(a_ref[...], b_ref[...], dims, preferred_element_type=F32)
        if res is not None:
            dh = dh + refs[5][...]
        _, vjp = jax.vjp(_rms, x_ref[...], g_ref[...])
        dxn, dg = vjp(dh)
        dx = dr_ref[...] + dxn
        dx_ref[...] = dx
        dxb_ref[...] = dx.astype(BF16)
        _acc_out(dg_ref, dg, pl.program_id(0) == 0)

    row = pl.BlockSpec((tm, n), lambda i: (i, 0))
    vec = pl.BlockSpec((1, n), lambda i: (0, 0))
    in_specs = [pl.BlockSpec((tm, k), lambda i: (i, 0)),
                _spec(b, (n, k), lambda i: (0, 0)) if nt else _spec(b, (k, n), lambda i: (0, 0)), row, _spec(g), row]
    args = [a, _arr(b), x, _arr(g), dres]
    if res is not None:
        in_specs.append(row)
        args.append(res)
    blk = _nbytes((tm, k), a.dtype) + _nbytes((k, n), b.dtype) + 6 * _nbytes((tm, n), F32)
    return _pcall(body, name=name, out_shape=(_sds((m, n), F32), _sds((m, n), BF16), _sds((1, n), F32)), grid=(m // tm,),
                  in_specs=in_specs, out_specs=(row, row, vec), semantics=("arbitrary",), block_bytes=blk)(*args)


def _ple_rms_bwd(dx3, gl, p, w_pe, w_pg, x, g, *, name):
    m, n = dx3.shape
    tm = _pick(m, (512, 256))

    def body(d3_ref, gl_ref, p_ref, wpe_ref, w_ref, x_ref, g_ref, dx_ref, dxb_ref, dg_ref, dpe_ref, dgl_ref):
        gate = jax.nn.sigmoid(gl_ref[...])
        d3 = d3_ref[...]
        pe = lax.dot_general(p_ref[...], wpe_ref[...], (((1,), (1,)), ((), ())), preferred_element_type=F32)
        dpe_ref[...] = (d3 * gate).astype(BF16)
        dgl = (d3 * pe * gate * (1.0 - gate)).astype(BF16)
        dgl_ref[...] = dgl
        dh = lax.dot_general(dgl, w_ref[...], (((1,), (1,)), ((), ())), preferred_element_type=F32)
        _, vjp = jax.vjp(_rms, x_ref[...], g_ref[...])
        dxn, dg = vjp(dh)
        dx = d3 + dxn
        dx_ref[...] = dx
        dxb_ref[...] = dx.astype(BF16)
        _acc_out(dg_ref, dg, pl.program_id(0) == 0)

    row = pl.BlockSpec((tm, n), lambda i: (i, 0))
    vec = pl.BlockSpec((1, n), lambda i: (0, 0))
    blk = _nbytes((n, n), BF16) + 9 * _nbytes((tm, n), F32)
    return _pcall(body, name=name,
                  out_shape=(_sds((m, n), F32), _sds((m, n), BF16), _sds((1, n), F32), _sds((m, n), BF16), _sds((m, n), BF16)),
                  grid=(m // tm,),
                  in_specs=[row, row, pl.BlockSpec((tm, p.shape[1]), lambda i: (i, 0)), _spec(w_pe), _spec(w_pg), row,
                            _spec(g)],
                  out_specs=(row, row, vec, row, row), semantics=("arbitrary",), block_bytes=blk)(
                      dx3, gl, p, _arr(w_pe), _arr(w_pg), x, _arr(g))


def _matmul_tn(a, b, *, name, out_dtype=BF16, out_rows=None, row_off=0, into=None):
    m, k1 = a.shape
    n = b.shape[1]
    tk = _pick(k1, (512, 256, 128))
    off = row_off // tk
    assert off * tk == row_off

    def body(a_ref, b_ref, *rest):
        rest[-1][...] = lax.dot_general(a_ref[...], b_ref[...], (((0,), (0,)), ((), ())),
                                        preferred_element_type=F32).astype(out_dtype)

    blk = 2 * _nbytes((m, tk), a.dtype) + _nbytes((m, n), b.dtype) + _nbytes((tk, n), F32)
    in_specs = [pl.BlockSpec((m, tk), lambda i: (0, i)), pl.BlockSpec((m, n), lambda i: (0, 0))]
    args = [a, b]
    if into is not None:
        in_specs.append(HBM_SPEC)
        args.append(into)
    return _pcall(body, name=name, out_shape=_sds((out_rows or k1, n), out_dtype), grid=(k1 // tk,), in_specs=in_specs,
                  out_specs=pl.BlockSpec((tk, n), lambda i: (i + off, 0)), semantics=("parallel",), block_bytes=blk,
                  aliases=None if into is None else {2: 0})(*args)


def _rms_matmul(x, g, bs, *, name, nt=False, ple=None):
    m, d = x.shape
    n = bs[0].shape[0] if nt else bs[0].shape[1]
    nb = len(bs)
    nout = nb if ple is None else 2
    best = None
    for tm_c in (1024, 512, 256):
        for tn_c in (1408, 1024, 768, 512, 256, 128):
            if m % tm_c or n % tn_c:
                continue
            blk_c = (_nbytes((tm_c, d), F32) + 2 * _nbytes((tm_c, d), BF16) + nb * _nbytes((d, tn_c), BF16)
                     + (nout + 1) * _nbytes((tm_c, tn_c), F32))
            steps = (m // tm_c) * (n // tn_c)
            if blk_c <= MATMUL_BLOCK_BUDGET and (best is None or steps < best[0]):
                best = (steps, tm_c, tn_c, blk_c)
    _, tm, tn, blk = best
    dims = (((1,), (1,)), ((), ())) if nt else (((1,), (0,)), ((), ()))

    def body(*refs):
        x_ref, g_ref, b_refs = refs[0], refs[1], refs[2:2 + nb]
        rest = refs[2 + nb:]
        h_scr = rest[-1]
        j = pl.program_id(1)

        @pl.when(j == 0)
        def _():
            h = _rms(x_ref[...], g_ref[...]).astype(BF16)
            h_scr[...] = h
            rest[-2 - nout][...] = h

        h = h_scr[...]
        if ple is None:
            for k in range(nb):
                rest[-1 - nb + k][...] = lax.dot_general(h, b_refs[k][...], dims, preferred_element_type=F32)
        else:
            p_ref, wpe_ref = rest[0], rest[1]
            gl_ref, out_ref = rest[-3], rest[-2]
            gl = lax.dot_general(h, b_refs[0][...], dims, preferred_element_type=F32)
            pe = lax.dot_general(p_ref[...], wpe_ref[...], (((1,), (1,)), ((), ())), preferred_element_type=F32)
            gl_ref[...] = gl
            upd = pe * jax.nn.sigmoid(gl)
            for jj in range(n // tn):
                @pl.when(j == jj)
                def _():
                    out_ref[...] = x_ref[:, jj * tn:(jj + 1) * tn] + upd

    row = pl.BlockSpec((tm, d), lambda i, j: (i, 0))
    tile = pl.BlockSpec((tm, tn), lambda i, j: (i, j))
    in_specs = [row, _spec(g)] + [_spec(b, (tn, d), lambda i, j: (j, 0)) if nt else _spec(b, (d, tn), lambda i, j: (0, j))
                                  for b in bs]
    args = [x, _arr(g)] + [_arr(b) for b in bs]
    out_shape, out_specs = [_sds((m, d), BF16)], [row]
    if ple is None:
        out_shape += [_sds((m, n), F32)] * nb
        out_specs += [tile] * nb
    else:
        p, wpe = ple
        in_specs += [pl.BlockSpec((tm, p.shape[1]), lambda i, j: (i, 0)), _spec(wpe, (tn, p.shape[1]), lambda i, j: (j, 0))]
        args += [p, _arr(wpe)]
        out_shape += [_sds((m, n), F32)] * 2
        out_specs += [tile] * 2
    outs = _pcall(body, name=name, out_shape=tuple(out_shape), grid=(m // tm, n // tn), in_specs=in_specs,
                  out_specs=tuple(out_specs), scratch_shapes=[pltpu.VMEM((tm, d), BF16)],
                  semantics=("parallel", "arbitrary"), block_bytes=blk)(*args)
    return outs[0], list(outs[1:])


def _up_ffn_fwd(x, g, wg, wv, cwf, cbf, *, name):
    m, d = x.shape
    n = wg.shape[0]
    tm = _pick(m, (256, 128))
    tn = _pick(n, (1408, 256, 128))
    nj = n // tn
    dims = (((1,), (1,)), ((), ()))

    def body(x_ref, g_ref, wg_ref, wv_ref, tg_ref, bg_ref, tv_ref, bv_ref, h_ref, hg_ref, hv_ref, a_ref, cg_scr, cv_scr):
        i = pl.program_id(1)
        h = _rms(x_ref[...], g_ref[...]).astype(BF16)
        h_ref[...] = h
        hg = lax.dot_general(h, wg_ref[...], dims, preferred_element_type=F32)
        hv = lax.dot_general(h, wv_ref[...], dims, preferred_element_type=F32)
        hg_ref[...] = hg
        hv_ref[...] = hv
        eg = jnp.concatenate([jnp.where(i == 0, 0.0, cg_scr[...]), hg], axis=0)
        ev = jnp.concatenate([jnp.where(i == 0, 0.0, cv_scr[...]), hv], axis=0)
        a_ref[...] = _ffn_tile(eg, ev, tg_ref[...], bg_ref[...], tv_ref[...], bv_ref[...]).astype(BF16)
        cg_scr[...] = hg[tm - 8:]
        cv_scr[...] = hv[tm - 8:]

    row = pl.BlockSpec((tm, d), lambda j, i: (i, 0))
    hrow = pl.BlockSpec((tm, d), lambda j, i: (j * (m // tm) + i, 0))
    tile = pl.BlockSpec((tm, tn), lambda j, i: (i, j))
    wspec = lambda w: _spec(w, (tn, d), lambda j, i: (j, 0))
    taps = lambda off: _spec(cwf, (3, tn), lambda j, i: (0, j + off))
    bias = lambda off: _spec(cbf, (1, tn), lambda j, i: (0, j + off))
    blk = (_nbytes((tm, d), F32) + _nbytes((tm, d), BF16) + 2 * _nbytes((tn, d), BF16) + 12 * _nbytes((tm, tn), F32))
    return _pcall(body, name=name,
                  out_shape=(_sds((nj * m, d), BF16), _sds((m, n), F32), _sds((m, n), F32), _sds((m, n), BF16)),
                  grid=(nj, m // tm),
                  in_specs=[row, _spec(g), wspec(wg), wspec(wv), taps(0), bias(0), taps(nj), bias(nj)],
                  out_specs=(hrow, tile, tile, tile),
                  scratch_shapes=[pltpu.VMEM((8, tn), F32), pltpu.VMEM((8, tn), F32)],
                  semantics=("arbitrary", "arbitrary"), block_bytes=blk)(
                      x, _arr(g), _arr(wg), _arr(wv), _arr(cwf), _arr(cbf), _arr(cwf), _arr(cbf))


def _loss_head(x, g, target, *, name):
    s, d = x.shape
    tm = _pick(s, (256, 128))

    def tile_loss(xv, gv, tv):
        err = jnp.square(_rms(xv, gv) - tv)
        return 0.5 * jnp.sum(jnp.mean(err, axis=-1, keepdims=True), axis=0, keepdims=True)

    def body(x_ref, g_ref, t_ref, l_ref, dx_ref, dg_ref):
        lv, vjp = jax.vjp(tile_loss, x_ref[...], g_ref[...], t_ref[...])
        dxv, dgv, _ = vjp(jnp.ones((1, 1), F32))
        dx_ref[...] = dxv

        @pl.when(pl.program_id(0) == 0)
        def _():
            l_ref[...] = jnp.zeros_like(l_ref)
            dg_ref[...] = jnp.zeros_like(dg_ref)

        l_ref[...] += jnp.broadcast_to(lv, l_ref.shape)
        dg_ref[...] += dgv

    row = pl.BlockSpec((tm, d), lambda i: (i, 0))
    vec = pl.BlockSpec((1, d), lambda i: (0, 0))
    return _pcall(body, name=name, out_shape=(_sds((8, 128), F32), _sds((s, d), F32), _sds((1, d), F32)),
                  grid=(s // tm,), in_specs=[row, vec, row],
                  out_specs=(pl.BlockSpec((8, 128), lambda i: (0, 0)), row, vec), semantics=("arbitrary",),
                  block_bytes=8 * _nbytes((tm, d), F32))(x, g, target)


def _acc_out(ref, val, first):
    @pl.when(first)
    def _():
        ref[...] = jnp.zeros_like(ref)

    ref[...] += val


def _gmlp_fwd(z, ln_g, ln_b, wcat, bfull, *, name):
    s = z.shape[0]
    t = _pick(s, (512, 256, 128))
    nch = t // GMLP_CHUNK

    def body(zu_ref, zv_ref, g_ref, b_ref, w_ref, bf_ref, o_ref):
        for c in range(nch):
            rows = pl.ds(c * GMLP_CHUNK, GMLP_CHUNK)
            o_ref[rows, :] = _gmlp_chunk(zu_ref[rows, :], zv_ref[rows, :], g_ref[...], b_ref[...], w_ref[...],
                                         bf_ref[...]).astype(BF16)

    col = lambda c: pl.BlockSpec((t, W_GRP), lambda i: (i, c))
    params = (ln_g, ln_b, wcat, bfull)
    return _pcall(body, name=name, out_shape=_sds((s, D_MODEL), BF16), grid=(s // t,),
                  in_specs=[col(0), col(1)] + [_spec(a) for a in params],
                  out_specs=pl.BlockSpec((t, W_GRP), lambda i: (i, 0)), semantics=("parallel",),
                  block_bytes=4 * _nbytes((t, W_GRP), F32))(z, z, *[_arr(a) for a in params])


def _gmlp_bwd(z, dmix, ln_g, ln_b, wcat, bfull, *, name):
    s = z.shape[0]
    t = _pick(s, (512, 256, 128))
    nch = t // GMLP_CHUNK

    def body(zu_ref, zv_ref, dy_ref, g_ref, b_ref, w_ref, bf_ref, dz_ref, dg_ref, db_ref, dw_ref, dbf_ref):
        acc = None
        for c in range(nch):
            rows = pl.ds(c * GMLP_CHUNK, GMLP_CHUNK)
            _, vjp = jax.vjp(_gmlp_chunk, zu_ref[rows, :], zv_ref[rows, :], g_ref[...], b_ref[...], w_ref[...],
                             bf_ref[...])
            du, dv, *dps = vjp(dy_ref[rows, :])
            dz_ref[rows, :] = jnp.concatenate([du, dv], axis=1).astype(BF16)
            acc = dps if acc is None else [x + y for x, y in zip(acc, dps)]
        first = pl.program_id(0) == 0
        for ref, val in zip((dg_ref, db_ref, dw_ref, dbf_ref), acc):
            _acc_out(ref, val, first)

    col = lambda c: pl.BlockSpec((t, W_GRP), lambda i: (i, c))
    params = (ln_g, ln_b, wcat, bfull)
    return _pcall(body, name=name,
                  out_shape=(_sds((s, D_PROJ), BF16),) + tuple(_sds(a.shape, F32) for a in params),
                  grid=(s // t,), in_specs=[col(0), col(1), col(0)] + [_spec(a) for a in params],
                  out_specs=(pl.BlockSpec((t, 2 * W_GRP), lambda i: (i, 0)),) + tuple(_ospec(a) for a in params),
                  semantics=("arbitrary",),
                  block_bytes=8 * _nbytes((t, W_GRP), F32))(z, z, dmix, *[_arr(a) for a in params])


def _rglru_fwd(z, prm, mix, *, name):
    s = z.shape[0]
    t = _pick(s, (512, 256, 128))
    nt = s // t

    def body(xb_ref, halo_ref, gb_ref, *rest):
        prm_refs, (y_ref, h0s_ref, h_scr) = rest[:len(prm)], rest[len(prm) + 1:]
        i = pl.program_id(0)

        @pl.when(i == 0)
        def _():
            h_scr[...] = jnp.zeros_like(h_scr)

        halo = jnp.where(i == 0, 0.0, halo_ref[...])
        h0 = h_scr[...]
        y, h_last = _rglru_tile(jnp.concatenate([halo, xb_ref[...]], axis=0), gb_ref[...], h0,
                                *[r[...] for r in prm_refs])
        y_ref[...] = y.astype(BF16)
        h0s_ref[...] = jnp.broadcast_to(h0, h0s_ref.shape)
        h_scr[...] = h_last

    in_specs = [pl.BlockSpec((t, W_GRP), lambda i: (i, 2)),
                pl.BlockSpec((8, W_GRP), lambda i: (jnp.maximum(i * (t // 8) - 1, 0), 2)),
                pl.BlockSpec((t, W_GRP), lambda i: (i, 3))] + [_spec(a) for a in prm] + [HBM_SPEC]
    return _pcall(body, name=name, out_shape=(_sds(mix.shape, BF16), _sds((nt, 8, W_GRP), F32)), grid=(nt,),
                  in_specs=in_specs,
                  out_specs=(pl.BlockSpec((t, W_GRP), lambda i: (i, 1)), pl.BlockSpec((None, 8, W_GRP), lambda i: (i, 0, 0))),
                  scratch_shapes=[pltpu.VMEM((1, W_GRP), F32)], semantics=("arbitrary",),
                  block_bytes=24 * _nbytes((t, W_GRP), F32), aliases={3 + len(prm): 0})(
                      z, z, z, *[_arr(a) for a in prm], mix)


def _rglru_bwd(z, dmix, h0s, prm, dz, *, name):
    s = z.shape[0]
    t = _pick(s, (512, 256, 128))
    nt = s // t
    npm = len(prm)

    def body(xb_ref, halo_ref, gb_ref, dy_ref, h0s_ref, *rest):
        prm_refs = rest[:npm]
        dz_ref = rest[npm + 1]
        dprm_refs = rest[npm + 2:2 * npm + 2]
        dh_scr, dhalo_scr = rest[2 * npm + 2:]
        i = pl.program_id(0)
        r = nt - 1 - i

        @pl.when(i == 0)
        def _():
            dh_scr[...] = jnp.zeros_like(dh_scr)
            dhalo_scr[...] = jnp.zeros_like(dhalo_scr)

        halo = jnp.where(r == 0, 0.0, halo_ref[...])
        h0 = h0s_ref[0:1, :]
        _, vjp = jax.vjp(_rglru_tile, jnp.concatenate([halo, xb_ref[...]], axis=0), gb_ref[...], h0,
                         *[p[...] for p in prm_refs])
        dext, dgb, _dh0, *dps = vjp((dy_ref[...], dh_scr[...]))
        dmain = dext[8:]
        dxb = jnp.concatenate([dmain[:t - 8], dmain[t - 8:] + dhalo_scr[...]], axis=0)
        dz_ref[...] = jnp.concatenate([dxb, dgb], axis=1).astype(BF16)
        dh_scr[...] = _dh0
        dhalo_scr[...] = dext[:8]
        for ref, val in zip(dprm_refs, dps):
            _acc_out(ref, val, i == 0)

    rev = lambda c: pl.BlockSpec((t, W_GRP), lambda i: (nt - 1 - i, c))
    in_specs = [rev(2), pl.BlockSpec((8, W_GRP), lambda i: (jnp.maximum((nt - 1 - i) * (t // 8) - 1, 0), 2)), rev(3),
                rev(1), pl.BlockSpec((None, 8, W_GRP), lambda i: (nt - 1 - i, 0, 0))] + [_spec(a) for a in prm] + [HBM_SPEC]
    return _pcall(body, name=name,
                  out_shape=(_sds(dz.shape, BF16),) + tuple(_sds(a.shape, F32) for a in prm),
                  grid=(nt,), in_specs=in_specs,
                  out_specs=(pl.BlockSpec((t, 2 * W_GRP), lambda i: (nt - 1 - i, 1)),) + tuple(_ospec(a) for a in prm),
                  scratch_shapes=[pltpu.VMEM((1, W_GRP), F32), pltpu.VMEM((8, W_GRP), F32)],
                  semantics=("arbitrary",), block_bytes=40 * _nbytes((t, W_GRP), F32), aliases={5 + npm: 0})(
                      z, z, z, dmix, h0s, *[_arr(a) for a in prm], dz)


def _pool_inv(i, t):
    pos = (_rows_of((t, W_GRP)) + i * t + 1).astype(F32)
    grp = _lanes_of((t, W_GRP)) // HEAD_DIM
    win = jnp.where(grp == 0, float(POOL_WINDOWS[0]), jnp.where(grp == 1, float(POOL_WINDOWS[1]),
                    jnp.where(grp == 2, float(POOL_WINDOWS[2]), float(POOL_WINDOWS[3]))))
    return 1.0 / jnp.minimum(pos, win)


def _pool_fwd(z, wd, scale, mix, *, name):
    s = z.shape[0]
    t = _pick(s, (512, 256, 128))

    def body(x_ref, halo_ref, wd_ref, sc_ref, _, y_ref):
        i = pl.program_id(0)
        halo = jnp.where(i == 0, 0.0, halo_ref[...])
        y = _pool_tile(jnp.concatenate([halo, x_ref[...]], axis=0), _pool_inv(i, t), wd_ref[...], sc_ref[...])
        y_ref[...] = y.astype(BF16)

    in_specs = [pl.BlockSpec((t, W_GRP), lambda i: (i, 8)),
                pl.BlockSpec((16, W_GRP), lambda i: (jnp.maximum(i * (t // 16) - 1, 0), 8)), _spec(wd), _spec(scale),
                HBM_SPEC]
    return _pcall(body, name=name, out_shape=_sds(mix.shape, BF16), grid=(s // t,), in_specs=in_specs,
                  out_specs=pl.BlockSpec((t, W_GRP), lambda i: (i, 3)), semantics=("parallel",),
                  block_bytes=12 * _nbytes((t, W_GRP), F32), aliases={4: 0})(z, z, _arr(wd), _arr(scale), mix)


def _pool_bwd(z, dmix, wd, scale, dz, *, name):
    s = z.shape[0]
    t = _pick(s, (512, 256, 128))
    nt = s // t

    def body(x_ref, halo_ref, dy_ref, wd_ref, sc_ref, _, dx_ref, dwd_ref, dsc_ref, dhalo_scr):
        i = pl.program_id(0)
        r = nt - 1 - i

        @pl.when(i == 0)
        def _():
            dhalo_scr[...] = jnp.zeros_like(dhalo_scr)

        halo = jnp.where(r == 0, 0.0, halo_ref[...])
        inv = _pool_inv(r, t)
        _, vjp = jax.vjp(lambda e, w, sc: _pool_tile(e, inv, w, sc), jnp.concatenate([halo, x_ref[...]], axis=0),
                         wd_ref[...], sc_ref[...])
        dext, dwd, dsc = vjp(dy_ref[...])
        dmain = dext[16:]
        dx = jnp.concatenate([dmain[:t - 16], dmain[t - 16:] + dhalo_scr[...]], axis=0)
        dx_ref[...] = dx.astype(BF16)
        dhalo_scr[...] = dext[:16]
        _acc_out(dwd_ref, dwd, i == 0)
        _acc_out(dsc_ref, dsc, i == 0)

    rev = lambda c: pl.BlockSpec((t, W_GRP), lambda i: (nt - 1 - i, c))
    in_specs = [rev(8), pl.BlockSpec((16, W_GRP), lambda i: (jnp.maximum((nt - 1 - i) * (t // 16) - 1, 0), 8)), rev(3),
                _spec(wd), _spec(scale), HBM_SPEC]
    return _pcall(body, name=name, out_shape=(_sds(dz.shape, BF16), _sds(wd.shape, F32), _sds(scale.shape, F32)),
                  grid=(nt,), in_specs=in_specs, out_specs=(rev(8), _ospec(wd), _ospec(scale)),
                  scratch_shapes=[pltpu.VMEM((16, W_GRP), F32)], semantics=("arbitrary",),
                  block_bytes=20 * _nbytes((t, W_GRP), F32), aliases={5: 0})(z, z, dmix, _arr(wd), _arr(scale), dz)


def _hgrn_fwd(z, lb, ngf, mix, *, name):
    s = z.shape[0]
    c = HGRN_CHUNK
    per = HGRN_STEP_CHUNKS
    ns = s // (c * per)

    def body(q_ref, f_ref, i_ref, g_ref, lb_ref, ng_ref, _, y_ref, sts_ref, st_scr):
        @pl.when(pl.program_id(0) == 0)
        def _():
            st_scr[...] = jnp.zeros_like(st_scr)

        st = st_scr[...]
        for k in range(per):
            rows = pl.ds(k * c, c)
            sts_ref[k] = st
            y, st = _hgrn_chunk(q_ref[rows, :], f_ref[rows, :], i_ref[rows, :], g_ref[rows, :], st, lb_ref[...],
                                ng_ref[...])
            y_ref[rows, :] = y.astype(BF16)
        st_scr[...] = st

    col = lambda k: pl.BlockSpec((per * c, W_GRP), lambda i: (i, k))
    return _pcall(body, name=name, out_shape=(_sds(mix.shape, BF16), _sds((ns * per, W_GRP, W_GRP), F32)), grid=(ns,),
                  in_specs=[col(4), col(5), col(6), col(7), _spec(lb), _spec(ngf), HBM_SPEC],
                  out_specs=(pl.BlockSpec((per * c, W_GRP), lambda i: (i, 2)),
                             pl.BlockSpec((per, W_GRP, W_GRP), lambda i: (i, 0, 0))),
                  scratch_shapes=[pltpu.VMEM((W_GRP, W_GRP), F32)], semantics=("arbitrary",),
                  block_bytes=16 * per * _nbytes((W_GRP, W_GRP), F32), aliases={6: 0})(
                      z, z, z, z, _arr(lb), _arr(ngf), mix)


def _hgrn_bwd(z, dmix, sts, lb, ngf, dz, *, name):
    s = z.shape[0]
    c = HGRN_CHUNK
    per = HGRN_STEP_CHUNKS
    ns = s // (c * per)

    def body(q_ref, f_ref, i_ref, g_ref, dy_ref, st_ref, lb_ref, ng_ref, _, dz_ref, dlb_ref, dng_ref, dst_scr):
        i = pl.program_id(0)

        @pl.when(i == 0)
        def _():
            dst_scr[...] = jnp.zeros_like(dst_scr)

        dst = dst_scr[...]
        dlb_sum = dng_sum = None
        for k in range(per - 1, -1, -1):
            rows = pl.ds(k * c, c)
            _, vjp = jax.vjp(_hgrn_chunk, q_ref[rows, :], f_ref[rows, :], i_ref[rows, :], g_ref[rows, :], st_ref[k],
                             lb_ref[...], ng_ref[...])
            dq, df, di, dg, dst, dlb, dng = vjp((dy_ref[rows, :], dst))
            dz_ref[rows, :] = jnp.concatenate([dq, df, di, dg], axis=1).astype(BF16)
            dlb_sum = dlb if dlb_sum is None else dlb_sum + dlb
            dng_sum = dng if dng_sum is None else dng_sum + dng
        dst_scr[...] = dst
        _acc_out(dlb_ref, dlb_sum, i == 0)
        _acc_out(dng_ref, dng_sum, i == 0)

    rev = lambda k: pl.BlockSpec((per * c, W_GRP), lambda i: (ns - 1 - i, k))
    vec = pl.BlockSpec((1, W_GRP), lambda i: (0, 0))
    return _pcall(body, name=name, out_shape=(_sds(dz.shape, BF16), _sds((1, W_GRP), F32), _sds((1, W_GRP), F32)),
                  grid=(ns,),
                  in_specs=[rev(4), rev(5), rev(6), rev(7), rev(2),
                            pl.BlockSpec((per, W_GRP, W_GRP), lambda i: (ns - 1 - i, 0, 0)), _spec(lb), _spec(ngf),
                            HBM_SPEC],
                  out_specs=(pl.BlockSpec((per * c, 4 * W_GRP), lambda i: (ns - 1 - i, 1)), vec, vec),
                  scratch_shapes=[pltpu.VMEM((W_GRP, W_GRP), F32)], semantics=("arbitrary",),
                  block_bytes=32 * per * _nbytes((W_GRP, W_GRP), F32), aliases={8: 0})(
                      z, z, z, z, dmix, sts, _arr(lb), _arr(ngf), dz)


def _lbs_fwd(c_lb, *, name):
    def body(c_ref, o_ref):
        c = c_ref[...]
        e = jnp.exp(c - jnp.max(c, axis=0, keepdims=True))
        sm = e / jnp.sum(e, axis=0, keepdims=True)
        run = jnp.zeros((1, W_GRP), F32)
        o_ref[0:1, :] = run
        for l in range(1, DEPTH):
            run = run + sm[l:l + 1]
            o_ref[l:l + 1, :] = run

    return _pcall(body, name=name, out_shape=_sds((DEPTH, W_GRP), F32), pin=False)(c_lb)


def _lbs_bwd(c_lb, dlbs, *, name):
    def body(c_ref, d_ref, o_ref):
        c = c_ref[...]
        e = jnp.exp(c - jnp.max(c, axis=0, keepdims=True))
        sm = e / jnp.sum(e, axis=0, keepdims=True)
        d = d_ref[...]
        dsm = [None] * DEPTH
        run = jnp.zeros((1, W_GRP), F32)
        for l in range(DEPTH - 1, 0, -1):
            run = run + d[l:l + 1]
            dsm[l] = run
        dsm[0] = jnp.zeros((1, W_GRP), F32)
        inner = sum(sm[l:l + 1] * dsm[l] for l in range(DEPTH))
        for l in range(DEPTH):
            o_ref[l:l + 1, :] = sm[l:l + 1] * (dsm[l] - inner)

    return _pcall(body, name=name, out_shape=_sds((DEPTH, W_GRP), F32), pin=False)(c_lb, dlbs)


def _ffn_bwd(hg, hv, dx, w_down, cwf, cbf, *, name):
    s, n = hg.shape
    t = _pick(s, (256, 128))
    cw = _pick(n, (1408, 256, 128))
    nt = s // t
    nj = n // cw

    def body(g_ref, gh_ref, v_ref, vh_ref, dx_ref, wd_ref, wg_ref, bg_ref, wv_ref, bv_ref, dg_ref, dv_ref, dwg_ref,
             dwv_ref, cg_scr, cv_scr):
        i = pl.program_id(1)
        r = nt - 1 - i

        @pl.when(i == 0)
        def _():
            cg_scr[...] = jnp.zeros_like(cg_scr)
            cv_scr[...] = jnp.zeros_like(cv_scr)

        da = lax.dot_general(dx_ref[...], wd_ref[...], (((1,), (1,)), ((), ())), preferred_element_type=F32)
        eg = jnp.concatenate([jnp.where(r == 0, 0.0, gh_ref[...]), g_ref[...]], axis=0)
        ev = jnp.concatenate([jnp.where(r == 0, 0.0, vh_ref[...]), v_ref[...]], axis=0)
        _, vjp = jax.vjp(_ffn_tile, eg, ev, wg_ref[...], bg_ref[...], wv_ref[...], bv_ref[...])
        deg, dev, dwg, dbg, dwv, dbv = vjp(da)
        for dext, scr, ref in ((deg, cg_scr, dg_ref), (dev, cv_scr, dv_ref)):
            dmain = dext[8:]
            ref[...] = jnp.concatenate([dmain[:t - 8], dmain[t - 8:] + scr[...]], axis=0).astype(BF16)
            scr[...] = dext[:8]
        zeros = jnp.zeros((4, cw), F32)
        _acc_out(dwg_ref, jnp.concatenate([dwg, dbg, zeros], axis=0), i == 0)
        _acc_out(dwv_ref, jnp.concatenate([dwv, dbv, zeros], axis=0), i == 0)

    main = pl.BlockSpec((t, cw), lambda j, i: (nt - 1 - i, j))
    halo = pl.BlockSpec((8, cw), lambda j, i: (jnp.maximum((nt - 1 - i) * (t // 8) - 1, 0), j))
    taps = lambda off: _spec(cwf, (3, cw), lambda j, i: (0, j + off))
    bias = lambda off: _spec(cbf, (1, cw), lambda j, i: (0, j + off))
    w8 = pl.BlockSpec((8, cw), lambda j, i: (0, j))
    d = dx.shape[1]
    in_specs = [main, halo, main, halo, pl.BlockSpec((t, d), lambda j, i: (nt - 1 - i, 0)),
                _spec(w_down, (cw, d), lambda j, i: (j, 0)), taps(0), bias(0), taps(nj), bias(nj)]
    return _pcall(body, name=name,
                  out_shape=(_sds((s, n), BF16), _sds((s, n), BF16), _sds((8, n), F32), _sds((8, n), F32)),
                  grid=(nj, nt), in_specs=in_specs, out_specs=(main, main, w8, w8),
                  scratch_shapes=[pltpu.VMEM((8, cw), F32), pltpu.VMEM((8, cw), F32)],
                  semantics=("parallel", "arbitrary"),
                  block_bytes=24 * _nbytes((t, cw), F32) + _nbytes((cw, d), BF16))(
                      hg, hg, hv, hv, dx, _arr(w_down), _arr(cwf), _arr(cbf), _arr(cwf), _arr(cbf))


def _all_gather(x, *, name):
    r, c = x.shape

    def body(x_ref, out_ref, send_sems, recv_sems, local_sem):
        mx, my, mc = lax.axis_index("x"), lax.axis_index("y"), lax.axis_index("c")
        me, sibling = (mx, my, mc), (mx, my, 1 - mc)
        chips = [(1 - mx, my), (mx, 1 - my), (1 - mx, 1 - my)]

        def slot(px, py, pc):
            return out_ref.at[4 * px + 2 * py + pc]

        def copy(k, block, to, src=None):
            return pltpu.make_async_remote_copy(src_ref=slot(*block) if src is None else src, dst_ref=slot(*block),
                                                send_sem=send_sems.at[k], recv_sem=recv_sems.at[k],
                                                device_id=to, device_id_type=MESH)

        mine = pltpu.make_async_copy(x_ref, slot(*me), local_sem)
        mine.start()
        first = [copy(0, me, sibling, src=x_ref)]
        first += [copy(1 + j, me, (*chip, mc), src=x_ref) for j, chip in enumerate(chips)]
        for cp in first:
            cp.start()
        passed = [copy(4 + j, (*chip, mc), sibling) for j, chip in enumerate(chips)]
        for j, chip in enumerate(chips):
            copy(1 + j, (*chip, mc), me).wait_recv()
            passed[j].start()
        copy(0, sibling, me).wait_recv()
        for j, chip in enumerate(chips):
            copy(4 + j, (*chip, 1 - mc), me).wait_recv()
        for cp in first + passed:
            cp.wait_send()
        mine.wait()

    hbm = pl.BlockSpec(memory_space=pl.ANY)
    return _pcall(body, name=name, out_shape=_sds((N_DEV, r, c), x.dtype), in_specs=[hbm], out_specs=hbm,
                  scratch_shapes=[pltpu.SemaphoreType.DMA((7,)), pltpu.SemaphoreType.DMA((7,)),
                                  pltpu.SemaphoreType.DMA(())])(x)


def _sum_slots(p, *, name):
    q, r, c = p.shape
    tr = _pick(r, (544, 408, 272, 192, 136, 64, 32, 16, 8))

    def body(p_ref, o_ref):
        acc = p_ref[0].astype(F32)
        for k in range(1, q):
            acc = acc + p_ref[k].astype(F32)
        o_ref[...] = acc

    return _pcall(body, name=name, out_shape=_sds((r, c), F32), grid=(r // tr,),
                  in_specs=[pl.BlockSpec((q, tr, c), lambda i: (0, i, 0))],
                  out_specs=pl.BlockSpec((tr, c), lambda i: (i, 0)), semantics=("parallel",),
                  block_bytes=(q + 2) * _nbytes((tr, c), F32))(p)


BIG_COMM = (('w_in', 288, D_MODEL), ('w_out', 128, D_MODEL), ('w_up', 704, D_MODEL), ('w_down', 352, D_MODEL),
            ('w_pe', 128, PLE_DIM), ('w_pg', 128, D_MODEL))
HBM_SPEC = pl.BlockSpec(memory_space=pl.ANY)


def _gather_layer(shards, l, *, name):
    na = len(shards)

    def body(*refs):
        x_refs, out_refs = refs[:na], refs[na:2 * na]
        send_sems, recv_sems, local_sems = refs[2 * na:]
        mx, my, mc = lax.axis_index("x"), lax.axis_index("y"), lax.axis_index("c")
        me, sibling = (mx, my, mc), (mx, my, 1 - mc)
        chips = [(1 - mx, my), (mx, 1 - my), (1 - mx, 1 - my)]

        def slot(a, px, py, pc):
            return out_refs[a].at[4 * px + 2 * py + pc]

        def copy(k, a, block, to, own=False):
            return pltpu.make_async_remote_copy(src_ref=x_refs[a].at[l] if own else slot(a, *block),
                                                dst_ref=slot(a, *block), send_sem=send_sems.at[k, a],
                                                recv_sem=recv_sems.at[k, a], device_id=to, device_id_type=MESH)

        mine = [pltpu.make_async_copy(x_refs[a].at[l], slot(a, *me), local_sems.at[a]) for a in range(na)]
        for cp in mine:
            cp.start()
        first = []
        for a in range(na):
            first.append(copy(0, a, me, sibling, own=True))
            first += [copy(1 + j, a, me, (*chip, mc), own=True) for j, chip in enumerate(chips)]
        for cp in first:
            cp.start()
        passed = []
        for j, chip in enumerate(chips):
            for a in range(na):
                copy(1 + j, a, (*chip, mc), me).wait_recv()
                fwd = copy(4 + j, a, (*chip, mc), sibling)
                fwd.start()
                passed.append(fwd)
        for a in range(na):
            copy(0, a, sibling, me).wait_recv()
        for j, chip in enumerate(chips):
            for a in range(na):
                copy(4 + j, a, (*chip, 1 - mc), me).wait_recv()
        for cp in first + passed:
            cp.wait_send()
        for cp in mine:
            cp.wait()

    return _pcall(body, name=name, out_shape=tuple(_sds((N_DEV,) + x.shape[1:], x.dtype) for x in shards),
                  in_specs=[HBM_SPEC] * na, out_specs=(HBM_SPEC,) * na,
                  scratch_shapes=[pltpu.SemaphoreType.DMA((7, na)), pltpu.SemaphoreType.DMA((7, na)),
                                  pltpu.SemaphoreType.DMA((na,))])(*shards)


SEM_SPEC = pl.BlockSpec(memory_space=pltpu.SEMAPHORE)
DATAFLOW_EFFECT = pltpu.SideEffectType.DATAFLOW_SIDE_EFFECTING


def _place_own(srcs, after, *, name):
    na = len(srcs)

    def body(*refs):
        x_refs, land_refs, sems = refs[:na], refs[na + len(after):2 * na + len(after)], refs[-1]
        me = 4 * lax.axis_index("x") + 2 * lax.axis_index("y") + lax.axis_index("c")
        cps = [pltpu.make_async_copy(x_refs[a], land_refs[a].at[me], sems.at[a]) for a in range(na)]
        for cp in cps:
            cp.start()
        for cp in cps:
            cp.wait()

    return _pcall(body, name=name, out_shape=tuple(_sds((N_DEV,) + x.shape, x.dtype) for x in srcs),
                  in_specs=[HBM_SPEC] * (na + len(after)), out_specs=(HBM_SPEC,) * na,
                  scratch_shapes=[pltpu.SemaphoreType.DMA((na,))], pin=False)(*srcs, *after)


def _exchange_start(srcs, lands, *, name, per_peer=False):
    na = len(srcs)

    def body(*refs):
        x_refs, land_refs = refs[:na], refs[na:2 * na]
        send_sems, recv_sems = refs[2 * na], refs[2 * na + 1]
        token = refs[-1]
        mx, my, mc = lax.axis_index("x"), lax.axis_index("y"), lax.axis_index("c")
        me = 4 * mx + 2 * my + mc
        peers = [(mx, my, 1 - mc)]
        for px, py in ((1 - mx, my), (mx, 1 - my), (1 - mx, 1 - my)):
            peers += [(px, py, mc), (px, py, 1 - mc)]
        for a in range(na):
            for peer in peers:
                src = x_refs[a].at[4 * peer[0] + 2 * peer[1] + peer[2]] if per_peer else x_refs[a]
                pltpu.make_async_remote_copy(src_ref=src, dst_ref=land_refs[a].at[me], send_sem=send_sems.at[a],
                                             recv_sem=recv_sems.at[a], device_id=peer, device_id_type=MESH).start()
        token[...] = jnp.zeros_like(token)

    hbm = lambda x: pltpu.HBM(x.shape, x.dtype)
    out_shape = ((pltpu.SemaphoreType.DMA((na,)), pltpu.SemaphoreType.DMA((na,))) + tuple(hbm(x) for x in srcs)
                 + tuple(hbm(x) for x in lands) + (_sds((8, 128), F32),))
    params = pltpu.CompilerParams(has_side_effects=DATAFLOW_EFFECT)
    pin = lambda x: pltpu.with_memory_space_constraint(x, pltpu.HBM)
    return pl.pallas_call(body, name=name, out_shape=out_shape, in_specs=[HBM_SPEC] * (2 * na),
                          out_specs=(SEM_SPEC, SEM_SPEC) + (HBM_SPEC,) * (2 * na) + (pl.BlockSpec(memory_space=pltpu.VMEM),),
                          input_output_aliases={i: 2 + i for i in range(2 * na)}, compiler_params=params)(
                              *[pin(x) for x in srcs], *[pin(x) for x in lands])


def _exchange_wait(started, after, *, name):
    send_sems, recv_sems, *bufs, _ = started
    na = len(bufs) // 2

    def body(*refs):
        land_refs = refs[na:2 * na]
        s_sems, r_sems = refs[2 * na], refs[2 * na + 1]
        me = (lax.axis_index("x"), lax.axis_index("y"), lax.axis_index("c"))
        for a in range(na):
            seven = land_refs[a].at[pl.ds(0, N_DEV - 1)]
            cp = pltpu.make_async_remote_copy(src_ref=seven, dst_ref=seven, send_sem=s_sems.at[a], recv_sem=r_sems.at[a],
                                              device_id=me, device_id_type=MESH)
            cp.wait_send()
            cp.wait_recv()

    hbm = lambda x: pltpu.HBM(x.shape, x.dtype)
    params = pltpu.CompilerParams(has_side_effects=DATAFLOW_EFFECT)
    outs = pl.pallas_call(body, name=name, out_shape=tuple(hbm(x) for x in bufs),
                          in_specs=[HBM_SPEC] * (2 * na) + [SEM_SPEC, SEM_SPEC, HBM_SPEC],
                          out_specs=(HBM_SPEC,) * (2 * na), input_output_aliases={i: i for i in range(2 * na)},
                          compiler_params=params)(*bufs, send_sems, recv_sems, after)
    return outs[:na], outs[na:]


def _pair_swap(grads, *, name):
    na = len(grads)

    def body(*refs):
        g_refs, recv_refs = refs[:na], refs[na:2 * na]
        send_sems, recv_sems = refs[2 * na:]
        mx, my, mc = lax.axis_index("x"), lax.axis_index("y"), lax.axis_index("c")
        sibling = (mx, my, 1 - mc)
        for a in range(na):
            for q in range(4):
                pltpu.make_async_remote_copy(src_ref=g_refs[a].at[q, 1 - mc], dst_ref=recv_refs[a].at[q],
                                             send_sem=send_sems.at[a], recv_sem=recv_sems.at[a],
                                             device_id=sibling, device_id_type=MESH).start()
        for a in range(na):
            pltpu.make_async_remote_copy(src_ref=recv_refs[a], dst_ref=recv_refs[a], send_sem=send_sems.at[a],
                                         recv_sem=recv_sems.at[a], device_id=sibling, device_id_type=MESH).wait()

    half = tuple(_sds((4,) + g.shape[2:], g.dtype) for g in grads)
    return _pcall(body, name=name, out_shape=half, in_specs=[HBM_SPEC] * na, out_specs=(HBM_SPEC,) * na,
                  scratch_shapes=[pltpu.SemaphoreType.DMA((na,)), pltpu.SemaphoreType.DMA((na,))])(*grads)


def _add_slabs(grads, recv, core, *, name):
    na = len(grads)

    def body(core_ref, *refs):
        for a in range(na):
            refs[2 * na + a][...] = (refs[a][...].astype(F32) + refs[na + a][...].astype(F32)).astype(BF16)

    own_specs = [pl.BlockSpec((None, None) + x.shape[2:], lambda q, core_ref: (q, core_ref[0], 0, 0)) for x in grads]
    specs = [pl.BlockSpec((None,) + x.shape[1:], lambda q, core_ref: (q, 0, 0)) for x in recv]
    blk = sum(_nbytes(x.shape[1:], F32) for x in recv)
    grid_spec = pltpu.PrefetchScalarGridSpec(num_scalar_prefetch=1, grid=(4,), in_specs=own_specs + specs,
                                             out_specs=tuple(specs))
    params = pltpu.CompilerParams(dimension_semantics=("parallel",), vmem_limit_bytes=_vmem_limit(2 * blk))
    return pl.pallas_call(body, name=name, out_shape=tuple(_sds(x.shape, BF16) for x in recv), grid_spec=grid_spec,
                          compiler_params=params)(core, *grads, *recv)


def _chip_exchange(parts, *, name):
    na = len(parts)

    def body(*refs):
        p_refs, out_refs = refs[:na], refs[na:2 * na]
        send_sems, recv_sems, local_sems = refs[2 * na:]
        mx, my, mc = lax.axis_index("x"), lax.axis_index("y"), lax.axis_index("c")
        mine_q = 2 * mx + my
        chips = [(1 - mx, my), (mx, 1 - my), (1 - mx, 1 - my)]
        owns = [pltpu.make_async_copy(p_refs[a].at[mine_q], out_refs[a].at[mine_q], local_sems.at[a]) for a in range(na)]
        for cp in owns:
            cp.start()
        sends = []
        for a in range(na):
            for k, chip in enumerate(chips):
                sends.append(pltpu.make_async_remote_copy(
                    src_ref=p_refs[a].at[2 * chip[0] + chip[1]], dst_ref=out_refs[a].at[mine_q],
                    send_sem=send_sems.at[k, a], recv_sem=recv_sems.at[k, a], device_id=(*chip, mc), device_id_type=MESH))
        for cp in sends:
            cp.start()
        for a in range(na):
            for k, chip in enumerate(chips):
                pltpu.make_async_remote_copy(
                    src_ref=p_refs[a].at[mine_q], dst_ref=out_refs[a].at[2 * chip[0] + chip[1]],
                    send_sem=send_sems.at[k, a], recv_sem=recv_sems.at[k, a], device_id=(*chip, mc),
                    device_id_type=MESH).wait_recv()
        for cp in sends:
            cp.wait_send()
        for cp in owns:
            cp.wait()

    return _pcall(body, name=name, out_shape=tuple(_sds(x.shape, x.dtype) for x in parts), in_specs=[HBM_SPEC] * na,
                  out_specs=(HBM_SPEC,) * na,
                  scratch_shapes=[pltpu.SemaphoreType.DMA((3, na)), pltpu.SemaphoreType.DMA((3, na)),
                                  pltpu.SemaphoreType.DMA((na,))])(*parts)


def _sum_chips(parts, *, name):
    na = len(parts)

    def body(*refs):
        for a in range(na):
            p_ref = refs[a]
            acc = p_ref[0].astype(F32)
            for k in range(1, p_ref.shape[0]):
                acc = acc + p_ref[k].astype(F32)
            refs[na + a][...] = acc

    half = lambda x: x.shape[1] // 2
    in_specs = [pl.BlockSpec((x.shape[0], half(x), x.shape[2]), lambda i: (0, i, 0)) for x in parts]
    out_specs = tuple(pl.BlockSpec((half(x), x.shape[2]), lambda i: (i, 0)) for x in parts)
    blk = sum(_nbytes((x.shape[0] + 2, half(x), x.shape[2]), BF16) for x in parts)
    return _pcall(body, name=name, out_shape=tuple(_sds(x.shape[1:], F32) for x in parts), grid=(2,),
                  in_specs=in_specs, out_specs=out_specs, semantics=("parallel",), block_bytes=blk)(*parts)


def _sum_devices(lands, own, me, *, name):
    na = len(lands)

    def body(me_ref, *refs):
        mine = me_ref[0]
        for a in range(na):
            l_ref, o_ref = refs[a], refs[na + a]
            acc = None
            for k in range(N_DEV):
                term = jnp.where(mine == k, o_ref[...], l_ref[k]).astype(F32)
                acc = term if acc is None else acc + term
            refs[2 * na + a][...] = acc

    half = lambda x: x.shape[1] // 2
    land_specs = [pl.BlockSpec((N_DEV, half(x), x.shape[2]), lambda i, me_ref: (0, i, 0)) for x in lands]
    own_specs = [pl.BlockSpec((None, half(x), x.shape[2]), lambda i, me_ref: (me_ref[0], i, 0)) for x in lands]
    out_specs = tuple(pl.BlockSpec((half(x), x.shape[2]), lambda i, me_ref: (i, 0)) for x in lands)
    blk = sum(_nbytes((N_DEV + 3, half(x), x.shape[2]), BF16) for x in lands)
    grid_spec = pltpu.PrefetchScalarGridSpec(num_scalar_prefetch=1, grid=(2,), in_specs=land_specs + own_specs,
                                             out_specs=out_specs)
    params = pltpu.CompilerParams(dimension_semantics=("parallel",), vmem_limit_bytes=_vmem_limit(blk))
    return pl.pallas_call(body, name=name, out_shape=tuple(_sds(x.shape[1:], F32) for x in lands), grid_spec=grid_spec,
                          compiler_params=params)(me, *lands, *own)


def _reduce_layer(grads, l):
    n = lambda s: f"l{l}_{s}"
    views = [g.reshape(4, 2, g.shape[0] // N_DEV, g.shape[1]) for g in grads]
    recv = _pair_swap(views, name=n("reduce_pair"))
    core = lax.axis_index("c").astype(jnp.int32).reshape(1)
    chip_sum = _add_slabs(views, recv, core, name=n("reduce_pair_add"))
    from_chips = _chip_exchange(chip_sum, name=n("reduce_chips"))
    return _sum_chips(from_chips, name=n("reduce_chips_add"))


def _adamw(w, g, m, v, *, name):
    lead, (r, c) = w.shape[:-2], w.shape[-2:]
    tr = _pick(r, (512, 352, 288, 256, 192, 128, 64, 32, 16, 8))
    c1 = 1.0 / (1.0 - ADAM_B1 ** ADAM_STEP)
    c2 = 1.0 / (1.0 - ADAM_B2 ** ADAM_STEP)

    def body(w_ref, g_ref, m_ref, v_ref, d_ref, nm_ref, nv_ref):
        gv = g_ref[...]
        nm = ADAM_B1 * m_ref[...] + (1.0 - ADAM_B1) * gv
        nv = ADAM_B2 * v_ref[...] + (1.0 - ADAM_B2) * jnp.square(gv)
        d_ref[...] = -ADAM_LR * ((nm * c1) / (jnp.sqrt(nv * c2) + ADAM_EPS) + ADAM_WD * w_ref[...])
        nm_ref[...] = nm
        nv_ref[...] = nv

    if lead:
        blk = pl.BlockSpec((None, tr, c), lambda k, i: (k, i, 0))
        grid, sem = (lead[0], r // tr), ("parallel", "parallel")
    else:
        blk = pl.BlockSpec((tr, c), lambda i: (i, 0))
        grid, sem = (r // tr,), ("parallel",)
    out = _sds(w.shape, F32)
    return _pcall(body, name=name, out_shape=(out, out, out), grid=grid, in_specs=[blk] * 4,
                  out_specs=(blk, blk, blk), semantics=sem, block_bytes=7 * _nbytes((tr, c), F32))(w, g, m, v)


def _pack_flat(arrs, rows, cols=1024):
    flat = jnp.concatenate([a.reshape(-1).astype(F32) for a in arrs])
    pad = rows * cols - flat.shape[0]
    return jnp.pad(flat, (0, pad)).reshape(rows, cols)


def _unpack_flat(buf, shapes):
    flat = buf.reshape(-1)
    out, off = [], 0
    for shp in shapes:
        n = 1
        for s in shp:
            n *= s
        out.append(flat[off:off + n].reshape(shp))
        off += n
    return out


def _flat_rows(shapes, cols=1024):
    n = sum(functools.reduce(lambda a, b: a * b, shp, 1) for shp in shapes)
    rows = -(-n // cols)
    return -(-rows // 64) * 64


def _block_diag(w):
    eye = jnp.eye(N_HEADS, dtype=w.dtype)
    return (w[:, :, :, None, :] * eye[None, :, None, :, None]).reshape(w.shape[0], W_GRP, W_GRP)


def _diag_blocks(w):
    w5 = w.reshape(w.shape[0], N_HEADS, HEAD_DIM, N_HEADS, HEAD_DIM)
    return jnp.stack([w5[:, h, :, h, :] for h in range(N_HEADS)], axis=1)


def _stacked_params(w, lbs):
    tril = jnp.tril(jnp.ones((GMLP_CHUNK, GMLP_CHUNK), bool))
    row = lambda a: a.reshape(DEPTH, 1, -1)
    return dict(
        g1=row(w['norm1_g']), g2=row(w['norm2_g']), g3=row(w['norm3_g']),
        a_ln_g=row(w['a_ln_g']), a_ln_b=row(w['a_ln_b']),
        a_wcat=jnp.where(tril, w['a_ws'], 0.0).reshape(DEPTH, N_HEADS * GMLP_CHUNK, GMLP_CHUNK),
        a_bfull=jnp.repeat(jnp.swapaxes(w['a_bs'], 1, 2), HEAD_DIM, axis=2),
        b_cw=w['b_conv_w_full'], b_cb=row(w['b_conv_b']), b_wa=_block_diag(w['b_wa']), b_ba=row(w['b_ba']),
        b_wx=_block_diag(w['b_wx']), b_bx=row(w['b_bx']), b_lam=row(w['b_lam']),
        c_lb=row(lbs), c_ngf=row(jnp.tile(w['c_norm_g'], (1, N_HEADS))),
        d_wd=_block_diag(w['d_w']), d_scale=row(w['d_scale']),
        f_cw=w['ffn_conv_w_full'], f_cb=row(w['ffn_conv_b']),
    )


B_PRM = ('b_cw', 'b_cb', 'b_wa', 'b_ba', 'b_wx', 'b_bx', 'b_lam')


def _layer_fwd(x, p_bf, wb, sp, l):
    n = lambda s: f"l{l}_{s}"
    h, (z,) = _rms_matmul(x, sp['g1'], [wb['w_in']], nt=True, name=n("proj_in"))
    mix = _gmlp_fwd(z, sp['a_ln_g'], sp['a_ln_b'], sp['a_wcat'], sp['a_bfull'], name=n("gmlp"))
    mix, h0s = _rglru_fwd(z, [sp[k] for k in B_PRM], mix, name=n("rglru"))
    mix, sts = _hgrn_fwd(z, sp['c_lb'], sp['c_ngf'], mix, name=n("hgrn"))
    mix = _pool_fwd(z, sp['d_wd'], sp['d_scale'], mix, name=n("pool"))
    x1 = _matmul(mix, wb['w_out'], res=x, name=n("proj_out"))
    h2, hg, hv, a = _up_ffn_fwd(x1, sp['g2'], wb['w_up_g'], wb['w_up_v'], sp['f_cw'], sp['f_cb'], name=n("up_ffn"))
    x2 = _matmul(a, wb['w_down'], res=x1, name=n("down"))
    h3, (gl, x3) = _rms_matmul(x2, sp['g3'], [wb['w_pg']], ple=(p_bf, wb['w_pe']), name=n("ple"))
    saved = dict(x=x, h=h, z=z, h0s=h0s, sts=sts, mix=mix, x1=x1, h2=h2, hg=hg, hv=hv, a=a, x2=x2, h3=h3, gl=gl)
    return x3, saved


def _layer_bwd(dx3, sv, p_bf, wb, sp, l, mid=None):
    n = lambda s: f"l{l}_{s}_bwd"
    gb, gs = {}, {}
    dx2, dx2b, gs['norm3_g'], dpe, dgl = _ple_rms_bwd(dx3, sv['gl'], p_bf, wb['w_pe'], wb['w_pg'], sv['x2'], sp['g3'],
                                                      name=n("ple"))
    gb['w_pe'] = _matmul_tn(dpe, p_bf, name=n("ple_emb_w"))
    gb['w_pg'] = _matmul_tn(sv['h3'], dgl, name=n("ple_gate_w"))
    gb['w_down'] = _matmul_tn(sv['a'], dx2b, name=n("down_w"))
    dhg, dhv, gs['f_dwg'], gs['f_dwv'] = _ffn_bwd(sv['hg'], sv['hv'], dx2b, wb['w_down'], sp['f_cw'], sp['f_cb'],
                                                  name=n("ffn_gate"))
    gate_rows = _matmul_tn(dhg, sv['h2'], name=n("up_gate_w"), out_rows=2 * D_FF)
    gb['w_up'] = _matmul_tn(dhv, sv['h2'], name=n("up_val_w"), out_rows=2 * D_FF, row_off=D_FF, into=gate_rows)
    if mid is not None:
        sp = mid(gb, sp)
    dh2 = _matmul(dhg, wb['w_up_g'], name=n("up_gate_x"))
    dx1, dx1b, gs['norm2_g'] = _matmul_rms_bwd(dhv, wb['w_up_v'], sv['x1'], sp['g2'], dx2, res=dh2, name=n("up_val_x"))
    dmix = _matmul(dx1b, wb['w_out'], nt=True, name=n("proj_out_x"))
    gb['w_out'] = _matmul_tn(sv['mix'], dx1b, name=n("proj_out_w"))
    z = sv['z']
    dz, gs['a_ln_g'], gs['a_ln_b'], gs['a_wcat'], gs['a_bfull'] = _gmlp_bwd(
        z, dmix, sp['a_ln_g'], sp['a_ln_b'], sp['a_wcat'], sp['a_bfull'], name=n("gmlp"))
    dz, *dbp = _rglru_bwd(z, dmix, sv['h0s'], [sp[k] for k in B_PRM], dz, name=n("rglru"))
    gs.update(zip(B_PRM, dbp))
    dz, gs['c_lb'], gs['c_ngf'] = _hgrn_bwd(z, dmix, sv['sts'], sp['c_lb'], sp['c_ngf'], dz, name=n("hgrn"))
    dz, gs['d_wd'], gs['d_scale'] = _pool_bwd(z, dmix, sp['d_wd'], sp['d_scale'], dz, name=n("pool"))
    gb['w_in'] = _matmul_tn(dz, sv['h'], name=n("proj_in_w"))
    dx0, _, gs['norm1_g'] = _matmul_rms_bwd(dz, wb['w_in'], sv['x'], sp['g1'], dx1, name=n("proj_in_x"))
    return dx0, gb, gs


SMALL_NAMES = [nm for nm in WEIGHT_NAMES if nm not in BIG_NAMES]
COL_SHARDED = ('w_in', 'w_up', 'w_pe')


def _comm_shards(w):
    return [(jnp.swapaxes(w[nm], 1, 2) if nm in COL_SHARDED else w[nm]).astype(BF16) for nm, _, _ in BIG_COMM]


def _full_weights(gathered):
    out = {nm: g.reshape(N_DEV * r, c) for g, (nm, r, c) in zip(gathered, BIG_COMM)}
    halves = out.pop('w_up').reshape(2, D_FF, D_MODEL)
    out['w_up_g'], out['w_up_v'] = _Sel(halves, 0), _Sel(halves, 1)
    return out


def _small_grads(raw):
    nl = len(raw)
    st = {k: jnp.stack([r[k] for r in raw]) for k in raw[0]}
    tril = jnp.tril(jnp.ones((GMLP_CHUNK, GMLP_CHUNK), bool))
    vec = lambda a: a.reshape(nl, -1)
    out = {nm: vec(st[k]) for nm, k in (('norm1_g', 'norm1_g'), ('norm2_g', 'norm2_g'), ('norm3_g', 'norm3_g'),
                                        ('a_ln_g', 'a_ln_g'), ('a_ln_b', 'a_ln_b'), ('b_conv_b', 'b_cb'),
                                        ('b_ba', 'b_ba'), ('b_bx', 'b_bx'), ('b_lam', 'b_lam'), ('c_lb', 'c_lb'),
                                        ('d_scale', 'd_scale'))}
    out['a_ws'] = jnp.where(tril, st['a_wcat'].reshape(nl, N_HEADS, GMLP_CHUNK, GMLP_CHUNK), 0.0)
    out['a_bs'] = jnp.swapaxes(st['a_bfull'].reshape(nl, GMLP_CHUNK, N_HEADS, HEAD_DIM).sum(-1), 1, 2)
    out['b_conv_w'] = st['b_cw']
    out['b_wa'], out['b_wx'], out['d_w'] = _diag_blocks(st['b_wa']), _diag_blocks(st['b_wx']), _diag_blocks(st['d_wd'])
    out['c_norm_g'] = st['c_ngf'].reshape(nl, N_HEADS, HEAD_DIM).sum(1)
    out['ffn_conv_w'] = jnp.concatenate([st['f_dwg'][:, 0:3], st['f_dwv'][:, 0:3]], axis=2)
    out['ffn_conv_b'] = jnp.concatenate([st['f_dwg'][:, 3], st['f_dwv'][:, 3]], axis=1)
    return out


def _step(w, m, v, x, p, target):
    s = x.shape[1]
    dev = 4 * lax.axis_index("x") + 2 * lax.axis_index("y") + lax.axis_index("c")
    xs = x.reshape(s, D_MODEL)

    shards = _comm_shards(w)
    conv_shapes = [w['b_conv_w'].shape, w['ffn_conv_w'].shape]
    conv_rows = _flat_rows(conv_shapes)
    conv_all = _all_gather(_pack_flat([w['b_conv_w'], w['ffn_conv_w']], conv_rows), name="gather_conv_weights")
    parts = [_unpack_flat(conv_all[d], conv_shapes) for d in range(N_DEV)]
    wf = dict(w)
    wf['b_conv_w_full'] = jnp.concatenate([pt[0] for pt in parts], axis=-1)
    wf['ffn_conv_w_full'] = jnp.concatenate([pt[1] for pt in parts], axis=-1)
    lbs = _lbs_fwd(w['c_lb'], name="hgrn_bounds")

    stacked = _stacked_params(wf, lbs)
    p_all = p.reshape(DEPTH, s, PLE_DIM).astype(BF16)
    xl, saved, wbs, sps = xs, [], [], []
    gathered = _gather_layer(shards, 0, name="l0_gather_weights")
    for l in range(DEPTH):
        sp = {k: _Sel(a, l) for k, a in stacked.items()}
        if l + 1 < DEPTH:
            own = [x[l + 1] for x in shards]
            after = [conv_all, *gathered] if l == 0 else [xl]
            lands = _place_own(own, after, name=f"l{l + 1}_gather_place")
            started = _exchange_start(own, lands, name=f"l{l + 1}_gather_start")
            sp['g1'] = stacked['g1'][l] + started[-1][0, 0]
        wb = _full_weights(gathered)
        p_bf = p_all[l]
        xl, sv = _layer_fwd(xl, p_bf, wb, sp, l)
        if l + 1 < DEPTH:
            gathered = _exchange_wait(started, xl, name=f"l{l + 1}_gather_wait")[1]
        saved.append((sv, p_bf))
        wbs.append(wb)
        sps.append(sp)
    loss_part, dx, dfinal = _loss_head(xl, w['final_g'].reshape(1, D_MODEL), target.reshape(s, D_MODEL), name="loss_head")
    loss = lax.psum(loss_part[0, 0], ("x", "y", "c"))

    dev1 = dev.astype(jnp.int32).reshape(1)
    names = [nm for nm, _, _ in BIG_COMM]

    def start_reduce(grads, name):
        views = [g.reshape(N_DEV, g.shape[0] // N_DEV, g.shape[1]) for g in grads]
        return _exchange_start(views, [lax.empty(g.shape, g.dtype) for g in views], name=name, per_peer=True)

    def finish_reduce(started, after, lname):
        own, lands = _exchange_wait(started, after, name=f"{lname}_reduce_wait")
        return _sum_devices(lands, own, dev1, name=f"{lname}_reduce_sum")

    reduced, small = [None] * DEPTH, [None] * DEPTH
    pending = None
    for l in range(DEPTH - 1, 0, -1):
        sv, p_bf = saved[l]
        sp = sps[l]
        if pending is not None:
            sp = dict(sp, g3=stacked['g3'][l] + pending[-1][0, 0])
        dx, gb, small[l] = _layer_bwd(dx, sv, p_bf, wbs[l], sp, l)
        if pending is not None:
            reduced[l + 1] = finish_reduce(pending, dx, f"l{l + 1}")
        pending = start_reduce([gb[nm] for nm in names], f"l{l}_reduce_start")
    early = ('w_up', 'w_down', 'w_pe', 'w_pg')
    mid_started = []

    def mid(gb, sp):
        mid_started.append(start_reduce([gb[nm] for nm in early], "l0_reduce_start"))
        return dict(sp, g2=stacked['g2'][0] + mid_started[0][-1][0, 0])

    upper_names = [nm for nm in SMALL_NAMES if nm != 'final_g']
    low_names = upper_names + ['final_g']
    upper = _small_grads(small[1:])
    upper_shapes = [upper[nm].shape for nm in upper_names]
    upper_packed = [_pack_flat([upper[nm] for nm in upper_names], _flat_rows(upper_shapes))]
    upper_started = _exchange_start(upper_packed, _place_own(upper_packed, [], name="upper_small_grads_place"),
                                    name="upper_small_grads_start")

    sv, p_bf = saved[0]
    g3 = stacked['g3'][0] + pending[-1][0, 0] + upper_started[-1][0, 0]
    dx, gb, small[0] = _layer_bwd(dx, sv, p_bf, wbs[0], dict(sps[0], g3=g3), 0, mid=mid)
    reduced[1] = finish_reduce(pending, dx, "l1")
    late = dict(zip(('w_in', 'w_out'), _reduce_layer([gb['w_in'], gb['w_out']], 0)))
    late.update(zip(early, finish_reduce(mid_started[0], late['w_in'], "l0")))
    reduced[0] = [late[nm] for nm in names]
    grad_x = dx.reshape(1, s, D_MODEL)
    low = _small_grads(small[:1])
    low['final_g'] = dfinal.reshape(D_MODEL)
    low_shapes = [low[nm].shape for nm in low_names]
    low_all = _all_gather(_pack_flat([low[nm] for nm in low_names], _flat_rows(low_shapes)), name="gather_small_grads")
    low_sum = dict(zip(low_names, _unpack_flat(_sum_slots(low_all, name="sum_small_grads"), low_shapes)))
    upper_all = _exchange_wait(upper_started, low_all, name="upper_small_grads_wait")[1][0]
    upper_sum = dict(zip(upper_names, _unpack_flat(_sum_slots(upper_all, name="sum_upper_small_grads"), upper_shapes)))
    gsmall = {nm: jnp.concatenate([low_sum[nm], upper_sum[nm]], axis=0) for nm in upper_names}
    gsmall['c_lb'] = _lbs_bwd(w['c_lb'], gsmall['c_lb'], name="hgrn_bounds_bwd")
    gsmall['final_g'] = low_sum['final_g']
    for nm in ('b_conv_w', 'ffn_conv_w'):
        width = w[nm].shape[-1]
        gsmall[nm] = lax.dynamic_slice_in_dim(gsmall[nm], dev * width, width, axis=2)

    grads, delta, new_m, new_v = {}, {}, {}, {}
    for a, (nm, _, _) in enumerate(BIG_COMM):
        t = (lambda x: jnp.swapaxes(x, 1, 2)) if nm in COL_SHARDED else (lambda x: x)
        g = jnp.stack([reduced[l][a] for l in range(DEPTH)])
        d, nm_, nv_ = _adamw(t(w[nm]), g, t(m[nm]), t(v[nm]), name=f"adamw_{nm}")
        grads[nm], delta[nm], new_m[nm], new_v[nm] = t(g), t(d), t(nm_), t(nv_)

    shapes = [w[nm].shape for nm in SMALL_NAMES]
    rows = _flat_rows(shapes)
    pk = lambda t: _pack_flat([t[nm] for nm in SMALL_NAMES], rows)
    d, nm_, nv_ = _adamw(pk(w), pk(gsmall), pk(m), pk(v), name="adamw_small")
    for nm, dd, mm_, vv_ in zip(SMALL_NAMES, _unpack_flat(d, shapes), _unpack_flat(nm_, shapes), _unpack_flat(nv_, shapes)):
        grads[nm], delta[nm], new_m[nm], new_v[nm] = gsmall[nm], dd, mm_, vv_

    return (loss, grad_x, *[grads[nm] for nm in WEIGHT_NAMES], *[delta[nm] for nm in WEIGHT_NAMES],
            *[new_m[nm] for nm in WEIGHT_NAMES], *[new_v[nm] for nm in WEIGHT_NAMES])


def kernel(x, p, norm1_g, w_in, a_ln_g, a_ln_b, a_ws, a_bs, b_conv_w, b_conv_b, b_wa, b_ba, b_wx, b_bx, b_lam, c_lb, c_norm_g, d_w, d_scale, w_out, norm2_g, w_up, ffn_conv_w, ffn_conv_b, w_down, norm3_g, w_pe, w_pg, final_g, loss_target, m_norm1_g, m_w_in, m_a_ln_g, m_a_ln_b, m_a_ws, m_a_bs, m_b_conv_w, m_b_conv_b, m_b_wa, m_b_ba, m_b_wx, m_b_bx, m_b_lam, m_c_lb, m_c_norm_g, m_d_w, m_d_scale, m_w_out, m_norm2_g, m_w_up, m_ffn_conv_w, m_ffn_conv_b, m_w_down, m_norm3_g, m_w_pe, m_w_pg, m_final_g, v_norm1_g, v_w_in, v_a_ln_g, v_a_ln_b, v_a_ws, v_a_bs, v_b_conv_w, v_b_conv_b, v_b_wa, v_b_ba, v_b_wx, v_b_bx, v_b_lam, v_c_lb, v_c_norm_g, v_d_w, v_d_scale, v_w_out, v_norm2_g, v_w_up, v_ffn_conv_w, v_ffn_conv_b, v_w_down, v_norm3_g, v_w_pe, v_w_pg, v_final_g):
    w = dict(norm1_g=norm1_g, w_in=w_in, a_ln_g=a_ln_g, a_ln_b=a_ln_b, a_ws=a_ws, a_bs=a_bs, b_conv_w=b_conv_w, b_conv_b=b_conv_b, b_wa=b_wa, b_ba=b_ba, b_wx=b_wx, b_bx=b_bx, b_lam=b_lam, c_lb=c_lb, c_norm_g=c_norm_g, d_w=d_w, d_scale=d_scale, w_out=w_out, norm2_g=norm2_g, w_up=w_up, ffn_conv_w=ffn_conv_w, ffn_conv_b=ffn_conv_b, w_down=w_down, norm3_g=norm3_g, w_pe=w_pe, w_pg=w_pg, final_g=final_g)
    m = dict(norm1_g=m_norm1_g, w_in=m_w_in, a_ln_g=m_a_ln_g, a_ln_b=m_a_ln_b, a_ws=m_a_ws, a_bs=m_a_bs, b_conv_w=m_b_conv_w, b_conv_b=m_b_conv_b, b_wa=m_b_wa, b_ba=m_b_ba, b_wx=m_b_wx, b_bx=m_b_bx, b_lam=m_b_lam, c_lb=m_c_lb, c_norm_g=m_c_norm_g, d_w=m_d_w, d_scale=m_d_scale, w_out=m_w_out, norm2_g=m_norm2_g, w_up=m_w_up, ffn_conv_w=m_ffn_conv_w, ffn_conv_b=m_ffn_conv_b, w_down=m_w_down, norm3_g=m_norm3_g, w_pe=m_w_pe, w_pg=m_w_pg, final_g=m_final_g)
    v = dict(norm1_g=v_norm1_g, w_in=v_w_in, a_ln_g=v_a_ln_g, a_ln_b=v_a_ln_b, a_ws=v_a_ws, a_bs=v_a_bs, b_conv_w=v_b_conv_w, b_conv_b=v_b_conv_b, b_wa=v_b_wa, b_ba=v_b_ba, b_wx=v_b_wx, b_bx=v_b_bx, b_lam=v_b_lam, c_lb=v_c_lb, c_norm_g=v_c_norm_g, d_w=v_d_w, d_scale=v_d_scale, w_out=v_w_out, norm2_g=v_norm2_g, w_up=v_w_up, ffn_conv_w=v_ffn_conv_w, ffn_conv_b=v_ffn_conv_b, w_down=v_w_down, norm3_g=v_norm3_g, w_pe=v_w_pe, w_pg=v_w_pg, final_g=v_final_g)
    return _step(w, m, v, x, p, loss_target)
```

```python
import functools

import jax
import jax.numpy as jnp
from jax import lax
from jax.experimental import pallas as pl
from jax.experimental.pallas import tpu as pltpu

F32 = jnp.float32
BF16 = jnp.bfloat16
MESH = pl.DeviceIdType.MESH

D_MODEL = 1024
DEPTH = 4
PLE_DIM = 256
W_GRP = 256
N_HEADS = 4
HEAD_DIM = 64
GMLP_CHUNK = 128
RGLRU_C = 8.0
HGRN_CHUNK = 64
HGRN_SUB = 32
HGRN_STEP_CHUNKS = 8
POOL_WINDOWS = (2, 4, 8, 16)
D_FF = 2816
D_PROJ = 2304
EPS = 1e-6
ADAM_LR = 0.001
ADAM_B1 = 0.9
ADAM_B2 = 0.999
ADAM_EPS = 1e-08
ADAM_WD = 0.01
ADAM_STEP = 10

N_DEV = 8
MIB = 2 ** 20
V7X_VMEM_BYTES = 64 * MIB
HGRN_EXP_CLAMP = 60.0

WEIGHT_NAMES = ['norm1_g', 'w_in', 'a_ln_g', 'a_ln_b', 'a_ws', 'a_bs', 'b_conv_w', 'b_conv_b', 'b_wa', 'b_ba', 'b_wx',
                'b_bx', 'b_lam', 'c_lb', 'c_norm_g', 'd_w', 'd_scale', 'w_out', 'norm2_g', 'w_up', 'ffn_conv_w',
                'ffn_conv_b', 'w_down', 'norm3_g', 'w_pe', 'w_pg', 'final_g']
BIG_NAMES = ('w_in', 'w_out', 'w_up', 'w_down', 'w_pe', 'w_pg')


def _vmem_limit(block_bytes):
    want = 2 * block_bytes + 24 * MIB
    return int(min(max(want, 32 * MIB), V7X_VMEM_BYTES - 8 * MIB))


def _in_hbm(x):
    return pltpu.with_memory_space_constraint(x, pltpu.HBM)


def _out_hbm(s):
    return pltpu.HBM(s.shape, s.dtype)


def _pcall(body, *, name, out_shape, grid=None, in_specs=None, out_specs=None, scratch_shapes=(),
           semantics=None, block_bytes=0, aliases=None, pin=True):
    kw = {} if aliases is None else {"input_output_aliases": aliases}
    if pin:
        out_shape = tuple(_out_hbm(s) for s in out_shape) if isinstance(out_shape, (tuple, list)) else _out_hbm(out_shape)
    if grid is not None:
        kw["grid"] = grid
    if in_specs is not None:
        kw["in_specs"] = in_specs
    if out_specs is not None:
        kw["out_specs"] = out_specs
    params = pltpu.CompilerParams(dimension_semantics=semantics, vmem_limit_bytes=_vmem_limit(block_bytes))
    call = pl.pallas_call(body, name=name, out_shape=out_shape, scratch_shapes=list(scratch_shapes),
                          compiler_params=params, **kw)
    return (lambda *args: call(*[_in_hbm(a) for a in args])) if pin else call


def _pick(n, cands):
    for c in cands:
        if n % c == 0:
            return c
    return n


def _nbytes(shape, dtype):
    n = 1
    for s in shape:
        n *= s
    return n * jnp.dtype(dtype).itemsize


def _sds(shape, dtype):
    return jax.ShapeDtypeStruct(tuple(shape), dtype)


class _Sel:
    def __init__(self, arr, *idx):
        self.arr, self.idx = arr, tuple(idx)
        self.shape = arr.shape[len(idx):]
        self.ndim = len(self.shape)
        self.dtype = arr.dtype


def _arr(a):
    return a.arr if isinstance(a, _Sel) else a


def _spec(a, block=None, index=None):
    block = tuple(a.shape) if block is None else tuple(block)
    index = (lambda *g: (0,) * len(block)) if index is None else index
    if isinstance(a, _Sel):
        lead = a.idx
        return pl.BlockSpec((None,) * len(lead) + block, lambda *g: lead + tuple(index(*g)))
    return pl.BlockSpec(block, lambda *g: tuple(index(*g)))


def _ospec(a):
    return pl.BlockSpec(tuple(a.shape), lambda *g: (0,) * a.ndim)


def _rows_of(shape):
    return lax.broadcasted_iota(jnp.int32, shape, 0)


def _lanes_of(shape):
    return lax.broadcasted_iota(jnp.int32, shape, 1)


def _sdn(x, k, fill):
    n = x.shape[0]
    return jnp.where(_rows_of(x.shape) >= k, pltpu.roll(x, k % n, 0), fill)


def _sup(x, k, fill):
    n = x.shape[0]
    return jnp.where(_rows_of(x.shape) < n - k, pltpu.roll(x, (n - k) % n, 0), fill)


@functools.partial(jax.custom_vjp, nondiff_argnums=(1,))
def _shift_dn(x, k):
    return pltpu.roll(x, k, 0)


def _shift_dn_fwd(x, k):
    return pltpu.roll(x, k, 0), None


def _shift_dn_bwd(k, _, g):
    return (pltpu.roll(g, g.shape[0] - k, 0),)


_shift_dn.defvjp(_shift_dn_fwd, _shift_dn_bwd)


SUBLANES = 8


def _lin_scan_impl(a, b, h0):
    n = a.shape[0]
    pos = _rows_of(a.shape) % SUBLANES
    aa, bb = a, b
    k = 1
    while k < SUBLANES:
        keep = pos >= k
        bb = bb + jnp.where(keep, aa * pltpu.roll(bb, k, 0), 0.0)
        aa = aa * jnp.where(keep, pltpu.roll(aa, k, 0), 1.0)
        k *= 2
    out, carry = [], h0
    for r in range(n // SUBLANES):
        rows = slice(r * SUBLANES, (r + 1) * SUBLANES)
        hr = bb[rows] + aa[rows] * carry
        out.append(hr)
        carry = hr[SUBLANES - 1:]
    return jnp.concatenate(out, axis=0)


@jax.custom_vjp
def _lin_scan(a, b, h0):
    return _lin_scan_impl(a, b, h0)


def _lin_scan_fwd(a, b, h0):
    h = _lin_scan_impl(a, b, h0)
    return h, (a, h, h0)


def _lin_scan_bwd(res, g):
    a, h, h0 = res
    n = a.shape[0]
    pos = _rows_of(a.shape) % SUBLANES
    cc, gg = _sup(a, 1, 0.0), g
    k = 1
    while k < SUBLANES:
        keep = pos < SUBLANES - k
        gg = gg + jnp.where(keep, cc * pltpu.roll(gg, n - k, 0), 0.0)
        cc = cc * jnp.where(keep, pltpu.roll(cc, n - k, 0), 1.0)
        k *= 2
    out, carry = [], jnp.zeros_like(h0)
    for r in range(n // SUBLANES - 1, -1, -1):
        rows = slice(r * SUBLANES, (r + 1) * SUBLANES)
        gr = gg[rows] + cc[rows] * carry
        out.append(gr)
        carry = gr[:1]
    gg = jnp.concatenate(out[::-1], axis=0)
    first = _rows_of(a.shape) == 0
    hprev = jnp.where(first, h0, _sdn(h, 1, 0.0))
    dh0 = jnp.sum(jnp.where(first, a * gg, 0.0), axis=0, keepdims=True)
    return gg * hprev, gg, dh0


_lin_scan.defvjp(_lin_scan_fwd, _lin_scan_bwd)


def _cumsum_sub_impl(x):
    pos = _rows_of(x.shape) % HGRN_SUB
    k = 1
    while k < HGRN_SUB:
        x = x + jnp.where(pos >= k, pltpu.roll(x, k, 0), 0.0)
        k *= 2
    return x


@jax.custom_vjp
def _cumsum_sub(x):
    return _cumsum_sub_impl(x)


def _cumsum_sub_fwd(x):
    return _cumsum_sub_impl(x), None


def _cumsum_sub_bwd(_, g):
    n = g.shape[0]
    pos = _rows_of(g.shape) % HGRN_SUB
    k = 1
    while k < HGRN_SUB:
        g = g + jnp.where(pos < HGRN_SUB - k, pltpu.roll(g, n - k, 0), 0.0)
        k *= 2
    return (g,)


_cumsum_sub.defvjp(_cumsum_sub_fwd, _cumsum_sub_bwd)


def _dot(a, b, ca, cb):
    return lax.dot_general(a.astype(BF16), b.astype(BF16), (((ca,), (cb,)), ((), ())), preferred_element_type=F32)


@jax.custom_vjp
def _mm(a, b):
    return _dot(a, b, 1, 0)


def _mm_fwd(a, b):
    return _dot(a, b, 1, 0), (a, b)


def _mm_bwd(res, g):
    a, b = res
    return _dot(g, b, 1, 1), _dot(a, g, 0, 0)


_mm.defvjp(_mm_fwd, _mm_bwd)


@jax.custom_vjp
def _mm_nt(a, b):
    return _dot(a, b, 1, 1)


def _mm_nt_fwd(a, b):
    return _dot(a, b, 1, 1), (a, b)


def _mm_nt_bwd(res, g):
    a, b = res
    return _dot(g, b, 1, 0), _dot(g, a, 0, 0)


_mm_nt.defvjp(_mm_nt_fwd, _mm_nt_bwd)


@jax.custom_vjp
def _mm_tn(a, b):
    return _dot(a, b, 0, 0)


def _mm_tn_fwd(a, b):
    return _dot(a, b, 0, 0), (a, b)


def _mm_tn_bwd(res, g):
    a, b = res
    return _dot(b, g, 1, 1), _dot(a, g, 1, 0)


_mm_tn.defvjp(_mm_tn_fwd, _mm_tn_bwd)


def _head_mask(shape, h):
    return (_lanes_of(shape) // HEAD_DIM) == h


def _stack_heads(x):
    return jnp.concatenate([jnp.where(_head_mask(x.shape, h), x, 0.0) for h in range(N_HEADS)], axis=0)


def _unstack_heads(p):
    r = p.shape[0] // N_HEADS
    out = None
    for h in range(N_HEADS):
        blk = p[h * r:(h + 1) * r]
        term = jnp.where(_head_mask(blk.shape, h), blk, 0.0)
        out = term if out is None else out + term
    return out


def _segmean_impl(x):
    n = x.shape[1]
    same = (lax.broadcasted_iota(jnp.int32, (n, n), 0) // HEAD_DIM) == (lax.broadcasted_iota(jnp.int32, (n, n), 1) // HEAD_DIM)
    m = jnp.where(same, 1.0 / HEAD_DIM, 0.0).astype(BF16)
    hi = x.astype(BF16)
    lo = (x - hi.astype(F32)).astype(BF16)
    dn = (((1,), (0,)), ((), ()))
    return (lax.dot_general(hi, m, dn, preferred_element_type=F32)
            + lax.dot_general(lo, m, dn, preferred_element_type=F32))


@jax.custom_vjp
def _segmean(x):
    return _segmean_impl(x)


def _segmean_fwd(x):
    return _segmean_impl(x), None


def _segmean_bwd(_, g):
    return (_segmean_impl(g),)


_segmean.defvjp(_segmean_fwd, _segmean_bwd)


GELU_C = 0.7978845608028654
GELU_A = 0.044715


@jax.custom_vjp
def _gelu(x):
    return 0.5 * x * (1.0 + jnp.tanh(GELU_C * x * (1.0 + GELU_A * (x * x))))


def _gelu_fwd(x):
    x2 = x * x
    t = jnp.tanh(GELU_C * x * (1.0 + GELU_A * x2))
    return 0.5 * x * (1.0 + t), (x, x2, t)


def _gelu_bwd(res, g):
    x, x2, t = res
    half = 0.5 * (1.0 + t)
    return (g * (half + (0.5 * GELU_C) * x * (1.0 - t * t) * (1.0 + (3.0 * GELU_A) * x2)),)


_gelu.defvjp(_gelu_fwd, _gelu_bwd)


def _log1p(u):
    w = 1.0 + u
    return jnp.where(w == 1.0, u, jnp.log(w) * (u / (w - 1.0)))


def _softplus(y):
    return jnp.maximum(y, 0.0) + _log1p(jnp.exp(-jnp.abs(y)))


def _rms(x, g):
    return x * lax.rsqrt(jnp.mean(x * x, axis=-1, keepdims=True) + EPS) * g


def _gmlp_chunk(zu, zv, ln_g, ln_b, wcat, bfull):
    u = _gelu(zu)
    v = _gelu(zv)
    mu = jnp.mean(v, axis=-1, keepdims=True)
    var = jnp.mean(jnp.square(v - mu), axis=-1, keepdims=True)
    vn = (v - mu) * lax.rsqrt(var + EPS) * ln_g + ln_b
    sv = _unstack_heads(_mm(wcat, vn)) + bfull
    return u * sv


def _rglru_tile(xb_ext, gb, h0, cw, cb, wa, ba, wx, bx, lam):
    xc = (cb + cw[0:1] * _shift_dn(xb_ext, 3) + cw[1:2] * _shift_dn(xb_ext, 2) + cw[2:3] * _shift_dn(xb_ext, 1)
          + cw[3:4] * xb_ext)[8:]
    r = jax.nn.sigmoid(_mm(xc, wa) + ba)
    i = jax.nn.sigmoid(_mm(xc, wx) + bx)
    log_a = (-RGLRU_C) * r * _softplus(-lam)
    a = jnp.exp(log_a)
    mult = jnp.sqrt(-jnp.tanh(log_a) * (a * a + 1.0))
    h = _lin_scan(a, mult * (i * xc), h0)
    y = h * _gelu(gb)
    h_last = jnp.sum(jnp.where(_rows_of(h.shape) == h.shape[0] - 1, h, 0.0), axis=0, keepdims=True)
    return y, h_last


def _pool_tile(xd_ext, inv, wd, scale):
    s1 = xd_ext + _shift_dn(xd_ext, 1)
    s2 = s1 + _shift_dn(s1, 2)
    s3 = s2 + _shift_dn(s2, 4)
    s4 = s3 + _shift_dn(s3, 8)
    grp = _lanes_of(xd_ext.shape) // HEAD_DIM
    win = jnp.where(grp == 0, s1, jnp.where(grp == 1, s2, jnp.where(grp == 2, s3, s4)))
    pooled = win[16:] * inv - xd_ext[16:]
    return _mm(pooled, wd) * scale


def _hgrn_chunk(q, f, i, g, st, lb, ngf):
    n = q.shape[0]
    nsub = n // HGRN_SUB
    qs = jax.nn.silu(q)
    fg = lb + (1.0 - lb) * jax.nn.sigmoid(f)
    lf = jnp.log(fg)
    k = 1.0 - fg
    bl = _cumsum_sub(lf)
    row = _rows_of(q.shape)
    blk = row // HGRN_SUB
    betas = [jnp.zeros_like(lb)]
    for s in range(nsub):
        tot = jnp.sum(jnp.where(row == s * HGRN_SUB + HGRN_SUB - 1, bl, 0.0), axis=0, keepdims=True)
        betas.append(betas[-1] + tot)
    b_end = betas[nsub]
    beta_full = jnp.zeros_like(q)
    for s in range(1, nsub):
        beta_full = jnp.where(blk == s, betas[s], beta_full)
    qh = qs * jnp.exp(bl)
    qt = qh * jnp.exp(beta_full)
    b_all = beta_full + bl
    kt = k * jnp.exp(b_end - b_all)
    outs = []
    for s in range(nsub):
        kh = k * jnp.exp(jnp.minimum(betas[s] - b_all, HGRN_EXP_CLAMP))
        qstk = _stack_heads(qh[s * HGRN_SUB:(s + 1) * HGRN_SUB])
        att = _mm_nt(qstk, kh)
        ar = _rows_of(att.shape) % HGRN_SUB + s * HGRN_SUB
        att = jnp.where(_lanes_of(att.shape) <= ar, att, 0.0)
        outs.append(_unstack_heads(_mm(att, i)))
    o = jnp.concatenate(outs, axis=0) + _mm_nt(qt, st)
    same = (_rows_of(st.shape) // HEAD_DIM) == (_lanes_of(st.shape) // HEAD_DIM)
    st_new = st * jnp.exp(b_end) + jnp.where(same, _mm_tn(i, kt), 0.0)
    on = o * lax.rsqrt(_segmean(o * o) + EPS) * ngf
    return on * jax.nn.silu(g), st_new


def _ffn_tile(eg, ev, wg, bg, wv, bv):
    gt = (bg + wg[0:1] * _shift_dn(eg, 2) + wg[1:2] * _shift_dn(eg, 1) + wg[2:3] * eg)[8:]
    val = (bv + wv[0:1] * _shift_dn(ev, 2) + wv[1:2] * _shift_dn(ev, 1) + wv[2:3] * ev)[8:]
    return _gelu(gt) * val


MXU_WIDTH = 256
MATMUL_BLOCK_BUDGET = 18 * MIB


def _matmul_tiles(m, k, n, a_dtype, b_dtype, out_dtype, has_res):
    best = None
    for tm in (2048, 1024, 512, 256):
        if m % tm:
            continue
        for tn in (1024, 768, 1408, 512, 256, 128):
            if n % tn:
                continue
            blk = (_nbytes((tm, k), a_dtype) + _nbytes((k, tn), b_dtype) + _nbytes((tm, tn), out_dtype)
                   + (_nbytes((tm, tn), F32) if has_res else 0))
            if blk > MATMUL_BLOCK_BUDGET:
                continue
            waste = -(-tn // MXU_WIDTH) * MXU_WIDTH / tn
            cost = (m // tm) * (n // tn) + 64 * (waste - 1.0) + 1e-3 * (n // tn) + blk / 2 ** 40
            if best is None or cost < best[0]:
                best = (cost, tm, tn, blk)
    assert best is not None, (m, k, n)
    return best[1:]


def _matmul(a, b, *, name, nt=False, res=None, out_dtype=F32):
    m, k = a.shape
    n = b.shape[0] if nt else b.shape[1]
    tm, tn, blk = _matmul_tiles(m, k, n, a.dtype, b.dtype, out_dtype, res is not None)
    dims = (((1,), (1,)), ((), ())) if nt else (((1,), (0,)), ((), ()))

    def body(*refs):
        if res is None:
            a_ref, b_ref, o_ref = refs
        else:
            a_ref, b_ref, r_ref, o_ref = refs
        acc = lax.dot_general(a_ref[...], b_ref[...], dims, preferred_element_type=F32)
        if res is not None:
            acc = acc + r_ref[...]
        o_ref[...] = acc.astype(out_dtype)

    in_specs = [pl.BlockSpec((tm, k), lambda i, j: (i, 0)),
                _spec(b, (tn, k), lambda i, j: (j, 0)) if nt else _spec(b, (k, tn), lambda i, j: (0, j))]
    args = [a, _arr(b)]
    if res is not None:
        in_specs.append(pl.BlockSpec((tm, tn), lambda i, j: (i, j)))
        args.append(res)
    return _pcall(body, name=name, out_shape=_sds((m, n), out_dtype), grid=(m // tm, n // tn), in_specs=in_specs,
                  out_specs=pl.BlockSpec((tm, tn), lambda i, j: (i, j)), semantics=("parallel", "parallel"),
                  block_bytes=blk + _nbytes((tm, tn), F32))(*args)


def _matmul_rms_bwd(a, b, x, g, dres, *, name, nt=False, res=None, second=None):
    m, k = a.shape
    n = b.shape[0] if nt else b.shape[1]
    tm = _pick(m, (512, 256) if second is None else (256,))
    dims = (((1,), (1,)), ((), ())) if nt else (((1,), (0,)), ((), ()))

    def body(*refs):
        a_ref, b_ref, x_ref, g_ref, dr_ref = refs[:5]
        dx_ref, dxb_ref, dg_ref = refs[-3:]
        dh = lax.dot_general(a_ref[...], b_ref[...], dims, preferred_element_type=F32)
        if second is not None:
            dh = dh + lax.dot_general(refs[5][...], refs[6][...], dims, preferred_element_type=F32)
        if res is not None:
            dh = dh + refs[5][...]
        _, vjp = jax.vjp(_rms, x_ref[...], g_ref[...])
        dxn, dg = vjp(dh)
        dx = dr_ref[...] + dxn
        dx_ref[...] = dx
        dxb_ref[...] = dx.astype(BF16)
        _acc_out(dg_ref, dg, pl.program_id(0) == 0)

    row = pl.BlockSpec((tm, n), lambda i: (i, 0))
    vec = pl.BlockSpec((1, n), lambda i: (0, 0))
    in_specs = [pl.BlockSpec((tm, k), lambda i: (i, 0)),
                _spec(b, (n, k), lambda i: (0, 0)) if nt else _spec(b, (k, n), lambda i: (0, 0)), row, _spec(g), row]
    args = [a, _arr(b), x, _arr(g), dres]
    nmat = 1
    if second is not None:
        assert res is None
        in_specs += [in_specs[0], _spec(second[1], (n, k), lambda i: (0, 0)) if nt else _spec(second[1], (k, n), lambda i: (0, 0))]
        args += [second[0], _arr(second[1])]
        nmat = 2
    if res is not None:
        in_specs.append(row)
        args.append(res)
    blk = nmat * (_nbytes((tm, k), a.dtype) + _nbytes((k, n), b.dtype)) + 6 * _nbytes((tm, n), F32)
    return _pcall(body, name=name, out_shape=(_sds((m, n), F32), _sds((m, n), BF16), _sds((1, n), F32)), grid=(m // tm,),
                  in_specs=in_specs, out_specs=(row, row, vec), semantics=("arbitrary",), block_bytes=blk)(*args)


def _ple_rms_bwd(dx3, gl, p, w_pe, w_pg, x, g, *, name):
    m, n = dx3.shape
    tm = _pick(m, (512, 256))

    def body(d3_ref, gl_ref, p_ref, wpe_ref, w_ref, x_ref, g_ref, dx_ref, dxb_ref, dg_ref, dpe_ref, dgl_ref):
        gate = jax.nn.sigmoid(gl_ref[...])
        d3 = d3_ref[...]
        pe = lax.dot_general(p_ref[...], wpe_ref[...], (((1,), (1,)), ((), ())), preferred_element_type=F32)
        dpe_ref[...] = (d3 * gate).astype(BF16)
        dgl = (d3 * pe * gate * (1.0 - gate)).astype(BF16)
        dgl_ref[...] = dgl
        dh = lax.dot_general(dgl, w_ref[...], (((1,), (1,)), ((), ())), preferred_element_type=F32)
        _, vjp = jax.vjp(_rms, x_ref[...], g_ref[...])
        dxn, dg = vjp(dh)
        dx = d3 + dxn
        dx_ref[...] = dx
        dxb_ref[...] = dx.astype(BF16)
        _acc_out(dg_ref, dg, pl.program_id(0) == 0)

    row = pl.BlockSpec((tm, n), lambda i: (i, 0))
    vec = pl.BlockSpec((1, n), lambda i: (0, 0))
    blk = _nbytes((n, n), BF16) + 9 * _nbytes((tm, n), F32)
    return _pcall(body, name=name,
                  out_shape=(_sds((m, n), F32), _sds((m, n), BF16), _sds((1, n), F32), _sds((m, n), BF16), _sds((m, n), BF16)),
                  grid=(m // tm,),
                  in_specs=[row, row, pl.BlockSpec((tm, p.shape[1]), lambda i: (i, 0)), _spec(w_pe), _spec(w_pg), row,
                            _spec(g)],
                  out_specs=(row, row, vec, row, row), semantics=("arbitrary",), block_bytes=blk)(
                      dx3, gl, p, _arr(w_pe), _arr(w_pg), x, _arr(g))


def _matmul_tn(a, b, *, name, out_dtype=BF16, out_rows=None, row_off=0, into=None):
    m, k1 = a.shape
    n = b.shape[1]
    tk = _pick(k1, (512, 256, 128))
    off = row_off // tk
    assert off * tk == row_off

    def body(a_ref, b_ref, *rest):
        rest[-1][...] = lax.dot_general(a_ref[...], b_ref[...], (((0,), (0,)), ((), ())),
                                        preferred_element_type=F32).astype(out_dtype)

    blk = 2 * _nbytes((m, tk), a.dtype) + _nbytes((m, n), b.dtype) + _nbytes((tk, n), F32)
    in_specs = [pl.BlockSpec((m, tk), lambda i: (0, i)), pl.BlockSpec((m, n), lambda i: (0, 0))]
    args = [a, b]
    if into is not None:
        in_specs.append(HBM_SPEC)
        args.append(into)
    return _pcall(body, name=name, out_shape=_sds((out_rows or k1, n), out_dtype), grid=(k1 // tk,), in_specs=in_specs,
                  out_specs=pl.BlockSpec((tk, n), lambda i: (i + off, 0)), semantics=("parallel",), block_bytes=blk,
                  aliases=None if into is None else {2: 0})(*args)


def _rms_matmul(x, g, bs, *, name, nt=False, ple=None):
    m, d = x.shape
    n = bs[0].shape[0] if nt else bs[0].shape[1]
    nb = len(bs)
    nout = nb if ple is None else 2
    best = None
    for tm_c in (1024, 512, 256):
        for tn_c in (1408, 1024, 768, 512, 256, 128):
            if m % tm_c or n % tn_c:
                continue
            blk_c = (_nbytes((tm_c, d), F32) + 2 * _nbytes((tm_c, d), BF16) + nb * _nbytes((d, tn_c), BF16)
                     + (nout + 1) * _nbytes((tm_c, tn_c), F32))
            steps = (m // tm_c) * (n // tn_c)
            if blk_c <= MATMUL_BLOCK_BUDGET and (best is None or steps < best[0]):
                best = (steps, tm_c, tn_c, blk_c)
    _, tm, tn, blk = best
    dims = (((1,), (1,)), ((), ())) if nt else (((1,), (0,)), ((), ()))

    def body(*refs):
        x_ref, g_ref, b_refs = refs[0], refs[1], refs[2:2 + nb]
        rest = refs[2 + nb:]
        h_scr = rest[-1]
        j = pl.program_id(1)

        @pl.when(j == 0)
        def _():
            h = _rms(x_ref[...], g_ref[...]).astype(BF16)
            h_scr[...] = h
            rest[-2 - nout][...] = h

        h = h_scr[...]
        if ple is None:
            for k in range(nb):
                rest[-1 - nb + k][...] = lax.dot_general(h, b_refs[k][...], dims, preferred_element_type=F32)
        else:
            p_ref, wpe_ref = rest[0], rest[1]
            gl_ref, out_ref = rest[-3], rest[-2]
            gl = lax.dot_general(h, b_refs[0][...], dims, preferred_element_type=F32)
            pe = lax.dot_general(p_ref[...], wpe_ref[...], (((1,), (1,)), ((), ())), preferred_element_type=F32)
            gl_ref[...] = gl
            upd = pe * jax.nn.sigmoid(gl)
            for jj in range(n // tn):
                @pl.when(j == jj)
                def _():
                    out_ref[...] = x_ref[:, jj * tn:(jj + 1) * tn] + upd

    row = pl.BlockSpec((tm, d), lambda i, j: (i, 0))
    tile = pl.BlockSpec((tm, tn), lambda i, j: (i, j))
    in_specs = [row, _spec(g)] + [_spec(b, (tn, d), lambda i, j: (j, 0)) if nt else _spec(b, (d, tn), lambda i, j: (0, j))
                                  for b in bs]
    args = [x, _arr(g)] + [_arr(b) for b in bs]
    out_shape, out_specs = [_sds((m, d), BF16)], [row]
    if ple is None:
        out_shape += [_sds((m, n), F32)] * nb
        out_specs += [tile] * nb
    else:
        p, wpe = ple
        in_specs += [pl.BlockSpec((tm, p.shape[1]), lambda i, j: (i, 0)), _spec(wpe, (tn, p.shape[1]), lambda i, j: (j, 0))]
        args += [p, _arr(wpe)]
        out_shape += [_sds((m, n), F32)] * 2
        out_specs += [tile] * 2
    outs = _pcall(body, name=name, out_shape=tuple(out_shape), grid=(m // tm, n // tn), in_specs=in_specs,
                  out_specs=tuple(out_specs), scratch_shapes=[pltpu.VMEM((tm, d), BF16)],
                  semantics=("parallel", "arbitrary"), block_bytes=blk)(*args)
    return outs[0], list(outs[1:])


def _up_ffn_fwd(x, g, wg, wv, cwf, cbf, *, name):
    m, d = x.shape
    n = wg.shape[0]
    tm = _pick(m, (256, 128))
    tn = _pick(n, (1408, 256, 128))
    nj = n // tn
    dims = (((1,), (1,)), ((), ()))

    def body(x_ref, g_ref, wg_ref, wv_ref, tg_ref, bg_ref, tv_ref, bv_ref, h_ref, hg_ref, hv_ref, a_ref, cg_scr, cv_scr):
        i = pl.program_id(1)
        h = _rms(x_ref[...], g_ref[...]).astype(BF16)
        h_ref[...] = h
        hg = lax.dot_general(h, wg_ref[...], dims, preferred_element_type=F32)
        hv = lax.dot_general(h, wv_ref[...], dims, preferred_element_type=F32)
        hg_ref[...] = hg
        hv_ref[...] = hv
        eg = jnp.concatenate([jnp.where(i == 0, 0.0, cg_scr[...]), hg], axis=0)
        ev = jnp.concatenate([jnp.where(i == 0, 0.0, cv_scr[...]), hv], axis=0)
        a_ref[...] = _ffn_tile(eg, ev, tg_ref[...], bg_ref[...], tv_ref[...], bv_ref[...]).astype(BF16)
        cg_scr[...] = hg[tm - 8:]
        cv_scr[...] = hv[tm - 8:]

    row = pl.BlockSpec((tm, d), lambda j, i: (i, 0))
    hrow = pl.BlockSpec((tm, d), lambda j, i: (j * (m // tm) + i, 0))
    tile = pl.BlockSpec((tm, tn), lambda j, i: (i, j))
    wspec = lambda w: _spec(w, (tn, d), lambda j, i: (j, 0))
    taps = lambda off: _spec(cwf, (3, tn), lambda j, i: (0, j + off))
    bias = lambda off: _spec(cbf, (1, tn), lambda j, i: (0, j + off))
    blk = (_nbytes((tm, d), F32) + _nbytes((tm, d), BF16) + 2 * _nbytes((tn, d), BF16) + 12 * _nbytes((tm, tn), F32))
    return _pcall(body, name=name,
                  out_shape=(_sds((nj * m, d), BF16), _sds((m, n), F32), _sds((m, n), F32), _sds((m, n), BF16)),
                  grid=(nj, m // tm),
                  in_specs=[row, _spec(g), wspec(wg), wspec(wv), taps(0), bias(0), taps(nj), bias(nj)],
                  out_specs=(hrow, tile, tile, tile),
                  scratch_shapes=[pltpu.VMEM((8, tn), F32), pltpu.VMEM((8, tn), F32)],
                  semantics=("arbitrary", "arbitrary"), block_bytes=blk)(
                      x, _arr(g), _arr(wg), _arr(wv), _arr(cwf), _arr(cbf), _arr(cwf), _arr(cbf))


def _loss_head(x, g, target, *, name):
    s, d = x.shape
    tm = _pick(s, (256, 128))

    def tile_loss(xv, gv, tv):
        err = jnp.square(_rms(xv, gv) - tv)
        return 0.5 * jnp.sum(jnp.mean(err, axis=-1, keepdims=True), axis=0, keepdims=True)

    def body(x_ref, g_ref, t_ref, l_ref, dx_ref, dg_ref):
        lv, vjp = jax.vjp(tile_loss, x_ref[...], g_ref[...], t_ref[...])
        dxv, dgv, _ = vjp(jnp.ones((1, 1), F32))
        dx_ref[...] = dxv

        @pl.when(pl.program_id(0) == 0)
        def _():
            l_ref[...] = jnp.zeros_like(l_ref)
            dg_ref[...] = jnp.zeros_like(dg_ref)

        l_ref[...] += jnp.broadcast_to(lv, l_ref.shape)
        dg_ref[...] += dgv

    row = pl.BlockSpec((tm, d), lambda i: (i, 0))
    vec = pl.BlockSpec((1, d), lambda i: (0, 0))
    return _pcall(body, name=name, out_shape=(_sds((8, 128), F32), _sds((s, d), F32), _sds((1, d), F32)),
                  grid=(s // tm,), in_specs=[row, vec, row],
                  out_specs=(pl.BlockSpec((8, 128), lambda i: (0, 0)), row, vec), semantics=("arbitrary",),
                  block_bytes=8 * _nbytes((tm, d), F32))(x, g, target)


def _acc_out(ref, val, first):
    @pl.when(first)
    def _():
        ref[...] = jnp.zeros_like(ref)

    ref[...] += val


def _gmlp_fwd(z, ln_g, ln_b, wcat, bfull, *, name):
    s = z.shape[0]
    t = _pick(s, (512, 256, 128))
    nch = t // GMLP_CHUNK

    def body(zu_ref, zv_ref, g_ref, b_ref, w_ref, bf_ref, o_ref):
        for c in range(nch):
            rows = pl.ds(c * GMLP_CHUNK, GMLP_CHUNK)
            o_ref[rows, :] = _gmlp_chunk(zu_ref[rows, :], zv_ref[rows, :], g_ref[...], b_ref[...], w_ref[...],
                                         bf_ref[...]).astype(BF16)

    col = lambda c: pl.BlockSpec((t, W_GRP), lambda i: (i, c))
    params = (ln_g, ln_b, wcat, bfull)
    return _pcall(body, name=name, out_shape=_sds((s, D_MODEL), BF16), grid=(s // t,),
                  in_specs=[col(0), col(1)] + [_spec(a) for a in params],
                  out_specs=pl.BlockSpec((t, W_GRP), lambda i: (i, 0)), semantics=("parallel",),
                  block_bytes=4 * _nbytes((t, W_GRP), F32))(z, z, *[_arr(a) for a in params])


def _gmlp_bwd(z, dmix, ln_g, ln_b, wcat, bfull, *, name):
    s = z.shape[0]
    t = _pick(s, (512, 256, 128))
    nch = t // GMLP_CHUNK

    def body(zu_ref, zv_ref, dy_ref, g_ref, b_ref, w_ref, bf_ref, dz_ref, dg_ref, db_ref, dw_ref, dbf_ref):
        acc = None
        for c in range(nch):
            rows = pl.ds(c * GMLP_CHUNK, GMLP_CHUNK)
            _, vjp = jax.vjp(_gmlp_chunk, zu_ref[rows, :], zv_ref[rows, :], g_ref[...], b_ref[...], w_ref[...],
                             bf_ref[...])
            du, dv, *dps = vjp(dy_ref[rows, :])
            dz_ref[rows, :] = jnp.concatenate([du, dv], axis=1).astype(BF16)
            acc = dps if acc is None else [x + y for x, y in zip(acc, dps)]
        first = pl.program_id(0) == 0
        for ref, val in zip((dg_ref, db_ref, dw_ref, dbf_ref), acc):
            _acc_out(ref, val, first)

    col = lambda c: pl.BlockSpec((t, W_GRP), lambda i: (i, c))
    params = (ln_g, ln_b, wcat, bfull)
    return _pcall(body, name=name,
                  out_shape=(_sds((s, D_PROJ), BF16),) + tuple(_sds(a.shape, F32) for a in params),
                  grid=(s // t,), in_specs=[col(0), col(1), col(0)] + [_spec(a) for a in params],
                  out_specs=(pl.BlockSpec((t, 2 * W_GRP), lambda i: (i, 0)),) + tuple(_ospec(a) for a in params),
                  semantics=("arbitrary",),
                  block_bytes=8 * _nbytes((t, W_GRP), F32))(z, z, dmix, *[_arr(a) for a in params])


def _rglru_fwd(z, prm, mix, *, name):
    s = z.shape[0]
    t = _pick(s, (512, 256, 128))
    nt = s // t

    def body(xb_ref, halo_ref, gb_ref, *rest):
        prm_refs, (y_ref, h0s_ref, h_scr) = rest[:len(prm)], rest[len(prm) + 1:]
        i = pl.program_id(0)

        @pl.when(i == 0)
        def _():
            h_scr[...] = jnp.zeros_like(h_scr)

        halo = jnp.where(i == 0, 0.0, halo_ref[...])
        h0 = h_scr[...]
        y, h_last = _rglru_tile(jnp.concatenate([halo, xb_ref[...]], axis=0), gb_ref[...], h0,
                                *[r[...] for r in prm_refs])
        y_ref[...] = y.astype(BF16)
        h0s_ref[...] = jnp.broadcast_to(h0, h0s_ref.shape)
        h_scr[...] = h_last

    in_specs = [pl.BlockSpec((t, W_GRP), lambda i: (i, 2)),
                pl.BlockSpec((8, W_GRP), lambda i: (jnp.maximum(i * (t // 8) - 1, 0), 2)),
                pl.BlockSpec((t, W_GRP), lambda i: (i, 3))] + [_spec(a) for a in prm] + [HBM_SPEC]
    return _pcall(body, name=name, out_shape=(_sds(mix.shape, BF16), _sds((nt, 8, W_GRP), F32)), grid=(nt,),
                  in_specs=in_specs,
                  out_specs=(pl.BlockSpec((t, W_GRP), lambda i: (i, 1)), pl.BlockSpec((None, 8, W_GRP), lambda i: (i, 0, 0))),
                  scratch_shapes=[pltpu.VMEM((1, W_GRP), F32)], semantics=("arbitrary",),
                  block_bytes=24 * _nbytes((t, W_GRP), F32), aliases={3 + len(prm): 0})(
                      z, z, z, *[_arr(a) for a in prm], mix)


def _rglru_bwd(z, dmix, h0s, prm, dz, *, name):
    s = z.shape[0]
    t = _pick(s, (512, 256, 128))
    nt = s // t
    npm = len(prm)

    def body(xb_ref, halo_ref, gb_ref, dy_ref, h0s_ref, *rest):
        prm_refs = rest[:npm]
        dz_ref = rest[npm + 1]
        dprm_refs = rest[npm + 2:2 * npm + 2]
        dh_scr, dhalo_scr = rest[2 * npm + 2:]
        i = pl.program_id(0)
        r = nt - 1 - i

        @pl.when(i == 0)
        def _():
            dh_scr[...] = jnp.zeros_like(dh_scr)
            dhalo_scr[...] = jnp.zeros_like(dhalo_scr)

        halo = jnp.where(r == 0, 0.0, halo_ref[...])
        h0 = h0s_ref[0:1, :]
        _, vjp = jax.vjp(_rglru_tile, jnp.concatenate([halo, xb_ref[...]], axis=0), gb_ref[...], h0,
                         *[p[...] for p in prm_refs])
        dext, dgb, _dh0, *dps = vjp((dy_ref[...], dh_scr[...]))
        dmain = dext[8:]
        dxb = jnp.concatenate([dmain[:t - 8], dmain[t - 8:] + dhalo_scr[...]], axis=0)
        dz_ref[...] = jnp.concatenate([dxb, dgb], axis=1).astype(BF16)
        dh_scr[...] = _dh0
        dhalo_scr[...] = dext[:8]
        for ref, val in zip(dprm_refs, dps):
            _acc_out(ref, val, i == 0)

    rev = lambda c: pl.BlockSpec((t, W_GRP), lambda i: (nt - 1 - i, c))
    in_specs = [rev(2), pl.BlockSpec((8, W_GRP), lambda i: (jnp.maximum((nt - 1 - i) * (t // 8) - 1, 0), 2)), rev(3),
                rev(1), pl.BlockSpec((None, 8, W_GRP), lambda i: (nt - 1 - i, 0, 0))] + [_spec(a) for a in prm] + [HBM_SPEC]
    return _pcall(body, name=name,
                  out_shape=(_sds(dz.shape, BF16),) + tuple(_sds(a.shape, F32) for a in prm),
                  grid=(nt,), in_specs=in_specs,
                  out_specs=(pl.BlockSpec((t, 2 * W_GRP), lambda i: (nt - 1 - i, 1)),) + tuple(_ospec(a) for a in prm),
                  scratch_shapes=[pltpu.VMEM((1, W_GRP), F32), pltpu.VMEM((8, W_GRP), F32)],
                  semantics=("arbitrary",), block_bytes=40 * _nbytes((t, W_GRP), F32), aliases={5 + npm: 0})(
                      z, z, z, dmix, h0s, *[_arr(a) for a in prm], dz)


def _pool_inv(i, t):
    pos = (_rows_of((t, W_GRP)) + i * t + 1).astype(F32)
    grp = _lanes_of((t, W_GRP)) // HEAD_DIM
    win = jnp.where(grp == 0, float(POOL_WINDOWS[0]), jnp.where(grp == 1, float(POOL_WINDOWS[1]),
                    jnp.where(grp == 2, float(POOL_WINDOWS[2]), float(POOL_WINDOWS[3]))))
    return 1.0 / jnp.minimum(pos, win)


def _pool_fwd(z, wd, scale, mix, *, name):
    s = z.shape[0]
    t = _pick(s, (512, 256, 128))

    def body(x_ref, halo_ref, wd_ref, sc_ref, _, y_ref):
        i = pl.program_id(0)
        halo = jnp.where(i == 0, 0.0, halo_ref[...])
        y = _pool_tile(jnp.concatenate([halo, x_ref[...]], axis=0), _pool_inv(i, t), wd_ref[...], sc_ref[...])
        y_ref[...] = y.astype(BF16)

    in_specs = [pl.BlockSpec((t, W_GRP), lambda i: (i, 8)),
                pl.BlockSpec((16, W_GRP), lambda i: (jnp.maximum(i * (t // 16) - 1, 0), 8)), _spec(wd), _spec(scale),
                HBM_SPEC]
    return _pcall(body, name=name, out_shape=_sds(mix.shape, BF16), grid=(s // t,), in_specs=in_specs,
                  out_specs=pl.BlockSpec((t, W_GRP), lambda i: (i, 3)), semantics=("parallel",),
                  block_bytes=12 * _nbytes((t, W_GRP), F32), aliases={4: 0})(z, z, _arr(wd), _arr(scale), mix)


def _pool_bwd(z, dmix, wd, scale, dz, *, name):
    s = z.shape[0]
    t = _pick(s, (512, 256, 128))
    nt = s // t

    def body(x_ref, halo_ref, dy_ref, wd_ref, sc_ref, _, dx_ref, dwd_ref, dsc_ref, dhalo_scr):
        i = pl.program_id(0)
        r = nt - 1 - i

        @pl.when(i == 0)
        def _():
            dhalo_scr[...] = jnp.zeros_like(dhalo_scr)

        halo = jnp.where(r == 0, 0.0, halo_ref[...])
        inv = _pool_inv(r, t)
        _, vjp = jax.vjp(lambda e, w, sc: _pool_tile(e, inv, w, sc), jnp.concatenate([halo, x_ref[...]], axis=0),
                         wd_ref[...], sc_ref[...])
        dext, dwd, dsc = vjp(dy_ref[...])
        dmain = dext[16:]
        dx = jnp.concatenate([dmain[:t - 16], dmain[t - 16:] + dhalo_scr[...]], axis=0)
        dx_ref[...] = dx.astype(BF16)
        dhalo_scr[...] = dext[:16]
        _acc_out(dwd_ref, dwd, i == 0)
        _acc_out(dsc_ref, dsc, i == 0)

    rev = lambda c: pl.BlockSpec((t, W_GRP), lambda i: (nt - 1 - i, c))
    in_specs = [rev(8), pl.BlockSpec((16, W_GRP), lambda i: (jnp.maximum((nt - 1 - i) * (t // 16) - 1, 0), 8)), rev(3),
                _spec(wd), _spec(scale), HBM_SPEC]
    return _pcall(body, name=name, out_shape=(_sds(dz.shape, BF16), _sds(wd.shape, F32), _sds(scale.shape, F32)),
                  grid=(nt,), in_specs=in_specs, out_specs=(rev(8), _ospec(wd), _ospec(scale)),
                  scratch_shapes=[pltpu.VMEM((16, W_GRP), F32)], semantics=("arbitrary",),
                  block_bytes=20 * _nbytes((t, W_GRP), F32), aliases={5: 0})(z, z, dmix, _arr(wd), _arr(scale), dz)


def _hgrn_fwd(z, lb, ngf, mix, *, name):
    s = z.shape[0]
    c = HGRN_CHUNK
    per = HGRN_STEP_CHUNKS
    ns = s // (c * per)

    def body(q_ref, f_ref, i_ref, g_ref, lb_ref, ng_ref, _, y_ref, sts_ref, st_scr):
        @pl.when(pl.program_id(0) == 0)
        def _():
            st_scr[...] = jnp.zeros_like(st_scr)

        st = st_scr[...]
        for k in range(per):
            rows = pl.ds(k * c, c)
            sts_ref[k] = st
            y, st = _hgrn_chunk(q_ref[rows, :], f_ref[rows, :], i_ref[rows, :], g_ref[rows, :], st, lb_ref[...],
                                ng_ref[...])
            y_ref[rows, :] = y.astype(BF16)
        st_scr[...] = st

    col = lambda k: pl.BlockSpec((per * c, W_GRP), lambda i: (i, k))
    return _pcall(body, name=name, out_shape=(_sds(mix.shape, BF16), _sds((ns * per, W_GRP, W_GRP), F32)), grid=(ns,),
                  in_specs=[col(4), col(5), col(6), col(7), _spec(lb), _spec(ngf), HBM_SPEC],
                  out_specs=(pl.BlockSpec((per * c, W_GRP), lambda i: (i, 2)),
                             pl.BlockSpec((per, W_GRP, W_GRP), lambda i: (i, 0, 0))),
                  scratch_shapes=[pltpu.VMEM((W_GRP, W_GRP), F32)], semantics=("arbitrary",),
                  block_bytes=16 * per * _nbytes((W_GRP, W_GRP), F32), aliases={6: 0})(
                      z, z, z, z, _arr(lb), _arr(ngf), mix)


def _hgrn_bwd(z, dmix, sts, lb, ngf, dz, *, name):
    s = z.shape[0]
    c = HGRN_CHUNK
    per = HGRN_STEP_CHUNKS
    ns = s // (c * per)

    def body(q_ref, f_ref, i_ref, g_ref, dy_ref, st_ref, lb_ref, ng_ref, _, dz_ref, dlb_ref, dng_ref, dst_scr):
        i = pl.program_id(0)

        @pl.when(i == 0)
        def _():
            dst_scr[...] = jnp.zeros_like(dst_scr)

        dst = dst_scr[...]
        dlb_sum = dng_sum = None
        for k in range(per - 1, -1, -1):
            rows = pl.ds(k * c, c)
            _, vjp = jax.vjp(_hgrn_chunk, q_ref[rows, :], f_ref[rows, :], i_ref[rows, :], g_ref[rows, :], st_ref[k],
                             lb_ref[...], ng_ref[...])
            dq, df, di, dg, dst, dlb, dng = vjp((dy_ref[rows, :], dst))
            dz_ref[rows, :] = jnp.concatenate([dq, df, di, dg], axis=1).astype(BF16)
            dlb_sum = dlb if dlb_sum is None else dlb_sum + dlb
            dng_sum = dng if dng_sum is None else dng_sum + dng
        dst_scr[...] = dst
        _acc_out(dlb_ref, dlb_sum, i == 0)
        _acc_out(dng_ref, dng_sum, i == 0)

    rev = lambda k: pl.BlockSpec((per * c, W_GRP), lambda i: (ns - 1 - i, k))
    vec = pl.BlockSpec((1, W_GRP), lambda i: (0, 0))
    return _pcall(body, name=name, out_shape=(_sds(dz.shape, BF16), _sds((1, W_GRP), F32), _sds((1, W_GRP), F32)),
                  grid=(ns,),
                  in_specs=[rev(4), rev(5), rev(6), rev(7), rev(2),
                            pl.BlockSpec((per, W_GRP, W_GRP), lambda i: (ns - 1 - i, 0, 0)), _spec(lb), _spec(ngf),
                            HBM_SPEC],
                  out_specs=(pl.BlockSpec((per * c, 4 * W_GRP), lambda i: (ns - 1 - i, 1)), vec, vec),
                  scratch_shapes=[pltpu.VMEM((W_GRP, W_GRP), F32)], semantics=("arbitrary",),
                  block_bytes=32 * per * _nbytes((W_GRP, W_GRP), F32), aliases={8: 0})(
                      z, z, z, z, dmix, sts, _arr(lb), _arr(ngf), dz)


def _lbs_fwd(c_lb, *, name):
    def body(c_ref, o_ref):
        c = c_ref[...]
        e = jnp.exp(c - jnp.max(c, axis=0, keepdims=True))
        sm = e / jnp.sum(e, axis=0, keepdims=True)
        run = jnp.zeros((1, W_GRP), F32)
        o_ref[0:1, :] = run
        for l in range(1, DEPTH):
            run = run + sm[l:l + 1]
            o_ref[l:l + 1, :] = run

    return _pcall(body, name=name, out_shape=_sds((DEPTH, W_GRP), F32), pin=False)(c_lb)


def _lbs_bwd(c_lb, dlbs, *, name):
    def body(c_ref, d_ref, o_ref):
        c = c_ref[...]
        e = jnp.exp(c - jnp.max(c, axis=0, keepdims=True))
        sm = e / jnp.sum(e, axis=0, keepdims=True)
        d = d_ref[...]
        dsm = [None] * DEPTH
        run = jnp.zeros((1, W_GRP), F32)
        for l in range(DEPTH - 1, 0, -1):
            run = run + d[l:l + 1]
            dsm[l] = run
        dsm[0] = jnp.zeros((1, W_GRP), F32)
        inner = sum(sm[l:l + 1] * dsm[l] for l in range(DEPTH))
        for l in range(DEPTH):
            o_ref[l:l + 1, :] = sm[l:l + 1] * (dsm[l] - inner)

    return _pcall(body, name=name, out_shape=_sds((DEPTH, W_GRP), F32), pin=False)(c_lb, dlbs)


def _ffn_bwd(hg, hv, dx, w_down, cwf, cbf, *, name):
    s, n = hg.shape
    t = _pick(s, (256, 128))
    cw = _pick(n, (1408, 256, 128))
    nt = s // t
    nj = n // cw

    def body(g_ref, gh_ref, v_ref, vh_ref, dx_ref, wd_ref, wg_ref, bg_ref, wv_ref, bv_ref, dg_ref, dv_ref, dwg_ref,
             dwv_ref, cg_scr, cv_scr):
        i = pl.program_id(1)
        r = nt - 1 - i

        @pl.when(i == 0)
        def _():
            cg_scr[...] = jnp.zeros_like(cg_scr)
            cv_scr[...] = jnp.zeros_like(cv_scr)

        da = lax.dot_general(dx_ref[...], wd_ref[...], (((1,), (1,)), ((), ())), preferred_element_type=F32)
        eg = jnp.concatenate([jnp.where(r == 0, 0.0, gh_ref[...]), g_ref[...]], axis=0)
        ev = jnp.concatenate([jnp.where(r == 0, 0.0, vh_ref[...]), v_ref[...]], axis=0)
        _, vjp = jax.vjp(_ffn_tile, eg, ev, wg_ref[...], bg_ref[...], wv_ref[...], bv_ref[...])
        deg, dev, dwg, dbg, dwv, dbv = vjp(da)
        for dext, scr, ref in ((deg, cg_scr, dg_ref), (dev, cv_scr, dv_ref)):
            dmain = dext[8:]
            ref[...] = jnp.concatenate([dmain[:t - 8], dmain[t - 8:] + scr[...]], axis=0).astype(BF16)
            scr[...] = dext[:8]
        zeros = jnp.zeros((4, cw), F32)
        _acc_out(dwg_ref, jnp.concatenate([dwg, dbg, zeros], axis=0), i == 0)
        _acc_out(dwv_ref, jnp.concatenate([dwv, dbv, zeros], axis=0), i == 0)

    main = pl.BlockSpec((t, cw), lambda j, i: (nt - 1 - i, j))
    halo = pl.BlockSpec((8, cw), lambda j, i: (jnp.maximum((nt - 1 - i) * (t // 8) - 1, 0), j))
    taps = lambda off: _spec(cwf, (3, cw), lambda j, i: (0, j + off))
    bias = lambda off: _spec(cbf, (1, cw), lambda j, i: (0, j + off))
    w8 = pl.BlockSpec((8, cw), lambda j, i: (0, j))
    d = dx.shape[1]
    in_specs = [main, halo, main, halo, pl.BlockSpec((t, d), lambda j, i: (nt - 1 - i, 0)),
                _spec(w_down, (cw, d), lambda j, i: (j, 0)), taps(0), bias(0), taps(nj), bias(nj)]
    return _pcall(body, name=name,
                  out_shape=(_sds((s, n), BF16), _sds((s, n), BF16), _sds((8, n), F32), _sds((8, n), F32)),
                  grid=(nj, nt), in_specs=in_specs, out_specs=(main, main, w8, w8),
                  scratch_shapes=[pltpu.VMEM((8, cw), F32), pltpu.VMEM((8, cw), F32)],
                  semantics=("parallel", "arbitrary"),
                  block_bytes=24 * _nbytes((t, cw), F32) + _nbytes((cw, d), BF16))(
                      hg, hg, hv, hv, dx, _arr(w_down), _arr(cwf), _arr(cbf), _arr(cwf), _arr(cbf))


def _all_gather(x, *, name):
    r, c = x.shape

    def body(x_ref, out_ref, send_sems, recv_sems, local_sem):
        mx, my, mc = lax.axis_index("x"), lax.axis_index("y"), lax.axis_index("c")
        me, sibling = (mx, my, mc), (mx, my, 1 - mc)
        chips = [(1 - mx, my), (mx, 1 - my), (1 - mx, 1 - my)]

        def slot(px, py, pc):
            return out_ref.at[4 * px + 2 * py + pc]

        def copy(k, block, to, src=None):
            return pltpu.make_async_remote_copy(src_ref=slot(*block) if src is None else src, dst_ref=slot(*block),
                                                send_sem=send_sems.at[k], recv_sem=recv_sems.at[k],
                                                device_id=to, device_id_type=MESH)

        mine = pltpu.make_async_copy(x_ref, slot(*me), local_sem)
        mine.start()
        first = [copy(0, me, sibling, src=x_ref)]
        first += [copy(1 + j, me, (*chip, mc), src=x_ref) for j, chip in enumerate(chips)]
        for cp in first:
            cp.start()
        passed = [copy(4 + j, (*chip, mc), sibling) for j, chip in enumerate(chips)]
        for j, chip in enumerate(chips):
            copy(1 + j, (*chip, mc), me).wait_recv()
            passed[j].start()
        copy(0, sibling, me).wait_recv()
        for j, chip in enumerate(chips):
            copy(4 + j, (*chip, 1 - mc), me).wait_recv()
        for cp in first + passed:
            cp.wait_send()
        mine.wait()

    hbm = pl.BlockSpec(memory_space=pl.ANY)
    return _pcall(body, name=name, out_shape=_sds((N_DEV, r, c), x.dtype), in_specs=[hbm], out_specs=hbm,
                  scratch_shapes=[pltpu.SemaphoreType.DMA((7,)), pltpu.SemaphoreType.DMA((7,)),
                                  pltpu.SemaphoreType.DMA(())])(x)


def _sum_slots(p, *, name):
    q, r, c = p.shape
    tr = _pick(r, (544, 408, 272, 192, 136, 64, 32, 16, 8))

    def body(p_ref, o_ref):
        acc = p_ref[0].astype(F32)
        for k in range(1, q):
            acc = acc + p_ref[k].astype(F32)
        o_ref[...] = acc

    return _pcall(body, name=name, out_shape=_sds((r, c), F32), grid=(r // tr,),
                  in_specs=[pl.BlockSpec((q, tr, c), lambda i: (0, i, 0))],
                  out_specs=pl.BlockSpec((tr, c), lambda i: (i, 0)), semantics=("parallel",),
                  block_bytes=(q + 2) * _nbytes((tr, c), F32))(p)


BIG_COMM = (('w_in', 288, D_MODEL), ('w_out', 128, D_MODEL), ('w_up', 704, D_MODEL), ('w_down', 352, D_MODEL),
            ('w_pe', 128, PLE_DIM), ('w_pg', 128, D_MODEL))
HBM_SPEC = pl.BlockSpec(memory_space=pl.ANY)


def _gather_layer(shards, l, *, name):
    na = len(shards)

    def body(*refs):
        x_refs, out_refs = refs[:na], refs[na:2 * na]
        send_sems, recv_sems, local_sems = refs[2 * na:]
        mx, my, mc = lax.axis_index("x"), lax.axis_index("y"), lax.axis_index("c")
        me, sibling = (mx, my, mc), (mx, my, 1 - mc)
        chips = [(1 - mx, my), (mx, 1 - my), (1 - mx, 1 - my)]

        def slot(a, px, py, pc):
            return out_refs[a].at[4 * px + 2 * py + pc]

        def copy(k, a, block, to, own=False):
            return pltpu.make_async_remote_copy(src_ref=x_refs[a].at[l] if own else slot(a, *block),
                                                dst_ref=slot(a, *block), send_sem=send_sems.at[k, a],
                                                recv_sem=recv_sems.at[k, a], device_id=to, device_id_type=MESH)

        mine = [pltpu.make_async_copy(x_refs[a].at[l], slot(a, *me), local_sems.at[a]) for a in range(na)]
        for cp in mine:
            cp.start()
        first = []
        for a in range(na):
            first.append(copy(0, a, me, sibling, own=True))
            first += [copy(1 + j, a, me, (*chip, mc), own=True) for j, chip in enumerate(chips)]
        for cp in first:
            cp.start()
        passed = []
        for j, chip in enumerate(chips):
            for a in range(na):
                copy(1 + j, a, (*chip, mc), me).wait_recv()
                fwd = copy(4 + j, a, (*chip, mc), sibling)
                fwd.start()
                passed.append(fwd)
        for a in range(na):
            copy(0, a, sibling, me).wait_recv()
        for j, chip in enumerate(chips):
            for a in range(na):
                copy(4 + j, a, (*chip, 1 - mc), me).wait_recv()
        for cp in first + passed:
            cp.wait_send()
        for cp in mine:
            cp.wait()

    return _pcall(body, name=name, out_shape=tuple(_sds((N_DEV,) + x.shape[1:], x.dtype) for x in shards),
                  in_specs=[HBM_SPEC] * na, out_specs=(HBM_SPEC,) * na,
                  scratch_shapes=[pltpu.SemaphoreType.DMA((7, na)), pltpu.SemaphoreType.DMA((7, na)),
                                  pltpu.SemaphoreType.DMA((na,))])(*shards)


SEM_SPEC = pl.BlockSpec(memory_space=pltpu.SEMAPHORE)
DATAFLOW_EFFECT = pltpu.SideEffectType.DATAFLOW_SIDE_EFFECTING


def _place_own(srcs, after, *, name):
    na = len(srcs)

    def body(*refs):
        x_refs, land_refs, sems = refs[:na], refs[na + len(after):2 * na + len(after)], refs[-1]
        me = 4 * lax.axis_index("x") + 2 * lax.axis_index("y") + lax.axis_index("c")
        cps = [pltpu.make_async_copy(x_refs[a], land_refs[a].at[me], sems.at[a]) for a in range(na)]
        for cp in cps:
            cp.start()
        for cp in cps:
            cp.wait()

    return _pcall(body, name=name, out_shape=tuple(_sds((N_DEV,) + x.shape, x.dtype) for x in srcs),
                  in_specs=[HBM_SPEC] * (na + len(after)), out_specs=(HBM_SPEC,) * na,
                  scratch_shapes=[pltpu.SemaphoreType.DMA((na,))], pin=False)(*srcs, *after)


def _exchange_start(srcs, lands, *, name, per_peer=False):
    na = len(srcs)

    def body(*refs):
        x_refs, land_refs = refs[:na], refs[na:2 * na]
        send_sems, recv_sems = refs[2 * na], refs[2 * na + 1]
        token = refs[-1]
        mx, my, mc = lax.axis_index("x"), lax.axis_index("y"), lax.axis_index("c")
        me = 4 * mx + 2 * my + mc
        peers = [(mx, my, 1 - mc)]
        for px, py in ((1 - mx, my), (mx, 1 - my), (1 - mx, 1 - my)):
            peers += [(px, py, mc), (px, py, 1 - mc)]
        for a in range(na):
            for peer in peers:
                src = x_refs[a].at[4 * peer[0] + 2 * peer[1] + peer[2]] if per_peer else x_refs[a]
                pltpu.make_async_remote_copy(src_ref=src, dst_ref=land_refs[a].at[me], send_sem=send_sems.at[a],
                                             recv_sem=recv_sems.at[a], device_id=peer, device_id_type=MESH).start()
        token[...] = jnp.zeros_like(token)

    hbm = lambda x: pltpu.HBM(x.shape, x.dtype)
    out_shape = ((pltpu.SemaphoreType.DMA((na,)), pltpu.SemaphoreType.DMA((na,))) + tuple(hbm(x) for x in srcs)
                 + tuple(hbm(x) for x in lands) + (_sds((8, 128), F32),))
    params = pltpu.CompilerParams(has_side_effects=DATAFLOW_EFFECT)
    pin = lambda x: pltpu.with_memory_space_constraint(x, pltpu.HBM)
    return pl.pallas_call(body, name=name, out_shape=out_shape, in_specs=[HBM_SPEC] * (2 * na),
                          out_specs=(SEM_SPEC, SEM_SPEC) + (HBM_SPEC,) * (2 * na) + (pl.BlockSpec(memory_space=pltpu.VMEM),),
                          input_output_aliases={i: 2 + i for i in range(2 * na)}, compiler_params=params)(
                              *[pin(x) for x in srcs], *[pin(x) for x in lands])


def _exchange_wait(started, after, *, name):
    send_sems, recv_sems, *bufs, _ = started
    na = len(bufs) // 2

    def body(*refs):
        land_refs = refs[na:2 * na]
        s_sems, r_sems = refs[2 * na], refs[2 * na + 1]
        me = (lax.axis_index("x"), lax.axis_index("y"), lax.axis_index("c"))
        for a in range(na):
            seven = land_refs[a].at[pl.ds(0, N_DEV - 1)]
            cp = pltpu.make_async_remote_copy(src_ref=seven, dst_ref=seven, send_sem=s_sems.at[a], recv_sem=r_sems.at[a],
                                              device_id=me, device_id_type=MESH)
            cp.wait_send()
            cp.wait_recv()

    hbm = lambda x: pltpu.HBM(x.shape, x.dtype)
    params = pltpu.CompilerParams(has_side_effects=DATAFLOW_EFFECT)
    outs = pl.pallas_call(body, name=name, out_shape=tuple(hbm(x) for x in bufs),
                          in_specs=[HBM_SPEC] * (2 * na) + [SEM_SPEC, SEM_SPEC, HBM_SPEC],
                          out_specs=(HBM_SPEC,) * (2 * na), input_output_aliases={i: i for i in range(2 * na)},
                          compiler_params=params)(*bufs, send_sems, recv_sems, after)
    return outs[:na], outs[na:]


def _pair_swap(grads, *, name):
    na = len(grads)

    def body(*refs):
        g_refs, recv_refs = refs[:na], refs[na:2 * na]
        send_sems, recv_sems = refs[2 * na:]
        mx, my, mc = lax.axis_index("x"), lax.axis_index("y"), lax.axis_index("c")
        sibling = (mx, my, 1 - mc)
        for a in range(na):
            for q in range(4):
                pltpu.make_async_remote_copy(src_ref=g_refs[a].at[q, 1 - mc], dst_ref=recv_refs[a].at[q],
                                             send_sem=send_sems.at[a], recv_sem=recv_sems.at[a],
                                             device_id=sibling, device_id_type=MESH).start()
        for a in range(na):
            pltpu.make_async_remote_copy(src_ref=recv_refs[a], dst_ref=recv_refs[a], send_sem=send_sems.at[a],
                                         recv_sem=recv_sems.at[a], device_id=sibling, device_id_type=MESH).wait()

    half = tuple(_sds((4,) + g.shape[2:], g.dtype) for g in grads)
    return _pcall(body, name=name, out_shape=half, in_specs=[HBM_SPEC] * na, out_specs=(HBM_SPEC,) * na,
                  scratch_shapes=[pltpu.SemaphoreType.DMA((na,)), pltpu.SemaphoreType.DMA((na,))])(*grads)


def _add_slabs(grads, recv, core, *, name):
    na = len(grads)

    def body(core_ref, *refs):
        for a in range(na):
            refs[2 * na + a][...] = (refs[a][...].astype(F32) + refs[na + a][...].astype(F32)).astype(BF16)

    own_specs = [pl.BlockSpec((None, None) + x.shape[2:], lambda q, core_ref: (q, core_ref[0], 0, 0)) for x in grads]
    specs = [pl.BlockSpec((None,) + x.shape[1:], lambda q, core_ref: (q, 0, 0)) for x in recv]
    blk = sum(_nbytes(x.shape[1:], F32) for x in recv)
    grid_spec = pltpu.PrefetchScalarGridSpec(num_scalar_prefetch=1, grid=(4,), in_specs=own_specs + specs,
                                             out_specs=tuple(specs))
    params = pltpu.CompilerParams(dimension_semantics=("parallel",), vmem_limit_bytes=_vmem_limit(2 * blk))
    return pl.pallas_call(body, name=name, out_shape=tuple(_sds(x.shape, BF16) for x in recv), grid_spec=grid_spec,
                          compiler_params=params)(core, *grads, *recv)


def _chip_exchange(parts, *, name):
    na = len(parts)

    def body(*refs):
        p_refs, out_refs = refs[:na], refs[na:2 * na]
        send_sems, recv_sems, local_sems = refs[2 * na:]
        mx, my, mc = lax.axis_index("x"), lax.axis_index("y"), lax.axis_index("c")
        mine_q = 2 * mx + my
        chips = [(1 - mx, my), (mx, 1 - my), (1 - mx, 1 - my)]
        owns = [pltpu.make_async_copy(p_refs[a].at[mine_q], out_refs[a].at[mine_q], local_sems.at[a]) for a in range(na)]
        for cp in owns:
            cp.start()
        sends = []
        for a in range(na):
            for k, chip in enumerate(chips):
                sends.append(pltpu.make_async_remote_copy(
                    src_ref=p_refs[a].at[2 * chip[0] + chip[1]], dst_ref=out_refs[a].at[mine_q],
                    send_sem=send_sems.at[k, a], recv_sem=recv_sems.at[k, a], device_id=(*chip, mc), device_id_type=MESH))
        for cp in sends:
            cp.start()
        for a in range(na):
            for k, chip in enumerate(chips):
                pltpu.make_async_remote_copy(
                    src_ref=p_refs[a].at[mine_q], dst_ref=out_refs[a].at[2 * chip[0] + chip[1]],
                    send_sem=send_sems.at[k, a], recv_sem=recv_sems.at[k, a], device_id=(*chip, mc),
                    device_id_type=MESH).wait_recv()
        for cp in sends:
            cp.wait_send()
        for cp in owns:
            cp.wait()

    return _pcall(body, name=name, out_shape=tuple(_sds(x.shape, x.dtype) for x in parts), in_specs=[HBM_SPEC] * na,
                  out_specs=(HBM_SPEC,) * na,
                  scratch_shapes=[pltpu.SemaphoreType.DMA((3, na)), pltpu.SemaphoreType.DMA((3, na)),
                                  pltpu.SemaphoreType.DMA((na,))])(*parts)


def _sum_chips(parts, *, name):
    na = len(parts)

    def body(*refs):
        for a in range(na):
            p_ref = refs[a]
            acc = p_ref[0].astype(F32)
            for k in range(1, p_ref.shape[0]):
                acc = acc + p_ref[k].astype(F32)
            refs[na + a][...] = acc

    half = lambda x: x.shape[1] // 2
    in_specs = [pl.BlockSpec((x.shape[0], half(x), x.shape[2]), lambda i: (0, i, 0)) for x in parts]
    out_specs = tuple(pl.BlockSpec((half(x), x.shape[2]), lambda i: (i, 0)) for x in parts)
    blk = sum(_nbytes((x.shape[0] + 2, half(x), x.shape[2]), BF16) for x in parts)
    return _pcall(body, name=name, out_shape=tuple(_sds(x.shape[1:], F32) for x in parts), grid=(2,),
                  in_specs=in_specs, out_specs=out_specs, semantics=("parallel",), block_bytes=blk)(*parts)


def _sum_devices(lands, own, me, *, name):
    na = len(lands)

    def body(me_ref, *refs):
        mine = me_ref[0]
        for a in range(na):
            l_ref, o_ref = refs[a], refs[na + a]
            acc = None
            for k in range(N_DEV):
                term = jnp.where(mine == k, o_ref[...], l_ref[k]).astype(F32)
                acc = term if acc is None else acc + term
            refs[2 * na + a][...] = acc

    half = lambda x: x.shape[1] // 2
    land_specs = [pl.BlockSpec((N_DEV, half(x), x.shape[2]), lambda i, me_ref: (0, i, 0)) for x in lands]
    own_specs = [pl.BlockSpec((None, half(x), x.shape[2]), lambda i, me_ref: (me_ref[0], i, 0)) for x in lands]
    out_specs = tuple(pl.BlockSpec((half(x), x.shape[2]), lambda i, me_ref: (i, 0)) for x in lands)
    blk = sum(_nbytes((N_DEV + 3, half(x), x.shape[2]), BF16) for x in lands)
    grid_spec = pltpu.PrefetchScalarGridSpec(num_scalar_prefetch=1, grid=(2,), in_specs=land_specs + own_specs,
                                             out_specs=out_specs)
    params = pltpu.CompilerParams(dimension_semantics=("parallel",), vmem_limit_bytes=_vmem_limit(blk))
    return pl.pallas_call(body, name=name, out_shape=tuple(_sds(x.shape[1:], F32) for x in lands), grid_spec=grid_spec,
                          compiler_params=params)(me, *lands, *own)


def _reduce_layer(grads, l):
    n = lambda s: f"l{l}_{s}"
    views = [g.reshape(4, 2, g.shape[0] // N_DEV, g.shape[1]) for g in grads]
    recv = _pair_swap(views, name=n("reduce_pair"))
    core = lax.axis_index("c").astype(jnp.int32).reshape(1)
    chip_sum = _add_slabs(views, recv, core, name=n("reduce_pair_add"))
    from_chips = _chip_exchange(chip_sum, name=n("reduce_chips"))
    return _sum_chips(from_chips, name=n("reduce_chips_add"))


def _adamw(w, g, m, v, *, name):
    lead, (r, c) = w.shape[:-2], w.shape[-2:]
    tr = _pick(r, (512, 352, 288, 256, 192, 128, 64, 32, 16, 8))
    c1 = 1.0 / (1.0 - ADAM_B1 ** ADAM_STEP)
    c2 = 1.0 / (1.0 - ADAM_B2 ** ADAM_STEP)

    def body(w_ref, g_ref, m_ref, v_ref, d_ref, nm_ref, nv_ref):
        gv = g_ref[...]
        nm = ADAM_B1 * m_ref[...] + (1.0 - ADAM_B1) * gv
        nv = ADAM_B2 * v_ref[...] + (1.0 - ADAM_B2) * jnp.square(gv)
        d_ref[...] = -ADAM_LR * ((nm * c1) / (jnp.sqrt(nv * c2) + ADAM_EPS) + ADAM_WD * w_ref[...])
        nm_ref[...] = nm
        nv_ref[...] = nv

    if lead:
        blk = pl.BlockSpec((None, tr, c), lambda k, i: (k, i, 0))
        grid, sem = (lead[0], r // tr), ("parallel", "parallel")
    else:
        blk = pl.BlockSpec((tr, c), lambda i: (i, 0))
        grid, sem = (r // tr,), ("parallel",)
    out = _sds(w.shape, F32)
    return _pcall(body, name=name, out_shape=(out, out, out), grid=grid, in_specs=[blk] * 4,
                  out_specs=(blk, blk, blk), semantics=sem, block_bytes=7 * _nbytes((tr, c), F32))(w, g, m, v)


def _pack_flat(arrs, rows, cols=1024):
    flat = jnp.concatenate([a.reshape(-1).astype(F32) for a in arrs])
    pad = rows * cols - flat.shape[0]
    return jnp.pad(flat, (0, pad)).reshape(rows, cols)


def _unpack_flat(buf, shapes):
    flat = buf.reshape(-1)
    out, off = [], 0
    for shp in shapes:
        n = 1
        for s in shp:
            n *= s
        out.append(flat[off:off + n].reshape(shp))
        off += n
    return out


def _flat_rows(shapes, cols=1024):
    n = sum(functools.reduce(lambda a, b: a * b, shp, 1) for shp in shapes)
    rows = -(-n // cols)
    return -(-rows // 64) * 64


def _block_diag(w):
    eye = jnp.eye(N_HEADS, dtype=w.dtype)
    return (w[:, :, :, None, :] * eye[None, :, None, :, None]).reshape(w.shape[0], W_GRP, W_GRP)


def _diag_blocks(w):
    w5 = w.reshape(w.shape[0], N_HEADS, HEAD_DIM, N_HEADS, HEAD_DIM)
    return jnp.stack([w5[:, h, :, h, :] for h in range(N_HEADS)], axis=1)


def _stacked_params(w, lbs):
    tril = jnp.tril(jnp.ones((GMLP_CHUNK, GMLP_CHUNK), bool))
    row = lambda a: a.reshape(DEPTH, 1, -1)
    return dict(
        g1=row(w['norm1_g']), g2=row(w['norm2_g']), g3=row(w['norm3_g']),
        a_ln_g=row(w['a_ln_g']), a_ln_b=row(w['a_ln_b']),
        a_wcat=jnp.where(tril, w['a_ws'], 0.0).reshape(DEPTH, N_HEADS * GMLP_CHUNK, GMLP_CHUNK),
        a_bfull=jnp.repeat(jnp.swapaxes(w['a_bs'], 1, 2), HEAD_DIM, axis=2),
        b_cw=w['b_conv_w_full'], b_cb=row(w['b_conv_b']), b_wa=_block_diag(w['b_wa']), b_ba=row(w['b_ba']),
        b_wx=_block_diag(w['b_wx']), b_bx=row(w['b_bx']), b_lam=row(w['b_lam']),
        c_lb=row(lbs), c_ngf=row(jnp.tile(w['c_norm_g'], (1, N_HEADS))),
        d_wd=_block_diag(w['d_w']), d_scale=row(w['d_scale']),
        f_cw=w['ffn_conv_w_full'], f_cb=row(w['ffn_conv_b']),
    )


B_PRM = ('b_cw', 'b_cb', 'b_wa', 'b_ba', 'b_wx', 'b_bx', 'b_lam')


def _layer_fwd(x, p_bf, wb, sp, l):
    n = lambda s: f"l{l}_{s}"
    h, (z,) = _rms_matmul(x, sp['g1'], [wb['w_in']], nt=True, name=n("proj_in"))
    mix = _gmlp_fwd(z, sp['a_ln_g'], sp['a_ln_b'], sp['a_wcat'], sp['a_bfull'], name=n("gmlp"))
    mix, h0s = _rglru_fwd(z, [sp[k] for k in B_PRM], mix, name=n("rglru"))
    mix, sts = _hgrn_fwd(z, sp['c_lb'], sp['c_ngf'], mix, name=n("hgrn"))
    mix = _pool_fwd(z, sp['d_wd'], sp['d_scale'], mix, name=n("pool"))
    x1 = _matmul(mix, wb['w_out'], res=x, name=n("proj_out"))
    h2, hg, hv, a = _up_ffn_fwd(x1, sp['g2'], wb['w_up_g'], wb['w_up_v'], sp['f_cw'], sp['f_cb'], name=n("up_ffn"))
    x2 = _matmul(a, wb['w_down'], res=x1, name=n("down"))
    h3, (gl, x3) = _rms_matmul(x2, sp['g3'], [wb['w_pg']], ple=(p_bf, wb['w_pe']), name=n("ple"))
    saved = dict(x=x, h=h, z=z, h0s=h0s, sts=sts, mix=mix, x1=x1, h2=h2, hg=hg, hv=hv, a=a, x2=x2, h3=h3, gl=gl)
    return x3, saved


def _layer_bwd(dx3, sv, p_bf, wb, sp, l, mid=None):
    n = lambda s: f"l{l}_{s}_bwd"
    gb, gs = {}, {}
    dx2, dx2b, gs['norm3_g'], dpe, dgl = _ple_rms_bwd(dx3, sv['gl'], p_bf, wb['w_pe'], wb['w_pg'], sv['x2'], sp['g3'],
                                                      name=n("ple"))
    gb['w_pe'] = _matmul_tn(dpe, p_bf, name=n("ple_emb_w"))
    gb['w_pg'] = _matmul_tn(sv['h3'], dgl, name=n("ple_gate_w"))
    gb['w_down'] = _matmul_tn(sv['a'], dx2b, name=n("down_w"))
    dhg, dhv, gs['f_dwg'], gs['f_dwv'] = _ffn_bwd(sv['hg'], sv['hv'], dx2b, wb['w_down'], sp['f_cw'], sp['f_cb'],
                                                  name=n("ffn_gate"))
    gate_rows = _matmul_tn(dhg, sv['h2'], name=n("up_gate_w"), out_rows=2 * D_FF)
    gb['w_up'] = _matmul_tn(dhv, sv['h2'], name=n("up_val_w"), out_rows=2 * D_FF, row_off=D_FF, into=gate_rows)
    if mid is not None:
        sp = mid(gb, sp)
    dx1, dx1b, gs['norm2_g'] = _matmul_rms_bwd(dhg, wb['w_up_g'], sv['x1'], sp['g2'], dx2,
                                               second=(dhv, wb['w_up_v']), name=n("up_x"))
    dmix = _matmul(dx1b, wb['w_out'], nt=True, name=n("proj_out_x"))
    gb['w_out'] = _matmul_tn(sv['mix'], dx1b, name=n("proj_out_w"))
    z = sv['z']
    dz, gs['a_ln_g'], gs['a_ln_b'], gs['a_wcat'], gs['a_bfull'] = _gmlp_bwd(
        z, dmix, sp['a_ln_g'], sp['a_ln_b'], sp['a_wcat'], sp['a_bfull'], name=n("gmlp"))
    dz, *dbp = _rglru_bwd(z, dmix, sv['h0s'], [sp[k] for k in B_PRM], dz, name=n("rglru"))
    gs.update(zip(B_PRM, dbp))
    dz, gs['c_lb'], gs['c_ngf'] = _hgrn_bwd(z, dmix, sv['sts'], sp['c_lb'], sp['c_ngf'], dz, name=n("hgrn"))
    dz, gs['d_wd'], gs['d_scale'] = _pool_bwd(z, dmix, sp['d_wd'], sp['d_scale'], dz, name=n("pool"))
    gb['w_in'] = _matmul_tn(dz, sv['h'], name=n("proj_in_w"))
    dx0, _, gs['norm1_g'] = _matmul_rms_bwd(dz, wb['w_in'], sv['x'], sp['g1'], dx1, name=n("proj_in_x"))
    return dx0, gb, gs


SMALL_NAMES = [nm for nm in WEIGHT_NAMES if nm not in BIG_NAMES]
COL_SHARDED = ('w_in', 'w_up', 'w_pe')


def _comm_shards(w):
    return [(jnp.swapaxes(w[nm], 1, 2) if nm in COL_SHARDED else w[nm]).astype(BF16) for nm, _, _ in BIG_COMM]


def _full_weights(gathered):
    out = {nm: g.reshape(N_DEV * r, c) for g, (nm, r, c) in zip(gathered, BIG_COMM)}
    halves = out.pop('w_up').reshape(2, D_FF, D_MODEL)
    out['w_up_g'], out['w_up_v'] = _Sel(halves, 0), _Sel(halves, 1)
    return out


def _small_grads(raw):
    nl = len(raw)
    st = {k: jnp.stack([r[k] for r in raw]) for k in raw[0]}
    tril = jnp.tril(jnp.ones((GMLP_CHUNK, GMLP_CHUNK), bool))
    vec = lambda a: a.reshape(nl, -1)
    out = {nm: vec(st[k]) for nm, k in (('norm1_g', 'norm1_g'), ('norm2_g', 'norm2_g'), ('norm3_g', 'norm3_g'),
                                        ('a_ln_g', 'a_ln_g'), ('a_ln_b', 'a_ln_b'), ('b_conv_b', 'b_cb'),
                                        ('b_ba', 'b_ba'), ('b_bx', 'b_bx'), ('b_lam', 'b_lam'), ('c_lb', 'c_lb'),
                                        ('d_scale', 'd_scale'))}
    out['a_ws'] = jnp.where(tril, st['a_wcat'].reshape(nl, N_HEADS, GMLP_CHUNK, GMLP_CHUNK), 0.0)
    out['a_bs'] = jnp.swapaxes(st['a_bfull'].reshape(nl, GMLP_CHUNK, N_HEADS, HEAD_DIM).sum(-1), 1, 2)
    out['b_conv_w'] = st['b_cw']
    out['b_wa'], out['b_wx'], out['d_w'] = _diag_blocks(st['b_wa']), _diag_blocks(st['b_wx']), _diag_blocks(st['d_wd'])
    out['c_norm_g'] = st['c_ngf'].reshape(nl, N_HEADS, HEAD_DIM).sum(1)
    out['ffn_conv_w'] = jnp.concatenate([st['f_dwg'][:, 0:3], st['f_dwv'][:, 0:3]], axis=2)
    out['ffn_conv_b'] = jnp.concatenate([st['f_dwg'][:, 3], st['f_dwv'][:, 3]], axis=1)
    return out


def _step(w, m, v, x, p, target):
    s = x.shape[1]
    dev = 4 * lax.axis_index("x") + 2 * lax.axis_index("y") + lax.axis_index("c")
    xs = x.reshape(s, D_MODEL)

    shards = _comm_shards(w)
    conv_shapes = [w['b_conv_w'].shape, w['ffn_conv_w'].shape]
    conv_rows = _flat_rows(conv_shapes)
    conv_all = _all_gather(_pack_flat([w['b_conv_w'], w['ffn_conv_w']], conv_rows), name="gather_conv_weights")
    parts = [_unpack_flat(conv_all[d], conv_shapes) for d in range(N_DEV)]
    wf = dict(w)
    wf['b_conv_w_full'] = jnp.concatenate([pt[0] for pt in parts], axis=-1)
    wf['ffn_conv_w_full'] = jnp.concatenate([pt[1] for pt in parts], axis=-1)
    lbs = _lbs_fwd(w['c_lb'], name="hgrn_bounds")

    stacked = _stacked_params(wf, lbs)
    p_all = p.reshape(DEPTH, s, PLE_DIM).astype(BF16)
    xl, saved, wbs, sps = xs, [], [], []
    gathered = _gather_layer(shards, 0, name="l0_gather_weights")
    for l in range(DEPTH):
        sp = {k: _Sel(a, l) for k, a in stacked.items()}
        if l + 1 < DEPTH:
            own = [x[l + 1] for x in shards]
            after = [conv_all, *gathered] if l == 0 else [xl]
            lands = _place_own(own, after, name=f"l{l + 1}_gather_place")
            started = _exchange_start(own, lands, name=f"l{l + 1}_gather_start")
            sp['g1'] = stacked['g1'][l] + started[-1][0, 0]
        wb = _full_weights(gathered)
        p_bf = p_all[l]
        xl, sv = _layer_fwd(xl, p_bf, wb, sp, l)
        if l + 1 < DEPTH:
            gathered = _exchange_wait(started, xl, name=f"l{l + 1}_gather_wait")[1]
        saved.append((sv, p_bf))
        wbs.append(wb)
        sps.append(sp)
    loss_part, dx, dfinal = _loss_head(xl, w['final_g'].reshape(1, D_MODEL), target.reshape(s, D_MODEL), name="loss_head")
    loss = lax.psum(loss_part[0, 0], ("x", "y", "c"))

    dev1 = dev.astype(jnp.int32).reshape(1)
    names = [nm for nm, _, _ in BIG_COMM]

    def start_reduce(grads, name):
        views = [g.reshape(N_DEV, g.shape[0] // N_DEV, g.shape[1]) for g in grads]
        return _exchange_start(views, [lax.empty(g.shape, g.dtype) for g in views], name=name, per_peer=True)

    def finish_reduce(started, after, lname):
        own, lands = _exchange_wait(started, after, name=f"{lname}_reduce_wait")
        return _sum_devices(lands, own, dev1, name=f"{lname}_reduce_sum")

    reduced, small = [None] * DEPTH, [None] * DEPTH
    pending = None
    for l in range(DEPTH - 1, 0, -1):
        sv, p_bf = saved[l]
        sp = sps[l]
        if pending is not None:
            sp = dict(sp, g3=stacked['g3'][l] + pending[-1][0, 0])
        dx, gb, small[l] = _layer_bwd(dx, sv, p_bf, wbs[l], sp, l)
        if pending is not None:
            reduced[l + 1] = finish_reduce(pending, dx, f"l{l + 1}")
        pending = start_reduce([gb[nm] for nm in names], f"l{l}_reduce_start")
    early = ('w_up', 'w_down', 'w_pe', 'w_pg')
    mid_started = []

    def mid(gb, sp):
        mid_started.append(start_reduce([gb[nm] for nm in early], "l0_reduce_start"))
        return dict(sp, g2=stacked['g2'][0] + mid_started[0][-1][0, 0])

    upper_names = [nm for nm in SMALL_NAMES if nm != 'final_g']
    low_names = upper_names + ['final_g']
    upper = _small_grads(small[1:])
    upper_shapes = [upper[nm].shape for nm in upper_names]
    upper_packed = [_pack_flat([upper[nm] for nm in upper_names], _flat_rows(upper_shapes))]
    upper_started = _exchange_start(upper_packed, _place_own(upper_packed, [], name="upper_small_grads_place"),
                                    name="upper_small_grads_start")

    sv, p_bf = saved[0]
    g3 = stacked['g3'][0] + pending[-1][0, 0] + upper_started[-1][0, 0]
    dx, gb, small[0] = _layer_bwd(dx, sv, p_bf, wbs[0], dict(sps[0], g3=g3), 0, mid=mid)
    reduced[1] = finish_reduce(pending, dx, "l1")
    late = dict(zip(('w_in', 'w_out'), _reduce_layer([gb['w_in'], gb['w_out']], 0)))
    late.update(zip(early, finish_reduce(mid_started[0], late['w_in'], "l0")))
    reduced[0] = [late[nm] for nm in names]
    grad_x = dx.reshape(1, s, D_MODEL)
    low = _small_grads(small[:1])
    low['final_g'] = dfinal.reshape(D_MODEL)
    low_shapes = [low[nm].shape for nm in low_names]
    low_all = _all_gather(_pack_flat([low[nm] for nm in low_names], _flat_rows(low_shapes)), name="gather_small_grads")
    low_sum = dict(zip(low_names, _unpack_flat(_sum_slots(low_all, name="sum_small_grads"), low_shapes)))
    upper_all = _exchange_wait(upper_started, low_all, name="upper_small_grads_wait")[1][0]
    upper_sum = dict(zip(upper_names, _unpack_flat(_sum_slots(upper_all, name="sum_upper_small_grads"), upper_shapes)))
    gsmall = {nm: jnp.concatenate([low_sum[nm], upper_sum[nm]], axis=0) for nm in upper_names}
    gsmall['c_lb'] = _lbs_bwd(w['c_lb'], gsmall['c_lb'], name="hgrn_bounds_bwd")
    gsmall['final_g'] = low_sum['final_g']
    for nm in ('b_conv_w', 'ffn_conv_w'):
        width = w[nm].shape[-1]
        gsmall[nm] = lax.dynamic_slice_in_dim(gsmall[nm], dev * width, width, axis=2)

    grads, delta, new_m, new_v = {}, {}, {}, {}
    for a, (nm, _, _) in enumerate(BIG_COMM):
        t = (lambda x: jnp.swapaxes(x, 1, 2)) if nm in COL_SHARDED else (lambda x: x)
        g = jnp.stack([reduced[l][a] for l in range(DEPTH)])
        d, nm_, nv_ = _adamw(t(w[nm]), g, t(m[nm]), t(v[nm]), name=f"adamw_{nm}")
        grads[nm], delta[nm], new_m[nm], new_v[nm] = t(g), t(d), t(nm_), t(nv_)

    shapes = [w[nm].shape for nm in SMALL_NAMES]
    rows = _flat_rows(shapes)
    pk = lambda t: _pack_flat([t[nm] for nm in SMALL_NAMES], rows)
    d, nm_, nv_ = _adamw(pk(w), pk(gsmall), pk(m), pk(v), name="adamw_small")
    for nm, dd, mm_, vv_ in zip(SMALL_NAMES, _unpack_flat(d, shapes), _unpack_flat(nm_, shapes), _unpack_flat(nv_, shapes)):
        grads[nm], delta[nm], new_m[nm], new_v[nm] = gsmall[nm], dd, mm_, vv_

    return (loss, grad_x, *[grads[nm] for nm in WEIGHT_NAMES], *[delta[nm] for nm in WEIGHT_NAMES],
            *[new_m[nm] for nm in WEIGHT_NAMES], *[new_v[nm] for nm in WEIGHT_NAMES])


def kernel(x, p, norm1_g, w_in, a_ln_g, a_ln_b, a_ws, a_bs, b_conv_w, b_conv_b, b_wa, b_ba, b_wx, b_bx, b_lam, c_lb, c_norm_g, d_w, d_scale, w_out, norm2_g, w_up, ffn_conv_w, ffn_conv_b, w_down, norm3_g, w_pe, w_pg, final_g, loss_target, m_norm1_g, m_w_in, m_a_ln_g, m_a_ln_b, m_a_ws, m_a_bs, m_b_conv_w, m_b_conv_b, m_b_wa, m_b_ba, m_b_wx, m_b_bx, m_b_lam, m_c_lb, m_c_norm_g, m_d_w, m_d_scale, m_w_out, m_norm2_g, m_w_up, m_ffn_conv_w, m_ffn_conv_b, m_w_down, m_norm3_g, m_w_pe, m_w_pg, m_final_g, v_norm1_g, v_w_in, v_a_ln_g, v_a_ln_b, v_a_ws, v_a_bs, v_b_conv_w, v_b_conv_b, v_b_wa, v_b_ba, v_b_wx, v_b_bx, v_b_lam, v_c_lb, v_c_norm_g, v_d_w, v_d_scale, v_w_out, v_norm2_g, v_w_up, v_ffn_conv_w, v_ffn_conv_b, v_w_down, v_norm3_g, v_w_pe, v_w_pg, v_final_g):
    w = dict(norm1_g=norm1_g, w_in=w_in, a_ln_g=a_ln_g, a_ln_b=a_ln_b, a_ws=a_ws, a_bs=a_bs, b_conv_w=b_conv_w, b_conv_b=b_conv_b, b_wa=b_wa, b_ba=b_ba, b_wx=b_wx, b_bx=b_bx, b_lam=b_lam, c_lb=c_lb, c_norm_g=c_norm_g, d_w=d_w, d_scale=d_scale, w_out=w_out, norm2_g=norm2_g, w_up=w_up, ffn_conv_w=ffn_conv_w, ffn_conv_b=ffn_conv_b, w_down=w_down, norm3_g=norm3_g, w_pe=w_pe, w_pg=w_pg, final_g=final_g)
    m = dict(norm1_g=m_norm1_g, w_in=m_w_in, a_ln_g=m_a_ln_g, a_ln_b=m_a_ln_b, a_ws=m_a_ws, a_bs=m_a_bs, b_conv_w=m_b_conv_w, b_conv_b=m_b_conv_b, b_wa=m_b_wa, b_ba=m_b_ba, b_wx=m_b_wx, b_bx=m_b_bx, b_lam=m_b_lam, c_lb=m_c_lb, c_norm_g=m_c_norm_g, d_w=m_d_w, d_scale=m_d_scale, w_out=m_w_out, norm2_g=m_norm2_g, w_up=m_w_up, ffn_conv_w=m_ffn_conv_w, ffn_conv_b=m_ffn_conv_b, w_down=m_w_down, norm3_g=m_norm3_g, w_pe=m_w_pe, w_pg=m_w_pg, final_g=m_final_g)
    v = dict(norm1_g=v_norm1_g, w_in=v_w_in, a_ln_g=v_a_ln_g, a_ln_b=v_a_ln_b, a_ws=v_a_ws, a_bs=v_a_bs, b_conv_w=v_b_conv_w, b_conv_b=v_b_conv_b, b_wa=v_b_wa, b_ba=v_b_ba, b_wx=v_b_wx, b_bx=v_b_bx, b_lam=v_b_lam, c_lb=v_c_lb, c_norm_g=v_c_norm_g, d_w=v_d_w, d_scale=v_d_scale, w_out=v_w_out, norm2_g=v_norm2_g, w_up=v_w_up, ffn_conv_w=v_ffn_conv_w, ffn_conv_b=v_ffn_conv_b, w_down=v_w_down, norm3_g=v_norm3_g, w_pe=v_w_pe, w_pg=v_w_pg, final_g=v_final_g)
    return _step(w, m, v, x, p, loss_target)
```
